```python
import math
import jax, jax.numpy as jnp
from jax import lax
import numpy as np

D_MODEL = 2048
BATCH = 8
SEQ = 2048
DEPTH = 1

HEAD_DIM = 128
N_HEADS = D_MODEL // HEAD_DIM
N_HEADS_SB = N_HEADS // 2
N_HEADS_DIL = N_HEADS - N_HEADS_SB
D_SB = N_HEADS_SB * HEAD_DIM
D_DIL = N_HEADS_DIL * HEAD_DIM
QKV_WIDTH = 3 * (D_SB + D_DIL)
DILATED_BRANCHES = ((128, 1), (512, 4), (2048, 16))
QUERY_BLOCK = 128
D_FF = 5504
CONV_WIDTH = 3
ROPE_THETA = 10000.0
RMS_EPS = 1e-6

kernel_name = 'hybrid_stickbreaking_dilated_convffn_layer'


def rmsnorm(x, gain):
    xf = x.astype(jnp.float32)
    y = xf * lax.rsqrt(jnp.mean(xf * xf, axis=-1, keepdims=True) + RMS_EPS)
    return (y * gain.astype(jnp.float32)).astype(x.dtype)


def head_rmsnorm(o, gain):
    H, Dh = o.shape[1], o.shape[3]
    of = o.astype(jnp.float32)
    y = of * lax.rsqrt(jnp.mean(of * of, axis=-1, keepdims=True) + RMS_EPS)
    return (y * gain.astype(jnp.float32).reshape(1, H, 1, Dh)).astype(o.dtype)


def apply_rope(x):
    S, Dh = x.shape[2], x.shape[3]
    inv_freq = ROPE_THETA ** (-jnp.arange(0, Dh, 2, dtype=jnp.float32) / Dh)
    ang = jnp.arange(S, dtype=jnp.float32)[:, None] * inv_freq[None, :]
    cos, sin = jnp.cos(ang), jnp.sin(ang)
    x1, x2 = jnp.split(x.astype(jnp.float32), 2, axis=-1)
    out = jnp.concatenate([x1 * cos - x2 * sin, x2 * cos + x1 * sin], axis=-1)
    return out.astype(x.dtype)


def to_heads(t, n_heads):
    B, S, _ = t.shape
    return t.reshape(B, S, n_heads, HEAD_DIM).transpose(0, 2, 1, 3)


def stick_breaking_attention(q, k, v):
    S, Dh = q.shape[2], q.shape[3]
    scale = Dh ** -0.5
    outs = []
    for blk in range(S // QUERY_BLOCK):
        q0 = blk * QUERY_BLOCK
        n_keys = q0 + QUERY_BLOCK
        q_blk = q[:, :, q0:n_keys]
        k_pre, v_pre = k[:, :, :n_keys], v[:, :, :n_keys]
        z = jnp.einsum('bhqd,bhkd->bhqk', q_blk, k_pre).astype(jnp.float32) * scale
        q_pos = q0 + jnp.arange(QUERY_BLOCK)
        k_pos = jnp.arange(n_keys)
        causal = k_pos[None, :] < q_pos[:, None]
        log_beta = jax.nn.log_sigmoid(z)
        log_keep = jnp.where(causal, jax.nn.log_sigmoid(-z), 0.0)
        log_remain = lax.cumsum(log_keep, axis=3, reverse=True) - log_keep
        a = jnp.where(causal, jnp.exp(log_beta + log_remain), 0.0)
        outs.append(jnp.einsum('bhqk,bhkd->bhqd', a.astype(v.dtype), v_pre))
    return jnp.concatenate(outs, axis=2)


def dilated_branch(q, k, v, window, dilation):
    B, H, S, Dh = q.shape
    n_back = window // dilation
    QB = QUERY_BLOCK
    L = S // dilation
    n_blocks = -(-L // QB)
    Lp = n_blocks * QB
    scale = Dh ** -0.5

    def to_sub(t):
        return t.reshape(B, H, L, dilation, Dh).transpose(0, 1, 3, 2, 4)

    qs = jnp.pad(to_sub(q), ((0, 0), (0, 0), (0, 0), (0, Lp - L), (0, 0)))
    pad_kv = ((0, 0), (0, 0), (0, 0), (QB, Lp - L), (0, 0))
    ks = jnp.pad(to_sub(k), pad_kv)
    vs = jnp.pad(to_sub(v), pad_kv)
    qb = qs.reshape(B, H, dilation, n_blocks, QB, Dh)

    def band(t):
        prev = t[:, :, :, :Lp].reshape(B, H, dilation, n_blocks, QB, Dh)
        cur = t[:, :, :, QB:QB + Lp].reshape(B, H, dilation, n_blocks, QB, Dh)
        return jnp.concatenate([prev, cur], axis=4)

    kb, vb = band(ks), band(vs)
    s = jnp.einsum('bhrnqd,bhrnkd->bhrnqk', qb, kb).astype(jnp.float32) * scale
    q_idx = jnp.arange(n_blocks)[:, None] * QB + jnp.arange(QB)[None, :]
    k_idx = jnp.arange(n_blocks)[:, None] * QB - QB + jnp.arange(2 * QB)[None, :]
    dist = q_idx[:, :, None] - k_idx[:, None, :]
    valid = (dist >= 0) & (dist <= n_back) & (k_idx[:, None, :] >= 0)
    s = jnp.where(valid, s, -jnp.inf)
    m = jnp.max(s, axis=-1, keepdims=True)
    p = jnp.exp(s - m)
    den = jnp.sum(p, axis=-1, keepdims=True)
    out = jnp.einsum('bhrnqk,bhrnkd->bhrnqd', p.astype(v.dtype), vb).astype(jnp.float32) / den
    lse = (m + jnp.log(den))[..., 0]
    out = out.reshape(B, H, dilation, Lp, Dh)[:, :, :, :L]
    out = out.transpose(0, 1, 3, 2, 4).reshape(B, H, S, Dh)
    lse = lse.reshape(B, H, dilation, Lp)[:, :, :, :L].transpose(0, 1, 3, 2).reshape(B, H, S)
    return out, lse


def dilated_attention(q, k, v):
    outs, lses = [], []
    for window, dilation in DILATED_BRANCHES:
        o, l = dilated_branch(q, k, v, window, dilation)
        outs.append(o)
        lses.append(l)
    w = jax.nn.softmax(jnp.stack(lses, axis=0), axis=0)
    out = jnp.sum(w[..., None] * jnp.stack(outs, axis=0), axis=0)
    return out.astype(q.dtype)


def conv_geglu_ffn(h, w_up, conv_w, conv_b, w_down):
    S = h.shape[1]
    u = jnp.einsum('bsd,df->bsf', h, w_up)
    up = jnp.pad(u, ((0, 0), (CONV_WIDTH - 1, 0), (0, 0)))
    u = sum(up[:, j:j + S] * conv_w[j] for j in range(CONV_WIDTH)) + conv_b
    gate, val = jnp.split(u, 2, axis=-1)
    y = jax.nn.gelu(gate, approximate=True) * val
    return jnp.einsum('bsf,fd->bsd', y, w_down)


def _fwd_setup_inputs(seed: int = 0) -> dict:
    key = jax.random.key(seed)
    ks = jax.random.split(key, 16)
    f32 = jnp.float32

    def gain(k, n):
        return 1.0 + 0.05 * jax.random.normal(k, (DEPTH, n), f32)

    return {
        'x': jax.random.normal(ks[0], (BATCH, SEQ, D_MODEL), f32),
        'pre_mix_gain': gain(ks[1], D_MODEL),
        'post_mix_gain': gain(ks[2], D_MODEL),
        'pre_ffn_gain': gain(ks[3], D_MODEL),
        'post_ffn_gain': gain(ks[4], D_MODEL),
        'w_in': jax.random.normal(ks[5], (DEPTH, D_MODEL, QKV_WIDTH), f32) * D_MODEL ** -0.5,
        'sb_out_gain': gain(ks[6], D_SB),
        'dil_out_gain': gain(ks[7], D_DIL),
        'w_out': jax.random.normal(ks[8], (DEPTH, D_SB + D_DIL, D_MODEL), f32) * (D_SB + D_DIL) ** -0.5,
        'w_up': jax.random.normal(ks[9], (DEPTH, D_MODEL, 2 * D_FF), f32) * D_MODEL ** -0.5,
        'conv_w': jax.random.normal(ks[10], (DEPTH, CONV_WIDTH, 2 * D_FF), f32) * CONV_WIDTH ** -0.5,
        'conv_b': 0.02 * jax.random.normal(ks[11], (DEPTH, 2 * D_FF), f32),
        'w_down': jax.random.normal(ks[12], (DEPTH, D_FF, D_MODEL), f32) * D_FF ** -0.5,
    }


def _fwd_reference(x, pre_mix_gain, post_mix_gain, pre_ffn_gain, post_ffn_gain, w_in,
              sb_out_gain, dil_out_gain, w_out, w_up, conv_w, conv_b, w_down):
    splits = [D_SB, 2 * D_SB, 3 * D_SB, 3 * D_SB + D_DIL, 3 * D_SB + 2 * D_DIL]
    for layer in range(DEPTH):
        h = rmsnorm(x, pre_mix_gain[layer])
        proj = jnp.einsum('bsd,de->bse', h, w_in[layer])
        q_sb, k_sb, v_sb, q_dl, k_dl, v_dl = jnp.split(proj, splits, axis=-1)
        o_sb = stick_breaking_attention(to_heads(q_sb, N_HEADS_SB), to_heads(k_sb, N_HEADS_SB),
                                        to_heads(v_sb, N_HEADS_SB))
        o_dl = dilated_attention(apply_rope(to_heads(q_dl, N_HEADS_DIL)),
                                 apply_rope(to_heads(k_dl, N_HEADS_DIL)),
                                 to_heads(v_dl, N_HEADS_DIL))
        o_sb = head_rmsnorm(o_sb, sb_out_gain[layer])
        o_dl = head_rmsnorm(o_dl, dil_out_gain[layer])
        B, _, S, _ = o_sb.shape
        mixed = jnp.concatenate([o_sb.transpose(0, 2, 1, 3).reshape(B, S, D_SB),
                                 o_dl.transpose(0, 2, 1, 3).reshape(B, S, D_DIL)], axis=-1)
        mix_out = jnp.einsum('bse,ed->bsd', mixed, w_out[layer])
        x = x + rmsnorm(mix_out, post_mix_gain[layer])
        h = rmsnorm(x, pre_ffn_gain[layer])
        f = conv_geglu_ffn(h, w_up[layer], conv_w[layer], conv_b[layer], w_down[layer])
        x = x + rmsnorm(f, post_ffn_gain[layer])
    return x


import jax as _jax
import jax.numpy as _jnp

TWIN_FORMAT = 'train_step'
FWD_PARAMS = ['x', 'pre_mix_gain', 'post_mix_gain', 'pre_ffn_gain', 'post_ffn_gain', 'w_in', 'sb_out_gain', 'dil_out_gain', 'w_out', 'w_up', 'conv_w', 'conv_b', 'w_down']
TWIN_WEIGHTS = ['pre_mix_gain', 'post_mix_gain', 'pre_ffn_gain', 'post_ffn_gain', 'w_in', 'sb_out_gain', 'dil_out_gain', 'w_out', 'w_up', 'conv_w', 'conv_b', 'w_down']
TWIN_DIFF_INPUT = 'x'
TWIN_INPUTS = ['x', 'pre_mix_gain', 'post_mix_gain', 'pre_ffn_gain', 'post_ffn_gain', 'w_in', 'sb_out_gain', 'dil_out_gain', 'w_out', 'w_up', 'conv_w', 'conv_b', 'w_down', 'loss_target', 'm_pre_mix_gain', 'm_post_mix_gain', 'm_pre_ffn_gain', 'm_post_ffn_gain', 'm_w_in', 'm_sb_out_gain', 'm_dil_out_gain', 'm_w_out', 'm_w_up', 'm_conv_w', 'm_conv_b', 'm_w_down', 'v_pre_mix_gain', 'v_post_mix_gain', 'v_pre_ffn_gain', 'v_post_ffn_gain', 'v_w_in', 'v_sb_out_gain', 'v_dil_out_gain', 'v_w_out', 'v_w_up', 'v_conv_w', 'v_conv_b', 'v_w_down']
TWIN_OUTPUTS = ['loss', 'grad_x', 'grad_pre_mix_gain', 'grad_post_mix_gain', 'grad_pre_ffn_gain', 'grad_post_ffn_gain', 'grad_w_in', 'grad_sb_out_gain', 'grad_dil_out_gain', 'grad_w_out', 'grad_w_up', 'grad_conv_w', 'grad_conv_b', 'grad_w_down', 'delta_pre_mix_gain', 'delta_post_mix_gain', 'delta_pre_ffn_gain', 'delta_post_ffn_gain', 'delta_w_in', 'delta_sb_out_gain', 'delta_dil_out_gain', 'delta_w_out', 'delta_w_up', 'delta_conv_w', 'delta_conv_b', 'delta_w_down', 'new_m_pre_mix_gain', 'new_m_post_mix_gain', 'new_m_pre_ffn_gain', 'new_m_post_ffn_gain', 'new_m_w_in', 'new_m_sb_out_gain', 'new_m_dil_out_gain', 'new_m_w_out', 'new_m_w_up', 'new_m_conv_w', 'new_m_conv_b', 'new_m_w_down', 'new_v_pre_mix_gain', 'new_v_post_mix_gain', 'new_v_pre_ffn_gain', 'new_v_post_ffn_gain', 'new_v_w_in', 'new_v_sb_out_gain', 'new_v_dil_out_gain', 'new_v_w_out', 'new_v_w_up', 'new_v_conv_w', 'new_v_conv_b', 'new_v_w_down']
TWIN_LEAF_KINDS = {'loss': 'loss', 'grad_x': 'grad_x', 'grad_pre_mix_gain': 'grad_w', 'grad_post_mix_gain': 'grad_w', 'grad_pre_ffn_gain': 'grad_w', 'grad_post_ffn_gain': 'grad_w', 'grad_w_in': 'grad_w', 'grad_sb_out_gain': 'grad_w', 'grad_dil_out_gain': 'grad_w', 'grad_w_out': 'grad_w', 'grad_w_up': 'grad_w', 'grad_conv_w': 'grad_w', 'grad_conv_b': 'grad_w', 'grad_w_down': 'grad_w', 'delta_pre_mix_gain': 'delta_w', 'delta_post_mix_gain': 'delta_w', 'delta_pre_ffn_gain': 'delta_w', 'delta_post_ffn_gain': 'delta_w', 'delta_w_in': 'delta_w', 'delta_sb_out_gain': 'delta_w', 'delta_dil_out_gain': 'delta_w', 'delta_w_out': 'delta_w', 'delta_w_up': 'delta_w', 'delta_conv_w': 'delta_w', 'delta_conv_b': 'delta_w', 'delta_w_down': 'delta_w', 'new_m_pre_mix_gain': 'new_m', 'new_m_post_mix_gain': 'new_m', 'new_m_pre_ffn_gain': 'new_m', 'new_m_post_ffn_gain': 'new_m', 'new_m_w_in': 'new_m', 'new_m_sb_out_gain': 'new_m', 'new_m_dil_out_gain': 'new_m', 'new_m_w_out': 'new_m', 'new_m_w_up': 'new_m', 'new_m_conv_w': 'new_m', 'new_m_conv_b': 'new_m', 'new_m_w_down': 'new_m', 'new_v_pre_mix_gain': 'new_v', 'new_v_post_mix_gain': 'new_v', 'new_v_pre_ffn_gain': 'new_v', 'new_v_post_ffn_gain': 'new_v', 'new_v_w_in': 'new_v', 'new_v_sb_out_gain': 'new_v', 'new_v_dil_out_gain': 'new_v', 'new_v_w_out': 'new_v', 'new_v_w_up': 'new_v', 'new_v_conv_w': 'new_v', 'new_v_conv_b': 'new_v', 'new_v_w_down': 'new_v'}


def _forward(args):
    return _fwd_reference(*[args[k] for k in FWD_PARAMS])


def _output_shape():
    out = _jax.eval_shape(lambda: _forward(_fwd_setup_inputs(0)))
    return out.shape, out.dtype

N_MICROBATCH = 1
ADAM_LR = 0.001
ADAM_B1 = 0.9
ADAM_B2 = 0.999
ADAM_EPS = 1e-08
ADAM_WD = 0.01
ADAM_STEP = 10
PER_EXAMPLE_BATCH_AXIS = {'x': 0, 'loss_target': 0}
SHARED_INPUTS = []
_WEIGHT_DTYPES = {'pre_mix_gain': _jnp.float32, 'post_mix_gain': _jnp.float32, 'pre_ffn_gain': _jnp.float32, 'post_ffn_gain': _jnp.float32, 'w_in': _jnp.float32, 'sb_out_gain': _jnp.float32, 'dil_out_gain': _jnp.float32, 'w_out': _jnp.float32, 'w_up': _jnp.float32, 'conv_w': _jnp.float32, 'conv_b': _jnp.float32, 'w_down': _jnp.float32}
MOMENT_SCALE = {'pre_mix_gain': 2.391201e-01, 'post_mix_gain': 7.984525e+00, 'pre_ffn_gain': 1.723368e-01, 'post_ffn_gain': 8.004640e+00, 'w_in': 1.378455e-01, 'sb_out_gain': 1.707373e-01, 'dil_out_gain': 1.837694e-01, 'w_out': 1.711865e-01, 'w_up': 7.350727e-02, 'conv_w': 7.599969e-02, 'conv_b': 1.300808e-01, 'w_down': 1.283045e-01}


def _to_microbatches(a, axis):
    t = _jnp.moveaxis(a, axis, 0)
    t = t.reshape((N_MICROBATCH, t.shape[0] // N_MICROBATCH) + t.shape[1:])
    return _jnp.moveaxis(t, 1, axis + 1)


def setup_inputs(seed: int = 0) -> dict:
    inp = _fwd_setup_inputs(seed)
    key = _jax.random.fold_in(_jax.random.key(seed), 7919)
    shape, _ = _output_shape()
    out = dict(inp)
    out["loss_target"] = _jax.random.normal(_jax.random.fold_in(key, 0), shape, _jnp.float32)
    for i, name in enumerate(TWIN_WEIGHTS):
        w = inp[name].astype(_jnp.float32)
        if MOMENT_SCALE is None:
            s = _jnp.sqrt(_jnp.mean(_jnp.square(w)) + 1e-30)
        else:
            s = MOMENT_SCALE[name]
        km, kv = _jax.random.split(_jax.random.fold_in(key, i + 1))
        out[name] = w
        out["m_" + name] = s * _jax.random.normal(km, w.shape, _jnp.float32)
        out["v_" + name] = (s * s) * _jax.random.uniform(kv, w.shape, _jnp.float32, 0.5, 1.5)
    if N_MICROBATCH > 1:
        for name, axis in PER_EXAMPLE_BATCH_AXIS.items():
            out[name] = _to_microbatches(out[name], axis)
    return {'x': out['x'], 'pre_mix_gain': out['pre_mix_gain'], 'post_mix_gain': out['post_mix_gain'], 'pre_ffn_gain': out['pre_ffn_gain'], 'post_ffn_gain': out['post_ffn_gain'], 'w_in': out['w_in'], 'sb_out_gain': out['sb_out_gain'], 'dil_out_gain': out['dil_out_gain'], 'w_out': out['w_out'], 'w_up': out['w_up'], 'conv_w': out['conv_w'], 'conv_b': out['conv_b'], 'w_down': out['w_down'], 'loss_target': out['loss_target'], 'm_pre_mix_gain': out['m_pre_mix_gain'], 'm_post_mix_gain': out['m_post_mix_gain'], 'm_pre_ffn_gain': out['m_pre_ffn_gain'], 'm_post_ffn_gain': out['m_post_ffn_gain'], 'm_w_in': out['m_w_in'], 'm_sb_out_gain': out['m_sb_out_gain'], 'm_dil_out_gain': out['m_dil_out_gain'], 'm_w_out': out['m_w_out'], 'm_w_up': out['m_w_up'], 'm_conv_w': out['m_conv_w'], 'm_conv_b': out['m_conv_b'], 'm_w_down': out['m_w_down'], 'v_pre_mix_gain': out['v_pre_mix_gain'], 'v_post_mix_gain': out['v_post_mix_gain'], 'v_pre_ffn_gain': out['v_pre_ffn_gain'], 'v_post_ffn_gain': out['v_post_ffn_gain'], 'v_w_in': out['v_w_in'], 'v_sb_out_gain': out['v_sb_out_gain'], 'v_dil_out_gain': out['v_dil_out_gain'], 'v_w_out': out['v_w_out'], 'v_w_up': out['v_w_up'], 'v_conv_w': out['v_conv_w'], 'v_conv_b': out['v_conv_b'], 'v_w_down': out['v_w_down']}


def _loss(weights, diff, rest, loss_target):
    with _jax.named_scope("forward"):
        args = {**rest, TWIN_DIFF_INPUT: diff, **{k: w.astype(_WEIGHT_DTYPES[k]) for k, w in weights.items()}}
        y = _forward(args)
    with _jax.named_scope("loss_head"):
        err = _jnp.square(y.astype(_jnp.float32) - loss_target)
        return 0.5 * _jnp.sum(_jnp.mean(err, axis=-1)) if err.ndim else 0.5 * err


def _adamw(w, g, m, v):
    m = ADAM_B1 * m + (1.0 - ADAM_B1) * g
    v = ADAM_B2 * v + (1.0 - ADAM_B2) * _jnp.square(g)
    m_hat = m / (1.0 - ADAM_B1 ** ADAM_STEP)
    v_hat = v / (1.0 - ADAM_B2 ** ADAM_STEP)
    delta = -ADAM_LR * (m_hat / (_jnp.sqrt(v_hat) + ADAM_EPS) + ADAM_WD * w)
    return delta, m, v


def reference(x, pre_mix_gain, post_mix_gain, pre_ffn_gain, post_ffn_gain, w_in, sb_out_gain, dil_out_gain, w_out, w_up, conv_w, conv_b, w_down, loss_target, m_pre_mix_gain, m_post_mix_gain, m_pre_ffn_gain, m_post_ffn_gain, m_w_in, m_sb_out_gain, m_dil_out_gain, m_w_out, m_w_up, m_conv_w, m_conv_b, m_w_down, v_pre_mix_gain, v_post_mix_gain, v_pre_ffn_gain, v_post_ffn_gain, v_w_in, v_sb_out_gain, v_dil_out_gain, v_w_out, v_w_up, v_conv_w, v_conv_b, v_w_down):
    given = dict(x=x, pre_mix_gain=pre_mix_gain, post_mix_gain=post_mix_gain, pre_ffn_gain=pre_ffn_gain, post_ffn_gain=post_ffn_gain, w_in=w_in, sb_out_gain=sb_out_gain, dil_out_gain=dil_out_gain, w_out=w_out, w_up=w_up, conv_w=conv_w, conv_b=conv_b, w_down=w_down, loss_target=loss_target, m_pre_mix_gain=m_pre_mix_gain, m_post_mix_gain=m_post_mix_gain, m_pre_ffn_gain=m_pre_ffn_gain, m_post_ffn_gain=m_post_ffn_gain, m_w_in=m_w_in, m_sb_out_gain=m_sb_out_gain, m_dil_out_gain=m_dil_out_gain, m_w_out=m_w_out, m_w_up=m_w_up, m_conv_w=m_conv_w, m_conv_b=m_conv_b, m_w_down=m_w_down, v_pre_mix_gain=v_pre_mix_gain, v_post_mix_gain=v_post_mix_gain, v_pre_ffn_gain=v_pre_ffn_gain, v_post_ffn_gain=v_post_ffn_gain, v_w_in=v_w_in, v_sb_out_gain=v_sb_out_gain, v_dil_out_gain=v_dil_out_gain, v_w_out=v_w_out, v_w_up=v_w_up, v_conv_w=v_conv_w, v_conv_b=v_conv_b, v_w_down=v_w_down)
    weights = {n: given[n] for n in TWIN_WEIGHTS}
    shared = {n: given[n] for n in SHARED_INPUTS}
    per_example = {n: given[n] for n in ['x']}
    grad_fn = _jax.value_and_grad(_loss, argnums=(0, 1))

    def one_microbatch(ex, loss_target):
        ex = dict(ex)
        diff = ex.pop(TWIN_DIFF_INPUT)
        return grad_fn(weights, diff, {**shared, **ex}, loss_target)

    if N_MICROBATCH == 1:
        loss, (grad_w, grad_x) = one_microbatch(per_example, given["loss_target"])
    else:
        def body(carry, xs):
            loss_sum, grad_sum = carry
            l_k, (gw_k, gx_k) = one_microbatch(xs[0], xs[1])
            with _jax.named_scope("update"):
                return (loss_sum + l_k, _jax.tree.map(_jnp.add, grad_sum, gw_k)), gx_k

        init = (_jnp.zeros((), _jnp.float32), _jax.tree.map(_jnp.zeros_like, weights))
        (loss, grad_w), grad_x = _jax.lax.scan(body, init, (per_example, given["loss_target"]))
    with _jax.named_scope("update"):
        delta_w, new_m, new_v = {}, {}, {}
        for n in TWIN_WEIGHTS:
            delta_w[n], new_m[n], new_v[n] = _adamw(weights[n], grad_w[n], given["m_" + n], given["v_" + n])
    return (loss, grad_x, *[grad_w[n] for n in TWIN_WEIGHTS], *[delta_w[n] for n in TWIN_WEIGHTS],
            *[new_m[n] for n in TWIN_WEIGHTS], *[new_v[n] for n in TWIN_WEIGHTS])
```

```python
import functools
import math
from typing import NamedTuple

import jax
import jax.numpy as jnp
from jax import lax
from jax.experimental import pallas as pl
from jax.experimental.pallas import tpu as pltpu

F32 = jnp.float32
BF16 = jnp.bfloat16
MESH = pl.DeviceIdType.MESH

ROPE_THETA = 10000.0
RMS_EPS = 1e-6
ADAM_LR = 0.001
ADAM_B1 = 0.9
ADAM_B2 = 0.999
ADAM_EPS = 1e-08
ADAM_WD = 0.01
ADAM_STEP = 10
GELU_C = math.sqrt(2.0 / math.pi)
NEG_BIG = -1e30
LANE = 128
N_CHIPS = 4
N_DEV = 8


class Cfg(NamedTuple):
    S: int = 2048
    D: int = 2048
    DH: int = 128
    HSB: int = 8
    HDL: int = 8
    QB: int = 128
    branches: tuple = ((128, 1), (512, 4), (2048, 16))
    FSH: int = 2752
    FSHP: int = 2816
    TM: int = 256
    TNF: int = 256
    TN: int = 512
    VMEM_MB: int = 56

    @property
    def DSB(self):
        return self.HSB * self.DH

    @property
    def DDL(self):
        return self.HDL * self.DH

    @property
    def DMIX(self):
        return self.DSB + self.DDL

    @property
    def FFP(self):
        return 2 * self.FSHP

    @property
    def FF2P(self):
        return 4 * self.FSHP


CFG = Cfg()


def _cp(cfg, sem=None):
    return pltpu.CompilerParams(dimension_semantics=sem, vmem_limit_bytes=cfg.VMEM_MB * 2**20)


def _dot(a, b):
    return jnp.dot(a, b, preferred_element_type=F32)


def _dot_nt(a, b):
    return lax.dot_general(a, b, (((1,), (1,)), ((), ())), preferred_element_type=F32)


def _dot_tn(a, b):
    return lax.dot_general(a, b, (((0,), (0,)), ((), ())), preferred_element_type=F32)


def _dot_split(x, u):
    hi = x.astype(BF16)
    lo = (x - hi.astype(F32)).astype(BF16)
    return _dot(hi, u) + _dot(lo, u)


def _rstd(x):
    return lax.rsqrt(jnp.mean(x * x, axis=-1, keepdims=True) + RMS_EPS)


def _rms_bwd(dy, x, g):
    r = _rstd(x)
    xh = x * r
    dxh = dy * g
    dx = r * (dxh - xh * jnp.mean(dxh * xh, axis=-1, keepdims=True))
    return dx, dy * xh


def _gelu(x):
    t = jnp.tanh(GELU_C * (x + 0.044715 * (x * x * x)))
    return 0.5 * x * (1.0 + t), t


def _gelu_grad(x, t):
    return 0.5 * (1.0 + t) + 0.5 * x * (1.0 - t * t) * (GELU_C * (1.0 + 3 * 0.044715 * (x * x)))


def _row(cfg, w):
    return pl.BlockSpec((cfg.TM, w), lambda i: (i, 0))


def _vec(w):
    return pl.BlockSpec((1, w), lambda i: (0, 0))


def rms_fwd(cfg, x, g):
    S, D = x.shape

    def body(x_ref, g_ref, h_ref):
        xv = x_ref[...]
        h_ref[...] = (xv * _rstd(xv) * g_ref[...]).astype(BF16)

    return pl.pallas_call(
        body, name="rms_fwd", grid=(S // cfg.TM,),
        in_specs=[_row(cfg, D), _vec(D)], out_specs=_row(cfg, D),
        out_shape=jax.ShapeDtypeStruct((S, D), BF16), compiler_params=_cp(cfg, ("parallel",)),
    )(x, g)


def mid_fwd(cfg, x, mo, g_post, g_pre):
    S, D = x.shape

    def body(x_ref, mo_ref, gp_ref, gn_ref, x1_ref, h2_ref):
        mo_v = mo_ref[...]
        x1 = x_ref[...] + mo_v * _rstd(mo_v) * gp_ref[...]
        x1_ref[...] = x1
        h2_ref[...] = (x1 * _rstd(x1) * gn_ref[...]).astype(BF16)

    return pl.pallas_call(
        body, name="mid_fwd", grid=(S // cfg.TM,),
        in_specs=[_row(cfg, D), _row(cfg, D), _vec(D), _vec(D)],
        out_specs=[_row(cfg, D), _row(cfg, D)],
        out_shape=[jax.ShapeDtypeStruct((S, D), F32), jax.ShapeDtypeStruct((S, D), BF16)],
        compiler_params=_cp(cfg, ("parallel",)),
    )(x, mo, g_post, g_pre)


def final_fwd_bwd(cfg, x1, f, g_post, target):
    S, D = x1.shape

    def body(x1_ref, f_ref, g_ref, t_ref, dout_ref, df_ref, dg_ref, loss_ref):
        @pl.when(pl.program_id(0) == 0)
        def _():
            dg_ref[...] = jnp.zeros_like(dg_ref)
            loss_ref[...] = jnp.zeros_like(loss_ref)

        fv = f_ref[...]
        g = g_ref[...]
        out = x1_ref[...] + fv * _rstd(fv) * g
        err = out - t_ref[...]
        loss_ref[...] += 0.5 * jnp.sum(jnp.mean(err * err, axis=-1, keepdims=True), axis=0, keepdims=True)
        dout = err * (1.0 / D)
        dout_ref[...] = dout
        df, dgx = _rms_bwd(dout, fv, g)
        df_ref[...] = df.astype(BF16)
        dg_ref[...] += jnp.sum(dgx, axis=0, keepdims=True)

    return pl.pallas_call(
        body, name="final_fwd_bwd", grid=(S // cfg.TM,),
        in_specs=[_row(cfg, D), _row(cfg, D), _vec(D), _row(cfg, D)],
        out_specs=[_row(cfg, D), _row(cfg, D), _vec(D), _vec(LANE)],
        out_shape=[jax.ShapeDtypeStruct((S, D), F32), jax.ShapeDtypeStruct((S, D), BF16),
                   jax.ShapeDtypeStruct((1, D), F32), jax.ShapeDtypeStruct((1, LANE), F32)],
        compiler_params=_cp(cfg, ("arbitrary",)),
    )(x1, f, g_post, target)


def mid_bwd(cfg, dh2, x1, g_pre, dout, mo, g_post):
    S, D = x1.shape

    def body(dh_ref, x1_ref, gn_ref, do_ref, mo_ref, gp_ref, dx1_ref, dmo_ref, dgn_ref, dgp_ref):
        @pl.when(pl.program_id(0) == 0)
        def _():
            dgn_ref[...] = jnp.zeros_like(dgn_ref)
            dgp_ref[...] = jnp.zeros_like(dgp_ref)

        dx, dgx = _rms_bwd(dh_ref[...], x1_ref[...], gn_ref[...])
        dx1 = do_ref[...] + dx
        dx1_ref[...] = dx1
        dgn_ref[...] += jnp.sum(dgx, axis=0, keepdims=True)
        dmo, dgy = _rms_bwd(dx1, mo_ref[...], gp_ref[...])
        dmo_ref[...] = dmo.astype(BF16)
        dgp_ref[...] += jnp.sum(dgy, axis=0, keepdims=True)

    return pl.pallas_call(
        body, name="mid_bwd", grid=(S // cfg.TM,),
        in_specs=[_row(cfg, D), _row(cfg, D), _vec(D), _row(cfg, D), _row(cfg, D), _vec(D)],
        out_specs=[_row(cfg, D), _row(cfg, D), _vec(D), _vec(D)],
        out_shape=[jax.ShapeDtypeStruct((S, D), F32), jax.ShapeDtypeStruct((S, D), BF16),
                   jax.ShapeDtypeStruct((1, D), F32), jax.ShapeDtypeStruct((1, D), F32)],
        compiler_params=_cp(cfg, ("arbitrary",)),
    )(dh2, x1, g_pre, dout, mo, g_post)


def first_bwd(cfg, dh1, x, g_pre, dx1):
    S, D = x.shape

    def body(dh_ref, x_ref, g_ref, r_ref, dx_ref, dg_ref):
        @pl.when(pl.program_id(0) == 0)
        def _():
            dg_ref[...] = jnp.zeros_like(dg_ref)

        dx, dgx = _rms_bwd(dh_ref[...], x_ref[...], g_ref[...])
        dx_ref[...] = r_ref[...] + dx
        dg_ref[...] += jnp.sum(dgx, axis=0, keepdims=True)

    return pl.pallas_call(
        body, name="first_bwd", grid=(S // cfg.TM,),
        in_specs=[_row(cfg, D), _row(cfg, D), _vec(D), _row(cfg, D)],
        out_specs=[_row(cfg, D), _vec(D)],
        out_shape=[jax.ShapeDtypeStruct((S, D), F32), jax.ShapeDtypeStruct((1, D), F32)],
        compiler_params=_cp(cfg, ("arbitrary",)),
    )(dh1, x, g_pre, dx1)


def _mm(cfg, name, a, b, *, nt, a_spec, b_spec, o_spec, grid, out_shape, acc_shape):
    nk = grid[-1]
    dot = _dot_nt if nt else _dot

    def body(a_ref, b_ref, o_ref, acc_ref):
        k = pl.program_id(len(grid) - 1)
        part = dot(a_ref[...], b_ref[...])
        if nk == 1:
            o_ref[...] = part.astype(o_ref.dtype)
            return

        @pl.when(k == 0)
        def _():
            acc_ref[...] = part

        @pl.when(k > 0)
        def _():
            acc_ref[...] += part

        @pl.when(k == nk - 1)
        def _():
            o_ref[...] = acc_ref[...].astype(o_ref.dtype)

    sem = ("parallel",) * (len(grid) - 1) + ("arbitrary",)
    return pl.pallas_call(
        body, name=name, grid=grid, in_specs=[a_spec, b_spec], out_specs=o_spec, out_shape=out_shape,
        scratch_shapes=[pltpu.VMEM(acc_shape, F32)], compiler_params=_cp(cfg, sem),
    )(a, b)


def _mm_tn(cfg, name, a, b, *, a_spec, b_spec, o_spec, grid, out_shape):
    def body(a_ref, b_ref, o_ref):
        o_ref[...] = _dot_tn(a_ref[...], b_ref[...]).astype(o_ref.dtype)

    return pl.pallas_call(
        body, name=name, grid=grid, in_specs=[a_spec, b_spec], out_specs=o_spec, out_shape=out_shape,
        compiler_params=_cp(cfg, ("parallel",) * len(grid)),
    )(a, b)


def qkv_proj(cfg, h1, w_in, cos2, sin2):
    S, D = h1.shape
    tn = 2 * cfg.DH
    per = cfg.DSB // tn
    assert cfg.DSB == cfg.DDL
    nblk = 6 * per

    def body(a_ref, b_ref, c_ref, s_ref, o_ref):
        j = pl.program_id(0)
        acc = _dot(a_ref[...], b_ref[...])
        rope = jnp.logical_and(j >= 3 * per, j < 5 * per)

        @pl.when(rope)
        def _():
            for c in range(tn // cfg.DH):
                xh = acc[:, c * cfg.DH:(c + 1) * cfg.DH]
                o_ref[:, c * cfg.DH:(c + 1) * cfg.DH] = (
                    xh * c_ref[...] + pltpu.roll(xh, cfg.DH // 2, 1) * s_ref[...]).astype(BF16)

        @pl.when(jnp.logical_not(rope))
        def _():
            o_ref[...] = acc.astype(BF16)

    return pl.pallas_call(
        body, name="qkv_proj", grid=(nblk,),
        in_specs=[pl.BlockSpec((S, D), lambda j: (0, 0)), pl.BlockSpec((D, tn), lambda j: (0, j)),
                  pl.BlockSpec((S, cfg.DH), lambda j: (0, 0)), pl.BlockSpec((S, cfg.DH), lambda j: (0, 0))],
        out_specs=pl.BlockSpec((None, S, tn), lambda j: (j // per, 0, j % per)),
        out_shape=jax.ShapeDtypeStruct((6, S, cfg.DSB), BF16),
        compiler_params=_cp(cfg, ("parallel",)),
    )(h1, w_in, cos2, sin2)


def _sb_tile(cfg, q, k, kb, qb):
    QB = cfg.QB
    z = _dot_nt(q, k) * (cfg.DH ** -0.5)
    t1 = jnp.log1p(jnp.exp(-jnp.abs(z)))
    lb = jnp.minimum(z, 0.0) - t1
    row = lax.broadcasted_iota(jnp.int32, (QB, QB), 0)
    col = lax.broadcasted_iota(jnp.int32, (QB, QB), 1)
    valid = jnp.logical_or(kb < qb, col < row)
    lk = jnp.where(valid, jnp.minimum(-z, 0.0) - t1, 0.0)
    return lb, lk, valid


def sb_fwd(cfg, qkv3):
    S, QB, DH = cfg.S, cfg.QB, cfg.DH

    def body(q_ref, k_ref, v_ref, o_ref, t_ref):
        row = lax.broadcasted_iota(jnp.int32, (QB, QB), 0)
        col = lax.broadcasted_iota(jnp.int32, (QB, QB), 1)
        u_after = (row > col).astype(BF16)

        def q_loop(qb, _):
            rows = pl.ds(pl.multiple_of(qb * QB, QB), QB)
            q = q_ref[rows, :]

            def k_loop(i, carry):
                o_acc, c = carry
                kb = qb - i
                krows = pl.ds(pl.multiple_of(kb * QB, QB), QB)
                lb, lk, valid = _sb_tile(cfg, q, k_ref[krows, :], kb, qb)
                rem = _dot_split(lk, u_after) + c
                a = jnp.where(valid, jnp.exp(lb + rem), 0.0)
                o_acc = o_acc + _dot(a.astype(BF16), v_ref[krows, :])
                return o_acc, c + jnp.sum(lk, axis=1, keepdims=True)

            o_acc, c = lax.fori_loop(0, qb + 1, k_loop, (jnp.zeros((QB, DH), F32), jnp.zeros((QB, 1), F32)))
            o_ref[rows, :] = o_acc
            t_ref[rows, :] = jnp.broadcast_to(c, (QB, DH))
            return 0

        lax.fori_loop(0, S // QB, q_loop, 0)

    def spec(i):
        return pl.BlockSpec((None, S, DH), lambda h: (i, 0, h))

    return pl.pallas_call(
        body, name="sb_fwd", grid=(cfg.HSB,),
        in_specs=[spec(0), spec(1), spec(2)],
        out_specs=[pl.BlockSpec((S, DH), lambda h: (0, h))] * 2,
        out_shape=[jax.ShapeDtypeStruct((S, cfg.DSB), F32)] * 2,
        compiler_params=_cp(cfg, ("parallel",)),
    )(qkv3, qkv3, qkv3)


def sb_bwd(cfg, qkv3, do_sb, tsum):
    S, QB, DH = cfg.S, cfg.QB, cfg.DH
    scale = DH ** -0.5

    def body(q_ref, k_ref, v_ref, do_ref, t_ref, d_ref, dk_acc, dv_acc):
        dk_acc[...] = jnp.zeros_like(dk_acc)
        dv_acc[...] = jnp.zeros_like(dv_acc)
        row = lax.broadcasted_iota(jnp.int32, (QB, QB), 0)
        col = lax.broadcasted_iota(jnp.int32, (QB, QB), 1)
        u_upto = (row <= col).astype(BF16)
        u_before = (row < col).astype(BF16)

        def q_loop(qb, _):
            rows = pl.ds(pl.multiple_of(qb * QB, QB), QB)
            q = q_ref[rows, :]
            do = do_ref[rows, :]
            total = t_ref[rows, 0:1]

            def k_loop(kb, carry):
                dq_acc, pc, gc = carry
                krows = pl.ds(pl.multiple_of(kb * QB, QB), QB)
                k = k_ref[krows, :]
                v = v_ref[krows, :]
                lb, lk, valid = _sb_tile(cfg, q, k, kb, qb)
                rem = total - pc - _dot_split(lk, u_upto)
                a = jnp.where(valid, jnp.exp(lb + rem), 0.0)
                g = a * _dot_nt(do, v)
                dv_acc[krows, :] += _dot_tn(a.astype(BF16), do)
                cum = gc + _dot_split(g, u_before)
                sig = jnp.exp(lb)
                dz = (jnp.where(valid, g * (1.0 - sig) - cum * sig, 0.0) * scale).astype(BF16)
                dq_acc = dq_acc + _dot(dz, k)
                dk_acc[krows, :] += _dot_tn(dz, q)
                return dq_acc, pc + jnp.sum(lk, axis=1, keepdims=True), gc + jnp.sum(g, axis=1, keepdims=True)

            z1 = jnp.zeros((QB, 1), F32)
            dq_acc, _, _ = lax.fori_loop(0, qb + 1, k_loop, (jnp.zeros((QB, DH), F32), z1, z1))
            d_ref[0, rows, :] = dq_acc.astype(BF16)
            return 0

        lax.fori_loop(0, S // QB, q_loop, 0)
        d_ref[1, :, :] = dk_acc[...].astype(BF16)
        d_ref[2, :, :] = dv_acc[...].astype(BF16)

    def spec(i):
        return pl.BlockSpec((None, S, DH), lambda h: (i, 0, h))

    return pl.pallas_call(
        body, name="sb_bwd", grid=(cfg.HSB,),
        in_specs=[spec(0), spec(1), spec(2), pl.BlockSpec((S, DH), lambda h: (0, h)),
                  pl.BlockSpec((S, DH), lambda h: (0, h))],
        out_specs=pl.BlockSpec((3, S, DH), lambda h: (0, 0, h)),
        out_shape=jax.ShapeDtypeStruct((3, S, cfg.DSB), BF16),
        scratch_shapes=[pltpu.VMEM((S, DH), F32), pltpu.VMEM((S, DH), F32)],
        compiler_params=_cp(cfg, ("parallel",)),
    )(qkv3, qkv3, qkv3, do_sb, tsum)


def _band_mask(cfg, n, n_back):
    QB = cfg.QB
    qi = lax.broadcasted_iota(jnp.int32, (QB, 2 * QB), 0)
    kj = lax.broadcasted_iota(jnp.int32, (QB, 2 * QB), 1)
    dist = QB + qi - kj
    return (dist >= 0) & (dist <= n_back) & jnp.logical_or(n > 0, kj >= QB)


def dil_fwd(cfg, qkv3, window, dil):
    S, QB, DH, H = cfg.S, cfg.QB, cfg.DH, cfg.HDL
    L = S // dil
    n_back = window // dil
    assert n_back <= QB and L % QB == 0
    scale = DH ** -0.5
    view = qkv3.reshape(6, L, dil * cfg.DDL)

    def body(q_ref, k_ref, v_ref, o_ref, lse_ref, kp, vp):
        kp[pl.ds(0, QB), :] = jnp.zeros((QB, DH), BF16)
        vp[pl.ds(0, QB), :] = jnp.zeros((QB, DH), BF16)
        kp[pl.ds(QB, L), :] = k_ref[...]
        vp[pl.ds(QB, L), :] = v_ref[...]

        def loop(n, _):
            rows = pl.ds(pl.multiple_of(n * QB, QB), QB)
            band = pl.ds(pl.multiple_of(n * QB, QB), 2 * QB)
            s = _dot_nt(q_ref[rows, :], kp[band, :]) * scale
            s = jnp.where(_band_mask(cfg, n, n_back), s, NEG_BIG)
            m = jnp.max(s, axis=1, keepdims=True)
            p = jnp.exp(s - m)
            den = jnp.sum(p, axis=1, keepdims=True)
            o_ref[rows, :] = _dot(p.astype(BF16), vp[band, :]) / den
            lse_ref[rows, :] = jnp.broadcast_to(m + jnp.log(den), (QB, DH))
            return 0

        lax.fori_loop(0, L // QB, loop, 0)

    def spec(i):
        return pl.BlockSpec((None, L, DH), lambda h, r: (i, 0, r * H + h))

    o_spec = pl.BlockSpec((L, DH), lambda h, r: (0, r * H + h))
    o, lse = pl.pallas_call(
        body, name=f"dil_fwd_d{dil}", grid=(H, dil),
        in_specs=[spec(3), spec(4), spec(5)], out_specs=[o_spec, o_spec],
        out_shape=[jax.ShapeDtypeStruct((L, dil * cfg.DDL), F32)] * 2,
        scratch_shapes=[pltpu.VMEM((L + QB, DH), BF16)] * 2,
        compiler_params=_cp(cfg, ("parallel", "parallel")),
    )(view, view, view)
    return o.reshape(S, cfg.DDL), lse.reshape(S, cfg.DDL)


def dil_bwd(cfg, qkv3, do_dl, delta, lse_tot, window, dil):
    S, QB, DH, H = cfg.S, cfg.QB, cfg.DH, cfg.HDL
    L = S // dil
    n_back = window // dil
    scale = DH ** -0.5
    view = qkv3.reshape(6, L, dil * cfg.DDL)
    dov, dlv, lsv = (t.reshape(L, dil * cfg.DDL) for t in (do_dl, delta, lse_tot))

    def body(q_ref, k_ref, v_ref, do_ref, dl_ref, ls_ref, dq_ref, dk_ref, dv_ref, kp, vp, dkp, dvp):
        kp[pl.ds(0, QB), :] = jnp.zeros((QB, DH), BF16)
        vp[pl.ds(0, QB), :] = jnp.zeros((QB, DH), BF16)
        kp[pl.ds(QB, L), :] = k_ref[...]
        vp[pl.ds(QB, L), :] = v_ref[...]
        dkp[...] = jnp.zeros_like(dkp)
        dvp[...] = jnp.zeros_like(dvp)

        def loop(n, _):
            rows = pl.ds(pl.multiple_of(n * QB, QB), QB)
            band = pl.ds(pl.multiple_of(n * QB, QB), 2 * QB)
            q = q_ref[rows, :]
            do = do_ref[rows, :]
            kb = kp[band, :]
            s = _dot_nt(q, kb) * scale
            s = jnp.where(_band_mask(cfg, n, n_back), s, NEG_BIG)
            p = jnp.exp(s - ls_ref[rows, 0:1])
            ds = (p * (_dot_nt(do, vp[band, :]) - dl_ref[rows, 0:1]) * scale).astype(BF16)
            dq_ref[rows, :] = _dot(ds, kb)
            dkp[band, :] += _dot_tn(ds, q)
            dvp[band, :] += _dot_tn(p.astype(BF16), do)
            return 0

        lax.fori_loop(0, L // QB, loop, 0)
        dk_ref[...] = dkp[pl.ds(QB, L), :]
        dv_ref[...] = dvp[pl.ds(QB, L), :]

    def spec(i):
        return pl.BlockSpec((None, L, DH), lambda h, r: (i, 0, r * H + h))

    o_spec = pl.BlockSpec((L, DH), lambda h, r: (0, r * H + h))
    outs = pl.pallas_call(
        body, name=f"dil_bwd_d{dil}", grid=(H, dil),
        in_specs=[spec(3), spec(4), spec(5), o_spec, o_spec, o_spec], out_specs=[o_spec] * 3,
        out_shape=[jax.ShapeDtypeStruct((L, dil * cfg.DDL), F32)] * 3,
        scratch_shapes=[pltpu.VMEM((L + QB, DH), BF16)] * 2 + [pltpu.VMEM((L + QB, DH), F32)] * 2,
        compiler_params=_cp(cfg, ("parallel", "parallel")),
    )(view, view, view, dov, dlv, lsv)
    return [t.reshape(S, cfg.DDL) for t in outs]


def combine_fwd(cfg, o_sb, obs, lses, g_sb, g_dl):
    S, DH = cfg.S, cfg.DH
    nb = len(obs)

    def head_norm(o, g):
        return o * lax.rsqrt(jnp.mean(o * o, axis=-1, keepdims=True) + RMS_EPS) * g

    def body(*refs):
        osb_ref, ob_refs, l_refs = refs[0], refs[1:1 + nb], refs[1 + nb:1 + 2 * nb]
        gsb_ref, gdl_ref, mix_ref, odl_ref, lt_ref = refs[1 + 2 * nb:]
        for h in range(cfg.HSB):
            c = slice(h * DH, (h + 1) * DH)
            mix_ref[:, c] = head_norm(osb_ref[:, c], gsb_ref[:, c]).astype(BF16)
        for h in range(cfg.HDL):
            c = slice(h * DH, (h + 1) * DH)
            ls = [r[:, c] for r in l_refs]
            m = functools.reduce(jnp.maximum, ls)
            es = [jnp.exp(l - m) for l in ls]
            tot = functools.reduce(jnp.add, es)
            o = functools.reduce(jnp.add, [(e / tot) * r[:, c] for e, r in zip(es, ob_refs)])
            odl_ref[:, c] = o
            lt_ref[:, c] = m + jnp.log(tot)
            mix_ref[:, cfg.DSB + h * DH:cfg.DSB + (h + 1) * DH] = head_norm(o, gdl_ref[:, c]).astype(BF16)

    return pl.pallas_call(
        body, name="combine_fwd", grid=(S // cfg.TM,),
        in_specs=[_row(cfg, cfg.DSB)] + [_row(cfg, cfg.DDL)] * (2 * nb) + [_vec(cfg.DSB), _vec(cfg.DDL)],
        out_specs=[_row(cfg, cfg.DMIX), _row(cfg, cfg.DDL), _row(cfg, cfg.DDL)],
        out_shape=[jax.ShapeDtypeStruct((S, cfg.DMIX), BF16), jax.ShapeDtypeStruct((S, cfg.DDL), F32),
                   jax.ShapeDtypeStruct((S, cfg.DDL), F32)],
        compiler_params=_cp(cfg, ("parallel",)),
    )(o_sb, *obs, *lses, g_sb, g_dl)


def combine_bwd(cfg, dmix, o_sb, o_dl, g_sb, g_dl):
    S, DH = cfg.S, cfg.DH

    def body(dm_ref, osb_ref, odl_ref, gsb_ref, gdl_ref, dsb_ref, ddl_ref, dl_ref, dgsb_ref, dgdl_ref):
        @pl.when(pl.program_id(0) == 0)
        def _():
            dgsb_ref[...] = jnp.zeros_like(dgsb_ref)
            dgdl_ref[...] = jnp.zeros_like(dgdl_ref)

        for h in range(cfg.HSB):
            c = slice(h * DH, (h + 1) * DH)
            dx, dgx = _rms_bwd(dm_ref[:, c], osb_ref[:, c], gsb_ref[:, c])
            dsb_ref[:, c] = dx.astype(BF16)
            dgsb_ref[:, c] += jnp.sum(dgx, axis=0, keepdims=True)
        for h in range(cfg.HDL):
            c = slice(h * DH, (h + 1) * DH)
            o = odl_ref[:, c]
            dx, dgx = _rms_bwd(dm_ref[:, cfg.DSB + h * DH:cfg.DSB + (h + 1) * DH], o, gdl_ref[:, c])
            ddl_ref[:, c] = dx.astype(BF16)
            dl_ref[:, c] = jnp.broadcast_to(jnp.sum(dx * o, axis=-1, keepdims=True), dx.shape)
            dgdl_ref[:, c] += jnp.sum(dgx, axis=0, keepdims=True)

    return pl.pallas_call(
        body, name="combine_bwd", grid=(S // cfg.TM,),
        in_specs=[_row(cfg, cfg.DMIX), _row(cfg, cfg.DSB), _row(cfg, cfg.DDL), _vec(cfg.DSB), _vec(cfg.DDL)],
        out_specs=[_row(cfg, cfg.DSB), _row(cfg, cfg.DDL), _row(cfg, cfg.DDL), _vec(cfg.DSB), _vec(cfg.DDL)],
        out_shape=[jax.ShapeDtypeStruct((S, cfg.DSB), BF16), jax.ShapeDtypeStruct((S, cfg.DDL), BF16),
                   jax.ShapeDtypeStruct((S, cfg.DDL), F32), jax.ShapeDtypeStruct((1, cfg.DSB), F32),
                   jax.ShapeDtypeStruct((1, cfg.DDL), F32)],
        compiler_params=_cp(cfg, ("arbitrary",)),
    )(dmix, o_sb, o_dl, g_sb, g_dl)


def rope_bwd_sum(cfg, parts, cos2, sin2):
    S, DH = cfg.S, cfg.DH
    nb = len(parts)

    def body(*refs):
        c_ref, s_ref, o_ref = refs[3 * nb], refs[3 * nb + 1], refs[3 * nb + 2]
        for i in range(3):
            for h in range(cfg.HDL):
                c = slice(h * DH, (h + 1) * DH)
                d = functools.reduce(jnp.add, [refs[b * 3 + i][:, c] for b in range(nb)])
                if i < 2:
                    d = d * c_ref[...] + pltpu.roll(d * s_ref[...], DH // 2, 1)
                o_ref[i, :, c] = d.astype(BF16)

    flat = [t for p in parts for t in p]
    return pl.pallas_call(
        body, name="rope_bwd_sum", grid=(S // cfg.TM,),
        in_specs=[_row(cfg, cfg.DDL)] * (3 * nb) + [_row(cfg, DH), _row(cfg, DH)],
        out_specs=pl.BlockSpec((3, cfg.TM, cfg.DDL), lambda i: (0, i, 0)),
        out_shape=jax.ShapeDtypeStruct((3, S, cfg.DDL), BF16),
        compiler_params=_cp(cfg, ("parallel",)),
    )(*flat, cos2, sin2)


def _shift_rows(u, j):
    row = lax.broadcasted_iota(jnp.int32, u.shape, 0)
    return jnp.where(row >= j, pltpu.roll(u, j, 0), 0.0)


def _shift_rows_up(u, j):
    n = u.shape[0]
    row = lax.broadcasted_iota(jnp.int32, u.shape, 0)
    return jnp.where(row < n - j, pltpu.roll(u, n - j, 0), 0.0)


def _conv(u, cw, cb):
    return u * cw[2:3, :] + _shift_rows(u, 1) * cw[1:2, :] + _shift_rows(u, 2) * cw[0:1, :] + cb


def ffn_fwd(cfg, h2, w_up, conv_w, conv_b):
    S, D = h2.shape
    tn, nt = cfg.TNF, cfg.FFP // cfg.TNF

    def body(h_ref, wg_ref, wv_ref, cwg_ref, cwv_ref, cbg_ref, cbv_ref, u_ref, y_ref):
        h = h_ref[...]
        ug = _dot(h, wg_ref[...])
        uv = _dot(h, wv_ref[...])
        u_ref[0] = ug
        u_ref[1] = uv
        gl, _ = _gelu(_conv(ug, cwg_ref[...], cbg_ref[...]))
        y_ref[...] = (gl * _conv(uv, cwv_ref[...], cbv_ref[...])).astype(BF16)

    return pl.pallas_call(
        body, name="ffn_fwd", grid=(nt,),
        in_specs=[pl.BlockSpec((S, D), lambda n: (0, 0)),
                  pl.BlockSpec((D, tn), lambda n: (0, n)), pl.BlockSpec((D, tn), lambda n: (0, n + nt)),
                  pl.BlockSpec((3, tn), lambda n: (0, n)), pl.BlockSpec((3, tn), lambda n: (0, n + nt)),
                  pl.BlockSpec((1, tn), lambda n: (0, n)), pl.BlockSpec((1, tn), lambda n: (0, n + nt))],
        out_specs=[pl.BlockSpec((2, S, tn), lambda n: (0, 0, n)), pl.BlockSpec((S, tn), lambda n: (0, n))],
        out_shape=[jax.ShapeDtypeStruct((2, S, cfg.FFP), F32), jax.ShapeDtypeStruct((S, cfg.FFP), BF16)],
        compiler_params=_cp(cfg, ("parallel",)),
    )(h2, w_up, w_up, conv_w, conv_w, conv_b, conv_b)


def ffn_bwd(cfg, df, w_down, u, conv_w, conv_b):
    S, D = df.shape
    tn, nt = cfg.TNF, cfg.FFP // cfg.TNF

    def conv_bwd(dc, uu, cw):
        du = dc * cw[2:3, :] + _shift_rows_up(dc, 1) * cw[1:2, :] + _shift_rows_up(dc, 2) * cw[0:1, :]
        dws = [jnp.sum(dc * _shift_rows(uu, 2), axis=0, keepdims=True),
               jnp.sum(dc * _shift_rows(uu, 1), axis=0, keepdims=True),
               jnp.sum(dc * uu, axis=0, keepdims=True)]
        return du, dws, jnp.sum(dc, axis=0, keepdims=True)

    def body(df_ref, wd_ref, u_ref, cwg_ref, cwv_ref, cbg_ref, cbv_ref, du_ref, dwd_ref, dcw_ref, dcb_ref):
        dfv = df_ref[...]
        dy = _dot_nt(dfv, wd_ref[...])
        ug, uv = u_ref[0], u_ref[1]
        cwg, cwv = cwg_ref[...], cwv_ref[...]
        cg = _conv(ug, cwg, cbg_ref[...])
        cv = _conv(uv, cwv, cbv_ref[...])
        gl, t = _gelu(cg)
        dwd_ref[...] = _dot_tn((gl * cv).astype(BF16), dfv).astype(BF16)
        dug, dwg, dbg = conv_bwd(dy * cv * _gelu_grad(cg, t), ug, cwg)
        duv, dwv, dbv = conv_bwd(dy * gl, uv, cwv)
        du_ref[0] = dug.astype(BF16)
        du_ref[1] = duv.astype(BF16)
        for j in range(3):
            dcw_ref[0, j:j + 1, :] = dwg[j]
            dcw_ref[1, j:j + 1, :] = dwv[j]
        dcb_ref[0] = dbg
        dcb_ref[1] = dbv

    return pl.pallas_call(
        body, name="ffn_bwd", grid=(nt,),
        in_specs=[pl.BlockSpec((S, D), lambda n: (0, 0)), pl.BlockSpec((tn, D), lambda n: (n, 0)),
                  pl.BlockSpec((2, S, tn), lambda n: (0, 0, n)),
                  pl.BlockSpec((3, tn), lambda n: (0, n)), pl.BlockSpec((3, tn), lambda n: (0, n + nt)),
                  pl.BlockSpec((1, tn), lambda n: (0, n)), pl.BlockSpec((1, tn), lambda n: (0, n + nt))],
        out_specs=[pl.BlockSpec((2, S, tn), lambda n: (0, 0, n)), pl.BlockSpec((tn, D), lambda n: (n, 0)),
                   pl.BlockSpec((2, 3, tn), lambda n: (0, 0, n)), pl.BlockSpec((2, 1, tn), lambda n: (0, 0, n))],
        out_shape=[jax.ShapeDtypeStruct((2, S, cfg.FFP), BF16), jax.ShapeDtypeStruct((cfg.FFP, D), BF16),
                   jax.ShapeDtypeStruct((2, 3, cfg.FFP), F32), jax.ShapeDtypeStruct((2, 1, cfg.FFP), F32)],
        compiler_params=_cp(cfg, ("parallel",)),
    )(df, w_down, u, conv_w, conv_w, conv_b, conv_b)


def rope_tables(cfg):
    inv_freq = ROPE_THETA ** (-jnp.arange(0, cfg.DH, 2, dtype=F32) / cfg.DH)
    ang = jnp.arange(cfg.S, dtype=F32)[:, None] * inv_freq[None, :]
    cos, sin = jnp.cos(ang), jnp.sin(ang)
    return jnp.concatenate([cos, cos], axis=1), jnp.concatenate([-sin, sin], axis=1)


def local_step(cfg, x, target, g1, g2, g3, g4, g_sb, g_dl, w_in, w_out, w_up, conv_w, conv_b, w_down):
    S, D = cfg.S, cfg.D
    cos2, sin2 = rope_tables(cfg)
    full = lambda r, c: pl.BlockSpec((r, c), lambda j, k: (0, 0))

    h1 = rms_fwd(cfg, x, g1)
    qkv3 = qkv_proj(cfg, h1, w_in, cos2, sin2)
    o_sb, tsum = sb_fwd(cfg, qkv3)
    obs, lses = zip(*[dil_fwd(cfg, qkv3, w, d) for w, d in cfg.branches])
    mixed, o_dl, lse_tot = combine_fwd(cfg, o_sb, obs, lses, g_sb, g_dl)
    tn = cfg.TN
    mo = _mm(cfg, "mix_out", mixed, w_out, nt=False, grid=(D // tn, 1),
             a_spec=full(S, cfg.DMIX), b_spec=pl.BlockSpec((cfg.DMIX, tn), lambda j, k: (0, j)),
             o_spec=pl.BlockSpec((S, tn), lambda j, k: (0, j)),
             out_shape=jax.ShapeDtypeStruct((S, D), F32), acc_shape=(8, LANE))
    x1, h2 = mid_fwd(cfg, x, mo, g2, g3)
    u, y = ffn_fwd(cfg, h2, w_up, conv_w, conv_b)
    tk = cfg.FFP // 4
    f = _mm(cfg, "ffn_down", y, w_down, nt=False, grid=(D // tn, 4),
            a_spec=pl.BlockSpec((S, tk), lambda j, k: (0, k)), b_spec=pl.BlockSpec((tk, tn), lambda j, k: (k, j)),
            o_spec=pl.BlockSpec((S, tn), lambda j, k: (0, j)),
            out_shape=jax.ShapeDtypeStruct((S, D), F32), acc_shape=(S, tn))
    dout, df, dg4, loss = final_fwd_bwd(cfg, x1, f, g4, target)

    du, dw_down, dconv_w, dconv_b = ffn_bwd(cfg, df, w_down, u, conv_w, conv_b)
    kt = cfg.FFP // tk
    dh2 = _mm(cfg, "d_h2", du, w_up, nt=True, grid=(D // tn, 2 * kt),
              a_spec=pl.BlockSpec((None, S, tk), lambda j, k: (k // kt, 0, k % kt)),
              b_spec=pl.BlockSpec((tn, tk), lambda j, k: (j, k)),
              o_spec=pl.BlockSpec((S, tn), lambda j, k: (0, j)),
              out_shape=jax.ShapeDtypeStruct((S, D), F32), acc_shape=(S, tn))
    nf = cfg.FFP // tn if cfg.FFP % tn == 0 else None
    tnu = tn if nf else cfg.TNF
    nf = cfg.FFP // tnu
    dw_up = _mm_tn(cfg, "d_w_up", h2, du, grid=(2 * nf,),
                   a_spec=pl.BlockSpec((S, D), lambda j: (0, 0)),
                   b_spec=pl.BlockSpec((None, S, tnu), lambda j: (j // nf, 0, j % nf)),
                   o_spec=pl.BlockSpec((D, tnu), lambda j: (0, j)),
                   out_shape=jax.ShapeDtypeStruct((D, cfg.FF2P), BF16))
    dx1, dmo, dg3, dg2 = mid_bwd(cfg, dh2, x1, g3, dout, mo, g2)

    dmix = _mm(cfg, "d_mixed", dmo, w_out, nt=True, grid=(cfg.DMIX // tn, 1),
               a_spec=full(S, D), b_spec=pl.BlockSpec((tn, D), lambda j, k: (j, 0)),
               o_spec=pl.BlockSpec((S, tn), lambda j, k: (0, j)),
               out_shape=jax.ShapeDtypeStruct((S, cfg.DMIX), F32), acc_shape=(8, LANE))
    dw_out = _mm_tn(cfg, "d_w_out", mixed, dmo, grid=(D // tn,),
                    a_spec=pl.BlockSpec((S, cfg.DMIX), lambda j: (0, 0)),
                    b_spec=pl.BlockSpec((S, tn), lambda j: (0, j)),
                    o_spec=pl.BlockSpec((cfg.DMIX, tn), lambda j: (0, j)),
                    out_shape=jax.ShapeDtypeStruct((cfg.DMIX, D), BF16))
    do_sb, do_dl, delta, dg_sb, dg_dl = combine_bwd(cfg, dmix, o_sb, o_dl, g_sb, g_dl)
    d_sb3 = sb_bwd(cfg, qkv3, do_sb, tsum)
    parts = [dil_bwd(cfg, qkv3, do_dl, delta, lse_tot, w, d) for w, d in cfg.branches]
    d_dl3 = rope_bwd_sum(cfg, parts, cos2, sin2)
    dqkv3 = jnp.concatenate([d_sb3, d_dl3], axis=0)
    tkq = min(tn, cfg.DSB)
    kq = cfg.DSB // tkq
    dh1 = _mm(cfg, "d_h1", dqkv3, w_in, nt=True, grid=(D // tn, 6 * kq),
              a_spec=pl.BlockSpec((None, S, tkq), lambda j, k: (k // kq, 0, k % kq)),
              b_spec=pl.BlockSpec((tn, tkq), lambda j, k: (j, k)),
              o_spec=pl.BlockSpec((S, tn), lambda j, k: (0, j)),
              out_shape=jax.ShapeDtypeStruct((S, D), F32), acc_shape=(S, tn))
    dw_in = _mm_tn(cfg, "d_w_in", h1, dqkv3, grid=(6 * kq,),
                   a_spec=pl.BlockSpec((S, D), lambda j: (0, 0)),
                   b_spec=pl.BlockSpec((None, S, tkq), lambda j: (j // kq, 0, j % kq)),
                   o_spec=pl.BlockSpec((D, tkq), lambda j: (0, j)),
                   out_shape=jax.ShapeDtypeStruct((D, 6 * cfg.DSB), BF16))
    grad_x, dg1 = first_bwd(cfg, dh1, x, g1, dx1)
    small = dict(loss=loss, g1=dg1, g2=dg2, g3=dg3, g4=dg4, g_sb=dg_sb, g_dl=dg_dl,
                 conv_b=dconv_b.reshape(1, cfg.FF2P), conv_w=dconv_w.transpose(1, 0, 2).reshape(3, cfg.FF2P))
    return grad_x, small, dict(w_in=dw_in, w_out=dw_out, w_up=dw_up, w_down=dw_down)


ANY = pl.BlockSpec(memory_space=pl.ANY)


def _me():
    return lax.axis_index("x"), lax.axis_index("y"), lax.axis_index("c")


def _other_chips(x, y):
    return [(1 - x, y), (x, 1 - y), (1 - x, 1 - y)]


def cast_shards(cfg, w_in, w_out, w_up, w_down, conv_w):
    D = cfg.D

    def cast(name, w, tm, pad_to=None):
        r, c = w.shape
        cp = pad_to or c
        dt = BF16 if name != "conv_w" else F32

        def body(w_ref, o_ref):
            o_ref[:, :c] = w_ref[...].astype(dt)
            if cp > c:
                o_ref[:, c:] = jnp.zeros((tm, cp - c), dt)

        return pl.pallas_call(
            body, name=f"cast_{name}", grid=(r // tm,),
            in_specs=[pl.BlockSpec((tm, c), lambda i: (i, 0))], out_specs=pl.BlockSpec((tm, cp), lambda i: (i, 0)),
            out_shape=jax.ShapeDtypeStruct((r, cp), dt), compiler_params=_cp(cfg, ("parallel",)),
        )(w)

    def cast_cols(name, w, tc):
        r, c = w.shape

        def body(w_ref, o_ref):
            o_ref[...] = w_ref[...].astype(BF16)

        blk = pl.BlockSpec((r, tc), lambda i: (0, i))
        return pl.pallas_call(
            body, name=f"cast_{name}", grid=(c // tc,), in_specs=[blk], out_specs=blk,
            out_shape=jax.ShapeDtypeStruct((r, c), BF16), compiler_params=_cp(cfg, ("parallel",)),
        )(w)

    return (cast("w_in", w_in, cfg.TM), cast("w_out", w_out, cfg.TM), cast("w_up", w_up, cfg.TM, cfg.FSHP),
            cast_cols("w_down", w_down, cfg.TN), cast("conv_w", conv_w, 3, cfg.FSHP))


def gather_weights(cfg, sh_in, sh_out, sh_up, sh_down, sh_cw):
    D = cfg.D
    cin, rout, rdn = sh_in.shape[1], sh_out.shape[0], sh_down.shape[0]
    zpad = jnp.zeros((cfg.FSHP - cfg.FSH, D), BF16)
    n_arr = 5

    def slots(refs, k):
        g_in, g_out, g_up, g_dn, g_cw = refs
        return [g_in.at[:, pl.ds(k * cin, cin)], g_out.at[pl.ds(k * rout, rout), :],
                g_up.at[:, pl.ds(k * cfg.FSHP, cfg.FSHP)],
                g_dn.at[pl.ds((k // 2) * cfg.FSHP + (k % 2) * rdn, rdn), :],
                g_cw.at[:, pl.ds(k * cfg.FSHP, cfg.FSHP)]]

    def body(*refs):
        srcs, z_ref, outs = refs[:n_arr], refs[n_arr], refs[n_arr + 1:2 * n_arr + 1]
        send_sems, recv_sems, local_sems = refs[2 * n_arr + 1:]
        x, y, c = _me()
        mine = slots(outs, 2 * x + y)
        local = [pltpu.make_async_copy(s, d, local_sems.at[i]) for i, (s, d) in enumerate(zip(srcs, mine))]
        local += [pltpu.make_async_copy(z_ref, outs[3].at[pl.ds(h * cfg.FSHP + 2 * rdn, cfg.FSHP - cfg.FSH), :],
                                        local_sems.at[n_arr + h]) for h in range(2)]
        for cp in local:
            cp.start()
        sends = []
        for j, (px, py) in enumerate(_other_chips(x, y)):
            for i, (s, d) in enumerate(zip(srcs, mine)):
                sends.append(pltpu.make_async_remote_copy(
                    src_ref=s, dst_ref=d, send_sem=send_sems.at[j * n_arr + i], recv_sem=recv_sems.at[j * n_arr + i],
                    device_id=(px, py, c), device_id_type=MESH))
        for cp in sends:
            cp.start()
        for j, (px, py) in enumerate(_other_chips(x, y)):
            theirs = slots(outs, 2 * px + py)
            for i, (s, d) in enumerate(zip(srcs, theirs)):
                pltpu.make_async_remote_copy(
                    src_ref=s, dst_ref=d, send_sem=send_sems.at[j * n_arr + i], recv_sem=recv_sems.at[j * n_arr + i],
                    device_id=(px, py, c), device_id_type=MESH).wait_recv()
        for cp in sends:
            cp.wait_send()
        for cp in local:
            cp.wait()

    return pl.pallas_call(
        body, name="gather_weights",
        in_specs=[ANY] * (n_arr + 1), out_specs=[ANY] * n_arr,
        out_shape=[jax.ShapeDtypeStruct((D, N_CHIPS * cin), BF16), jax.ShapeDtypeStruct((N_CHIPS * rout, D), BF16),
                   jax.ShapeDtypeStruct((D, cfg.FF2P), BF16), jax.ShapeDtypeStruct((cfg.FFP, D), BF16),
                   jax.ShapeDtypeStruct((3, cfg.FF2P), F32)],
        scratch_shapes=[pltpu.SemaphoreType.DMA((3 * n_arr,)), pltpu.SemaphoreType.DMA((3 * n_arr,)),
                        pltpu.SemaphoreType.DMA((n_arr + 2,))],
    )(sh_in, sh_out, sh_up, sh_down, sh_cw, zpad)


def scatter_grads(cfg, dw_in, dw_out, dw_up, dw_down):
    D = cfg.D
    cin, rout, rdn = dw_in.shape[1] // N_CHIPS, dw_out.shape[0] // N_CHIPS, cfg.FSH // 2
    n_arr = 4

    def slabs(refs, k):
        g_in, g_out, g_up, g_dn = refs
        return [g_in.at[:, pl.ds(k * cin, cin)], g_out.at[pl.ds(k * rout, rout), :],
                g_up.at[:, pl.ds(k * cfg.FSHP, cfg.FSHP)],
                g_dn.at[pl.ds((k // 2) * cfg.FSHP + (k % 2) * rdn, rdn), :]]

    def body(*refs):
        srcs, outs = refs[:n_arr], refs[n_arr:2 * n_arr]
        send_sems, recv_sems, local_sems = refs[2 * n_arr:]
        x, y, c = _me()
        me = 2 * x + y
        local = [pltpu.make_async_copy(s, o.at[me], local_sems.at[i])
                 for i, (s, o) in enumerate(zip(slabs(srcs, me), outs))]
        for cp in local:
            cp.start()
        sends = []
        for j, (px, py) in enumerate(_other_chips(x, y)):
            for i, (s, o) in enumerate(zip(slabs(srcs, 2 * px + py), outs)):
                sends.append(pltpu.make_async_remote_copy(
                    src_ref=s, dst_ref=o.at[me], send_sem=send_sems.at[j * n_arr + i],
                    recv_sem=recv_sems.at[j * n_arr + i], device_id=(px, py, c), device_id_type=MESH))
        for cp in sends:
            cp.start()
        for j, (px, py) in enumerate(_other_chips(x, y)):
            for i, (s, o) in enumerate(zip(slabs(srcs, me), outs)):
                pltpu.make_async_remote_copy(
                    src_ref=s, dst_ref=o.at[2 * px + py], send_sem=send_sems.at[j * n_arr + i],
                    recv_sem=recv_sems.at[j * n_arr + i], device_id=(px, py, c), device_id_type=MESH).wait_recv()
        for cp in sends:
            cp.wait_send()
        for cp in local:
            cp.wait()

    return pl.pallas_call(
        body, name="scatter_grads",
        in_specs=[ANY] * n_arr, out_specs=[ANY] * n_arr,
        out_shape=[jax.ShapeDtypeStruct((N_CHIPS, D, cin), BF16), jax.ShapeDtypeStruct((N_CHIPS, rout, D), BF16),
                   jax.ShapeDtypeStruct((N_CHIPS, D, cfg.FSHP), BF16), jax.ShapeDtypeStruct((N_CHIPS, rdn, D), BF16)],
        scratch_shapes=[pltpu.SemaphoreType.DMA((3 * n_arr,)), pltpu.SemaphoreType.DMA((3 * n_arr,)),
                        pltpu.SemaphoreType.DMA((n_arr,))],
    )(dw_in, dw_out, dw_up, dw_down)


def sum_slots(cfg, name, parts, tile):
    n, r, c = parts.shape
    tm, tc = tile[0] or r, tile[1] or c

    def body(p_ref, o_ref):
        acc = p_ref[0].astype(F32)
        for j in range(1, n):
            acc = acc + p_ref[j].astype(F32)
        o_ref[...] = acc

    return pl.pallas_call(
        body, name=f"sum_{name}", grid=(r // tm, c // tc),
        in_specs=[pl.BlockSpec((n, tm, tc), lambda i, j: (0, i, j))], out_specs=pl.BlockSpec((tm, tc), lambda i, j: (i, j)),
        out_shape=jax.ShapeDtypeStruct((r, c), F32), compiler_params=_cp(cfg, ("parallel", "parallel")),
    )(parts)


def swap_cores(cfg, arrs):
    n = len(arrs)

    def body(*refs):
        srcs, outs, send_sems, recv_sems = refs[:n], refs[n:2 * n], refs[2 * n], refs[2 * n + 1]
        x, y, c = _me()
        cps = [pltpu.make_async_remote_copy(src_ref=s, dst_ref=o, send_sem=send_sems.at[i], recv_sem=recv_sems.at[i],
                                            device_id=(x, y, 1 - c), device_id_type=MESH)
               for i, (s, o) in enumerate(zip(srcs, outs))]
        for cp in cps:
            cp.start()
        for cp in cps:
            cp.wait()

    return pl.pallas_call(
        body, name="swap_cores", in_specs=[ANY] * n, out_specs=[ANY] * n,
        out_shape=[jax.ShapeDtypeStruct(a.shape, a.dtype) for a in arrs],
        scratch_shapes=[pltpu.SemaphoreType.DMA((n,)), pltpu.SemaphoreType.DMA((n,))],
    )(*arrs)


def allreduce_small(cfg, vec):
    R = vec.shape[0]

    def body(v_ref, o_ref, buf, send_sems, recv_sems):
        x, y, c = _me()
        me = 4 * x + 2 * y + c
        buf[me] = v_ref[...]
        sends = []
        for k in range(1, N_DEV):
            px, py, pc = x ^ (k >> 2), y ^ ((k >> 1) & 1), c ^ (k & 1)
            sends.append(pltpu.make_async_remote_copy(
                src_ref=v_ref, dst_ref=buf.at[me], send_sem=send_sems.at[k], recv_sem=recv_sems.at[k],
                device_id=(px, py, pc), device_id_type=MESH))
        for cp in sends:
            cp.start()
        for k in range(1, N_DEV):
            px, py, pc = x ^ (k >> 2), y ^ ((k >> 1) & 1), c ^ (k & 1)
            pltpu.make_async_remote_copy(
                src_ref=v_ref, dst_ref=buf.at[4 * px + 2 * py + pc], send_sem=send_sems.at[k],
                recv_sem=recv_sems.at[k], device_id=(px, py, pc), device_id_type=MESH).wait_recv()
        for cp in sends:
            cp.wait_send()
        acc = buf[0]
        for j in range(1, N_DEV):
            acc = acc + buf[j]
        o_ref[...] = acc

    return pl.pallas_call(
        body, name="allreduce_small",
        in_specs=[pl.BlockSpec(memory_space=pltpu.VMEM)], out_specs=pl.BlockSpec(memory_space=pltpu.VMEM),
        out_shape=jax.ShapeDtypeStruct((R, LANE), F32),
        scratch_shapes=[pltpu.VMEM((N_DEV, R, LANE), F32), pltpu.SemaphoreType.DMA((N_DEV,)),
                        pltpu.SemaphoreType.DMA((N_DEV,))],
    )(vec)


def adamw(cfg, name, w, m, v, g_parts, tile):
    r, c = w.shape
    tm, tc = tile[0] or r, tile[1] or c
    assert tc == c or all(g.shape[1] == c for g in g_parts)
    n = len(g_parts)
    bc1 = 1.0 - ADAM_B1 ** ADAM_STEP
    bc2 = 1.0 - ADAM_B2 ** ADAM_STEP

    def body(*refs):
        w_ref, m_ref, v_ref = refs[:3]
        g_refs = refs[3:3 + n]
        g_out, d_out, m_out, v_out = refs[3 + n:]
        g = g_refs[0][:, :tc]
        for gr in g_refs[1:]:
            g = g + gr[:, :tc]
        m_new = ADAM_B1 * m_ref[...] + (1.0 - ADAM_B1) * g
        v_new = ADAM_B2 * v_ref[...] + (1.0 - ADAM_B2) * jnp.square(g)
        m_hat = m_new / bc1
        v_hat = v_new / bc2
        g_out[...] = g
        d_out[...] = -ADAM_LR * (m_hat / (jnp.sqrt(v_hat) + ADAM_EPS) + ADAM_WD * w_ref[...])
        m_out[...] = m_new
        v_out[...] = v_new

    blk = pl.BlockSpec((tm, tc), lambda i, j: (i, j))
    return pl.pallas_call(
        body, name=f"adamw_{name}", grid=(r // tm, c // tc),
        in_specs=[blk] * 3 + [pl.BlockSpec((tm, tc if tc < c else g.shape[1]), lambda i, j: (i, j)) for g in g_parts],
        out_specs=[blk] * 4, out_shape=[jax.ShapeDtypeStruct((r, c), F32)] * 4,
        compiler_params=_cp(cfg, ("parallel", "parallel")),
    )(w, m, v, *g_parts)


SMALL_ORDER = ("loss", "g1", "g2", "g3", "g4", "g_sb", "g_dl", "conv_b", "conv_w")


def pack_small(small):
    rows = []
    for k in SMALL_ORDER:
        a = small[k].reshape(-1, LANE)
        rows.append(a)
    flat = jnp.concatenate(rows, axis=0)
    pad = (-flat.shape[0]) % 8
    return jnp.pad(flat, ((0, pad), (0, 0))), [r.shape[0] for r in rows]


def unpack_small(red, small, counts):
    out, at = {}, 0
    for k, n in zip(SMALL_ORDER, counts):
        out[k] = red[at:at + n].reshape(small[k].shape)
        at += n
    return out


def pad_ff(cfg, a):
    r = a.shape[0]
    return jnp.pad(a.reshape(r, N_CHIPS, cfg.FSH), ((0, 0), (0, 0), (0, cfg.FSHP - cfg.FSH))).reshape(r, cfg.FF2P)


def step(cfg, x, target, gains, w_sh, conv_b, m_all, v_all):
    chip = 2 * lax.axis_index("x") + lax.axis_index("y")
    sh = cast_shards(cfg, w_sh["w_in"], w_sh["w_out"], w_sh["w_up"], w_sh["w_down"], w_sh["conv_w"])
    w_in, w_out, w_up, w_down, conv_w = gather_weights(cfg, *sh)
    grad_x, small, big = local_step(cfg, x, target, gains["g1"], gains["g2"], gains["g3"], gains["g4"],
                                    gains["g_sb"], gains["g_dl"], w_in, w_out, w_up, conv_w, pad_ff(cfg, conv_b), w_down)

    packed, counts = pack_small(small)
    red = unpack_small(allreduce_small(cfg, packed), small, counts)

    parts = scatter_grads(cfg, big["w_in"], big["w_out"], big["w_up"], big["w_down"])
    names = ("w_in", "w_out", "w_up", "w_down")
    tms = dict(w_in=(cfg.TM, None), w_out=(cfg.TM, None), w_up=(cfg.TM // 2, None), w_down=(None, cfg.TN // 2))
    mine = [sum_slots(cfg, n, p, tms[n]) for n, p in zip(names, parts)]
    theirs = swap_cores(cfg, mine)

    res = {}
    for n, a, b in zip(names, mine, theirs):
        res[n] = adamw(cfg, n, w_sh[n], m_all[n], v_all[n], [a, b], tms[n])
    g_cw = lax.dynamic_slice_in_dim(red["conv_w"].reshape(3, N_CHIPS, cfg.FSHP), chip, 1, axis=1)[:, 0, :cfg.FSH]
    res["conv_w"] = adamw(cfg, "conv_w", w_sh["conv_w"], m_all["conv_w"], v_all["conv_w"], [g_cw], (None, None))
    g_cb = red["conv_b"].reshape(1, N_CHIPS, cfg.FSHP)[:, :, :cfg.FSH].reshape(1, N_CHIPS * cfg.FSH)
    res["conv_b"] = adamw(cfg, "conv_b", conv_b, m_all["conv_b"], v_all["conv_b"], [g_cb], (None, None))
    for k in ("g1", "g2", "g3", "g4", "g_sb", "g_dl"):
        res[k] = adamw(cfg, k, gains[k], m_all[k], v_all[k], [red[k]], (None, None))
    return red["loss"][0, 0], grad_x, res


PARAMS = ("pre_mix_gain", "post_mix_gain", "pre_ffn_gain", "post_ffn_gain", "w_in", "sb_out_gain", "dil_out_gain",
          "w_out", "w_up", "conv_w", "conv_b", "w_down")
SHORT = dict(pre_mix_gain="g1", post_mix_gain="g2", pre_ffn_gain="g3", post_ffn_gain="g4", sb_out_gain="g_sb",
             dil_out_gain="g_dl", w_in="w_in", w_out="w_out", w_up="w_up", conv_w="conv_w", conv_b="conv_b",
             w_down="w_down")


def kernel(x, pre_mix_gain, post_mix_gain, pre_ffn_gain, post_ffn_gain, w_in, sb_out_gain, dil_out_gain, w_out, w_up, conv_w, conv_b, w_down, loss_target, m_pre_mix_gain, m_post_mix_gain, m_pre_ffn_gain, m_post_ffn_gain, m_w_in, m_sb_out_gain, m_dil_out_gain, m_w_out, m_w_up, m_conv_w, m_conv_b, m_w_down, v_pre_mix_gain, v_post_mix_gain, v_pre_ffn_gain, v_post_ffn_gain, v_w_in, v_sb_out_gain, v_dil_out_gain, v_w_out, v_w_up, v_conv_w, v_conv_b, v_w_down):
    cfg = CFG
    w = dict(zip(PARAMS, (pre_mix_gain, post_mix_gain, pre_ffn_gain, post_ffn_gain, w_in, sb_out_gain, dil_out_gain,
                          w_out, w_up, conv_w, conv_b, w_down)))
    m = dict(zip(PARAMS, (m_pre_mix_gain, m_post_mix_gain, m_pre_ffn_gain, m_post_ffn_gain, m_w_in, m_sb_out_gain,
                          m_dil_out_gain, m_w_out, m_w_up, m_conv_w, m_conv_b, m_w_down)))
    v = dict(zip(PARAMS, (v_pre_mix_gain, v_post_mix_gain, v_pre_ffn_gain, v_post_ffn_gain, v_w_in, v_sb_out_gain,
                          v_dil_out_gain, v_w_out, v_w_up, v_conv_w, v_conv_b, v_w_down)))
    sq = lambda a: a.reshape(a.shape[1:])
    ws = {SHORT[k]: sq(a) if a.ndim == 3 else a for k, a in w.items()}
    ms = {SHORT[k]: sq(a) if a.ndim == 3 else a for k, a in m.items()}
    vs = {SHORT[k]: sq(a) if a.ndim == 3 else a for k, a in v.items()}
    gains = {k: ws[k] for k in ("g1", "g2", "g3", "g4", "g_sb", "g_dl")}
    w_sh = {k: ws[k] for k in ("w_in", "w_out", "w_up", "conv_w", "w_down")}
    loss, grad_x, res = step(cfg, sq(x), sq(loss_target), gains, w_sh, ws["conv_b"], ms, vs)
    outs = [loss, grad_x.reshape(x.shape)]
    for i in range(4):
        for k in PARAMS:
            outs.append(res[SHORT[k]][i].reshape(w[k].shape))
    return tuple(outs)
```

```python
import functools
import math
from typing import NamedTuple

import jax
import jax.numpy as jnp
from jax import lax
from jax.experimental import pallas as pl
from jax.experimental.pallas import tpu as pltpu

F32 = jnp.float32
BF16 = jnp.bfloat16
MESH = pl.DeviceIdType.MESH

ROPE_THETA = 10000.0
RMS_EPS = 1e-6
ADAM_LR = 0.001
ADAM_B1 = 0.9
ADAM_B2 = 0.999
ADAM_EPS = 1e-08
ADAM_WD = 0.01
ADAM_STEP = 10
GELU_C = math.sqrt(2.0 / math.pi)
NEG_BIG = -1e30
LANE = 128
N_CHIPS = 4
N_DEV = 8


class Cfg(NamedTuple):
    S: int = 2048
    D: int = 2048
    DH: int = 128
    HSB: int = 8
    HDL: int = 8
    QB: int = 128
    branches: tuple = ((128, 1), (512, 4), (2048, 16))
    FSH: int = 2752
    FSHP: int = 2816
    TM: int = 256
    TNF: int = 256
    TN: int = 512
    VMEM_MB: int = 56

    @property
    def DSB(self):
        return self.HSB * self.DH

    @property
    def DDL(self):
        return self.HDL * self.DH

    @property
    def DMIX(self):
        return self.DSB + self.DDL

    @property
    def FFP(self):
        return 2 * self.FSHP

    @property
    def FF2P(self):
        return 4 * self.FSHP


CFG = Cfg()


def _cp(cfg, sem=None):
    return pltpu.CompilerParams(dimension_semantics=sem, vmem_limit_bytes=cfg.VMEM_MB * 2**20)


def _dot(a, b):
    return jnp.dot(a, b, preferred_element_type=F32)


def _dot_nt(a, b):
    return lax.dot_general(a, b, (((1,), (1,)), ((), ())), preferred_element_type=F32)


def _dot_tn(a, b):
    return lax.dot_general(a, b, (((0,), (0,)), ((), ())), preferred_element_type=F32)


def _dot_split(x, u):
    hi = x.astype(BF16)
    lo = (x - hi.astype(F32)).astype(BF16)
    return _dot(hi, u) + _dot(lo, u)


def _rstd(x):
    return lax.rsqrt(jnp.mean(x * x, axis=-1, keepdims=True) + RMS_EPS)


def _rms_bwd(dy, x, g):
    r = _rstd(x)
    xh = x * r
    dxh = dy * g
    dx = r * (dxh - xh * jnp.mean(dxh * xh, axis=-1, keepdims=True))
    return dx, dy * xh


def _gelu(x):
    t = jnp.tanh(GELU_C * (x + 0.044715 * (x * x * x)))
    return 0.5 * x * (1.0 + t), t


def _gelu_grad(x, t):
    return 0.5 * (1.0 + t) + 0.5 * x * (1.0 - t * t) * (GELU_C * (1.0 + 3 * 0.044715 * (x * x)))


def _row(cfg, w):
    return pl.BlockSpec((cfg.TM, w), lambda i: (i, 0))


def _vec(w):
    return pl.BlockSpec((1, w), lambda i: (0, 0))


def rms_fwd(cfg, x, g):
    S, D = x.shape

    def body(x_ref, g_ref, h_ref):
        xv = x_ref[...]
        h_ref[...] = (xv * _rstd(xv) * g_ref[...]).astype(BF16)

    return pl.pallas_call(
        body, name="rms_fwd", grid=(S // cfg.TM,),
        in_specs=[_row(cfg, D), _vec(D)], out_specs=_row(cfg, D),
        out_shape=jax.ShapeDtypeStruct((S, D), BF16), compiler_params=_cp(cfg, ("parallel",)),
    )(x, g)


def mid_fwd(cfg, x, mo, g_post, g_pre):
    S, D = x.shape

    def body(x_ref, mo_ref, gp_ref, gn_ref, x1_ref, h2_ref):
        mo_v = mo_ref[...]
        x1 = x_ref[...] + mo_v * _rstd(mo_v) * gp_ref[...]
        x1_ref[...] = x1
        h2_ref[...] = (x1 * _rstd(x1) * gn_ref[...]).astype(BF16)

    return pl.pallas_call(
        body, name="mid_fwd", grid=(S // cfg.TM,),
        in_specs=[_row(cfg, D), _row(cfg, D), _vec(D), _vec(D)],
        out_specs=[_row(cfg, D), _row(cfg, D)],
        out_shape=[jax.ShapeDtypeStruct((S, D), F32), jax.ShapeDtypeStruct((S, D), BF16)],
        compiler_params=_cp(cfg, ("parallel",)),
    )(x, mo, g_post, g_pre)


def final_fwd_bwd(cfg, x1, f, g_post, target):
    S, D = x1.shape

    def body(x1_ref, f_ref, g_ref, t_ref, dout_ref, df_ref, dg_ref, loss_ref):
        @pl.when(pl.program_id(0) == 0)
        def _():
            dg_ref[...] = jnp.zeros_like(dg_ref)
            loss_ref[...] = jnp.zeros_like(loss_ref)

        fv = f_ref[...]
        g = g_ref[...]
        out = x1_ref[...] + fv * _rstd(fv) * g
        err = out - t_ref[...]
        loss_ref[...] += 0.5 * jnp.sum(jnp.mean(err * err, axis=-1, keepdims=True), axis=0, keepdims=True)
        dout = err * (1.0 / D)
        dout_ref[...] = dout
        df, dgx = _rms_bwd(dout, fv, g)
        df_ref[...] = df.astype(BF16)
        dg_ref[...] += jnp.sum(dgx, axis=0, keepdims=True)

    return pl.pallas_call(
        body, name="final_fwd_bwd", grid=(S // cfg.TM,),
        in_specs=[_row(cfg, D), _row(cfg, D), _vec(D), _row(cfg, D)],
        out_specs=[_row(cfg, D), _row(cfg, D), _vec(D), _vec(LANE)],
        out_shape=[jax.ShapeDtypeStruct((S, D), F32), jax.ShapeDtypeStruct((S, D), BF16),
                   jax.ShapeDtypeStruct((1, D), F32), jax.ShapeDtypeStruct((1, LANE), F32)],
        compiler_params=_cp(cfg, ("arbitrary",)),
    )(x1, f, g_post, target)


def mid_bwd(cfg, dh2, x1, g_pre, dout, mo, g_post):
    S, D = x1.shape

    def body(dh_ref, x1_ref, gn_ref, do_ref, mo_ref, gp_ref, dx1_ref, dmo_ref, dgn_ref, dgp_ref):
        @pl.when(pl.program_id(0) == 0)
        def _():
            dgn_ref[...] = jnp.zeros_like(dgn_ref)
            dgp_ref[...] = jnp.zeros_like(dgp_ref)

        dx, dgx = _rms_bwd(dh_ref[...], x1_ref[...], gn_ref[...])
        dx1 = do_ref[...] + dx
        dx1_ref[...] = dx1
        dgn_ref[...] += jnp.sum(dgx, axis=0, keepdims=True)
        dmo, dgy = _rms_bwd(dx1, mo_ref[...], gp_ref[...])
        dmo_ref[...] = dmo.astype(BF16)
        dgp_ref[...] += jnp.sum(dgy, axis=0, keepdims=True)

    return pl.pallas_call(
        body, name="mid_bwd", grid=(S // cfg.TM,),
        in_specs=[_row(cfg, D), _row(cfg, D), _vec(D), _row(cfg, D), _row(cfg, D), _vec(D)],
        out_specs=[_row(cfg, D), _row(cfg, D), _vec(D), _vec(D)],
        out_shape=[jax.ShapeDtypeStruct((S, D), F32), jax.ShapeDtypeStruct((S, D), BF16),
                   jax.ShapeDtypeStruct((1, D), F32), jax.ShapeDtypeStruct((1, D), F32)],
        compiler_params=_cp(cfg, ("arbitrary",)),
    )(dh2, x1, g_pre, dout, mo, g_post)


def first_bwd(cfg, dh1, x, g_pre, dx1):
    S, D = x.shape

    def body(dh_ref, x_ref, g_ref, r_ref, dx_ref, dg_ref):
        @pl.when(pl.program_id(0) == 0)
        def _():
            dg_ref[...] = jnp.zeros_like(dg_ref)

        dx, dgx = _rms_bwd(dh_ref[...], x_ref[...], g_ref[...])
        dx_ref[...] = r_ref[...] + dx
        dg_ref[...] += jnp.sum(dgx, axis=0, keepdims=True)

    return pl.pallas_call(
        body, name="first_bwd", grid=(S // cfg.TM,),
        in_specs=[_row(cfg, D), _row(cfg, D), _vec(D), _row(cfg, D)],
        out_specs=[_row(cfg, D), _vec(D)],
        out_shape=[jax.ShapeDtypeStruct((S, D), F32), jax.ShapeDtypeStruct((1, D), F32)],
        compiler_params=_cp(cfg, ("arbitrary",)),
    )(dh1, x, g_pre, dx1)


def _mm(cfg, name, a, b, *, nt, a_spec, b_spec, o_spec, grid, out_shape, acc_shape, dep=None):
    nk = grid[-1]
    dot = _dot_nt if nt else _dot
    deps = [] if dep is None else [dep]

    def body(a_ref, b_ref, *rest):
        o_ref, acc_ref = rest[-2:]
        k = pl.program_id(len(grid) - 1)
        part = dot(a_ref[...], b_ref[...])
        if deps:
            part = part + rest[0][0:1, 0:1]
        if nk == 1:
            o_ref[...] = part.astype(o_ref.dtype)
            return

        @pl.when(k == 0)
        def _():
            acc_ref[...] = part

        @pl.when(k > 0)
        def _():
            acc_ref[...] += part

        @pl.when(k == nk - 1)
        def _():
            o_ref[...] = acc_ref[...].astype(o_ref.dtype)

    sem = ("parallel",) * (len(grid) - 1) + ("arbitrary",)
    dep_specs = [pl.BlockSpec((8, LANE), lambda *_: (0, 0))] * len(deps)
    return pl.pallas_call(
        body, name=name, grid=grid, in_specs=[a_spec, b_spec] + dep_specs, out_specs=o_spec, out_shape=out_shape,
        scratch_shapes=[pltpu.VMEM(acc_shape, F32)], compiler_params=_cp(cfg, sem),
    )(a, b, *deps)


def _mm_tn(cfg, name, a, b, *, a_spec, b_spec, o_spec, grid, out_shape):
    def body(a_ref, b_ref, o_ref):
        o_ref[...] = _dot_tn(a_ref[...], b_ref[...]).astype(o_ref.dtype)

    return pl.pallas_call(
        body, name=name, grid=grid, in_specs=[a_spec, b_spec], out_specs=o_spec, out_shape=out_shape,
        compiler_params=_cp(cfg, ("parallel",) * len(grid)),
    )(a, b)


def qkv_proj(cfg, h1, w_in, cos2, sin2):
    S, D = h1.shape
    tn = 2 * cfg.DH
    per = cfg.DSB // tn
    assert cfg.DSB == cfg.DDL
    nblk = 6 * per

    def body(a_ref, b_ref, c_ref, s_ref, o_ref):
        j = pl.program_id(0)
        acc = _dot(a_ref[...], b_ref[...])
        rope = jnp.logical_and(j >= 3 * per, j < 5 * per)

        @pl.when(rope)
        def _():
            for c in range(tn // cfg.DH):
                xh = acc[:, c * cfg.DH:(c + 1) * cfg.DH]
                o_ref[:, c * cfg.DH:(c + 1) * cfg.DH] = (
                    xh * c_ref[...] + pltpu.roll(xh, cfg.DH // 2, 1) * s_ref[...]).astype(BF16)

        @pl.when(jnp.logical_not(rope))
        def _():
            o_ref[...] = acc.astype(BF16)

    return pl.pallas_call(
        body, name="qkv_proj", grid=(nblk,),
        in_specs=[pl.BlockSpec((S, D), lambda j: (0, 0)), pl.BlockSpec((D, tn), lambda j: (0, j)),
                  pl.BlockSpec((S, cfg.DH), lambda j: (0, 0)), pl.BlockSpec((S, cfg.DH), lambda j: (0, 0))],
        out_specs=pl.BlockSpec((None, S, tn), lambda j: (j // per, 0, j % per)),
        out_shape=jax.ShapeDtypeStruct((6, S, cfg.DSB), BF16),
        compiler_params=_cp(cfg, ("parallel",)),
    )(h1, w_in, cos2, sin2)


def _sb_tile(cfg, q, k, kb, qb):
    QB = cfg.QB
    z = _dot_nt(q, k) * (cfg.DH ** -0.5)
    t1 = jnp.log1p(jnp.exp(-jnp.abs(z)))
    lb = jnp.minimum(z, 0.0) - t1
    row = lax.broadcasted_iota(jnp.int32, (QB, QB), 0)
    col = lax.broadcasted_iota(jnp.int32, (QB, QB), 1)
    valid = jnp.logical_or(kb < qb, col < row)
    lk = jnp.where(valid, jnp.minimum(-z, 0.0) - t1, 0.0)
    return lb, lk, valid


def sb_fwd(cfg, qkv3):
    S, QB, DH = cfg.S, cfg.QB, cfg.DH

    def body(q_ref, k_ref, v_ref, o_ref, t_ref):
        row = lax.broadcasted_iota(jnp.int32, (QB, QB), 0)
        col = lax.broadcasted_iota(jnp.int32, (QB, QB), 1)
        u_after = (row > col).astype(BF16)

        def q_loop(qb, _):
            rows = pl.ds(pl.multiple_of(qb * QB, QB), QB)
            q = q_ref[rows, :]

            def k_loop(i, carry):
                o_acc, c = carry
                kb = qb - i
                krows = pl.ds(pl.multiple_of(kb * QB, QB), QB)
                lb, lk, valid = _sb_tile(cfg, q, k_ref[krows, :], kb, qb)
                rem = _dot_split(lk, u_after) + c
                a = jnp.where(valid, jnp.exp(lb + rem), 0.0)
                o_acc = o_acc + _dot(a.astype(BF16), v_ref[krows, :])
                return o_acc, c + jnp.sum(lk, axis=1, keepdims=True)

            o_acc, c = lax.fori_loop(0, qb + 1, k_loop, (jnp.zeros((QB, DH), F32), jnp.zeros((QB, 1), F32)))
            o_ref[rows, :] = o_acc
            t_ref[rows, :] = jnp.broadcast_to(c, (QB, DH))
            return 0

        lax.fori_loop(0, S // QB, q_loop, 0)

    def spec(i):
        return pl.BlockSpec((None, S, DH), lambda h: (i, 0, h))

    return pl.pallas_call(
        body, name="sb_fwd", grid=(cfg.HSB,),
        in_specs=[spec(0), spec(1), spec(2)],
        out_specs=[pl.BlockSpec((S, DH), lambda h: (0, h))] * 2,
        out_shape=[jax.ShapeDtypeStruct((S, cfg.DSB), F32)] * 2,
        compiler_params=_cp(cfg, ("parallel",)),
    )(qkv3, qkv3, qkv3)


def sb_bwd(cfg, qkv3, do_sb, tsum):
    S, QB, DH = cfg.S, cfg.QB, cfg.DH
    scale = DH ** -0.5

    def body(q_ref, k_ref, v_ref, do_ref, t_ref, d_ref, dk_acc, dv_acc):
        dk_acc[...] = jnp.zeros_like(dk_acc)
        dv_acc[...] = jnp.zeros_like(dv_acc)
        row = lax.broadcasted_iota(jnp.int32, (QB, QB), 0)
        col = lax.broadcasted_iota(jnp.int32, (QB, QB), 1)
        u_upto = (row <= col).astype(BF16)
        u_before = (row < col).astype(BF16)

        def q_loop(qb, _):
            rows = pl.ds(pl.multiple_of(qb * QB, QB), QB)
            q = q_ref[rows, :]
            do = do_ref[rows, :]
            total = t_ref[rows, 0:1]

            def k_loop(kb, carry):
                dq_acc, pc, gc = carry
                krows = pl.ds(pl.multiple_of(kb * QB, QB), QB)
                k = k_ref[krows, :]
                v = v_ref[krows, :]
                lb, lk, valid = _sb_tile(cfg, q, k, kb, qb)
                rem = total - pc - _dot_split(lk, u_upto)
                a = jnp.where(valid, jnp.exp(lb + rem), 0.0)
                g = a * _dot_nt(do, v)
                dv_acc[krows, :] += _dot_tn(a.astype(BF16), do)
                cum = gc + _dot_split(g, u_before)
                sig = jnp.exp(lb)
                dz = (jnp.where(valid, g * (1.0 - sig) - cum * sig, 0.0) * scale).astype(BF16)
                dq_acc = dq_acc + _dot(dz, k)
                dk_acc[krows, :] += _dot_tn(dz, q)
                return dq_acc, pc + jnp.sum(lk, axis=1, keepdims=True), gc + jnp.sum(g, axis=1, keepdims=True)

            z1 = jnp.zeros((QB, 1), F32)
            dq_acc, _, _ = lax.fori_loop(0, qb + 1, k_loop, (jnp.zeros((QB, DH), F32), z1, z1))
            d_ref[0, rows, :] = dq_acc.astype(BF16)
            return 0

        lax.fori_loop(0, S // QB, q_loop, 0)
        d_ref[1, :, :] = dk_acc[...].astype(BF16)
        d_ref[2, :, :] = dv_acc[...].astype(BF16)

    def spec(i):
        return pl.BlockSpec((None, S, DH), lambda h: (i, 0, h))

    return pl.pallas_call(
        body, name="sb_bwd", grid=(cfg.HSB,),
        in_specs=[spec(0), spec(1), spec(2), pl.BlockSpec((S, DH), lambda h: (0, h)),
                  pl.BlockSpec((S, DH), lambda h: (0, h))],
        out_specs=pl.BlockSpec((3, S, DH), lambda h: (0, 0, h)),
        out_shape=jax.ShapeDtypeStruct((3, S, cfg.DSB), BF16),
        scratch_shapes=[pltpu.VMEM((S, DH), F32), pltpu.VMEM((S, DH), F32)],
        compiler_params=_cp(cfg, ("parallel",)),
    )(qkv3, qkv3, qkv3, do_sb, tsum)


def _band_mask(cfg, n, n_back):
    QB = cfg.QB
    qi = lax.broadcasted_iota(jnp.int32, (QB, 2 * QB), 0)
    kj = lax.broadcasted_iota(jnp.int32, (QB, 2 * QB), 1)
    dist = QB + qi - kj
    return (dist >= 0) & (dist <= n_back) & jnp.logical_or(n > 0, kj >= QB)


def dil_fwd(cfg, qkv3, window, dil):
    S, QB, DH, H = cfg.S, cfg.QB, cfg.DH, cfg.HDL
    L = S // dil
    n_back = window // dil
    assert n_back <= QB and L % QB == 0
    scale = DH ** -0.5
    view = qkv3.reshape(6, L, dil * cfg.DDL)

    def body(q_ref, k_ref, v_ref, o_ref, lse_ref, kp, vp):
        kp[pl.ds(0, QB), :] = jnp.zeros((QB, DH), BF16)
        vp[pl.ds(0, QB), :] = jnp.zeros((QB, DH), BF16)
        kp[pl.ds(QB, L), :] = k_ref[...]
        vp[pl.ds(QB, L), :] = v_ref[...]

        def loop(n, _):
            rows = pl.ds(pl.multiple_of(n * QB, QB), QB)
            band = pl.ds(pl.multiple_of(n * QB, QB), 2 * QB)
            s = _dot_nt(q_ref[rows, :], kp[band, :]) * scale
            s = jnp.where(_band_mask(cfg, n, n_back), s, NEG_BIG)
            m = jnp.max(s, axis=1, keepdims=True)
            p = jnp.exp(s - m)
            den = jnp.sum(p, axis=1, keepdims=True)
            o_ref[rows, :] = _dot(p.astype(BF16), vp[band, :]) / den
            lse_ref[rows, :] = jnp.broadcast_to(m + jnp.log(den), (QB, DH))
            return 0

        lax.fori_loop(0, L // QB, loop, 0)

    def spec(i):
        return pl.BlockSpec((None, L, DH), lambda h, r: (i, 0, r * H + h))

    o_spec = pl.BlockSpec((L, DH), lambda h, r: (0, r * H + h))
    o, lse = pl.pallas_call(
        body, name=f"dil_fwd_d{dil}", grid=(H, dil),
        in_specs=[spec(3), spec(4), spec(5)], out_specs=[o_spec, o_spec],
        out_shape=[jax.ShapeDtypeStruct((L, dil * cfg.DDL), F32)] * 2,
        scratch_shapes=[pltpu.VMEM((L + QB, DH), BF16)] * 2,
        compiler_params=_cp(cfg, ("parallel", "parallel")),
    )(view, view, view)
    return o.reshape(S, cfg.DDL), lse.reshape(S, cfg.DDL)


def dil_bwd(cfg, qkv3, do_dl, delta, lse_tot, window, dil):
    S, QB, DH, H = cfg.S, cfg.QB, cfg.DH, cfg.HDL
    L = S // dil
    n_back = window // dil
    scale = DH ** -0.5
    view = qkv3.reshape(6, L, dil * cfg.DDL)
    dov, dlv, lsv = (t.reshape(L, dil * cfg.DDL) for t in (do_dl, delta, lse_tot))

    def body(q_ref, k_ref, v_ref, do_ref, dl_ref, ls_ref, dq_ref, dk_ref, dv_ref, kp, vp, dkp, dvp):
        kp[pl.ds(0, QB), :] = jnp.zeros((QB, DH), BF16)
        vp[pl.ds(0, QB), :] = jnp.zeros((QB, DH), BF16)
        kp[pl.ds(QB, L), :] = k_ref[...]
        vp[pl.ds(QB, L), :] = v_ref[...]
        dkp[...] = jnp.zeros_like(dkp)
        dvp[...] = jnp.zeros_like(dvp)

        def loop(n, _):
            rows = pl.ds(pl.multiple_of(n * QB, QB), QB)
            band = pl.ds(pl.multiple_of(n * QB, QB), 2 * QB)
            q = q_ref[rows, :]
            do = do_ref[rows, :]
            kb = kp[band, :]
            s = _dot_nt(q, kb) * scale
            s = jnp.where(_band_mask(cfg, n, n_back), s, NEG_BIG)
            p = jnp.exp(s - ls_ref[rows, 0:1])
            ds = (p * (_dot_nt(do, vp[band, :]) - dl_ref[rows, 0:1]) * scale).astype(BF16)
            dq_ref[rows, :] = _dot(ds, kb)
            dkp[band, :] += _dot_tn(ds, q)
            dvp[band, :] += _dot_tn(p.astype(BF16), do)
            return 0

        lax.fori_loop(0, L // QB, loop, 0)
        dk_ref[...] = dkp[pl.ds(QB, L), :]
        dv_ref[...] = dvp[pl.ds(QB, L), :]

    def spec(i):
        return pl.BlockSpec((None, L, DH), lambda h, r: (i, 0, r * H + h))

    o_spec = pl.BlockSpec((L, DH), lambda h, r: (0, r * H + h))
    outs = pl.pallas_call(
        body, name=f"dil_bwd_d{dil}", grid=(H, dil),
        in_specs=[spec(3), spec(4), spec(5), o_spec, o_spec, o_spec], out_specs=[o_spec] * 3,
        out_shape=[jax.ShapeDtypeStruct((L, dil * cfg.DDL), F32)] * 3,
        scratch_shapes=[pltpu.VMEM((L + QB, DH), BF16)] * 2 + [pltpu.VMEM((L + QB, DH), F32)] * 2,
        compiler_params=_cp(cfg, ("parallel", "parallel")),
    )(view, view, view, dov, dlv, lsv)
    return [t.reshape(S, cfg.DDL) for t in outs]


def combine_fwd(cfg, o_sb, obs, lses, g_sb, g_dl):
    S, DH = cfg.S, cfg.DH
    nb = len(obs)

    def head_norm(o, g):
        return o * lax.rsqrt(jnp.mean(o * o, axis=-1, keepdims=True) + RMS_EPS) * g

    def body(*refs):
        osb_ref, ob_refs, l_refs = refs[0], refs[1:1 + nb], refs[1 + nb:1 + 2 * nb]
        gsb_ref, gdl_ref, mix_ref, odl_ref, lt_ref = refs[1 + 2 * nb:]
        for h in range(cfg.HSB):
            c = slice(h * DH, (h + 1) * DH)
            mix_ref[:, c] = head_norm(osb_ref[:, c], gsb_ref[:, c]).astype(BF16)
        for h in range(cfg.HDL):
            c = slice(h * DH, (h + 1) * DH)
            ls = [r[:, c] for r in l_refs]
            m = functools.reduce(jnp.maximum, ls)
            es = [jnp.exp(l - m) for l in ls]
            tot = functools.reduce(jnp.add, es)
            o = functools.reduce(jnp.add, [(e / tot) * r[:, c] for e, r in zip(es, ob_refs)])
            odl_ref[:, c] = o
            lt_ref[:, c] = m + jnp.log(tot)
            mix_ref[:, cfg.DSB + h * DH:cfg.DSB + (h + 1) * DH] = head_norm(o, gdl_ref[:, c]).astype(BF16)

    return pl.pallas_call(
        body, name="combine_fwd", grid=(S // cfg.TM,),
        in_specs=[_row(cfg, cfg.DSB)] + [_row(cfg, cfg.DDL)] * (2 * nb) + [_vec(cfg.DSB), _vec(cfg.DDL)],
        out_specs=[_row(cfg, cfg.DMIX), _row(cfg, cfg.DDL), _row(cfg, cfg.DDL)],
        out_shape=[jax.ShapeDtypeStruct((S, cfg.DMIX), BF16), jax.ShapeDtypeStruct((S, cfg.DDL), F32),
                   jax.ShapeDtypeStruct((S, cfg.DDL), F32)],
        compiler_params=_cp(cfg, ("parallel",)),
    )(o_sb, *obs, *lses, g_sb, g_dl)


def combine_bwd(cfg, dmix, o_sb, o_dl, g_sb, g_dl):
    S, DH = cfg.S, cfg.DH

    def body(dm_ref, osb_ref, odl_ref, gsb_ref, gdl_ref, dsb_ref, ddl_ref, dl_ref, dgsb_ref, dgdl_ref):
        @pl.when(pl.program_id(0) == 0)
        def _():
            dgsb_ref[...] = jnp.zeros_like(dgsb_ref)
            dgdl_ref[...] = jnp.zeros_like(dgdl_ref)

        for h in range(cfg.HSB):
            c = slice(h * DH, (h + 1) * DH)
            dx, dgx = _rms_bwd(dm_ref[:, c], osb_ref[:, c], gsb_ref[:, c])
            dsb_ref[:, c] = dx.astype(BF16)
            dgsb_ref[:, c] += jnp.sum(dgx, axis=0, keepdims=True)
        for h in range(cfg.HDL):
            c = slice(h * DH, (h + 1) * DH)
            o = odl_ref[:, c]
            dx, dgx = _rms_bwd(dm_ref[:, cfg.DSB + h * DH:cfg.DSB + (h + 1) * DH], o, gdl_ref[:, c])
            ddl_ref[:, c] = dx.astype(BF16)
            dl_ref[:, c] = jnp.broadcast_to(jnp.sum(dx * o, axis=-1, keepdims=True), dx.shape)
            dgdl_ref[:, c] += jnp.sum(dgx, axis=0, keepdims=True)

    return pl.pallas_call(
        body, name="combine_bwd", grid=(S // cfg.TM,),
        in_specs=[_row(cfg, cfg.DMIX), _row(cfg, cfg.DSB), _row(cfg, cfg.DDL), _vec(cfg.DSB), _vec(cfg.DDL)],
        out_specs=[_row(cfg, cfg.DSB), _row(cfg, cfg.DDL), _row(cfg, cfg.DDL), _vec(cfg.DSB), _vec(cfg.DDL)],
        out_shape=[jax.ShapeDtypeStruct((S, cfg.DSB), BF16), jax.ShapeDtypeStruct((S, cfg.DDL), BF16),
                   jax.ShapeDtypeStruct((S, cfg.DDL), F32), jax.ShapeDtypeStruct((1, cfg.DSB), F32),
                   jax.ShapeDtypeStruct((1, cfg.DDL), F32)],
        compiler_params=_cp(cfg, ("arbitrary",)),
    )(dmix, o_sb, o_dl, g_sb, g_dl)


def rope_bwd_sum(cfg, parts, cos2, sin2):
    S, DH = cfg.S, cfg.DH
    nb = len(parts)

    def body(*refs):
        c_ref, s_ref, o_ref = refs[3 * nb], refs[3 * nb + 1], refs[3 * nb + 2]
        for i in range(3):
            for h in range(cfg.HDL):
                c = slice(h * DH, (h + 1) * DH)
                d = functools.reduce(jnp.add, [refs[b * 3 + i][:, c] for b in range(nb)])
                if i < 2:
                    d = d * c_ref[...] + pltpu.roll(d * s_ref[...], DH // 2, 1)
                o_ref[i, :, c] = d.astype(BF16)

    flat = [t for p in parts for t in p]
    return pl.pallas_call(
        body, name="rope_bwd_sum", grid=(S // cfg.TM,),
        in_specs=[_row(cfg, cfg.DDL)] * (3 * nb) + [_row(cfg, DH), _row(cfg, DH)],
        out_specs=pl.BlockSpec((3, cfg.TM, cfg.DDL), lambda i: (0, i, 0)),
        out_shape=jax.ShapeDtypeStruct((3, S, cfg.DDL), BF16),
        compiler_params=_cp(cfg, ("parallel",)),
    )(*flat, cos2, sin2)


def _shift_rows(u, j):
    row = lax.broadcasted_iota(jnp.int32, u.shape, 0)
    return jnp.where(row >= j, pltpu.roll(u, j, 0), 0.0)


def _shift_rows_up(u, j):
    n = u.shape[0]
    row = lax.broadcasted_iota(jnp.int32, u.shape, 0)
    return jnp.where(row < n - j, pltpu.roll(u, n - j, 0), 0.0)


def _conv(u, cw, cb):
    return u * cw[2:3, :] + _shift_rows(u, 1) * cw[1:2, :] + _shift_rows(u, 2) * cw[0:1, :] + cb


def ffn_fwd(cfg, h2, w_up, conv_w, conv_b):
    S, D = h2.shape
    tn, nt = cfg.TNF, cfg.FFP // cfg.TNF

    def body(h_ref, wg_ref, wv_ref, cwg_ref, cwv_ref, cbg_ref, cbv_ref, u_ref, y_ref):
        h = h_ref[...]
        ug = _dot(h, wg_ref[...])
        uv = _dot(h, wv_ref[...])
        u_ref[0] = ug
        u_ref[1] = uv
        gl, _ = _gelu(_conv(ug, cwg_ref[...], cbg_ref[...]))
        y_ref[...] = (gl * _conv(uv, cwv_ref[...], cbv_ref[...])).astype(BF16)

    return pl.pallas_call(
        body, name="ffn_fwd", grid=(nt,),
        in_specs=[pl.BlockSpec((S, D), lambda n: (0, 0)),
                  pl.BlockSpec((D, tn), lambda n: (0, n)), pl.BlockSpec((D, tn), lambda n: (0, n + nt)),
                  pl.BlockSpec((3, tn), lambda n: (0, n)), pl.BlockSpec((3, tn), lambda n: (0, n + nt)),
                  pl.BlockSpec((1, tn), lambda n: (0, n)), pl.BlockSpec((1, tn), lambda n: (0, n + nt))],
        out_specs=[pl.BlockSpec((2, S, tn), lambda n: (0, 0, n)), pl.BlockSpec((S, tn), lambda n: (0, n))],
        out_shape=[jax.ShapeDtypeStruct((2, S, cfg.FFP), F32), jax.ShapeDtypeStruct((S, cfg.FFP), BF16)],
        compiler_params=_cp(cfg, ("parallel",)),
    )(h2, w_up, w_up, conv_w, conv_w, conv_b, conv_b)


def ffn_bwd(cfg, df, w_down, u, conv_w, conv_b):
    S, D = df.shape
    tn, nt = cfg.TNF, cfg.FFP // cfg.TNF

    def conv_bwd(dc, uu, cw):
        du = dc * cw[2:3, :] + _shift_rows_up(dc, 1) * cw[1:2, :] + _shift_rows_up(dc, 2) * cw[0:1, :]
        dws = [jnp.sum(dc * _shift_rows(uu, 2), axis=0, keepdims=True),
               jnp.sum(dc * _shift_rows(uu, 1), axis=0, keepdims=True),
               jnp.sum(dc * uu, axis=0, keepdims=True)]
        return du, dws, jnp.sum(dc, axis=0, keepdims=True)

    def body(df_ref, wd_ref, u_ref, cwg_ref, cwv_ref, cbg_ref, cbv_ref, du_ref, dwd_ref, dcw_ref, dcb_ref):
        dfv = df_ref[...]
        dy = _dot_nt(dfv, wd_ref[...])
        ug, uv = u_ref[0], u_ref[1]
        cwg, cwv = cwg_ref[...], cwv_ref[...]
        cg = _conv(ug, cwg, cbg_ref[...])
        cv = _conv(uv, cwv, cbv_ref[...])
        gl, t = _gelu(cg)
        dwd_ref[...] = _dot_tn((gl * cv).astype(BF16), dfv).astype(BF16)
        dug, dwg, dbg = conv_bwd(dy * cv * _gelu_grad(cg, t), ug, cwg)
        duv, dwv, dbv = conv_bwd(dy * gl, uv, cwv)
        du_ref[0] = dug.astype(BF16)
        du_ref[1] = duv.astype(BF16)
        for j in range(3):
            dcw_ref[0, j:j + 1, :] = dwg[j]
            dcw_ref[1, j:j + 1, :] = dwv[j]
        dcb_ref[0] = dbg
        dcb_ref[1] = dbv

    return pl.pallas_call(
        body, name="ffn_bwd", grid=(nt,),
        in_specs=[pl.BlockSpec((S, D), lambda n: (0, 0)), pl.BlockSpec((tn, D), lambda n: (n, 0)),
                  pl.BlockSpec((2, S, tn), lambda n: (0, 0, n)),
                  pl.BlockSpec((3, tn), lambda n: (0, n)), pl.BlockSpec((3, tn), lambda n: (0, n + nt)),
                  pl.BlockSpec((1, tn), lambda n: (0, n)), pl.BlockSpec((1, tn), lambda n: (0, n + nt))],
        out_specs=[pl.BlockSpec((2, S, tn), lambda n: (0, 0, n)), pl.BlockSpec((tn, D), lambda n: (n, 0)),
                   pl.BlockSpec((2, 3, tn), lambda n: (0, 0, n)), pl.BlockSpec((2, 1, tn), lambda n: (0, 0, n))],
        out_shape=[jax.ShapeDtypeStruct((2, S, cfg.FFP), BF16), jax.ShapeDtypeStruct((cfg.FFP, D), BF16),
                   jax.ShapeDtypeStruct((2, 3, cfg.FFP), F32), jax.ShapeDtypeStruct((2, 1, cfg.FFP), F32)],
        compiler_params=_cp(cfg, ("parallel",)),
    )(df, w_down, u, conv_w, conv_w, conv_b, conv_b)


def rope_tables(cfg):
    inv_freq = ROPE_THETA ** (-jnp.arange(0, cfg.DH, 2, dtype=F32) / cfg.DH)
    ang = jnp.arange(cfg.S, dtype=F32)[:, None] * inv_freq[None, :]
    cos, sin = jnp.cos(ang), jnp.sin(ang)
    return jnp.concatenate([cos, cos], axis=1), jnp.concatenate([-sin, sin], axis=1)


class LocalWeights:
    def __init__(self, w_in, w_out, w_up, conv_w, w_down):
        self.w = (w_in, w_out, w_up, conv_w, w_down)
        self.grads = {}

    def weights_first(self):
        return self.w[0], self.w[3]

    def start_rest(self):
        return None

    def weights_rest(self, after):
        return self.w[1], self.w[2], self.w[4]

    def reduce_start(self, grads):
        self.grads.update(grads)
        return None

    def reduce_wait(self, names, after):
        pass


def _after(a, token):
    return a if token is None else a + token[0, 0].astype(a.dtype)


def local_step(cfg, comm, x, target, g1, g2, g3, g4, g_sb, g_dl, conv_b):
    S, D = cfg.S, cfg.D
    cos2, sin2 = rope_tables(cfg)
    full = lambda r, c: pl.BlockSpec((r, c), lambda j, k: (0, 0))

    w_in, conv_w = comm.weights_first()
    h1 = rms_fwd(cfg, x, g1)
    qkv3 = qkv_proj(cfg, h1, w_in, _after(cos2, comm.start_rest()), sin2)
    o_sb, tsum = sb_fwd(cfg, qkv3)
    obs, lses = zip(*[dil_fwd(cfg, qkv3, w, d) for w, d in cfg.branches])
    mixed, o_dl, lse_tot = combine_fwd(cfg, o_sb, obs, lses, g_sb, g_dl)
    w_out, w_up, w_down = comm.weights_rest(after=mixed)
    tn = cfg.TN
    mo = _mm(cfg, "mix_out", mixed, w_out, nt=False, grid=(D // tn, 1),
             a_spec=full(S, cfg.DMIX), b_spec=pl.BlockSpec((cfg.DMIX, tn), lambda j, k: (0, j)),
             o_spec=pl.BlockSpec((S, tn), lambda j, k: (0, j)),
             out_shape=jax.ShapeDtypeStruct((S, D), F32), acc_shape=(8, LANE))
    x1, h2 = mid_fwd(cfg, x, mo, g2, g3)
    u, y = ffn_fwd(cfg, h2, w_up, conv_w, conv_b)
    tk = cfg.FFP // 4
    f = _mm(cfg, "ffn_down", y, w_down, nt=False, grid=(D // tn, 4),
            a_spec=pl.BlockSpec((S, tk), lambda j, k: (0, k)), b_spec=pl.BlockSpec((tk, tn), lambda j, k: (k, j)),
            o_spec=pl.BlockSpec((S, tn), lambda j, k: (0, j)),
            out_shape=jax.ShapeDtypeStruct((S, D), F32), acc_shape=(S, tn))
    dout, df, dg4, loss = final_fwd_bwd(cfg, x1, f, g4, target)

    du, dw_down, dconv_w, dconv_b = ffn_bwd(cfg, df, w_down, u, conv_w, conv_b)
    kt = cfg.FFP // tk
    dh2 = _mm(cfg, "d_h2", du, w_up, nt=True, grid=(D // tn, 2 * kt),
              a_spec=pl.BlockSpec((None, S, tk), lambda j, k: (k // kt, 0, k % kt)),
              b_spec=pl.BlockSpec((tn, tk), lambda j, k: (j, k)),
              o_spec=pl.BlockSpec((S, tn), lambda j, k: (0, j)),
              out_shape=jax.ShapeDtypeStruct((S, D), F32), acc_shape=(S, tn))
    nf = cfg.FFP // tn if cfg.FFP % tn == 0 else None
    tnu = tn if nf else cfg.TNF
    nf = cfg.FFP // tnu
    dw_up = _mm_tn(cfg, "d_w_up", h2, du, grid=(2 * nf,),
                   a_spec=pl.BlockSpec((S, D), lambda j: (0, 0)),
                   b_spec=pl.BlockSpec((None, S, tnu), lambda j: (j // nf, 0, j % nf)),
                   o_spec=pl.BlockSpec((D, tnu), lambda j: (0, j)),
                   out_shape=jax.ShapeDtypeStruct((D, cfg.FF2P), BF16))
    dx1, dmo, dg3, dg2 = mid_bwd(cfg, dh2, x1, g3, dout, mo, g2)

    dmix = _mm(cfg, "d_mixed", dmo, w_out, nt=True, grid=(cfg.DMIX // tn, 1),
               a_spec=full(S, D), b_spec=pl.BlockSpec((tn, D), lambda j, k: (j, 0)),
               o_spec=pl.BlockSpec((S, tn), lambda j, k: (0, j)),
               out_shape=jax.ShapeDtypeStruct((S, cfg.DMIX), F32), acc_shape=(8, LANE))
    dw_out = _mm_tn(cfg, "d_w_out", mixed, dmo, grid=(D // tn,),
                    a_spec=pl.BlockSpec((S, cfg.DMIX), lambda j: (0, 0)),
                    b_spec=pl.BlockSpec((S, tn), lambda j: (0, j)),
                    o_spec=pl.BlockSpec((cfg.DMIX, tn), lambda j: (0, j)),
                    out_shape=jax.ShapeDtypeStruct((cfg.DMIX, D), BF16))
    token = comm.reduce_start(dict(w_out=dw_out, w_up=dw_up, w_down=dw_down))
    do_sb, do_dl, delta, dg_sb, dg_dl = combine_bwd(cfg, dmix, o_sb, o_dl, _after(g_sb, token), g_dl)
    d_sb3 = sb_bwd(cfg, qkv3, do_sb, tsum)
    parts = [dil_bwd(cfg, qkv3, do_dl, delta, lse_tot, w, d) for w, d in cfg.branches]
    d_dl3 = rope_bwd_sum(cfg, parts, cos2, sin2)
    comm.reduce_wait(("w_out", "w_up", "w_down"), after=d_dl3)
    dqkv3 = jnp.concatenate([d_sb3, d_dl3], axis=0)
    tkq = min(tn, cfg.DSB)
    kq = cfg.DSB // tkq
    dw_in = _mm_tn(cfg, "d_w_in", h1, dqkv3, grid=(6 * kq,),
                   a_spec=pl.BlockSpec((S, D), lambda j: (0, 0)),
                   b_spec=pl.BlockSpec((None, S, tkq), lambda j: (j // kq, 0, j % kq)),
                   o_spec=pl.BlockSpec((D, tkq), lambda j: (0, j)),
                   out_shape=jax.ShapeDtypeStruct((D, 6 * cfg.DSB), BF16))
    token = comm.reduce_start(dict(w_in=dw_in))
    dh1 = _mm(cfg, "d_h1", dqkv3, w_in, nt=True, grid=(D // tn, 6 * kq),
              a_spec=pl.BlockSpec((None, S, tkq), lambda j, k: (k // kq, 0, k % kq)),
              b_spec=pl.BlockSpec((tn, tkq), lambda j, k: (j, k)),
              o_spec=pl.BlockSpec((S, tn), lambda j, k: (0, j)),
              out_shape=jax.ShapeDtypeStruct((S, D), F32), acc_shape=(S, tn), dep=token)
    grad_x, dg1 = first_bwd(cfg, dh1, x, g1, dx1)
    comm.reduce_wait(("w_in",), after=grad_x)
    small = dict(loss=loss, g1=dg1, g2=dg2, g3=dg3, g4=dg4, g_sb=dg_sb, g_dl=dg_dl,
                 conv_b=dconv_b.reshape(1, cfg.FF2P), conv_w=dconv_w.transpose(1, 0, 2).reshape(3, cfg.FF2P))
    return grad_x, small


ANY = pl.BlockSpec(memory_space=pl.ANY)


def _me():
    return lax.axis_index("x"), lax.axis_index("y"), lax.axis_index("c")


def _other_chips(x, y):
    return [(1 - x, y), (x, 1 - y), (1 - x, 1 - y)]


def cast_shards(cfg, w_in, w_out, w_up, w_down, conv_w):
    D = cfg.D

    def cast(name, w, tm, pad_to=None):
        r, c = w.shape
        cp = pad_to or c
        dt = BF16 if name != "conv_w" else F32

        def body(w_ref, o_ref):
            o_ref[:, :c] = w_ref[...].astype(dt)
            if cp > c:
                o_ref[:, c:] = jnp.zeros((tm, cp - c), dt)

        return pl.pallas_call(
            body, name=f"cast_{name}", grid=(r // tm,),
            in_specs=[pl.BlockSpec((tm, c), lambda i: (i, 0))], out_specs=pl.BlockSpec((tm, cp), lambda i: (i, 0)),
            out_shape=jax.ShapeDtypeStruct((r, cp), dt), compiler_params=_cp(cfg, ("parallel",)),
        )(w)

    def cast_cols(name, w, tc):
        r, c = w.shape

        def body(w_ref, o_ref):
            o_ref[...] = w_ref[...].astype(BF16)

        blk = pl.BlockSpec((r, tc), lambda i: (0, i))
        return pl.pallas_call(
            body, name=f"cast_{name}", grid=(c // tc,), in_specs=[blk], out_specs=blk,
            out_shape=jax.ShapeDtypeStruct((r, c), BF16), compiler_params=_cp(cfg, ("parallel",)),
        )(w)

    return (cast("w_in", w_in, cfg.TM), cast("w_out", w_out, cfg.TM), cast("w_up", w_up, cfg.TM, cfg.FSHP),
            cast_cols("w_down", w_down, cfg.TN), cast("conv_w", conv_w, 3, cfg.FSHP))


HBM = pl.BlockSpec(memory_space=pltpu.HBM)
SEM = pl.BlockSpec(memory_space=pltpu.SEMAPHORE)
TOKEN = pl.BlockSpec(memory_space=pltpu.VMEM)
EFFECT = pltpu.SideEffectType.DATAFLOW_SIDE_EFFECTING


def _slab(cfg, name, k):
    D = cfg.D
    if name == "w_in":
        cin = 6 * cfg.DSB // N_CHIPS
        return 0, D, k * cin, cin
    if name == "w_out":
        rout = cfg.DMIX // N_CHIPS
        return k * rout, rout, 0, D
    if name == "w_up":
        return 0, D, k * cfg.FSHP, cfg.FSHP
    rdn = cfg.FSH // 2
    return (k // 2) * cfg.FSHP + (k % 2) * rdn, rdn, 0, D


def _full_shape(cfg, name):
    return dict(w_in=(cfg.D, 6 * cfg.DSB), w_out=(cfg.DMIX, cfg.D), w_up=(cfg.D, cfg.FF2P), w_down=(cfg.FFP, cfg.D))[name]


def _whole(cfg, name, ref, k):
    r0, nr, c0, nc = _slab(cfg, name, k)
    return ref.at[pl.ds(r0, nr), pl.ds(c0, nc)]


def _half(cfg, name, ref, k, h):
    r0, nr, c0, nc = _slab(cfg, name, k)
    return ref.at[pl.ds(r0 + h * (nr // 2), nr // 2), pl.ds(c0, nc)]


def _rows_half(ref, h):
    nr = ref.shape[0] // 2
    return ref.at[pl.ds(h * nr, nr), :]


def _remote(src, dst, send_sem, recv_sem, dev):
    return pltpu.make_async_remote_copy(src_ref=src, dst_ref=dst, send_sem=send_sem, recv_sem=recv_sem,
                                        device_id=dev, device_id_type=MESH)


def gather_first(cfg, sh_in, sh_cw):
    def body(in_ref, cw_ref, g_in, g_cw, token, ssem, rsem, fssem, frsem, lsem):
        x, y, c = _me()
        me, sib = 2 * x + y, (x, y, 1 - c)
        cw_slot = lambda k: g_cw.at[:, pl.ds(k * cfg.FSHP, cfg.FSHP)]
        local = [pltpu.make_async_copy(in_ref, _whole(cfg, "w_in", g_in, me), lsem.at[0]),
                 pltpu.make_async_copy(cw_ref, cw_slot(me), lsem.at[1])]
        sends = []
        for j, (px, py) in enumerate(_other_chips(x, y)):
            sends.append(_remote(_rows_half(in_ref, c), _half(cfg, "w_in", g_in, me, c), ssem.at[j], rsem.at[j], (px, py, c)))
            sends.append(_remote(cw_ref, cw_slot(me), ssem.at[3 + j], rsem.at[3 + j], (px, py, c)))
        for cp in local + sends:
            cp.start()
        for j, (px, py) in enumerate(_other_chips(x, y)):
            k = 2 * px + py
            landed = _half(cfg, "w_in", g_in, k, c)
            _remote(landed, landed, ssem.at[j], rsem.at[j], (px, py, c)).wait_recv()
            fwd = _remote(landed, landed, fssem.at[j], frsem.at[j], sib)
            fwd.start()
            sends.append(fwd)
        for j, (px, py) in enumerate(_other_chips(x, y)):
            k = 2 * px + py
            passed = _half(cfg, "w_in", g_in, k, 1 - c)
            _remote(passed, passed, fssem.at[j], frsem.at[j], sib).wait_recv()
            _remote(cw_ref, cw_slot(k), ssem.at[3 + j], rsem.at[3 + j], (px, py, c)).wait_recv()
        for cp in sends:
            cp.wait_send()
        for cp in local:
            cp.wait()
        token[...] = jnp.zeros_like(token)

    return pl.pallas_call(
        body, name="gather_first", in_specs=[ANY, ANY], out_specs=[ANY, ANY, TOKEN],
        out_shape=[jax.ShapeDtypeStruct(_full_shape(cfg, "w_in"), BF16), jax.ShapeDtypeStruct((3, cfg.FF2P), F32),
                   jax.ShapeDtypeStruct((8, LANE), F32)],
        scratch_shapes=[pltpu.SemaphoreType.DMA((6,)), pltpu.SemaphoreType.DMA((6,)), pltpu.SemaphoreType.DMA((3,)),
                        pltpu.SemaphoreType.DMA((3,)), pltpu.SemaphoreType.DMA((2,))],
    )(sh_in, sh_cw)


REST = ("w_out", "w_up", "w_down")


def _hbm(a):
    return pltpu.with_memory_space_constraint(a, pltpu.HBM)


def gather_start(cfg, shards, after):
    n = len(REST)

    def body(*refs):
        srcs, lands = refs[:n], refs[n:2 * n]
        ssem, rsem = refs[2 * n + 1], refs[2 * n + 2]
        token = refs[-1]
        x, y, c = _me()
        me = 2 * x + y
        for i, name in enumerate(REST):
            for j, (px, py) in enumerate(_other_chips(x, y)):
                _remote(_rows_half(srcs[i], c), _half(cfg, name, lands[i], me, c),
                        ssem.at[3 * i + j], rsem.at[3 * i + j], (px, py, c)).start()
        token[...] = jnp.zeros_like(token)

    lands = [lax.empty(_full_shape(cfg, name), BF16) for name in REST]
    ops = [_hbm(a) for a in list(shards) + lands]
    outs = pl.pallas_call(
        body, name="gather_start",
        in_specs=[HBM] * (2 * n) + [ANY],
        out_specs=[SEM, SEM] + [HBM] * (2 * n) + [TOKEN],
        out_shape=[pltpu.SemaphoreType.DMA((3 * n,)), pltpu.SemaphoreType.DMA((3 * n,))]
        + [pltpu.HBM(a.shape, a.dtype) for a in ops] + [jax.ShapeDtypeStruct((8, LANE), F32)],
        input_output_aliases={i: 2 + i for i in range(2 * n)},
        compiler_params=pltpu.CompilerParams(has_side_effects=EFFECT),
    )(*ops, after)
    return outs[0], outs[1], outs[2:2 + n], outs[2 + n:2 + 2 * n], outs[-1]


def gather_wait(cfg, ssem, rsem, shards, lands, after):
    n = len(REST)

    def body(*refs):
        srcs, lands_ = refs[:n], refs[n:2 * n]
        ssem_, rsem_ = refs[2 * n], refs[2 * n + 1]
        x, y, c = _me()
        for i, name in enumerate(REST):
            for j, (px, py) in enumerate(_other_chips(x, y)):
                cp = _remote(_rows_half(srcs[i], c), _half(cfg, name, lands_[i], 2 * px + py, c),
                             ssem_.at[3 * i + j], rsem_.at[3 * i + j], (px, py, c))
                cp.wait_send()
                cp.wait_recv()

    outs = pl.pallas_call(
        body, name="gather_wait",
        in_specs=[HBM] * (2 * n) + [SEM, SEM, ANY], out_specs=[HBM] * (2 * n),
        out_shape=[pltpu.HBM(a.shape, a.dtype) for a in list(shards) + list(lands)],
        input_output_aliases={i: i for i in range(2 * n)},
        compiler_params=pltpu.CompilerParams(has_side_effects=EFFECT),
    )(*shards, *lands, ssem, rsem, after)
    return outs[:n], outs[n:]


def gather_finish(cfg, shards, lands):
    n = len(REST)
    rdn = cfg.FSH // 2
    zpad = jnp.zeros((cfg.FSHP - cfg.FSH, cfg.D), BF16)

    def body(*refs):
        srcs, z_ref, outs = refs[:n], refs[n], refs[2 * n + 1:3 * n + 1]
        ssem, rsem, lsem = refs[3 * n + 1:]
        x, y, c = _me()
        me, sib = 2 * x + y, (x, y, 1 - c)
        local = [pltpu.make_async_copy(srcs[i], _whole(cfg, name, outs[i], me), lsem.at[i]) for i, name in enumerate(REST)]
        local += [pltpu.make_async_copy(z_ref, outs[2].at[pl.ds(h * cfg.FSHP + 2 * rdn, cfg.FSHP - cfg.FSH), :],
                                        lsem.at[n + h]) for h in range(2)]
        fwds = []
        for i, name in enumerate(REST):
            for j, (px, py) in enumerate(_other_chips(x, y)):
                landed = _half(cfg, name, outs[i], 2 * px + py, c)
                fwds.append(_remote(landed, landed, ssem.at[3 * i + j], rsem.at[3 * i + j], sib))
        for cp in local + fwds:
            cp.start()
        for i, name in enumerate(REST):
            for j, (px, py) in enumerate(_other_chips(x, y)):
                passed = _half(cfg, name, outs[i], 2 * px + py, 1 - c)
                _remote(passed, passed, ssem.at[3 * i + j], rsem.at[3 * i + j], sib).wait_recv()
        for cp in fwds:
            cp.wait_send()
        for cp in local:
            cp.wait()

    return pl.pallas_call(
        body, name="gather_finish", in_specs=[ANY] * (2 * n + 1), out_specs=[ANY] * n,
        out_shape=[jax.ShapeDtypeStruct(a.shape, a.dtype) for a in lands],
        input_output_aliases={n + 1 + i: i for i in range(n)},
        scratch_shapes=[pltpu.SemaphoreType.DMA((3 * n,)), pltpu.SemaphoreType.DMA((3 * n,)),
                        pltpu.SemaphoreType.DMA((n + 2,))],
    )(*shards, zpad, *lands)


def pair_send(cfg, grads):
    names = list(grads)
    n = len(names)

    def half_shape(name):
        _, nr, _, nc = _slab(cfg, name, 0)
        return (N_CHIPS, nr // 2, nc)

    def body(*refs):
        srcs, mine, theirs = refs[:n], refs[n:2 * n], refs[2 * n:3 * n]
        ssem, rsem, lsem = refs[3 * n:]
        x, y, c = _me()
        cps, local = [], []
        for i, name in enumerate(names):
            for k in range(N_CHIPS):
                local.append(pltpu.make_async_copy(_half(cfg, name, srcs[i], k, c), mine[i].at[k], lsem.at[N_CHIPS * i + k]))
                cps.append(_remote(_half(cfg, name, srcs[i], k, 1 - c), theirs[i].at[k],
                                   ssem.at[N_CHIPS * i + k], rsem.at[N_CHIPS * i + k], (x, y, 1 - c)))
        for cp in cps + local:
            cp.start()
        for cp in cps + local:
            cp.wait()

    shapes = [jax.ShapeDtypeStruct(half_shape(name), BF16) for name in names]
    outs = pl.pallas_call(
        body, name="pair_send_" + "_".join(names), in_specs=[ANY] * n, out_specs=[ANY] * (2 * n), out_shape=shapes + shapes,
        scratch_shapes=[pltpu.SemaphoreType.DMA((N_CHIPS * n,))] * 3,
    )(*[grads[k] for k in names])
    return dict(zip(names, outs[:n])), dict(zip(names, outs[n:]))


def _tile3(shape):
    n, r, c = shape
    tm = 256 if r % 256 == 0 else r
    tc = c if tm < r or c % 512 else 512
    return tm, tc, (r // tm, c // tc)


def pair_sum(cfg, name, a, b):
    n, r, c = a.shape
    tm, tc, grid = _tile3(a.shape)

    def body(a_ref, b_ref, o_ref):
        o_ref[...] = (a_ref[...].astype(F32) + b_ref[...].astype(F32)).astype(BF16)

    blk = pl.BlockSpec((n, tm, tc), lambda i, j: (0, i, j))
    return pl.pallas_call(
        body, name=f"pair_sum_{name}", grid=grid, in_specs=[blk, blk], out_specs=blk,
        out_shape=jax.ShapeDtypeStruct(a.shape, BF16), compiler_params=_cp(cfg, ("parallel", "parallel")),
    )(a, b)


def scatter_start(cfg, pres, after):
    names = list(pres)
    n = len(names)

    def body(*refs):
        srcs, lands = refs[:n], refs[n:2 * n]
        ssem, rsem = refs[2 * n + 1], refs[2 * n + 2]
        token = refs[-1]
        x, y, c = _me()
        for i in range(n):
            for j, (px, py) in enumerate(_other_chips(x, y)):
                _remote(srcs[i].at[2 * px + py], lands[i].at[j], ssem.at[3 * i + j], rsem.at[3 * i + j], (px, py, c)).start()
        token[...] = jnp.zeros_like(token)

    lands = [lax.empty((3,) + pres[k].shape[1:], BF16) for k in names]
    ops = [_hbm(a) for a in [pres[k] for k in names] + lands]
    outs = pl.pallas_call(
        body, name="scatter_start_" + "_".join(names),
        in_specs=[HBM] * (2 * n) + [ANY],
        out_specs=[SEM, SEM] + [HBM] * (2 * n) + [TOKEN],
        out_shape=[pltpu.SemaphoreType.DMA((3 * n,)), pltpu.SemaphoreType.DMA((3 * n,))]
        + [pltpu.HBM(a.shape, a.dtype) for a in ops] + [jax.ShapeDtypeStruct((8, LANE), F32)],
        input_output_aliases={i: 2 + i for i in range(2 * n)},
        compiler_params=pltpu.CompilerParams(has_side_effects=EFFECT),
    )(*ops, after)
    return outs[0], outs[1], dict(zip(names, outs[2:2 + n])), dict(zip(names, outs[2 + n:2 + 2 * n])), outs[-1]


def scatter_wait(cfg, ssem, rsem, pres, lands, after):
    names = list(pres)
    n = len(names)

    def body(*refs):
        srcs, lands_ = refs[:n], refs[n:2 * n]
        ssem_, rsem_ = refs[2 * n], refs[2 * n + 1]
        x, y, c = _me()
        for i in range(n):
            for j, (px, py) in enumerate(_other_chips(x, y)):
                cp = _remote(srcs[i].at[2 * px + py], lands_[i].at[j], ssem_.at[3 * i + j], rsem_.at[3 * i + j], (px, py, c))
                cp.wait_send()
                cp.wait_recv()

    ops = [pres[k] for k in names] + [lands[k] for k in names]
    outs = pl.pallas_call(
        body, name="scatter_wait_" + "_".join(names),
        in_specs=[HBM] * (2 * n) + [SEM, SEM, ANY], out_specs=[HBM] * (2 * n),
        out_shape=[pltpu.HBM(a.shape, a.dtype) for a in ops],
        input_output_aliases={i: i for i in range(2 * n)},
        compiler_params=pltpu.CompilerParams(has_side_effects=EFFECT),
    )(*ops, ssem, rsem, after)
    return dict(zip(names, outs[:n])), dict(zip(names, outs[n:]))


def sum_landed(cfg, name, pre, land, chip):
    _, r, c = pre.shape
    tm, tc, grid = _tile3(pre.shape)

    def body(chip_ref, p_ref, l_ref, o_ref):
        acc = p_ref[...].astype(F32)
        for j in range(3):
            acc = acc + l_ref[j].astype(F32)
        o_ref[...] = acc

    return pl.pallas_call(
        body, name=f"sum_landed_{name}",
        grid_spec=pltpu.PrefetchScalarGridSpec(
            num_scalar_prefetch=1, grid=grid,
            in_specs=[pl.BlockSpec((None, tm, tc), lambda i, j, chip_ref: (chip_ref[0], i, j)),
                      pl.BlockSpec((3, tm, tc), lambda i, j, chip_ref: (0, i, j))],
            out_specs=pl.BlockSpec((tm, tc), lambda i, j, chip_ref: (i, j))),
        out_shape=jax.ShapeDtypeStruct((r, c), F32), compiler_params=_cp(cfg, ("parallel", "parallel")),
    )(chip.reshape(1), pre, land)


def half_swap(cfg, halves):
    names = list(halves)
    n = len(names)

    def body(*refs):
        srcs, outs = refs[:n], refs[n:2 * n]
        ssem, rsem, lsem = refs[2 * n:]
        x, y, c = _me()
        cps = []
        for i in range(n):
            cps.append(pltpu.make_async_copy(srcs[i], _rows_half(outs[i], c), lsem.at[i]))
            cps.append(_remote(srcs[i], _rows_half(outs[i], c), ssem.at[i], rsem.at[i], (x, y, 1 - c)))
        for cp in cps:
            cp.start()
        for i in range(n):
            theirs = _rows_half(outs[i], 1 - c)
            _remote(srcs[i], theirs, ssem.at[i], rsem.at[i], (x, y, 1 - c)).wait_recv()
        for i in range(n):
            cps[2 * i].wait()
            cps[2 * i + 1].wait_send()

    outs = pl.pallas_call(
        body, name="half_swap_" + "_".join(names), in_specs=[ANY] * n, out_specs=[ANY] * n,
        out_shape=[jax.ShapeDtypeStruct((2 * halves[k].shape[0], halves[k].shape[1]), F32) for k in names],
        scratch_shapes=[pltpu.SemaphoreType.DMA((n,))] * 3,
    )(*[halves[k] for k in names])
    return dict(zip(names, outs))


class MeshWeights:
    def __init__(self, cfg, shards):
        self.cfg = cfg
        self.sh = shards
        self.inflight = {}
        self.grads = {}

    def weights_first(self):
        w_in, conv_w, self.token = gather_first(self.cfg, self.sh["w_in"], self.sh["conv_w"])
        return w_in, conv_w

    def start_rest(self):
        out = gather_start(self.cfg, [self.sh[k] for k in REST], self.token)
        self.rest = out[:4]
        return out[4]

    def weights_rest(self, after):
        shards, lands = gather_wait(self.cfg, *self.rest, after)
        return gather_finish(self.cfg, shards, lands)

    def reduce_start(self, grads):
        mine, theirs = pair_send(self.cfg, grads)
        pres = {k: pair_sum(self.cfg, k, mine[k], theirs[k]) for k in grads}
        out = scatter_start(self.cfg, pres, jnp.zeros((8, LANE), F32))
        self.inflight[tuple(grads)] = out[:4]
        return out[4]

    def reduce_wait(self, names, after):
        cfg = self.cfg
        pres, lands = scatter_wait(cfg, *self.inflight.pop(tuple(names)), after)
        chip = (2 * lax.axis_index("x") + lax.axis_index("y")).astype(jnp.int32)
        halves = {k: sum_landed(cfg, k, pres[k], lands[k], chip) for k in names}
        self.grads.update(half_swap(cfg, halves))


def allreduce_small(cfg, vec):
    R = vec.shape[0]

    def body(v_ref, o_ref, buf, send_sems, recv_sems):
        x, y, c = _me()
        me = 4 * x + 2 * y + c
        buf[me] = v_ref[...]
        sends = []
        for k in range(1, N_DEV):
            px, py, pc = x ^ (k >> 2), y ^ ((k >> 1) & 1), c ^ (k & 1)
            sends.append(pltpu.make_async_remote_copy(
                src_ref=v_ref, dst_ref=buf.at[me], send_sem=send_sems.at[k], recv_sem=recv_sems.at[k],
                device_id=(px, py, pc), device_id_type=MESH))
        for cp in sends:
            cp.start()
        for k in range(1, N_DEV):
            px, py, pc = x ^ (k >> 2), y ^ ((k >> 1) & 1), c ^ (k & 1)
            pltpu.make_async_remote_copy(
                src_ref=v_ref, dst_ref=buf.at[4 * px + 2 * py + pc], send_sem=send_sems.at[k],
                recv_sem=recv_sems.at[k], device_id=(px, py, pc), device_id_type=MESH).wait_recv()
        for cp in sends:
            cp.wait_send()
        acc = buf[0]
        for j in range(1, N_DEV):
            acc = acc + buf[j]
        o_ref[...] = acc

    return pl.pallas_call(
        body, name="allreduce_small",
        in_specs=[pl.BlockSpec(memory_space=pltpu.VMEM)], out_specs=pl.BlockSpec(memory_space=pltpu.VMEM),
        out_shape=jax.ShapeDtypeStruct((R, LANE), F32),
        scratch_shapes=[pltpu.VMEM((N_DEV, R, LANE), F32), pltpu.SemaphoreType.DMA((N_DEV,)),
                        pltpu.SemaphoreType.DMA((N_DEV,))],
    )(vec)


def adamw(cfg, name, w, m, v, g_parts, tile):
    r, c = w.shape
    tm, tc = tile[0] or r, tile[1] or c
    assert tc == c or all(g.shape[1] == c for g in g_parts)
    n = len(g_parts)
    bc1 = 1.0 - ADAM_B1 ** ADAM_STEP
    bc2 = 1.0 - ADAM_B2 ** ADAM_STEP

    def body(*refs):
        w_ref, m_ref, v_ref = refs[:3]
        g_refs = refs[3:3 + n]
        g_out, d_out, m_out, v_out = refs[3 + n:]
        g = g_refs[0][:, :tc]
        for gr in g_refs[1:]:
            g = g + gr[:, :tc]
        m_new = ADAM_B1 * m_ref[...] + (1.0 - ADAM_B1) * g
        v_new = ADAM_B2 * v_ref[...] + (1.0 - ADAM_B2) * jnp.square(g)
        m_hat = m_new / bc1
        v_hat = v_new / bc2
        g_out[...] = g
        d_out[...] = -ADAM_LR * (m_hat / (jnp.sqrt(v_hat) + ADAM_EPS) + ADAM_WD * w_ref[...])
        m_out[...] = m_new
        v_out[...] = v_new

    blk = pl.BlockSpec((tm, tc), lambda i, j: (i, j))
    return pl.pallas_call(
        body, name=f"adamw_{name}", grid=(r // tm, c // tc),
        in_specs=[blk] * 3 + [pl.BlockSpec((tm, tc if tc < c else g.shape[1]), lambda i, j: (i, j)) for g in g_parts],
        out_specs=[blk] * 4, out_shape=[jax.ShapeDtypeStruct((r, c), F32)] * 4,
        compiler_params=_cp(cfg, ("parallel", "parallel")),
    )(w, m, v, *g_parts)


SMALL_ORDER = ("loss", "g1", "g2", "g3", "g4", "g_sb", "g_dl", "conv_b", "conv_w")


def pack_small(small):
    rows = []
    for k in SMALL_ORDER:
        a = small[k].reshape(-1, LANE)
        rows.append(a)
    flat = jnp.concatenate(rows, axis=0)
    pad = (-flat.shape[0]) % 8
    return jnp.pad(flat, ((0, pad), (0, 0))), [r.shape[0] for r in rows]


def unpack_small(red, small, counts):
    out, at = {}, 0
    for k, n in zip(SMALL_ORDER, counts):
        out[k] = red[at:at + n].reshape(small[k].shape)
        at += n
    return out


def pad_ff(cfg, a):
    r = a.shape[0]
    return jnp.pad(a.reshape(r, N_CHIPS, cfg.FSH), ((0, 0), (0, 0), (0, cfg.FSHP - cfg.FSH))).reshape(r, cfg.FF2P)


def step(cfg, x, target, gains, w_sh, conv_b, m_all, v_all):
    chip = 2 * lax.axis_index("x") + lax.axis_index("y")
    sh = cast_shards(cfg, w_sh["w_in"], w_sh["w_out"], w_sh["w_up"], w_sh["w_down"], w_sh["conv_w"])
    comm = MeshWeights(cfg, dict(zip(("w_in", "w_out", "w_up", "w_down", "conv_w"), sh)))
    grad_x, small = local_step(cfg, comm, x, target, gains["g1"], gains["g2"], gains["g3"], gains["g4"],
                               gains["g_sb"], gains["g_dl"], pad_ff(cfg, conv_b))

    packed, counts = pack_small(small)
    red = unpack_small(allreduce_small(cfg, packed), small, counts)

    names = ("w_in", "w_out", "w_up", "w_down")
    tms = dict(w_in=(cfg.TM, None), w_out=(cfg.TM, None), w_up=(cfg.TM // 2, None), w_down=(None, cfg.TN // 2))
    res = {}
    for n in names:
        res[n] = adamw(cfg, n, w_sh[n], m_all[n], v_all[n], [comm.grads[n]], tms[n])
    g_cw = lax.dynamic_slice_in_dim(red["conv_w"].reshape(3, N_CHIPS, cfg.FSHP), chip, 1, axis=1)[:, 0, :cfg.FSH]
    res["conv_w"] = adamw(cfg, "conv_w", w_sh["conv_w"], m_all["conv_w"], v_all["conv_w"], [g_cw], (None, None))
    g_cb = red["conv_b"].reshape(1, N_CHIPS, cfg.FSHP)[:, :, :cfg.FSH].reshape(1, N_CHIPS * cfg.FSH)
    res["conv_b"] = adamw(cfg, "conv_b", conv_b, m_all["conv_b"], v_all["conv_b"], [g_cb], (None, None))
    for k in ("g1", "g2", "g3", "g4", "g_sb", "g_dl"):
        res[k] = adamw(cfg, k, gains[k], m_all[k], v_all[k], [red[k]], (None, None))
    return red["loss"][0, 0], grad_x, res


PARAMS = ("pre_mix_gain", "post_mix_gain", "pre_ffn_gain", "post_ffn_gain", "w_in", "sb_out_gain", "dil_out_gain",
          "w_out", "w_up", "conv_w", "conv_b", "w_down")
SHORT = dict(pre_mix_gain="g1", post_mix_gain="g2", pre_ffn_gain="g3", post_ffn_gain="g4", sb_out_gain="g_sb",
             dil_out_gain="g_dl", w_in="w_in", w_out="w_out", w_up="w_up", conv_w="conv_w", conv_b="conv_b",
             w_down="w_down")


def kernel(x, pre_mix_gain, post_mix_gain, pre_ffn_gain, post_ffn_gain, w_in, sb_out_gain, dil_out_gain, w_out, w_up, conv_w, conv_b, w_down, loss_target, m_pre_mix_gain, m_post_mix_gain, m_pre_ffn_gain, m_post_ffn_gain, m_w_in, m_sb_out_gain, m_dil_out_gain, m_w_out, m_w_up, m_conv_w, m_conv_b, m_w_down, v_pre_mix_gain, v_post_mix_gain, v_pre_ffn_gain, v_post_ffn_gain, v_w_in, v_sb_out_gain, v_dil_out_gain, v_w_out, v_w_up, v_conv_w, v_conv_b, v_w_down):
    cfg = CFG
    w = dict(zip(PARAMS, (pre_mix_gain, post_mix_gain, pre_ffn_gain, post_ffn_gain, w_in, sb_out_gain, dil_out_gain,
                          w_out, w_up, conv_w, conv_b, w_down)))
    m = dict(zip(PARAMS, (m_pre_mix_gain, m_post_mix_gain, m_pre_ffn_gain, m_post_ffn_gain, m_w_in, m_sb_out_gain,
                          m_dil_out_gain, m_w_out, m_w_up, m_conv_w, m_conv_b, m_w_down)))
    v = dict(zip(PARAMS, (v_pre_mix_gain, v_post_mix_gain, v_pre_ffn_gain, v_post_ffn_gain, v_w_in, v_sb_out_gain,
                          v_dil_out_gain, v_w_out, v_w_up, v_conv_w, v_conv_b, v_w_down)))
    sq = lambda a: a.reshape(a.shape[1:])
    ws = {SHORT[k]: sq(a) if a.ndim == 3 else a for k, a in w.items()}
    ms = {SHORT[k]: sq(a) if a.ndim == 3 else a for k, a in m.items()}
    vs = {SHORT[k]: sq(a) if a.ndim == 3 else a for k, a in v.items()}
    gains = {k: ws[k] for k in ("g1", "g2", "g3", "g4", "g_sb", "g_dl")}
    w_sh = {k: ws[k] for k in ("w_in", "w_out", "w_up", "conv_w", "w_down")}
    loss, grad_x, res = step(cfg, sq(x), sq(loss_target), gains, w_sh, ws["conv_b"], ms, vs)
    outs = [loss, grad_x.reshape(x.shape)]
    for i in range(4):
        for k in PARAMS:
            outs.append(res[SHORT[k]][i].reshape(w[k].shape))
    return tuple(outs)
```

```python
import functools
import math
from typing import NamedTuple

import jax
import jax.numpy as jnp
from jax import lax
from jax.experimental import pallas as pl
from jax.experimental.pallas import tpu as pltpu

F32 = jnp.float32
BF16 = jnp.bfloat16
MESH = pl.DeviceIdType.MESH

ROPE_THETA = 10000.0
RMS_EPS = 1e-6
ADAM_LR = 0.001
ADAM_B1 = 0.9
ADAM_B2 = 0.999
ADAM_EPS = 1e-08
ADAM_WD = 0.01
ADAM_STEP = 10
GELU_C = math.sqrt(2.0 / math.pi)
NEG_BIG = -1e30
LANE = 128
N_CHIPS = 4
N_DEV = 8


class Cfg(NamedTuple):
    S: int = 2048
    D: int = 2048
    DH: int = 128
    HSB: int = 8
    HDL: int = 8
    QB: int = 128
    branches: tuple = ((128, 1), (512, 4), (2048, 16))
    FSH: int = 2752
    FSHP: int = 2816
    TM: int = 256
    TNF: int = 256
    TN: int = 512
    VMEM_MB: int = 56

    @property
    def DSB(self):
        return self.HSB * self.DH

    @property
    def DDL(self):
        return self.HDL * self.DH

    @property
    def DMIX(self):
        return self.DSB + self.DDL

    @property
    def FFP(self):
        return 2 * self.FSHP

    @property
    def FF2P(self):
        return 4 * self.FSHP


CFG = Cfg()


def _cp(cfg, sem=None):
    return pltpu.CompilerParams(dimension_semantics=sem, vmem_limit_bytes=cfg.VMEM_MB * 2**20)


def _dot(a, b):
    return jnp.dot(a, b, preferred_element_type=F32)


def _dot_nt(a, b):
    return lax.dot_general(a, b, (((1,), (1,)), ((), ())), preferred_element_type=F32)


def _dot_tn(a, b):
    return lax.dot_general(a, b, (((0,), (0,)), ((), ())), preferred_element_type=F32)


def _dot_split(x, u):
    hi = x.astype(BF16)
    lo = (x - hi.astype(F32)).astype(BF16)
    return _dot(hi, u) + _dot(lo, u)


def _rstd(x):
    return lax.rsqrt(jnp.mean(x * x, axis=-1, keepdims=True) + RMS_EPS)


def _rms_bwd(dy, x, g):
    r = _rstd(x)
    xh = x * r
    dxh = dy * g
    dx = r * (dxh - xh * jnp.mean(dxh * xh, axis=-1, keepdims=True))
    return dx, dy * xh


def _gelu(x):
    t = jnp.tanh(GELU_C * (x + 0.044715 * (x * x * x)))
    return 0.5 * x * (1.0 + t), t


def _gelu_grad(x, t):
    return 0.5 * (1.0 + t) + 0.5 * x * (1.0 - t * t) * (GELU_C * (1.0 + 3 * 0.044715 * (x * x)))


def _row(cfg, w):
    return pl.BlockSpec((cfg.TM, w), lambda i: (i, 0))


def _vec(w):
    return pl.BlockSpec((1, w), lambda i: (0, 0))


def rms_fwd(cfg, x, g):
    S, D = x.shape

    def body(x_ref, g_ref, h_ref):
        xv = x_ref[...]
        h_ref[...] = (xv * _rstd(xv) * g_ref[...]).astype(BF16)

    return pl.pallas_call(
        body, name="rms_fwd", grid=(S // cfg.TM,),
        in_specs=[_row(cfg, D), _vec(D)], out_specs=_row(cfg, D),
        out_shape=jax.ShapeDtypeStruct((S, D), BF16), compiler_params=_cp(cfg, ("parallel",)),
    )(x, g)


def mid_fwd(cfg, x, mo, g_post, g_pre):
    S, D = x.shape

    def body(x_ref, mo_ref, gp_ref, gn_ref, x1_ref, h2_ref):
        mo_v = mo_ref[...]
        x1 = x_ref[...] + mo_v * _rstd(mo_v) * gp_ref[...]
        x1_ref[...] = x1
        h2_ref[...] = (x1 * _rstd(x1) * gn_ref[...]).astype(BF16)

    return pl.pallas_call(
        body, name="mid_fwd", grid=(S // cfg.TM,),
        in_specs=[_row(cfg, D), _row(cfg, D), _vec(D), _vec(D)],
        out_specs=[_row(cfg, D), _row(cfg, D)],
        out_shape=[jax.ShapeDtypeStruct((S, D), F32), jax.ShapeDtypeStruct((S, D), BF16)],
        compiler_params=_cp(cfg, ("parallel",)),
    )(x, mo, g_post, g_pre)


def final_fwd_bwd(cfg, x1, f, g_post, target):
    S, D = x1.shape

    def body(x1_ref, f_ref, g_ref, t_ref, dout_ref, df_ref, dg_ref, loss_ref):
        @pl.when(pl.program_id(0) == 0)
        def _():
            dg_ref[...] = jnp.zeros_like(dg_ref)
            loss_ref[...] = jnp.zeros_like(loss_ref)

        fv = f_ref[...]
        g = g_ref[...]
        out = x1_ref[...] + fv * _rstd(fv) * g
        err = out - t_ref[...]
        loss_ref[...] += 0.5 * jnp.sum(jnp.mean(err * err, axis=-1, keepdims=True), axis=0, keepdims=True)
        dout = err * (1.0 / D)
        dout_ref[...] = dout
        df, dgx = _rms_bwd(dout, fv, g)
        df_ref[...] = df.astype(BF16)
        dg_ref[...] += jnp.sum(dgx, axis=0, keepdims=True)

    return pl.pallas_call(
        body, name="final_fwd_bwd", grid=(S // cfg.TM,),
        in_specs=[_row(cfg, D), _row(cfg, D), _vec(D), _row(cfg, D)],
        out_specs=[_row(cfg, D), _row(cfg, D), _vec(D), _vec(LANE)],
        out_shape=[jax.ShapeDtypeStruct((S, D), F32), jax.ShapeDtypeStruct((S, D), BF16),
                   jax.ShapeDtypeStruct((1, D), F32), jax.ShapeDtypeStruct((1, LANE), F32)],
        compiler_params=_cp(cfg, ("arbitrary",)),
    )(x1, f, g_post, target)


def mid_bwd(cfg, dh2, x1, g_pre, dout, mo, g_post):
    S, D = x1.shape

    def body(dh_ref, x1_ref, gn_ref, do_ref, mo_ref, gp_ref, dx1_ref, dmo_ref, dgn_ref, dgp_ref):
        @pl.when(pl.program_id(0) == 0)
        def _():
            dgn_ref[...] = jnp.zeros_like(dgn_ref)
            dgp_ref[...] = jnp.zeros_like(dgp_ref)

        dx, dgx = _rms_bwd(dh_ref[...], x1_ref[...], gn_ref[...])
        dx1 = do_ref[...] + dx
        dx1_ref[...] = dx1
        dgn_ref[...] += jnp.sum(dgx, axis=0, keepdims=True)
        dmo, dgy = _rms_bwd(dx1, mo_ref[...], gp_ref[...])
        dmo_ref[...] = dmo.astype(BF16)
        dgp_ref[...] += jnp.sum(dgy, axis=0, keepdims=True)

    return pl.pallas_call(
        body, name="mid_bwd", grid=(S // cfg.TM,),
        in_specs=[_row(cfg, D), _row(cfg, D), _vec(D), _row(cfg, D), _row(cfg, D), _vec(D)],
        out_specs=[_row(cfg, D), _row(cfg, D), _vec(D), _vec(D)],
        out_shape=[jax.ShapeDtypeStruct((S, D), F32), jax.ShapeDtypeStruct((S, D), BF16),
                   jax.ShapeDtypeStruct((1, D), F32), jax.ShapeDtypeStruct((1, D), F32)],
        compiler_params=_cp(cfg, ("arbitrary",)),
    )(dh2, x1, g_pre, dout, mo, g_post)


def first_bwd(cfg, dh1, x, g_pre, dx1):
    S, D = x.shape

    def body(dh_ref, x_ref, g_ref, r_ref, dx_ref, dg_ref):
        @pl.when(pl.program_id(0) == 0)
        def _():
            dg_ref[...] = jnp.zeros_like(dg_ref)

        dx, dgx = _rms_bwd(dh_ref[...], x_ref[...], g_ref[...])
        dx_ref[...] = r_ref[...] + dx
        dg_ref[...] += jnp.sum(dgx, axis=0, keepdims=True)

    return pl.pallas_call(
        body, name="first_bwd", grid=(S // cfg.TM,),
        in_specs=[_row(cfg, D), _row(cfg, D), _vec(D), _row(cfg, D)],
        out_specs=[_row(cfg, D), _vec(D)],
        out_shape=[jax.ShapeDtypeStruct((S, D), F32), jax.ShapeDtypeStruct((1, D), F32)],
        compiler_params=_cp(cfg, ("arbitrary",)),
    )(dh1, x, g_pre, dx1)


def _mm(cfg, name, a, b, *, nt, a_spec, b_spec, o_spec, grid, out_shape, acc_shape, dep=None):
    nk = grid[-1]
    dot = _dot_nt if nt else _dot
    deps = [] if dep is None else [dep]

    def body(a_ref, b_ref, *rest):
        o_ref, acc_ref = rest[-2:]
        k = pl.program_id(len(grid) - 1)
        part = dot(a_ref[...], b_ref[...])
        if deps:
            part = part + rest[0][0:1, 0:1]
        if nk == 1:
            o_ref[...] = part.astype(o_ref.dtype)
            return

        @pl.when(k == 0)
        def _():
            acc_ref[...] = part

        @pl.when(k > 0)
        def _():
            acc_ref[...] += part

        @pl.when(k == nk - 1)
        def _():
            o_ref[...] = acc_ref[...].astype(o_ref.dtype)

    sem = ("parallel",) * (len(grid) - 1) + ("arbitrary",)
    dep_specs = [pl.BlockSpec((8, LANE), lambda *_: (0, 0))] * len(deps)
    return pl.pallas_call(
        body, name=name, grid=grid, in_specs=[a_spec, b_spec] + dep_specs, out_specs=o_spec, out_shape=out_shape,
        scratch_shapes=[pltpu.VMEM(acc_shape, F32)], compiler_params=_cp(cfg, sem),
    )(a, b, *deps)


def _mm_tn(cfg, name, a, b, *, a_spec, b_spec, o_spec, grid, out_shape):
    def body(a_ref, b_ref, o_ref):
        o_ref[...] = _dot_tn(a_ref[...], b_ref[...]).astype(o_ref.dtype)

    return pl.pallas_call(
        body, name=name, grid=grid, in_specs=[a_spec, b_spec], out_specs=o_spec, out_shape=out_shape,
        compiler_params=_cp(cfg, ("parallel",) * len(grid)),
    )(a, b)


def qkv_proj(cfg, h1, w_in, cos2, sin2):
    S, D = h1.shape
    tn = 2 * cfg.DH
    per = cfg.DSB // tn
    assert cfg.DSB == cfg.DDL
    nblk = 6 * per

    def body(a_ref, b_ref, c_ref, s_ref, o_ref):
        j = pl.program_id(0)
        acc = _dot(a_ref[...], b_ref[...])
        rope = jnp.logical_and(j >= 3 * per, j < 5 * per)

        @pl.when(rope)
        def _():
            for c in range(tn // cfg.DH):
                xh = acc[:, c * cfg.DH:(c + 1) * cfg.DH]
                o_ref[:, c * cfg.DH:(c + 1) * cfg.DH] = (
                    xh * c_ref[...] + pltpu.roll(xh, cfg.DH // 2, 1) * s_ref[...]).astype(BF16)

        @pl.when(jnp.logical_not(rope))
        def _():
            o_ref[...] = acc.astype(BF16)

    return pl.pallas_call(
        body, name="qkv_proj", grid=(nblk,),
        in_specs=[pl.BlockSpec((S, D), lambda j: (0, 0)), pl.BlockSpec((D, tn), lambda j: (0, j)),
                  pl.BlockSpec((S, cfg.DH), lambda j: (0, 0)), pl.BlockSpec((S, cfg.DH), lambda j: (0, 0))],
        out_specs=pl.BlockSpec((None, S, tn), lambda j: (j // per, 0, j % per)),
        out_shape=jax.ShapeDtypeStruct((6, S, cfg.DSB), BF16),
        compiler_params=_cp(cfg, ("parallel",)),
    )(h1, w_in, cos2, sin2)


def _sb_tile(cfg, q, k, kb, qb):
    QB = cfg.QB
    z = _dot_nt(q, k) * (cfg.DH ** -0.5)
    t1 = jnp.log1p(jnp.exp(-jnp.abs(z)))
    lb = jnp.minimum(z, 0.0) - t1
    row = lax.broadcasted_iota(jnp.int32, (QB, QB), 0)
    col = lax.broadcasted_iota(jnp.int32, (QB, QB), 1)
    valid = jnp.logical_or(kb < qb, col < row)
    lk = jnp.where(valid, jnp.minimum(-z, 0.0) - t1, 0.0)
    return lb, lk, valid


def sb_fwd(cfg, qkv3):
    S, QB, DH = cfg.S, cfg.QB, cfg.DH

    def body(q_ref, k_ref, v_ref, o_ref, t_ref):
        row = lax.broadcasted_iota(jnp.int32, (QB, QB), 0)
        col = lax.broadcasted_iota(jnp.int32, (QB, QB), 1)
        u_after = (row > col).astype(BF16)

        def q_loop(qb, _):
            rows = pl.ds(pl.multiple_of(qb * QB, QB), QB)
            q = q_ref[rows, :]

            def k_loop(i, carry):
                o_acc, c = carry
                kb = qb - i
                krows = pl.ds(pl.multiple_of(kb * QB, QB), QB)
                lb, lk, valid = _sb_tile(cfg, q, k_ref[krows, :], kb, qb)
                rem = _dot_split(lk, u_after) + c
                a = jnp.where(valid, jnp.exp(lb + rem), 0.0)
                o_acc = o_acc + _dot(a.astype(BF16), v_ref[krows, :])
                return o_acc, c + jnp.sum(lk, axis=1, keepdims=True)

            o_acc, c = lax.fori_loop(0, qb + 1, k_loop, (jnp.zeros((QB, DH), F32), jnp.zeros((QB, 1), F32)))
            o_ref[rows, :] = o_acc
            t_ref[rows, :] = jnp.broadcast_to(c, (QB, DH))
            return 0

        lax.fori_loop(0, S // QB, q_loop, 0)

    def spec(i):
        return pl.BlockSpec((None, S, DH), lambda h: (i, 0, h))

    return pl.pallas_call(
        body, name="sb_fwd", grid=(cfg.HSB,),
        in_specs=[spec(0), spec(1), spec(2)],
        out_specs=[pl.BlockSpec((S, DH), lambda h: (0, h))] * 2,
        out_shape=[jax.ShapeDtypeStruct((S, cfg.DSB), F32)] * 2,
        compiler_params=_cp(cfg, ("parallel",)),
    )(qkv3, qkv3, qkv3)


def sb_bwd(cfg, qkv3, do_sb, tsum):
    S, QB, DH = cfg.S, cfg.QB, cfg.DH
    scale = DH ** -0.5

    def body(q_ref, k_ref, v_ref, do_ref, t_ref, d_ref, dk_acc, dv_acc):
        dk_acc[...] = jnp.zeros_like(dk_acc)
        dv_acc[...] = jnp.zeros_like(dv_acc)
        row = lax.broadcasted_iota(jnp.int32, (QB, QB), 0)
        col = lax.broadcasted_iota(jnp.int32, (QB, QB), 1)
        u_upto = (row <= col).astype(BF16)
        u_before = (row < col).astype(BF16)

        def q_loop(qb, _):
            rows = pl.ds(pl.multiple_of(qb * QB, QB), QB)
            q = q_ref[rows, :]
            do = do_ref[rows, :]
            total = t_ref[rows, 0:1]

            def k_loop(kb, carry):
                dq_acc, pc, gc = carry
                krows = pl.ds(pl.multiple_of(kb * QB, QB), QB)
                k = k_ref[krows, :]
                v = v_ref[krows, :]
                lb, lk, valid = _sb_tile(cfg, q, k, kb, qb)
                rem = total - pc - _dot_split(lk, u_upto)
                a = jnp.where(valid, jnp.exp(lb + rem), 0.0)
                g = a * _dot_nt(do, v)
                dv_acc[krows, :] += _dot_tn(a.astype(BF16), do)
                cum = gc + _dot_split(g, u_before)
                sig = jnp.exp(lb)
                dz = (jnp.where(valid, g * (1.0 - sig) - cum * sig, 0.0) * scale).astype(BF16)
                dq_acc = dq_acc + _dot(dz, k)
                dk_acc[krows, :] += _dot_tn(dz, q)
                return dq_acc, pc + jnp.sum(lk, axis=1, keepdims=True), gc + jnp.sum(g, axis=1, keepdims=True)

            z1 = jnp.zeros((QB, 1), F32)
            dq_acc, _, _ = lax.fori_loop(0, qb + 1, k_loop, (jnp.zeros((QB, DH), F32), z1, z1))
            d_ref[0, rows, :] = dq_acc.astype(BF16)
            return 0

        lax.fori_loop(0, S // QB, q_loop, 0)
        d_ref[1, :, :] = dk_acc[...].astype(BF16)
        d_ref[2, :, :] = dv_acc[...].astype(BF16)

    def spec(i):
        return pl.BlockSpec((None, S, DH), lambda h: (i, 0, h))

    return pl.pallas_call(
        body, name="sb_bwd", grid=(cfg.HSB,),
        in_specs=[spec(0), spec(1), spec(2), pl.BlockSpec((S, DH), lambda h: (0, h)),
                  pl.BlockSpec((S, DH), lambda h: (0, h))],
        out_specs=pl.BlockSpec((3, S, DH), lambda h: (0, 0, h)),
        out_shape=jax.ShapeDtypeStruct((3, S, cfg.DSB), BF16),
        scratch_shapes=[pltpu.VMEM((S, DH), F32), pltpu.VMEM((S, DH), F32)],
        compiler_params=_cp(cfg, ("parallel",)),
    )(qkv3, qkv3, qkv3, do_sb, tsum)


def _band_mask(cfg, n, n_back):
    QB = cfg.QB
    qi = lax.broadcasted_iota(jnp.int32, (QB, 2 * QB), 0)
    kj = lax.broadcasted_iota(jnp.int32, (QB, 2 * QB), 1)
    dist = QB + qi - kj
    return (dist >= 0) & (dist <= n_back) & jnp.logical_or(n > 0, kj >= QB)


def dil_fwd(cfg, qkv3, window, dil):
    S, QB, DH, H = cfg.S, cfg.QB, cfg.DH, cfg.HDL
    L = S // dil
    n_back = window // dil
    assert n_back <= QB and L % QB == 0
    scale = DH ** -0.5
    view = qkv3.reshape(6, L, dil * cfg.DDL)

    def body(q_ref, k_ref, v_ref, o_ref, lse_ref, kp, vp):
        kp[pl.ds(0, QB), :] = jnp.zeros((QB, DH), BF16)
        vp[pl.ds(0, QB), :] = jnp.zeros((QB, DH), BF16)
        kp[pl.ds(QB, L), :] = k_ref[...]
        vp[pl.ds(QB, L), :] = v_ref[...]

        def loop(n, _):
            rows = pl.ds(pl.multiple_of(n * QB, QB), QB)
            band = pl.ds(pl.multiple_of(n * QB, QB), 2 * QB)
            s = _dot_nt(q_ref[rows, :], kp[band, :]) * scale
            s = jnp.where(_band_mask(cfg, n, n_back), s, NEG_BIG)
            m = jnp.max(s, axis=1, keepdims=True)
            p = jnp.exp(s - m)
            den = jnp.sum(p, axis=1, keepdims=True)
            o_ref[rows, :] = _dot(p.astype(BF16), vp[band, :]) / den
            lse_ref[rows, :] = jnp.broadcast_to(m + jnp.log(den), (QB, DH))
            return 0

        lax.fori_loop(0, L // QB, loop, 0)

    def spec(i):
        return pl.BlockSpec((None, L, DH), lambda h, r: (i, 0, r * H + h))

    o_spec = pl.BlockSpec((L, DH), lambda h, r: (0, r * H + h))
    o, lse = pl.pallas_call(
        body, name=f"dil_fwd_d{dil}", grid=(H, dil),
        in_specs=[spec(3), spec(4), spec(5)], out_specs=[o_spec, o_spec],
        out_shape=[jax.ShapeDtypeStruct((L, dil * cfg.DDL), F32)] * 2,
        scratch_shapes=[pltpu.VMEM((L + QB, DH), BF16)] * 2,
        compiler_params=_cp(cfg, ("parallel", "parallel")),
    )(view, view, view)
    return o.reshape(S, cfg.DDL), lse.reshape(S, cfg.DDL)


def dil_bwd(cfg, qkv3, do_dl, delta, lse_tot, window, dil):
    S, QB, DH, H = cfg.S, cfg.QB, cfg.DH, cfg.HDL
    L = S // dil
    n_back = window // dil
    scale = DH ** -0.5
    view = qkv3.reshape(6, L, dil * cfg.DDL)
    dov, dlv, lsv = (t.reshape(L, dil * cfg.DDL) for t in (do_dl, delta, lse_tot))

    def body(q_ref, k_ref, v_ref, do_ref, dl_ref, ls_ref, dq_ref, dk_ref, dv_ref, kp, vp, dkp, dvp):
        kp[pl.ds(0, QB), :] = jnp.zeros((QB, DH), BF16)
        vp[pl.ds(0, QB), :] = jnp.zeros((QB, DH), BF16)
        kp[pl.ds(QB, L), :] = k_ref[...]
        vp[pl.ds(QB, L), :] = v_ref[...]
        dkp[...] = jnp.zeros_like(dkp)
        dvp[...] = jnp.zeros_like(dvp)

        def loop(n, _):
            rows = pl.ds(pl.multiple_of(n * QB, QB), QB)
            band = pl.ds(pl.multiple_of(n * QB, QB), 2 * QB)
            q = q_ref[rows, :]
            do = do_ref[rows, :]
            kb = kp[band, :]
            s = _dot_nt(q, kb) * scale
            s = jnp.where(_band_mask(cfg, n, n_back), s, NEG_BIG)
            p = jnp.exp(s - ls_ref[rows, 0:1])
            ds = (p * (_dot_nt(do, vp[band, :]) - dl_ref[rows, 0:1]) * scale).astype(BF16)
            dq_ref[rows, :] = _dot(ds, kb)
            dkp[band, :] += _dot_tn(ds, q)
            dvp[band, :] += _dot_tn(p.astype(BF16), do)
            return 0

        lax.fori_loop(0, L // QB, loop, 0)
        dk_ref[...] = dkp[pl.ds(QB, L), :]
        dv_ref[...] = dvp[pl.ds(QB, L), :]

    def spec(i):
        return pl.BlockSpec((None, L, DH), lambda h, r: (i, 0, r * H + h))

    o_spec = pl.BlockSpec((L, DH), lambda h, r: (0, r * H + h))
    outs = pl.pallas_call(
        body, name=f"dil_bwd_d{dil}", grid=(H, dil),
        in_specs=[spec(3), spec(4), spec(5), o_spec, o_spec, o_spec], out_specs=[o_spec] * 3,
        out_shape=[jax.ShapeDtypeStruct((L, dil * cfg.DDL), F32)] * 3,
        scratch_shapes=[pltpu.VMEM((L + QB, DH), BF16)] * 2 + [pltpu.VMEM((L + QB, DH), F32)] * 2,
        compiler_params=_cp(cfg, ("parallel", "parallel")),
    )(view, view, view, dov, dlv, lsv)
    return [t.reshape(S, cfg.DDL) for t in outs]


def combine_fwd(cfg, o_sb, obs, lses, g_sb, g_dl):
    S, DH = cfg.S, cfg.DH
    nb = len(obs)

    def head_norm(o, g):
        return o * lax.rsqrt(jnp.mean(o * o, axis=-1, keepdims=True) + RMS_EPS) * g

    def body(*refs):
        osb_ref, ob_refs, l_refs = refs[0], refs[1:1 + nb], refs[1 + nb:1 + 2 * nb]
        gsb_ref, gdl_ref, mix_ref, odl_ref, lt_ref = refs[1 + 2 * nb:]
        for h in range(cfg.HSB):
            c = slice(h * DH, (h + 1) * DH)
            mix_ref[:, c] = head_norm(osb_ref[:, c], gsb_ref[:, c]).astype(BF16)
        for h in range(cfg.HDL):
            c = slice(h * DH, (h + 1) * DH)
            ls = [r[:, c] for r in l_refs]
            m = functools.reduce(jnp.maximum, ls)
            es = [jnp.exp(l - m) for l in ls]
            tot = functools.reduce(jnp.add, es)
            o = functools.reduce(jnp.add, [(e / tot) * r[:, c] for e, r in zip(es, ob_refs)])
            odl_ref[:, c] = o
            lt_ref[:, c] = m + jnp.log(tot)
            mix_ref[:, cfg.DSB + h * DH:cfg.DSB + (h + 1) * DH] = head_norm(o, gdl_ref[:, c]).astype(BF16)

    return pl.pallas_call(
        body, name="combine_fwd", grid=(S // cfg.TM,),
        in_specs=[_row(cfg, cfg.DSB)] + [_row(cfg, cfg.DDL)] * (2 * nb) + [_vec(cfg.DSB), _vec(cfg.DDL)],
        out_specs=[_row(cfg, cfg.DMIX), _row(cfg, cfg.DDL), _row(cfg, cfg.DDL)],
        out_shape=[jax.ShapeDtypeStruct((S, cfg.DMIX), BF16), jax.ShapeDtypeStruct((S, cfg.DDL), F32),
                   jax.ShapeDtypeStruct((S, cfg.DDL), F32)],
        compiler_params=_cp(cfg, ("parallel",)),
    )(o_sb, *obs, *lses, g_sb, g_dl)


def combine_bwd(cfg, dmix, o_sb, o_dl, g_sb, g_dl):
    S, DH = cfg.S, cfg.DH

    def body(dm_ref, osb_ref, odl_ref, gsb_ref, gdl_ref, dsb_ref, ddl_ref, dl_ref, dgsb_ref, dgdl_ref):
        @pl.when(pl.program_id(0) == 0)
        def _():
            dgsb_ref[...] = jnp.zeros_like(dgsb_ref)
            dgdl_ref[...] = jnp.zeros_like(dgdl_ref)

        for h in range(cfg.HSB):
            c = slice(h * DH, (h + 1) * DH)
            dx, dgx = _rms_bwd(dm_ref[:, c], osb_ref[:, c], gsb_ref[:, c])
            dsb_ref[:, c] = dx.astype(BF16)
            dgsb_ref[:, c] += jnp.sum(dgx, axis=0, keepdims=True)
        for h in range(cfg.HDL):
            c = slice(h * DH, (h + 1) * DH)
            o = odl_ref[:, c]
            dx, dgx = _rms_bwd(dm_ref[:, cfg.DSB + h * DH:cfg.DSB + (h + 1) * DH], o, gdl_ref[:, c])
            ddl_ref[:, c] = dx.astype(BF16)
            dl_ref[:, c] = jnp.broadcast_to(jnp.sum(dx * o, axis=-1, keepdims=True), dx.shape)
            dgdl_ref[:, c] += jnp.sum(dgx, axis=0, keepdims=True)

    return pl.pallas_call(
        body, name="combine_bwd", grid=(S // cfg.TM,),
        in_specs=[_row(cfg, cfg.DMIX), _row(cfg, cfg.DSB), _row(cfg, cfg.DDL), _vec(cfg.DSB), _vec(cfg.DDL)],
        out_specs=[_row(cfg, cfg.DSB), _row(cfg, cfg.DDL), _row(cfg, cfg.DDL), _vec(cfg.DSB), _vec(cfg.DDL)],
        out_shape=[jax.ShapeDtypeStruct((S, cfg.DSB), BF16), jax.ShapeDtypeStruct((S, cfg.DDL), BF16),
                   jax.ShapeDtypeStruct((S, cfg.DDL), F32), jax.ShapeDtypeStruct((1, cfg.DSB), F32),
                   jax.ShapeDtypeStruct((1, cfg.DDL), F32)],
        compiler_params=_cp(cfg, ("arbitrary",)),
    )(dmix, o_sb, o_dl, g_sb, g_dl)


def rope_bwd_sum(cfg, parts, cos2, sin2):
    S, DH = cfg.S, cfg.DH
    nb = len(parts)

    def body(*refs):
        c_ref, s_ref, o_ref = refs[3 * nb], refs[3 * nb + 1], refs[3 * nb + 2]
        for i in range(3):
            for h in range(cfg.HDL):
                c = slice(h * DH, (h + 1) * DH)
                d = functools.reduce(jnp.add, [refs[b * 3 + i][:, c] for b in range(nb)])
                if i < 2:
                    d = d * c_ref[...] + pltpu.roll(d * s_ref[...], DH // 2, 1)
                o_ref[i, :, c] = d.astype(BF16)

    flat = [t for p in parts for t in p]
    return pl.pallas_call(
        body, name="rope_bwd_sum", grid=(S // cfg.TM,),
        in_specs=[_row(cfg, cfg.DDL)] * (3 * nb) + [_row(cfg, DH), _row(cfg, DH)],
        out_specs=pl.BlockSpec((3, cfg.TM, cfg.DDL), lambda i: (0, i, 0)),
        out_shape=jax.ShapeDtypeStruct((3, S, cfg.DDL), BF16),
        compiler_params=_cp(cfg, ("parallel",)),
    )(*flat, cos2, sin2)


def _shift_rows(u, j):
    row = lax.broadcasted_iota(jnp.int32, u.shape, 0)
    return jnp.where(row >= j, pltpu.roll(u, j, 0), 0.0)


def _shift_rows_up(u, j):
    n = u.shape[0]
    row = lax.broadcasted_iota(jnp.int32, u.shape, 0)
    return jnp.where(row < n - j, pltpu.roll(u, n - j, 0), 0.0)


def _conv(u, cw, cb):
    return u * cw[2:3, :] + _shift_rows(u, 1) * cw[1:2, :] + _shift_rows(u, 2) * cw[0:1, :] + cb


def ffn_fwd(cfg, h2, w_up, conv_w, conv_b):
    S, D = h2.shape
    tn, nt = cfg.TNF, cfg.FFP // cfg.TNF

    def body(h_ref, wg_ref, wv_ref, cwg_ref, cwv_ref, cbg_ref, cbv_ref, u_ref, y_ref):
        h = h_ref[...]
        ug = _dot(h, wg_ref[...])
        uv = _dot(h, wv_ref[...])
        u_ref[0] = ug
        u_ref[1] = uv
        gl, _ = _gelu(_conv(ug, cwg_ref[...], cbg_ref[...]))
        y_ref[...] = (gl * _conv(uv, cwv_ref[...], cbv_ref[...])).astype(BF16)

    return pl.pallas_call(
        body, name="ffn_fwd", grid=(nt,),
        in_specs=[pl.BlockSpec((S, D), lambda n: (0, 0)),
                  pl.BlockSpec((D, tn), lambda n: (0, n)), pl.BlockSpec((D, tn), lambda n: (0, n + nt)),
                  pl.BlockSpec((3, tn), lambda n: (0, n)), pl.BlockSpec((3, tn), lambda n: (0, n + nt)),
                  pl.BlockSpec((1, tn), lambda n: (0, n)), pl.BlockSpec((1, tn), lambda n: (0, n + nt))],
        out_specs=[pl.BlockSpec((2, S, tn), lambda n: (0, 0, n)), pl.BlockSpec((S, tn), lambda n: (0, n))],
        out_shape=[jax.ShapeDtypeStruct((2, S, cfg.FFP), F32), jax.ShapeDtypeStruct((S, cfg.FFP), BF16)],
        compiler_params=_cp(cfg, ("parallel",)),
    )(h2, w_up, w_up, conv_w, conv_w, conv_b, conv_b)


def ffn_bwd(cfg, df, w_down, u, conv_w, conv_b):
    S, D = df.shape
    tn, nt = cfg.TNF, cfg.FFP // cfg.TNF

    def conv_bwd(dc, uu, cw):
        du = dc * cw[2:3, :] + _shift_rows_up(dc, 1) * cw[1:2, :] + _shift_rows_up(dc, 2) * cw[0:1, :]
        dws = [jnp.sum(dc * _shift_rows(uu, 2), axis=0, keepdims=True),
               jnp.sum(dc * _shift_rows(uu, 1), axis=0, keepdims=True),
               jnp.sum(dc * uu, axis=0, keepdims=True)]
        return du, dws, jnp.sum(dc, axis=0, keepdims=True)

    def body(df_ref, wd_ref, u_ref, cwg_ref, cwv_ref, cbg_ref, cbv_ref, du_ref, dwd_ref, dcw_ref, dcb_ref):
        dfv = df_ref[...]
        dy = _dot_nt(dfv, wd_ref[...])
        ug, uv = u_ref[0], u_ref[1]
        cwg, cwv = cwg_ref[...], cwv_ref[...]
        cg = _conv(ug, cwg, cbg_ref[...])
        cv = _conv(uv, cwv, cbv_ref[...])
        gl, t = _gelu(cg)
        dwd_ref[...] = _dot_tn((gl * cv).astype(BF16), dfv).astype(BF16)
        dug, dwg, dbg = conv_bwd(dy * cv * _gelu_grad(cg, t), ug, cwg)
        duv, dwv, dbv = conv_bwd(dy * gl, uv, cwv)
        du_ref[0] = dug.astype(BF16)
        du_ref[1] = duv.astype(BF16)
        for j in range(3):
            dcw_ref[0, j:j + 1, :] = dwg[j]
            dcw_ref[1, j:j + 1, :] = dwv[j]
        dcb_ref[0] = dbg
        dcb_ref[1] = dbv

    return pl.pallas_call(
        body, name="ffn_bwd", grid=(nt,),
        in_specs=[pl.BlockSpec((S, D), lambda n: (0, 0)), pl.BlockSpec((tn, D), lambda n: (n, 0)),
                  pl.BlockSpec((2, S, tn), lambda n: (0, 0, n)),
                  pl.BlockSpec((3, tn), lambda n: (0, n)), pl.BlockSpec((3, tn), lambda n: (0, n + nt)),
                  pl.BlockSpec((1, tn), lambda n: (0, n)), pl.BlockSpec((1, tn), lambda n: (0, n + nt))],
        out_specs=[pl.BlockSpec((2, S, tn), lambda n: (0, 0, n)), pl.BlockSpec((tn, D), lambda n: (n, 0)),
                   pl.BlockSpec((2, 3, tn), lambda n: (0, 0, n)), pl.BlockSpec((2, 1, tn), lambda n: (0, 0, n))],
        out_shape=[jax.ShapeDtypeStruct((2, S, cfg.FFP), BF16), jax.ShapeDtypeStruct((cfg.FFP, D), BF16),
                   jax.ShapeDtypeStruct((2, 3, cfg.FFP), F32), jax.ShapeDtypeStruct((2, 1, cfg.FFP), F32)],
        compiler_params=_cp(cfg, ("parallel",)),
    )(df, w_down, u, conv_w, conv_w, conv_b, conv_b)


def rope_tables(cfg):
    inv_freq = ROPE_THETA ** (-jnp.arange(0, cfg.DH, 2, dtype=F32) / cfg.DH)
    ang = jnp.arange(cfg.S, dtype=F32)[:, None] * inv_freq[None, :]
    cos, sin = jnp.cos(ang), jnp.sin(ang)
    return jnp.concatenate([cos, cos], axis=1), jnp.concatenate([-sin, sin], axis=1)


class LocalWeights:
    def __init__(self, w_in, w_out, w_up, conv_w, w_down):
        self.w = (w_in, w_out, w_up, conv_w, w_down)
        self.grads = {}

    def weights_first(self):
        return self.w[0], self.w[3]

    def start_rest(self):
        return None

    def weights_rest(self, after):
        return self.w[1], self.w[2], self.w[4]

    def reduce_start(self, grads):
        self.grads.update(grads)
        return None

    def reduce_wait(self, names, after):
        pass


def _after(a, token):
    return a if token is None else a + token[0, 0].astype(a.dtype)


def local_step(cfg, comm, x, target, g1, g2, g3, g4, g_sb, g_dl, conv_b):
    S, D = cfg.S, cfg.D
    cos2, sin2 = rope_tables(cfg)
    full = lambda r, c: pl.BlockSpec((r, c), lambda j, k: (0, 0))

    w_in, conv_w = comm.weights_first()
    h1 = rms_fwd(cfg, x, g1)
    qkv3 = qkv_proj(cfg, h1, w_in, _after(cos2, comm.start_rest()), sin2)
    o_sb, tsum = sb_fwd(cfg, qkv3)
    obs, lses = zip(*[dil_fwd(cfg, qkv3, w, d) for w, d in cfg.branches])
    mixed, o_dl, lse_tot = combine_fwd(cfg, o_sb, obs, lses, g_sb, g_dl)
    w_out, w_up, w_down = comm.weights_rest(after=mixed)
    tn = cfg.TN
    mo = _mm(cfg, "mix_out", mixed, w_out, nt=False, grid=(D // tn, 1),
             a_spec=full(S, cfg.DMIX), b_spec=pl.BlockSpec((cfg.DMIX, tn), lambda j, k: (0, j)),
             o_spec=pl.BlockSpec((S, tn), lambda j, k: (0, j)),
             out_shape=jax.ShapeDtypeStruct((S, D), F32), acc_shape=(8, LANE))
    x1, h2 = mid_fwd(cfg, x, mo, g2, g3)
    u, y = ffn_fwd(cfg, h2, w_up, conv_w, conv_b)
    tk = cfg.FFP // 4
    f = _mm(cfg, "ffn_down", y, w_down, nt=False, grid=(D // tn, 4),
            a_spec=pl.BlockSpec((S, tk), lambda j, k: (0, k)), b_spec=pl.BlockSpec((tk, tn), lambda j, k: (k, j)),
            o_spec=pl.BlockSpec((S, tn), lambda j, k: (0, j)),
            out_shape=jax.ShapeDtypeStruct((S, D), F32), acc_shape=(S, tn))
    dout, df, dg4, loss = final_fwd_bwd(cfg, x1, f, g4, target)

    du, dw_down, dconv_w, dconv_b = ffn_bwd(cfg, df, w_down, u, conv_w, conv_b)
    kt = cfg.FFP // tk
    dh2 = _mm(cfg, "d_h2", du, w_up, nt=True, grid=(D // tn, 2 * kt),
              a_spec=pl.BlockSpec((None, S, tk), lambda j, k: (k // kt, 0, k % kt)),
              b_spec=pl.BlockSpec((tn, tk), lambda j, k: (j, k)),
              o_spec=pl.BlockSpec((S, tn), lambda j, k: (0, j)),
              out_shape=jax.ShapeDtypeStruct((S, D), F32), acc_shape=(S, tn))
    nf = cfg.FFP // tn if cfg.FFP % tn == 0 else None
    tnu = tn if nf else cfg.TNF
    nf = cfg.FFP // tnu
    dw_up = _mm_tn(cfg, "d_w_up", h2, du, grid=(2 * nf,),
                   a_spec=pl.BlockSpec((S, D), lambda j: (0, 0)),
                   b_spec=pl.BlockSpec((None, S, tnu), lambda j: (j // nf, 0, j % nf)),
                   o_spec=pl.BlockSpec((D, tnu), lambda j: (0, j)),
                   out_shape=jax.ShapeDtypeStruct((D, cfg.FF2P), BF16))
    dx1, dmo, dg3, dg2 = mid_bwd(cfg, dh2, x1, g3, dout, mo, g2)

    dmix = _mm(cfg, "d_mixed", dmo, w_out, nt=True, grid=(cfg.DMIX // tn, 1),
               a_spec=full(S, D), b_spec=pl.BlockSpec((tn, D), lambda j, k: (j, 0)),
               o_spec=pl.BlockSpec((S, tn), lambda j, k: (0, j)),
               out_shape=jax.ShapeDtypeStruct((S, cfg.DMIX), F32), acc_shape=(8, LANE))
    dw_out = _mm_tn(cfg, "d_w_out", mixed, dmo, grid=(D // tn,),
                    a_spec=pl.BlockSpec((S, cfg.DMIX), lambda j: (0, 0)),
                    b_spec=pl.BlockSpec((S, tn), lambda j: (0, j)),
                    o_spec=pl.BlockSpec((cfg.DMIX, tn), lambda j: (0, j)),
                    out_shape=jax.ShapeDtypeStruct((cfg.DMIX, D), BF16))
    token = comm.reduce_start(dict(w_out=dw_out, w_up=dw_up, w_down=dw_down))
    do_sb, do_dl, delta, dg_sb, dg_dl = combine_bwd(cfg, dmix, o_sb, o_dl, _after(g_sb, token), g_dl)
    d_sb3 = sb_bwd(cfg, qkv3, do_sb, tsum)
    parts = [dil_bwd(cfg, qkv3, do_dl, delta, lse_tot, w, d) for w, d in cfg.branches]
    d_dl3 = rope_bwd_sum(cfg, parts, cos2, sin2)
    comm.reduce_wait(("w_out", "w_up", "w_down"), after=d_dl3)
    dqkv3 = jnp.concatenate([d_sb3, d_dl3], axis=0)
    tkq = min(tn, cfg.DSB)
    kq = cfg.DSB // tkq
    dw_in = _mm_tn(cfg, "d_w_in", h1, dqkv3, grid=(6 * kq,),
                   a_spec=pl.BlockSpec((S, D), lambda j: (0, 0)),
                   b_spec=pl.BlockSpec((None, S, tkq), lambda j: (j // kq, 0, j % kq)),
                   o_spec=pl.BlockSpec((D, tkq), lambda j: (0, j)),
                   out_shape=jax.ShapeDtypeStruct((D, 6 * cfg.DSB), BF16))
    token = comm.reduce_start(dict(w_in=dw_in))
    dh1 = _mm(cfg, "d_h1", dqkv3, w_in, nt=True, grid=(D // tn, 6 * kq),
              a_spec=pl.BlockSpec((None, S, tkq), lambda j, k: (k // kq, 0, k % kq)),
              b_spec=pl.BlockSpec((tn, tkq), lambda j, k: (j, k)),
              o_spec=pl.BlockSpec((S, tn), lambda j, k: (0, j)),
              out_shape=jax.ShapeDtypeStruct((S, D), F32), acc_shape=(S, tn), dep=token)
    grad_x, dg1 = first_bwd(cfg, dh1, x, g1, dx1)
    comm.reduce_wait(("w_in",), after=grad_x)
    small = dict(loss=loss, g1=dg1, g2=dg2, g3=dg3, g4=dg4, g_sb=dg_sb, g_dl=dg_dl,
                 conv_b=dconv_b.reshape(1, cfg.FF2P), conv_w=dconv_w.transpose(1, 0, 2).reshape(3, cfg.FF2P))
    return grad_x, small


ANY = pl.BlockSpec(memory_space=pl.ANY)


def _me():
    return lax.axis_index("x"), lax.axis_index("y"), lax.axis_index("c")


def _other_chips(x, y):
    return [(1 - x, y), (x, 1 - y), (1 - x, 1 - y)]


def pad_conv_w(cfg, conv_w):
    r, c = conv_w.shape

    def body(w_ref, o_ref):
        o_ref[:, :c] = w_ref[...]
        o_ref[:, c:] = jnp.zeros((r, cfg.FSHP - c), F32)

    return pl.pallas_call(body, name="pad_conv_w", out_shape=jax.ShapeDtypeStruct((r, cfg.FSHP), F32))(conv_w)


def _tile2(r, c):
    return (256, c) if r % 256 == 0 else (r, 512 if c % 512 == 0 else c)


def cast_into(cfg, name, w, pos):
    r, c = w.shape
    _, nr, _, nc = _slab(cfg, name, 0)
    tm, tc = _tile2(r, c)
    wc = nc if tc == c else tc
    assert nr == r and (nc == c or tc == c)

    def body(pos_ref, w_ref, full_ref, scr, sem):
        scr[:, :tc] = w_ref[...].astype(BF16)
        if wc > tc:
            scr[:, tc:] = jnp.zeros((tm, wc - tc), BF16)
        r0, _, c0, _ = _slab(cfg, name, pos_ref[0])
        rows = pl.ds(pl.multiple_of(r0 + pl.program_id(0) * tm, 16), tm)
        cols = pl.ds(pl.multiple_of(c0 + pl.program_id(1) * tc, LANE), wc)
        cp = pltpu.make_async_copy(scr, full_ref.at[rows, cols], sem)
        cp.start()
        cp.wait()

    return pl.pallas_call(
        body, name=f"cast_{name}",
        grid_spec=pltpu.PrefetchScalarGridSpec(
            num_scalar_prefetch=1, grid=(r // tm, c // tc),
            in_specs=[pl.BlockSpec((tm, tc), lambda i, j, p: (i, j))], out_specs=ANY,
            scratch_shapes=[pltpu.VMEM((tm, wc), BF16), pltpu.SemaphoreType.DMA]),
        out_shape=jax.ShapeDtypeStruct(_full_shape(cfg, name), BF16),
        compiler_params=_cp(cfg, ("arbitrary", "arbitrary")),
    )(pos, w)


HBM = pl.BlockSpec(memory_space=pltpu.HBM)
SEM = pl.BlockSpec(memory_space=pltpu.SEMAPHORE)
TOKEN = pl.BlockSpec(memory_space=pltpu.VMEM)
EFFECT = pltpu.SideEffectType.DATAFLOW_SIDE_EFFECTING


def _slab(cfg, name, k):
    D = cfg.D
    if name == "w_in":
        cin = 6 * cfg.DSB // N_CHIPS
        return 0, D, k * cin, cin
    if name == "w_out":
        rout = cfg.DMIX // N_CHIPS
        return k * rout, rout, 0, D
    if name == "w_up":
        return 0, D, k * cfg.FSHP, cfg.FSHP
    rdn = cfg.FSH // 2
    return (k // 2) * cfg.FSHP + (k % 2) * rdn, rdn, 0, D


def _full_shape(cfg, name):
    return dict(w_in=(cfg.D, 6 * cfg.DSB), w_out=(cfg.DMIX, cfg.D), w_up=(cfg.D, cfg.FF2P), w_down=(cfg.FFP, cfg.D))[name]


def _half(cfg, name, ref, k, h):
    r0, nr, c0, nc = _slab(cfg, name, k)
    return ref.at[pl.ds(r0 + h * (nr // 2), nr // 2), pl.ds(c0, nc)]


def _rows_half(ref, h):
    nr = ref.shape[0] // 2
    return ref.at[pl.ds(h * nr, nr), :]


def _remote(src, dst, send_sem, recv_sem, dev):
    return pltpu.make_async_remote_copy(src_ref=src, dst_ref=dst, send_sem=send_sem, recv_sem=recv_sem,
                                        device_id=dev, device_id_type=MESH)


def gather_first(cfg, g_in, sh_cw):
    def body(in_ref, cw_ref, g_in, g_cw, token, ssem, rsem, fssem, frsem, lsem):
        x, y, c = _me()
        me, sib = 2 * x + y, (x, y, 1 - c)
        cw_slot = lambda k: g_cw.at[:, pl.ds(k * cfg.FSHP, cfg.FSHP)]
        local = [pltpu.make_async_copy(cw_ref, cw_slot(me), lsem.at[0])]
        sends = []
        for j, (px, py) in enumerate(_other_chips(x, y)):
            mine = _half(cfg, "w_in", g_in, me, c)
            sends.append(_remote(mine, mine, ssem.at[j], rsem.at[j], (px, py, c)))
            sends.append(_remote(cw_ref, cw_slot(me), ssem.at[3 + j], rsem.at[3 + j], (px, py, c)))
        for cp in local + sends:
            cp.start()
        for j, (px, py) in enumerate(_other_chips(x, y)):
            k = 2 * px + py
            landed = _half(cfg, "w_in", g_in, k, c)
            _remote(landed, landed, ssem.at[j], rsem.at[j], (px, py, c)).wait_recv()
            fwd = _remote(landed, landed, fssem.at[j], frsem.at[j], sib)
            fwd.start()
            sends.append(fwd)
        for j, (px, py) in enumerate(_other_chips(x, y)):
            k = 2 * px + py
            passed = _half(cfg, "w_in", g_in, k, 1 - c)
            _remote(passed, passed, fssem.at[j], frsem.at[j], sib).wait_recv()
            _remote(cw_ref, cw_slot(k), ssem.at[3 + j], rsem.at[3 + j], (px, py, c)).wait_recv()
        for cp in sends:
            cp.wait_send()
        for cp in local:
            cp.wait()
        token[...] = jnp.zeros_like(token)

    return pl.pallas_call(
        body, name="gather_first", in_specs=[ANY, ANY], out_specs=[ANY, ANY, TOKEN],
        out_shape=[jax.ShapeDtypeStruct(_full_shape(cfg, "w_in"), BF16), jax.ShapeDtypeStruct((3, cfg.FF2P), F32),
                   jax.ShapeDtypeStruct((8, LANE), F32)],
        input_output_aliases={0: 0},
        scratch_shapes=[pltpu.SemaphoreType.DMA((6,)), pltpu.SemaphoreType.DMA((6,)), pltpu.SemaphoreType.DMA((3,)),
                        pltpu.SemaphoreType.DMA((3,)), pltpu.SemaphoreType.DMA((1,))],
    )(g_in, sh_cw)


REST = ("w_out", "w_up", "w_down")


def _hbm(a):
    return pltpu.with_memory_space_constraint(a, pltpu.HBM)


def gather_start(cfg, fulls, after):
    n = len(REST)

    def body(*refs):
        lands = refs[:n]
        ssem, rsem = refs[n + 1], refs[n + 2]
        token = refs[-1]
        x, y, c = _me()
        me = 2 * x + y
        for i, name in enumerate(REST):
            mine = _half(cfg, name, lands[i], me, c)
            for j, (px, py) in enumerate(_other_chips(x, y)):
                _remote(mine, mine, ssem.at[3 * i + j], rsem.at[3 * i + j], (px, py, c)).start()
        token[...] = jnp.zeros_like(token)

    ops = [_hbm(a) for a in fulls]
    outs = pl.pallas_call(
        body, name="gather_start",
        in_specs=[HBM] * n + [ANY],
        out_specs=[SEM, SEM] + [HBM] * n + [TOKEN],
        out_shape=[pltpu.SemaphoreType.DMA((3 * n,)), pltpu.SemaphoreType.DMA((3 * n,))]
        + [pltpu.HBM(a.shape, a.dtype) for a in ops] + [jax.ShapeDtypeStruct((8, LANE), F32)],
        input_output_aliases={i: 2 + i for i in range(n)},
        compiler_params=pltpu.CompilerParams(has_side_effects=EFFECT),
    )(*ops, after)
    return outs[0], outs[1], outs[2:2 + n], outs[-1]


def gather_wait(cfg, ssem, rsem, lands, after):
    n = len(REST)

    def body(*refs):
        lands_ = refs[:n]
        ssem_, rsem_ = refs[n], refs[n + 1]
        x, y, c = _me()
        me = 2 * x + y
        for i, name in enumerate(REST):
            for j, (px, py) in enumerate(_other_chips(x, y)):
                cp = _remote(_half(cfg, name, lands_[i], me, c), _half(cfg, name, lands_[i], 2 * px + py, c),
                             ssem_.at[3 * i + j], rsem_.at[3 * i + j], (px, py, c))
                cp.wait_send()
                cp.wait_recv()

    return pl.pallas_call(
        body, name="gather_wait",
        in_specs=[HBM] * n + [SEM, SEM, ANY], out_specs=[HBM] * n,
        out_shape=[pltpu.HBM(a.shape, a.dtype) for a in lands],
        input_output_aliases={i: i for i in range(n)},
        compiler_params=pltpu.CompilerParams(has_side_effects=EFFECT),
    )(*lands, ssem, rsem, after)


def gather_finish(cfg, lands):
    n = len(REST)
    rdn = cfg.FSH // 2
    zpad = jnp.zeros((cfg.FSHP - cfg.FSH, cfg.D), BF16)

    def body(*refs):
        z_ref, outs = refs[0], refs[n + 1:2 * n + 1]
        ssem, rsem, lsem = refs[2 * n + 1:]
        x, y, c = _me()
        sib = (x, y, 1 - c)
        local = [pltpu.make_async_copy(z_ref, outs[2].at[pl.ds(h * cfg.FSHP + 2 * rdn, cfg.FSHP - cfg.FSH), :],
                                       lsem.at[h]) for h in range(2)]
        fwds = []
        for i, name in enumerate(REST):
            for j, (px, py) in enumerate(_other_chips(x, y)):
                landed = _half(cfg, name, outs[i], 2 * px + py, c)
                fwds.append(_remote(landed, landed, ssem.at[3 * i + j], rsem.at[3 * i + j], sib))
        for cp in local + fwds:
            cp.start()
        for i, name in enumerate(REST):
            for j, (px, py) in enumerate(_other_chips(x, y)):
                passed = _half(cfg, name, outs[i], 2 * px + py, 1 - c)
                _remote(passed, passed, ssem.at[3 * i + j], rsem.at[3 * i + j], sib).wait_recv()
        for cp in fwds:
            cp.wait_send()
        for cp in local:
            cp.wait()

    return pl.pallas_call(
        body, name="gather_finish", in_specs=[ANY] * (n + 1), out_specs=[ANY] * n,
        out_shape=[jax.ShapeDtypeStruct(a.shape, a.dtype) for a in lands],
        input_output_aliases={1 + i: i for i in range(n)},
        scratch_shapes=[pltpu.SemaphoreType.DMA((3 * n,)), pltpu.SemaphoreType.DMA((3 * n,)),
                        pltpu.SemaphoreType.DMA((2,))],
    )(zpad, *lands)


def pair_send(cfg, grads):
    names = list(grads)
    n = len(names)

    def half_shape(name):
        _, nr, _, nc = _slab(cfg, name, 0)
        return (N_CHIPS, nr // 2, nc)

    def body(*refs):
        srcs, theirs = refs[:n], refs[n:2 * n]
        ssem, rsem = refs[2 * n:]
        x, y, c = _me()
        cps = []
        for i, name in enumerate(names):
            for k in range(N_CHIPS):
                cps.append(_remote(_half(cfg, name, srcs[i], k, 1 - c), theirs[i].at[k],
                                   ssem.at[N_CHIPS * i + k], rsem.at[N_CHIPS * i + k], (x, y, 1 - c)))
        for cp in cps:
            cp.start()
        for cp in cps:
            cp.wait()

    outs = pl.pallas_call(
        body, name="pair_send_" + "_".join(names), in_specs=[ANY] * n, out_specs=[ANY] * n,
        out_shape=[jax.ShapeDtypeStruct(half_shape(name), BF16) for name in names],
        scratch_shapes=[pltpu.SemaphoreType.DMA((N_CHIPS * n,))] * 2,
    )(*[grads[k] for k in names])
    return dict(zip(names, outs))


def pair_sum(cfg, name, grad, theirs, pos):
    _, r, c = theirs.shape
    tm, tc = _tile2(r, c)

    def body(pos_ref, g_ref, t_ref, o_ref, scr, sem):
        k, i, j = pl.program_id(0), pl.program_id(1), pl.program_id(2)
        r0, nr, c0, _ = _slab(cfg, name, k)
        rows = pl.ds(pl.multiple_of(r0 + pos_ref[1] * (nr // 2) + i * tm, 16), tm)
        cols = pl.ds(pl.multiple_of(c0 + j * tc, LANE), tc)
        cp = pltpu.make_async_copy(g_ref.at[rows, cols], scr, sem)
        cp.start()
        cp.wait()
        o_ref[...] = (scr[...].astype(F32) + t_ref[...].astype(F32)).astype(BF16)

    blk = pl.BlockSpec((None, tm, tc), lambda k, i, j, p: (k, i, j))
    return pl.pallas_call(
        body, name=f"pair_sum_{name}",
        grid_spec=pltpu.PrefetchScalarGridSpec(
            num_scalar_prefetch=1, grid=(N_CHIPS, r // tm, c // tc), in_specs=[ANY, blk], out_specs=blk,
            scratch_shapes=[pltpu.VMEM((tm, tc), BF16), pltpu.SemaphoreType.DMA]),
        out_shape=jax.ShapeDtypeStruct(theirs.shape, BF16),
        compiler_params=_cp(cfg, ("arbitrary",) * 3),
    )(pos, grad, theirs)


def scatter_start(cfg, pres, after):
    names = list(pres)
    n = len(names)

    def body(*refs):
        srcs, lands = refs[:n], refs[n:2 * n]
        ssem, rsem = refs[2 * n + 1], refs[2 * n + 2]
        token = refs[-1]
        x, y, c = _me()
        for i in range(n):
            for j, (px, py) in enumerate(_other_chips(x, y)):
                _remote(srcs[i].at[2 * px + py], lands[i].at[j], ssem.at[3 * i + j], rsem.at[3 * i + j], (px, py, c)).start()
        token[...] = jnp.zeros_like(token)

    lands = [lax.empty((3,) + pres[k].shape[1:], BF16) for k in names]
    ops = [_hbm(a) for a in [pres[k] for k in names] + lands]
    outs = pl.pallas_call(
        body, name="scatter_start_" + "_".join(names),
        in_specs=[HBM] * (2 * n) + [ANY],
        out_specs=[SEM, SEM] + [HBM] * (2 * n) + [TOKEN],
        out_shape=[pltpu.SemaphoreType.DMA((3 * n,)), pltpu.SemaphoreType.DMA((3 * n,))]
        + [pltpu.HBM(a.shape, a.dtype) for a in ops] + [jax.ShapeDtypeStruct((8, LANE), F32)],
        input_output_aliases={i: 2 + i for i in range(2 * n)},
        compiler_params=pltpu.CompilerParams(has_side_effects=EFFECT),
    )(*ops, after)
    return outs[0], outs[1], dict(zip(names, outs[2:2 + n])), dict(zip(names, outs[2 + n:2 + 2 * n])), outs[-1]


def scatter_wait(cfg, ssem, rsem, pres, lands, after):
    names = list(pres)
    n = len(names)

    def body(*refs):
        srcs, lands_ = refs[:n], refs[n:2 * n]
        ssem_, rsem_ = refs[2 * n], refs[2 * n + 1]
        x, y, c = _me()
        for i in range(n):
            for j, (px, py) in enumerate(_other_chips(x, y)):
                cp = _remote(srcs[i].at[2 * px + py], lands_[i].at[j], ssem_.at[3 * i + j], rsem_.at[3 * i + j], (px, py, c))
                cp.wait_send()
                cp.wait_recv()

    ops = [pres[k] for k in names] + [lands[k] for k in names]
    outs = pl.pallas_call(
        body, name="scatter_wait_" + "_".join(names),
        in_specs=[HBM] * (2 * n) + [SEM, SEM, ANY], out_specs=[HBM] * (2 * n),
        out_shape=[pltpu.HBM(a.shape, a.dtype) for a in ops],
        input_output_aliases={i: i for i in range(2 * n)},
        compiler_params=pltpu.CompilerParams(has_side_effects=EFFECT),
    )(*ops, ssem, rsem, after)
    return dict(zip(names, outs[:n])), dict(zip(names, outs[n:]))


def sum_landed(cfg, name, pre, land, pos):
    _, r, c = pre.shape
    tm, tc = _tile2(r, c)
    nrt = r // tm

    def body(pos_ref, p_ref, l_ref, o_ref):
        acc = p_ref[...].astype(F32)
        for j in range(3):
            acc = acc + l_ref[j].astype(F32)
        o_ref[...] = acc

    return pl.pallas_call(
        body, name=f"sum_landed_{name}",
        grid_spec=pltpu.PrefetchScalarGridSpec(
            num_scalar_prefetch=1, grid=(nrt, c // tc),
            in_specs=[pl.BlockSpec((None, tm, tc), lambda i, j, p: (p[0], i, j)),
                      pl.BlockSpec((3, tm, tc), lambda i, j, p: (0, i, j))],
            out_specs=pl.BlockSpec((tm, tc), lambda i, j, p: (p[1] * nrt + i, j))),
        out_shape=jax.ShapeDtypeStruct((2 * r, c), F32), compiler_params=_cp(cfg, ("parallel", "parallel")),
    )(pos, pre, land)


def half_swap(cfg, sums):
    names = list(sums)
    n = len(names)

    def body(*refs):
        outs = refs[n:2 * n]
        ssem, rsem = refs[2 * n:]
        x, y, c = _me()
        cps = [_remote(_rows_half(outs[i], c), _rows_half(outs[i], c), ssem.at[i], rsem.at[i], (x, y, 1 - c))
               for i in range(n)]
        for cp in cps:
            cp.start()
        for i in range(n):
            theirs = _rows_half(outs[i], 1 - c)
            _remote(theirs, theirs, ssem.at[i], rsem.at[i], (x, y, 1 - c)).wait_recv()
        for cp in cps:
            cp.wait_send()

    outs = pl.pallas_call(
        body, name="half_swap_" + "_".join(names), in_specs=[ANY] * n, out_specs=[ANY] * n,
        out_shape=[jax.ShapeDtypeStruct(sums[k].shape, F32) for k in names],
        input_output_aliases={i: i for i in range(n)},
        scratch_shapes=[pltpu.SemaphoreType.DMA((n,))] * 2,
    )(*[sums[k] for k in names])
    return dict(zip(names, outs))


class MeshWeights:
    def __init__(self, cfg, w_sh):
        self.cfg = cfg
        self.pos = jnp.stack([2 * lax.axis_index("x") + lax.axis_index("y"), lax.axis_index("c")]).astype(jnp.int32)
        self.full = {k: cast_into(cfg, k, w_sh[k], self.pos) for k in ("w_in",) + REST}
        self.conv_w = pad_conv_w(cfg, w_sh["conv_w"])
        self.inflight = {}
        self.grads = {}

    def weights_first(self):
        w_in, conv_w, self.token = gather_first(self.cfg, self.full["w_in"], self.conv_w)
        return w_in, conv_w

    def start_rest(self):
        out = gather_start(self.cfg, [self.full[k] for k in REST], self.token)
        self.rest = out[:3]
        return out[3]

    def weights_rest(self, after):
        return gather_finish(self.cfg, gather_wait(self.cfg, *self.rest, after))

    def reduce_start(self, grads):
        theirs = pair_send(self.cfg, grads)
        pres = {k: pair_sum(self.cfg, k, grads[k], theirs[k], self.pos) for k in grads}
        out = scatter_start(self.cfg, pres, jnp.zeros((8, LANE), F32))
        self.inflight[tuple(grads)] = out[:4]
        return out[4]

    def reduce_wait(self, names, after):
        cfg = self.cfg
        pres, lands = scatter_wait(cfg, *self.inflight.pop(tuple(names)), after)
        sums = {k: sum_landed(cfg, k, pres[k], lands[k], self.pos) for k in names}
        self.grads.update(half_swap(cfg, sums))


def allreduce_small(cfg, vec):
    R = vec.shape[0]

    def body(v_ref, o_ref, buf, send_sems, recv_sems):
        x, y, c = _me()
        me = 4 * x + 2 * y + c
        buf[me] = v_ref[...]
        sends = []
        for k in range(1, N_DEV):
            px, py, pc = x ^ (k >> 2), y ^ ((k >> 1) & 1), c ^ (k & 1)
            sends.append(pltpu.make_async_remote_copy(
                src_ref=v_ref, dst_ref=buf.at[me], send_sem=send_sems.at[k], recv_sem=recv_sems.at[k],
                device_id=(px, py, pc), device_id_type=MESH))
        for cp in sends:
            cp.start()
        for k in range(1, N_DEV):
            px, py, pc = x ^ (k >> 2), y ^ ((k >> 1) & 1), c ^ (k & 1)
            pltpu.make_async_remote_copy(
                src_ref=v_ref, dst_ref=buf.at[4 * px + 2 * py + pc], send_sem=send_sems.at[k],
                recv_sem=recv_sems.at[k], device_id=(px, py, pc), device_id_type=MESH).wait_recv()
        for cp in sends:
            cp.wait_send()
        acc = buf[0]
        for j in range(1, N_DEV):
            acc = acc + buf[j]
        o_ref[...] = acc

    return pl.pallas_call(
        body, name="allreduce_small",
        in_specs=[pl.BlockSpec(memory_space=pltpu.VMEM)], out_specs=pl.BlockSpec(memory_space=pltpu.VMEM),
        out_shape=jax.ShapeDtypeStruct((R, LANE), F32),
        scratch_shapes=[pltpu.VMEM((N_DEV, R, LANE), F32), pltpu.SemaphoreType.DMA((N_DEV,)),
                        pltpu.SemaphoreType.DMA((N_DEV,))],
    )(vec)


def adamw(cfg, name, w, m, v, g_parts, tile):
    r, c = w.shape
    tm, tc = tile[0] or r, tile[1] or c
    assert tc == c or all(g.shape[1] == c for g in g_parts)
    n = len(g_parts)
    bc1 = 1.0 - ADAM_B1 ** ADAM_STEP
    bc2 = 1.0 - ADAM_B2 ** ADAM_STEP

    def body(*refs):
        w_ref, m_ref, v_ref = refs[:3]
        g_refs = refs[3:3 + n]
        g_out, d_out, m_out, v_out = refs[3 + n:]
        g = g_refs[0][:, :tc]
        for gr in g_refs[1:]:
            g = g + gr[:, :tc]
        m_new = ADAM_B1 * m_ref[...] + (1.0 - ADAM_B1) * g
        v_new = ADAM_B2 * v_ref[...] + (1.0 - ADAM_B2) * jnp.square(g)
        m_hat = m_new / bc1
        v_hat = v_new / bc2
        g_out[...] = g
        d_out[...] = -ADAM_LR * (m_hat / (jnp.sqrt(v_hat) + ADAM_EPS) + ADAM_WD * w_ref[...])
        m_out[...] = m_new
        v_out[...] = v_new

    blk = pl.BlockSpec((tm, tc), lambda i, j: (i, j))
    return pl.pallas_call(
        body, name=f"adamw_{name}", grid=(r // tm, c // tc),
        in_specs=[blk] * 3 + [pl.BlockSpec((tm, tc if tc < c else g.shape[1]), lambda i, j: (i, j)) for g in g_parts],
        out_specs=[blk] * 4, out_shape=[jax.ShapeDtypeStruct((r, c), F32)] * 4,
        compiler_params=_cp(cfg, ("parallel", "parallel")),
    )(w, m, v, *g_parts)


SMALL_ORDER = ("loss", "g1", "g2", "g3", "g4", "g_sb", "g_dl", "conv_b", "conv_w")


def pack_small(small):
    rows = []
    for k in SMALL_ORDER:
        a = small[k].reshape(-1, LANE)
        rows.append(a)
    flat = jnp.concatenate(rows, axis=0)
    pad = (-flat.shape[0]) % 8
    return jnp.pad(flat, ((0, pad), (0, 0))), [r.shape[0] for r in rows]


def unpack_small(red, small, counts):
    out, at = {}, 0
    for k, n in zip(SMALL_ORDER, counts):
        out[k] = red[at:at + n].reshape(small[k].shape)
        at += n
    return out


def pad_ff(cfg, a):
    r = a.shape[0]
    return jnp.pad(a.reshape(r, N_CHIPS, cfg.FSH), ((0, 0), (0, 0), (0, cfg.FSHP - cfg.FSH))).reshape(r, cfg.FF2P)


def step(cfg, x, target, gains, w_sh, conv_b, m_all, v_all):
    chip = 2 * lax.axis_index("x") + lax.axis_index("y")
    comm = MeshWeights(cfg, w_sh)
    grad_x, small = local_step(cfg, comm, x, target, gains["g1"], gains["g2"], gains["g3"], gains["g4"],
                               gains["g_sb"], gains["g_dl"], pad_ff(cfg, conv_b))

    packed, counts = pack_small(small)
    red = unpack_small(allreduce_small(cfg, packed), small, counts)

    names = ("w_in", "w_out", "w_up", "w_down")
    tms = dict(w_in=(cfg.TM, None), w_out=(cfg.TM, None), w_up=(cfg.TM // 2, None), w_down=(None, cfg.TN // 2))
    res = {}
    for n in names:
        res[n] = adamw(cfg, n, w_sh[n], m_all[n], v_all[n], [comm.grads[n]], tms[n])
    g_cw = lax.dynamic_slice_in_dim(red["conv_w"].reshape(3, N_CHIPS, cfg.FSHP), chip, 1, axis=1)[:, 0, :cfg.FSH]
    res["conv_w"] = adamw(cfg, "conv_w", w_sh["conv_w"], m_all["conv_w"], v_all["conv_w"], [g_cw], (None, None))
    g_cb = red["conv_b"].reshape(1, N_CHIPS, cfg.FSHP)[:, :, :cfg.FSH].reshape(1, N_CHIPS * cfg.FSH)
    res["conv_b"] = adamw(cfg, "conv_b", conv_b, m_all["conv_b"], v_all["conv_b"], [g_cb], (None, None))
    for k in ("g1", "g2", "g3", "g4", "g_sb", "g_dl"):
        res[k] = adamw(cfg, k, gains[k], m_all[k], v_all[k], [red[k]], (None, None))
    return red["loss"][0, 0], grad_x, res


PARAMS = ("pre_mix_gain", "post_mix_gain", "pre_ffn_gain", "post_ffn_gain", "w_in", "sb_out_gain", "dil_out_gain",
          "w_out", "w_up", "conv_w", "conv_b", "w_down")
SHORT = dict(pre_mix_gain="g1", post_mix_gain="g2", pre_ffn_gain="g3", post_ffn_gain="g4", sb_out_gain="g_sb",
             dil_out_gain="g_dl", w_in="w_in", w_out="w_out", w_up="w_up", conv_w="conv_w", conv_b="conv_b",
             w_down="w_down")


def kernel(x, pre_mix_gain, post_mix_gain, pre_ffn_gain, post_ffn_gain, w_in, sb_out_gain, dil_out_gain, w_out, w_up, conv_w, conv_b, w_down, loss_target, m_pre_mix_gain, m_post_mix_gain, m_pre_ffn_gain, m_post_ffn_gain, m_w_in, m_sb_out_gain, m_dil_out_gain, m_w_out, m_w_up, m_conv_w, m_conv_b, m_w_down, v_pre_mix_gain, v_post_mix_gain, v_pre_ffn_gain, v_post_ffn_gain, v_w_in, v_sb_out_gain, v_dil_out_gain, v_w_out, v_w_up, v_conv_w, v_conv_b, v_w_down):
    cfg = CFG
    w = dict(zip(PARAMS, (pre_mix_gain, post_mix_gain, pre_ffn_gain, post_ffn_gain, w_in, sb_out_gain, dil_out_gain,
                          w_out, w_up, conv_w, conv_b, w_down)))
    m = dict(zip(PARAMS, (m_pre_mix_gain, m_post_mix_gain, m_pre_ffn_gain, m_post_ffn_gain, m_w_in, m_sb_out_gain,
                          m_dil_out_gain, m_w_out, m_w_up, m_conv_w, m_conv_b, m_w_down)))
    v = dict(zip(PARAMS, (v_pre_mix_gain, v_post_mix_gain, v_pre_ffn_gain, v_post_ffn_gain, v_w_in, v_sb_out_gain,
                          v_dil_out_gain, v_w_out, v_w_up, v_conv_w, v_conv_b, v_w_down)))
    sq = lambda a: a.reshape(a.shape[1:])
    ws = {SHORT[k]: sq(a) if a.ndim == 3 else a for k, a in w.items()}
    ms = {SHORT[k]: sq(a) if a.ndim == 3 else a for k, a in m.items()}
    vs = {SHORT[k]: sq(a) if a.ndim == 3 else a for k, a in v.items()}
    gains = {k: ws[k] for k in ("g1", "g2", "g3", "g4", "g_sb", "g_dl")}
    w_sh = {k: ws[k] for k in ("w_in", "w_out", "w_up", "conv_w", "w_down")}
    loss, grad_x, res = step(cfg, sq(x), sq(loss_target), gains, w_sh, ws["conv_b"], ms, vs)
    outs = [loss, grad_x.reshape(x.shape)]
    for i in range(4):
        for k in PARAMS:
            outs.append(res[SHORT[k]][i].reshape(w[k].shape))
    return tuple(outs)
```

```python
import functools
import math
from typing import NamedTuple

import jax
import jax.numpy as jnp
from jax import lax
from jax.experimental import pallas as pl
from jax.experimental.pallas import tpu as pltpu

F32 = jnp.float32
BF16 = jnp.bfloat16
MESH = pl.DeviceIdType.MESH

ROPE_THETA = 10000.0
RMS_EPS = 1e-6
ADAM_LR = 0.001
ADAM_B1 = 0.9
ADAM_B2 = 0.999
ADAM_EPS = 1e-08
ADAM_WD = 0.01
ADAM_STEP = 10
GELU_C = math.sqrt(2.0 / math.pi)
NEG_BIG = -1e30
LANE = 128
N_CHIPS = 4
N_DEV = 8


class Cfg(NamedTuple):
    S: int = 2048
    D: int = 2048
    DH: int = 128
    HSB: int = 8
    HDL: int = 8
    QB: int = 128
    SBT: int = 256
    branches: tuple = ((128, 1), (512, 4), (2048, 16))
    FSH: int = 2752
    FSHP: int = 2816
    TM: int = 256
    TNF: int = 256
    TN: int = 512
    VMEM_MB: int = 56

    @property
    def DSB(self):
        return self.HSB * self.DH

    @property
    def DDL(self):
        return self.HDL * self.DH

    @property
    def DMIX(self):
        return self.DSB + self.DDL

    @property
    def FFP(self):
        return 2 * self.FSHP

    @property
    def FF2P(self):
        return 4 * self.FSHP


CFG = Cfg()


def _cp(cfg, sem=None):
    return pltpu.CompilerParams(dimension_semantics=sem, vmem_limit_bytes=cfg.VMEM_MB * 2**20)


def _dot(a, b):
    return jnp.dot(a, b, preferred_element_type=F32)


def _dot_nt(a, b):
    return lax.dot_general(a, b, (((1,), (1,)), ((), ())), preferred_element_type=F32)


def _dot_tn(a, b):
    return lax.dot_general(a, b, (((0,), (0,)), ((), ())), preferred_element_type=F32)


def _dot_split(x, u):
    hi = x.astype(BF16)
    lo = (x - hi.astype(F32)).astype(BF16)
    return _dot(hi, u) + _dot(lo, u)


def _rstd(x):
    return lax.rsqrt(jnp.mean(x * x, axis=-1, keepdims=True) + RMS_EPS)


def _rms_bwd(dy, x, g):
    r = _rstd(x)
    xh = x * r
    dxh = dy * g
    dx = r * (dxh - xh * jnp.mean(dxh * xh, axis=-1, keepdims=True))
    return dx, dy * xh


def _gelu(x):
    t = jnp.tanh(GELU_C * (x + 0.044715 * (x * x * x)))
    return 0.5 * x * (1.0 + t), t


def _gelu_grad(x, t):
    return 0.5 * (1.0 + t) + 0.5 * x * (1.0 - t * t) * (GELU_C * (1.0 + 3 * 0.044715 * (x * x)))


def _row(cfg, w):
    return pl.BlockSpec((cfg.TM, w), lambda i: (i, 0))


def _vec(w):
    return pl.BlockSpec((1, w), lambda i: (0, 0))


def rms_fwd(cfg, x, g):
    S, D = x.shape

    def body(x_ref, g_ref, h_ref):
        xv = x_ref[...]
        h_ref[...] = (xv * _rstd(xv) * g_ref[...]).astype(BF16)

    return pl.pallas_call(
        body, name="rms_fwd", grid=(S // cfg.TM,),
        in_specs=[_row(cfg, D), _vec(D)], out_specs=_row(cfg, D),
        out_shape=jax.ShapeDtypeStruct((S, D), BF16), compiler_params=_cp(cfg, ("parallel",)),
    )(x, g)


def mid_fwd(cfg, x, mo, g_post, g_pre):
    S, D = x.shape

    def body(x_ref, mo_ref, gp_ref, gn_ref, x1_ref, h2_ref):
        mo_v = mo_ref[...]
        x1 = x_ref[...] + mo_v * _rstd(mo_v) * gp_ref[...]
        x1_ref[...] = x1
        h2_ref[...] = (x1 * _rstd(x1) * gn_ref[...]).astype(BF16)

    return pl.pallas_call(
        body, name="mid_fwd", grid=(S // cfg.TM,),
        in_specs=[_row(cfg, D), _row(cfg, D), _vec(D), _vec(D)],
        out_specs=[_row(cfg, D), _row(cfg, D)],
        out_shape=[jax.ShapeDtypeStruct((S, D), F32), jax.ShapeDtypeStruct((S, D), BF16)],
        compiler_params=_cp(cfg, ("parallel",)),
    )(x, mo, g_post, g_pre)


def final_fwd_bwd(cfg, x1, f, g_post, target):
    S, D = x1.shape

    def body(x1_ref, f_ref, g_ref, t_ref, dout_ref, df_ref, dg_ref, loss_ref):
        @pl.when(pl.program_id(0) == 0)
        def _():
            dg_ref[...] = jnp.zeros_like(dg_ref)
            loss_ref[...] = jnp.zeros_like(loss_ref)

        fv = f_ref[...]
        g = g_ref[...]
        out = x1_ref[...] + fv * _rstd(fv) * g
        err = out - t_ref[...]
        loss_ref[...] += 0.5 * jnp.sum(jnp.mean(err * err, axis=-1, keepdims=True), axis=0, keepdims=True)
        dout = err * (1.0 / D)
        dout_ref[...] = dout
        df, dgx = _rms_bwd(dout, fv, g)
        df_ref[...] = df.astype(BF16)
        dg_ref[...] += jnp.sum(dgx, axis=0, keepdims=True)

    return pl.pallas_call(
        body, name="final_fwd_bwd", grid=(S // cfg.TM,),
        in_specs=[_row(cfg, D), _row(cfg, D), _vec(D), _row(cfg, D)],
        out_specs=[_row(cfg, D), _row(cfg, D), _vec(D), _vec(LANE)],
        out_shape=[jax.ShapeDtypeStruct((S, D), F32), jax.ShapeDtypeStruct((S, D), BF16),
                   jax.ShapeDtypeStruct((1, D), F32), jax.ShapeDtypeStruct((1, LANE), F32)],
        compiler_params=_cp(cfg, ("arbitrary",)),
    )(x1, f, g_post, target)


def mid_bwd(cfg, dh2, x1, g_pre, dout, mo, g_post):
    S, D = x1.shape

    def body(dh_ref, x1_ref, gn_ref, do_ref, mo_ref, gp_ref, dx1_ref, dmo_ref, dgn_ref, dgp_ref):
        @pl.when(pl.program_id(0) == 0)
        def _():
            dgn_ref[...] = jnp.zeros_like(dgn_ref)
            dgp_ref[...] = jnp.zeros_like(dgp_ref)

        dx, dgx = _rms_bwd(dh_ref[...], x1_ref[...], gn_ref[...])
        dx1 = do_ref[...] + dx
        dx1_ref[...] = dx1
        dgn_ref[...] += jnp.sum(dgx, axis=0, keepdims=True)
        dmo, dgy = _rms_bwd(dx1, mo_ref[...], gp_ref[...])
        dmo_ref[...] = dmo.astype(BF16)
        dgp_ref[...] += jnp.sum(dgy, axis=0, keepdims=True)

    return pl.pallas_call(
        body, name="mid_bwd", grid=(S // cfg.TM,),
        in_specs=[_row(cfg, D), _row(cfg, D), _vec(D), _row(cfg, D), _row(cfg, D), _vec(D)],
        out_specs=[_row(cfg, D), _row(cfg, D), _vec(D), _vec(D)],
        out_shape=[jax.ShapeDtypeStruct((S, D), F32), jax.ShapeDtypeStruct((S, D), BF16),
                   jax.ShapeDtypeStruct((1, D), F32), jax.ShapeDtypeStruct((1, D), F32)],
        compiler_params=_cp(cfg, ("arbitrary",)),
    )(dh2, x1, g_pre, dout, mo, g_post)


def first_bwd(cfg, dh1, x, g_pre, dx1):
    S, D = x.shape

    def body(dh_ref, x_ref, g_ref, r_ref, dx_ref, dg_ref):
        @pl.when(pl.program_id(0) == 0)
        def _():
            dg_ref[...] = jnp.zeros_like(dg_ref)

        dx, dgx = _rms_bwd(dh_ref[...], x_ref[...], g_ref[...])
        dx_ref[...] = r_ref[...] + dx
        dg_ref[...] += jnp.sum(dgx, axis=0, keepdims=True)

    return pl.pallas_call(
        body, name="first_bwd", grid=(S // cfg.TM,),
        in_specs=[_row(cfg, D), _row(cfg, D), _vec(D), _row(cfg, D)],
        out_specs=[_row(cfg, D), _vec(D)],
        out_shape=[jax.ShapeDtypeStruct((S, D), F32), jax.ShapeDtypeStruct((1, D), F32)],
        compiler_params=_cp(cfg, ("arbitrary",)),
    )(dh1, x, g_pre, dx1)


def _mm(cfg, name, a, b, *, nt, a_spec, b_spec, o_spec, grid, out_shape, acc_shape, dep=None):
    nk = grid[-1]
    dot = _dot_nt if nt else _dot
    deps = [] if dep is None else [dep]

    def body(a_ref, b_ref, *rest):
        o_ref, acc_ref = rest[-2:]
        k = pl.program_id(len(grid) - 1)
        part = dot(a_ref[...], b_ref[...])
        if deps:
            part = part + rest[0][0:1, 0:1]
        if nk == 1:
            o_ref[...] = part.astype(o_ref.dtype)
            return

        @pl.when(k == 0)
        def _():
            acc_ref[...] = part

        @pl.when(k > 0)
        def _():
            acc_ref[...] += part

        @pl.when(k == nk - 1)
        def _():
            o_ref[...] = acc_ref[...].astype(o_ref.dtype)

    sem = ("parallel",) * (len(grid) - 1) + ("arbitrary",)
    dep_specs = [pl.BlockSpec((8, LANE), lambda *_: (0, 0))] * len(deps)
    return pl.pallas_call(
        body, name=name, grid=grid, in_specs=[a_spec, b_spec] + dep_specs, out_specs=o_spec, out_shape=out_shape,
        scratch_shapes=[pltpu.VMEM(acc_shape, F32)], compiler_params=_cp(cfg, sem),
    )(a, b, *deps)


def _mm_tn(cfg, name, a, b, *, a_spec, b_spec, o_spec, grid, out_shape):
    def body(a_ref, b_ref, o_ref):
        o_ref[...] = _dot_tn(a_ref[...], b_ref[...]).astype(o_ref.dtype)

    return pl.pallas_call(
        body, name=name, grid=grid, in_specs=[a_spec, b_spec], out_specs=o_spec, out_shape=out_shape,
        compiler_params=_cp(cfg, ("parallel",) * len(grid)),
    )(a, b)


def qkv_proj(cfg, h1, w_in, cos2, sin2):
    S, D = h1.shape
    tn = 2 * cfg.DH
    per = cfg.DSB // tn
    assert cfg.DSB == cfg.DDL
    nblk = 6 * per

    def body(a_ref, b_ref, c_ref, s_ref, o_ref):
        j = pl.program_id(0)
        acc = _dot(a_ref[...], b_ref[...])
        rope = jnp.logical_and(j >= 3 * per, j < 5 * per)

        @pl.when(rope)
        def _():
            for c in range(tn // cfg.DH):
                xh = acc[:, c * cfg.DH:(c + 1) * cfg.DH]
                o_ref[:, c * cfg.DH:(c + 1) * cfg.DH] = (
                    xh * c_ref[...] + pltpu.roll(xh, cfg.DH // 2, 1) * s_ref[...]).astype(BF16)

        @pl.when(jnp.logical_not(rope))
        def _():
            o_ref[...] = acc.astype(BF16)

    return pl.pallas_call(
        body, name="qkv_proj", grid=(nblk,),
        in_specs=[pl.BlockSpec((S, D), lambda j: (0, 0)), pl.BlockSpec((D, tn), lambda j: (0, j)),
                  pl.BlockSpec((S, cfg.DH), lambda j: (0, 0)), pl.BlockSpec((S, cfg.DH), lambda j: (0, 0))],
        out_specs=pl.BlockSpec((None, S, tn), lambda j: (j // per, 0, j % per)),
        out_shape=jax.ShapeDtypeStruct((6, S, cfg.DSB), BF16),
        compiler_params=_cp(cfg, ("parallel",)),
    )(h1, w_in, cos2, sin2)


def _sb_tile(cfg, q, k, kb, qb):
    QB = cfg.SBT
    z = _dot_nt(q, k) * (cfg.DH ** -0.5)
    t1 = jnp.log1p(jnp.exp(-jnp.abs(z)))
    lb = jnp.minimum(z, 0.0) - t1
    row = lax.broadcasted_iota(jnp.int32, (QB, QB), 0)
    col = lax.broadcasted_iota(jnp.int32, (QB, QB), 1)
    valid = jnp.logical_or(kb < qb, col < row)
    lk = jnp.where(valid, jnp.minimum(-z, 0.0) - t1, 0.0)
    return lb, lk, valid


def sb_fwd(cfg, qkv3):
    S, QB, DH = cfg.S, cfg.SBT, cfg.DH

    def body(q_ref, k_ref, v_ref, o_ref, t_ref):
        row = lax.broadcasted_iota(jnp.int32, (QB, QB), 0)
        col = lax.broadcasted_iota(jnp.int32, (QB, QB), 1)
        u_after = (row > col).astype(BF16)

        def q_loop(qb, _):
            rows = pl.ds(pl.multiple_of(qb * QB, QB), QB)
            q = q_ref[rows, :]

            def k_loop(i, carry):
                o_acc, c = carry
                kb = qb - i
                krows = pl.ds(pl.multiple_of(kb * QB, QB), QB)
                lb, lk, valid = _sb_tile(cfg, q, k_ref[krows, :], kb, qb)
                rem = _dot_split(lk, u_after) + c
                a = jnp.where(valid, jnp.exp(lb + rem), 0.0)
                o_acc = o_acc + _dot(a.astype(BF16), v_ref[krows, :])
                return o_acc, c + jnp.sum(lk, axis=1, keepdims=True)

            o_acc, c = lax.fori_loop(0, qb + 1, k_loop, (jnp.zeros((QB, DH), F32), jnp.zeros((QB, 1), F32)))
            o_ref[rows, :] = o_acc
            t_ref[rows, :] = jnp.broadcast_to(c, (QB, DH))
            return 0

        lax.fori_loop(0, S // QB, q_loop, 0)

    def spec(i):
        return pl.BlockSpec((None, S, DH), lambda h: (i, 0, h))

    return pl.pallas_call(
        body, name="sb_fwd", grid=(cfg.HSB,),
        in_specs=[spec(0), spec(1), spec(2)],
        out_specs=[pl.BlockSpec((S, DH), lambda h: (0, h))] * 2,
        out_shape=[jax.ShapeDtypeStruct((S, cfg.DSB), F32)] * 2,
        compiler_params=_cp(cfg, ("parallel",)),
    )(qkv3, qkv3, qkv3)


def sb_bwd(cfg, qkv3, do_sb, tsum):
    S, QB, DH = cfg.S, cfg.SBT, cfg.DH
    scale = DH ** -0.5

    def body(q_ref, k_ref, v_ref, do_ref, t_ref, d_ref, dk_acc, dv_acc):
        dk_acc[...] = jnp.zeros_like(dk_acc)
        dv_acc[...] = jnp.zeros_like(dv_acc)
        row = lax.broadcasted_iota(jnp.int32, (QB, QB), 0)
        col = lax.broadcasted_iota(jnp.int32, (QB, QB), 1)
        u_upto = (row <= col).astype(BF16)
        u_before = (row < col).astype(BF16)

        def q_loop(qb, _):
            rows = pl.ds(pl.multiple_of(qb * QB, QB), QB)
            q = q_ref[rows, :]
            do = do_ref[rows, :]
            total = t_ref[rows, 0:1]

            def k_loop(kb, carry):
                dq_acc, pc, gc = carry
                krows = pl.ds(pl.multiple_of(kb * QB, QB), QB)
                k = k_ref[krows, :]
                v = v_ref[krows, :]
                lb, lk, valid = _sb_tile(cfg, q, k, kb, qb)
                rem = total - pc - _dot_split(lk, u_upto)
                a = jnp.where(valid, jnp.exp(lb + rem), 0.0)
                g = a * _dot_nt(do, v)
                dv_acc[krows, :] += _dot_tn(a.astype(BF16), do)
                cum = gc + _dot(g.astype(BF16), u_before)
                sig = jnp.exp(lb)
                dz = (jnp.where(valid, g * (1.0 - sig) - cum * sig, 0.0) * scale).astype(BF16)
                dq_acc = dq_acc + _dot(dz, k)
                dk_acc[krows, :] += _dot_tn(dz, q)
                return dq_acc, pc + jnp.sum(lk, axis=1, keepdims=True), gc + jnp.sum(g, axis=1, keepdims=True)

            z1 = jnp.zeros((QB, 1), F32)
            dq_acc, _, _ = lax.fori_loop(0, qb + 1, k_loop, (jnp.zeros((QB, DH), F32), z1, z1))
            d_ref[0, rows, :] = dq_acc.astype(BF16)
            return 0

        lax.fori_loop(0, S // QB, q_loop, 0)
        d_ref[1, :, :] = dk_acc[...].astype(BF16)
        d_ref[2, :, :] = dv_acc[...].astype(BF16)

    def spec(i):
        return pl.BlockSpec((None, S, DH), lambda h: (i, 0, h))

    return pl.pallas_call(
        body, name="sb_bwd", grid=(cfg.HSB,),
        in_specs=[spec(0), spec(1), spec(2), pl.BlockSpec((S, DH), lambda h: (0, h)),
                  pl.BlockSpec((S, DH), lambda h: (0, h))],
        out_specs=pl.BlockSpec((3, S, DH), lambda h: (0, 0, h)),
        out_shape=jax.ShapeDtypeStruct((3, S, cfg.DSB), BF16),
        scratch_shapes=[pltpu.VMEM((S, DH), F32), pltpu.VMEM((S, DH), F32)],
        compiler_params=_cp(cfg, ("parallel",)),
    )(qkv3, qkv3, qkv3, do_sb, tsum)


def _band_mask(cfg, n, n_back):
    QB = cfg.QB
    qi = lax.broadcasted_iota(jnp.int32, (QB, 2 * QB), 0)
    kj = lax.broadcasted_iota(jnp.int32, (QB, 2 * QB), 1)
    dist = QB + qi - kj
    return (dist >= 0) & (dist <= n_back) & jnp.logical_or(n > 0, kj >= QB)


def dil_fwd(cfg, qkv3, window, dil):
    S, QB, DH, H = cfg.S, cfg.QB, cfg.DH, cfg.HDL
    L = S // dil
    n_back = window // dil
    assert n_back <= QB and L % QB == 0
    scale = DH ** -0.5
    view = qkv3.reshape(6, L, dil * cfg.DDL)

    def body(q_ref, k_ref, v_ref, o_ref, lse_ref, kp, vp):
        kp[pl.ds(0, QB), :] = jnp.zeros((QB, DH), BF16)
        vp[pl.ds(0, QB), :] = jnp.zeros((QB, DH), BF16)
        kp[pl.ds(QB, L), :] = k_ref[...]
        vp[pl.ds(QB, L), :] = v_ref[...]

        def loop(n, _):
            rows = pl.ds(pl.multiple_of(n * QB, QB), QB)
            band = pl.ds(pl.multiple_of(n * QB, QB), 2 * QB)
            s = _dot_nt(q_ref[rows, :], kp[band, :]) * scale
            s = jnp.where(_band_mask(cfg, n, n_back), s, NEG_BIG)
            m = jnp.max(s, axis=1, keepdims=True)
            p = jnp.exp(s - m)
            den = jnp.sum(p, axis=1, keepdims=True)
            o_ref[rows, :] = _dot(p.astype(BF16), vp[band, :]) / den
            lse_ref[rows, :] = jnp.broadcast_to(m + jnp.log(den), (QB, DH))
            return 0

        lax.fori_loop(0, L // QB, loop, 0)

    def spec(i):
        return pl.BlockSpec((None, L, DH), lambda h, r: (i, 0, r * H + h))

    o_spec = pl.BlockSpec((L, DH), lambda h, r: (0, r * H + h))
    o, lse = pl.pallas_call(
        body, name=f"dil_fwd_d{dil}", grid=(H, dil),
        in_specs=[spec(3), spec(4), spec(5)], out_specs=[o_spec, o_spec],
        out_shape=[jax.ShapeDtypeStruct((L, dil * cfg.DDL), F32)] * 2,
        scratch_shapes=[pltpu.VMEM((L + QB, DH), BF16)] * 2,
        compiler_params=_cp(cfg, ("parallel", "parallel")),
    )(view, view, view)
    return o.reshape(S, cfg.DDL), lse.reshape(S, cfg.DDL)


def dil_bwd(cfg, qkv3, do_dl, delta, lse_tot, window, dil):
    S, QB, DH, H = cfg.S, cfg.QB, cfg.DH, cfg.HDL
    L = S // dil
    n_back = window // dil
    scale = DH ** -0.5
    view = qkv3.reshape(6, L, dil * cfg.DDL)
    dov, dlv, lsv = (t.reshape(L, dil * cfg.DDL) for t in (do_dl, delta, lse_tot))

    def body(q_ref, k_ref, v_ref, do_ref, dl_ref, ls_ref, dq_ref, dk_ref, dv_ref, kp, vp, dkp, dvp):
        kp[pl.ds(0, QB), :] = jnp.zeros((QB, DH), BF16)
        vp[pl.ds(0, QB), :] = jnp.zeros((QB, DH), BF16)
        kp[pl.ds(QB, L), :] = k_ref[...]
        vp[pl.ds(QB, L), :] = v_ref[...]
        dkp[...] = jnp.zeros_like(dkp)
        dvp[...] = jnp.zeros_like(dvp)

        def loop(n, _):
            rows = pl.ds(pl.multiple_of(n * QB, QB), QB)
            band = pl.ds(pl.multiple_of(n * QB, QB), 2 * QB)
            q = q_ref[rows, :]
            do = do_ref[rows, :]
            kb = kp[band, :]
            s = _dot_nt(q, kb) * scale
            s = jnp.where(_band_mask(cfg, n, n_back), s, NEG_BIG)
            p = jnp.exp(s - ls_ref[rows, 0:1])
            ds = (p * (_dot_nt(do, vp[band, :]) - dl_ref[rows, 0:1]) * scale).astype(BF16)
            dq_ref[rows, :] = _dot(ds, kb)
            dkp[band, :] += _dot_tn(ds, q)
            dvp[band, :] += _dot_tn(p.astype(BF16), do)
            return 0

        lax.fori_loop(0, L // QB, loop, 0)
        dk_ref[...] = dkp[pl.ds(QB, L), :]
        dv_ref[...] = dvp[pl.ds(QB, L), :]

    def spec(i):
        return pl.BlockSpec((None, L, DH), lambda h, r: (i, 0, r * H + h))

    o_spec = pl.BlockSpec((L, DH), lambda h, r: (0, r * H + h))
    outs = pl.pallas_call(
        body, name=f"dil_bwd_d{dil}", grid=(H, dil),
        in_specs=[spec(3), spec(4), spec(5), o_spec, o_spec, o_spec], out_specs=[o_spec] * 3,
        out_shape=[jax.ShapeDtypeStruct((L, dil * cfg.DDL), F32)] * 3,
        scratch_shapes=[pltpu.VMEM((L + QB, DH), BF16)] * 2 + [pltpu.VMEM((L + QB, DH), F32)] * 2,
        compiler_params=_cp(cfg, ("parallel", "parallel")),
    )(view, view, view, dov, dlv, lsv)
    return [t.reshape(S, cfg.DDL) for t in outs]


def combine_fwd(cfg, o_sb, obs, lses, g_sb, g_dl):
    S, DH = cfg.S, cfg.DH
    nb = len(obs)

    def head_norm(o, g):
        return o * lax.rsqrt(jnp.mean(o * o, axis=-1, keepdims=True) + RMS_EPS) * g

    def body(*refs):
        osb_ref, ob_refs, l_refs = refs[0], refs[1:1 + nb], refs[1 + nb:1 + 2 * nb]
        gsb_ref, gdl_ref, mix_ref, odl_ref, lt_ref = refs[1 + 2 * nb:]
        for h in range(cfg.HSB):
            c = slice(h * DH, (h + 1) * DH)
            mix_ref[:, c] = head_norm(osb_ref[:, c], gsb_ref[:, c]).astype(BF16)
        for h in range(cfg.HDL):
            c = slice(h * DH, (h + 1) * DH)
            ls = [r[:, c] for r in l_refs]
            m = functools.reduce(jnp.maximum, ls)
            es = [jnp.exp(l - m) for l in ls]
            tot = functools.reduce(jnp.add, es)
            o = functools.reduce(jnp.add, [(e / tot) * r[:, c] for e, r in zip(es, ob_refs)])
            odl_ref[:, c] = o
            lt_ref[:, c] = m + jnp.log(tot)
            mix_ref[:, cfg.DSB + h * DH:cfg.DSB + (h + 1) * DH] = head_norm(o, gdl_ref[:, c]).astype(BF16)

    return pl.pallas_call(
        body, name="combine_fwd", grid=(S // cfg.TM,),
        in_specs=[_row(cfg, cfg.DSB)] + [_row(cfg, cfg.DDL)] * (2 * nb) + [_vec(cfg.DSB), _vec(cfg.DDL)],
        out_specs=[_row(cfg, cfg.DMIX), _row(cfg, cfg.DDL), _row(cfg, cfg.DDL)],
        out_shape=[jax.ShapeDtypeStruct((S, cfg.DMIX), BF16), jax.ShapeDtypeStruct((S, cfg.DDL), F32),
                   jax.ShapeDtypeStruct((S, cfg.DDL), F32)],
        compiler_params=_cp(cfg, ("parallel",)),
    )(o_sb, *obs, *lses, g_sb, g_dl)


def combine_bwd(cfg, dmix, o_sb, o_dl, g_sb, g_dl):
    S, DH = cfg.S, cfg.DH

    def body(dm_ref, osb_ref, odl_ref, gsb_ref, gdl_ref, dsb_ref, ddl_ref, dl_ref, dgsb_ref, dgdl_ref):
        @pl.when(pl.program_id(0) == 0)
        def _():
            dgsb_ref[...] = jnp.zeros_like(dgsb_ref)
            dgdl_ref[...] = jnp.zeros_like(dgdl_ref)

        for h in range(cfg.HSB):
            c = slice(h * DH, (h + 1) * DH)
            dx, dgx = _rms_bwd(dm_ref[:, c], osb_ref[:, c], gsb_ref[:, c])
            dsb_ref[:, c] = dx.astype(BF16)
            dgsb_ref[:, c] += jnp.sum(dgx, axis=0, keepdims=True)
        for h in range(cfg.HDL):
            c = slice(h * DH, (h + 1) * DH)
            o = odl_ref[:, c]
            dx, dgx = _rms_bwd(dm_ref[:, cfg.DSB + h * DH:cfg.DSB + (h + 1) * DH], o, gdl_ref[:, c])
            ddl_ref[:, c] = dx.astype(BF16)
            dl_ref[:, c] = jnp.broadcast_to(jnp.sum(dx * o, axis=-1, keepdims=True), dx.shape)
            dgdl_ref[:, c] += jnp.sum(dgx, axis=0, keepdims=True)

    return pl.pallas_call(
        body, name="combine_bwd", grid=(S // cfg.TM,),
        in_specs=[_row(cfg, cfg.DMIX), _row(cfg, cfg.DSB), _row(cfg, cfg.DDL), _vec(cfg.DSB), _vec(cfg.DDL)],
        out_specs=[_row(cfg, cfg.DSB), _row(cfg, cfg.DDL), _row(cfg, cfg.DDL), _vec(cfg.DSB), _vec(cfg.DDL)],
        out_shape=[jax.ShapeDtypeStruct((S, cfg.DSB), BF16), jax.ShapeDtypeStruct((S, cfg.DDL), BF16),
                   jax.ShapeDtypeStruct((S, cfg.DDL), F32), jax.ShapeDtypeStruct((1, cfg.DSB), F32),
                   jax.ShapeDtypeStruct((1, cfg.DDL), F32)],
        compiler_params=_cp(cfg, ("arbitrary",)),
    )(dmix, o_sb, o_dl, g_sb, g_dl)


def rope_bwd_sum(cfg, parts, cos2, sin2):
    S, DH = cfg.S, cfg.DH
    nb = len(parts)

    def body(*refs):
        c_ref, s_ref, o_ref = refs[3 * nb], refs[3 * nb + 1], refs[3 * nb + 2]
        for i in range(3):
            for h in range(cfg.HDL):
                c = slice(h * DH, (h + 1) * DH)
                d = functools.reduce(jnp.add, [refs[b * 3 + i][:, c] for b in range(nb)])
                if i < 2:
                    d = d * c_ref[...] + pltpu.roll(d * s_ref[...], DH // 2, 1)
                o_ref[i, :, c] = d.astype(BF16)

    flat = [t for p in parts for t in p]
    return pl.pallas_call(
        body, name="rope_bwd_sum", grid=(S // cfg.TM,),
        in_specs=[_row(cfg, cfg.DDL)] * (3 * nb) + [_row(cfg, DH), _row(cfg, DH)],
        out_specs=pl.BlockSpec((3, cfg.TM, cfg.DDL), lambda i: (0, i, 0)),
        out_shape=jax.ShapeDtypeStruct((3, S, cfg.DDL), BF16),
        compiler_params=_cp(cfg, ("parallel",)),
    )(*flat, cos2, sin2)


def _shift_rows(u, j):
    row = lax.broadcasted_iota(jnp.int32, u.shape, 0)
    return jnp.where(row >= j, pltpu.roll(u, j, 0), 0.0)


def _shift_rows_up(u, j):
    n = u.shape[0]
    row = lax.broadcasted_iota(jnp.int32, u.shape, 0)
    return jnp.where(row < n - j, pltpu.roll(u, n - j, 0), 0.0)


def _conv(u, cw, cb):
    return u * cw[2:3, :] + _shift_rows(u, 1) * cw[1:2, :] + _shift_rows(u, 2) * cw[0:1, :] + cb


def ffn_fwd(cfg, h2, w_up, conv_w, conv_b):
    S, D = h2.shape
    tn, nt = cfg.TNF, cfg.FFP // cfg.TNF

    def body(h_ref, wg_ref, wv_ref, cwg_ref, cwv_ref, cbg_ref, cbv_ref, u_ref, y_ref):
        h = h_ref[...]
        ug = _dot(h, wg_ref[...])
        uv = _dot(h, wv_ref[...])
        u_ref[0] = ug
        u_ref[1] = uv
        gl, _ = _gelu(_conv(ug, cwg_ref[...], cbg_ref[...]))
        y_ref[...] = (gl * _conv(uv, cwv_ref[...], cbv_ref[...])).astype(BF16)

    return pl.pallas_call(
        body, name="ffn_fwd", grid=(nt,),
        in_specs=[pl.BlockSpec((S, D), lambda n: (0, 0)),
                  pl.BlockSpec((D, tn), lambda n: (0, n)), pl.BlockSpec((D, tn), lambda n: (0, n + nt)),
                  pl.BlockSpec((3, tn), lambda n: (0, n)), pl.BlockSpec((3, tn), lambda n: (0, n + nt)),
                  pl.BlockSpec((1, tn), lambda n: (0, n)), pl.BlockSpec((1, tn), lambda n: (0, n + nt))],
        out_specs=[pl.BlockSpec((2, S, tn), lambda n: (0, 0, n)), pl.BlockSpec((S, tn), lambda n: (0, n))],
        out_shape=[jax.ShapeDtypeStruct((2, S, cfg.FFP), F32), jax.ShapeDtypeStruct((S, cfg.FFP), BF16)],
        compiler_params=_cp(cfg, ("parallel",)),
    )(h2, w_up, w_up, conv_w, conv_w, conv_b, conv_b)


def ffn_bwd(cfg, df, w_down, u, conv_w, conv_b):
    S, D = df.shape
    tn, nt = cfg.TNF, cfg.FFP // cfg.TNF

    def conv_bwd(dc, uu, cw):
        du = dc * cw[2:3, :] + _shift_rows_up(dc, 1) * cw[1:2, :] + _shift_rows_up(dc, 2) * cw[0:1, :]
        dws = [jnp.sum(dc * _shift_rows(uu, 2), axis=0, keepdims=True),
               jnp.sum(dc * _shift_rows(uu, 1), axis=0, keepdims=True),
               jnp.sum(dc * uu, axis=0, keepdims=True)]
        return du, dws, jnp.sum(dc, axis=0, keepdims=True)

    def body(df_ref, wd_ref, u_ref, cwg_ref, cwv_ref, cbg_ref, cbv_ref, du_ref, dwd_ref, dcw_ref, dcb_ref):
        dfv = df_ref[...]
        dy = _dot_nt(dfv, wd_ref[...])
        ug, uv = u_ref[0], u_ref[1]
        cwg, cwv = cwg_ref[...], cwv_ref[...]
        cg = _conv(ug, cwg, cbg_ref[...])
        cv = _conv(uv, cwv, cbv_ref[...])
        gl, t = _gelu(cg)
        dwd_ref[...] = _dot_tn((gl * cv).astype(BF16), dfv).astype(BF16)
        dug, dwg, dbg = conv_bwd(dy * cv * _gelu_grad(cg, t), ug, cwg)
        duv, dwv, dbv = conv_bwd(dy * gl, uv, cwv)
        du_ref[0] = dug.astype(BF16)
        du_ref[1] = duv.astype(BF16)
        for j in range(3):
            dcw_ref[0, j:j + 1, :] = dwg[j]
            dcw_ref[1, j:j + 1, :] = dwv[j]
        dcb_ref[0] = dbg
        dcb_ref[1] = dbv

    return pl.pallas_call(
        body, name="ffn_bwd", grid=(nt,),
        in_specs=[pl.BlockSpec((S, D), lambda n: (0, 0)), pl.BlockSpec((tn, D), lambda n: (n, 0)),
                  pl.BlockSpec((2, S, tn), lambda n: (0, 0, n)),
                  pl.BlockSpec((3, tn), lambda n: (0, n)), pl.BlockSpec((3, tn), lambda n: (0, n + nt)),
                  pl.BlockSpec((1, tn), lambda n: (0, n)), pl.BlockSpec((1, tn), lambda n: (0, n + nt))],
        out_specs=[pl.BlockSpec((2, S, tn), lambda n: (0, 0, n)), pl.BlockSpec((tn, D), lambda n: (n, 0)),
                   pl.BlockSpec((2, 3, tn), lambda n: (0, 0, n)), pl.BlockSpec((2, 1, tn), lambda n: (0, 0, n))],
        out_shape=[jax.ShapeDtypeStruct((2, S, cfg.FFP), BF16), jax.ShapeDtypeStruct((cfg.FFP, D), BF16),
                   jax.ShapeDtypeStruct((2, 3, cfg.FFP), F32), jax.ShapeDtypeStruct((2, 1, cfg.FFP), F32)],
        compiler_params=_cp(cfg, ("parallel",)),
    )(df, w_down, u, conv_w, conv_w, conv_b, conv_b)


def rope_tables(cfg):
    inv_freq = ROPE_THETA ** (-jnp.arange(0, cfg.DH, 2, dtype=F32) / cfg.DH)
    ang = jnp.arange(cfg.S, dtype=F32)[:, None] * inv_freq[None, :]
    cos, sin = jnp.cos(ang), jnp.sin(ang)
    return jnp.concatenate([cos, cos], axis=1), jnp.concatenate([-sin, sin], axis=1)


class LocalWeights:
    def __init__(self, w_in, w_out, w_up, conv_w, w_down):
        self.w = (w_in, w_out, w_up, conv_w, w_down)
        self.grads = {}

    def weights_first(self):
        return self.w[0], self.w[3]

    def start_rest(self):
        return None

    def weights_rest(self, after):
        return self.w[1], self.w[2], self.w[4]

    def reduce_start(self, grads):
        self.grads.update(grads)
        return None

    def reduce_wait(self, names, after):
        pass


def _after(a, token):
    return a if token is None else a + token[0, 0].astype(a.dtype)


def local_step(cfg, comm, x, target, g1, g2, g3, g4, g_sb, g_dl, conv_b):
    S, D = cfg.S, cfg.D
    cos2, sin2 = rope_tables(cfg)
    full = lambda r, c: pl.BlockSpec((r, c), lambda j, k: (0, 0))

    w_in, conv_w = comm.weights_first()
    h1 = rms_fwd(cfg, x, g1)
    qkv3 = qkv_proj(cfg, h1, w_in, _after(cos2, comm.start_rest()), sin2)
    o_sb, tsum = sb_fwd(cfg, qkv3)
    obs, lses = zip(*[dil_fwd(cfg, qkv3, w, d) for w, d in cfg.branches])
    mixed, o_dl, lse_tot = combine_fwd(cfg, o_sb, obs, lses, g_sb, g_dl)
    w_out, w_up, w_down = comm.weights_rest(after=mixed)
    tn = cfg.TN
    mo = _mm(cfg, "mix_out", mixed, w_out, nt=False, grid=(D // tn, 1),
             a_spec=full(S, cfg.DMIX), b_spec=pl.BlockSpec((cfg.DMIX, tn), lambda j, k: (0, j)),
             o_spec=pl.BlockSpec((S, tn), lambda j, k: (0, j)),
             out_shape=jax.ShapeDtypeStruct((S, D), F32), acc_shape=(8, LANE))
    x1, h2 = mid_fwd(cfg, x, mo, g2, g3)
    u, y = ffn_fwd(cfg, h2, w_up, conv_w, conv_b)
    tk = cfg.FFP // 4
    f = _mm(cfg, "ffn_down", y, w_down, nt=False, grid=(D // tn, 4),
            a_spec=pl.BlockSpec((S, tk), lambda j, k: (0, k)), b_spec=pl.BlockSpec((tk, tn), lambda j, k: (k, j)),
            o_spec=pl.BlockSpec((S, tn), lambda j, k: (0, j)),
            out_shape=jax.ShapeDtypeStruct((S, D), F32), acc_shape=(S, tn))
    dout, df, dg4, loss = final_fwd_bwd(cfg, x1, f, g4, target)

    du, dw_down, dconv_w, dconv_b = ffn_bwd(cfg, df, w_down, u, conv_w, conv_b)
    kt = cfg.FFP // tk
    dh2 = _mm(cfg, "d_h2", du, w_up, nt=True, grid=(D // tn, 2 * kt),
              a_spec=pl.BlockSpec((None, S, tk), lambda j, k: (k // kt, 0, k % kt)),
              b_spec=pl.BlockSpec((tn, tk), lambda j, k: (j, k)),
              o_spec=pl.BlockSpec((S, tn), lambda j, k: (0, j)),
              out_shape=jax.ShapeDtypeStruct((S, D), F32), acc_shape=(S, tn))
    nf = cfg.FFP // tn if cfg.FFP % tn == 0 else None
    tnu = tn if nf else cfg.TNF
    nf = cfg.FFP // tnu
    dw_up = _mm_tn(cfg, "d_w_up", h2, du, grid=(2 * nf,),
                   a_spec=pl.BlockSpec((S, D), lambda j: (0, 0)),
                   b_spec=pl.BlockSpec((None, S, tnu), lambda j: (j // nf, 0, j % nf)),
                   o_spec=pl.BlockSpec((D, tnu), lambda j: (0, j)),
                   out_shape=jax.ShapeDtypeStruct((D, cfg.FF2P), BF16))
    dx1, dmo, dg3, dg2 = mid_bwd(cfg, dh2, x1, g3, dout, mo, g2)

    dmix = _mm(cfg, "d_mixed", dmo, w_out, nt=True, grid=(cfg.DMIX // tn, 1),
               a_spec=full(S, D), b_spec=pl.BlockSpec((tn, D), lambda j, k: (j, 0)),
               o_spec=pl.BlockSpec((S, tn), lambda j, k: (0, j)),
               out_shape=jax.ShapeDtypeStruct((S, cfg.DMIX), F32), acc_shape=(8, LANE))
    dw_out = _mm_tn(cfg, "d_w_out", mixed, dmo, grid=(D // tn,),
                    a_spec=pl.BlockSpec((S, cfg.DMIX), lambda j: (0, 0)),
                    b_spec=pl.BlockSpec((S, tn), lambda j: (0, j)),
                    o_spec=pl.BlockSpec((cfg.DMIX, tn), lambda j: (0, j)),
                    out_shape=jax.ShapeDtypeStruct((cfg.DMIX, D), BF16))
    token = comm.reduce_start(dict(w_out=dw_out, w_up=dw_up, w_down=dw_down))
    do_sb, do_dl, delta, dg_sb, dg_dl = combine_bwd(cfg, dmix, o_sb, o_dl, _after(g_sb, token), g_dl)
    d_sb3 = sb_bwd(cfg, qkv3, do_sb, tsum)
    parts = [dil_bwd(cfg, qkv3, do_dl, delta, lse_tot, w, d) for w, d in cfg.branches]
    d_dl3 = rope_bwd_sum(cfg, parts, cos2, sin2)
    comm.reduce_wait(("w_out", "w_up", "w_down"), after=d_dl3)
    dqkv3 = jnp.concatenate([d_sb3, d_dl3], axis=0)
    tkq = min(tn, cfg.DSB)
    kq = cfg.DSB // tkq
    dw_in = _mm_tn(cfg, "d_w_in", h1, dqkv3, grid=(6 * kq,),
                   a_spec=pl.BlockSpec((S, D), lambda j: (0, 0)),
                   b_spec=pl.BlockSpec((None, S, tkq), lambda j: (j // kq, 0, j % kq)),
                   o_spec=pl.BlockSpec((D, tkq), lambda j: (0, j)),
                   out_shape=jax.ShapeDtypeStruct((D, 6 * cfg.DSB), BF16))
    token = comm.reduce_start(dict(w_in=dw_in))
    dh1 = _mm(cfg, "d_h1", dqkv3, w_in, nt=True, grid=(D // tn, 6 * kq),
              a_spec=pl.BlockSpec((None, S, tkq), lambda j, k: (k // kq, 0, k % kq)),
              b_spec=pl.BlockSpec((tn, tkq), lambda j, k: (j, k)),
              o_spec=pl.BlockSpec((S, tn), lambda j, k: (0, j)),
              out_shape=jax.ShapeDtypeStruct((S, D), F32), acc_shape=(S, tn), dep=token)
    grad_x, dg1 = first_bwd(cfg, dh1, x, g1, dx1)
    comm.reduce_wait(("w_in",), after=grad_x)
    small = dict(loss=loss, g1=dg1, g2=dg2, g3=dg3, g4=dg4, g_sb=dg_sb, g_dl=dg_dl,
                 conv_b=dconv_b.reshape(1, cfg.FF2P), conv_w=dconv_w.transpose(1, 0, 2).reshape(3, cfg.FF2P))
    return grad_x, small


ANY = pl.BlockSpec(memory_space=pl.ANY)


def _me():
    return lax.axis_index("x"), lax.axis_index("y"), lax.axis_index("c")


def _other_chips(x, y):
    return [(1 - x, y), (x, 1 - y), (1 - x, 1 - y)]


def pad_conv_w(cfg, conv_w):
    r, c = conv_w.shape

    def body(w_ref, o_ref):
        o_ref[:, :c] = w_ref[...]
        o_ref[:, c:] = jnp.zeros((r, cfg.FSHP - c), F32)

    return pl.pallas_call(body, name="pad_conv_w", out_shape=jax.ShapeDtypeStruct((r, cfg.FSHP), F32))(conv_w)


def _tile2(r, c):
    return (256, c) if r % 256 == 0 else (r, 512 if c % 512 == 0 else c)


def cast_into(cfg, name, w, pos):
    r, c = w.shape
    _, nr, _, nc = _slab(cfg, name, 0)
    tm, tc = _tile2(r, c)
    wc = nc if tc == c else tc
    assert nr == r and (nc == c or tc == c)

    def body(pos_ref, w_ref, full_ref, scr, sem):
        scr[:, :tc] = w_ref[...].astype(BF16)
        if wc > tc:
            scr[:, tc:] = jnp.zeros((tm, wc - tc), BF16)
        r0, _, c0, _ = _slab(cfg, name, pos_ref[0])
        rows = pl.ds(pl.multiple_of(r0 + pl.program_id(0) * tm, 16), tm)
        cols = pl.ds(pl.multiple_of(c0 + pl.program_id(1) * tc, LANE), wc)
        cp = pltpu.make_async_copy(scr, full_ref.at[rows, cols], sem)
        cp.start()
        cp.wait()

    return pl.pallas_call(
        body, name=f"cast_{name}",
        grid_spec=pltpu.PrefetchScalarGridSpec(
            num_scalar_prefetch=1, grid=(r // tm, c // tc),
            in_specs=[pl.BlockSpec((tm, tc), lambda i, j, p: (i, j))], out_specs=ANY,
            scratch_shapes=[pltpu.VMEM((tm, wc), BF16), pltpu.SemaphoreType.DMA]),
        out_shape=jax.ShapeDtypeStruct(_full_shape(cfg, name), BF16),
        compiler_params=_cp(cfg, ("arbitrary", "arbitrary")),
    )(pos, w)


HBM = pl.BlockSpec(memory_space=pltpu.HBM)
SEM = pl.BlockSpec(memory_space=pltpu.SEMAPHORE)
TOKEN = pl.BlockSpec(memory_space=pltpu.VMEM)
EFFECT = pltpu.SideEffectType.DATAFLOW_SIDE_EFFECTING


def _slab(cfg, name, k):
    D = cfg.D
    if name == "w_in":
        cin = 6 * cfg.DSB // N_CHIPS
        return 0, D, k * cin, cin
    if name == "w_out":
        rout = cfg.DMIX // N_CHIPS
        return k * rout, rout, 0, D
    if name == "w_up":
        return 0, D, k * cfg.FSHP, cfg.FSHP
    rdn = cfg.FSH // 2
    return (k // 2) * cfg.FSHP + (k % 2) * rdn, rdn, 0, D


def _full_shape(cfg, name):
    return dict(w_in=(cfg.D, 6 * cfg.DSB), w_out=(cfg.DMIX, cfg.D), w_up=(cfg.D, cfg.FF2P), w_down=(cfg.FFP, cfg.D))[name]


def _half(cfg, name, ref, k, h):
    r0, nr, c0, nc = _slab(cfg, name, k)
    return ref.at[pl.ds(r0 + h * (nr // 2), nr // 2), pl.ds(c0, nc)]


def _rows_half(ref, h):
    nr = ref.shape[0] // 2
    return ref.at[pl.ds(h * nr, nr), :]


def _remote(src, dst, send_sem, recv_sem, dev):
    return pltpu.make_async_remote_copy(src_ref=src, dst_ref=dst, send_sem=send_sem, recv_sem=recv_sem,
                                        device_id=dev, device_id_type=MESH)


def gather_first(cfg, g_in, sh_cw):
    def body(in_ref, cw_ref, g_in, g_cw, token, ssem, rsem, fssem, frsem, lsem):
        x, y, c = _me()
        me, sib = 2 * x + y, (x, y, 1 - c)
        cw_slot = lambda k: g_cw.at[:, pl.ds(k * cfg.FSHP, cfg.FSHP)]
        local = [pltpu.make_async_copy(cw_ref, cw_slot(me), lsem.at[0])]
        sends = []
        for j, (px, py) in enumerate(_other_chips(x, y)):
            mine = _half(cfg, "w_in", g_in, me, c)
            sends.append(_remote(mine, mine, ssem.at[j], rsem.at[j], (px, py, c)))
            sends.append(_remote(cw_ref, cw_slot(me), ssem.at[3 + j], rsem.at[3 + j], (px, py, c)))
        for cp in local + sends:
            cp.start()
        for j, (px, py) in enumerate(_other_chips(x, y)):
            k = 2 * px + py
            landed = _half(cfg, "w_in", g_in, k, c)
            _remote(landed, landed, ssem.at[j], rsem.at[j], (px, py, c)).wait_recv()
            fwd = _remote(landed, landed, fssem.at[j], frsem.at[j], sib)
            fwd.start()
            sends.append(fwd)
        for j, (px, py) in enumerate(_other_chips(x, y)):
            k = 2 * px + py
            passed = _half(cfg, "w_in", g_in, k, 1 - c)
            _remote(passed, passed, fssem.at[j], frsem.at[j], sib).wait_recv()
            _remote(cw_ref, cw_slot(k), ssem.at[3 + j], rsem.at[3 + j], (px, py, c)).wait_recv()
        for cp in sends:
            cp.wait_send()
        for cp in local:
            cp.wait()
        token[...] = jnp.zeros_like(token)

    return pl.pallas_call(
        body, name="gather_first", in_specs=[ANY, ANY], out_specs=[ANY, ANY, TOKEN],
        out_shape=[jax.ShapeDtypeStruct(_full_shape(cfg, "w_in"), BF16), jax.ShapeDtypeStruct((3, cfg.FF2P), F32),
                   jax.ShapeDtypeStruct((8, LANE), F32)],
        input_output_aliases={0: 0},
        scratch_shapes=[pltpu.SemaphoreType.DMA((6,)), pltpu.SemaphoreType.DMA((6,)), pltpu.SemaphoreType.DMA((3,)),
                        pltpu.SemaphoreType.DMA((3,)), pltpu.SemaphoreType.DMA((1,))],
    )(g_in, sh_cw)


REST = ("w_out", "w_up", "w_down")


def _hbm(a):
    return pltpu.with_memory_space_constraint(a, pltpu.HBM)


def gather_start(cfg, fulls, after):
    n = len(REST)

    def body(*refs):
        lands = refs[:n]
        ssem, rsem = refs[n + 1], refs[n + 2]
        token = refs[-1]
        x, y, c = _me()
        me = 2 * x + y
        for i, name in enumerate(REST):
            mine = _half(cfg, name, lands[i], me, c)
            for j, (px, py) in enumerate(_other_chips(x, y)):
                _remote(mine, mine, ssem.at[3 * i + j], rsem.at[3 * i + j], (px, py, c)).start()
        token[...] = jnp.zeros_like(token)

    ops = [_hbm(a) for a in fulls]
    outs = pl.pallas_call(
        body, name="gather_start",
        in_specs=[HBM] * n + [ANY],
        out_specs=[SEM, SEM] + [HBM] * n + [TOKEN],
        out_shape=[pltpu.SemaphoreType.DMA((3 * n,)), pltpu.SemaphoreType.DMA((3 * n,))]
        + [pltpu.HBM(a.shape, a.dtype) for a in ops] + [jax.ShapeDtypeStruct((8, LANE), F32)],
        input_output_aliases={i: 2 + i for i in range(n)},
        compiler_params=pltpu.CompilerParams(has_side_effects=EFFECT),
    )(*ops, after)
    return outs[0], outs[1], outs[2:2 + n], outs[-1]


def gather_wait(cfg, ssem, rsem, lands, after):
    n = len(REST)

    def body(*refs):
        lands_ = refs[:n]
        ssem_, rsem_ = refs[n], refs[n + 1]
        x, y, c = _me()
        me = 2 * x + y
        for i, name in enumerate(REST):
            for j, (px, py) in enumerate(_other_chips(x, y)):
                cp = _remote(_half(cfg, name, lands_[i], me, c), _half(cfg, name, lands_[i], 2 * px + py, c),
                             ssem_.at[3 * i + j], rsem_.at[3 * i + j], (px, py, c))
                cp.wait_send()
                cp.wait_recv()

    return pl.pallas_call(
        body, name="gather_wait",
        in_specs=[HBM] * n + [SEM, SEM, ANY], out_specs=[HBM] * n,
        out_shape=[pltpu.HBM(a.shape, a.dtype) for a in lands],
        input_output_aliases={i: i for i in range(n)},
        compiler_params=pltpu.CompilerParams(has_side_effects=EFFECT),
    )(*lands, ssem, rsem, after)


def gather_finish(cfg, lands):
    n = len(REST)
    rdn = cfg.FSH // 2
    zpad = jnp.zeros((cfg.FSHP - cfg.FSH, cfg.D), BF16)

    def body(*refs):
        z_ref, outs = refs[0], refs[n + 1:2 * n + 1]
        ssem, rsem, lsem = refs[2 * n + 1:]
        x, y, c = _me()
        sib = (x, y, 1 - c)
        local = [pltpu.make_async_copy(z_ref, outs[2].at[pl.ds(h * cfg.FSHP + 2 * rdn, cfg.FSHP - cfg.FSH), :],
                                       lsem.at[h]) for h in range(2)]
        fwds = []
        for i, name in enumerate(REST):
            for j, (px, py) in enumerate(_other_chips(x, y)):
                landed = _half(cfg, name, outs[i], 2 * px + py, c)
                fwds.append(_remote(landed, landed, ssem.at[3 * i + j], rsem.at[3 * i + j], sib))
        for cp in local + fwds:
            cp.start()
        for i, name in enumerate(REST):
            for j, (px, py) in enumerate(_other_chips(x, y)):
                passed = _half(cfg, name, outs[i], 2 * px + py, 1 - c)
                _remote(passed, passed, ssem.at[3 * i + j], rsem.at[3 * i + j], sib).wait_recv()
        for cp in fwds:
            cp.wait_send()
        for cp in local:
            cp.wait()

    return pl.pallas_call(
        body, name="gather_finish", in_specs=[ANY] * (n + 1), out_specs=[ANY] * n,
        out_shape=[jax.ShapeDtypeStruct(a.shape, a.dtype) for a in lands],
        input_output_aliases={1 + i: i for i in range(n)},
        scratch_shapes=[pltpu.SemaphoreType.DMA((3 * n,)), pltpu.SemaphoreType.DMA((3 * n,)),
                        pltpu.SemaphoreType.DMA((2,))],
    )(zpad, *lands)


def pair_send(cfg, grads):
    names = list(grads)
    n = len(names)

    def half_shape(name):
        _, nr, _, nc = _slab(cfg, name, 0)
        return (N_CHIPS, nr // 2, nc)

    def body(*refs):
        srcs, theirs = refs[:n], refs[n:2 * n]
        ssem, rsem = refs[2 * n:]
        x, y, c = _me()
        cps = []
        for i, name in enumerate(names):
            for k in range(N_CHIPS):
                cps.append(_remote(_half(cfg, name, srcs[i], k, 1 - c), theirs[i].at[k],
                                   ssem.at[N_CHIPS * i + k], rsem.at[N_CHIPS * i + k], (x, y, 1 - c)))
        for cp in cps:
            cp.start()
        for cp in cps:
            cp.wait()

    outs = pl.pallas_call(
        body, name="pair_send_" + "_".join(names), in_specs=[ANY] * n, out_specs=[ANY] * n,
        out_shape=[jax.ShapeDtypeStruct(half_shape(name), BF16) for name in names],
        scratch_shapes=[pltpu.SemaphoreType.DMA((N_CHIPS * n,))] * 2,
    )(*[grads[k] for k in names])
    return dict(zip(names, outs))


def pair_sum(cfg, name, grad, theirs, pos):
    _, r, c = theirs.shape
    tm, tc = _tile2(r, c)

    def body(pos_ref, g_ref, t_ref, o_ref, scr, sem):
        k, i, j = pl.program_id(0), pl.program_id(1), pl.program_id(2)
        r0, nr, c0, _ = _slab(cfg, name, k)
        rows = pl.ds(pl.multiple_of(r0 + pos_ref[1] * (nr // 2) + i * tm, 16), tm)
        cols = pl.ds(pl.multiple_of(c0 + j * tc, LANE), tc)
        cp = pltpu.make_async_copy(g_ref.at[rows, cols], scr, sem)
        cp.start()
        cp.wait()
        o_ref[...] = (scr[...].astype(F32) + t_ref[...].astype(F32)).astype(BF16)

    blk = pl.BlockSpec((None, tm, tc), lambda k, i, j, p: (k, i, j))
    return pl.pallas_call(
        body, name=f"pair_sum_{name}",
        grid_spec=pltpu.PrefetchScalarGridSpec(
            num_scalar_prefetch=1, grid=(N_CHIPS, r // tm, c // tc), in_specs=[ANY, blk], out_specs=blk,
            scratch_shapes=[pltpu.VMEM((tm, tc), BF16), pltpu.SemaphoreType.DMA]),
        out_shape=jax.ShapeDtypeStruct(theirs.shape, BF16),
        compiler_params=_cp(cfg, ("arbitrary",) * 3),
    )(pos, grad, theirs)


def scatter_start(cfg, pres, after):
    names = list(pres)
    n = len(names)

    def body(*refs):
        srcs, lands = refs[:n], refs[n:2 * n]
        ssem, rsem = refs[2 * n + 1], refs[2 * n + 2]
        token = refs[-1]
        x, y, c = _me()
        for i in range(n):
            for j, (px, py) in enumerate(_other_chips(x, y)):
                _remote(srcs[i].at[2 * px + py], lands[i].at[j], ssem.at[3 * i + j], rsem.at[3 * i + j], (px, py, c)).start()
        token[...] = jnp.zeros_like(token)

    lands = [lax.empty((3,) + pres[k].shape[1:], BF16) for k in names]
    ops = [_hbm(a) for a in [pres[k] for k in names] + lands]
    outs = pl.pallas_call(
        body, name="scatter_start_" + "_".join(names),
        in_specs=[HBM] * (2 * n) + [ANY],
        out_specs=[SEM, SEM] + [HBM] * (2 * n) + [TOKEN],
        out_shape=[pltpu.SemaphoreType.DMA((3 * n,)), pltpu.SemaphoreType.DMA((3 * n,))]
        + [pltpu.HBM(a.shape, a.dtype) for a in ops] + [jax.ShapeDtypeStruct((8, LANE), F32)],
        input_output_aliases={i: 2 + i for i in range(2 * n)},
        compiler_params=pltpu.CompilerParams(has_side_effects=EFFECT),
    )(*ops, after)
    return outs[0], outs[1], dict(zip(names, outs[2:2 + n])), dict(zip(names, outs[2 + n:2 + 2 * n])), outs[-1]


def scatter_wait(cfg, ssem, rsem, pres, lands, after):
    names = list(pres)
    n = len(names)

    def body(*refs):
        srcs, lands_ = refs[:n], refs[n:2 * n]
        ssem_, rsem_ = refs[2 * n], refs[2 * n + 1]
        x, y, c = _me()
        for i in range(n):
            for j, (px, py) in enumerate(_other_chips(x, y)):
                cp = _remote(srcs[i].at[2 * px + py], lands_[i].at[j], ssem_.at[3 * i + j], rsem_.at[3 * i + j], (px, py, c))
                cp.wait_send()
                cp.wait_recv()

    ops = [pres[k] for k in names] + [lands[k] for k in names]
    outs = pl.pallas_call(
        body, name="scatter_wait_" + "_".join(names),
        in_specs=[HBM] * (2 * n) + [SEM, SEM, ANY], out_specs=[HBM] * (2 * n),
        out_shape=[pltpu.HBM(a.shape, a.dtype) for a in ops],
        input_output_aliases={i: i for i in range(2 * n)},
        compiler_params=pltpu.CompilerParams(has_side_effects=EFFECT),
    )(*ops, ssem, rsem, after)
    return dict(zip(names, outs[:n])), dict(zip(names, outs[n:]))


def sum_landed(cfg, name, pre, land, pos):
    _, r, c = pre.shape
    tm, tc = _tile2(r, c)
    nrt = r // tm

    def body(pos_ref, p_ref, l_ref, o_ref):
        acc = p_ref[...].astype(F32)
        for j in range(3):
            acc = acc + l_ref[j].astype(F32)
        o_ref[...] = acc

    return pl.pallas_call(
        body, name=f"sum_landed_{name}",
        grid_spec=pltpu.PrefetchScalarGridSpec(
            num_scalar_prefetch=1, grid=(nrt, c // tc),
            in_specs=[pl.BlockSpec((None, tm, tc), lambda i, j, p: (p[0], i, j)),
                      pl.BlockSpec((3, tm, tc), lambda i, j, p: (0, i, j))],
            out_specs=pl.BlockSpec((tm, tc), lambda i, j, p: (p[1] * nrt + i, j))),
        out_shape=jax.ShapeDtypeStruct((2 * r, c), F32), compiler_params=_cp(cfg, ("parallel", "parallel")),
    )(pos, pre, land)


def half_swap(cfg, sums):
    names = list(sums)
    n = len(names)

    def body(*refs):
        outs = refs[n:2 * n]
        ssem, rsem = refs[2 * n:]
        x, y, c = _me()
        cps = [_remote(_rows_half(outs[i], c), _rows_half(outs[i], c), ssem.at[i], rsem.at[i], (x, y, 1 - c))
               for i in range(n)]
        for cp in cps:
            cp.start()
        for i in range(n):
            theirs = _rows_half(outs[i], 1 - c)
            _remote(theirs, theirs, ssem.at[i], rsem.at[i], (x, y, 1 - c)).wait_recv()
        for cp in cps:
            cp.wait_send()

    outs = pl.pallas_call(
        body, name="half_swap_" + "_".join(names), in_specs=[ANY] * n, out_specs=[ANY] * n,
        out_shape=[jax.ShapeDtypeStruct(sums[k].shape, F32) for k in names],
        input_output_aliases={i: i for i in range(n)},
        scratch_shapes=[pltpu.SemaphoreType.DMA((n,))] * 2,
    )(*[sums[k] for k in names])
    return dict(zip(names, outs))


class MeshWeights:
    def __init__(self, cfg, w_sh):
        self.cfg = cfg
        self.pos = jnp.stack([2 * lax.axis_index("x") + lax.axis_index("y"), lax.axis_index("c")]).astype(jnp.int32)
        self.full = {k: cast_into(cfg, k, w_sh[k], self.pos) for k in ("w_in",) + REST}
        self.conv_w = pad_conv_w(cfg, w_sh["conv_w"])
        self.inflight = {}
        self.grads = {}

    def weights_first(self):
        w_in, conv_w, self.token = gather_first(self.cfg, self.full["w_in"], self.conv_w)
        return w_in, conv_w

    def start_rest(self):
        out = gather_start(self.cfg, [self.full[k] for k in REST], self.token)
        self.rest = out[:3]
        return out[3]

    def weights_rest(self, after):
        return gather_finish(self.cfg, gather_wait(self.cfg, *self.rest, after))

    def reduce_start(self, grads):
        theirs = pair_send(self.cfg, grads)
        pres = {k: pair_sum(self.cfg, k, grads[k], theirs[k], self.pos) for k in grads}
        out = scatter_start(self.cfg, pres, jnp.zeros((8, LANE), F32))
        self.inflight[tuple(grads)] = out[:4]
        return out[4]

    def reduce_wait(self, names, after):
        cfg = self.cfg
        pres, lands = scatter_wait(cfg, *self.inflight.pop(tuple(names)), after)
        sums = {k: sum_landed(cfg, k, pres[k], lands[k], self.pos) for k in names}
        self.grads.update(half_swap(cfg, sums))


def allreduce_small(cfg, vec):
    R = vec.shape[0]

    def body(v_ref, o_ref, buf, send_sems, recv_sems):
        x, y, c = _me()
        me = 4 * x + 2 * y + c
        buf[me] = v_ref[...]
        sends = []
        for k in range(1, N_DEV):
            px, py, pc = x ^ (k >> 2), y ^ ((k >> 1) & 1), c ^ (k & 1)
            sends.append(pltpu.make_async_remote_copy(
                src_ref=v_ref, dst_ref=buf.at[me], send_sem=send_sems.at[k], recv_sem=recv_sems.at[k],
                device_id=(px, py, pc), device_id_type=MESH))
        for cp in sends:
            cp.start()
        for k in range(1, N_DEV):
            px, py, pc = x ^ (k >> 2), y ^ ((k >> 1) & 1), c ^ (k & 1)
            pltpu.make_async_remote_copy(
                src_ref=v_ref, dst_ref=buf.at[4 * px + 2 * py + pc], send_sem=send_sems.at[k],
                recv_sem=recv_sems.at[k], device_id=(px, py, pc), device_id_type=MESH).wait_recv()
        for cp in sends:
            cp.wait_send()
        acc = buf[0]
        for j in range(1, N_DEV):
            acc = acc + buf[j]
        o_ref[...] = acc

    return pl.pallas_call(
        body, name="allreduce_small",
        in_specs=[pl.BlockSpec(memory_space=pltpu.VMEM)], out_specs=pl.BlockSpec(memory_space=pltpu.VMEM),
        out_shape=jax.ShapeDtypeStruct((R, LANE), F32),
        scratch_shapes=[pltpu.VMEM((N_DEV, R, LANE), F32), pltpu.SemaphoreType.DMA((N_DEV,)),
                        pltpu.SemaphoreType.DMA((N_DEV,))],
    )(vec)


def adamw(cfg, name, w, m, v, g_parts, tile):
    r, c = w.shape
    tm, tc = tile[0] or r, tile[1] or c
    assert tc == c or all(g.shape[1] == c for g in g_parts)
    n = len(g_parts)
    bc1 = 1.0 - ADAM_B1 ** ADAM_STEP
    bc2 = 1.0 - ADAM_B2 ** ADAM_STEP

    def body(*refs):
        w_ref, m_ref, v_ref = refs[:3]
        g_refs = refs[3:3 + n]
        g_out, d_out, m_out, v_out = refs[3 + n:]
        g = g_refs[0][:, :tc]
        for gr in g_refs[1:]:
            g = g + gr[:, :tc]
        m_new = ADAM_B1 * m_ref[...] + (1.0 - ADAM_B1) * g
        v_new = ADAM_B2 * v_ref[...] + (1.0 - ADAM_B2) * jnp.square(g)
        m_hat = m_new / bc1
        v_hat = v_new / bc2
        g_out[...] = g
        d_out[...] = -ADAM_LR * (m_hat / (jnp.sqrt(v_hat) + ADAM_EPS) + ADAM_WD * w_ref[...])
        m_out[...] = m_new
        v_out[...] = v_new

    blk = pl.BlockSpec((tm, tc), lambda i, j: (i, j))
    return pl.pallas_call(
        body, name=f"adamw_{name}", grid=(r // tm, c // tc),
        in_specs=[blk] * 3 + [pl.BlockSpec((tm, tc if tc < c else g.shape[1]), lambda i, j: (i, j)) for g in g_parts],
        out_specs=[blk] * 4, out_shape=[jax.ShapeDtypeStruct((r, c), F32)] * 4,
        compiler_params=_cp(cfg, ("parallel", "parallel")),
    )(w, m, v, *g_parts)


SMALL_ORDER = ("loss", "g1", "g2", "g3", "g4", "g_sb", "g_dl", "conv_b", "conv_w")


def pack_small(small):
    rows = []
    for k in SMALL_ORDER:
        a = small[k].reshape(-1, LANE)
        rows.append(a)
    flat = jnp.concatenate(rows, axis=0)
    pad = (-flat.shape[0]) % 8
    return jnp.pad(flat, ((0, pad), (0, 0))), [r.shape[0] for r in rows]


def unpack_small(red, small, counts):
    out, at = {}, 0
    for k, n in zip(SMALL_ORDER, counts):
        out[k] = red[at:at + n].reshape(small[k].shape)
        at += n
    return out


def pad_ff(cfg, a):
    r = a.shape[0]
    return jnp.pad(a.reshape(r, N_CHIPS, cfg.FSH), ((0, 0), (0, 0), (0, cfg.FSHP - cfg.FSH))).reshape(r, cfg.FF2P)


def step(cfg, x, target, gains, w_sh, conv_b, m_all, v_all):
    chip = 2 * lax.axis_index("x") + lax.axis_index("y")
    comm = MeshWeights(cfg, w_sh)
    grad_x, small = local_step(cfg, comm, x, target, gains["g1"], gains["g2"], gains["g3"], gains["g4"],
                               gains["g_sb"], gains["g_dl"], pad_ff(cfg, conv_b))

    packed, counts = pack_small(small)
    red = unpack_small(allreduce_small(cfg, packed), small, counts)

    names = ("w_in", "w_out", "w_up", "w_down")
    tms = dict(w_in=(cfg.TM, None), w_out=(cfg.TM, None), w_up=(cfg.TM // 2, None), w_down=(None, cfg.TN // 2))
    res = {}
    for n in names:
        res[n] = adamw(cfg, n, w_sh[n], m_all[n], v_all[n], [comm.grads[n]], tms[n])
    g_cw = lax.dynamic_slice_in_dim(red["conv_w"].reshape(3, N_CHIPS, cfg.FSHP), chip, 1, axis=1)[:, 0, :cfg.FSH]
    res["conv_w"] = adamw(cfg, "conv_w", w_sh["conv_w"], m_all["conv_w"], v_all["conv_w"], [g_cw], (None, None))
    g_cb = red["conv_b"].reshape(1, N_CHIPS, cfg.FSHP)[:, :, :cfg.FSH].reshape(1, N_CHIPS * cfg.FSH)
    res["conv_b"] = adamw(cfg, "conv_b", conv_b, m_all["conv_b"], v_all["conv_b"], [g_cb], (None, None))
    for k in ("g1", "g2", "g3", "g4", "g_sb", "g_dl"):
        res[k] = adamw(cfg, k, gains[k], m_all[k], v_all[k], [red[k]], (None, None))
    return red["loss"][0, 0], grad_x, res


PARAMS = ("pre_mix_gain", "post_mix_gain", "pre_ffn_gain", "post_ffn_gain", "w_in", "sb_out_gain", "dil_out_gain",
          "w_out", "w_up", "conv_w", "conv_b", "w_down")
SHORT = dict(pre_mix_gain="g1", post_mix_gain="g2", pre_ffn_gain="g3", post_ffn_gain="g4", sb_out_gain="g_sb",
             dil_out_gain="g_dl", w_in="w_in", w_out="w_out", w_up="w_up", conv_w="conv_w", conv_b="conv_b",
             w_down="w_down")


def kernel(x, pre_mix_gain, post_mix_gain, pre_ffn_gain, post_ffn_gain, w_in, sb_out_gain, dil_out_gain, w_out, w_up, conv_w, conv_b, w_down, loss_target, m_pre_mix_gain, m_post_mix_gain, m_pre_ffn_gain, m_post_ffn_gain, m_w_in, m_sb_out_gain, m_dil_out_gain, m_w_out, m_w_up, m_conv_w, m_conv_b, m_w_down, v_pre_mix_gain, v_post_mix_gain, v_pre_ffn_gain, v_post_ffn_gain, v_w_in, v_sb_out_gain, v_dil_out_gain, v_w_out, v_w_up, v_conv_w, v_conv_b, v_w_down):
    cfg = CFG
    w = dict(zip(PARAMS, (pre_mix_gain, post_mix_gain, pre_ffn_gain, post_ffn_gain, w_in, sb_out_gain, dil_out_gain,
                          w_out, w_up, conv_w, conv_b, w_down)))
    m = dict(zip(PARAMS, (m_pre_mix_gain, m_post_mix_gain, m_pre_ffn_gain, m_post_ffn_gain, m_w_in, m_sb_out_gain,
                          m_dil_out_gain, m_w_out, m_w_up, m_conv_w, m_conv_b, m_w_down)))
    v = dict(zip(PARAMS, (v_pre_mix_gain, v_post_mix_gain, v_pre_ffn_gain, v_post_ffn_gain, v_w_in, v_sb_out_gain,
                          v_dil_out_gain, v_w_out, v_w_up, v_conv_w, v_conv_b, v_w_down)))
    sq = lambda a: a.reshape(a.shape[1:])
    ws = {SHORT[k]: sq(a) if a.ndim == 3 else a for k, a in w.items()}
    ms = {SHORT[k]: sq(a) if a.ndim == 3 else a for k, a in m.items()}
    vs = {SHORT[k]: sq(a) if a.ndim == 3 else a for k, a in v.items()}
    gains = {k: ws[k] for k in ("g1", "g2", "g3", "g4", "g_sb", "g_dl")}
    w_sh = {k: ws[k] for k in ("w_in", "w_out", "w_up", "conv_w", "w_down")}
    loss, grad_x, res = step(cfg, sq(x), sq(loss_target), gains, w_sh, ws["conv_b"], ms, vs)
    outs = [loss, grad_x.reshape(x.shape)]
    for i in range(4):
        for k in PARAMS:
            outs.append(res[SHORT[k]][i].reshape(w[k].shape))
    return tuple(outs)
```

```python
import functools
import math
from typing import NamedTuple

import jax
import jax.numpy as jnp
from jax import lax
from jax.experimental import pallas as pl
from jax.experimental.pallas import tpu as pltpu

F32 = jnp.float32
BF16 = jnp.bfloat16
MESH = pl.DeviceIdType.MESH

ROPE_THETA = 10000.0
RMS_EPS = 1e-6
ADAM_LR = 0.001
ADAM_B1 = 0.9
ADAM_B2 = 0.999
ADAM_EPS = 1e-08
ADAM_WD = 0.01
ADAM_STEP = 10
GELU_C = math.sqrt(2.0 / math.pi)
NEG_BIG = -1e30
LANE = 128
N_CHIPS = 4
N_DEV = 8


class Cfg(NamedTuple):
    S: int = 2048
    D: int = 2048
    DH: int = 128
    HSB: int = 8
    HDL: int = 8
    QB: int = 128
    SBT: int = 256
    branches: tuple = ((128, 1), (512, 4), (2048, 16))
    FSH: int = 2752
    FSHP: int = 2816
    TM: int = 256
    TNF: int = 256
    TN: int = 512
    VMEM_MB: int = 56

    @property
    def DSB(self):
        return self.HSB * self.DH

    @property
    def DDL(self):
        return self.HDL * self.DH

    @property
    def DMIX(self):
        return self.DSB + self.DDL

    @property
    def FFP(self):
        return 2 * self.FSHP

    @property
    def FF2P(self):
        return 4 * self.FSHP


CFG = Cfg()


def _cp(cfg, sem=None):
    return pltpu.CompilerParams(dimension_semantics=sem, vmem_limit_bytes=cfg.VMEM_MB * 2**20)


def _dot(a, b):
    return jnp.dot(a, b, preferred_element_type=F32)


def _dot_nt(a, b):
    return lax.dot_general(a, b, (((1,), (1,)), ((), ())), preferred_element_type=F32)


def _dot_tn(a, b):
    return lax.dot_general(a, b, (((0,), (0,)), ((), ())), preferred_element_type=F32)


def _dot_split(x, u):
    hi = x.astype(BF16)
    lo = (x - hi.astype(F32)).astype(BF16)
    return _dot(hi, u) + _dot(lo, u)


def _rstd(x):
    return lax.rsqrt(jnp.mean(x * x, axis=-1, keepdims=True) + RMS_EPS)


def _rms_bwd(dy, x, g):
    r = _rstd(x)
    xh = x * r
    dxh = dy * g
    dx = r * (dxh - xh * jnp.mean(dxh * xh, axis=-1, keepdims=True))
    return dx, dy * xh


def _gelu(x):
    t = jnp.tanh(GELU_C * (x + 0.044715 * (x * x * x)))
    return 0.5 * x * (1.0 + t), t


def _gelu_grad(x, t):
    return 0.5 * (1.0 + t) + 0.5 * x * (1.0 - t * t) * (GELU_C * (1.0 + 3 * 0.044715 * (x * x)))


def _row(cfg, w):
    return pl.BlockSpec((cfg.TM, w), lambda i: (i, 0))


def _vec(w):
    return pl.BlockSpec((1, w), lambda i: (0, 0))


def rms_fwd(cfg, x, g):
    S, D = x.shape

    def body(x_ref, g_ref, h_ref):
        xv = x_ref[...]
        h_ref[...] = (xv * _rstd(xv) * g_ref[...]).astype(BF16)

    return pl.pallas_call(
        body, name="rms_fwd", grid=(S // cfg.TM,),
        in_specs=[_row(cfg, D), _vec(D)], out_specs=_row(cfg, D),
        out_shape=jax.ShapeDtypeStruct((S, D), BF16), compiler_params=_cp(cfg, ("parallel",)),
    )(x, g)


def mid_fwd(cfg, x, mo, g_post, g_pre):
    S, D = x.shape

    def body(x_ref, mo_ref, gp_ref, gn_ref, x1_ref, h2_ref):
        mo_v = mo_ref[...]
        x1 = x_ref[...] + mo_v * _rstd(mo_v) * gp_ref[...]
        x1_ref[...] = x1
        h2_ref[...] = (x1 * _rstd(x1) * gn_ref[...]).astype(BF16)

    return pl.pallas_call(
        body, name="mid_fwd", grid=(S // cfg.TM,),
        in_specs=[_row(cfg, D), _row(cfg, D), _vec(D), _vec(D)],
        out_specs=[_row(cfg, D), _row(cfg, D)],
        out_shape=[jax.ShapeDtypeStruct((S, D), F32), jax.ShapeDtypeStruct((S, D), BF16)],
        compiler_params=_cp(cfg, ("parallel",)),
    )(x, mo, g_post, g_pre)


def final_fwd_bwd(cfg, x1, f, g_post, target):
    S, D = x1.shape

    def body(x1_ref, f_ref, g_ref, t_ref, dout_ref, df_ref, dg_ref, loss_ref):
        @pl.when(pl.program_id(0) == 0)
        def _():
            dg_ref[...] = jnp.zeros_like(dg_ref)
            loss_ref[...] = jnp.zeros_like(loss_ref)

        fv = f_ref[...]
        g = g_ref[...]
        out = x1_ref[...] + fv * _rstd(fv) * g
        err = out - t_ref[...]
        loss_ref[...] += 0.5 * jnp.sum(jnp.mean(err * err, axis=-1, keepdims=True), axis=0, keepdims=True)
        dout = err * (1.0 / D)
        dout_ref[...] = dout
        df, dgx = _rms_bwd(dout, fv, g)
        df_ref[...] = df.astype(BF16)
        dg_ref[...] += jnp.sum(dgx, axis=0, keepdims=True)

    return pl.pallas_call(
        body, name="final_fwd_bwd", grid=(S // cfg.TM,),
        in_specs=[_row(cfg, D), _row(cfg, D), _vec(D), _row(cfg, D)],
        out_specs=[_row(cfg, D), _row(cfg, D), _vec(D), _vec(LANE)],
        out_shape=[jax.ShapeDtypeStruct((S, D), F32), jax.ShapeDtypeStruct((S, D), BF16),
                   jax.ShapeDtypeStruct((1, D), F32), jax.ShapeDtypeStruct((1, LANE), F32)],
        compiler_params=_cp(cfg, ("arbitrary",)),
    )(x1, f, g_post, target)


def mid_bwd(cfg, dh2, x1, g_pre, dout, mo, g_post):
    S, D = x1.shape

    def body(dh_ref, x1_ref, gn_ref, do_ref, mo_ref, gp_ref, dx1_ref, dmo_ref, dgn_ref, dgp_ref):
        @pl.when(pl.program_id(0) == 0)
        def _():
            dgn_ref[...] = jnp.zeros_like(dgn_ref)
            dgp_ref[...] = jnp.zeros_like(dgp_ref)

        dx, dgx = _rms_bwd(dh_ref[...], x1_ref[...], gn_ref[...])
        dx1 = do_ref[...] + dx
        dx1_ref[...] = dx1
        dgn_ref[...] += jnp.sum(dgx, axis=0, keepdims=True)
        dmo, dgy = _rms_bwd(dx1, mo_ref[...], gp_ref[...])
        dmo_ref[...] = dmo.astype(BF16)
        dgp_ref[...] += jnp.sum(dgy, axis=0, keepdims=True)

    return pl.pallas_call(
        body, name="mid_bwd", grid=(S // cfg.TM,),
        in_specs=[_row(cfg, D), _row(cfg, D), _vec(D), _row(cfg, D), _row(cfg, D), _vec(D)],
        out_specs=[_row(cfg, D), _row(cfg, D), _vec(D), _vec(D)],
        out_shape=[jax.ShapeDtypeStruct((S, D), F32), jax.ShapeDtypeStruct((S, D), BF16),
                   jax.ShapeDtypeStruct((1, D), F32), jax.ShapeDtypeStruct((1, D), F32)],
        compiler_params=_cp(cfg, ("arbitrary",)),
    )(dh2, x1, g_pre, dout, mo, g_post)


def first_bwd(cfg, dh1, x, g_pre, dx1):
    S, D = x.shape

    def body(dh_ref, x_ref, g_ref, r_ref, dx_ref, dg_ref):
        @pl.when(pl.program_id(0) == 0)
        def _():
            dg_ref[...] = jnp.zeros_like(dg_ref)

        dx, dgx = _rms_bwd(dh_ref[...], x_ref[...], g_ref[...])
        dx_ref[...] = r_ref[...] + dx
        dg_ref[...] += jnp.sum(dgx, axis=0, keepdims=True)

    return pl.pallas_call(
        body, name="first_bwd", grid=(S // cfg.TM,),
        in_specs=[_row(cfg, D), _row(cfg, D), _vec(D), _row(cfg, D)],
        out_specs=[_row(cfg, D), _vec(D)],
        out_shape=[jax.ShapeDtypeStruct((S, D), F32), jax.ShapeDtypeStruct((1, D), F32)],
        compiler_params=_cp(cfg, ("arbitrary",)),
    )(dh1, x, g_pre, dx1)


def _mm(cfg, name, a, b, *, nt, a_spec, b_spec, o_spec, grid, out_shape, acc_shape, dep=None):
    nk = grid[-1]
    dot = _dot_nt if nt else _dot
    deps = [] if dep is None else [dep]

    def body(a_ref, b_ref, *rest):
        o_ref, acc_ref = rest[-2:]
        k = pl.program_id(len(grid) - 1)
        part = dot(a_ref[...], b_ref[...])
        if deps:
            part = part + rest[0][0:1, 0:1]
        if nk == 1:
            o_ref[...] = part.astype(o_ref.dtype)
            return

        @pl.when(k == 0)
        def _():
            acc_ref[...] = part

        @pl.when(k > 0)
        def _():
            acc_ref[...] += part

        @pl.when(k == nk - 1)
        def _():
            o_ref[...] = acc_ref[...].astype(o_ref.dtype)

    sem = ("parallel",) * (len(grid) - 1) + ("arbitrary",)
    dep_specs = [pl.BlockSpec((8, LANE), lambda *_: (0, 0))] * len(deps)
    return pl.pallas_call(
        body, name=name, grid=grid, in_specs=[a_spec, b_spec] + dep_specs, out_specs=o_spec, out_shape=out_shape,
        scratch_shapes=[pltpu.VMEM(acc_shape, F32)], compiler_params=_cp(cfg, sem),
    )(a, b, *deps)


def _mm_tn(cfg, name, a, b, *, a_spec, b_spec, o_spec, grid, out_shape):
    def body(a_ref, b_ref, o_ref):
        o_ref[...] = _dot_tn(a_ref[...], b_ref[...]).astype(o_ref.dtype)

    return pl.pallas_call(
        body, name=name, grid=grid, in_specs=[a_spec, b_spec], out_specs=o_spec, out_shape=out_shape,
        compiler_params=_cp(cfg, ("parallel",) * len(grid)),
    )(a, b)


def qkv_proj(cfg, h1, w_in, cos2, sin2):
    S, D = h1.shape
    tn = 2 * cfg.DH
    per = cfg.DSB // tn
    assert cfg.DSB == cfg.DDL
    nblk = 6 * per

    def body(a_ref, b_ref, c_ref, s_ref, o_ref):
        j = pl.program_id(0)
        acc = _dot(a_ref[...], b_ref[...])
        rope = jnp.logical_and(j >= 3 * per, j < 5 * per)

        @pl.when(rope)
        def _():
            for c in range(tn // cfg.DH):
                xh = acc[:, c * cfg.DH:(c + 1) * cfg.DH]
                o_ref[:, c * cfg.DH:(c + 1) * cfg.DH] = (
                    xh * c_ref[...] + pltpu.roll(xh, cfg.DH // 2, 1) * s_ref[...]).astype(BF16)

        @pl.when(jnp.logical_not(rope))
        def _():
            o_ref[...] = acc.astype(BF16)

    return pl.pallas_call(
        body, name="qkv_proj", grid=(nblk,),
        in_specs=[pl.BlockSpec((S, D), lambda j: (0, 0)), pl.BlockSpec((D, tn), lambda j: (0, j)),
                  pl.BlockSpec((S, cfg.DH), lambda j: (0, 0)), pl.BlockSpec((S, cfg.DH), lambda j: (0, 0))],
        out_specs=pl.BlockSpec((None, S, tn), lambda j: (j // per, 0, j % per)),
        out_shape=jax.ShapeDtypeStruct((6, S, cfg.DSB), BF16),
        compiler_params=_cp(cfg, ("parallel",)),
    )(h1, w_in, cos2, sin2)


def _sb_tile(cfg, q, k, kb, qb):
    QB = cfg.SBT
    z = _dot_nt(q, k) * (cfg.DH ** -0.5)
    t1 = jnp.log1p(jnp.exp(-jnp.abs(z)))
    lb = jnp.minimum(z, 0.0) - t1
    row = lax.broadcasted_iota(jnp.int32, (QB, QB), 0)
    col = lax.broadcasted_iota(jnp.int32, (QB, QB), 1)
    valid = jnp.logical_or(kb < qb, col < row)
    lk = jnp.where(valid, jnp.minimum(-z, 0.0) - t1, 0.0)
    return lb, lk, valid


def sb_fwd(cfg, qkv3):
    S, QB, DH = cfg.S, cfg.SBT, cfg.DH

    def body(q_ref, k_ref, v_ref, o_ref, t_ref):
        row = lax.broadcasted_iota(jnp.int32, (QB, QB), 0)
        col = lax.broadcasted_iota(jnp.int32, (QB, QB), 1)
        u_after = (row > col).astype(BF16)

        def q_loop(qb, _):
            rows = pl.ds(pl.multiple_of(qb * QB, QB), QB)
            q = q_ref[rows, :]

            def k_loop(i, carry):
                o_acc, c = carry
                kb = qb - i
                krows = pl.ds(pl.multiple_of(kb * QB, QB), QB)
                lb, lk, valid = _sb_tile(cfg, q, k_ref[krows, :], kb, qb)
                rem = _dot_split(lk, u_after) + c
                a = jnp.where(valid, jnp.exp(lb + rem), 0.0)
                o_acc = o_acc + _dot(a.astype(BF16), v_ref[krows, :])
                return o_acc, c + jnp.sum(lk, axis=1, keepdims=True)

            o_acc, c = lax.fori_loop(0, qb + 1, k_loop, (jnp.zeros((QB, DH), F32), jnp.zeros((QB, 1), F32)))
            o_ref[rows, :] = o_acc
            t_ref[rows, :] = jnp.broadcast_to(c, (QB, DH))
            return 0

        lax.fori_loop(0, S // QB, q_loop, 0)

    def spec(i):
        return pl.BlockSpec((None, S, DH), lambda h: (i, 0, h))

    return pl.pallas_call(
        body, name="sb_fwd", grid=(cfg.HSB,),
        in_specs=[spec(0), spec(1), spec(2)],
        out_specs=[pl.BlockSpec((S, DH), lambda h: (0, h))] * 2,
        out_shape=[jax.ShapeDtypeStruct((S, cfg.DSB), F32)] * 2,
        compiler_params=_cp(cfg, ("parallel",)),
    )(qkv3, qkv3, qkv3)


def sb_bwd(cfg, qkv3, do_sb, tsum):
    S, QB, DH = cfg.S, cfg.SBT, cfg.DH
    scale = DH ** -0.5

    def body(q_ref, k_ref, v_ref, do_ref, t_ref, d_ref, dk_acc, dv_acc):
        dk_acc[...] = jnp.zeros_like(dk_acc)
        dv_acc[...] = jnp.zeros_like(dv_acc)
        row = lax.broadcasted_iota(jnp.int32, (QB, QB), 0)
        col = lax.broadcasted_iota(jnp.int32, (QB, QB), 1)
        u_upto = (row <= col).astype(BF16)
        u_before = (row < col).astype(BF16)

        def q_loop(qb, _):
            rows = pl.ds(pl.multiple_of(qb * QB, QB), QB)
            q = q_ref[rows, :]
            do = do_ref[rows, :]
            total = t_ref[rows, 0:1]

            def k_loop(kb, carry):
                dq_acc, pc, gc = carry
                krows = pl.ds(pl.multiple_of(kb * QB, QB), QB)
                k = k_ref[krows, :]
                v = v_ref[krows, :]
                lb, lk, valid = _sb_tile(cfg, q, k, kb, qb)
                rem = total - pc - _dot_split(lk, u_upto)
                a = jnp.where(valid, jnp.exp(lb + rem), 0.0)
                g = a * _dot_nt(do, v)
                dv_acc[krows, :] += _dot_tn(a.astype(BF16), do)
                cum = gc + _dot(g.astype(BF16), u_before)
                sig = jnp.exp(lb)
                dz = (jnp.where(valid, g * (1.0 - sig) - cum * sig, 0.0) * scale).astype(BF16)
                dq_acc = dq_acc + _dot(dz, k)
                dk_acc[krows, :] += _dot_tn(dz, q)
                return dq_acc, pc + jnp.sum(lk, axis=1, keepdims=True), gc + jnp.sum(g, axis=1, keepdims=True)

            z1 = jnp.zeros((QB, 1), F32)
            dq_acc, _, _ = lax.fori_loop(0, qb + 1, k_loop, (jnp.zeros((QB, DH), F32), z1, z1))
            d_ref[0, rows, :] = dq_acc.astype(BF16)
            return 0

        lax.fori_loop(0, S // QB, q_loop, 0)
        d_ref[1, :, :] = dk_acc[...].astype(BF16)
        d_ref[2, :, :] = dv_acc[...].astype(BF16)

    def spec(i):
        return pl.BlockSpec((None, S, DH), lambda h: (i, 0, h))

    return pl.pallas_call(
        body, name="sb_bwd", grid=(cfg.HSB,),
        in_specs=[spec(0), spec(1), spec(2), pl.BlockSpec((S, DH), lambda h: (0, h)),
                  pl.BlockSpec((S, DH), lambda h: (0, h))],
        out_specs=pl.BlockSpec((3, S, DH), lambda h: (0, 0, h)),
        out_shape=jax.ShapeDtypeStruct((3, S, cfg.DSB), BF16),
        scratch_shapes=[pltpu.VMEM((S, DH), F32), pltpu.VMEM((S, DH), F32)],
        compiler_params=_cp(cfg, ("parallel",)),
    )(qkv3, qkv3, qkv3, do_sb, tsum)


def _band_mask(cfg, n, n_back):
    QB = cfg.QB
    qi = lax.broadcasted_iota(jnp.int32, (QB, 2 * QB), 0)
    kj = lax.broadcasted_iota(jnp.int32, (QB, 2 * QB), 1)
    dist = QB + qi - kj
    return (dist >= 0) & (dist <= n_back) & jnp.logical_or(n > 0, kj >= QB)


def _sub_rows(start, n, dil):
    if dil > 1:
        return pl.ds(start, n, stride=dil)
    return pl.ds(start if isinstance(start, int) else pl.multiple_of(start, 8), n)


def _for_residues(dil, fn):
    if dil == 1:
        fn(0, 0)
    else:
        lax.fori_loop(0, dil, fn, 0)


def _lane_value(x):
    return jnp.max(x, axis=1, keepdims=True)


def dil_fwd(cfg, qkv3):
    S, QB, DH = cfg.S, cfg.QB, cfg.DH
    scale = DH ** -0.5
    nb = len(cfg.branches)
    mix_rows = min(256, S)

    def body(q_ref, k_ref, v_ref, o_ref, lt_ref, qf, kf, vf, kp, vp, *obl):
        obs, lbs = obl[:nb], obl[nb:]
        qf[...] = q_ref[...].astype(F32)
        kf[...] = k_ref[...].astype(F32)
        vf[...] = v_ref[...].astype(F32)
        kp[pl.ds(0, QB), :] = jnp.zeros((QB, DH), BF16)
        vp[pl.ds(0, QB), :] = jnp.zeros((QB, DH), BF16)
        for b, (window, dil) in enumerate(cfg.branches):
            L, n_back = S // dil, window // dil
            assert n_back <= QB and L % QB == 0

            def residue(r, _, b=b, dil=dil, L=L, n_back=n_back):
                kp[pl.ds(QB, L), :] = kf[_sub_rows(r, L, dil), :].astype(BF16)
                vp[pl.ds(QB, L), :] = vf[_sub_rows(r, L, dil), :].astype(BF16)

                def block(n, _):
                    rows = _sub_rows(r + n * (QB * dil), QB, dil)
                    band = pl.ds(pl.multiple_of(n * QB, QB), 2 * QB)
                    s = _dot_nt(qf[rows, :].astype(BF16), kp[band, :]) * scale
                    s = jnp.where(_band_mask(cfg, n, n_back), s, NEG_BIG)
                    m = jnp.max(s, axis=1, keepdims=True)
                    p = jnp.exp(s - m)
                    den = jnp.sum(p, axis=1, keepdims=True)
                    obs[b][rows, :] = _dot(p.astype(BF16), vp[band, :]) / den
                    lbs[b][rows, :] = jnp.broadcast_to(m + jnp.log(den), (QB, DH))
                    return 0

                lax.fori_loop(0, L // QB, block, 0)
                return 0

            _for_residues(dil, residue)

        def mix(i, _):
            rows = pl.ds(pl.multiple_of(i * mix_rows, mix_rows), mix_rows)
            ls = [r[rows, :] for r in lbs]
            m = functools.reduce(jnp.maximum, ls)
            es = [jnp.exp(l - m) for l in ls]
            tot = functools.reduce(jnp.add, es)
            o_ref[rows, :] = functools.reduce(jnp.add, [(e / tot) * r[rows, :] for e, r in zip(es, obs)])
            lt_ref[rows, :] = m + jnp.log(tot)
            return 0

        lax.fori_loop(0, S // mix_rows, mix, 0)

    def spec(i):
        return pl.BlockSpec((None, S, DH), lambda h: (i, 0, h))

    o_spec = pl.BlockSpec((S, DH), lambda h: (0, h))
    return pl.pallas_call(
        body, name="dil_fwd", grid=(cfg.HDL,),
        in_specs=[spec(3), spec(4), spec(5)], out_specs=[o_spec, o_spec],
        out_shape=[jax.ShapeDtypeStruct((S, cfg.DDL), F32)] * 2,
        scratch_shapes=[pltpu.VMEM((S, DH), F32)] * 3 + [pltpu.VMEM((S + QB, DH), BF16)] * 2
        + [pltpu.VMEM((S, DH), F32)] * (2 * nb),
        compiler_params=_cp(cfg, ("parallel",)),
    )(qkv3, qkv3, qkv3)


def dil_bwd(cfg, qkv3, do_dl, delta, lse_tot, cos2, sin2):
    S, QB, DH = cfg.S, cfg.QB, cfg.DH
    scale = DH ** -0.5
    out_rows = min(256, S)

    def body(q_ref, k_ref, v_ref, do_ref, dl_ref, lt_ref, c_ref, s_ref, d_ref,
             qf, kf, vf, dof, kp, vp, dkp, dvp, dqn, dkn, dvn):
        qf[...] = q_ref[...].astype(F32)
        kf[...] = k_ref[...].astype(F32)
        vf[...] = v_ref[...].astype(F32)
        dof[...] = do_ref[...].astype(F32)
        for acc in (dqn, dkn, dvn):
            acc[...] = jnp.zeros_like(acc)
        kp[pl.ds(0, QB), :] = jnp.zeros((QB, DH), BF16)
        vp[pl.ds(0, QB), :] = jnp.zeros((QB, DH), BF16)
        for window, dil in cfg.branches:
            L, n_back = S // dil, window // dil

            def residue(r, _, dil=dil, L=L, n_back=n_back):
                sub = _sub_rows(r, L, dil)
                kp[pl.ds(QB, L), :] = kf[sub, :].astype(BF16)
                vp[pl.ds(QB, L), :] = vf[sub, :].astype(BF16)
                dkp[pl.ds(0, QB + L), :] = jnp.zeros((QB + L, DH), F32)
                dvp[pl.ds(0, QB + L), :] = jnp.zeros((QB + L, DH), F32)

                def block(n, _):
                    rows = _sub_rows(r + n * (QB * dil), QB, dil)
                    band = pl.ds(pl.multiple_of(n * QB, QB), 2 * QB)
                    q = qf[rows, :].astype(BF16)
                    do = dof[rows, :].astype(BF16)
                    kb = kp[band, :]
                    s = _dot_nt(q, kb) * scale
                    s = jnp.where(_band_mask(cfg, n, n_back), s, NEG_BIG)
                    p = jnp.exp(s - _lane_value(lt_ref[rows, :]))
                    ds = (p * (_dot_nt(do, vp[band, :]) - _lane_value(dl_ref[rows, :])) * scale).astype(BF16)
                    dqn[rows, :] += _dot(ds, kb)
                    dkp[band, :] += _dot_tn(ds, q)
                    dvp[band, :] += _dot_tn(p.astype(BF16), do)
                    return 0

                lax.fori_loop(0, L // QB, block, 0)
                dkn[sub, :] += dkp[pl.ds(QB, L), :]
                dvn[sub, :] += dvp[pl.ds(QB, L), :]
                return 0

            _for_residues(dil, residue)

        def finish(i, _):
            rows = pl.ds(pl.multiple_of(i * out_rows, out_rows), out_rows)
            c, sn = c_ref[rows, :], s_ref[rows, :]
            for j, acc in enumerate((dqn, dkn)):
                d = acc[rows, :]
                d_ref[j, rows, :] = (d * c + pltpu.roll(d * sn, DH // 2, 1)).astype(BF16)
            d_ref[2, rows, :] = dvn[rows, :].astype(BF16)
            return 0

        lax.fori_loop(0, S // out_rows, finish, 0)

    def spec(i):
        return pl.BlockSpec((None, S, DH), lambda h: (i, 0, h))

    hd = pl.BlockSpec((S, DH), lambda h: (0, h))
    tab = pl.BlockSpec((S, DH), lambda h: (0, 0))
    return pl.pallas_call(
        body, name="dil_bwd", grid=(cfg.HDL,),
        in_specs=[spec(3), spec(4), spec(5), hd, hd, hd, tab, tab],
        out_specs=pl.BlockSpec((3, S, DH), lambda h: (0, 0, h)),
        out_shape=jax.ShapeDtypeStruct((3, S, cfg.DDL), BF16),
        scratch_shapes=[pltpu.VMEM((S, DH), F32)] * 4 + [pltpu.VMEM((S + QB, DH), BF16)] * 2
        + [pltpu.VMEM((S + QB, DH), F32)] * 2 + [pltpu.VMEM((S, DH), F32)] * 3,
        compiler_params=_cp(cfg, ("parallel",)),
    )(qkv3, qkv3, qkv3, do_dl, delta, lse_tot, cos2, sin2)


def combine_fwd(cfg, o_sb, o_dl, g_sb, g_dl):
    S, DH = cfg.S, cfg.DH

    def head_norm(o, g):
        return o * lax.rsqrt(jnp.mean(o * o, axis=-1, keepdims=True) + RMS_EPS) * g

    def body(osb_ref, odl_ref, gsb_ref, gdl_ref, mix_ref):
        for h in range(cfg.HSB):
            c = slice(h * DH, (h + 1) * DH)
            mix_ref[:, c] = head_norm(osb_ref[:, c], gsb_ref[:, c]).astype(BF16)
        for h in range(cfg.HDL):
            c = slice(h * DH, (h + 1) * DH)
            mix_ref[:, cfg.DSB + h * DH:cfg.DSB + (h + 1) * DH] = head_norm(odl_ref[:, c], gdl_ref[:, c]).astype(BF16)

    return pl.pallas_call(
        body, name="combine_fwd", grid=(S // cfg.TM,),
        in_specs=[_row(cfg, cfg.DSB), _row(cfg, cfg.DDL), _vec(cfg.DSB), _vec(cfg.DDL)],
        out_specs=_row(cfg, cfg.DMIX), out_shape=jax.ShapeDtypeStruct((S, cfg.DMIX), BF16),
        compiler_params=_cp(cfg, ("parallel",)),
    )(o_sb, o_dl, g_sb, g_dl)


def combine_bwd(cfg, dmix, o_sb, o_dl, g_sb, g_dl):
    S, DH = cfg.S, cfg.DH

    def body(dm_ref, osb_ref, odl_ref, gsb_ref, gdl_ref, dsb_ref, ddl_ref, dl_ref, dgsb_ref, dgdl_ref):
        @pl.when(pl.program_id(0) == 0)
        def _():
            dgsb_ref[...] = jnp.zeros_like(dgsb_ref)
            dgdl_ref[...] = jnp.zeros_like(dgdl_ref)

        for h in range(cfg.HSB):
            c = slice(h * DH, (h + 1) * DH)
            dx, dgx = _rms_bwd(dm_ref[:, c], osb_ref[:, c], gsb_ref[:, c])
            dsb_ref[:, c] = dx.astype(BF16)
            dgsb_ref[:, c] += jnp.sum(dgx, axis=0, keepdims=True)
        for h in range(cfg.HDL):
            c = slice(h * DH, (h + 1) * DH)
            o = odl_ref[:, c]
            dx, dgx = _rms_bwd(dm_ref[:, cfg.DSB + h * DH:cfg.DSB + (h + 1) * DH], o, gdl_ref[:, c])
            ddl_ref[:, c] = dx.astype(BF16)
            dl_ref[:, c] = jnp.broadcast_to(jnp.sum(dx * o, axis=-1, keepdims=True), dx.shape)
            dgdl_ref[:, c] += jnp.sum(dgx, axis=0, keepdims=True)

    return pl.pallas_call(
        body, name="combine_bwd", grid=(S // cfg.TM,),
        in_specs=[_row(cfg, cfg.DMIX), _row(cfg, cfg.DSB), _row(cfg, cfg.DDL), _vec(cfg.DSB), _vec(cfg.DDL)],
        out_specs=[_row(cfg, cfg.DSB), _row(cfg, cfg.DDL), _row(cfg, cfg.DDL), _vec(cfg.DSB), _vec(cfg.DDL)],
        out_shape=[jax.ShapeDtypeStruct((S, cfg.DSB), BF16), jax.ShapeDtypeStruct((S, cfg.DDL), BF16),
                   jax.ShapeDtypeStruct((S, cfg.DDL), F32), jax.ShapeDtypeStruct((1, cfg.DSB), F32),
                   jax.ShapeDtypeStruct((1, cfg.DDL), F32)],
        compiler_params=_cp(cfg, ("arbitrary",)),
    )(dmix, o_sb, o_dl, g_sb, g_dl)


def _shift_rows(u, j):
    row = lax.broadcasted_iota(jnp.int32, u.shape, 0)
    return jnp.where(row >= j, pltpu.roll(u, j, 0), 0.0)


def _shift_rows_up(u, j):
    n = u.shape[0]
    row = lax.broadcasted_iota(jnp.int32, u.shape, 0)
    return jnp.where(row < n - j, pltpu.roll(u, n - j, 0), 0.0)


def _conv(u, cw, cb):
    return u * cw[2:3, :] + _shift_rows(u, 1) * cw[1:2, :] + _shift_rows(u, 2) * cw[0:1, :] + cb


def ffn_fwd(cfg, h2, w_up, conv_w, conv_b):
    S, D = h2.shape
    tn, nt = cfg.TNF, cfg.FFP // cfg.TNF

    def body(h_ref, wg_ref, wv_ref, cwg_ref, cwv_ref, cbg_ref, cbv_ref, u_ref, y_ref):
        h = h_ref[...]
        ug = _dot(h, wg_ref[...])
        uv = _dot(h, wv_ref[...])
        u_ref[0] = ug
        u_ref[1] = uv
        gl, _ = _gelu(_conv(ug, cwg_ref[...], cbg_ref[...]))
        y_ref[...] = (gl * _conv(uv, cwv_ref[...], cbv_ref[...])).astype(BF16)

    return pl.pallas_call(
        body, name="ffn_fwd", grid=(nt,),
        in_specs=[pl.BlockSpec((S, D), lambda n: (0, 0)),
                  pl.BlockSpec((D, tn), lambda n: (0, n)), pl.BlockSpec((D, tn), lambda n: (0, n + nt)),
                  pl.BlockSpec((3, tn), lambda n: (0, n)), pl.BlockSpec((3, tn), lambda n: (0, n + nt)),
                  pl.BlockSpec((1, tn), lambda n: (0, n)), pl.BlockSpec((1, tn), lambda n: (0, n + nt))],
        out_specs=[pl.BlockSpec((2, S, tn), lambda n: (0, 0, n)), pl.BlockSpec((S, tn), lambda n: (0, n))],
        out_shape=[jax.ShapeDtypeStruct((2, S, cfg.FFP), F32), jax.ShapeDtypeStruct((S, cfg.FFP), BF16)],
        compiler_params=_cp(cfg, ("parallel",)),
    )(h2, w_up, w_up, conv_w, conv_w, conv_b, conv_b)


def ffn_bwd(cfg, df, w_down, u, conv_w, conv_b):
    S, D = df.shape
    tn, nt = cfg.TNF, cfg.FFP // cfg.TNF

    def conv_bwd(dc, uu, cw):
        du = dc * cw[2:3, :] + _shift_rows_up(dc, 1) * cw[1:2, :] + _shift_rows_up(dc, 2) * cw[0:1, :]
        dws = [jnp.sum(dc * _shift_rows(uu, 2), axis=0, keepdims=True),
               jnp.sum(dc * _shift_rows(uu, 1), axis=0, keepdims=True),
               jnp.sum(dc * uu, axis=0, keepdims=True)]
        return du, dws, jnp.sum(dc, axis=0, keepdims=True)

    def body(df_ref, wd_ref, u_ref, cwg_ref, cwv_ref, cbg_ref, cbv_ref, du_ref, dwd_ref, dcw_ref, dcb_ref):
        dfv = df_ref[...]
        dy = _dot_nt(dfv, wd_ref[...])
        ug, uv = u_ref[0], u_ref[1]
        cwg, cwv = cwg_ref[...], cwv_ref[...]
        cg = _conv(ug, cwg, cbg_ref[...])
        cv = _conv(uv, cwv, cbv_ref[...])
        gl, t = _gelu(cg)
        dwd_ref[...] = _dot_tn((gl * cv).astype(BF16), dfv).astype(BF16)
        dug, dwg, dbg = conv_bwd(dy * cv * _gelu_grad(cg, t), ug, cwg)
        duv, dwv, dbv = conv_bwd(dy * gl, uv, cwv)
        du_ref[0] = dug.astype(BF16)
        du_ref[1] = duv.astype(BF16)
        for j in range(3):
            dcw_ref[0, j:j + 1, :] = dwg[j]
            dcw_ref[1, j:j + 1, :] = dwv[j]
        dcb_ref[0] = dbg
        dcb_ref[1] = dbv

    return pl.pallas_call(
        body, name="ffn_bwd", grid=(nt,),
        in_specs=[pl.BlockSpec((S, D), lambda n: (0, 0)), pl.BlockSpec((tn, D), lambda n: (n, 0)),
                  pl.BlockSpec((2, S, tn), lambda n: (0, 0, n)),
                  pl.BlockSpec((3, tn), lambda n: (0, n)), pl.BlockSpec((3, tn), lambda n: (0, n + nt)),
                  pl.BlockSpec((1, tn), lambda n: (0, n)), pl.BlockSpec((1, tn), lambda n: (0, n + nt))],
        out_specs=[pl.BlockSpec((2, S, tn), lambda n: (0, 0, n)), pl.BlockSpec((tn, D), lambda n: (n, 0)),
                   pl.BlockSpec((2, 3, tn), lambda n: (0, 0, n)), pl.BlockSpec((2, 1, tn), lambda n: (0, 0, n))],
        out_shape=[jax.ShapeDtypeStruct((2, S, cfg.FFP), BF16), jax.ShapeDtypeStruct((cfg.FFP, D), BF16),
                   jax.ShapeDtypeStruct((2, 3, cfg.FFP), F32), jax.ShapeDtypeStruct((2, 1, cfg.FFP), F32)],
        compiler_params=_cp(cfg, ("parallel",)),
    )(df, w_down, u, conv_w, conv_w, conv_b, conv_b)


def rope_tables(cfg):
    inv_freq = ROPE_THETA ** (-jnp.arange(0, cfg.DH, 2, dtype=F32) / cfg.DH)
    ang = jnp.arange(cfg.S, dtype=F32)[:, None] * inv_freq[None, :]
    cos, sin = jnp.cos(ang), jnp.sin(ang)
    return jnp.concatenate([cos, cos], axis=1), jnp.concatenate([-sin, sin], axis=1)


class LocalWeights:
    def __init__(self, w_in, w_out, w_up, conv_w, w_down):
        self.w = (w_in, w_out, w_up, conv_w, w_down)
        self.grads = {}

    def weights_first(self):
        return self.w[0], self.w[3]

    def start_rest(self):
        return None

    def weights_rest(self, after):
        return self.w[1], self.w[2], self.w[4]

    def reduce_start(self, grads):
        self.grads.update(grads)
        return None

    def reduce_wait(self, names, after):
        pass


def _after(a, token):
    return a if token is None else a + token[0, 0].astype(a.dtype)


def local_step(cfg, comm, x, target, g1, g2, g3, g4, g_sb, g_dl, conv_b):
    S, D = cfg.S, cfg.D
    cos2, sin2 = rope_tables(cfg)
    full = lambda r, c: pl.BlockSpec((r, c), lambda j, k: (0, 0))

    w_in, conv_w = comm.weights_first()
    h1 = rms_fwd(cfg, x, g1)
    qkv3 = qkv_proj(cfg, h1, w_in, _after(cos2, comm.start_rest()), sin2)
    o_sb, tsum = sb_fwd(cfg, qkv3)
    o_dl, lse_tot = dil_fwd(cfg, qkv3)
    mixed = combine_fwd(cfg, o_sb, o_dl, g_sb, g_dl)
    w_out, w_up, w_down = comm.weights_rest(after=mixed)
    tn = cfg.TN
    mo = _mm(cfg, "mix_out", mixed, w_out, nt=False, grid=(D // tn, 1),
             a_spec=full(S, cfg.DMIX), b_spec=pl.BlockSpec((cfg.DMIX, tn), lambda j, k: (0, j)),
             o_spec=pl.BlockSpec((S, tn), lambda j, k: (0, j)),
             out_shape=jax.ShapeDtypeStruct((S, D), F32), acc_shape=(8, LANE))
    x1, h2 = mid_fwd(cfg, x, mo, g2, g3)
    u, y = ffn_fwd(cfg, h2, w_up, conv_w, conv_b)
    tk = cfg.FFP // 4
    f = _mm(cfg, "ffn_down", y, w_down, nt=False, grid=(D // tn, 4),
            a_spec=pl.BlockSpec((S, tk), lambda j, k: (0, k)), b_spec=pl.BlockSpec((tk, tn), lambda j, k: (k, j)),
            o_spec=pl.BlockSpec((S, tn), lambda j, k: (0, j)),
            out_shape=jax.ShapeDtypeStruct((S, D), F32), acc_shape=(S, tn))
    dout, df, dg4, loss = final_fwd_bwd(cfg, x1, f, g4, target)

    du, dw_down, dconv_w, dconv_b = ffn_bwd(cfg, df, w_down, u, conv_w, conv_b)
    kt = cfg.FFP // tk
    dh2 = _mm(cfg, "d_h2", du, w_up, nt=True, grid=(D // tn, 2 * kt),
              a_spec=pl.BlockSpec((None, S, tk), lambda j, k: (k // kt, 0, k % kt)),
              b_spec=pl.BlockSpec((tn, tk), lambda j, k: (j, k)),
              o_spec=pl.BlockSpec((S, tn), lambda j, k: (0, j)),
              out_shape=jax.ShapeDtypeStruct((S, D), F32), acc_shape=(S, tn))
    nf = cfg.FFP // tn if cfg.FFP % tn == 0 else None
    tnu = tn if nf else cfg.TNF
    nf = cfg.FFP // tnu
    dw_up = _mm_tn(cfg, "d_w_up", h2, du, grid=(2 * nf,),
                   a_spec=pl.BlockSpec((S, D), lambda j: (0, 0)),
                   b_spec=pl.BlockSpec((None, S, tnu), lambda j: (j // nf, 0, j % nf)),
                   o_spec=pl.BlockSpec((D, tnu), lambda j: (0, j)),
                   out_shape=jax.ShapeDtypeStruct((D, cfg.FF2P), BF16))
    dx1, dmo, dg3, dg2 = mid_bwd(cfg, dh2, x1, g3, dout, mo, g2)

    dmix = _mm(cfg, "d_mixed", dmo, w_out, nt=True, grid=(cfg.DMIX // tn, 1),
               a_spec=full(S, D), b_spec=pl.BlockSpec((tn, D), lambda j, k: (j, 0)),
               o_spec=pl.BlockSpec((S, tn), lambda j, k: (0, j)),
               out_shape=jax.ShapeDtypeStruct((S, cfg.DMIX), F32), acc_shape=(8, LANE))
    dw_out = _mm_tn(cfg, "d_w_out", mixed, dmo, grid=(D // tn,),
                    a_spec=pl.BlockSpec((S, cfg.DMIX), lambda j: (0, 0)),
                    b_spec=pl.BlockSpec((S, tn), lambda j: (0, j)),
                    o_spec=pl.BlockSpec((cfg.DMIX, tn), lambda j: (0, j)),
                    out_shape=jax.ShapeDtypeStruct((cfg.DMIX, D), BF16))
    token = comm.reduce_start(dict(w_out=dw_out, w_up=dw_up, w_down=dw_down))
    do_sb, do_dl, delta, dg_sb, dg_dl = combine_bwd(cfg, dmix, o_sb, o_dl, _after(g_sb, token), g_dl)
    d_sb3 = sb_bwd(cfg, qkv3, do_sb, tsum)
    d_dl3 = dil_bwd(cfg, qkv3, do_dl, delta, lse_tot, cos2, sin2)
    comm.reduce_wait(("w_out", "w_up", "w_down"), after=d_dl3)
    dqkv3 = jnp.concatenate([d_sb3, d_dl3], axis=0)
    tkq = min(tn, cfg.DSB)
    kq = cfg.DSB // tkq
    dw_in = _mm_tn(cfg, "d_w_in", h1, dqkv3, grid=(6 * kq,),
                   a_spec=pl.BlockSpec((S, D), lambda j: (0, 0)),
                   b_spec=pl.BlockSpec((None, S, tkq), lambda j: (j // kq, 0, j % kq)),
                   o_spec=pl.BlockSpec((D, tkq), lambda j: (0, j)),
                   out_shape=jax.ShapeDtypeStruct((D, 6 * cfg.DSB), BF16))
    token = comm.reduce_start(dict(w_in=dw_in))
    dh1 = _mm(cfg, "d_h1", dqkv3, w_in, nt=True, grid=(D // tn, 6 * kq),
              a_spec=pl.BlockSpec((None, S, tkq), lambda j, k: (k // kq, 0, k % kq)),
              b_spec=pl.BlockSpec((tn, tkq), lambda j, k: (j, k)),
              o_spec=pl.BlockSpec((S, tn), lambda j, k: (0, j)),
              out_shape=jax.ShapeDtypeStruct((S, D), F32), acc_shape=(S, tn), dep=token)
    grad_x, dg1 = first_bwd(cfg, dh1, x, g1, dx1)
    comm.reduce_wait(("w_in",), after=grad_x)
    small = dict(loss=loss, g1=dg1, g2=dg2, g3=dg3, g4=dg4, g_sb=dg_sb, g_dl=dg_dl,
                 conv_b=dconv_b.reshape(1, cfg.FF2P), conv_w=dconv_w.transpose(1, 0, 2).reshape(3, cfg.FF2P))
    return grad_x, small


ANY = pl.BlockSpec(memory_space=pl.ANY)


def _me():
    return lax.axis_index("x"), lax.axis_index("y"), lax.axis_index("c")


def _other_chips(x, y):
    return [(1 - x, y), (x, 1 - y), (1 - x, 1 - y)]


def pad_conv_w(cfg, conv_w):
    r, c = conv_w.shape

    def body(w_ref, o_ref):
        o_ref[:, :c] = w_ref[...]
        o_ref[:, c:] = jnp.zeros((r, cfg.FSHP - c), F32)

    return pl.pallas_call(body, name="pad_conv_w", out_shape=jax.ShapeDtypeStruct((r, cfg.FSHP), F32))(conv_w)


def _tile2(r, c):
    return (256, c) if r % 256 == 0 else (r, 512 if c % 512 == 0 else c)


def cast_into(cfg, name, w, pos):
    r, c = w.shape
    _, nr, _, nc = _slab(cfg, name, 0)
    tm, tc = _tile2(r, c)
    wc = nc if tc == c else tc
    assert nr == r and (nc == c or tc == c)

    def body(pos_ref, w_ref, full_ref, scr, sem):
        scr[:, :tc] = w_ref[...].astype(BF16)
        if wc > tc:
            scr[:, tc:] = jnp.zeros((tm, wc - tc), BF16)
        r0, _, c0, _ = _slab(cfg, name, pos_ref[0])
        rows = pl.ds(pl.multiple_of(r0 + pl.program_id(0) * tm, 16), tm)
        cols = pl.ds(pl.multiple_of(c0 + pl.program_id(1) * tc, LANE), wc)
        cp = pltpu.make_async_copy(scr, full_ref.at[rows, cols], sem)
        cp.start()
        cp.wait()

    return pl.pallas_call(
        body, name=f"cast_{name}",
        grid_spec=pltpu.PrefetchScalarGridSpec(
            num_scalar_prefetch=1, grid=(r // tm, c // tc),
            in_specs=[pl.BlockSpec((tm, tc), lambda i, j, p: (i, j))], out_specs=ANY,
            scratch_shapes=[pltpu.VMEM((tm, wc), BF16), pltpu.SemaphoreType.DMA]),
        out_shape=jax.ShapeDtypeStruct(_full_shape(cfg, name), BF16),
        compiler_params=_cp(cfg, ("arbitrary", "arbitrary")),
    )(pos, w)


HBM = pl.BlockSpec(memory_space=pltpu.HBM)
SEM = pl.BlockSpec(memory_space=pltpu.SEMAPHORE)
TOKEN = pl.BlockSpec(memory_space=pltpu.VMEM)
EFFECT = pltpu.SideEffectType.DATAFLOW_SIDE_EFFECTING


def _slab(cfg, name, k):
    D = cfg.D
    if name == "w_in":
        cin = 6 * cfg.DSB // N_CHIPS
        return 0, D, k * cin, cin
    if name == "w_out":
        rout = cfg.DMIX // N_CHIPS
        return k * rout, rout, 0, D
    if name == "w_up":
        return 0, D, k * cfg.FSHP, cfg.FSHP
    rdn = cfg.FSH // 2
    return (k // 2) * cfg.FSHP + (k % 2) * rdn, rdn, 0, D


def _full_shape(cfg, name):
    return dict(w_in=(cfg.D, 6 * cfg.DSB), w_out=(cfg.DMIX, cfg.D), w_up=(cfg.D, cfg.FF2P), w_down=(cfg.FFP, cfg.D))[name]


def _half(cfg, name, ref, k, h):
    r0, nr, c0, nc = _slab(cfg, name, k)
    return ref.at[pl.ds(r0 + h * (nr // 2), nr // 2), pl.ds(c0, nc)]


def _rows_half(ref, h):
    nr = ref.shape[0] // 2
    return ref.at[pl.ds(h * nr, nr), :]


def _remote(src, dst, send_sem, recv_sem, dev):
    return pltpu.make_async_remote_copy(src_ref=src, dst_ref=dst, send_sem=send_sem, recv_sem=recv_sem,
                                        device_id=dev, device_id_type=MESH)


def gather_first(cfg, g_in, sh_cw):
    def body(in_ref, cw_ref, g_in, g_cw, token, ssem, rsem, fssem, frsem, lsem):
        x, y, c = _me()
        me, sib = 2 * x + y, (x, y, 1 - c)
        cw_slot = lambda k: g_cw.at[:, pl.ds(k * cfg.FSHP, cfg.FSHP)]
        local = [pltpu.make_async_copy(cw_ref, cw_slot(me), lsem.at[0])]
        sends = []
        for j, (px, py) in enumerate(_other_chips(x, y)):
            mine = _half(cfg, "w_in", g_in, me, c)
            sends.append(_remote(mine, mine, ssem.at[j], rsem.at[j], (px, py, c)))
            sends.append(_remote(cw_ref, cw_slot(me), ssem.at[3 + j], rsem.at[3 + j], (px, py, c)))
        for cp in local + sends:
            cp.start()
        for j, (px, py) in enumerate(_other_chips(x, y)):
            k = 2 * px + py
            landed = _half(cfg, "w_in", g_in, k, c)
            _remote(landed, landed, ssem.at[j], rsem.at[j], (px, py, c)).wait_recv()
            fwd = _remote(landed, landed, fssem.at[j], frsem.at[j], sib)
            fwd.start()
            sends.append(fwd)
        for j, (px, py) in enumerate(_other_chips(x, y)):
            k = 2 * px + py
            passed = _half(cfg, "w_in", g_in, k, 1 - c)
            _remote(passed, passed, fssem.at[j], frsem.at[j], sib).wait_recv()
            _remote(cw_ref, cw_slot(k), ssem.at[3 + j], rsem.at[3 + j], (px, py, c)).wait_recv()
        for cp in sends:
            cp.wait_send()
        for cp in local:
            cp.wait()
        token[...] = jnp.zeros_like(token)

    return pl.pallas_call(
        body, name="gather_first", in_specs=[ANY, ANY], out_specs=[ANY, ANY, TOKEN],
        out_shape=[jax.ShapeDtypeStruct(_full_shape(cfg, "w_in"), BF16), jax.ShapeDtypeStruct((3, cfg.FF2P), F32),
                   jax.ShapeDtypeStruct((8, LANE), F32)],
        input_output_aliases={0: 0},
        scratch_shapes=[pltpu.SemaphoreType.DMA((6,)), pltpu.SemaphoreType.DMA((6,)), pltpu.SemaphoreType.DMA((3,)),
                        pltpu.SemaphoreType.DMA((3,)), pltpu.SemaphoreType.DMA((1,))],
    )(g_in, sh_cw)


REST = ("w_out", "w_up", "w_down")


def _hbm(a):
    return pltpu.with_memory_space_constraint(a, pltpu.HBM)


def gather_start(cfg, fulls, after):
    n = len(REST)

    def body(*refs):
        lands = refs[:n]
        ssem, rsem = refs[n + 1], refs[n + 2]
        token = refs[-1]
        x, y, c = _me()
        me = 2 * x + y
        for i, name in enumerate(REST):
            mine = _half(cfg, name, lands[i], me, c)
            for j, (px, py) in enumerate(_other_chips(x, y)):
                _remote(mine, mine, ssem.at[3 * i + j], rsem.at[3 * i + j], (px, py, c)).start()
        token[...] = jnp.zeros_like(token)

    ops = [_hbm(a) for a in fulls]
    outs = pl.pallas_call(
        body, name="gather_start",
        in_specs=[HBM] * n + [ANY],
        out_specs=[SEM, SEM] + [HBM] * n + [TOKEN],
        out_shape=[pltpu.SemaphoreType.DMA((3 * n,)), pltpu.SemaphoreType.DMA((3 * n,))]
        + [pltpu.HBM(a.shape, a.dtype) for a in ops] + [jax.ShapeDtypeStruct((8, LANE), F32)],
        input_output_aliases={i: 2 + i for i in range(n)},
        compiler_params=pltpu.CompilerParams(has_side_effects=EFFECT),
    )(*ops, after)
    return outs[0], outs[1], outs[2:2 + n], outs[-1]


def gather_wait(cfg, ssem, rsem, lands, after):
    n = len(REST)

    def body(*refs):
        lands_ = refs[:n]
        ssem_, rsem_ = refs[n], refs[n + 1]
        x, y, c = _me()
        me = 2 * x + y
        for i, name in enumerate(REST):
            for j, (px, py) in enumerate(_other_chips(x, y)):
                cp = _remote(_half(cfg, name, lands_[i], me, c), _half(cfg, name, lands_[i], 2 * px + py, c),
                             ssem_.at[3 * i + j], rsem_.at[3 * i + j], (px, py, c))
                cp.wait_send()
                cp.wait_recv()

    return pl.pallas_call(
        body, name="gather_wait",
        in_specs=[HBM] * n + [SEM, SEM, ANY], out_specs=[HBM] * n,
        out_shape=[pltpu.HBM(a.shape, a.dtype) for a in lands],
        input_output_aliases={i: i for i in range(n)},
        compiler_params=pltpu.CompilerParams(has_side_effects=EFFECT),
    )(*lands, ssem, rsem, after)


def gather_finish(cfg, lands):
    n = len(REST)
    rdn = cfg.FSH // 2
    zpad = jnp.zeros((cfg.FSHP - cfg.FSH, cfg.D), BF16)

    def body(*refs):
        z_ref, outs = refs[0], refs[n + 1:2 * n + 1]
        ssem, rsem, lsem = refs[2 * n + 1:]
        x, y, c = _me()
        sib = (x, y, 1 - c)
        local = [pltpu.make_async_copy(z_ref, outs[2].at[pl.ds(h * cfg.FSHP + 2 * rdn, cfg.FSHP - cfg.FSH), :],
                                       lsem.at[h]) for h in range(2)]
        fwds = []
        for i, name in enumerate(REST):
            for j, (px, py) in enumerate(_other_chips(x, y)):
                landed = _half(cfg, name, outs[i], 2 * px + py, c)
                fwds.append(_remote(landed, landed, ssem.at[3 * i + j], rsem.at[3 * i + j], sib))
        for cp in local + fwds:
            cp.start()
        for i, name in enumerate(REST):
            for j, (px, py) in enumerate(_other_chips(x, y)):
                passed = _half(cfg, name, outs[i], 2 * px + py, 1 - c)
                _remote(passed, passed, ssem.at[3 * i + j], rsem.at[3 * i + j], sib).wait_recv()
        for cp in fwds:
            cp.wait_send()
        for cp in local:
            cp.wait()

    return pl.pallas_call(
        body, name="gather_finish", in_specs=[ANY] * (n + 1), out_specs=[ANY] * n,
        out_shape=[jax.ShapeDtypeStruct(a.shape, a.dtype) for a in lands],
        input_output_aliases={1 + i: i for i in range(n)},
        scratch_shapes=[pltpu.SemaphoreType.DMA((3 * n,)), pltpu.SemaphoreType.DMA((3 * n,)),
                        pltpu.SemaphoreType.DMA((2,))],
    )(zpad, *lands)


def pair_send(cfg, grads):
    names = list(grads)
    n = len(names)

    def half_shape(name):
        _, nr, _, nc = _slab(cfg, name, 0)
        return (N_CHIPS, nr // 2, nc)

    def body(*refs):
        srcs, theirs = refs[:n], refs[n:2 * n]
        ssem, rsem = refs[2 * n:]
        x, y, c = _me()
        cps = []
        for i, name in enumerate(names):
            for k in range(N_CHIPS):
                cps.append(_remote(_half(cfg, name, srcs[i], k, 1 - c), theirs[i].at[k],
                                   ssem.at[N_CHIPS * i + k], rsem.at[N_CHIPS * i + k], (x, y, 1 - c)))
        for cp in cps:
            cp.start()
        for cp in cps:
            cp.wait()

    outs = pl.pallas_call(
        body, name="pair_send_" + "_".join(names), in_specs=[ANY] * n, out_specs=[ANY] * n,
        out_shape=[jax.ShapeDtypeStruct(half_shape(name), BF16) for name in names],
        scratch_shapes=[pltpu.SemaphoreType.DMA((N_CHIPS * n,))] * 2,
    )(*[grads[k] for k in names])
    return dict(zip(names, outs))


def pair_sum(cfg, name, grad, theirs, pos):
    _, r, c = theirs.shape
    tm, tc = _tile2(r, c)

    def body(pos_ref, g_ref, t_ref, o_ref, scr, sem):
        k, i, j = pl.program_id(0), pl.program_id(1), pl.program_id(2)
        r0, nr, c0, _ = _slab(cfg, name, k)
        rows = pl.ds(pl.multiple_of(r0 + pos_ref[1] * (nr // 2) + i * tm, 16), tm)
        cols = pl.ds(pl.multiple_of(c0 + j * tc, LANE), tc)
        cp = pltpu.make_async_copy(g_ref.at[rows, cols], scr, sem)
        cp.start()
        cp.wait()
        o_ref[...] = (scr[...].astype(F32) + t_ref[...].astype(F32)).astype(BF16)

    blk = pl.BlockSpec((None, tm, tc), lambda k, i, j, p: (k, i, j))
    return pl.pallas_call(
        body, name=f"pair_sum_{name}",
        grid_spec=pltpu.PrefetchScalarGridSpec(
            num_scalar_prefetch=1, grid=(N_CHIPS, r // tm, c // tc), in_specs=[ANY, blk], out_specs=blk,
            scratch_shapes=[pltpu.VMEM((tm, tc), BF16), pltpu.SemaphoreType.DMA]),
        out_shape=jax.ShapeDtypeStruct(theirs.shape, BF16),
        compiler_params=_cp(cfg, ("arbitrary",) * 3),
    )(pos, grad, theirs)


def scatter_start(cfg, pres, after):
    names = list(pres)
    n = len(names)

    def body(*refs):
        srcs, lands = refs[:n], refs[n:2 * n]
        ssem, rsem = refs[2 * n + 1], refs[2 * n + 2]
        token = refs[-1]
        x, y, c = _me()
        for i in range(n):
            for j, (px, py) in enumerate(_other_chips(x, y)):
                _remote(srcs[i].at[2 * px + py], lands[i].at[j], ssem.at[3 * i + j], rsem.at[3 * i + j], (px, py, c)).start()
        token[...] = jnp.zeros_like(token)

    lands = [lax.empty((3,) + pres[k].shape[1:], BF16) for k in names]
    ops = [_hbm(a) for a in [pres[k] for k in names] + lands]
    outs = pl.pallas_call(
        body, name="scatter_start_" + "_".join(names),
        in_specs=[HBM] * (2 * n) + [ANY],
        out_specs=[SEM, SEM] + [HBM] * (2 * n) + [TOKEN],
        out_shape=[pltpu.SemaphoreType.DMA((3 * n,)), pltpu.SemaphoreType.DMA((3 * n,))]
        + [pltpu.HBM(a.shape, a.dtype) for a in ops] + [jax.ShapeDtypeStruct((8, LANE), F32)],
        input_output_aliases={i: 2 + i for i in range(2 * n)},
        compiler_params=pltpu.CompilerParams(has_side_effects=EFFECT),
    )(*ops, after)
    return outs[0], outs[1], dict(zip(names, outs[2:2 + n])), dict(zip(names, outs[2 + n:2 + 2 * n])), outs[-1]


def scatter_wait(cfg, ssem, rsem, pres, lands, after):
    names = list(pres)
    n = len(names)

    def body(*refs):
        srcs, lands_ = refs[:n], refs[n:2 * n]
        ssem_, rsem_ = refs[2 * n], refs[2 * n + 1]
        x, y, c = _me()
        for i in range(n):
            for j, (px, py) in enumerate(_other_chips(x, y)):
                cp = _remote(srcs[i].at[2 * px + py], lands_[i].at[j], ssem_.at[3 * i + j], rsem_.at[3 * i + j], (px, py, c))
                cp.wait_send()
                cp.wait_recv()

    ops = [pres[k] for k in names] + [lands[k] for k in names]
    outs = pl.pallas_call(
        body, name="scatter_wait_" + "_".join(names),
        in_specs=[HBM] * (2 * n) + [SEM, SEM, ANY], out_specs=[HBM] * (2 * n),
        out_shape=[pltpu.HBM(a.shape, a.dtype) for a in ops],
        input_output_aliases={i: i for i in range(2 * n)},
        compiler_params=pltpu.CompilerParams(has_side_effects=EFFECT),
    )(*ops, ssem, rsem, after)
    return dict(zip(names, outs[:n])), dict(zip(names, outs[n:]))


def sum_landed(cfg, name, pre, land, pos):
    _, r, c = pre.shape
    tm, tc = _tile2(r, c)
    nrt = r // tm

    def body(pos_ref, p_ref, l_ref, o_ref):
        acc = p_ref[...].astype(F32)
        for j in range(3):
            acc = acc + l_ref[j].astype(F32)
        o_ref[...] = acc

    return pl.pallas_call(
        body, name=f"sum_landed_{name}",
        grid_spec=pltpu.PrefetchScalarGridSpec(
            num_scalar_prefetch=1, grid=(nrt, c // tc),
            in_specs=[pl.BlockSpec((None, tm, tc), lambda i, j, p: (p[0], i, j)),
                      pl.BlockSpec((3, tm, tc), lambda i, j, p: (0, i, j))],
            out_specs=pl.BlockSpec((tm, tc), lambda i, j, p: (p[1] * nrt + i, j))),
        out_shape=jax.ShapeDtypeStruct((2 * r, c), F32), compiler_params=_cp(cfg, ("parallel", "parallel")),
    )(pos, pre, land)


def half_swap(cfg, sums):
    names = list(sums)
    n = len(names)

    def body(*refs):
        outs = refs[n:2 * n]
        ssem, rsem = refs[2 * n:]
        x, y, c = _me()
        cps = [_remote(_rows_half(outs[i], c), _rows_half(outs[i], c), ssem.at[i], rsem.at[i], (x, y, 1 - c))
               for i in range(n)]
        for cp in cps:
            cp.start()
        for i in range(n):
            theirs = _rows_half(outs[i], 1 - c)
            _remote(theirs, theirs, ssem.at[i], rsem.at[i], (x, y, 1 - c)).wait_recv()
        for cp in cps:
            cp.wait_send()

    outs = pl.pallas_call(
        body, name="half_swap_" + "_".join(names), in_specs=[ANY] * n, out_specs=[ANY] * n,
        out_shape=[jax.ShapeDtypeStruct(sums[k].shape, F32) for k in names],
        input_output_aliases={i: i for i in range(n)},
        scratch_shapes=[pltpu.SemaphoreType.DMA((n,))] * 2,
    )(*[sums[k] for k in names])
    return dict(zip(names, outs))


class MeshWeights:
    def __init__(self, cfg, w_sh):
        self.cfg = cfg
        self.pos = jnp.stack([2 * lax.axis_index("x") + lax.axis_index("y"), lax.axis_index("c")]).astype(jnp.int32)
        self.full = {k: cast_into(cfg, k, w_sh[k], self.pos) for k in ("w_in",) + REST}
        self.conv_w = pad_conv_w(cfg, w_sh["conv_w"])
        self.inflight = {}
        self.grads = {}

    def weights_first(self):
        w_in, conv_w, self.token = gather_first(self.cfg, self.full["w_in"], self.conv_w)
        return w_in, conv_w

    def start_rest(self):
        out = gather_start(self.cfg, [self.full[k] for k in REST], self.token)
        self.rest = out[:3]
        return out[3]

    def weights_rest(self, after):
        return gather_finish(self.cfg, gather_wait(self.cfg, *self.rest, after))

    def reduce_start(self, grads):
        theirs = pair_send(self.cfg, grads)
        pres = {k: pair_sum(self.cfg, k, grads[k], theirs[k], self.pos) for k in grads}
        out = scatter_start(self.cfg, pres, jnp.zeros((8, LANE), F32))
        self.inflight[tuple(grads)] = out[:4]
        return out[4]

    def reduce_wait(self, names, after):
        cfg = self.cfg
        pres, lands = scatter_wait(cfg, *self.inflight.pop(tuple(names)), after)
        sums = {k: sum_landed(cfg, k, pres[k], lands[k], self.pos) for k in names}
        self.grads.update(half_swap(cfg, sums))


def allreduce_small(cfg, vec):
    R = vec.shape[0]

    def body(v_ref, o_ref, buf, send_sems, recv_sems):
        x, y, c = _me()
        me = 4 * x + 2 * y + c
        buf[me] = v_ref[...]
        sends = []
        for k in range(1, N_DEV):
            px, py, pc = x ^ (k >> 2), y ^ ((k >> 1) & 1), c ^ (k & 1)
            sends.append(pltpu.make_async_remote_copy(
                src_ref=v_ref, dst_ref=buf.at[me], send_sem=send_sems.at[k], recv_sem=recv_sems.at[k],
                device_id=(px, py, pc), device_id_type=MESH))
        for cp in sends:
            cp.start()
        for k in range(1, N_DEV):
            px, py, pc = x ^ (k >> 2), y ^ ((k >> 1) & 1), c ^ (k & 1)
            pltpu.make_async_remote_copy(
                src_ref=v_ref, dst_ref=buf.at[4 * px + 2 * py + pc], send_sem=send_sems.at[k],
                recv_sem=recv_sems.at[k], device_id=(px, py, pc), device_id_type=MESH).wait_recv()
        for cp in sends:
            cp.wait_send()
        acc = buf[0]
        for j in range(1, N_DEV):
            acc = acc + buf[j]
        o_ref[...] = acc

    return pl.pallas_call(
        body, name="allreduce_small",
        in_specs=[pl.BlockSpec(memory_space=pltpu.VMEM)], out_specs=pl.BlockSpec(memory_space=pltpu.VMEM),
        out_shape=jax.ShapeDtypeStruct((R, LANE), F32),
        scratch_shapes=[pltpu.VMEM((N_DEV, R, LANE), F32), pltpu.SemaphoreType.DMA((N_DEV,)),
                        pltpu.SemaphoreType.DMA((N_DEV,))],
    )(vec)


def adamw(cfg, name, w, m, v, g_parts, tile):
    r, c = w.shape
    tm, tc = tile[0] or r, tile[1] or c
    assert tc == c or all(g.shape[1] == c for g in g_parts)
    n = len(g_parts)
    bc1 = 1.0 - ADAM_B1 ** ADAM_STEP
    bc2 = 1.0 - ADAM_B2 ** ADAM_STEP

    def body(*refs):
        w_ref, m_ref, v_ref = refs[:3]
        g_refs = refs[3:3 + n]
        g_out, d_out, m_out, v_out = refs[3 + n:]
        g = g_refs[0][:, :tc]
        for gr in g_refs[1:]:
            g = g + gr[:, :tc]
        m_new = ADAM_B1 * m_ref[...] + (1.0 - ADAM_B1) * g
        v_new = ADAM_B2 * v_ref[...] + (1.0 - ADAM_B2) * jnp.square(g)
        m_hat = m_new / bc1
        v_hat = v_new / bc2
        g_out[...] = g
        d_out[...] = -ADAM_LR * (m_hat / (jnp.sqrt(v_hat) + ADAM_EPS) + ADAM_WD * w_ref[...])
        m_out[...] = m_new
        v_out[...] = v_new

    blk = pl.BlockSpec((tm, tc), lambda i, j: (i, j))
    return pl.pallas_call(
        body, name=f"adamw_{name}", grid=(r // tm, c // tc),
        in_specs=[blk] * 3 + [pl.BlockSpec((tm, tc if tc < c else g.shape[1]), lambda i, j: (i, j)) for g in g_parts],
        out_specs=[blk] * 4, out_shape=[jax.ShapeDtypeStruct((r, c), F32)] * 4,
        compiler_params=_cp(cfg, ("parallel", "parallel")),
    )(w, m, v, *g_parts)


SMALL_ORDER = ("loss", "g1", "g2", "g3", "g4", "g_sb", "g_dl", "conv_b", "conv_w")


def pack_small(small):
    rows = []
    for k in SMALL_ORDER:
        a = small[k].reshape(-1, LANE)
        rows.append(a)
    flat = jnp.concatenate(rows, axis=0)
    pad = (-flat.shape[0]) % 8
    return jnp.pad(flat, ((0, pad), (0, 0))), [r.shape[0] for r in rows]


def unpack_small(red, small, counts):
    out, at = {}, 0
    for k, n in zip(SMALL_ORDER, counts):
        out[k] = red[at:at + n].reshape(small[k].shape)
        at += n
    return out


def pad_ff(cfg, a):
    r = a.shape[0]
    return jnp.pad(a.reshape(r, N_CHIPS, cfg.FSH), ((0, 0), (0, 0), (0, cfg.FSHP - cfg.FSH))).reshape(r, cfg.FF2P)


def step(cfg, x, target, gains, w_sh, conv_b, m_all, v_all):
    chip = 2 * lax.axis_index("x") + lax.axis_index("y")
    comm = MeshWeights(cfg, w_sh)
    grad_x, small = local_step(cfg, comm, x, target, gains["g1"], gains["g2"], gains["g3"], gains["g4"],
                               gains["g_sb"], gains["g_dl"], pad_ff(cfg, conv_b))

    packed, counts = pack_small(small)
    red = unpack_small(allreduce_small(cfg, packed), small, counts)

    names = ("w_in", "w_out", "w_up", "w_down")
    tms = dict(w_in=(cfg.TM, None), w_out=(cfg.TM, None), w_up=(cfg.TM // 2, None), w_down=(None, cfg.TN // 2))
    res = {}
    for n in names:
        res[n] = adamw(cfg, n, w_sh[n], m_all[n], v_all[n], [comm.grads[n]], tms[n])
    g_cw = lax.dynamic_slice_in_dim(red["conv_w"].reshape(3, N_CHIPS, cfg.FSHP), chip, 1, axis=1)[:, 0, :cfg.FSH]
    res["conv_w"] = adamw(cfg, "conv_w", w_sh["conv_w"], m_all["conv_w"], v_all["conv_w"], [g_cw], (None, None))
    g_cb = red["conv_b"].reshape(1, N_CHIPS, cfg.FSHP)[:, :, :cfg.FSH].reshape(1, N_CHIPS * cfg.FSH)
    res["conv_b"] = adamw(cfg, "conv_b", conv_b, m_all["conv_b"], v_all["conv_b"], [g_cb], (None, None))
    for k in ("g1", "g2", "g3", "g4", "g_sb", "g_dl"):
        res[k] = adamw(cfg, k, gains[k], m_all[k], v_all[k], [red[k]], (None, None))
    return red["loss"][0, 0], grad_x, res


PARAMS = ("pre_mix_gain", "post_mix_gain", "pre_ffn_gain", "post_ffn_gain", "w_in", "sb_out_gain", "dil_out_gain",
          "w_out", "w_up", "conv_w", "conv_b", "w_down")
SHORT = dict(pre_mix_gain="g1", post_mix_gain="g2", pre_ffn_gain="g3", post_ffn_gain="g4", sb_out_gain="g_sb",
             dil_out_gain="g_dl", w_in="w_in", w_out="w_out", w_up="w_up", conv_w="conv_w", conv_b="conv_b",
             w_down="w_down")


def kernel(x, pre_mix_gain, post_mix_gain, pre_ffn_gain, post_ffn_gain, w_in, sb_out_gain, dil_out_gain, w_out, w_up, conv_w, conv_b, w_down, loss_target, m_pre_mix_gain, m_post_mix_gain, m_pre_ffn_gain, m_post_ffn_gain, m_w_in, m_sb_out_gain, m_dil_out_gain, m_w_out, m_w_up, m_conv_w, m_conv_b, m_w_down, v_pre_mix_gain, v_post_mix_gain, v_pre_ffn_gain, v_post_ffn_gain, v_w_in, v_sb_out_gain, v_dil_out_gain, v_w_out, v_w_up, v_conv_w, v_conv_b, v_w_down):
    cfg = CFG
    w = dict(zip(PARAMS, (pre_mix_gain, post_mix_gain, pre_ffn_gain, post_ffn_gain, w_in, sb_out_gain, dil_out_gain,
                          w_out, w_up, conv_w, conv_b, w_down)))
    m = dict(zip(PARAMS, (m_pre_mix_gain, m_post_mix_gain, m_pre_ffn_gain, m_post_ffn_gain, m_w_in, m_sb_out_gain,
                          m_dil_out_gain, m_w_out, m_w_up, m_conv_w, m_conv_b, m_w_down)))
    v = dict(zip(PARAMS, (v_pre_mix_gain, v_post_mix_gain, v_pre_ffn_gain, v_post_ffn_gain, v_w_in, v_sb_out_gain,
                          v_dil_out_gain, v_w_out, v_w_up, v_conv_w, v_conv_b, v_w_down)))
    sq = lambda a: a.reshape(a.shape[1:])
    ws = {SHORT[k]: sq(a) if a.ndim == 3 else a for k, a in w.items()}
    ms = {SHORT[k]: sq(a) if a.ndim == 3 else a for k, a in m.items()}
    vs = {SHORT[k]: sq(a) if a.ndim == 3 else a for k, a in v.items()}
    gains = {k: ws[k] for k in ("g1", "g2", "g3", "g4", "g_sb", "g_dl")}
    w_sh = {k: ws[k] for k in ("w_in", "w_out", "w_up", "conv_w", "w_down")}
    loss, grad_x, res = step(cfg, sq(x), sq(loss_target), gains, w_sh, ws["conv_b"], ms, vs)
    outs = [loss, grad_x.reshape(x.shape)]
    for i in range(4):
        for k in PARAMS:
            outs.append(res[SHORT[k]][i].reshape(w[k].shape))
    return tuple(outs)
```

```python
import functools
import math
from typing import NamedTuple

import jax
import jax.numpy as jnp
from jax import lax
from jax.experimental import pallas as pl
from jax.experimental.pallas import tpu as pltpu

F32 = jnp.float32
BF16 = jnp.bfloat16
MESH = pl.DeviceIdType.MESH

ROPE_THETA = 10000.0
RMS_EPS = 1e-6
ADAM_LR = 0.001
ADAM_B1 = 0.9
ADAM_B2 = 0.999
ADAM_EPS = 1e-08
ADAM_WD = 0.01
ADAM_STEP = 10
GELU_C = math.sqrt(2.0 / math.pi)
NEG_BIG = -1e30
LANE = 128
N_CHIPS = 4
N_DEV = 8


class Cfg(NamedTuple):
    S: int = 2048
    D: int = 2048
    DH: int = 128
    HSB: int = 8
    HDL: int = 8
    QB: int = 128
    SBT: int = 256
    branches: tuple = ((128, 1), (512, 4), (2048, 16))
    FSH: int = 2752
    FSHP: int = 2816
    TM: int = 256
    TNF: int = 256
    TN: int = 512
    VMEM_MB: int = 56

    @property
    def DSB(self):
        return self.HSB * self.DH

    @property
    def DDL(self):
        return self.HDL * self.DH

    @property
    def DMIX(self):
        return self.DSB + self.DDL

    @property
    def FFP(self):
        return 2 * self.FSHP

    @property
    def FF2P(self):
        return 4 * self.FSHP


CFG = Cfg()


def _cp(cfg, sem=None):
    return pltpu.CompilerParams(dimension_semantics=sem, vmem_limit_bytes=cfg.VMEM_MB * 2**20)


def _dot(a, b):
    return jnp.dot(a, b, preferred_element_type=F32)


def _dot_nt(a, b):
    return lax.dot_general(a, b, (((1,), (1,)), ((), ())), preferred_element_type=F32)


def _dot_tn(a, b):
    return lax.dot_general(a, b, (((0,), (0,)), ((), ())), preferred_element_type=F32)


def _dot_split(x, u):
    hi = x.astype(BF16)
    lo = (x - hi.astype(F32)).astype(BF16)
    return _dot(hi, u) + _dot(lo, u)


def _rstd(x):
    return lax.rsqrt(jnp.mean(x * x, axis=-1, keepdims=True) + RMS_EPS)


def _rms_bwd(dy, x, g):
    r = _rstd(x)
    xh = x * r
    dxh = dy * g
    dx = r * (dxh - xh * jnp.mean(dxh * xh, axis=-1, keepdims=True))
    return dx, dy * xh


def _gelu(x):
    t = jnp.tanh(GELU_C * (x + 0.044715 * (x * x * x)))
    return 0.5 * x * (1.0 + t), t


def _gelu_grad(x, t):
    return 0.5 * (1.0 + t) + 0.5 * x * (1.0 - t * t) * (GELU_C * (1.0 + 3 * 0.044715 * (x * x)))


def _row(cfg, w):
    return pl.BlockSpec((cfg.TM, w), lambda i: (i, 0))


def _vec(w):
    return pl.BlockSpec((1, w), lambda i: (0, 0))


def rms_fwd(cfg, x, g):
    S, D = x.shape

    def body(x_ref, g_ref, h_ref):
        xv = x_ref[...]
        h_ref[...] = (xv * _rstd(xv) * g_ref[...]).astype(BF16)

    return pl.pallas_call(
        body, name="rms_fwd", grid=(S // cfg.TM,),
        in_specs=[_row(cfg, D), _vec(D)], out_specs=_row(cfg, D),
        out_shape=jax.ShapeDtypeStruct((S, D), BF16), compiler_params=_cp(cfg, ("parallel",)),
    )(x, g)


def mid_fwd(cfg, x, mo, g_post, g_pre):
    S, D = x.shape

    def body(x_ref, mo_ref, gp_ref, gn_ref, x1_ref, h2_ref):
        mo_v = mo_ref[...]
        x1 = x_ref[...] + mo_v * _rstd(mo_v) * gp_ref[...]
        x1_ref[...] = x1
        h2_ref[...] = (x1 * _rstd(x1) * gn_ref[...]).astype(BF16)

    return pl.pallas_call(
        body, name="mid_fwd", grid=(S // cfg.TM,),
        in_specs=[_row(cfg, D), _row(cfg, D), _vec(D), _vec(D)],
        out_specs=[_row(cfg, D), _row(cfg, D)],
        out_shape=[jax.ShapeDtypeStruct((S, D), F32), jax.ShapeDtypeStruct((S, D), BF16)],
        compiler_params=_cp(cfg, ("parallel",)),
    )(x, mo, g_post, g_pre)


def final_fwd_bwd(cfg, x1, f, g_post, target):
    S, D = x1.shape

    def body(x1_ref, f_ref, g_ref, t_ref, dout_ref, df_ref, dg_ref, loss_ref):
        @pl.when(pl.program_id(0) == 0)
        def _():
            dg_ref[...] = jnp.zeros_like(dg_ref)
            loss_ref[...] = jnp.zeros_like(loss_ref)

        fv = f_ref[...]
        g = g_ref[...]
        out = x1_ref[...] + fv * _rstd(fv) * g
        err = out - t_ref[...]
        loss_ref[...] += 0.5 * jnp.sum(jnp.mean(err * err, axis=-1, keepdims=True), axis=0, keepdims=True)
        dout = err * (1.0 / D)
        dout_ref[...] = dout
        df, dgx = _rms_bwd(dout, fv, g)
        df_ref[...] = df.astype(BF16)
        dg_ref[...] += jnp.sum(dgx, axis=0, keepdims=True)

    return pl.pallas_call(
        body, name="final_fwd_bwd", grid=(S // cfg.TM,),
        in_specs=[_row(cfg, D), _row(cfg, D), _vec(D), _row(cfg, D)],
        out_specs=[_row(cfg, D), _row(cfg, D), _vec(D), _vec(LANE)],
        out_shape=[jax.ShapeDtypeStruct((S, D), F32), jax.ShapeDtypeStruct((S, D), BF16),
                   jax.ShapeDtypeStruct((1, D), F32), jax.ShapeDtypeStruct((1, LANE), F32)],
        compiler_params=_cp(cfg, ("arbitrary",)),
    )(x1, f, g_post, target)


def mid_bwd(cfg, dh2, x1, g_pre, dout, mo, g_post):
    S, D = x1.shape

    def body(dh_ref, x1_ref, gn_ref, do_ref, mo_ref, gp_ref, dx1_ref, dmo_ref, dgn_ref, dgp_ref):
        @pl.when(pl.program_id(0) == 0)
        def _():
            dgn_ref[...] = jnp.zeros_like(dgn_ref)
            dgp_ref[...] = jnp.zeros_like(dgp_ref)

        dx, dgx = _rms_bwd(dh_ref[...], x1_ref[...], gn_ref[...])
        dx1 = do_ref[...] + dx
        dx1_ref[...] = dx1
        dgn_ref[...] += jnp.sum(dgx, axis=0, keepdims=True)
        dmo, dgy = _rms_bwd(dx1, mo_ref[...], gp_ref[...])
        dmo_ref[...] = dmo.astype(BF16)
        dgp_ref[...] += jnp.sum(dgy, axis=0, keepdims=True)

    return pl.pallas_call(
        body, name="mid_bwd", grid=(S // cfg.TM,),
        in_specs=[_row(cfg, D), _row(cfg, D), _vec(D), _row(cfg, D), _row(cfg, D), _vec(D)],
        out_specs=[_row(cfg, D), _row(cfg, D), _vec(D), _vec(D)],
        out_shape=[jax.ShapeDtypeStruct((S, D), F32), jax.ShapeDtypeStruct((S, D), BF16),
                   jax.ShapeDtypeStruct((1, D), F32), jax.ShapeDtypeStruct((1, D), F32)],
        compiler_params=_cp(cfg, ("arbitrary",)),
    )(dh2, x1, g_pre, dout, mo, g_post)


def first_bwd(cfg, dh1, x, g_pre, dx1):
    S, D = x.shape

    def body(dh_ref, x_ref, g_ref, r_ref, dx_ref, dg_ref):
        @pl.when(pl.program_id(0) == 0)
        def _():
            dg_ref[...] = jnp.zeros_like(dg_ref)

        dx, dgx = _rms_bwd(dh_ref[...], x_ref[...], g_ref[...])
        dx_ref[...] = r_ref[...] + dx
        dg_ref[...] += jnp.sum(dgx, axis=0, keepdims=True)

    return pl.pallas_call(
        body, name="first_bwd", grid=(S // cfg.TM,),
        in_specs=[_row(cfg, D), _row(cfg, D), _vec(D), _row(cfg, D)],
        out_specs=[_row(cfg, D), _vec(D)],
        out_shape=[jax.ShapeDtypeStruct((S, D), F32), jax.ShapeDtypeStruct((1, D), F32)],
        compiler_params=_cp(cfg, ("arbitrary",)),
    )(dh1, x, g_pre, dx1)


def _mm(cfg, name, a, b, *, nt, a_spec, b_spec, o_spec, grid, out_shape, acc_shape, dep=None):
    nk = grid[-1]
    dot = _dot_nt if nt else _dot
    deps = [] if dep is None else [dep]

    def body(a_ref, b_ref, *rest):
        o_ref, acc_ref = rest[-2:]
        k = pl.program_id(len(grid) - 1)
        part = dot(a_ref[...], b_ref[...])
        if deps:
            part = part + rest[0][0:1, 0:1]
        if nk == 1:
            o_ref[...] = part.astype(o_ref.dtype)
            return

        @pl.when(k == 0)
        def _():
            acc_ref[...] = part

        @pl.when(k > 0)
        def _():
            acc_ref[...] += part

        @pl.when(k == nk - 1)
        def _():
            o_ref[...] = acc_ref[...].astype(o_ref.dtype)

    sem = ("parallel",) * (len(grid) - 1) + ("arbitrary",)
    dep_specs = [pl.BlockSpec((8, LANE), lambda *_: (0, 0))] * len(deps)
    return pl.pallas_call(
        body, name=name, grid=grid, in_specs=[a_spec, b_spec] + dep_specs, out_specs=o_spec, out_shape=out_shape,
        scratch_shapes=[pltpu.VMEM(acc_shape, F32)], compiler_params=_cp(cfg, sem),
    )(a, b, *deps)


def _mm_tn(cfg, name, a, b, *, a_spec, b_spec, o_spec, grid, out_shape):
    def body(a_ref, b_ref, o_ref):
        o_ref[...] = _dot_tn(a_ref[...], b_ref[...]).astype(o_ref.dtype)

    return pl.pallas_call(
        body, name=name, grid=grid, in_specs=[a_spec, b_spec], out_specs=o_spec, out_shape=out_shape,
        compiler_params=_cp(cfg, ("parallel",) * len(grid)),
    )(a, b)


def qkv_proj(cfg, h1, w_in, cos2, sin2):
    S, D = h1.shape
    tn = 2 * cfg.DH
    per = cfg.DSB // tn
    assert cfg.DSB == cfg.DDL
    nblk = 6 * per

    def body(a_ref, b_ref, c_ref, s_ref, o_ref):
        j = pl.program_id(0)
        acc = _dot(a_ref[...], b_ref[...])
        rope = jnp.logical_and(j >= 3 * per, j < 5 * per)

        @pl.when(rope)
        def _():
            for c in range(tn // cfg.DH):
                xh = acc[:, c * cfg.DH:(c + 1) * cfg.DH]
                o_ref[:, c * cfg.DH:(c + 1) * cfg.DH] = (
                    xh * c_ref[...] + pltpu.roll(xh, cfg.DH // 2, 1) * s_ref[...]).astype(BF16)

        @pl.when(jnp.logical_not(rope))
        def _():
            o_ref[...] = acc.astype(BF16)

    return pl.pallas_call(
        body, name="qkv_proj", grid=(nblk,),
        in_specs=[pl.BlockSpec((S, D), lambda j: (0, 0)), pl.BlockSpec((D, tn), lambda j: (0, j)),
                  pl.BlockSpec((S, cfg.DH), lambda j: (0, 0)), pl.BlockSpec((S, cfg.DH), lambda j: (0, 0))],
        out_specs=pl.BlockSpec((None, S, tn), lambda j: (j // per, 0, j % per)),
        out_shape=jax.ShapeDtypeStruct((6, S, cfg.DSB), BF16),
        compiler_params=_cp(cfg, ("parallel",)),
    )(h1, w_in, cos2, sin2)


def _sb_tile(cfg, q, k, kb, qb):
    QB = cfg.SBT
    z = _dot_nt(q, k) * (cfg.DH ** -0.5)
    t1 = jnp.log1p(jnp.exp(-jnp.abs(z)))
    lb = jnp.minimum(z, 0.0) - t1
    row = lax.broadcasted_iota(jnp.int32, (QB, QB), 0)
    col = lax.broadcasted_iota(jnp.int32, (QB, QB), 1)
    valid = jnp.logical_or(kb < qb, col < row)
    lk = jnp.where(valid, jnp.minimum(-z, 0.0) - t1, 0.0)
    return lb, lk, valid


def sb_fwd(cfg, qkv3):
    S, QB, DH = cfg.S, cfg.SBT, cfg.DH

    def body(q_ref, k_ref, v_ref, o_ref, t_ref):
        row = lax.broadcasted_iota(jnp.int32, (QB, QB), 0)
        col = lax.broadcasted_iota(jnp.int32, (QB, QB), 1)
        u_after = (row > col).astype(BF16)

        def q_loop(qb, _):
            rows = pl.ds(pl.multiple_of(qb * QB, QB), QB)
            q = q_ref[rows, :]

            def k_loop(i, carry):
                o_acc, c = carry
                kb = qb - i
                krows = pl.ds(pl.multiple_of(kb * QB, QB), QB)
                lb, lk, valid = _sb_tile(cfg, q, k_ref[krows, :], kb, qb)
                rem = _dot_split(lk, u_after) + c
                a = jnp.where(valid, jnp.exp(lb + rem), 0.0)
                o_acc = o_acc + _dot(a.astype(BF16), v_ref[krows, :])
                return o_acc, c + jnp.sum(lk, axis=1, keepdims=True)

            o_acc, c = lax.fori_loop(0, qb + 1, k_loop, (jnp.zeros((QB, DH), F32), jnp.zeros((QB, 1), F32)))
            o_ref[rows, :] = o_acc
            t_ref[rows, :] = jnp.broadcast_to(c, (QB, DH))
            return 0

        lax.fori_loop(0, S // QB, q_loop, 0)

    def spec(i):
        return pl.BlockSpec((None, S, DH), lambda h: (i, 0, h))

    return pl.pallas_call(
        body, name="sb_fwd", grid=(cfg.HSB,),
        in_specs=[spec(0), spec(1), spec(2)],
        out_specs=[pl.BlockSpec((S, DH), lambda h: (0, h))] * 2,
        out_shape=[jax.ShapeDtypeStruct((S, cfg.DSB), F32)] * 2,
        compiler_params=_cp(cfg, ("parallel",)),
    )(qkv3, qkv3, qkv3)


def sb_bwd(cfg, qkv3, do_sb, tsum):
    S, QB, DH = cfg.S, cfg.SBT, cfg.DH
    scale = DH ** -0.5

    def body(q_ref, k_ref, v_ref, do_ref, t_ref, d_ref, dk_acc, dv_acc):
        dk_acc[...] = jnp.zeros_like(dk_acc)
        dv_acc[...] = jnp.zeros_like(dv_acc)
        row = lax.broadcasted_iota(jnp.int32, (QB, QB), 0)
        col = lax.broadcasted_iota(jnp.int32, (QB, QB), 1)
        u_upto = (row <= col).astype(BF16)
        u_before = (row < col).astype(BF16)

        def q_loop(qb, _):
            rows = pl.ds(pl.multiple_of(qb * QB, QB), QB)
            q = q_ref[rows, :]
            do = do_ref[rows, :]
            total = t_ref[rows, 0:1]

            def k_loop(kb, carry):
                dq_acc, pc, gc = carry
                krows = pl.ds(pl.multiple_of(kb * QB, QB), QB)
                k = k_ref[krows, :]
                v = v_ref[krows, :]
                lb, lk, valid = _sb_tile(cfg, q, k, kb, qb)
                rem = total - pc - _dot_split(lk, u_upto)
                a = jnp.where(valid, jnp.exp(lb + rem), 0.0)
                g = a * _dot_nt(do, v)
                dv_acc[krows, :] += _dot_tn(a.astype(BF16), do)
                cum = gc + _dot(g.astype(BF16), u_before)
                sig = jnp.exp(lb)
                dz = (jnp.where(valid, g * (1.0 - sig) - cum * sig, 0.0) * scale).astype(BF16)
                dq_acc = dq_acc + _dot(dz, k)
                dk_acc[krows, :] += _dot_tn(dz, q)
                return dq_acc, pc + jnp.sum(lk, axis=1, keepdims=True), gc + jnp.sum(g, axis=1, keepdims=True)

            z1 = jnp.zeros((QB, 1), F32)
            dq_acc, _, _ = lax.fori_loop(0, qb + 1, k_loop, (jnp.zeros((QB, DH), F32), z1, z1))
            d_ref[0, rows, :] = dq_acc.astype(BF16)
            return 0

        lax.fori_loop(0, S // QB, q_loop, 0)
        d_ref[1, :, :] = dk_acc[...].astype(BF16)
        d_ref[2, :, :] = dv_acc[...].astype(BF16)

    def spec(i):
        return pl.BlockSpec((None, S, DH), lambda h: (i, 0, h))

    return pl.pallas_call(
        body, name="sb_bwd", grid=(cfg.HSB,),
        in_specs=[spec(0), spec(1), spec(2), pl.BlockSpec((S, DH), lambda h: (0, h)),
                  pl.BlockSpec((S, DH), lambda h: (0, h))],
        out_specs=pl.BlockSpec((3, S, DH), lambda h: (0, 0, h)),
        out_shape=jax.ShapeDtypeStruct((3, S, cfg.DSB), BF16),
        scratch_shapes=[pltpu.VMEM((S, DH), F32), pltpu.VMEM((S, DH), F32)],
        compiler_params=_cp(cfg, ("parallel",)),
    )(qkv3, qkv3, qkv3, do_sb, tsum)


def _band_mask(cfg, n, n_back):
    QB = cfg.QB
    qi = lax.broadcasted_iota(jnp.int32, (QB, 2 * QB), 0)
    kj = lax.broadcasted_iota(jnp.int32, (QB, 2 * QB), 1)
    dist = QB + qi - kj
    return (dist >= 0) & (dist <= n_back) & jnp.logical_or(n > 0, kj >= QB)


def _sub_rows(start, n, dil):
    if dil > 1:
        return pl.ds(start, n, stride=dil)
    return pl.ds(start if isinstance(start, int) else pl.multiple_of(start, 8), n)


def _for_residues(dil, fn):
    if dil == 1:
        fn(0, 0)
    else:
        lax.fori_loop(0, dil, fn, 0)


def _lane_value(x):
    return jnp.max(x, axis=1, keepdims=True)


def dil_fwd(cfg, qkv3):
    S, QB, DH = cfg.S, cfg.QB, cfg.DH
    scale = DH ** -0.5
    nb = len(cfg.branches)
    mix_rows = min(256, S)

    def body(q_ref, k_ref, v_ref, o_ref, lt_ref, qf, kf, vf, kp, vp, *obl):
        obs, lbs = obl[:nb], obl[nb:]
        qf[...] = q_ref[...].astype(F32)
        kf[...] = k_ref[...].astype(F32)
        vf[...] = v_ref[...].astype(F32)
        kp[pl.ds(0, QB), :] = jnp.zeros((QB, DH), BF16)
        vp[pl.ds(0, QB), :] = jnp.zeros((QB, DH), BF16)
        for b, (window, dil) in enumerate(cfg.branches):
            L, n_back = S // dil, window // dil
            assert n_back <= QB and L % QB == 0

            def residue(r, _, b=b, dil=dil, L=L, n_back=n_back):
                kp[pl.ds(QB, L), :] = kf[_sub_rows(r, L, dil), :].astype(BF16)
                vp[pl.ds(QB, L), :] = vf[_sub_rows(r, L, dil), :].astype(BF16)

                def block(n, _):
                    rows = _sub_rows(r + n * (QB * dil), QB, dil)
                    band = pl.ds(pl.multiple_of(n * QB, QB), 2 * QB)
                    s = _dot_nt(qf[rows, :].astype(BF16), kp[band, :]) * scale
                    s = jnp.where(_band_mask(cfg, n, n_back), s, NEG_BIG)
                    m = jnp.max(s, axis=1, keepdims=True)
                    p = jnp.exp(s - m)
                    den = jnp.sum(p, axis=1, keepdims=True)
                    obs[b][rows, :] = _dot(p.astype(BF16), vp[band, :]) / den
                    lbs[b][rows, :] = jnp.broadcast_to(m + jnp.log(den), (QB, DH))
                    return 0

                lax.fori_loop(0, L // QB, block, 0)
                return 0

            _for_residues(dil, residue)

        def mix(i, _):
            rows = pl.ds(pl.multiple_of(i * mix_rows, mix_rows), mix_rows)
            ls = [r[rows, :] for r in lbs]
            m = functools.reduce(jnp.maximum, ls)
            es = [jnp.exp(l - m) for l in ls]
            tot = functools.reduce(jnp.add, es)
            o_ref[rows, :] = functools.reduce(jnp.add, [(e / tot) * r[rows, :] for e, r in zip(es, obs)])
            lt_ref[rows, :] = m + jnp.log(tot)
            return 0

        lax.fori_loop(0, S // mix_rows, mix, 0)

    def spec(i):
        return pl.BlockSpec((None, S, DH), lambda h: (i, 0, h))

    o_spec = pl.BlockSpec((S, DH), lambda h: (0, h))
    return pl.pallas_call(
        body, name="dil_fwd", grid=(cfg.HDL,),
        in_specs=[spec(3), spec(4), spec(5)], out_specs=[o_spec, o_spec],
        out_shape=[jax.ShapeDtypeStruct((S, cfg.DDL), F32)] * 2,
        scratch_shapes=[pltpu.VMEM((S, DH), F32)] * 3 + [pltpu.VMEM((S + QB, DH), BF16)] * 2
        + [pltpu.VMEM((S, DH), F32)] * (2 * nb),
        compiler_params=_cp(cfg, ("parallel",)),
    )(qkv3, qkv3, qkv3)


def dil_bwd(cfg, qkv3, do_dl, delta, lse_tot, cos2, sin2):
    S, QB, DH = cfg.S, cfg.QB, cfg.DH
    scale = DH ** -0.5
    out_rows = min(256, S)

    def body(q_ref, k_ref, v_ref, do_ref, dl_ref, lt_ref, c_ref, s_ref, d_ref,
             qf, kf, vf, dof, kp, vp, dkp, dvp, dqn, dkn, dvn):
        qf[...] = q_ref[...].astype(F32)
        kf[...] = k_ref[...].astype(F32)
        vf[...] = v_ref[...].astype(F32)
        dof[...] = do_ref[...].astype(F32)
        for acc in (dqn, dkn, dvn):
            acc[...] = jnp.zeros_like(acc)
        kp[pl.ds(0, QB), :] = jnp.zeros((QB, DH), BF16)
        vp[pl.ds(0, QB), :] = jnp.zeros((QB, DH), BF16)
        for window, dil in cfg.branches:
            L, n_back = S // dil, window // dil

            def residue(r, _, dil=dil, L=L, n_back=n_back):
                sub = _sub_rows(r, L, dil)
                kp[pl.ds(QB, L), :] = kf[sub, :].astype(BF16)
                vp[pl.ds(QB, L), :] = vf[sub, :].astype(BF16)
                dkp[pl.ds(0, QB + L), :] = jnp.zeros((QB + L, DH), F32)
                dvp[pl.ds(0, QB + L), :] = jnp.zeros((QB + L, DH), F32)

                def block(n, _):
                    rows = _sub_rows(r + n * (QB * dil), QB, dil)
                    band = pl.ds(pl.multiple_of(n * QB, QB), 2 * QB)
                    q = qf[rows, :].astype(BF16)
                    do = dof[rows, :].astype(BF16)
                    kb = kp[band, :]
                    s = _dot_nt(q, kb) * scale
                    s = jnp.where(_band_mask(cfg, n, n_back), s, NEG_BIG)
                    p = jnp.exp(s - _lane_value(lt_ref[rows, :]))
                    ds = (p * (_dot_nt(do, vp[band, :]) - _lane_value(dl_ref[rows, :])) * scale).astype(BF16)
                    dqn[rows, :] += _dot(ds, kb)
                    dkp[band, :] += _dot_tn(ds, q)
                    dvp[band, :] += _dot_tn(p.astype(BF16), do)
                    return 0

                lax.fori_loop(0, L // QB, block, 0)
                dkn[sub, :] += dkp[pl.ds(QB, L), :]
                dvn[sub, :] += dvp[pl.ds(QB, L), :]
                return 0

            _for_residues(dil, residue)

        def finish(i, _):
            rows = pl.ds(pl.multiple_of(i * out_rows, out_rows), out_rows)
            c, sn = c_ref[rows, :], s_ref[rows, :]
            for j, acc in enumerate((dqn, dkn)):
                d = acc[rows, :]
                d_ref[j, rows, :] = (d * c + pltpu.roll(d * sn, DH // 2, 1)).astype(BF16)
            d_ref[2, rows, :] = dvn[rows, :].astype(BF16)
            return 0

        lax.fori_loop(0, S // out_rows, finish, 0)

    def spec(i):
        return pl.BlockSpec((None, S, DH), lambda h: (i, 0, h))

    hd = pl.BlockSpec((S, DH), lambda h: (0, h))
    tab = pl.BlockSpec((S, DH), lambda h: (0, 0))
    return pl.pallas_call(
        body, name="dil_bwd", grid=(cfg.HDL,),
        in_specs=[spec(3), spec(4), spec(5), hd, hd, hd, tab, tab],
        out_specs=pl.BlockSpec((3, S, DH), lambda h: (0, 0, h)),
        out_shape=jax.ShapeDtypeStruct((3, S, cfg.DDL), BF16),
        scratch_shapes=[pltpu.VMEM((S, DH), F32)] * 4 + [pltpu.VMEM((S + QB, DH), BF16)] * 2
        + [pltpu.VMEM((S + QB, DH), F32)] * 2 + [pltpu.VMEM((S, DH), F32)] * 3,
        compiler_params=_cp(cfg, ("parallel",)),
    )(qkv3, qkv3, qkv3, do_dl, delta, lse_tot, cos2, sin2)


def combine_fwd(cfg, o_sb, o_dl, g_sb, g_dl):
    S, DH = cfg.S, cfg.DH

    def head_norm(o, g):
        return o * lax.rsqrt(jnp.mean(o * o, axis=-1, keepdims=True) + RMS_EPS) * g

    def body(osb_ref, odl_ref, gsb_ref, gdl_ref, mix_ref):
        for h in range(cfg.HSB):
            c = slice(h * DH, (h + 1) * DH)
            mix_ref[:, c] = head_norm(osb_ref[:, c], gsb_ref[:, c]).astype(BF16)
        for h in range(cfg.HDL):
            c = slice(h * DH, (h + 1) * DH)
            mix_ref[:, cfg.DSB + h * DH:cfg.DSB + (h + 1) * DH] = head_norm(odl_ref[:, c], gdl_ref[:, c]).astype(BF16)

    return pl.pallas_call(
        body, name="combine_fwd", grid=(S // cfg.TM,),
        in_specs=[_row(cfg, cfg.DSB), _row(cfg, cfg.DDL), _vec(cfg.DSB), _vec(cfg.DDL)],
        out_specs=_row(cfg, cfg.DMIX), out_shape=jax.ShapeDtypeStruct((S, cfg.DMIX), BF16),
        compiler_params=_cp(cfg, ("parallel",)),
    )(o_sb, o_dl, g_sb, g_dl)


def combine_bwd(cfg, dmix, o_sb, o_dl, g_sb, g_dl):
    S, DH = cfg.S, cfg.DH

    def body(dm_ref, osb_ref, odl_ref, gsb_ref, gdl_ref, dsb_ref, ddl_ref, dl_ref, dgsb_ref, dgdl_ref):
        @pl.when(pl.program_id(0) == 0)
        def _():
            dgsb_ref[...] = jnp.zeros_like(dgsb_ref)
            dgdl_ref[...] = jnp.zeros_like(dgdl_ref)

        for h in range(cfg.HSB):
            c = slice(h * DH, (h + 1) * DH)
            dx, dgx = _rms_bwd(dm_ref[:, c], osb_ref[:, c], gsb_ref[:, c])
            dsb_ref[:, c] = dx.astype(BF16)
            dgsb_ref[:, c] += jnp.sum(dgx, axis=0, keepdims=True)
        for h in range(cfg.HDL):
            c = slice(h * DH, (h + 1) * DH)
            o = odl_ref[:, c]
            dx, dgx = _rms_bwd(dm_ref[:, cfg.DSB + h * DH:cfg.DSB + (h + 1) * DH], o, gdl_ref[:, c])
            ddl_ref[:, c] = dx.astype(BF16)
            dl_ref[:, c] = jnp.broadcast_to(jnp.sum(dx * o, axis=-1, keepdims=True), dx.shape)
            dgdl_ref[:, c] += jnp.sum(dgx, axis=0, keepdims=True)

    return pl.pallas_call(
        body, name="combine_bwd", grid=(S // cfg.TM,),
        in_specs=[_row(cfg, cfg.DMIX), _row(cfg, cfg.DSB), _row(cfg, cfg.DDL), _vec(cfg.DSB), _vec(cfg.DDL)],
        out_specs=[_row(cfg, cfg.DSB), _row(cfg, cfg.DDL), _row(cfg, cfg.DDL), _vec(cfg.DSB), _vec(cfg.DDL)],
        out_shape=[jax.ShapeDtypeStruct((S, cfg.DSB), BF16), jax.ShapeDtypeStruct((S, cfg.DDL), BF16),
                   jax.ShapeDtypeStruct((S, cfg.DDL), F32), jax.ShapeDtypeStruct((1, cfg.DSB), F32),
                   jax.ShapeDtypeStruct((1, cfg.DDL), F32)],
        compiler_params=_cp(cfg, ("arbitrary",)),
    )(dmix, o_sb, o_dl, g_sb, g_dl)


def _shift_rows(u, j):
    row = lax.broadcasted_iota(jnp.int32, u.shape, 0)
    return jnp.where(row >= j, pltpu.roll(u, j, 0), 0.0)


def _shift_rows_up(u, j):
    n = u.shape[0]
    row = lax.broadcasted_iota(jnp.int32, u.shape, 0)
    return jnp.where(row < n - j, pltpu.roll(u, n - j, 0), 0.0)


def _conv(u, cw, cb):
    return u * cw[2:3, :] + _shift_rows(u, 1) * cw[1:2, :] + _shift_rows(u, 2) * cw[0:1, :] + cb


def ffn_fwd(cfg, h2, w_up, conv_w, conv_b):
    S, D = h2.shape
    tn, nt = cfg.TNF, cfg.FFP // cfg.TNF

    def body(h_ref, wg_ref, wv_ref, cwg_ref, cwv_ref, cbg_ref, cbv_ref, u_ref, y_ref):
        h = h_ref[...]
        ug = _dot(h, wg_ref[...])
        uv = _dot(h, wv_ref[...])
        u_ref[0] = ug
        u_ref[1] = uv
        gl, _ = _gelu(_conv(ug, cwg_ref[...], cbg_ref[...]))
        y_ref[...] = (gl * _conv(uv, cwv_ref[...], cbv_ref[...])).astype(BF16)

    return pl.pallas_call(
        body, name="ffn_fwd", grid=(nt,),
        in_specs=[pl.BlockSpec((S, D), lambda n: (0, 0)),
                  pl.BlockSpec((D, tn), lambda n: (0, n)), pl.BlockSpec((D, tn), lambda n: (0, n + nt)),
                  pl.BlockSpec((3, tn), lambda n: (0, n)), pl.BlockSpec((3, tn), lambda n: (0, n + nt)),
                  pl.BlockSpec((1, tn), lambda n: (0, n)), pl.BlockSpec((1, tn), lambda n: (0, n + nt))],
        out_specs=[pl.BlockSpec((2, S, tn), lambda n: (0, 0, n)), pl.BlockSpec((S, tn), lambda n: (0, n))],
        out_shape=[jax.ShapeDtypeStruct((2, S, cfg.FFP), F32), jax.ShapeDtypeStruct((S, cfg.FFP), BF16)],
        compiler_params=_cp(cfg, ("parallel",)),
    )(h2, w_up, w_up, conv_w, conv_w, conv_b, conv_b)


def ffn_bwd(cfg, df, w_down, u, conv_w, conv_b):
    S, D = df.shape
    tn, nt = cfg.TNF, cfg.FFP // cfg.TNF

    def conv_bwd(dc, uu, cw):
        du = dc * cw[2:3, :] + _shift_rows_up(dc, 1) * cw[1:2, :] + _shift_rows_up(dc, 2) * cw[0:1, :]
        dws = [jnp.sum(dc * _shift_rows(uu, 2), axis=0, keepdims=True),
               jnp.sum(dc * _shift_rows(uu, 1), axis=0, keepdims=True),
               jnp.sum(dc * uu, axis=0, keepdims=True)]
        return du, dws, jnp.sum(dc, axis=0, keepdims=True)

    def body(df_ref, wd_ref, u_ref, cwg_ref, cwv_ref, cbg_ref, cbv_ref, du_ref, dwd_ref, dcw_ref, dcb_ref):
        dfv = df_ref[...]
        dy = _dot_nt(dfv, wd_ref[...])
        ug, uv = u_ref[0], u_ref[1]
        cwg, cwv = cwg_ref[...], cwv_ref[...]
        cg = _conv(ug, cwg, cbg_ref[...])
        cv = _conv(uv, cwv, cbv_ref[...])
        gl, t = _gelu(cg)
        dwd_ref[...] = _dot_tn((gl * cv).astype(BF16), dfv).astype(BF16)
        dug, dwg, dbg = conv_bwd(dy * cv * _gelu_grad(cg, t), ug, cwg)
        duv, dwv, dbv = conv_bwd(dy * gl, uv, cwv)
        du_ref[0] = dug.astype(BF16)
        du_ref[1] = duv.astype(BF16)
        for j in range(3):
            dcw_ref[0, j:j + 1, :] = dwg[j]
            dcw_ref[1, j:j + 1, :] = dwv[j]
        dcb_ref[0] = dbg
        dcb_ref[1] = dbv

    return pl.pallas_call(
        body, name="ffn_bwd", grid=(nt,),
        in_specs=[pl.BlockSpec((S, D), lambda n: (0, 0)), pl.BlockSpec((tn, D), lambda n: (n, 0)),
                  pl.BlockSpec((2, S, tn), lambda n: (0, 0, n)),
                  pl.BlockSpec((3, tn), lambda n: (0, n)), pl.BlockSpec((3, tn), lambda n: (0, n + nt)),
                  pl.BlockSpec((1, tn), lambda n: (0, n)), pl.BlockSpec((1, tn), lambda n: (0, n + nt))],
        out_specs=[pl.BlockSpec((2, S, tn), lambda n: (0, 0, n)), pl.BlockSpec((tn, D), lambda n: (n, 0)),
                   pl.BlockSpec((2, 3, tn), lambda n: (0, 0, n)), pl.BlockSpec((2, 1, tn), lambda n: (0, 0, n))],
        out_shape=[jax.ShapeDtypeStruct((2, S, cfg.FFP), BF16), jax.ShapeDtypeStruct((cfg.FFP, D), BF16),
                   jax.ShapeDtypeStruct((2, 3, cfg.FFP), F32), jax.ShapeDtypeStruct((2, 1, cfg.FFP), F32)],
        compiler_params=_cp(cfg, ("parallel",)),
    )(df, w_down, u, conv_w, conv_w, conv_b, conv_b)


def rope_tables(cfg):
    inv_freq = ROPE_THETA ** (-jnp.arange(0, cfg.DH, 2, dtype=F32) / cfg.DH)
    ang = jnp.arange(cfg.S, dtype=F32)[:, None] * inv_freq[None, :]
    cos, sin = jnp.cos(ang), jnp.sin(ang)
    return jnp.concatenate([cos, cos], axis=1), jnp.concatenate([-sin, sin], axis=1)


class LocalWeights:
    def __init__(self, w_in, w_out, w_up, conv_w, w_down):
        self.w = (w_in, w_out, w_up, conv_w, w_down)
        self.grads = {}

    def weights_first(self):
        return self.w[0], self.w[3]

    def start_rest(self):
        return None

    def weights_rest(self, after):
        return self.w[1], self.w[2], self.w[4]

    def reduce_start(self, grads):
        self.grads.update(grads)
        return None

    def reduce_wait(self, names, after):
        pass


def _after(a, token):
    return a if token is None else a + token[0, 0].astype(a.dtype)


def local_step(cfg, comm, x, target, g1, g2, g3, g4, g_sb, g_dl, conv_b):
    S, D = cfg.S, cfg.D
    cos2, sin2 = rope_tables(cfg)
    full = lambda r, c: pl.BlockSpec((r, c), lambda j, k: (0, 0))

    w_in, conv_w = comm.weights_first()
    h1 = rms_fwd(cfg, x, g1)
    qkv3 = qkv_proj(cfg, h1, w_in, _after(cos2, comm.start_rest()), sin2)
    o_sb, tsum = sb_fwd(cfg, qkv3)
    o_dl, lse_tot = dil_fwd(cfg, qkv3)
    mixed = combine_fwd(cfg, o_sb, o_dl, g_sb, g_dl)
    w_out, w_up, w_down = comm.weights_rest(after=mixed)
    tn = cfg.TN
    mo = _mm(cfg, "mix_out", mixed, w_out, nt=False, grid=(D // tn, 1),
             a_spec=full(S, cfg.DMIX), b_spec=pl.BlockSpec((cfg.DMIX, tn), lambda j, k: (0, j)),
             o_spec=pl.BlockSpec((S, tn), lambda j, k: (0, j)),
             out_shape=jax.ShapeDtypeStruct((S, D), F32), acc_shape=(8, LANE))
    x1, h2 = mid_fwd(cfg, x, mo, g2, g3)
    u, y = ffn_fwd(cfg, h2, w_up, conv_w, conv_b)
    tk = cfg.FFP // 4
    f = _mm(cfg, "ffn_down", y, w_down, nt=False, grid=(D // tn, 4),
            a_spec=pl.BlockSpec((S, tk), lambda j, k: (0, k)), b_spec=pl.BlockSpec((tk, tn), lambda j, k: (k, j)),
            o_spec=pl.BlockSpec((S, tn), lambda j, k: (0, j)),
            out_shape=jax.ShapeDtypeStruct((S, D), F32), acc_shape=(S, tn))
    dout, df, dg4, loss = final_fwd_bwd(cfg, x1, f, g4, target)

    du, dw_down, dconv_w, dconv_b = ffn_bwd(cfg, df, w_down, u, conv_w, conv_b)
    kt = cfg.FFP // tk
    dh2 = _mm(cfg, "d_h2", du, w_up, nt=True, grid=(D // tn, 2 * kt),
              a_spec=pl.BlockSpec((None, S, tk), lambda j, k: (k // kt, 0, k % kt)),
              b_spec=pl.BlockSpec((tn, tk), lambda j, k: (j, k)),
              o_spec=pl.BlockSpec((S, tn), lambda j, k: (0, j)),
              out_shape=jax.ShapeDtypeStruct((S, D), F32), acc_shape=(S, tn))
    nf = cfg.FFP // tn if cfg.FFP % tn == 0 else None
    tnu = tn if nf else cfg.TNF
    nf = cfg.FFP // tnu
    dw_up = _mm_tn(cfg, "d_w_up", h2, du, grid=(2 * nf,),
                   a_spec=pl.BlockSpec((S, D), lambda j: (0, 0)),
                   b_spec=pl.BlockSpec((None, S, tnu), lambda j: (j // nf, 0, j % nf)),
                   o_spec=pl.BlockSpec((D, tnu), lambda j: (0, j)),
                   out_shape=jax.ShapeDtypeStruct((D, cfg.FF2P), BF16))
    dx1, dmo, dg3, dg2 = mid_bwd(cfg, dh2, x1, g3, dout, mo, g2)

    dmix = _mm(cfg, "d_mixed", dmo, w_out, nt=True, grid=(cfg.DMIX // tn, 1),
               a_spec=full(S, D), b_spec=pl.BlockSpec((tn, D), lambda j, k: (j, 0)),
               o_spec=pl.BlockSpec((S, tn), lambda j, k: (0, j)),
               out_shape=jax.ShapeDtypeStruct((S, cfg.DMIX), F32), acc_shape=(8, LANE))
    dw_out = _mm_tn(cfg, "d_w_out", mixed, dmo, grid=(D // tn,),
                    a_spec=pl.BlockSpec((S, cfg.DMIX), lambda j: (0, 0)),
                    b_spec=pl.BlockSpec((S, tn), lambda j: (0, j)),
                    o_spec=pl.BlockSpec((cfg.DMIX, tn), lambda j: (0, j)),
                    out_shape=jax.ShapeDtypeStruct((cfg.DMIX, D), BF16))
    token = comm.reduce_start(dict(w_out=dw_out, w_up=dw_up, w_down=dw_down))
    do_sb, do_dl, delta, dg_sb, dg_dl = combine_bwd(cfg, dmix, o_sb, o_dl, _after(g_sb, token), g_dl)
    d_sb3 = sb_bwd(cfg, qkv3, do_sb, tsum)
    d_dl3 = dil_bwd(cfg, qkv3, do_dl, delta, lse_tot, cos2, sin2)
    dqkv3 = jnp.concatenate([d_sb3, d_dl3], axis=0)
    comm.reduce_wait(("w_out", "w_up", "w_down"), after=dqkv3)
    tkq = min(tn, cfg.DSB)
    kq = cfg.DSB // tkq
    dw_in = _mm_tn(cfg, "d_w_in", h1, dqkv3, grid=(6 * kq,),
                   a_spec=pl.BlockSpec((S, D), lambda j: (0, 0)),
                   b_spec=pl.BlockSpec((None, S, tkq), lambda j: (j // kq, 0, j % kq)),
                   o_spec=pl.BlockSpec((D, tkq), lambda j: (0, j)),
                   out_shape=jax.ShapeDtypeStruct((D, 6 * cfg.DSB), BF16))
    token = comm.reduce_start(dict(w_in=dw_in))
    dh1 = _mm(cfg, "d_h1", dqkv3, w_in, nt=True, grid=(D // tn, 6 * kq),
              a_spec=pl.BlockSpec((None, S, tkq), lambda j, k: (k // kq, 0, k % kq)),
              b_spec=pl.BlockSpec((tn, tkq), lambda j, k: (j, k)),
              o_spec=pl.BlockSpec((S, tn), lambda j, k: (0, j)),
              out_shape=jax.ShapeDtypeStruct((S, D), F32), acc_shape=(S, tn), dep=token)
    grad_x, dg1 = first_bwd(cfg, dh1, x, g1, dx1)
    comm.reduce_wait(("w_in",), after=grad_x)
    small = dict(loss=loss, g1=dg1, g2=dg2, g3=dg3, g4=dg4, g_sb=dg_sb, g_dl=dg_dl,
                 conv_b=dconv_b.reshape(1, cfg.FF2P), conv_w=dconv_w.transpose(1, 0, 2).reshape(3, cfg.FF2P))
    return grad_x, small


ANY = pl.BlockSpec(memory_space=pl.ANY)


def _me():
    return lax.axis_index("x"), lax.axis_index("y"), lax.axis_index("c")


def _other_chips(x, y):
    return [(1 - x, y), (x, 1 - y), (1 - x, 1 - y)]


def pad_conv_w(cfg, conv_w):
    r, c = conv_w.shape

    def body(w_ref, o_ref):
        o_ref[:, :c] = w_ref[...]
        o_ref[:, c:] = jnp.zeros((r, cfg.FSHP - c), F32)

    return pl.pallas_call(body, name="pad_conv_w", out_shape=jax.ShapeDtypeStruct((r, cfg.FSHP), F32))(conv_w)


def _tile2(r, c):
    return (256, c) if r % 256 == 0 else (r, 512 if c % 512 == 0 else c)


def cast_into(cfg, name, w, pos):
    r, c = w.shape
    _, nr, _, nc = _slab(cfg, name, 0)
    tm, tc = _tile2(r, c)
    wc = nc if tc == c else tc
    assert nr == r and (nc == c or tc == c)

    def body(pos_ref, w_ref, full_ref, scr, sem):
        scr[:, :tc] = w_ref[...].astype(BF16)
        if wc > tc:
            scr[:, tc:] = jnp.zeros((tm, wc - tc), BF16)
        r0, _, c0, _ = _slab(cfg, name, pos_ref[0])
        rows = pl.ds(pl.multiple_of(r0 + pl.program_id(0) * tm, 16), tm)
        cols = pl.ds(pl.multiple_of(c0 + pl.program_id(1) * tc, LANE), wc)
        cp = pltpu.make_async_copy(scr, full_ref.at[rows, cols], sem)
        cp.start()
        cp.wait()

    return pl.pallas_call(
        body, name=f"cast_{name}",
        grid_spec=pltpu.PrefetchScalarGridSpec(
            num_scalar_prefetch=1, grid=(r // tm, c // tc),
            in_specs=[pl.BlockSpec((tm, tc), lambda i, j, p: (i, j))], out_specs=ANY,
            scratch_shapes=[pltpu.VMEM((tm, wc), BF16), pltpu.SemaphoreType.DMA]),
        out_shape=jax.ShapeDtypeStruct(_full_shape(cfg, name), BF16),
        compiler_params=_cp(cfg, ("arbitrary", "arbitrary")),
    )(pos, w)


HBM = pl.BlockSpec(memory_space=pltpu.HBM)
SEM = pl.BlockSpec(memory_space=pltpu.SEMAPHORE)
TOKEN = pl.BlockSpec(memory_space=pltpu.VMEM)
EFFECT = pltpu.SideEffectType.DATAFLOW_SIDE_EFFECTING


def _slab(cfg, name, k):
    D = cfg.D
    if name == "w_in":
        cin = 6 * cfg.DSB // N_CHIPS
        return 0, D, k * cin, cin
    if name == "w_out":
        rout = cfg.DMIX // N_CHIPS
        return k * rout, rout, 0, D
    if name == "w_up":
        return 0, D, k * cfg.FSHP, cfg.FSHP
    rdn = cfg.FSH // 2
    return (k // 2) * cfg.FSHP + (k % 2) * rdn, rdn, 0, D


def _full_shape(cfg, name):
    return dict(w_in=(cfg.D, 6 * cfg.DSB), w_out=(cfg.DMIX, cfg.D), w_up=(cfg.D, cfg.FF2P), w_down=(cfg.FFP, cfg.D))[name]


def _half(cfg, name, ref, k, h):
    r0, nr, c0, nc = _slab(cfg, name, k)
    return ref.at[pl.ds(r0 + h * (nr // 2), nr // 2), pl.ds(c0, nc)]


def _rows_half(ref, h):
    nr = ref.shape[0] // 2
    return ref.at[pl.ds(h * nr, nr), :]


def _remote(src, dst, send_sem, recv_sem, dev):
    return pltpu.make_async_remote_copy(src_ref=src, dst_ref=dst, send_sem=send_sem, recv_sem=recv_sem,
                                        device_id=dev, device_id_type=MESH)


def gather_first(cfg, g_in, sh_cw):
    def body(in_ref, cw_ref, g_in, g_cw, token, ssem, rsem, fssem, frsem, lsem):
        x, y, c = _me()
        me, sib = 2 * x + y, (x, y, 1 - c)
        cw_slot = lambda k: g_cw.at[:, pl.ds(k * cfg.FSHP, cfg.FSHP)]
        local = [pltpu.make_async_copy(cw_ref, cw_slot(me), lsem.at[0])]
        sends = []
        for j, (px, py) in enumerate(_other_chips(x, y)):
            mine = _half(cfg, "w_in", g_in, me, c)
            sends.append(_remote(mine, mine, ssem.at[j], rsem.at[j], (px, py, c)))
            sends.append(_remote(cw_ref, cw_slot(me), ssem.at[3 + j], rsem.at[3 + j], (px, py, c)))
        for cp in local + sends:
            cp.start()
        for j, (px, py) in enumerate(_other_chips(x, y)):
            k = 2 * px + py
            landed = _half(cfg, "w_in", g_in, k, c)
            _remote(landed, landed, ssem.at[j], rsem.at[j], (px, py, c)).wait_recv()
            fwd = _remote(landed, landed, fssem.at[j], frsem.at[j], sib)
            fwd.start()
            sends.append(fwd)
        for j, (px, py) in enumerate(_other_chips(x, y)):
            k = 2 * px + py
            passed = _half(cfg, "w_in", g_in, k, 1 - c)
            _remote(passed, passed, fssem.at[j], frsem.at[j], sib).wait_recv()
            _remote(cw_ref, cw_slot(k), ssem.at[3 + j], rsem.at[3 + j], (px, py, c)).wait_recv()
        for cp in sends:
            cp.wait_send()
        for cp in local:
            cp.wait()
        token[...] = jnp.zeros_like(token)

    return pl.pallas_call(
        body, name="gather_first", in_specs=[ANY, ANY], out_specs=[ANY, ANY, TOKEN],
        out_shape=[jax.ShapeDtypeStruct(_full_shape(cfg, "w_in"), BF16), jax.ShapeDtypeStruct((3, cfg.FF2P), F32),
                   jax.ShapeDtypeStruct((8, LANE), F32)],
        input_output_aliases={0: 0},
        scratch_shapes=[pltpu.SemaphoreType.DMA((6,)), pltpu.SemaphoreType.DMA((6,)), pltpu.SemaphoreType.DMA((3,)),
                        pltpu.SemaphoreType.DMA((3,)), pltpu.SemaphoreType.DMA((1,))],
    )(g_in, sh_cw)


REST = ("w_out", "w_up", "w_down")


def _hbm(a):
    return pltpu.with_memory_space_constraint(a, pltpu.HBM)


def gather_start(cfg, fulls, after):
    n = len(REST)

    def body(*refs):
        lands = refs[:n]
        ssem, rsem = refs[n + 1], refs[n + 2]
        token = refs[-1]
        x, y, c = _me()
        me = 2 * x + y
        for i, name in enumerate(REST):
            mine = _half(cfg, name, lands[i], me, c)
            for j, (px, py) in enumerate(_other_chips(x, y)):
                _remote(mine, mine, ssem.at[3 * i + j], rsem.at[3 * i + j], (px, py, c)).start()
        token[...] = jnp.zeros_like(token)

    ops = [_hbm(a) for a in fulls]
    outs = pl.pallas_call(
        body, name="gather_start",
        in_specs=[HBM] * n + [ANY],
        out_specs=[SEM, SEM] + [HBM] * n + [TOKEN],
        out_shape=[pltpu.SemaphoreType.DMA((3 * n,)), pltpu.SemaphoreType.DMA((3 * n,))]
        + [pltpu.HBM(a.shape, a.dtype) for a in ops] + [jax.ShapeDtypeStruct((8, LANE), F32)],
        input_output_aliases={i: 2 + i for i in range(n)},
        compiler_params=pltpu.CompilerParams(has_side_effects=EFFECT),
    )(*ops, after)
    return outs[0], outs[1], outs[2:2 + n], outs[-1]


def gather_wait(cfg, ssem, rsem, lands, after):
    n = len(REST)

    def body(*refs):
        lands_ = refs[:n]
        ssem_, rsem_ = refs[n], refs[n + 1]
        x, y, c = _me()
        me = 2 * x + y
        for i, name in enumerate(REST):
            for j, (px, py) in enumerate(_other_chips(x, y)):
                cp = _remote(_half(cfg, name, lands_[i], me, c), _half(cfg, name, lands_[i], 2 * px + py, c),
                             ssem_.at[3 * i + j], rsem_.at[3 * i + j], (px, py, c))
                cp.wait_send()
                cp.wait_recv()

    return pl.pallas_call(
        body, name="gather_wait",
        in_specs=[HBM] * n + [SEM, SEM, ANY], out_specs=[HBM] * n,
        out_shape=[pltpu.HBM(a.shape, a.dtype) for a in lands],
        input_output_aliases={i: i for i in range(n)},
        compiler_params=pltpu.CompilerParams(has_side_effects=EFFECT),
    )(*lands, ssem, rsem, after)


def gather_finish(cfg, lands):
    n = len(REST)
    rdn = cfg.FSH // 2
    zpad = jnp.zeros((cfg.FSHP - cfg.FSH, cfg.D), BF16)

    def body(*refs):
        z_ref, outs = refs[0], refs[n + 1:2 * n + 1]
        ssem, rsem, lsem = refs[2 * n + 1:]
        x, y, c = _me()
        sib = (x, y, 1 - c)
        local = [pltpu.make_async_copy(z_ref, outs[2].at[pl.ds(h * cfg.FSHP + 2 * rdn, cfg.FSHP - cfg.FSH), :],
                                       lsem.at[h]) for h in range(2)]
        fwds = []
        for i, name in enumerate(REST):
            for j, (px, py) in enumerate(_other_chips(x, y)):
                landed = _half(cfg, name, outs[i], 2 * px + py, c)
                fwds.append(_remote(landed, landed, ssem.at[3 * i + j], rsem.at[3 * i + j], sib))
        for cp in local + fwds:
            cp.start()
        for i, name in enumerate(REST):
            for j, (px, py) in enumerate(_other_chips(x, y)):
                passed = _half(cfg, name, outs[i], 2 * px + py, 1 - c)
                _remote(passed, passed, ssem.at[3 * i + j], rsem.at[3 * i + j], sib).wait_recv()
        for cp in fwds:
            cp.wait_send()
        for cp in local:
            cp.wait()

    return pl.pallas_call(
        body, name="gather_finish", in_specs=[ANY] * (n + 1), out_specs=[ANY] * n,
        out_shape=[jax.ShapeDtypeStruct(a.shape, a.dtype) for a in lands],
        input_output_aliases={1 + i: i for i in range(n)},
        scratch_shapes=[pltpu.SemaphoreType.DMA((3 * n,)), pltpu.SemaphoreType.DMA((3 * n,)),
                        pltpu.SemaphoreType.DMA((2,))],
    )(zpad, *lands)


def pair_send(cfg, grads):
    names = list(grads)
    n = len(names)

    def half_shape(name):
        _, nr, _, nc = _slab(cfg, name, 0)
        return (N_CHIPS, nr // 2, nc)

    def body(*refs):
        srcs, theirs = refs[:n], refs[n:2 * n]
        ssem, rsem = refs[2 * n:]
        x, y, c = _me()
        cps = []
        for i, name in enumerate(names):
            for k in range(N_CHIPS):
                cps.append(_remote(_half(cfg, name, srcs[i], k, 1 - c), theirs[i].at[k],
                                   ssem.at[N_CHIPS * i + k], rsem.at[N_CHIPS * i + k], (x, y, 1 - c)))
        for cp in cps:
            cp.start()
        for cp in cps:
            cp.wait()

    outs = pl.pallas_call(
        body, name="pair_send_" + "_".join(names), in_specs=[ANY] * n, out_specs=[ANY] * n,
        out_shape=[jax.ShapeDtypeStruct(half_shape(name), BF16) for name in names],
        scratch_shapes=[pltpu.SemaphoreType.DMA((N_CHIPS * n,))] * 2,
    )(*[grads[k] for k in names])
    return dict(zip(names, outs))


def pair_sum(cfg, name, grad, theirs, pos):
    _, r, c = theirs.shape
    tm, tc = _tile2(r, c)

    ni, nj = r // tm, c // tc
    total = N_CHIPS * ni * nj

    def body(pos_ref, g_ref, t_ref, o_ref, scr, sem):
        step = (pl.program_id(0) * ni + pl.program_id(1)) * nj + pl.program_id(2)

        def fetch(flat, slot):
            k, rem = flat // (ni * nj), flat % (ni * nj)
            r0, nr, c0, _ = _slab(cfg, name, k)
            rows = pl.ds(pl.multiple_of(r0 + pos_ref[1] * (nr // 2) + (rem // nj) * tm, 16), tm)
            cols = pl.ds(pl.multiple_of(c0 + (rem % nj) * tc, LANE), tc)
            return pltpu.make_async_copy(g_ref.at[rows, cols], scr.at[slot], sem.at[slot])

        @pl.when(step == 0)
        def _():
            fetch(0, 0).start()

        @pl.when(step + 1 < total)
        def _():
            fetch(step + 1, (step + 1) % 2).start()

        fetch(step, step % 2).wait()
        o_ref[...] = (scr[step % 2].astype(F32) + t_ref[...].astype(F32)).astype(BF16)

    blk = pl.BlockSpec((None, tm, tc), lambda k, i, j, p: (k, i, j))
    return pl.pallas_call(
        body, name=f"pair_sum_{name}",
        grid_spec=pltpu.PrefetchScalarGridSpec(
            num_scalar_prefetch=1, grid=(N_CHIPS, ni, nj), in_specs=[ANY, blk], out_specs=blk,
            scratch_shapes=[pltpu.VMEM((2, tm, tc), BF16), pltpu.SemaphoreType.DMA((2,))]),
        out_shape=jax.ShapeDtypeStruct(theirs.shape, BF16),
        compiler_params=_cp(cfg, ("arbitrary",) * 3),
    )(pos, grad, theirs)


def scatter_start(cfg, pres, after):
    names = list(pres)
    n = len(names)

    def body(*refs):
        srcs, lands = refs[:n], refs[n:2 * n]
        ssem, rsem = refs[2 * n + 1], refs[2 * n + 2]
        token = refs[-1]
        x, y, c = _me()
        for i in range(n):
            for j, (px, py) in enumerate(_other_chips(x, y)):
                _remote(srcs[i].at[2 * px + py], lands[i].at[j], ssem.at[3 * i + j], rsem.at[3 * i + j], (px, py, c)).start()
        token[...] = jnp.zeros_like(token)

    lands = [lax.empty((3,) + pres[k].shape[1:], BF16) for k in names]
    ops = [_hbm(a) for a in [pres[k] for k in names] + lands]
    outs = pl.pallas_call(
        body, name="scatter_start_" + "_".join(names),
        in_specs=[HBM] * (2 * n) + [ANY],
        out_specs=[SEM, SEM] + [HBM] * (2 * n) + [TOKEN],
        out_shape=[pltpu.SemaphoreType.DMA((3 * n,)), pltpu.SemaphoreType.DMA((3 * n,))]
        + [pltpu.HBM(a.shape, a.dtype) for a in ops] + [jax.ShapeDtypeStruct((8, LANE), F32)],
        input_output_aliases={i: 2 + i for i in range(2 * n)},
        compiler_params=pltpu.CompilerParams(has_side_effects=EFFECT),
    )(*ops, after)
    return outs[0], outs[1], dict(zip(names, outs[2:2 + n])), dict(zip(names, outs[2 + n:2 + 2 * n])), outs[-1]


def scatter_wait(cfg, ssem, rsem, pres, lands, after):
    names = list(pres)
    n = len(names)

    def body(*refs):
        srcs, lands_ = refs[:n], refs[n:2 * n]
        ssem_, rsem_ = refs[2 * n], refs[2 * n + 1]
        x, y, c = _me()
        for i in range(n):
            for j, (px, py) in enumerate(_other_chips(x, y)):
                cp = _remote(srcs[i].at[2 * px + py], lands_[i].at[j], ssem_.at[3 * i + j], rsem_.at[3 * i + j], (px, py, c))
                cp.wait_send()
                cp.wait_recv()

    ops = [pres[k] for k in names] + [lands[k] for k in names]
    outs = pl.pallas_call(
        body, name="scatter_wait_" + "_".join(names),
        in_specs=[HBM] * (2 * n) + [SEM, SEM, ANY], out_specs=[HBM] * (2 * n),
        out_shape=[pltpu.HBM(a.shape, a.dtype) for a in ops],
        input_output_aliases={i: i for i in range(2 * n)},
        compiler_params=pltpu.CompilerParams(has_side_effects=EFFECT),
    )(*ops, ssem, rsem, after)
    return dict(zip(names, outs[:n])), dict(zip(names, outs[n:]))


def sum_landed(cfg, name, pre, land, pos):
    _, r, c = pre.shape
    tm, tc = _tile2(r, c)
    nrt = r // tm

    def body(pos_ref, p_ref, l_ref, o_ref):
        acc = p_ref[...].astype(F32)
        for j in range(3):
            acc = acc + l_ref[j].astype(F32)
        o_ref[...] = acc

    return pl.pallas_call(
        body, name=f"sum_landed_{name}",
        grid_spec=pltpu.PrefetchScalarGridSpec(
            num_scalar_prefetch=1, grid=(nrt, c // tc),
            in_specs=[pl.BlockSpec((None, tm, tc), lambda i, j, p: (p[0], i, j)),
                      pl.BlockSpec((3, tm, tc), lambda i, j, p: (0, i, j))],
            out_specs=pl.BlockSpec((tm, tc), lambda i, j, p: (p[1] * nrt + i, j))),
        out_shape=jax.ShapeDtypeStruct((2 * r, c), F32), compiler_params=_cp(cfg, ("parallel", "parallel")),
    )(pos, pre, land)


def half_swap(cfg, sums):
    names = list(sums)
    n = len(names)

    def body(*refs):
        outs = refs[n:2 * n]
        ssem, rsem = refs[2 * n:]
        x, y, c = _me()
        cps = [_remote(_rows_half(outs[i], c), _rows_half(outs[i], c), ssem.at[i], rsem.at[i], (x, y, 1 - c))
               for i in range(n)]
        for cp in cps:
            cp.start()
        for i in range(n):
            theirs = _rows_half(outs[i], 1 - c)
            _remote(theirs, theirs, ssem.at[i], rsem.at[i], (x, y, 1 - c)).wait_recv()
        for cp in cps:
            cp.wait_send()

    outs = pl.pallas_call(
        body, name="half_swap_" + "_".join(names), in_specs=[ANY] * n, out_specs=[ANY] * n,
        out_shape=[jax.ShapeDtypeStruct(sums[k].shape, F32) for k in names],
        input_output_aliases={i: i for i in range(n)},
        scratch_shapes=[pltpu.SemaphoreType.DMA((n,))] * 2,
    )(*[sums[k] for k in names])
    return dict(zip(names, outs))


class MeshWeights:
    def __init__(self, cfg, w_sh):
        self.cfg = cfg
        self.pos = jnp.stack([2 * lax.axis_index("x") + lax.axis_index("y"), lax.axis_index("c")]).astype(jnp.int32)
        self.full = {k: cast_into(cfg, k, w_sh[k], self.pos) for k in ("w_in",) + REST}
        self.conv_w = pad_conv_w(cfg, w_sh["conv_w"])
        self.inflight = {}
        self.grads = {}

    def weights_first(self):
        w_in, conv_w, self.token = gather_first(self.cfg, self.full["w_in"], self.conv_w)
        return w_in, conv_w

    def start_rest(self):
        out = gather_start(self.cfg, [self.full[k] for k in REST], self.token)
        self.rest = out[:3]
        return out[3]

    def weights_rest(self, after):
        return gather_finish(self.cfg, gather_wait(self.cfg, *self.rest, after))

    def reduce_start(self, grads):
        theirs = pair_send(self.cfg, grads)
        pres = {k: pair_sum(self.cfg, k, grads[k], theirs[k], self.pos) for k in grads}
        out = scatter_start(self.cfg, pres, jnp.zeros((8, LANE), F32))
        self.inflight[tuple(grads)] = out[:4]
        return out[4]

    def reduce_wait(self, names, after):
        cfg = self.cfg
        pres, lands = scatter_wait(cfg, *self.inflight.pop(tuple(names)), after)
        sums = {k: sum_landed(cfg, k, pres[k], lands[k], self.pos) for k in names}
        self.grads.update(half_swap(cfg, sums))


def allreduce_small(cfg, vec):
    R = vec.shape[0]

    def body(v_ref, o_ref, buf, send_sems, recv_sems):
        x, y, c = _me()
        me = 4 * x + 2 * y + c
        buf[me] = v_ref[...]
        sends = []
        for k in range(1, N_DEV):
            px, py, pc = x ^ (k >> 2), y ^ ((k >> 1) & 1), c ^ (k & 1)
            sends.append(pltpu.make_async_remote_copy(
                src_ref=v_ref, dst_ref=buf.at[me], send_sem=send_sems.at[k], recv_sem=recv_sems.at[k],
                device_id=(px, py, pc), device_id_type=MESH))
        for cp in sends:
            cp.start()
        for k in range(1, N_DEV):
            px, py, pc = x ^ (k >> 2), y ^ ((k >> 1) & 1), c ^ (k & 1)
            pltpu.make_async_remote_copy(
                src_ref=v_ref, dst_ref=buf.at[4 * px + 2 * py + pc], send_sem=send_sems.at[k],
                recv_sem=recv_sems.at[k], device_id=(px, py, pc), device_id_type=MESH).wait_recv()
        for cp in sends:
            cp.wait_send()
        acc = buf[0]
        for j in range(1, N_DEV):
            acc = acc + buf[j]
        o_ref[...] = acc

    return pl.pallas_call(
        body, name="allreduce_small",
        in_specs=[pl.BlockSpec(memory_space=pltpu.VMEM)], out_specs=pl.BlockSpec(memory_space=pltpu.VMEM),
        out_shape=jax.ShapeDtypeStruct((R, LANE), F32),
        scratch_shapes=[pltpu.VMEM((N_DEV, R, LANE), F32), pltpu.SemaphoreType.DMA((N_DEV,)),
                        pltpu.SemaphoreType.DMA((N_DEV,))],
    )(vec)


def adamw(cfg, name, w, m, v, g_parts, tile):
    r, c = w.shape
    tm, tc = tile[0] or r, tile[1] or c
    assert tc == c or all(g.shape[1] == c for g in g_parts)
    n = len(g_parts)
    bc1 = 1.0 - ADAM_B1 ** ADAM_STEP
    bc2 = 1.0 - ADAM_B2 ** ADAM_STEP

    def body(*refs):
        w_ref, m_ref, v_ref = refs[:3]
        g_refs = refs[3:3 + n]
        g_out, d_out, m_out, v_out = refs[3 + n:]
        g = g_refs[0][:, :tc]
        for gr in g_refs[1:]:
            g = g + gr[:, :tc]
        m_new = ADAM_B1 * m_ref[...] + (1.0 - ADAM_B1) * g
        v_new = ADAM_B2 * v_ref[...] + (1.0 - ADAM_B2) * jnp.square(g)
        m_hat = m_new / bc1
        v_hat = v_new / bc2
        g_out[...] = g
        d_out[...] = -ADAM_LR * (m_hat / (jnp.sqrt(v_hat) + ADAM_EPS) + ADAM_WD * w_ref[...])
        m_out[...] = m_new
        v_out[...] = v_new

    blk = pl.BlockSpec((tm, tc), lambda i, j: (i, j))
    return pl.pallas_call(
        body, name=f"adamw_{name}", grid=(r // tm, c // tc),
        in_specs=[blk] * 3 + [pl.BlockSpec((tm, tc if tc < c else g.shape[1]), lambda i, j: (i, j)) for g in g_parts],
        out_specs=[blk] * 4, out_shape=[jax.ShapeDtypeStruct((r, c), F32)] * 4,
        compiler_params=_cp(cfg, ("parallel", "parallel")),
    )(w, m, v, *g_parts)


SMALL_ORDER = ("loss", "g1", "g2", "g3", "g4", "g_sb", "g_dl", "conv_b", "conv_w")


def pack_small(small):
    rows = []
    for k in SMALL_ORDER:
        a = small[k].reshape(-1, LANE)
        rows.append(a)
    flat = jnp.concatenate(rows, axis=0)
    pad = (-flat.shape[0]) % 8
    return jnp.pad(flat, ((0, pad), (0, 0))), [r.shape[0] for r in rows]


def unpack_small(red, small, counts):
    out, at = {}, 0
    for k, n in zip(SMALL_ORDER, counts):
        out[k] = red[at:at + n].reshape(small[k].shape)
        at += n
    return out


def pad_ff(cfg, a):
    r = a.shape[0]
    return jnp.pad(a.reshape(r, N_CHIPS, cfg.FSH), ((0, 0), (0, 0), (0, cfg.FSHP - cfg.FSH))).reshape(r, cfg.FF2P)


def step(cfg, x, target, gains, w_sh, conv_b, m_all, v_all):
    chip = 2 * lax.axis_index("x") + lax.axis_index("y")
    comm = MeshWeights(cfg, w_sh)
    grad_x, small = local_step(cfg, comm, x, target, gains["g1"], gains["g2"], gains["g3"], gains["g4"],
                               gains["g_sb"], gains["g_dl"], pad_ff(cfg, conv_b))

    packed, counts = pack_small(small)
    red = unpack_small(allreduce_small(cfg, packed), small, counts)

    names = ("w_in", "w_out", "w_up", "w_down")
    tms = dict(w_in=(cfg.TM, None), w_out=(cfg.TM, None), w_up=(cfg.TM // 2, None), w_down=(None, cfg.TN // 2))
    res = {}
    for n in names:
        res[n] = adamw(cfg, n, w_sh[n], m_all[n], v_all[n], [comm.grads[n]], tms[n])
    g_cw = lax.dynamic_slice_in_dim(red["conv_w"].reshape(3, N_CHIPS, cfg.FSHP), chip, 1, axis=1)[:, 0, :cfg.FSH]
    res["conv_w"] = adamw(cfg, "conv_w", w_sh["conv_w"], m_all["conv_w"], v_all["conv_w"], [g_cw], (None, None))
    g_cb = red["conv_b"].reshape(1, N_CHIPS, cfg.FSHP)[:, :, :cfg.FSH].reshape(1, N_CHIPS * cfg.FSH)
    res["conv_b"] = adamw(cfg, "conv_b", conv_b, m_all["conv_b"], v_all["conv_b"], [g_cb], (None, None))
    for k in ("g1", "g2", "g3", "g4", "g_sb", "g_dl"):
        res[k] = adamw(cfg, k, gains[k], m_all[k], v_all[k], [red[k]], (None, None))
    return red["loss"][0, 0], grad_x, res


PARAMS = ("pre_mix_gain", "post_mix_gain", "pre_ffn_gain", "post_ffn_gain", "w_in", "sb_out_gain", "dil_out_gain",
          "w_out", "w_up", "conv_w", "conv_b", "w_down")
SHORT = dict(pre_mix_gain="g1", post_mix_gain="g2", pre_ffn_gain="g3", post_ffn_gain="g4", sb_out_gain="g_sb",
             dil_out_gain="g_dl", w_in="w_in", w_out="w_out", w_up="w_up", conv_w="conv_w", conv_b="conv_b",
             w_down="w_down")


def kernel(x, pre_mix_gain, post_mix_gain, pre_ffn_gain, post_ffn_gain, w_in, sb_out_gain, dil_out_gain, w_out, w_up, conv_w, conv_b, w_down, loss_target, m_pre_mix_gain, m_post_mix_gain, m_pre_ffn_gain, m_post_ffn_gain, m_w_in, m_sb_out_gain, m_dil_out_gain, m_w_out, m_w_up, m_conv_w, m_conv_b, m_w_down, v_pre_mix_gain, v_post_mix_gain, v_pre_ffn_gain, v_post_ffn_gain, v_w_in, v_sb_out_gain, v_dil_out_gain, v_w_out, v_w_up, v_conv_w, v_conv_b, v_w_down):
    cfg = CFG
    w = dict(zip(PARAMS, (pre_mix_gain, post_mix_gain, pre_ffn_gain, post_ffn_gain, w_in, sb_out_gain, dil_out_gain,
                          w_out, w_up, conv_w, conv_b, w_down)))
    m = dict(zip(PARAMS, (m_pre_mix_gain, m_post_mix_gain, m_pre_ffn_gain, m_post_ffn_gain, m_w_in, m_sb_out_gain,
                          m_dil_out_gain, m_w_out, m_w_up, m_conv_w, m_conv_b, m_w_down)))
    v = dict(zip(PARAMS, (v_pre_mix_gain, v_post_mix_gain, v_pre_ffn_gain, v_post_ffn_gain, v_w_in, v_sb_out_gain,
                          v_dil_out_gain, v_w_out, v_w_up, v_conv_w, v_conv_b, v_w_down)))
    sq = lambda a: a.reshape(a.shape[1:])
    ws = {SHORT[k]: sq(a) if a.ndim == 3 else a for k, a in w.items()}
    ms = {SHORT[k]: sq(a) if a.ndim == 3 else a for k, a in m.items()}
    vs = {SHORT[k]: sq(a) if a.ndim == 3 else a for k, a in v.items()}
    gains = {k: ws[k] for k in ("g1", "g2", "g3", "g4", "g_sb", "g_dl")}
    w_sh = {k: ws[k] for k in ("w_in", "w_out", "w_up", "conv_w", "w_down")}
    loss, grad_x, res = step(cfg, sq(x), sq(loss_target), gains, w_sh, ws["conv_b"], ms, vs)
    outs = [loss, grad_x.reshape(x.shape)]
    for i in range(4):
        for k in PARAMS:
            outs.append(res[SHORT[k]][i].reshape(w[k].shape))
    return tuple(outs)
```

```python
import functools
import math
from typing import NamedTuple

import jax
import jax.numpy as jnp
from jax import lax
from jax.experimental import pallas as pl
from jax.experimental.pallas import tpu as pltpu

F32 = jnp.float32
BF16 = jnp.bfloat16
MESH = pl.DeviceIdType.MESH

ROPE_THETA = 10000.0
RMS_EPS = 1e-6
ADAM_LR = 0.001
ADAM_B1 = 0.9
ADAM_B2 = 0.999
ADAM_EPS = 1e-08
ADAM_WD = 0.01
ADAM_STEP = 10
GELU_C = math.sqrt(2.0 / math.pi)
NEG_BIG = -1e30
LANE = 128
N_CHIPS = 4
N_DEV = 8


class Cfg(NamedTuple):
    S: int = 2048
    D: int = 2048
    DH: int = 128
    HSB: int = 8
    HDL: int = 8
    QB: int = 128
    SBT: int = 256
    branches: tuple = ((128, 1), (512, 4), (2048, 16))
    FSH: int = 2752
    FSHP: int = 2816
    TM: int = 256
    TNF: int = 256
    TN: int = 512
    VMEM_MB: int = 56

    @property
    def DSB(self):
        return self.HSB * self.DH

    @property
    def DDL(self):
        return self.HDL * self.DH

    @property
    def DMIX(self):
        return self.DSB + self.DDL

    @property
    def FFP(self):
        return 2 * self.FSHP

    @property
    def FF2P(self):
        return 4 * self.FSHP


CFG = Cfg()


def _cp(cfg, sem=None):
    return pltpu.CompilerParams(dimension_semantics=sem, vmem_limit_bytes=cfg.VMEM_MB * 2**20)


def _dot(a, b):
    return jnp.dot(a, b, preferred_element_type=F32)


def _dot_nt(a, b):
    return lax.dot_general(a, b, (((1,), (1,)), ((), ())), preferred_element_type=F32)


def _dot_tn(a, b):
    return lax.dot_general(a, b, (((0,), (0,)), ((), ())), preferred_element_type=F32)


def _dot_split(x, u):
    hi = x.astype(BF16)
    lo = (x - hi.astype(F32)).astype(BF16)
    return _dot(hi, u) + _dot(lo, u)


def _rstd(x):
    return lax.rsqrt(jnp.mean(x * x, axis=-1, keepdims=True) + RMS_EPS)


def _rms_bwd(dy, x, g):
    r = _rstd(x)
    xh = x * r
    dxh = dy * g
    dx = r * (dxh - xh * jnp.mean(dxh * xh, axis=-1, keepdims=True))
    return dx, dy * xh


def _gelu(x):
    t = jnp.tanh(GELU_C * (x + 0.044715 * (x * x * x)))
    return 0.5 * x * (1.0 + t), t


def _gelu_grad(x, t):
    return 0.5 * (1.0 + t) + 0.5 * x * (1.0 - t * t) * (GELU_C * (1.0 + 3 * 0.044715 * (x * x)))


def _row(cfg, w):
    return pl.BlockSpec((cfg.TM, w), lambda i: (i, 0))


def _vec(w):
    return pl.BlockSpec((1, w), lambda i: (0, 0))


def rms_fwd(cfg, x, g):
    S, D = x.shape

    def body(x_ref, g_ref, h_ref):
        xv = x_ref[...]
        h_ref[...] = (xv * _rstd(xv) * g_ref[...]).astype(BF16)

    return pl.pallas_call(
        body, name="rms_fwd", grid=(S // cfg.TM,),
        in_specs=[_row(cfg, D), _vec(D)], out_specs=_row(cfg, D),
        out_shape=jax.ShapeDtypeStruct((S, D), BF16), compiler_params=_cp(cfg, ("parallel",)),
    )(x, g)


def mid_fwd(cfg, x, mo, g_post, g_pre):
    S, D = x.shape

    def body(x_ref, mo_ref, gp_ref, gn_ref, x1_ref, h2_ref):
        mo_v = mo_ref[...]
        x1 = x_ref[...] + mo_v * _rstd(mo_v) * gp_ref[...]
        x1_ref[...] = x1
        h2_ref[...] = (x1 * _rstd(x1) * gn_ref[...]).astype(BF16)

    return pl.pallas_call(
        body, name="mid_fwd", grid=(S // cfg.TM,),
        in_specs=[_row(cfg, D), _row(cfg, D), _vec(D), _vec(D)],
        out_specs=[_row(cfg, D), _row(cfg, D)],
        out_shape=[jax.ShapeDtypeStruct((S, D), F32), jax.ShapeDtypeStruct((S, D), BF16)],
        compiler_params=_cp(cfg, ("parallel",)),
    )(x, mo, g_post, g_pre)


def final_fwd_bwd(cfg, x1, f, g_post, target):
    S, D = x1.shape

    def body(x1_ref, f_ref, g_ref, t_ref, dout_ref, df_ref, dg_ref, loss_ref):
        @pl.when(pl.program_id(0) == 0)
        def _():
            dg_ref[...] = jnp.zeros_like(dg_ref)
            loss_ref[...] = jnp.zeros_like(loss_ref)

        fv = f_ref[...]
        g = g_ref[...]
        out = x1_ref[...] + fv * _rstd(fv) * g
        err = out - t_ref[...]
        loss_ref[...] += 0.5 * jnp.sum(jnp.mean(err * err, axis=-1, keepdims=True), axis=0, keepdims=True)
        dout = err * (1.0 / D)
        dout_ref[...] = dout
        df, dgx = _rms_bwd(dout, fv, g)
        df_ref[...] = df.astype(BF16)
        dg_ref[...] += jnp.sum(dgx, axis=0, keepdims=True)

    return pl.pallas_call(
        body, name="final_fwd_bwd", grid=(S // cfg.TM,),
        in_specs=[_row(cfg, D), _row(cfg, D), _vec(D), _row(cfg, D)],
        out_specs=[_row(cfg, D), _row(cfg, D), _vec(D), _vec(LANE)],
        out_shape=[jax.ShapeDtypeStruct((S, D), F32), jax.ShapeDtypeStruct((S, D), BF16),
                   jax.ShapeDtypeStruct((1, D), F32), jax.ShapeDtypeStruct((1, LANE), F32)],
        compiler_params=_cp(cfg, ("arbitrary",)),
    )(x1, f, g_post, target)


def mid_bwd(cfg, dh2, x1, g_pre, dout, mo, g_post):
    S, D = x1.shape

    def body(dh_ref, x1_ref, gn_ref, do_ref, mo_ref, gp_ref, dx1_ref, dmo_ref, dgn_ref, dgp_ref):
        @pl.when(pl.program_id(0) == 0)
        def _():
            dgn_ref[...] = jnp.zeros_like(dgn_ref)
            dgp_ref[...] = jnp.zeros_like(dgp_ref)

        dx, dgx = _rms_bwd(dh_ref[...], x1_ref[...], gn_ref[...])
        dx1 = do_ref[...] + dx
        dx1_ref[...] = dx1
        dgn_ref[...] += jnp.sum(dgx, axis=0, keepdims=True)
        dmo, dgy = _rms_bwd(dx1, mo_ref[...], gp_ref[...])
        dmo_ref[...] = dmo.astype(BF16)
        dgp_ref[...] += jnp.sum(dgy, axis=0, keepdims=True)

    return pl.pallas_call(
        body, name="mid_bwd", grid=(S // cfg.TM,),
        in_specs=[_row(cfg, D), _row(cfg, D), _vec(D), _row(cfg, D), _row(cfg, D), _vec(D)],
        out_specs=[_row(cfg, D), _row(cfg, D), _vec(D), _vec(D)],
        out_shape=[jax.ShapeDtypeStruct((S, D), F32), jax.ShapeDtypeStruct((S, D), BF16),
                   jax.ShapeDtypeStruct((1, D), F32), jax.ShapeDtypeStruct((1, D), F32)],
        compiler_params=_cp(cfg, ("arbitrary",)),
    )(dh2, x1, g_pre, dout, mo, g_post)


def first_bwd(cfg, dh1, x, g_pre, dx1):
    S, D = x.shape

    def body(dh_ref, x_ref, g_ref, r_ref, dx_ref, dg_ref):
        @pl.when(pl.program_id(0) == 0)
        def _():
            dg_ref[...] = jnp.zeros_like(dg_ref)

        dx, dgx = _rms_bwd(dh_ref[...], x_ref[...], g_ref[...])
        dx_ref[...] = r_ref[...] + dx
        dg_ref[...] += jnp.sum(dgx, axis=0, keepdims=True)

    return pl.pallas_call(
        body, name="first_bwd", grid=(S // cfg.TM,),
        in_specs=[_row(cfg, D), _row(cfg, D), _vec(D), _row(cfg, D)],
        out_specs=[_row(cfg, D), _vec(D)],
        out_shape=[jax.ShapeDtypeStruct((S, D), F32), jax.ShapeDtypeStruct((1, D), F32)],
        compiler_params=_cp(cfg, ("arbitrary",)),
    )(dh1, x, g_pre, dx1)


def _mm(cfg, name, a, b, *, nt, a_spec, b_spec, o_spec, grid, out_shape, acc_shape, dep=None):
    nk = grid[-1]
    dot = _dot_nt if nt else _dot
    deps = [] if dep is None else [dep]

    def body(a_ref, b_ref, *rest):
        o_ref, acc_ref = rest[-2:]
        k = pl.program_id(len(grid) - 1)
        part = dot(a_ref[...], b_ref[...])
        if deps:
            part = part + rest[0][0:1, 0:1]
        if nk == 1:
            o_ref[...] = part.astype(o_ref.dtype)
            return

        @pl.when(k == 0)
        def _():
            acc_ref[...] = part

        @pl.when(k > 0)
        def _():
            acc_ref[...] += part

        @pl.when(k == nk - 1)
        def _():
            o_ref[...] = acc_ref[...].astype(o_ref.dtype)

    sem = ("parallel",) * (len(grid) - 1) + ("arbitrary",)
    dep_specs = [pl.BlockSpec((8, LANE), lambda *_: (0, 0))] * len(deps)
    return pl.pallas_call(
        body, name=name, grid=grid, in_specs=[a_spec, b_spec] + dep_specs, out_specs=o_spec, out_shape=out_shape,
        scratch_shapes=[pltpu.VMEM(acc_shape, F32)], compiler_params=_cp(cfg, sem),
    )(a, b, *deps)


def _mm_tn(cfg, name, a, b, *, a_spec, b_spec, o_spec, grid, out_shape):
    def body(a_ref, b_ref, o_ref):
        o_ref[...] = _dot_tn(a_ref[...], b_ref[...]).astype(o_ref.dtype)

    return pl.pallas_call(
        body, name=name, grid=grid, in_specs=[a_spec, b_spec], out_specs=o_spec, out_shape=out_shape,
        compiler_params=_cp(cfg, ("parallel",) * len(grid)),
    )(a, b)


def qkv_proj(cfg, h1, w_in, cos2, sin2):
    S, D = h1.shape
    tn = 2 * cfg.DH
    per = cfg.DSB // tn
    assert cfg.DSB == cfg.DDL
    nblk = 6 * per

    def body(a_ref, b_ref, c_ref, s_ref, o_ref):
        j = pl.program_id(0)
        acc = _dot(a_ref[...], b_ref[...])
        rope = jnp.logical_and(j >= 3 * per, j < 5 * per)

        @pl.when(rope)
        def _():
            for c in range(tn // cfg.DH):
                xh = acc[:, c * cfg.DH:(c + 1) * cfg.DH]
                o_ref[:, c * cfg.DH:(c + 1) * cfg.DH] = (
                    xh * c_ref[...] + pltpu.roll(xh, cfg.DH // 2, 1) * s_ref[...]).astype(BF16)

        @pl.when(jnp.logical_not(rope))
        def _():
            o_ref[...] = acc.astype(BF16)

    return pl.pallas_call(
        body, name="qkv_proj", grid=(nblk,),
        in_specs=[pl.BlockSpec((S, D), lambda j: (0, 0)), pl.BlockSpec((D, tn), lambda j: (0, j)),
                  pl.BlockSpec((S, cfg.DH), lambda j: (0, 0)), pl.BlockSpec((S, cfg.DH), lambda j: (0, 0))],
        out_specs=pl.BlockSpec((None, S, tn), lambda j: (j // per, 0, j % per)),
        out_shape=jax.ShapeDtypeStruct((6, S, cfg.DSB), BF16),
        compiler_params=_cp(cfg, ("parallel",)),
    )(h1, w_in, cos2, sin2)


def _sb_tile(cfg, q, k, kb, qb):
    QB = cfg.SBT
    z = _dot_nt(q, k) * (cfg.DH ** -0.5)
    t1 = jnp.log1p(jnp.exp(-jnp.abs(z)))
    lb = jnp.minimum(z, 0.0) - t1
    row = lax.broadcasted_iota(jnp.int32, (QB, QB), 0)
    col = lax.broadcasted_iota(jnp.int32, (QB, QB), 1)
    valid = jnp.logical_or(kb < qb, col < row)
    lk = jnp.where(valid, jnp.minimum(-z, 0.0) - t1, 0.0)
    return lb, lk, valid


def sb_fwd(cfg, qkv3):
    S, QB, DH = cfg.S, cfg.SBT, cfg.DH

    def body(q_ref, k_ref, v_ref, o_ref, t_ref):
        row = lax.broadcasted_iota(jnp.int32, (QB, QB), 0)
        col = lax.broadcasted_iota(jnp.int32, (QB, QB), 1)
        u_after = (row > col).astype(BF16)

        def q_loop(qb, _):
            rows = pl.ds(pl.multiple_of(qb * QB, QB), QB)
            q = q_ref[rows, :]

            def k_loop(i, carry):
                o_acc, c = carry
                kb = qb - i
                krows = pl.ds(pl.multiple_of(kb * QB, QB), QB)
                lb, lk, valid = _sb_tile(cfg, q, k_ref[krows, :], kb, qb)
                rem = _dot_split(lk, u_after) + c
                a = jnp.where(valid, jnp.exp(lb + rem), 0.0)
                o_acc = o_acc + _dot(a.astype(BF16), v_ref[krows, :])
                return o_acc, c + jnp.sum(lk, axis=1, keepdims=True)

            o_acc, c = lax.fori_loop(0, qb + 1, k_loop, (jnp.zeros((QB, DH), F32), jnp.zeros((QB, 1), F32)))
            o_ref[rows, :] = o_acc
            t_ref[rows, :] = jnp.broadcast_to(c, (QB, DH))
            return 0

        lax.fori_loop(0, S // QB, q_loop, 0)

    def spec(i):
        return pl.BlockSpec((None, S, DH), lambda h: (i, 0, h))

    return pl.pallas_call(
        body, name="sb_fwd", grid=(cfg.HSB,),
        in_specs=[spec(0), spec(1), spec(2)],
        out_specs=[pl.BlockSpec((S, DH), lambda h: (0, h))] * 2,
        out_shape=[jax.ShapeDtypeStruct((S, cfg.DSB), F32)] * 2,
        compiler_params=_cp(cfg, ("parallel",)),
    )(qkv3, qkv3, qkv3)


def sb_bwd(cfg, qkv3, do_sb, tsum):
    S, QB, DH = cfg.S, cfg.SBT, cfg.DH
    scale = DH ** -0.5

    def body(q_ref, k_ref, v_ref, do_ref, t_ref, d_ref, dk_acc, dv_acc):
        dk_acc[...] = jnp.zeros_like(dk_acc)
        dv_acc[...] = jnp.zeros_like(dv_acc)
        row = lax.broadcasted_iota(jnp.int32, (QB, QB), 0)
        col = lax.broadcasted_iota(jnp.int32, (QB, QB), 1)
        u_upto = (row <= col).astype(BF16)
        u_before = (row < col).astype(BF16)

        def q_loop(qb, _):
            rows = pl.ds(pl.multiple_of(qb * QB, QB), QB)
            q = q_ref[rows, :]
            do = do_ref[rows, :]
            total = t_ref[rows, 0:1]

            def k_loop(kb, carry):
                dq_acc, pc, gc = carry
                krows = pl.ds(pl.multiple_of(kb * QB, QB), QB)
                k = k_ref[krows, :]
                v = v_ref[krows, :]
                lb, lk, valid = _sb_tile(cfg, q, k, kb, qb)
                rem = total - pc - _dot_split(lk, u_upto)
                a = jnp.where(valid, jnp.exp(lb + rem), 0.0)
                g = a * _dot_nt(do, v)
                dv_acc[krows, :] += _dot_tn(a.astype(BF16), do)
                cum = gc + _dot(g.astype(BF16), u_before)
                sig = jnp.exp(lb)
                dz = (jnp.where(valid, g * (1.0 - sig) - cum * sig, 0.0) * scale).astype(BF16)
                dq_acc = dq_acc + _dot(dz, k)
                dk_acc[krows, :] += _dot_tn(dz, q)
                return dq_acc, pc + jnp.sum(lk, axis=1, keepdims=True), gc + jnp.sum(g, axis=1, keepdims=True)

            z1 = jnp.zeros((QB, 1), F32)
            dq_acc, _, _ = lax.fori_loop(0, qb + 1, k_loop, (jnp.zeros((QB, DH), F32), z1, z1))
            d_ref[0, rows, :] = dq_acc.astype(BF16)
            return 0

        lax.fori_loop(0, S // QB, q_loop, 0)
        d_ref[1, :, :] = dk_acc[...].astype(BF16)
        d_ref[2, :, :] = dv_acc[...].astype(BF16)

    def spec(i):
        return pl.BlockSpec((None, S, DH), lambda h: (i, 0, h))

    return pl.pallas_call(
        body, name="sb_bwd", grid=(cfg.HSB,),
        in_specs=[spec(0), spec(1), spec(2), pl.BlockSpec((S, DH), lambda h: (0, h)),
                  pl.BlockSpec((S, DH), lambda h: (0, h))],
        out_specs=pl.BlockSpec((3, S, DH), lambda h: (0, 0, h)),
        out_shape=jax.ShapeDtypeStruct((6, S, cfg.DSB), BF16),
        scratch_shapes=[pltpu.VMEM((S, DH), F32), pltpu.VMEM((S, DH), F32)],
        compiler_params=_cp(cfg, ("parallel",)),
    )(qkv3, qkv3, qkv3, do_sb, tsum)


def _band_mask(cfg, n, n_back):
    QB = cfg.QB
    qi = lax.broadcasted_iota(jnp.int32, (QB, 2 * QB), 0)
    kj = lax.broadcasted_iota(jnp.int32, (QB, 2 * QB), 1)
    dist = QB + qi - kj
    return (dist >= 0) & (dist <= n_back) & jnp.logical_or(n > 0, kj >= QB)


def _sub_rows(start, n, dil):
    if dil > 1:
        return pl.ds(start, n, stride=dil)
    return pl.ds(start if isinstance(start, int) else pl.multiple_of(start, 8), n)


def _stage_residues(cfg, dil, pairs):
    QB, L = cfg.QB, cfg.S // dil
    for src, dst in pairs:
        for r in range(dil):
            dst[pl.ds(r * (QB + L), QB), :] = jnp.zeros((QB, cfg.DH), BF16)
            dst[pl.ds(r * (QB + L) + QB, L), :] = src[_sub_rows(r, L, dil), :].astype(BF16)


def _staged_rows(cfg):
    return cfg.S + cfg.QB * max(d for _, d in cfg.branches)


def _lane_value(x):
    return jnp.max(x, axis=1, keepdims=True)


def dil_fwd(cfg, qkv3):
    S, QB, DH = cfg.S, cfg.QB, cfg.DH
    scale = DH ** -0.5
    nb = len(cfg.branches)
    mix_rows = min(256, S)

    def body(q_ref, k_ref, v_ref, o_ref, lt_ref, qf, kf, vf, kp, vp, *obl):
        obs, lbs = obl[:nb], obl[nb:]
        qf[...] = q_ref[...].astype(F32)
        kf[...] = k_ref[...].astype(F32)
        vf[...] = v_ref[...].astype(F32)
        for b, (window, dil) in enumerate(cfg.branches):
            L, n_back = S // dil, window // dil
            assert n_back <= QB and L % QB == 0
            _stage_residues(cfg, dil, [(kf, kp), (vf, vp)])
            for r in range(dil):
                for n in range(L // QB):
                    rows = _sub_rows(r + n * (QB * dil), QB, dil)
                    band = pl.ds(r * (QB + L) + n * QB, 2 * QB)
                    s = _dot_nt(qf[rows, :].astype(BF16), kp[band, :]) * scale
                    s = jnp.where(_band_mask(cfg, n, n_back), s, NEG_BIG)
                    m = jnp.max(s, axis=1, keepdims=True)
                    p = jnp.exp(s - m)
                    den = jnp.sum(p, axis=1, keepdims=True)
                    obs[b][rows, :] = _dot(p.astype(BF16), vp[band, :]) / den
                    lbs[b][rows, :] = jnp.broadcast_to(m + jnp.log(den), (QB, DH))

        def mix(i, _):
            rows = pl.ds(pl.multiple_of(i * mix_rows, mix_rows), mix_rows)
            ls = [r[rows, :] for r in lbs]
            m = functools.reduce(jnp.maximum, ls)
            es = [jnp.exp(l - m) for l in ls]
            tot = functools.reduce(jnp.add, es)
            o_ref[rows, :] = functools.reduce(jnp.add, [(e / tot) * r[rows, :] for e, r in zip(es, obs)])
            lt_ref[rows, :] = m + jnp.log(tot)
            return 0

        lax.fori_loop(0, S // mix_rows, mix, 0)

    def spec(i):
        return pl.BlockSpec((None, S, DH), lambda h: (i, 0, h))

    o_spec = pl.BlockSpec((S, DH), lambda h: (0, h))
    return pl.pallas_call(
        body, name="dil_fwd", grid=(cfg.HDL,),
        in_specs=[spec(3), spec(4), spec(5)], out_specs=[o_spec, o_spec],
        out_shape=[jax.ShapeDtypeStruct((S, cfg.DDL), F32)] * 2,
        scratch_shapes=[pltpu.VMEM((S, DH), F32)] * 3 + [pltpu.VMEM((_staged_rows(cfg), DH), BF16)] * 2
        + [pltpu.VMEM((S, DH), F32)] * (2 * nb),
        compiler_params=_cp(cfg, ("parallel",)),
    )(qkv3, qkv3, qkv3)


def dil_bwd(cfg, qkv3, do_dl, delta, lse_tot, cos2, sin2, d_sb3):
    S, QB, DH = cfg.S, cfg.QB, cfg.DH
    scale = DH ** -0.5
    out_rows = min(256, S)

    def body(q_ref, k_ref, v_ref, do_ref, dl_ref, lt_ref, c_ref, s_ref, base_ref, d_ref,
             qf, kf, vf, dof, kp, vp, dkp, dvp, dqn, dkn, dvn):
        qf[...] = q_ref[...].astype(F32)
        kf[...] = k_ref[...].astype(F32)
        vf[...] = v_ref[...].astype(F32)
        dof[...] = do_ref[...].astype(F32)
        for acc in (dqn, dkn, dvn):
            acc[...] = jnp.zeros_like(acc)
        for window, dil in cfg.branches:
            L, n_back = S // dil, window // dil
            reg = QB + L
            _stage_residues(cfg, dil, [(kf, kp), (vf, vp)])
            dkp[pl.ds(0, dil * reg), :] = jnp.zeros((dil * reg, DH), F32)
            dvp[pl.ds(0, dil * reg), :] = jnp.zeros((dil * reg, DH), F32)
            for r in range(dil):
                for n in range(L // QB):
                    rows = _sub_rows(r + n * (QB * dil), QB, dil)
                    band = pl.ds(r * reg + n * QB, 2 * QB)
                    q = qf[rows, :].astype(BF16)
                    do = dof[rows, :].astype(BF16)
                    kb = kp[band, :]
                    s = _dot_nt(q, kb) * scale
                    s = jnp.where(_band_mask(cfg, n, n_back), s, NEG_BIG)
                    p = jnp.exp(s - _lane_value(lt_ref[rows, :]))
                    ds = (p * (_dot_nt(do, vp[band, :]) - _lane_value(dl_ref[rows, :])) * scale).astype(BF16)
                    dqn[rows, :] += _dot(ds, kb)
                    dkp[band, :] += _dot_tn(ds, q)
                    dvp[band, :] += _dot_tn(p.astype(BF16), do)
            for r in range(dil):
                sub = _sub_rows(r, L, dil)
                dkn[sub, :] += dkp[pl.ds(r * reg + QB, L), :]
                dvn[sub, :] += dvp[pl.ds(r * reg + QB, L), :]

        def finish(i, _):
            rows = pl.ds(pl.multiple_of(i * out_rows, out_rows), out_rows)
            c, sn = c_ref[rows, :], s_ref[rows, :]
            for j, acc in enumerate((dqn, dkn)):
                d = acc[rows, :]
                d_ref[j, rows, :] = (d * c + pltpu.roll(d * sn, DH // 2, 1)).astype(BF16)
            d_ref[2, rows, :] = dvn[rows, :].astype(BF16)
            return 0

        lax.fori_loop(0, S // out_rows, finish, 0)

    def spec(i):
        return pl.BlockSpec((None, S, DH), lambda h: (i, 0, h))

    hd = pl.BlockSpec((S, DH), lambda h: (0, h))
    tab = pl.BlockSpec((S, DH), lambda h: (0, 0))
    ns = _staged_rows(cfg)
    return pl.pallas_call(
        body, name="dil_bwd", grid=(cfg.HDL,),
        in_specs=[spec(3), spec(4), spec(5), hd, hd, hd, tab, tab, ANY],
        out_specs=pl.BlockSpec((3, S, DH), lambda h: (1, 0, h)),
        out_shape=jax.ShapeDtypeStruct((6, S, cfg.DDL), BF16),
        input_output_aliases={8: 0},
        scratch_shapes=[pltpu.VMEM((S, DH), F32)] * 4 + [pltpu.VMEM((ns, DH), BF16)] * 2
        + [pltpu.VMEM((ns, DH), F32)] * 2 + [pltpu.VMEM((S, DH), F32)] * 3,
        compiler_params=_cp(cfg, ("parallel",)),
    )(qkv3, qkv3, qkv3, do_dl, delta, lse_tot, cos2, sin2, d_sb3)


def combine_fwd(cfg, o_sb, o_dl, g_sb, g_dl):
    S, DH = cfg.S, cfg.DH

    def head_norm(o, g):
        return o * lax.rsqrt(jnp.mean(o * o, axis=-1, keepdims=True) + RMS_EPS) * g

    def body(osb_ref, odl_ref, gsb_ref, gdl_ref, mix_ref):
        for h in range(cfg.HSB):
            c = slice(h * DH, (h + 1) * DH)
            mix_ref[:, c] = head_norm(osb_ref[:, c], gsb_ref[:, c]).astype(BF16)
        for h in range(cfg.HDL):
            c = slice(h * DH, (h + 1) * DH)
            mix_ref[:, cfg.DSB + h * DH:cfg.DSB + (h + 1) * DH] = head_norm(odl_ref[:, c], gdl_ref[:, c]).astype(BF16)

    return pl.pallas_call(
        body, name="combine_fwd", grid=(S // cfg.TM,),
        in_specs=[_row(cfg, cfg.DSB), _row(cfg, cfg.DDL), _vec(cfg.DSB), _vec(cfg.DDL)],
        out_specs=_row(cfg, cfg.DMIX), out_shape=jax.ShapeDtypeStruct((S, cfg.DMIX), BF16),
        compiler_params=_cp(cfg, ("parallel",)),
    )(o_sb, o_dl, g_sb, g_dl)


def combine_bwd(cfg, dmix, o_sb, o_dl, g_sb, g_dl):
    S, DH = cfg.S, cfg.DH

    def body(dm_ref, osb_ref, odl_ref, gsb_ref, gdl_ref, dsb_ref, ddl_ref, dl_ref, dgsb_ref, dgdl_ref):
        @pl.when(pl.program_id(0) == 0)
        def _():
            dgsb_ref[...] = jnp.zeros_like(dgsb_ref)
            dgdl_ref[...] = jnp.zeros_like(dgdl_ref)

        for h in range(cfg.HSB):
            c = slice(h * DH, (h + 1) * DH)
            dx, dgx = _rms_bwd(dm_ref[:, c], osb_ref[:, c], gsb_ref[:, c])
            dsb_ref[:, c] = dx.astype(BF16)
            dgsb_ref[:, c] += jnp.sum(dgx, axis=0, keepdims=True)
        for h in range(cfg.HDL):
            c = slice(h * DH, (h + 1) * DH)
            o = odl_ref[:, c]
            dx, dgx = _rms_bwd(dm_ref[:, cfg.DSB + h * DH:cfg.DSB + (h + 1) * DH], o, gdl_ref[:, c])
            ddl_ref[:, c] = dx.astype(BF16)
            dl_ref[:, c] = jnp.broadcast_to(jnp.sum(dx * o, axis=-1, keepdims=True), dx.shape)
            dgdl_ref[:, c] += jnp.sum(dgx, axis=0, keepdims=True)

    return pl.pallas_call(
        body, name="combine_bwd", grid=(S // cfg.TM,),
        in_specs=[_row(cfg, cfg.DMIX), _row(cfg, cfg.DSB), _row(cfg, cfg.DDL), _vec(cfg.DSB), _vec(cfg.DDL)],
        out_specs=[_row(cfg, cfg.DSB), _row(cfg, cfg.DDL), _row(cfg, cfg.DDL), _vec(cfg.DSB), _vec(cfg.DDL)],
        out_shape=[jax.ShapeDtypeStruct((S, cfg.DSB), BF16), jax.ShapeDtypeStruct((S, cfg.DDL), BF16),
                   jax.ShapeDtypeStruct((S, cfg.DDL), F32), jax.ShapeDtypeStruct((1, cfg.DSB), F32),
                   jax.ShapeDtypeStruct((1, cfg.DDL), F32)],
        compiler_params=_cp(cfg, ("arbitrary",)),
    )(dmix, o_sb, o_dl, g_sb, g_dl)


def _shift_rows(u, j):
    row = lax.broadcasted_iota(jnp.int32, u.shape, 0)
    return jnp.where(row >= j, pltpu.roll(u, j, 0), 0.0)


def _shift_rows_up(u, j):
    n = u.shape[0]
    row = lax.broadcasted_iota(jnp.int32, u.shape, 0)
    return jnp.where(row < n - j, pltpu.roll(u, n - j, 0), 0.0)


def _conv(u, cw, cb):
    return u * cw[2:3, :] + _shift_rows(u, 1) * cw[1:2, :] + _shift_rows(u, 2) * cw[0:1, :] + cb


def ffn_fwd(cfg, h2, w_up, conv_w, conv_b):
    S, D = h2.shape
    tn, nt = cfg.TNF, cfg.FFP // cfg.TNF

    def body(h_ref, wg_ref, wv_ref, cwg_ref, cwv_ref, cbg_ref, cbv_ref, u_ref, y_ref):
        h = h_ref[...]
        ug = _dot(h, wg_ref[...])
        uv = _dot(h, wv_ref[...])
        u_ref[0] = ug
        u_ref[1] = uv
        gl, _ = _gelu(_conv(ug, cwg_ref[...], cbg_ref[...]))
        y_ref[...] = (gl * _conv(uv, cwv_ref[...], cbv_ref[...])).astype(BF16)

    return pl.pallas_call(
        body, name="ffn_fwd", grid=(nt,),
        in_specs=[pl.BlockSpec((S, D), lambda n: (0, 0)),
                  pl.BlockSpec((D, tn), lambda n: (0, n)), pl.BlockSpec((D, tn), lambda n: (0, n + nt)),
                  pl.BlockSpec((3, tn), lambda n: (0, n)), pl.BlockSpec((3, tn), lambda n: (0, n + nt)),
                  pl.BlockSpec((1, tn), lambda n: (0, n)), pl.BlockSpec((1, tn), lambda n: (0, n + nt))],
        out_specs=[pl.BlockSpec((2, S, tn), lambda n: (0, 0, n)), pl.BlockSpec((S, tn), lambda n: (0, n))],
        out_shape=[jax.ShapeDtypeStruct((2, S, cfg.FFP), F32), jax.ShapeDtypeStruct((S, cfg.FFP), BF16)],
        compiler_params=_cp(cfg, ("parallel",)),
    )(h2, w_up, w_up, conv_w, conv_w, conv_b, conv_b)


def ffn_bwd(cfg, df, w_down, u, conv_w, conv_b):
    S, D = df.shape
    tn, nt = cfg.TNF, cfg.FFP // cfg.TNF

    def conv_bwd(dc, uu, cw):
        du = dc * cw[2:3, :] + _shift_rows_up(dc, 1) * cw[1:2, :] + _shift_rows_up(dc, 2) * cw[0:1, :]
        dws = [jnp.sum(dc * _shift_rows(uu, 2), axis=0, keepdims=True),
               jnp.sum(dc * _shift_rows(uu, 1), axis=0, keepdims=True),
               jnp.sum(dc * uu, axis=0, keepdims=True)]
        return du, dws, jnp.sum(dc, axis=0, keepdims=True)

    def body(df_ref, wd_ref, u_ref, cwg_ref, cwv_ref, cbg_ref, cbv_ref, du_ref, dwd_ref, dcw_ref, dcb_ref):
        dfv = df_ref[...]
        dy = _dot_nt(dfv, wd_ref[...])
        ug, uv = u_ref[0], u_ref[1]
        cwg, cwv = cwg_ref[...], cwv_ref[...]
        cg = _conv(ug, cwg, cbg_ref[...])
        cv = _conv(uv, cwv, cbv_ref[...])
        gl, t = _gelu(cg)
        dwd_ref[...] = _dot_tn((gl * cv).astype(BF16), dfv).astype(BF16)
        dug, dwg, dbg = conv_bwd(dy * cv * _gelu_grad(cg, t), ug, cwg)
        duv, dwv, dbv = conv_bwd(dy * gl, uv, cwv)
        du_ref[0] = dug.astype(BF16)
        du_ref[1] = duv.astype(BF16)
        for j in range(3):
            dcw_ref[0, j:j + 1, :] = dwg[j]
            dcw_ref[1, j:j + 1, :] = dwv[j]
        dcb_ref[0] = dbg
        dcb_ref[1] = dbv

    return pl.pallas_call(
        body, name="ffn_bwd", grid=(nt,),
        in_specs=[pl.BlockSpec((S, D), lambda n: (0, 0)), pl.BlockSpec((tn, D), lambda n: (n, 0)),
                  pl.BlockSpec((2, S, tn), lambda n: (0, 0, n)),
                  pl.BlockSpec((3, tn), lambda n: (0, n)), pl.BlockSpec((3, tn), lambda n: (0, n + nt)),
                  pl.BlockSpec((1, tn), lambda n: (0, n)), pl.BlockSpec((1, tn), lambda n: (0, n + nt))],
        out_specs=[pl.BlockSpec((2, S, tn), lambda n: (0, 0, n)), pl.BlockSpec((tn, D), lambda n: (n, 0)),
                   pl.BlockSpec((2, 3, tn), lambda n: (0, 0, n)), pl.BlockSpec((2, 1, tn), lambda n: (0, 0, n))],
        out_shape=[jax.ShapeDtypeStruct((2, S, cfg.FFP), BF16), jax.ShapeDtypeStruct((cfg.FFP, D), BF16),
                   jax.ShapeDtypeStruct((2, 3, cfg.FFP), F32), jax.ShapeDtypeStruct((2, 1, cfg.FFP), F32)],
        compiler_params=_cp(cfg, ("parallel",)),
    )(df, w_down, u, conv_w, conv_w, conv_b, conv_b)


def rope_tables(cfg):
    inv_freq = ROPE_THETA ** (-jnp.arange(0, cfg.DH, 2, dtype=F32) / cfg.DH)
    ang = jnp.arange(cfg.S, dtype=F32)[:, None] * inv_freq[None, :]
    cos, sin = jnp.cos(ang), jnp.sin(ang)
    return jnp.concatenate([cos, cos], axis=1), jnp.concatenate([-sin, sin], axis=1)


class LocalWeights:
    def __init__(self, w_in, w_out, w_up, conv_w, w_down):
        self.w = (w_in, w_out, w_up, conv_w, w_down)
        self.grads = {}

    def weights_first(self):
        return self.w[0], self.w[3]

    def start_rest(self):
        return None

    def weights_rest(self, after):
        return self.w[1], self.w[2], self.w[4]

    def reduce_start(self, grads):
        self.grads.update(grads)
        return None

    def reduce_wait(self, names, after):
        pass


def _after(a, token):
    return a if token is None else a + token[0, 0].astype(a.dtype)


def local_step(cfg, comm, x, target, g1, g2, g3, g4, g_sb, g_dl, conv_b):
    S, D = cfg.S, cfg.D
    cos2, sin2 = rope_tables(cfg)
    full = lambda r, c: pl.BlockSpec((r, c), lambda j, k: (0, 0))

    w_in, conv_w = comm.weights_first()
    h1 = rms_fwd(cfg, x, g1)
    qkv3 = qkv_proj(cfg, h1, w_in, _after(cos2, comm.start_rest()), sin2)
    o_sb, tsum = sb_fwd(cfg, qkv3)
    o_dl, lse_tot = dil_fwd(cfg, qkv3)
    mixed = combine_fwd(cfg, o_sb, o_dl, g_sb, g_dl)
    w_out, w_up, w_down = comm.weights_rest(after=mixed)
    tn = cfg.TN
    mo = _mm(cfg, "mix_out", mixed, w_out, nt=False, grid=(D // tn, 1),
             a_spec=full(S, cfg.DMIX), b_spec=pl.BlockSpec((cfg.DMIX, tn), lambda j, k: (0, j)),
             o_spec=pl.BlockSpec((S, tn), lambda j, k: (0, j)),
             out_shape=jax.ShapeDtypeStruct((S, D), F32), acc_shape=(8, LANE))
    x1, h2 = mid_fwd(cfg, x, mo, g2, g3)
    u, y = ffn_fwd(cfg, h2, w_up, conv_w, conv_b)
    tk = cfg.FFP // 4
    f = _mm(cfg, "ffn_down", y, w_down, nt=False, grid=(D // tn, 4),
            a_spec=pl.BlockSpec((S, tk), lambda j, k: (0, k)), b_spec=pl.BlockSpec((tk, tn), lambda j, k: (k, j)),
            o_spec=pl.BlockSpec((S, tn), lambda j, k: (0, j)),
            out_shape=jax.ShapeDtypeStruct((S, D), F32), acc_shape=(S, tn))
    dout, df, dg4, loss = final_fwd_bwd(cfg, x1, f, g4, target)

    du, dw_down, dconv_w, dconv_b = ffn_bwd(cfg, df, w_down, u, conv_w, conv_b)
    kt = cfg.FFP // tk
    dh2 = _mm(cfg, "d_h2", du, w_up, nt=True, grid=(D // tn, 2 * kt),
              a_spec=pl.BlockSpec((None, S, tk), lambda j, k: (k // kt, 0, k % kt)),
              b_spec=pl.BlockSpec((tn, tk), lambda j, k: (j, k)),
              o_spec=pl.BlockSpec((S, tn), lambda j, k: (0, j)),
              out_shape=jax.ShapeDtypeStruct((S, D), F32), acc_shape=(S, tn))
    nf = cfg.FFP // tn if cfg.FFP % tn == 0 else None
    tnu = tn if nf else cfg.TNF
    nf = cfg.FFP // tnu
    dw_up = _mm_tn(cfg, "d_w_up", h2, du, grid=(2 * nf,),
                   a_spec=pl.BlockSpec((S, D), lambda j: (0, 0)),
                   b_spec=pl.BlockSpec((None, S, tnu), lambda j: (j // nf, 0, j % nf)),
                   o_spec=pl.BlockSpec((D, tnu), lambda j: (0, j)),
                   out_shape=jax.ShapeDtypeStruct((D, cfg.FF2P), BF16))
    dx1, dmo, dg3, dg2 = mid_bwd(cfg, dh2, x1, g3, dout, mo, g2)

    dmix = _mm(cfg, "d_mixed", dmo, w_out, nt=True, grid=(cfg.DMIX // tn, 1),
               a_spec=full(S, D), b_spec=pl.BlockSpec((tn, D), lambda j, k: (j, 0)),
               o_spec=pl.BlockSpec((S, tn), lambda j, k: (0, j)),
               out_shape=jax.ShapeDtypeStruct((S, cfg.DMIX), F32), acc_shape=(8, LANE))
    dw_out = _mm_tn(cfg, "d_w_out", mixed, dmo, grid=(D // tn,),
                    a_spec=pl.BlockSpec((S, cfg.DMIX), lambda j: (0, 0)),
                    b_spec=pl.BlockSpec((S, tn), lambda j: (0, j)),
                    o_spec=pl.BlockSpec((cfg.DMIX, tn), lambda j: (0, j)),
                    out_shape=jax.ShapeDtypeStruct((cfg.DMIX, D), BF16))
    token = comm.reduce_start(dict(w_out=dw_out, w_up=dw_up, w_down=dw_down))
    do_sb, do_dl, delta, dg_sb, dg_dl = combine_bwd(cfg, dmix, o_sb, o_dl, _after(g_sb, token), g_dl)
    d_sb3 = sb_bwd(cfg, qkv3, do_sb, tsum)
    dqkv3 = dil_bwd(cfg, qkv3, do_dl, delta, lse_tot, cos2, sin2, d_sb3)
    comm.reduce_wait(("w_out", "w_up", "w_down"), after=dqkv3)
    tkq = min(tn, cfg.DSB)
    kq = cfg.DSB // tkq
    dw_in = _mm_tn(cfg, "d_w_in", h1, dqkv3, grid=(6 * kq,),
                   a_spec=pl.BlockSpec((S, D), lambda j: (0, 0)),
                   b_spec=pl.BlockSpec((None, S, tkq), lambda j: (j // kq, 0, j % kq)),
                   o_spec=pl.BlockSpec((D, tkq), lambda j: (0, j)),
                   out_shape=jax.ShapeDtypeStruct((D, 6 * cfg.DSB), BF16))
    token = comm.reduce_start(dict(w_in=dw_in))
    dh1 = _mm(cfg, "d_h1", dqkv3, w_in, nt=True, grid=(D // tn, 6 * kq),
              a_spec=pl.BlockSpec((None, S, tkq), lambda j, k: (k // kq, 0, k % kq)),
              b_spec=pl.BlockSpec((tn, tkq), lambda j, k: (j, k)),
              o_spec=pl.BlockSpec((S, tn), lambda j, k: (0, j)),
              out_shape=jax.ShapeDtypeStruct((S, D), F32), acc_shape=(S, tn), dep=token)
    grad_x, dg1 = first_bwd(cfg, dh1, x, g1, dx1)
    comm.reduce_wait(("w_in",), after=grad_x)
    small = dict(loss=loss, g1=dg1, g2=dg2, g3=dg3, g4=dg4, g_sb=dg_sb, g_dl=dg_dl,
                 conv_b=dconv_b.reshape(1, cfg.FF2P), conv_w=dconv_w.transpose(1, 0, 2).reshape(3, cfg.FF2P))
    return grad_x, small


ANY = pl.BlockSpec(memory_space=pl.ANY)


def _me():
    return lax.axis_index("x"), lax.axis_index("y"), lax.axis_index("c")


def _other_chips(x, y):
    return [(1 - x, y), (x, 1 - y), (1 - x, 1 - y)]


def pad_conv_w(cfg, conv_w):
    r, c = conv_w.shape

    def body(w_ref, o_ref):
        o_ref[:, :c] = w_ref[...]
        o_ref[:, c:] = jnp.zeros((r, cfg.FSHP - c), F32)

    return pl.pallas_call(body, name="pad_conv_w", out_shape=jax.ShapeDtypeStruct((r, cfg.FSHP), F32))(conv_w)


def _tile2(r, c):
    return (256, c) if r % 256 == 0 else (r, 512 if c % 512 == 0 else c)


def cast_into(cfg, name, w, pos):
    r, c = w.shape
    _, nr, _, nc = _slab(cfg, name, 0)
    tm, tc = _tile2(r, c)
    wc = nc if tc == c else tc
    assert nr == r and (nc == c or tc == c)

    def body(pos_ref, w_ref, full_ref, scr, sem):
        scr[:, :tc] = w_ref[...].astype(BF16)
        if wc > tc:
            scr[:, tc:] = jnp.zeros((tm, wc - tc), BF16)
        r0, _, c0, _ = _slab(cfg, name, pos_ref[0])
        rows = pl.ds(pl.multiple_of(r0 + pl.program_id(0) * tm, 16), tm)
        cols = pl.ds(pl.multiple_of(c0 + pl.program_id(1) * tc, LANE), wc)
        cp = pltpu.make_async_copy(scr, full_ref.at[rows, cols], sem)
        cp.start()
        cp.wait()

    return pl.pallas_call(
        body, name=f"cast_{name}",
        grid_spec=pltpu.PrefetchScalarGridSpec(
            num_scalar_prefetch=1, grid=(r // tm, c // tc),
            in_specs=[pl.BlockSpec((tm, tc), lambda i, j, p: (i, j))], out_specs=ANY,
            scratch_shapes=[pltpu.VMEM((tm, wc), BF16), pltpu.SemaphoreType.DMA]),
        out_shape=jax.ShapeDtypeStruct(_full_shape(cfg, name), BF16),
        compiler_params=_cp(cfg, ("arbitrary", "arbitrary")),
    )(pos, w)


HBM = pl.BlockSpec(memory_space=pltpu.HBM)
SEM = pl.BlockSpec(memory_space=pltpu.SEMAPHORE)
TOKEN = pl.BlockSpec(memory_space=pltpu.VMEM)
EFFECT = pltpu.SideEffectType.DATAFLOW_SIDE_EFFECTING


def _slab(cfg, name, k):
    D = cfg.D
    if name == "w_in":
        cin = 6 * cfg.DSB // N_CHIPS
        return 0, D, k * cin, cin
    if name == "w_out":
        rout = cfg.DMIX // N_CHIPS
        return k * rout, rout, 0, D
    if name == "w_up":
        return 0, D, k * cfg.FSHP, cfg.FSHP
    rdn = cfg.FSH // 2
    return (k // 2) * cfg.FSHP + (k % 2) * rdn, rdn, 0, D


def _full_shape(cfg, name):
    return dict(w_in=(cfg.D, 6 * cfg.DSB), w_out=(cfg.DMIX, cfg.D), w_up=(cfg.D, cfg.FF2P), w_down=(cfg.FFP, cfg.D))[name]


def _half(cfg, name, ref, k, h):
    r0, nr, c0, nc = _slab(cfg, name, k)
    return ref.at[pl.ds(r0 + h * (nr // 2), nr // 2), pl.ds(c0, nc)]


def _rows_half(ref, h):
    nr = ref.shape[0] // 2
    return ref.at[pl.ds(h * nr, nr), :]


def _remote(src, dst, send_sem, recv_sem, dev):
    return pltpu.make_async_remote_copy(src_ref=src, dst_ref=dst, send_sem=send_sem, recv_sem=recv_sem,
                                        device_id=dev, device_id_type=MESH)


def gather_first(cfg, g_in, sh_cw):
    def body(in_ref, cw_ref, g_in, g_cw, token, ssem, rsem, fssem, frsem, lsem):
        x, y, c = _me()
        me, sib = 2 * x + y, (x, y, 1 - c)
        cw_slot = lambda k: g_cw.at[:, pl.ds(k * cfg.FSHP, cfg.FSHP)]
        local = [pltpu.make_async_copy(cw_ref, cw_slot(me), lsem.at[0])]
        sends = []
        for j, (px, py) in enumerate(_other_chips(x, y)):
            mine = _half(cfg, "w_in", g_in, me, c)
            sends.append(_remote(mine, mine, ssem.at[j], rsem.at[j], (px, py, c)))
            sends.append(_remote(cw_ref, cw_slot(me), ssem.at[3 + j], rsem.at[3 + j], (px, py, c)))
        for cp in local + sends:
            cp.start()
        for j, (px, py) in enumerate(_other_chips(x, y)):
            k = 2 * px + py
            landed = _half(cfg, "w_in", g_in, k, c)
            _remote(landed, landed, ssem.at[j], rsem.at[j], (px, py, c)).wait_recv()
            fwd = _remote(landed, landed, fssem.at[j], frsem.at[j], sib)
            fwd.start()
            sends.append(fwd)
        for j, (px, py) in enumerate(_other_chips(x, y)):
            k = 2 * px + py
            passed = _half(cfg, "w_in", g_in, k, 1 - c)
            _remote(passed, passed, fssem.at[j], frsem.at[j], sib).wait_recv()
            _remote(cw_ref, cw_slot(k), ssem.at[3 + j], rsem.at[3 + j], (px, py, c)).wait_recv()
        for cp in sends:
            cp.wait_send()
        for cp in local:
            cp.wait()
        token[...] = jnp.zeros_like(token)

    return pl.pallas_call(
        body, name="gather_first", in_specs=[ANY, ANY], out_specs=[ANY, ANY, TOKEN],
        out_shape=[jax.ShapeDtypeStruct(_full_shape(cfg, "w_in"), BF16), jax.ShapeDtypeStruct((3, cfg.FF2P), F32),
                   jax.ShapeDtypeStruct((8, LANE), F32)],
        input_output_aliases={0: 0},
        scratch_shapes=[pltpu.SemaphoreType.DMA((6,)), pltpu.SemaphoreType.DMA((6,)), pltpu.SemaphoreType.DMA((3,)),
                        pltpu.SemaphoreType.DMA((3,)), pltpu.SemaphoreType.DMA((1,))],
    )(g_in, sh_cw)


REST = ("w_out", "w_up", "w_down")


def _hbm(a):
    return pltpu.with_memory_space_constraint(a, pltpu.HBM)


def gather_start(cfg, fulls, after):
    n = len(REST)

    def body(*refs):
        lands = refs[:n]
        ssem, rsem = refs[n + 1], refs[n + 2]
        token = refs[-1]
        x, y, c = _me()
        me = 2 * x + y
        for i, name in enumerate(REST):
            mine = _half(cfg, name, lands[i], me, c)
            for j, (px, py) in enumerate(_other_chips(x, y)):
                _remote(mine, mine, ssem.at[3 * i + j], rsem.at[3 * i + j], (px, py, c)).start()
        token[...] = jnp.zeros_like(token)

    ops = [_hbm(a) for a in fulls]
    outs = pl.pallas_call(
        body, name="gather_start",
        in_specs=[HBM] * n + [ANY],
        out_specs=[SEM, SEM] + [HBM] * n + [TOKEN],
        out_shape=[pltpu.SemaphoreType.DMA((3 * n,)), pltpu.SemaphoreType.DMA((3 * n,))]
        + [pltpu.HBM(a.shape, a.dtype) for a in ops] + [jax.ShapeDtypeStruct((8, LANE), F32)],
        input_output_aliases={i: 2 + i for i in range(n)},
        compiler_params=pltpu.CompilerParams(has_side_effects=EFFECT),
    )(*ops, after)
    return outs[0], outs[1], outs[2:2 + n], outs[-1]


def gather_wait(cfg, ssem, rsem, lands, after):
    n = len(REST)

    def body(*refs):
        lands_ = refs[:n]
        ssem_, rsem_ = refs[n], refs[n + 1]
        x, y, c = _me()
        me = 2 * x + y
        for i, name in enumerate(REST):
            for j, (px, py) in enumerate(_other_chips(x, y)):
                cp = _remote(_half(cfg, name, lands_[i], me, c), _half(cfg, name, lands_[i], 2 * px + py, c),
                             ssem_.at[3 * i + j], rsem_.at[3 * i + j], (px, py, c))
                cp.wait_send()
                cp.wait_recv()

    return pl.pallas_call(
        body, name="gather_wait",
        in_specs=[HBM] * n + [SEM, SEM, ANY], out_specs=[HBM] * n,
        out_shape=[pltpu.HBM(a.shape, a.dtype) for a in lands],
        input_output_aliases={i: i for i in range(n)},
        compiler_params=pltpu.CompilerParams(has_side_effects=EFFECT),
    )(*lands, ssem, rsem, after)


def gather_finish(cfg, lands):
    n = len(REST)
    rdn = cfg.FSH // 2
    zpad = jnp.zeros((cfg.FSHP - cfg.FSH, cfg.D), BF16)

    def body(*refs):
        z_ref, outs = refs[0], refs[n + 1:2 * n + 1]
        ssem, rsem, lsem = refs[2 * n + 1:]
        x, y, c = _me()
        sib = (x, y, 1 - c)
        local = [pltpu.make_async_copy(z_ref, outs[2].at[pl.ds(h * cfg.FSHP + 2 * rdn, cfg.FSHP - cfg.FSH), :],
                                       lsem.at[h]) for h in range(2)]
        fwds = []
        for i, name in enumerate(REST):
            for j, (px, py) in enumerate(_other_chips(x, y)):
                landed = _half(cfg, name, outs[i], 2 * px + py, c)
                fwds.append(_remote(landed, landed, ssem.at[3 * i + j], rsem.at[3 * i + j], sib))
        for cp in local + fwds:
            cp.start()
        for i, name in enumerate(REST):
            for j, (px, py) in enumerate(_other_chips(x, y)):
                passed = _half(cfg, name, outs[i], 2 * px + py, 1 - c)
                _remote(passed, passed, ssem.at[3 * i + j], rsem.at[3 * i + j], sib).wait_recv()
        for cp in fwds:
            cp.wait_send()
        for cp in local:
            cp.wait()

    return pl.pallas_call(
        body, name="gather_finish", in_specs=[ANY] * (n + 1), out_specs=[ANY] * n,
        out_shape=[jax.ShapeDtypeStruct(a.shape, a.dtype) for a in lands],
        input_output_aliases={1 + i: i for i in range(n)},
        scratch_shapes=[pltpu.SemaphoreType.DMA((3 * n,)), pltpu.SemaphoreType.DMA((3 * n,)),
                        pltpu.SemaphoreType.DMA((2,))],
    )(zpad, *lands)


def pair_send(cfg, grads):
    names = list(grads)
    n = len(names)

    def half_shape(name):
        _, nr, _, nc = _slab(cfg, name, 0)
        return (N_CHIPS, nr // 2, nc)

    def body(*refs):
        srcs, theirs = refs[:n], refs[n:2 * n]
        ssem, rsem = refs[2 * n:]
        x, y, c = _me()
        cps = []
        for i, name in enumerate(names):
            for k in range(N_CHIPS):
                cps.append(_remote(_half(cfg, name, srcs[i], k, 1 - c), theirs[i].at[k],
                                   ssem.at[N_CHIPS * i + k], rsem.at[N_CHIPS * i + k], (x, y, 1 - c)))
        for cp in cps:
            cp.start()
        for cp in cps:
            cp.wait()

    outs = pl.pallas_call(
        body, name="pair_send_" + "_".join(names), in_specs=[ANY] * n, out_specs=[ANY] * n,
        out_shape=[jax.ShapeDtypeStruct(half_shape(name), BF16) for name in names],
        scratch_shapes=[pltpu.SemaphoreType.DMA((N_CHIPS * n,))] * 2,
    )(*[grads[k] for k in names])
    return dict(zip(names, outs))


def pair_sum(cfg, name, grad, theirs, pos):
    _, r, c = theirs.shape
    tm, tc = _tile2(r, c)

    ni, nj = r // tm, c // tc
    total = N_CHIPS * ni * nj

    def body(pos_ref, g_ref, t_ref, o_ref, scr, sem):
        step = (pl.program_id(0) * ni + pl.program_id(1)) * nj + pl.program_id(2)

        def fetch(flat, slot):
            k, rem = flat // (ni * nj), flat % (ni * nj)
            r0, nr, c0, _ = _slab(cfg, name, k)
            rows = pl.ds(pl.multiple_of(r0 + pos_ref[1] * (nr // 2) + (rem // nj) * tm, 16), tm)
            cols = pl.ds(pl.multiple_of(c0 + (rem % nj) * tc, LANE), tc)
            return pltpu.make_async_copy(g_ref.at[rows, cols], scr.at[slot], sem.at[slot])

        @pl.when(step == 0)
        def _():
            fetch(0, 0).start()

        @pl.when(step + 1 < total)
        def _():
            fetch(step + 1, (step + 1) % 2).start()

        fetch(step, step % 2).wait()
        o_ref[...] = (scr[step % 2].astype(F32) + t_ref[...].astype(F32)).astype(BF16)

    blk = pl.BlockSpec((None, tm, tc), lambda k, i, j, p: (k, i, j))
    return pl.pallas_call(
        body, name=f"pair_sum_{name}",
        grid_spec=pltpu.PrefetchScalarGridSpec(
            num_scalar_prefetch=1, grid=(N_CHIPS, ni, nj), in_specs=[ANY, blk], out_specs=blk,
            scratch_shapes=[pltpu.VMEM((2, tm, tc), BF16), pltpu.SemaphoreType.DMA((2,))]),
        out_shape=jax.ShapeDtypeStruct(theirs.shape, BF16),
        compiler_params=_cp(cfg, ("arbitrary",) * 3),
    )(pos, grad, theirs)


def scatter_start(cfg, pres, after):
    names = list(pres)
    n = len(names)

    def body(*refs):
        srcs, lands = refs[:n], refs[n:2 * n]
        ssem, rsem = refs[2 * n + 1], refs[2 * n + 2]
        token = refs[-1]
        x, y, c = _me()
        for i in range(n):
            for j, (px, py) in enumerate(_other_chips(x, y)):
                _remote(srcs[i].at[2 * px + py], lands[i].at[j], ssem.at[3 * i + j], rsem.at[3 * i + j], (px, py, c)).start()
        token[...] = jnp.zeros_like(token)

    lands = [lax.empty((3,) + pres[k].shape[1:], BF16) for k in names]
    ops = [_hbm(a) for a in [pres[k] for k in names] + lands]
    outs = pl.pallas_call(
        body, name="scatter_start_" + "_".join(names),
        in_specs=[HBM] * (2 * n) + [ANY],
        out_specs=[SEM, SEM] + [HBM] * (2 * n) + [TOKEN],
        out_shape=[pltpu.SemaphoreType.DMA((3 * n,)), pltpu.SemaphoreType.DMA((3 * n,))]
        + [pltpu.HBM(a.shape, a.dtype) for a in ops] + [jax.ShapeDtypeStruct((8, LANE), F32)],
        input_output_aliases={i: 2 + i for i in range(2 * n)},
        compiler_params=pltpu.CompilerParams(has_side_effects=EFFECT),
    )(*ops, after)
    return outs[0], outs[1], dict(zip(names, outs[2:2 + n])), dict(zip(names, outs[2 + n:2 + 2 * n])), outs[-1]


def scatter_wait(cfg, ssem, rsem, pres, lands, after):
    names = list(pres)
    n = len(names)

    def body(*refs):
        srcs, lands_ = refs[:n], refs[n:2 * n]
        ssem_, rsem_ = refs[2 * n], refs[2 * n + 1]
        x, y, c = _me()
        for i in range(n):
            for j, (px, py) in enumerate(_other_chips(x, y)):
                cp = _remote(srcs[i].at[2 * px + py], lands_[i].at[j], ssem_.at[3 * i + j], rsem_.at[3 * i + j], (px, py, c))
                cp.wait_send()
                cp.wait_recv()

    ops = [pres[k] for k in names] + [lands[k] for k in names]
    outs = pl.pallas_call(
        body, name="scatter_wait_" + "_".join(names),
        in_specs=[HBM] * (2 * n) + [SEM, SEM, ANY], out_specs=[HBM] * (2 * n),
        out_shape=[pltpu.HBM(a.shape, a.dtype) for a in ops],
        input_output_aliases={i: i for i in range(2 * n)},
        compiler_params=pltpu.CompilerParams(has_side_effects=EFFECT),
    )(*ops, ssem, rsem, after)
    return dict(zip(names, outs[:n])), dict(zip(names, outs[n:]))


def sum_landed(cfg, name, pre, land, pos):
    _, r, c = pre.shape
    tm, tc = _tile2(r, c)
    nrt = r // tm

    def body(pos_ref, p_ref, l_ref, o_ref):
        acc = p_ref[...].astype(F32)
        for j in range(3):
            acc = acc + l_ref[j].astype(F32)
        o_ref[...] = acc

    return pl.pallas_call(
        body, name=f"sum_landed_{name}",
        grid_spec=pltpu.PrefetchScalarGridSpec(
            num_scalar_prefetch=1, grid=(nrt, c // tc),
            in_specs=[pl.BlockSpec((None, tm, tc), lambda i, j, p: (p[0], i, j)),
                      pl.BlockSpec((3, tm, tc), lambda i, j, p: (0, i, j))],
            out_specs=pl.BlockSpec((tm, tc), lambda i, j, p: (p[1] * nrt + i, j))),
        out_shape=jax.ShapeDtypeStruct((2 * r, c), F32), compiler_params=_cp(cfg, ("parallel", "parallel")),
    )(pos, pre, land)


def half_swap(cfg, sums):
    names = list(sums)
    n = len(names)

    def body(*refs):
        outs = refs[n:2 * n]
        ssem, rsem = refs[2 * n:]
        x, y, c = _me()
        cps = [_remote(_rows_half(outs[i], c), _rows_half(outs[i], c), ssem.at[i], rsem.at[i], (x, y, 1 - c))
               for i in range(n)]
        for cp in cps:
            cp.start()
        for i in range(n):
            theirs = _rows_half(outs[i], 1 - c)
            _remote(theirs, theirs, ssem.at[i], rsem.at[i], (x, y, 1 - c)).wait_recv()
        for cp in cps:
            cp.wait_send()

    outs = pl.pallas_call(
        body, name="half_swap_" + "_".join(names), in_specs=[ANY] * n, out_specs=[ANY] * n,
        out_shape=[jax.ShapeDtypeStruct(sums[k].shape, F32) for k in names],
        input_output_aliases={i: i for i in range(n)},
        scratch_shapes=[pltpu.SemaphoreType.DMA((n,))] * 2,
    )(*[sums[k] for k in names])
    return dict(zip(names, outs))


class MeshWeights:
    def __init__(self, cfg, w_sh):
        self.cfg = cfg
        self.pos = jnp.stack([2 * lax.axis_index("x") + lax.axis_index("y"), lax.axis_index("c")]).astype(jnp.int32)
        self.full = {k: cast_into(cfg, k, w_sh[k], self.pos) for k in ("w_in",) + REST}
        self.conv_w = pad_conv_w(cfg, w_sh["conv_w"])
        self.inflight = {}
        self.grads = {}

    def weights_first(self):
        w_in, conv_w, self.token = gather_first(self.cfg, self.full["w_in"], self.conv_w)
        return w_in, conv_w

    def start_rest(self):
        out = gather_start(self.cfg, [self.full[k] for k in REST], self.token)
        self.rest = out[:3]
        return out[3]

    def weights_rest(self, after):
        return gather_finish(self.cfg, gather_wait(self.cfg, *self.rest, after))

    def reduce_start(self, grads):
        theirs = pair_send(self.cfg, grads)
        pres = {k: pair_sum(self.cfg, k, grads[k], theirs[k], self.pos) for k in grads}
        out = scatter_start(self.cfg, pres, jnp.zeros((8, LANE), F32))
        self.inflight[tuple(grads)] = out[:4]
        return out[4]

    def reduce_wait(self, names, after):
        cfg = self.cfg
        pres, lands = scatter_wait(cfg, *self.inflight.pop(tuple(names)), after)
        sums = {k: sum_landed(cfg, k, pres[k], lands[k], self.pos) for k in names}
        self.grads.update(half_swap(cfg, sums))


def allreduce_small(cfg, vec):
    R = vec.shape[0]

    def body(v_ref, o_ref, buf, send_sems, recv_sems):
        x, y, c = _me()
        me = 4 * x + 2 * y + c
        buf[me] = v_ref[...]
        sends = []
        for k in range(1, N_DEV):
            px, py, pc = x ^ (k >> 2), y ^ ((k >> 1) & 1), c ^ (k & 1)
            sends.append(pltpu.make_async_remote_copy(
                src_ref=v_ref, dst_ref=buf.at[me], send_sem=send_sems.at[k], recv_sem=recv_sems.at[k],
                device_id=(px, py, pc), device_id_type=MESH))
        for cp in sends:
            cp.start()
        for k in range(1, N_DEV):
            px, py, pc = x ^ (k >> 2), y ^ ((k >> 1) & 1), c ^ (k & 1)
            pltpu.make_async_remote_copy(
                src_ref=v_ref, dst_ref=buf.at[4 * px + 2 * py + pc], send_sem=send_sems.at[k],
                recv_sem=recv_sems.at[k], device_id=(px, py, pc), device_id_type=MESH).wait_recv()
        for cp in sends:
            cp.wait_send()
        acc = buf[0]
        for j in range(1, N_DEV):
            acc = acc + buf[j]
        o_ref[...] = acc

    return pl.pallas_call(
        body, name="allreduce_small",
        in_specs=[pl.BlockSpec(memory_space=pltpu.VMEM)], out_specs=pl.BlockSpec(memory_space=pltpu.VMEM),
        out_shape=jax.ShapeDtypeStruct((R, LANE), F32),
        scratch_shapes=[pltpu.VMEM((N_DEV, R, LANE), F32), pltpu.SemaphoreType.DMA((N_DEV,)),
                        pltpu.SemaphoreType.DMA((N_DEV,))],
    )(vec)


def adamw(cfg, name, w, m, v, g_parts, tile):
    r, c = w.shape
    tm, tc = tile[0] or r, tile[1] or c
    assert tc == c or all(g.shape[1] == c for g in g_parts)
    n = len(g_parts)
    bc1 = 1.0 - ADAM_B1 ** ADAM_STEP
    bc2 = 1.0 - ADAM_B2 ** ADAM_STEP

    def body(*refs):
        w_ref, m_ref, v_ref = refs[:3]
        g_refs = refs[3:3 + n]
        g_out, d_out, m_out, v_out = refs[3 + n:]
        g = g_refs[0][:, :tc]
        for gr in g_refs[1:]:
            g = g + gr[:, :tc]
        m_new = ADAM_B1 * m_ref[...] + (1.0 - ADAM_B1) * g
        v_new = ADAM_B2 * v_ref[...] + (1.0 - ADAM_B2) * jnp.square(g)
        m_hat = m_new / bc1
        v_hat = v_new / bc2
        g_out[...] = g
        d_out[...] = -ADAM_LR * (m_hat / (jnp.sqrt(v_hat) + ADAM_EPS) + ADAM_WD * w_ref[...])
        m_out[...] = m_new
        v_out[...] = v_new

    blk = pl.BlockSpec((tm, tc), lambda i, j: (i, j))
    return pl.pallas_call(
        body, name=f"adamw_{name}", grid=(r // tm, c // tc),
        in_specs=[blk] * 3 + [pl.BlockSpec((tm, tc if tc < c else g.shape[1]), lambda i, j: (i, j)) for g in g_parts],
        out_specs=[blk] * 4, out_shape=[jax.ShapeDtypeStruct((r, c), F32)] * 4,
        compiler_params=_cp(cfg, ("parallel", "parallel")),
    )(w, m, v, *g_parts)


SMALL_ORDER = ("loss", "g1", "g2", "g3", "g4", "g_sb", "g_dl", "conv_b", "conv_w")


def pack_small(small):
    rows = []
    for k in SMALL_ORDER:
        a = small[k].reshape(-1, LANE)
        rows.append(a)
    flat = jnp.concatenate(rows, axis=0)
    pad = (-flat.shape[0]) % 8
    return jnp.pad(flat, ((0, pad), (0, 0))), [r.shape[0] for r in rows]


def unpack_small(red, small, counts):
    out, at = {}, 0
    for k, n in zip(SMALL_ORDER, counts):
        out[k] = red[at:at + n].reshape(small[k].shape)
        at += n
    return out


def pad_ff(cfg, a):
    r = a.shape[0]
    return jnp.pad(a.reshape(r, N_CHIPS, cfg.FSH), ((0, 0), (0, 0), (0, cfg.FSHP - cfg.FSH))).reshape(r, cfg.FF2P)


def step(cfg, x, target, gains, w_sh, conv_b, m_all, v_all):
    chip = 2 * lax.axis_index("x") + lax.axis_index("y")
    comm = MeshWeights(cfg, w_sh)
    grad_x, small = local_step(cfg, comm, x, target, gains["g1"], gains["g2"], gains["g3"], gains["g4"],
                               gains["g_sb"], gains["g_dl"], pad_ff(cfg, conv_b))

    packed, counts = pack_small(small)
    red = unpack_small(allreduce_small(cfg, packed), small, counts)

    names = ("w_in", "w_out", "w_up", "w_down")
    tms = dict(w_in=(cfg.TM, None), w_out=(cfg.TM, None), w_up=(cfg.TM // 2, None), w_down=(None, cfg.TN // 2))
    res = {}
    for n in names:
        res[n] = adamw(cfg, n, w_sh[n], m_all[n], v_all[n], [comm.grads[n]], tms[n])
    g_cw = lax.dynamic_slice_in_dim(red["conv_w"].reshape(3, N_CHIPS, cfg.FSHP), chip, 1, axis=1)[:, 0, :cfg.FSH]
    res["conv_w"] = adamw(cfg, "conv_w", w_sh["conv_w"], m_all["conv_w"], v_all["conv_w"], [g_cw], (None, None))
    g_cb = red["conv_b"].reshape(1, N_CHIPS, cfg.FSHP)[:, :, :cfg.FSH].reshape(1, N_CHIPS * cfg.FSH)
    res["conv_b"] = adamw(cfg, "conv_b", conv_b, m_all["conv_b"], v_all["conv_b"], [g_cb], (None, None))
    for k in ("g1", "g2", "g3", "g4", "g_sb", "g_dl"):
        res[k] = adamw(cfg, k, gains[k], m_all[k], v_all[k], [red[k]], (None, None))
    return red["loss"][0, 0], grad_x, res


PARAMS = ("pre_mix_gain", "post_mix_gain", "pre_ffn_gain", "post_ffn_gain", "w_in", "sb_out_gain", "dil_out_gain",
          "w_out", "w_up", "conv_w", "conv_b", "w_down")
SHORT = dict(pre_mix_gain="g1", post_mix_gain="g2", pre_ffn_gain="g3", post_ffn_gain="g4", sb_out_gain="g_sb",
             dil_out_gain="g_dl", w_in="w_in", w_out="w_out", w_up="w_up", conv_w="conv_w", conv_b="conv_b",
             w_down="w_down")


def kernel(x, pre_mix_gain, post_mix_gain, pre_ffn_gain, post_ffn_gain, w_in, sb_out_gain, dil_out_gain, w_out, w_up, conv_w, conv_b, w_down, loss_target, m_pre_mix_gain, m_post_mix_gain, m_pre_ffn_gain, m_post_ffn_gain, m_w_in, m_sb_out_gain, m_dil_out_gain, m_w_out, m_w_up, m_conv_w, m_conv_b, m_w_down, v_pre_mix_gain, v_post_mix_gain, v_pre_ffn_gain, v_post_ffn_gain, v_w_in, v_sb_out_gain, v_dil_out_gain, v_w_out, v_w_up, v_conv_w, v_conv_b, v_w_down):
    cfg = CFG
    w = dict(zip(PARAMS, (pre_mix_gain, post_mix_gain, pre_ffn_gain, post_ffn_gain, w_in, sb_out_gain, dil_out_gain,
                          w_out, w_up, conv_w, conv_b, w_down)))
    m = dict(zip(PARAMS, (m_pre_mix_gain, m_post_mix_gain, m_pre_ffn_gain, m_post_ffn_gain, m_w_in, m_sb_out_gain,
                          m_dil_out_gain, m_w_out, m_w_up, m_conv_w, m_conv_b, m_w_down)))
    v = dict(zip(PARAMS, (v_pre_mix_gain, v_post_mix_gain, v_pre_ffn_gain, v_post_ffn_gain, v_w_in, v_sb_out_gain,
                          v_dil_out_gain, v_w_out, v_w_up, v_conv_w, v_conv_b, v_w_down)))
    sq = lambda a: a.reshape(a.shape[1:])
    ws = {SHORT[k]: sq(a) if a.ndim == 3 else a for k, a in w.items()}
    ms = {SHORT[k]: sq(a) if a.ndim == 3 else a for k, a in m.items()}
    vs = {SHORT[k]: sq(a) if a.ndim == 3 else a for k, a in v.items()}
    gains = {k: ws[k] for k in ("g1", "g2", "g3", "g4", "g_sb", "g_dl")}
    w_sh = {k: ws[k] for k in ("w_in", "w_out", "w_up", "conv_w", "w_down")}
    loss, grad_x, res = step(cfg, sq(x), sq(loss_target), gains, w_sh, ws["conv_b"], ms, vs)
    outs = [loss, grad_x.reshape(x.shape)]
    for i in range(4):
        for k in PARAMS:
            outs.append(res[SHORT[k]][i].reshape(w[k].shape))
    return tuple(outs)
```

```python
import functools
import math
from typing import NamedTuple

import jax
import jax.numpy as jnp
from jax import lax
from jax.experimental import pallas as pl
from jax.experimental.pallas import tpu as pltpu

F32 = jnp.float32
BF16 = jnp.bfloat16
MESH = pl.DeviceIdType.MESH

ROPE_THETA = 10000.0
RMS_EPS = 1e-6
ADAM_LR = 0.001
ADAM_B1 = 0.9
ADAM_B2 = 0.999
ADAM_EPS = 1e-08
ADAM_WD = 0.01
ADAM_STEP = 10
GELU_C = math.sqrt(2.0 / math.pi)
NEG_BIG = -1e30
LANE = 128
N_CHIPS = 4
N_DEV = 8


class Cfg(NamedTuple):
    S: int = 2048
    D: int = 2048
    DH: int = 128
    HSB: int = 8
    HDL: int = 8
    QB: int = 128
    SBT: int = 256
    branches: tuple = ((128, 1), (512, 4), (2048, 16))
    FSH: int = 2752
    FSHP: int = 2816
    TM: int = 256
    TNF: int = 256
    FCH: int = 512
    TN: int = 512
    VMEM_MB: int = 56

    @property
    def DSB(self):
        return self.HSB * self.DH

    @property
    def DDL(self):
        return self.HDL * self.DH

    @property
    def DMIX(self):
        return self.DSB + self.DDL

    @property
    def FFP(self):
        return 2 * self.FSHP

    @property
    def FF2P(self):
        return 4 * self.FSHP


CFG = Cfg()


def _cp(cfg, sem=None):
    return pltpu.CompilerParams(dimension_semantics=sem, vmem_limit_bytes=cfg.VMEM_MB * 2**20)


def _dot(a, b):
    return jnp.dot(a, b, preferred_element_type=F32)


def _dot_nt(a, b):
    return lax.dot_general(a, b, (((1,), (1,)), ((), ())), preferred_element_type=F32)


def _dot_tn(a, b):
    return lax.dot_general(a, b, (((0,), (0,)), ((), ())), preferred_element_type=F32)


def _dot_split(x, u):
    hi = x.astype(BF16)
    lo = (x - hi.astype(F32)).astype(BF16)
    return _dot(hi, u) + _dot(lo, u)


def _rstd(x):
    return lax.rsqrt(jnp.mean(x * x, axis=-1, keepdims=True) + RMS_EPS)


def _rms_bwd(dy, x, g):
    r = _rstd(x)
    xh = x * r
    dxh = dy * g
    dx = r * (dxh - xh * jnp.mean(dxh * xh, axis=-1, keepdims=True))
    return dx, dy * xh


def _gelu(x):
    t = jnp.tanh(GELU_C * (x + 0.044715 * (x * x * x)))
    return 0.5 * x * (1.0 + t), t


def _gelu_grad(x, t):
    return 0.5 * (1.0 + t) + 0.5 * x * (1.0 - t * t) * (GELU_C * (1.0 + 3 * 0.044715 * (x * x)))


def _row(cfg, w):
    return pl.BlockSpec((cfg.TM, w), lambda i: (i, 0))


def _vec(w):
    return pl.BlockSpec((1, w), lambda i: (0, 0))


def rms_fwd(cfg, x, g):
    S, D = x.shape

    def body(x_ref, g_ref, h_ref):
        xv = x_ref[...]
        h_ref[...] = (xv * _rstd(xv) * g_ref[...]).astype(BF16)

    return pl.pallas_call(
        body, name="rms_fwd", grid=(S // cfg.TM,),
        in_specs=[_row(cfg, D), _vec(D)], out_specs=_row(cfg, D),
        out_shape=jax.ShapeDtypeStruct((S, D), BF16), compiler_params=_cp(cfg, ("parallel",)),
    )(x, g)


def mid_fwd(cfg, x, mo, g_post, g_pre):
    S, D = x.shape

    def body(x_ref, mo_ref, gp_ref, gn_ref, x1_ref, h2_ref):
        mo_v = mo_ref[...]
        x1 = x_ref[...] + mo_v * _rstd(mo_v) * gp_ref[...]
        x1_ref[...] = x1
        h2_ref[...] = (x1 * _rstd(x1) * gn_ref[...]).astype(BF16)

    return pl.pallas_call(
        body, name="mid_fwd", grid=(S // cfg.TM,),
        in_specs=[_row(cfg, D), _row(cfg, D), _vec(D), _vec(D)],
        out_specs=[_row(cfg, D), _row(cfg, D)],
        out_shape=[jax.ShapeDtypeStruct((S, D), F32), jax.ShapeDtypeStruct((S, D), BF16)],
        compiler_params=_cp(cfg, ("parallel",)),
    )(x, mo, g_post, g_pre)


def final_fwd_bwd(cfg, x1, f, g_post, target):
    S, D = x1.shape

    def body(x1_ref, f_ref, g_ref, t_ref, dout_ref, df_ref, dg_ref, loss_ref):
        @pl.when(pl.program_id(0) == 0)
        def _():
            dg_ref[...] = jnp.zeros_like(dg_ref)
            loss_ref[...] = jnp.zeros_like(loss_ref)

        fv = f_ref[...]
        g = g_ref[...]
        out = x1_ref[...] + fv * _rstd(fv) * g
        err = out - t_ref[...]
        loss_ref[...] += 0.5 * jnp.sum(jnp.mean(err * err, axis=-1, keepdims=True), axis=0, keepdims=True)
        dout = err * (1.0 / D)
        dout_ref[...] = dout
        df, dgx = _rms_bwd(dout, fv, g)
        df_ref[...] = df.astype(BF16)
        dg_ref[...] += jnp.sum(dgx, axis=0, keepdims=True)

    return pl.pallas_call(
        body, name="final_fwd_bwd", grid=(S // cfg.TM,),
        in_specs=[_row(cfg, D), _row(cfg, D), _vec(D), _row(cfg, D)],
        out_specs=[_row(cfg, D), _row(cfg, D), _vec(D), _vec(LANE)],
        out_shape=[jax.ShapeDtypeStruct((S, D), F32), jax.ShapeDtypeStruct((S, D), BF16),
                   jax.ShapeDtypeStruct((1, D), F32), jax.ShapeDtypeStruct((1, LANE), F32)],
        compiler_params=_cp(cfg, ("arbitrary",)),
    )(x1, f, g_post, target)


def mid_bwd(cfg, dh2, x1, g_pre, dout, mo, g_post):
    S, D = x1.shape

    def body(dh_ref, x1_ref, gn_ref, do_ref, mo_ref, gp_ref, dx1_ref, dmo_ref, dgn_ref, dgp_ref):
        @pl.when(pl.program_id(0) == 0)
        def _():
            dgn_ref[...] = jnp.zeros_like(dgn_ref)
            dgp_ref[...] = jnp.zeros_like(dgp_ref)

        dx, dgx = _rms_bwd(dh_ref[...], x1_ref[...], gn_ref[...])
        dx1 = do_ref[...] + dx
        dx1_ref[...] = dx1
        dgn_ref[...] += jnp.sum(dgx, axis=0, keepdims=True)
        dmo, dgy = _rms_bwd(dx1, mo_ref[...], gp_ref[...])
        dmo_ref[...] = dmo.astype(BF16)
        dgp_ref[...] += jnp.sum(dgy, axis=0, keepdims=True)

    return pl.pallas_call(
        body, name="mid_bwd", grid=(S // cfg.TM,),
        in_specs=[_row(cfg, D), _row(cfg, D), _vec(D), _row(cfg, D), _row(cfg, D), _vec(D)],
        out_specs=[_row(cfg, D), _row(cfg, D), _vec(D), _vec(D)],
        out_shape=[jax.ShapeDtypeStruct((S, D), F32), jax.ShapeDtypeStruct((S, D), BF16),
                   jax.ShapeDtypeStruct((1, D), F32), jax.ShapeDtypeStruct((1, D), F32)],
        compiler_params=_cp(cfg, ("arbitrary",)),
    )(dh2, x1, g_pre, dout, mo, g_post)


def first_bwd(cfg, dh1, x, g_pre, dx1):
    S, D = x.shape

    def body(dh_ref, x_ref, g_ref, r_ref, dx_ref, dg_ref):
        @pl.when(pl.program_id(0) == 0)
        def _():
            dg_ref[...] = jnp.zeros_like(dg_ref)

        dx, dgx = _rms_bwd(dh_ref[...], x_ref[...], g_ref[...])
        dx_ref[...] = r_ref[...] + dx
        dg_ref[...] += jnp.sum(dgx, axis=0, keepdims=True)

    return pl.pallas_call(
        body, name="first_bwd", grid=(S // cfg.TM,),
        in_specs=[_row(cfg, D), _row(cfg, D), _vec(D), _row(cfg, D)],
        out_specs=[_row(cfg, D), _vec(D)],
        out_shape=[jax.ShapeDtypeStruct((S, D), F32), jax.ShapeDtypeStruct((1, D), F32)],
        compiler_params=_cp(cfg, ("arbitrary",)),
    )(dh1, x, g_pre, dx1)


def _mm(cfg, name, a, b, *, nt, a_spec, b_spec, o_spec, grid, out_shape, acc_shape, dep=None):
    nk = grid[-1]
    dot = _dot_nt if nt else _dot
    deps = [] if dep is None else [dep]

    def body(a_ref, b_ref, *rest):
        o_ref, acc_ref = rest[-2:]
        k = pl.program_id(len(grid) - 1)
        part = dot(a_ref[...], b_ref[...])
        if deps:
            part = part + rest[0][0:1, 0:1]
        if nk == 1:
            o_ref[...] = part.astype(o_ref.dtype)
            return

        @pl.when(k == 0)
        def _():
            acc_ref[...] = part

        @pl.when(k > 0)
        def _():
            acc_ref[...] += part

        @pl.when(k == nk - 1)
        def _():
            o_ref[...] = acc_ref[...].astype(o_ref.dtype)

    sem = ("parallel",) * (len(grid) - 1) + ("arbitrary",)
    dep_specs = [pl.BlockSpec((8, LANE), lambda *_: (0, 0))] * len(deps)
    return pl.pallas_call(
        body, name=name, grid=grid, in_specs=[a_spec, b_spec] + dep_specs, out_specs=o_spec, out_shape=out_shape,
        scratch_shapes=[pltpu.VMEM(acc_shape, F32)], compiler_params=_cp(cfg, sem),
    )(a, b, *deps)


def _mm_tn(cfg, name, a, b, *, a_spec, b_spec, o_spec, grid, out_shape):
    def body(a_ref, b_ref, o_ref):
        o_ref[...] = _dot_tn(a_ref[...], b_ref[...]).astype(o_ref.dtype)

    return pl.pallas_call(
        body, name=name, grid=grid, in_specs=[a_spec, b_spec], out_specs=o_spec, out_shape=out_shape,
        compiler_params=_cp(cfg, ("parallel",) * len(grid)),
    )(a, b)


def qkv_proj(cfg, h1, w_in, cos2, sin2):
    S, D = h1.shape
    tn = 2 * cfg.DH
    per = cfg.DSB // tn
    assert cfg.DSB == cfg.DDL
    nblk = 6 * per

    def body(a_ref, b_ref, c_ref, s_ref, o_ref):
        j = pl.program_id(0)
        acc = _dot(a_ref[...], b_ref[...])
        rope = jnp.logical_and(j >= 3 * per, j < 5 * per)

        @pl.when(rope)
        def _():
            for c in range(tn // cfg.DH):
                xh = acc[:, c * cfg.DH:(c + 1) * cfg.DH]
                o_ref[:, c * cfg.DH:(c + 1) * cfg.DH] = (
                    xh * c_ref[...] + pltpu.roll(xh, cfg.DH // 2, 1) * s_ref[...]).astype(BF16)

        @pl.when(jnp.logical_not(rope))
        def _():
            o_ref[...] = acc.astype(BF16)

    return pl.pallas_call(
        body, name="qkv_proj", grid=(nblk,),
        in_specs=[pl.BlockSpec((S, D), lambda j: (0, 0)), pl.BlockSpec((D, tn), lambda j: (0, j)),
                  pl.BlockSpec((S, cfg.DH), lambda j: (0, 0)), pl.BlockSpec((S, cfg.DH), lambda j: (0, 0))],
        out_specs=pl.BlockSpec((None, S, tn), lambda j: (j // per, 0, j % per)),
        out_shape=jax.ShapeDtypeStruct((6, S, cfg.DSB), BF16),
        compiler_params=_cp(cfg, ("parallel",)),
    )(h1, w_in, cos2, sin2)


def _sb_tile(cfg, q, k, kb, qb):
    QB = cfg.SBT
    z = _dot_nt(q, k) * (cfg.DH ** -0.5)
    t1 = jnp.log1p(jnp.exp(-jnp.abs(z)))
    lb = jnp.minimum(z, 0.0) - t1
    row = lax.broadcasted_iota(jnp.int32, (QB, QB), 0)
    col = lax.broadcasted_iota(jnp.int32, (QB, QB), 1)
    valid = jnp.logical_or(kb < qb, col < row)
    lk = jnp.where(valid, jnp.minimum(-z, 0.0) - t1, 0.0)
    return lb, lk, valid


def sb_fwd(cfg, qkv3):
    S, QB, DH = cfg.S, cfg.SBT, cfg.DH

    def body(q_ref, k_ref, v_ref, o_ref, t_ref):
        row = lax.broadcasted_iota(jnp.int32, (QB, QB), 0)
        col = lax.broadcasted_iota(jnp.int32, (QB, QB), 1)
        u_after = (row > col).astype(BF16)

        def q_loop(qb, _):
            rows = pl.ds(pl.multiple_of(qb * QB, QB), QB)
            q = q_ref[rows, :]

            def k_loop(i, carry):
                o_acc, c = carry
                kb = qb - i
                krows = pl.ds(pl.multiple_of(kb * QB, QB), QB)
                lb, lk, valid = _sb_tile(cfg, q, k_ref[krows, :], kb, qb)
                rem = _dot_split(lk, u_after) + c
                a = jnp.where(valid, jnp.exp(lb + rem), 0.0)
                o_acc = o_acc + _dot(a.astype(BF16), v_ref[krows, :])
                return o_acc, c + jnp.sum(lk, axis=1, keepdims=True)

            o_acc, c = lax.fori_loop(0, qb + 1, k_loop, (jnp.zeros((QB, DH), F32), jnp.zeros((QB, 1), F32)))
            o_ref[rows, :] = o_acc
            t_ref[rows, :] = jnp.broadcast_to(c, (QB, DH))
            return 0

        lax.fori_loop(0, S // QB, q_loop, 0)

    def spec(i):
        return pl.BlockSpec((None, S, DH), lambda h: (i, 0, h))

    return pl.pallas_call(
        body, name="sb_fwd", grid=(cfg.HSB,),
        in_specs=[spec(0), spec(1), spec(2)],
        out_specs=[pl.BlockSpec((S, DH), lambda h: (0, h))] * 2,
        out_shape=[jax.ShapeDtypeStruct((S, cfg.DSB), F32)] * 2,
        compiler_params=_cp(cfg, ("parallel",)),
    )(qkv3, qkv3, qkv3)


def sb_bwd(cfg, qkv3, do_sb, tsum):
    S, QB, DH = cfg.S, cfg.SBT, cfg.DH
    scale = DH ** -0.5

    def body(q_ref, k_ref, v_ref, do_ref, t_ref, d_ref, dk_acc, dv_acc):
        dk_acc[...] = jnp.zeros_like(dk_acc)
        dv_acc[...] = jnp.zeros_like(dv_acc)
        row = lax.broadcasted_iota(jnp.int32, (QB, QB), 0)
        col = lax.broadcasted_iota(jnp.int32, (QB, QB), 1)
        u_upto = (row <= col).astype(BF16)
        u_before = (row < col).astype(BF16)

        def q_loop(qb, _):
            rows = pl.ds(pl.multiple_of(qb * QB, QB), QB)
            q = q_ref[rows, :]
            do = do_ref[rows, :]
            total = t_ref[rows, 0:1]

            def k_loop(kb, carry):
                dq_acc, pc, gc = carry
                krows = pl.ds(pl.multiple_of(kb * QB, QB), QB)
                k = k_ref[krows, :]
                v = v_ref[krows, :]
                lb, lk, valid = _sb_tile(cfg, q, k, kb, qb)
                rem = total - pc - _dot_split(lk, u_upto)
                a = jnp.where(valid, jnp.exp(lb + rem), 0.0)
                g = a * _dot_nt(do, v)
                dv_acc[krows, :] += _dot_tn(a.astype(BF16), do)
                cum = gc + _dot(g.astype(BF16), u_before)
                sig = jnp.exp(lb)
                dz = (jnp.where(valid, g * (1.0 - sig) - cum * sig, 0.0) * scale).astype(BF16)
                dq_acc = dq_acc + _dot(dz, k)
                dk_acc[krows, :] += _dot_tn(dz, q)
                return dq_acc, pc + jnp.sum(lk, axis=1, keepdims=True), gc + jnp.sum(g, axis=1, keepdims=True)

            z1 = jnp.zeros((QB, 1), F32)
            dq_acc, _, _ = lax.fori_loop(0, qb + 1, k_loop, (jnp.zeros((QB, DH), F32), z1, z1))
            d_ref[0, rows, :] = dq_acc.astype(BF16)
            return 0

        lax.fori_loop(0, S // QB, q_loop, 0)
        d_ref[1, :, :] = dk_acc[...].astype(BF16)
        d_ref[2, :, :] = dv_acc[...].astype(BF16)

    def spec(i):
        return pl.BlockSpec((None, S, DH), lambda h: (i, 0, h))

    return pl.pallas_call(
        body, name="sb_bwd", grid=(cfg.HSB,),
        in_specs=[spec(0), spec(1), spec(2), pl.BlockSpec((S, DH), lambda h: (0, h)),
                  pl.BlockSpec((S, DH), lambda h: (0, h))],
        out_specs=pl.BlockSpec((3, S, DH), lambda h: (0, 0, h)),
        out_shape=jax.ShapeDtypeStruct((6, S, cfg.DSB), BF16),
        scratch_shapes=[pltpu.VMEM((S, DH), F32), pltpu.VMEM((S, DH), F32)],
        compiler_params=_cp(cfg, ("parallel",)),
    )(qkv3, qkv3, qkv3, do_sb, tsum)


def _band_mask(cfg, n, n_back):
    QB = cfg.QB
    qi = lax.broadcasted_iota(jnp.int32, (QB, 2 * QB), 0)
    kj = lax.broadcasted_iota(jnp.int32, (QB, 2 * QB), 1)
    dist = QB + qi - kj
    return (dist >= 0) & (dist <= n_back) & jnp.logical_or(n > 0, kj >= QB)


def _sub_rows(start, n, dil):
    if dil > 1:
        return pl.ds(start, n, stride=dil)
    return pl.ds(start if isinstance(start, int) else pl.multiple_of(start, 8), n)


def _stage_residues(cfg, dil, pairs):
    QB, L = cfg.QB, cfg.S // dil
    for src, dst in pairs:
        for r in range(dil):
            dst[pl.ds(r * (QB + L), QB), :] = jnp.zeros((QB, cfg.DH), BF16)
            dst[pl.ds(r * (QB + L) + QB, L), :] = src[_sub_rows(r, L, dil), :].astype(BF16)


def _staged_rows(cfg):
    return cfg.S + cfg.QB * max(d for _, d in cfg.branches)


def _lane_value(x):
    return jnp.max(x, axis=1, keepdims=True)


def dil_fwd(cfg, qkv3):
    S, QB, DH = cfg.S, cfg.QB, cfg.DH
    scale = DH ** -0.5
    nb = len(cfg.branches)
    mix_rows = min(256, S)

    def body(q_ref, k_ref, v_ref, o_ref, lt_ref, qf, kf, vf, kp, vp, *obl):
        obs, lbs = obl[:nb], obl[nb:]
        qf[...] = q_ref[...].astype(F32)
        kf[...] = k_ref[...].astype(F32)
        vf[...] = v_ref[...].astype(F32)
        for b, (window, dil) in enumerate(cfg.branches):
            L, n_back = S // dil, window // dil
            assert n_back <= QB and L % QB == 0
            _stage_residues(cfg, dil, [(kf, kp), (vf, vp)])
            for r in range(dil):
                for n in range(L // QB):
                    rows = _sub_rows(r + n * (QB * dil), QB, dil)
                    band = pl.ds(r * (QB + L) + n * QB, 2 * QB)
                    s = _dot_nt(qf[rows, :].astype(BF16), kp[band, :]) * scale
                    s = jnp.where(_band_mask(cfg, n, n_back), s, NEG_BIG)
                    m = jnp.max(s, axis=1, keepdims=True)
                    p = jnp.exp(s - m)
                    den = jnp.sum(p, axis=1, keepdims=True)
                    obs[b][rows, :] = _dot(p.astype(BF16), vp[band, :]) / den
                    lbs[b][rows, :] = jnp.broadcast_to(m + jnp.log(den), (QB, DH))

        def mix(i, _):
            rows = pl.ds(pl.multiple_of(i * mix_rows, mix_rows), mix_rows)
            ls = [r[rows, :] for r in lbs]
            m = functools.reduce(jnp.maximum, ls)
            es = [jnp.exp(l - m) for l in ls]
            tot = functools.reduce(jnp.add, es)
            o_ref[rows, :] = functools.reduce(jnp.add, [(e / tot) * r[rows, :] for e, r in zip(es, obs)])
            lt_ref[rows, :] = m + jnp.log(tot)
            return 0

        lax.fori_loop(0, S // mix_rows, mix, 0)

    def spec(i):
        return pl.BlockSpec((None, S, DH), lambda h: (i, 0, h))

    o_spec = pl.BlockSpec((S, DH), lambda h: (0, h))
    return pl.pallas_call(
        body, name="dil_fwd", grid=(cfg.HDL,),
        in_specs=[spec(3), spec(4), spec(5)], out_specs=[o_spec, o_spec],
        out_shape=[jax.ShapeDtypeStruct((S, cfg.DDL), F32)] * 2,
        scratch_shapes=[pltpu.VMEM((S, DH), F32)] * 3 + [pltpu.VMEM((_staged_rows(cfg), DH), BF16)] * 2
        + [pltpu.VMEM((S, DH), F32)] * (2 * nb),
        compiler_params=_cp(cfg, ("parallel",)),
    )(qkv3, qkv3, qkv3)


def dil_bwd(cfg, qkv3, do_dl, delta, lse_tot, cos2, sin2, d_sb3):
    S, QB, DH = cfg.S, cfg.QB, cfg.DH
    scale = DH ** -0.5
    out_rows = min(256, S)

    def body(q_ref, k_ref, v_ref, do_ref, dl_ref, lt_ref, c_ref, s_ref, base_ref, d_ref,
             qf, kf, vf, dof, kp, vp, dkp, dvp, dqn, dkn, dvn):
        qf[...] = q_ref[...].astype(F32)
        kf[...] = k_ref[...].astype(F32)
        vf[...] = v_ref[...].astype(F32)
        dof[...] = do_ref[...].astype(F32)
        for acc in (dqn, dkn, dvn):
            acc[...] = jnp.zeros_like(acc)
        for window, dil in cfg.branches:
            L, n_back = S // dil, window // dil
            reg = QB + L
            _stage_residues(cfg, dil, [(kf, kp), (vf, vp)])
            dkp[pl.ds(0, dil * reg), :] = jnp.zeros((dil * reg, DH), F32)
            dvp[pl.ds(0, dil * reg), :] = jnp.zeros((dil * reg, DH), F32)
            for r in range(dil):
                for n in range(L // QB):
                    rows = _sub_rows(r + n * (QB * dil), QB, dil)
                    band = pl.ds(r * reg + n * QB, 2 * QB)
                    q = qf[rows, :].astype(BF16)
                    do = dof[rows, :].astype(BF16)
                    kb = kp[band, :]
                    s = _dot_nt(q, kb) * scale
                    s = jnp.where(_band_mask(cfg, n, n_back), s, NEG_BIG)
                    p = jnp.exp(s - _lane_value(lt_ref[rows, :]))
                    ds = (p * (_dot_nt(do, vp[band, :]) - _lane_value(dl_ref[rows, :])) * scale).astype(BF16)
                    dqn[rows, :] += _dot(ds, kb)
                    dkp[band, :] += _dot_tn(ds, q)
                    dvp[band, :] += _dot_tn(p.astype(BF16), do)
            for r in range(dil):
                sub = _sub_rows(r, L, dil)
                dkn[sub, :] += dkp[pl.ds(r * reg + QB, L), :]
                dvn[sub, :] += dvp[pl.ds(r * reg + QB, L), :]

        def finish(i, _):
            rows = pl.ds(pl.multiple_of(i * out_rows, out_rows), out_rows)
            c, sn = c_ref[rows, :], s_ref[rows, :]
            for j, acc in enumerate((dqn, dkn)):
                d = acc[rows, :]
                d_ref[j, rows, :] = (d * c + pltpu.roll(d * sn, DH // 2, 1)).astype(BF16)
            d_ref[2, rows, :] = dvn[rows, :].astype(BF16)
            return 0

        lax.fori_loop(0, S // out_rows, finish, 0)

    def spec(i):
        return pl.BlockSpec((None, S, DH), lambda h: (i, 0, h))

    hd = pl.BlockSpec((S, DH), lambda h: (0, h))
    tab = pl.BlockSpec((S, DH), lambda h: (0, 0))
    ns = _staged_rows(cfg)
    return pl.pallas_call(
        body, name="dil_bwd", grid=(cfg.HDL,),
        in_specs=[spec(3), spec(4), spec(5), hd, hd, hd, tab, tab, ANY],
        out_specs=pl.BlockSpec((3, S, DH), lambda h: (1, 0, h)),
        out_shape=jax.ShapeDtypeStruct((6, S, cfg.DDL), BF16),
        input_output_aliases={8: 0},
        scratch_shapes=[pltpu.VMEM((S, DH), F32)] * 4 + [pltpu.VMEM((ns, DH), BF16)] * 2
        + [pltpu.VMEM((ns, DH), F32)] * 2 + [pltpu.VMEM((S, DH), F32)] * 3,
        compiler_params=_cp(cfg, ("parallel",)),
    )(qkv3, qkv3, qkv3, do_dl, delta, lse_tot, cos2, sin2, d_sb3)


def combine_fwd(cfg, o_sb, o_dl, g_sb, g_dl):
    S, DH = cfg.S, cfg.DH

    def head_norm(o, g):
        return o * lax.rsqrt(jnp.mean(o * o, axis=-1, keepdims=True) + RMS_EPS) * g

    def body(osb_ref, odl_ref, gsb_ref, gdl_ref, mix_ref):
        for h in range(cfg.HSB):
            c = slice(h * DH, (h + 1) * DH)
            mix_ref[:, c] = head_norm(osb_ref[:, c], gsb_ref[:, c]).astype(BF16)
        for h in range(cfg.HDL):
            c = slice(h * DH, (h + 1) * DH)
            mix_ref[:, cfg.DSB + h * DH:cfg.DSB + (h + 1) * DH] = head_norm(odl_ref[:, c], gdl_ref[:, c]).astype(BF16)

    return pl.pallas_call(
        body, name="combine_fwd", grid=(S // cfg.TM,),
        in_specs=[_row(cfg, cfg.DSB), _row(cfg, cfg.DDL), _vec(cfg.DSB), _vec(cfg.DDL)],
        out_specs=_row(cfg, cfg.DMIX), out_shape=jax.ShapeDtypeStruct((S, cfg.DMIX), BF16),
        compiler_params=_cp(cfg, ("parallel",)),
    )(o_sb, o_dl, g_sb, g_dl)


def combine_bwd(cfg, dmix, o_sb, o_dl, g_sb, g_dl):
    S, DH = cfg.S, cfg.DH

    def body(dm_ref, osb_ref, odl_ref, gsb_ref, gdl_ref, dsb_ref, ddl_ref, dl_ref, dgsb_ref, dgdl_ref):
        @pl.when(pl.program_id(0) == 0)
        def _():
            dgsb_ref[...] = jnp.zeros_like(dgsb_ref)
            dgdl_ref[...] = jnp.zeros_like(dgdl_ref)

        for h in range(cfg.HSB):
            c = slice(h * DH, (h + 1) * DH)
            dx, dgx = _rms_bwd(dm_ref[:, c], osb_ref[:, c], gsb_ref[:, c])
            dsb_ref[:, c] = dx.astype(BF16)
            dgsb_ref[:, c] += jnp.sum(dgx, axis=0, keepdims=True)
        for h in range(cfg.HDL):
            c = slice(h * DH, (h + 1) * DH)
            o = odl_ref[:, c]
            dx, dgx = _rms_bwd(dm_ref[:, cfg.DSB + h * DH:cfg.DSB + (h + 1) * DH], o, gdl_ref[:, c])
            ddl_ref[:, c] = dx.astype(BF16)
            dl_ref[:, c] = jnp.broadcast_to(jnp.sum(dx * o, axis=-1, keepdims=True), dx.shape)
            dgdl_ref[:, c] += jnp.sum(dgx, axis=0, keepdims=True)

    return pl.pallas_call(
        body, name="combine_bwd", grid=(S // cfg.TM,),
        in_specs=[_row(cfg, cfg.DMIX), _row(cfg, cfg.DSB), _row(cfg, cfg.DDL), _vec(cfg.DSB), _vec(cfg.DDL)],
        out_specs=[_row(cfg, cfg.DSB), _row(cfg, cfg.DDL), _row(cfg, cfg.DDL), _vec(cfg.DSB), _vec(cfg.DDL)],
        out_shape=[jax.ShapeDtypeStruct((S, cfg.DSB), BF16), jax.ShapeDtypeStruct((S, cfg.DDL), BF16),
                   jax.ShapeDtypeStruct((S, cfg.DDL), F32), jax.ShapeDtypeStruct((1, cfg.DSB), F32),
                   jax.ShapeDtypeStruct((1, cfg.DDL), F32)],
        compiler_params=_cp(cfg, ("arbitrary",)),
    )(dmix, o_sb, o_dl, g_sb, g_dl)


SUB = 8


def _shift_down(u, prev, j):
    rolled = pltpu.roll(u, j, 0)
    row = lax.broadcasted_iota(jnp.int32, (SUB, u.shape[1]), 0)
    head = jnp.where(row >= j, rolled[:SUB], pltpu.roll(prev, j, 0))
    return jnp.concatenate([head, rolled[SUB:]], axis=0)


def _shift_up(u, nxt, j):
    n = u.shape[0]
    rolled = pltpu.roll(u, n - j, 0)
    row = lax.broadcasted_iota(jnp.int32, (SUB, u.shape[1]), 0)
    tail = jnp.where(row < SUB - j, rolled[n - SUB:], pltpu.roll(nxt, SUB - j, 0))
    return jnp.concatenate([rolled[:n - SUB], tail], axis=0)


def _conv(u, s1, s2, cw, cb):
    return u * cw[2:3, :] + s1 * cw[1:2, :] + s2 * cw[0:1, :] + cb


def _chunk_rows(cfg):
    ch = min(cfg.FCH, cfg.S)
    return ch, cfg.S // ch


def ffn_fwd(cfg, h2, w_up, conv_w, conv_b):
    S, D = h2.shape
    tn, nt = cfg.TNF, cfg.FFP // cfg.TNF
    ch, nch = _chunk_rows(cfg)

    def body(h_ref, wg_ref, wv_ref, cwg_ref, cwv_ref, cbg_ref, cbv_ref, u_ref, y_ref):
        prev = [jnp.zeros((SUB, tn), F32)] * 2
        for ci in range(nch):
            rows = pl.ds(ci * ch, ch)
            h = h_ref[rows, :]
            us = [_dot(h, wg_ref[...]), _dot(h, wv_ref[...])]
            cs = []
            for i, (cw_ref, cb_ref) in enumerate(((cwg_ref, cbg_ref), (cwv_ref, cbv_ref))):
                u_ref[i, rows, :] = us[i]
                cs.append(_conv(us[i], _shift_down(us[i], prev[i], 1), _shift_down(us[i], prev[i], 2),
                                cw_ref[...], cb_ref[...]))
            y_ref[rows, :] = (_gelu(cs[0])[0] * cs[1]).astype(BF16)
            prev = [u[ch - SUB:] for u in us]

    return pl.pallas_call(
        body, name="ffn_fwd", grid=(nt,),
        in_specs=[pl.BlockSpec((S, D), lambda n: (0, 0)),
                  pl.BlockSpec((D, tn), lambda n: (0, n)), pl.BlockSpec((D, tn), lambda n: (0, n + nt)),
                  pl.BlockSpec((3, tn), lambda n: (0, n)), pl.BlockSpec((3, tn), lambda n: (0, n + nt)),
                  pl.BlockSpec((1, tn), lambda n: (0, n)), pl.BlockSpec((1, tn), lambda n: (0, n + nt))],
        out_specs=[pl.BlockSpec((2, S, tn), lambda n: (0, 0, n)), pl.BlockSpec((S, tn), lambda n: (0, n))],
        out_shape=[jax.ShapeDtypeStruct((2, S, cfg.FFP), F32), jax.ShapeDtypeStruct((S, cfg.FFP), BF16)],
        compiler_params=_cp(cfg, ("parallel",)),
    )(h2, w_up, w_up, conv_w, conv_w, conv_b, conv_b)


def ffn_bwd(cfg, df, w_down, u, conv_w, conv_b):
    S, D = df.shape
    tn, nt = cfg.TNF, cfg.FFP // cfg.TNF

    ch, nch = _chunk_rows(cfg)

    def body(df_ref, wd_ref, u_ref, cwg_ref, cwv_ref, cbg_ref, cbv_ref, du_ref, dwd_ref, dcw_ref, dcb_ref):
        cws = (cwg_ref[...], cwv_ref[...])
        cbs = (cbg_ref[...], cbv_ref[...])
        zero = jnp.zeros((SUB, tn), F32)
        nxt = [zero, zero]
        dws = [[jnp.zeros((1, tn), F32)] * 4 for _ in range(2)]
        dwd = jnp.zeros((tn, D), F32)
        for ci in reversed(range(nch)):
            rows = pl.ds(ci * ch, ch)
            dfv = df_ref[rows, :]
            dy = _dot_nt(dfv, wd_ref[...])
            us, s1, s2, cs = [], [], [], []
            for i in range(2):
                u = u_ref[i, rows, :]
                prev = u_ref[i, pl.ds(ci * ch - SUB, SUB), :] if ci else zero
                us.append(u)
                s1.append(_shift_down(u, prev, 1))
                s2.append(_shift_down(u, prev, 2))
                cs.append(_conv(u, s1[i], s2[i], cws[i], cbs[i]))
            gl, t = _gelu(cs[0])
            dwd = dwd + _dot_tn((gl * cs[1]).astype(BF16), dfv)
            dcs = (dy * cs[1] * _gelu_grad(cs[0], t), dy * gl)
            for i, dc in enumerate(dcs):
                du = dc * cws[i][2:3, :] + _shift_up(dc, nxt[i], 1) * cws[i][1:2, :] + _shift_up(dc, nxt[i], 2) * cws[i][0:1, :]
                du_ref[i, rows, :] = du.astype(BF16)
                for j, tap in enumerate((s2[i], s1[i], us[i])):
                    dws[i][j] = dws[i][j] + jnp.sum(dc * tap, axis=0, keepdims=True)
                dws[i][3] = dws[i][3] + jnp.sum(dc, axis=0, keepdims=True)
            nxt = [dc[:SUB] for dc in dcs]
        dwd_ref[...] = dwd.astype(BF16)
        for i in range(2):
            for j in range(3):
                dcw_ref[i, j:j + 1, :] = dws[i][j]
            dcb_ref[i] = dws[i][3]

    return pl.pallas_call(
        body, name="ffn_bwd", grid=(nt,),
        in_specs=[pl.BlockSpec((S, D), lambda n: (0, 0)), pl.BlockSpec((tn, D), lambda n: (n, 0)),
                  pl.BlockSpec((2, S, tn), lambda n: (0, 0, n)),
                  pl.BlockSpec((3, tn), lambda n: (0, n)), pl.BlockSpec((3, tn), lambda n: (0, n + nt)),
                  pl.BlockSpec((1, tn), lambda n: (0, n)), pl.BlockSpec((1, tn), lambda n: (0, n + nt))],
        out_specs=[pl.BlockSpec((2, S, tn), lambda n: (0, 0, n)), pl.BlockSpec((tn, D), lambda n: (n, 0)),
                   pl.BlockSpec((2, 3, tn), lambda n: (0, 0, n)), pl.BlockSpec((2, 1, tn), lambda n: (0, 0, n))],
        out_shape=[jax.ShapeDtypeStruct((2, S, cfg.FFP), BF16), jax.ShapeDtypeStruct((cfg.FFP, D), BF16),
                   jax.ShapeDtypeStruct((2, 3, cfg.FFP), F32), jax.ShapeDtypeStruct((2, 1, cfg.FFP), F32)],
        compiler_params=_cp(cfg, ("parallel",)),
    )(df, w_down, u, conv_w, conv_w, conv_b, conv_b)


def rope_tables(cfg):
    inv_freq = ROPE_THETA ** (-jnp.arange(0, cfg.DH, 2, dtype=F32) / cfg.DH)
    ang = jnp.arange(cfg.S, dtype=F32)[:, None] * inv_freq[None, :]
    cos, sin = jnp.cos(ang), jnp.sin(ang)
    return jnp.concatenate([cos, cos], axis=1), jnp.concatenate([-sin, sin], axis=1)


class LocalWeights:
    def __init__(self, w_in, w_out, w_up, conv_w, w_down):
        self.w = (w_in, w_out, w_up, conv_w, w_down)
        self.grads = {}

    def weights_first(self):
        return self.w[0], self.w[3]

    def start_rest(self):
        return None

    def weights_rest(self, after):
        return self.w[1], self.w[2], self.w[4]

    def reduce_start(self, grads):
        self.grads.update(grads)
        return None

    def reduce_wait(self, names, after):
        pass


def _after(a, token):
    return a if token is None else a + token[0, 0].astype(a.dtype)


def local_step(cfg, comm, x, target, g1, g2, g3, g4, g_sb, g_dl, conv_b):
    S, D = cfg.S, cfg.D
    cos2, sin2 = rope_tables(cfg)
    full = lambda r, c: pl.BlockSpec((r, c), lambda j, k: (0, 0))

    w_in, conv_w = comm.weights_first()
    h1 = rms_fwd(cfg, x, g1)
    qkv3 = qkv_proj(cfg, h1, w_in, _after(cos2, comm.start_rest()), sin2)
    o_sb, tsum = sb_fwd(cfg, qkv3)
    o_dl, lse_tot = dil_fwd(cfg, qkv3)
    mixed = combine_fwd(cfg, o_sb, o_dl, g_sb, g_dl)
    w_out, w_up, w_down = comm.weights_rest(after=mixed)
    tn = cfg.TN
    mo = _mm(cfg, "mix_out", mixed, w_out, nt=False, grid=(D // tn, 1),
             a_spec=full(S, cfg.DMIX), b_spec=pl.BlockSpec((cfg.DMIX, tn), lambda j, k: (0, j)),
             o_spec=pl.BlockSpec((S, tn), lambda j, k: (0, j)),
             out_shape=jax.ShapeDtypeStruct((S, D), F32), acc_shape=(8, LANE))
    x1, h2 = mid_fwd(cfg, x, mo, g2, g3)
    u, y = ffn_fwd(cfg, h2, w_up, conv_w, conv_b)
    tk = cfg.FFP // 2
    f = _mm(cfg, "ffn_down", y, w_down, nt=False, grid=(D // tn, cfg.FFP // tk),
            a_spec=pl.BlockSpec((S, tk), lambda j, k: (0, k)), b_spec=pl.BlockSpec((tk, tn), lambda j, k: (k, j)),
            o_spec=pl.BlockSpec((S, tn), lambda j, k: (0, j)),
            out_shape=jax.ShapeDtypeStruct((S, D), F32), acc_shape=(S, tn))
    dout, df, dg4, loss = final_fwd_bwd(cfg, x1, f, g4, target)

    du, dw_down, dconv_w, dconv_b = ffn_bwd(cfg, df, w_down, u, conv_w, conv_b)
    kt = cfg.FFP // tk
    dh2 = _mm(cfg, "d_h2", du, w_up, nt=True, grid=(D // tn, 2 * kt),
              a_spec=pl.BlockSpec((None, S, tk), lambda j, k: (k // kt, 0, k % kt)),
              b_spec=pl.BlockSpec((tn, tk), lambda j, k: (j, k)),
              o_spec=pl.BlockSpec((S, tn), lambda j, k: (0, j)),
              out_shape=jax.ShapeDtypeStruct((S, D), F32), acc_shape=(S, tn))
    nf = cfg.FFP // tn if cfg.FFP % tn == 0 else None
    tnu = tn if nf else cfg.TNF
    nf = cfg.FFP // tnu
    dw_up = _mm_tn(cfg, "d_w_up", h2, du, grid=(2 * nf,),
                   a_spec=pl.BlockSpec((S, D), lambda j: (0, 0)),
                   b_spec=pl.BlockSpec((None, S, tnu), lambda j: (j // nf, 0, j % nf)),
                   o_spec=pl.BlockSpec((D, tnu), lambda j: (0, j)),
                   out_shape=jax.ShapeDtypeStruct((D, cfg.FF2P), BF16))
    dx1, dmo, dg3, dg2 = mid_bwd(cfg, dh2, x1, g3, dout, mo, g2)

    dmix = _mm(cfg, "d_mixed", dmo, w_out, nt=True, grid=(cfg.DMIX // tn, 1),
               a_spec=full(S, D), b_spec=pl.BlockSpec((tn, D), lambda j, k: (j, 0)),
               o_spec=pl.BlockSpec((S, tn), lambda j, k: (0, j)),
               out_shape=jax.ShapeDtypeStruct((S, cfg.DMIX), F32), acc_shape=(8, LANE))
    dw_out = _mm_tn(cfg, "d_w_out", mixed, dmo, grid=(D // tn,),
                    a_spec=pl.BlockSpec((S, cfg.DMIX), lambda j: (0, 0)),
                    b_spec=pl.BlockSpec((S, tn), lambda j: (0, j)),
                    o_spec=pl.BlockSpec((cfg.DMIX, tn), lambda j: (0, j)),
                    out_shape=jax.ShapeDtypeStruct((cfg.DMIX, D), BF16))
    token = comm.reduce_start(dict(w_out=dw_out, w_up=dw_up, w_down=dw_down))
    do_sb, do_dl, delta, dg_sb, dg_dl = combine_bwd(cfg, dmix, o_sb, o_dl, _after(g_sb, token), g_dl)
    d_sb3 = sb_bwd(cfg, qkv3, do_sb, tsum)
    dqkv3 = dil_bwd(cfg, qkv3, do_dl, delta, lse_tot, cos2, sin2, d_sb3)
    comm.reduce_wait(("w_out", "w_up", "w_down"), after=dqkv3)
    tkq = min(tn, cfg.DSB)
    kq = cfg.DSB // tkq
    dw_in = _mm_tn(cfg, "d_w_in", h1, dqkv3, grid=(6 * kq,),
                   a_spec=pl.BlockSpec((S, D), lambda j: (0, 0)),
                   b_spec=pl.BlockSpec((None, S, tkq), lambda j: (j // kq, 0, j % kq)),
                   o_spec=pl.BlockSpec((D, tkq), lambda j: (0, j)),
                   out_shape=jax.ShapeDtypeStruct((D, 6 * cfg.DSB), BF16))
    token = comm.reduce_start(dict(w_in=dw_in))
    dh1 = _mm(cfg, "d_h1", dqkv3, w_in, nt=True, grid=(D // tn, 6),
              a_spec=pl.BlockSpec((None, S, cfg.DSB), lambda j, k: (k, 0, 0)),
              b_spec=pl.BlockSpec((tn, cfg.DSB), lambda j, k: (j, k)),
              o_spec=pl.BlockSpec((S, tn), lambda j, k: (0, j)),
              out_shape=jax.ShapeDtypeStruct((S, D), F32), acc_shape=(S, tn), dep=token)
    grad_x, dg1 = first_bwd(cfg, dh1, x, g1, dx1)
    comm.reduce_wait(("w_in",), after=grad_x)
    small = dict(loss=loss, g1=dg1, g2=dg2, g3=dg3, g4=dg4, g_sb=dg_sb, g_dl=dg_dl,
                 conv_b=dconv_b.reshape(1, cfg.FF2P), conv_w=dconv_w.transpose(1, 0, 2).reshape(3, cfg.FF2P))
    return grad_x, small


ANY = pl.BlockSpec(memory_space=pl.ANY)


def _me():
    return lax.axis_index("x"), lax.axis_index("y"), lax.axis_index("c")


def _other_chips(x, y):
    return [(1 - x, y), (x, 1 - y), (1 - x, 1 - y)]


def pad_conv_w(cfg, conv_w):
    r, c = conv_w.shape

    def body(w_ref, o_ref):
        o_ref[:, :c] = w_ref[...]
        o_ref[:, c:] = jnp.zeros((r, cfg.FSHP - c), F32)

    return pl.pallas_call(body, name="pad_conv_w", out_shape=jax.ShapeDtypeStruct((r, cfg.FSHP), F32))(conv_w)


def _tile2(r, c):
    return (256, c) if r % 256 == 0 else (r, 512 if c % 512 == 0 else c)


def cast_into(cfg, name, w, pos):
    r, c = w.shape
    _, nr, _, nc = _slab(cfg, name, 0)
    tm, tc = _tile2(r, c)
    wc = nc if tc == c else tc
    assert nr == r and (nc == c or tc == c)

    def body(pos_ref, w_ref, full_ref, scr, sem):
        scr[:, :tc] = w_ref[...].astype(BF16)
        if wc > tc:
            scr[:, tc:] = jnp.zeros((tm, wc - tc), BF16)
        r0, _, c0, _ = _slab(cfg, name, pos_ref[0])
        rows = pl.ds(pl.multiple_of(r0 + pl.program_id(0) * tm, 16), tm)
        cols = pl.ds(pl.multiple_of(c0 + pl.program_id(1) * tc, LANE), wc)
        cp = pltpu.make_async_copy(scr, full_ref.at[rows, cols], sem)
        cp.start()
        cp.wait()

    return pl.pallas_call(
        body, name=f"cast_{name}",
        grid_spec=pltpu.PrefetchScalarGridSpec(
            num_scalar_prefetch=1, grid=(r // tm, c // tc),
            in_specs=[pl.BlockSpec((tm, tc), lambda i, j, p: (i, j))], out_specs=ANY,
            scratch_shapes=[pltpu.VMEM((tm, wc), BF16), pltpu.SemaphoreType.DMA]),
        out_shape=jax.ShapeDtypeStruct(_full_shape(cfg, name), BF16),
        compiler_params=_cp(cfg, ("arbitrary", "arbitrary")),
    )(pos, w)


HBM = pl.BlockSpec(memory_space=pltpu.HBM)
SEM = pl.BlockSpec(memory_space=pltpu.SEMAPHORE)
TOKEN = pl.BlockSpec(memory_space=pltpu.VMEM)
EFFECT = pltpu.SideEffectType.DATAFLOW_SIDE_EFFECTING


def _slab(cfg, name, k):
    D = cfg.D
    if name == "w_in":
        cin = 6 * cfg.DSB // N_CHIPS
        return 0, D, k * cin, cin
    if name == "w_out":
        rout = cfg.DMIX // N_CHIPS
        return k * rout, rout, 0, D
    if name == "w_up":
        return 0, D, k * cfg.FSHP, cfg.FSHP
    rdn = cfg.FSH // 2
    return (k // 2) * cfg.FSHP + (k % 2) * rdn, rdn, 0, D


def _full_shape(cfg, name):
    return dict(w_in=(cfg.D, 6 * cfg.DSB), w_out=(cfg.DMIX, cfg.D), w_up=(cfg.D, cfg.FF2P), w_down=(cfg.FFP, cfg.D))[name]


def _half(cfg, name, ref, k, h):
    r0, nr, c0, nc = _slab(cfg, name, k)
    return ref.at[pl.ds(r0 + h * (nr // 2), nr // 2), pl.ds(c0, nc)]


def _rows_half(ref, h):
    nr = ref.shape[0] // 2
    return ref.at[pl.ds(h * nr, nr), :]


def _remote(src, dst, send_sem, recv_sem, dev):
    return pltpu.make_async_remote_copy(src_ref=src, dst_ref=dst, send_sem=send_sem, recv_sem=recv_sem,
                                        device_id=dev, device_id_type=MESH)


def gather_first(cfg, g_in, sh_cw):
    def body(in_ref, cw_ref, g_in, g_cw, token, ssem, rsem, fssem, frsem, lsem):
        x, y, c = _me()
        me, sib = 2 * x + y, (x, y, 1 - c)
        cw_slot = lambda k: g_cw.at[:, pl.ds(k * cfg.FSHP, cfg.FSHP)]
        local = [pltpu.make_async_copy(cw_ref, cw_slot(me), lsem.at[0])]
        sends = []
        for j, (px, py) in enumerate(_other_chips(x, y)):
            mine = _half(cfg, "w_in", g_in, me, c)
            sends.append(_remote(mine, mine, ssem.at[j], rsem.at[j], (px, py, c)))
            sends.append(_remote(cw_ref, cw_slot(me), ssem.at[3 + j], rsem.at[3 + j], (px, py, c)))
        for cp in local + sends:
            cp.start()
        for j, (px, py) in enumerate(_other_chips(x, y)):
            k = 2 * px + py
            landed = _half(cfg, "w_in", g_in, k, c)
            _remote(landed, landed, ssem.at[j], rsem.at[j], (px, py, c)).wait_recv()
            fwd = _remote(landed, landed, fssem.at[j], frsem.at[j], sib)
            fwd.start()
            sends.append(fwd)
        for j, (px, py) in enumerate(_other_chips(x, y)):
            k = 2 * px + py
            passed = _half(cfg, "w_in", g_in, k, 1 - c)
            _remote(passed, passed, fssem.at[j], frsem.at[j], sib).wait_recv()
            _remote(cw_ref, cw_slot(k), ssem.at[3 + j], rsem.at[3 + j], (px, py, c)).wait_recv()
        for cp in sends:
            cp.wait_send()
        for cp in local:
            cp.wait()
        token[...] = jnp.zeros_like(token)

    return pl.pallas_call(
        body, name="gather_first", in_specs=[ANY, ANY], out_specs=[ANY, ANY, TOKEN],
        out_shape=[jax.ShapeDtypeStruct(_full_shape(cfg, "w_in"), BF16), jax.ShapeDtypeStruct((3, cfg.FF2P), F32),
                   jax.ShapeDtypeStruct((8, LANE), F32)],
        input_output_aliases={0: 0},
        scratch_shapes=[pltpu.SemaphoreType.DMA((6,)), pltpu.SemaphoreType.DMA((6,)), pltpu.SemaphoreType.DMA((3,)),
                        pltpu.SemaphoreType.DMA((3,)), pltpu.SemaphoreType.DMA((1,))],
    )(g_in, sh_cw)


REST = ("w_out", "w_up", "w_down")


def _hbm(a):
    return pltpu.with_memory_space_constraint(a, pltpu.HBM)


def gather_start(cfg, fulls, after):
    n = len(REST)

    def body(*refs):
        lands = refs[:n]
        ssem, rsem = refs[n + 1], refs[n + 2]
        token = refs[-1]
        x, y, c = _me()
        me = 2 * x + y
        for i, name in enumerate(REST):
            mine = _half(cfg, name, lands[i], me, c)
            for j, (px, py) in enumerate(_other_chips(x, y)):
                _remote(mine, mine, ssem.at[3 * i + j], rsem.at[3 * i + j], (px, py, c)).start()
        token[...] = jnp.zeros_like(token)

    ops = [_hbm(a) for a in fulls]
    outs = pl.pallas_call(
        body, name="gather_start",
        in_specs=[HBM] * n + [ANY],
        out_specs=[SEM, SEM] + [HBM] * n + [TOKEN],
        out_shape=[pltpu.SemaphoreType.DMA((3 * n,)), pltpu.SemaphoreType.DMA((3 * n,))]
        + [pltpu.HBM(a.shape, a.dtype) for a in ops] + [jax.ShapeDtypeStruct((8, LANE), F32)],
        input_output_aliases={i: 2 + i for i in range(n)},
        compiler_params=pltpu.CompilerParams(has_side_effects=EFFECT),
    )(*ops, after)
    return outs[0], outs[1], outs[2:2 + n], outs[-1]


def gather_wait(cfg, ssem, rsem, lands, after):
    n = len(REST)

    def body(*refs):
        lands_ = refs[:n]
        ssem_, rsem_ = refs[n], refs[n + 1]
        x, y, c = _me()
        me = 2 * x + y
        for i, name in enumerate(REST):
            for j, (px, py) in enumerate(_other_chips(x, y)):
                cp = _remote(_half(cfg, name, lands_[i], me, c), _half(cfg, name, lands_[i], 2 * px + py, c),
                             ssem_.at[3 * i + j], rsem_.at[3 * i + j], (px, py, c))
                cp.wait_send()
                cp.wait_recv()

    return pl.pallas_call(
        body, name="gather_wait",
        in_specs=[HBM] * n + [SEM, SEM, ANY], out_specs=[HBM] * n,
        out_shape=[pltpu.HBM(a.shape, a.dtype) for a in lands],
        input_output_aliases={i: i for i in range(n)},
        compiler_params=pltpu.CompilerParams(has_side_effects=EFFECT),
    )(*lands, ssem, rsem, after)


def gather_finish(cfg, lands):
    n = len(REST)
    rdn = cfg.FSH // 2
    zpad = jnp.zeros((cfg.FSHP - cfg.FSH, cfg.D), BF16)

    def body(*refs):
        z_ref, outs = refs[0], refs[n + 1:2 * n + 1]
        ssem, rsem, lsem = refs[2 * n + 1:]
        x, y, c = _me()
        sib = (x, y, 1 - c)
        local = [pltpu.make_async_copy(z_ref, outs[2].at[pl.ds(h * cfg.FSHP + 2 * rdn, cfg.FSHP - cfg.FSH), :],
                                       lsem.at[h]) for h in range(2)]
        fwds = []
        for i, name in enumerate(REST):
            for j, (px, py) in enumerate(_other_chips(x, y)):
                landed = _half(cfg, name, outs[i], 2 * px + py, c)
                fwds.append(_remote(landed, landed, ssem.at[3 * i + j], rsem.at[3 * i + j], sib))
        for cp in local + fwds:
            cp.start()
        for i, name in enumerate(REST):
            for j, (px, py) in enumerate(_other_chips(x, y)):
                passed = _half(cfg, name, outs[i], 2 * px + py, 1 - c)
                _remote(passed, passed, ssem.at[3 * i + j], rsem.at[3 * i + j], sib).wait_recv()
        for cp in fwds:
            cp.wait_send()
        for cp in local:
            cp.wait()

    return pl.pallas_call(
        body, name="gather_finish", in_specs=[ANY] * (n + 1), out_specs=[ANY] * n,
        out_shape=[jax.ShapeDtypeStruct(a.shape, a.dtype) for a in lands],
        input_output_aliases={1 + i: i for i in range(n)},
        scratch_shapes=[pltpu.SemaphoreType.DMA((3 * n,)), pltpu.SemaphoreType.DMA((3 * n,)),
                        pltpu.SemaphoreType.DMA((2,))],
    )(zpad, *lands)


def pair_send(cfg, grads):
    names = list(grads)
    n = len(names)

    def half_shape(name):
        _, nr, _, nc = _slab(cfg, name, 0)
        return (N_CHIPS, nr // 2, nc)

    def body(*refs):
        srcs, theirs = refs[:n], refs[n:2 * n]
        ssem, rsem = refs[2 * n:]
        x, y, c = _me()
        cps = []
        for i, name in enumerate(names):
            for k in range(N_CHIPS):
                cps.append(_remote(_half(cfg, name, srcs[i], k, 1 - c), theirs[i].at[k],
                                   ssem.at[N_CHIPS * i + k], rsem.at[N_CHIPS * i + k], (x, y, 1 - c)))
        for cp in cps:
            cp.start()
        for cp in cps:
            cp.wait()

    outs = pl.pallas_call(
        body, name="pair_send_" + "_".join(names), in_specs=[ANY] * n, out_specs=[ANY] * n,
        out_shape=[jax.ShapeDtypeStruct(half_shape(name), BF16) for name in names],
        scratch_shapes=[pltpu.SemaphoreType.DMA((N_CHIPS * n,))] * 2,
    )(*[grads[k] for k in names])
    return dict(zip(names, outs))


def pair_sum(cfg, name, grad, theirs, pos):
    _, r, c = theirs.shape
    tm, tc = _tile2(r, c)

    ni, nj = r // tm, c // tc
    total = N_CHIPS * ni * nj

    def body(pos_ref, g_ref, t_ref, o_ref, scr, sem):
        step = (pl.program_id(0) * ni + pl.program_id(1)) * nj + pl.program_id(2)

        def fetch(flat, slot):
            k, rem = flat // (ni * nj), flat % (ni * nj)
            r0, nr, c0, _ = _slab(cfg, name, k)
            rows = pl.ds(pl.multiple_of(r0 + pos_ref[1] * (nr // 2) + (rem // nj) * tm, 16), tm)
            cols = pl.ds(pl.multiple_of(c0 + (rem % nj) * tc, LANE), tc)
            return pltpu.make_async_copy(g_ref.at[rows, cols], scr.at[slot], sem.at[slot])

        @pl.when(step == 0)
        def _():
            fetch(0, 0).start()

        @pl.when(step + 1 < total)
        def _():
            fetch(step + 1, (step + 1) % 2).start()

        fetch(step, step % 2).wait()
        o_ref[...] = (scr[step % 2].astype(F32) + t_ref[...].astype(F32)).astype(BF16)

    blk = pl.BlockSpec((None, tm, tc), lambda k, i, j, p: (k, i, j))
    return pl.pallas_call(
        body, name=f"pair_sum_{name}",
        grid_spec=pltpu.PrefetchScalarGridSpec(
            num_scalar_prefetch=1, grid=(N_CHIPS, ni, nj), in_specs=[ANY, blk], out_specs=blk,
            scratch_shapes=[pltpu.VMEM((2, tm, tc), BF16), pltpu.SemaphoreType.DMA((2,))]),
        out_shape=jax.ShapeDtypeStruct(theirs.shape, BF16),
        compiler_params=_cp(cfg, ("arbitrary",) * 3),
    )(pos, grad, theirs)


def scatter_start(cfg, pres, after):
    names = list(pres)
    n = len(names)

    def body(*refs):
        srcs, lands = refs[:n], refs[n:2 * n]
        ssem, rsem = refs[2 * n + 1], refs[2 * n + 2]
        token = refs[-1]
        x, y, c = _me()
        for i in range(n):
            for j, (px, py) in enumerate(_other_chips(x, y)):
                _remote(srcs[i].at[2 * px + py], lands[i].at[j], ssem.at[3 * i + j], rsem.at[3 * i + j], (px, py, c)).start()
        token[...] = jnp.zeros_like(token)

    lands = [lax.empty((3,) + pres[k].shape[1:], BF16) for k in names]
    ops = [_hbm(a) for a in [pres[k] for k in names] + lands]
    outs = pl.pallas_call(
        body, name="scatter_start_" + "_".join(names),
        in_specs=[HBM] * (2 * n) + [ANY],
        out_specs=[SEM, SEM] + [HBM] * (2 * n) + [TOKEN],
        out_shape=[pltpu.SemaphoreType.DMA((3 * n,)), pltpu.SemaphoreType.DMA((3 * n,))]
        + [pltpu.HBM(a.shape, a.dtype) for a in ops] + [jax.ShapeDtypeStruct((8, LANE), F32)],
        input_output_aliases={i: 2 + i for i in range(2 * n)},
        compiler_params=pltpu.CompilerParams(has_side_effects=EFFECT),
    )(*ops, after)
    return outs[0], outs[1], dict(zip(names, outs[2:2 + n])), dict(zip(names, outs[2 + n:2 + 2 * n])), outs[-1]


def scatter_wait(cfg, ssem, rsem, pres, lands, after):
    names = list(pres)
    n = len(names)

    def body(*refs):
        srcs, lands_ = refs[:n], refs[n:2 * n]
        ssem_, rsem_ = refs[2 * n], refs[2 * n + 1]
        x, y, c = _me()
        for i in range(n):
            for j, (px, py) in enumerate(_other_chips(x, y)):
                cp = _remote(srcs[i].at[2 * px + py], lands_[i].at[j], ssem_.at[3 * i + j], rsem_.at[3 * i + j], (px, py, c))
                cp.wait_send()
                cp.wait_recv()

    ops = [pres[k] for k in names] + [lands[k] for k in names]
    outs = pl.pallas_call(
        body, name="scatter_wait_" + "_".join(names),
        in_specs=[HBM] * (2 * n) + [SEM, SEM, ANY], out_specs=[HBM] * (2 * n),
        out_shape=[pltpu.HBM(a.shape, a.dtype) for a in ops],
        input_output_aliases={i: i for i in range(2 * n)},
        compiler_params=pltpu.CompilerParams(has_side_effects=EFFECT),
    )(*ops, ssem, rsem, after)
    return dict(zip(names, outs[:n])), dict(zip(names, outs[n:]))


def sum_landed(cfg, name, pre, land, pos):
    _, r, c = pre.shape
    tm, tc = _tile2(r, c)
    nrt = r // tm

    def body(pos_ref, p_ref, l_ref, o_ref):
        acc = p_ref[...].astype(F32)
        for j in range(3):
            acc = acc + l_ref[j].astype(F32)
        o_ref[...] = acc

    return pl.pallas_call(
        body, name=f"sum_landed_{name}",
        grid_spec=pltpu.PrefetchScalarGridSpec(
            num_scalar_prefetch=1, grid=(nrt, c // tc),
            in_specs=[pl.BlockSpec((None, tm, tc), lambda i, j, p: (p[0], i, j)),
                      pl.BlockSpec((3, tm, tc), lambda i, j, p: (0, i, j))],
            out_specs=pl.BlockSpec((tm, tc), lambda i, j, p: (p[1] * nrt + i, j))),
        out_shape=jax.ShapeDtypeStruct((2 * r, c), F32), compiler_params=_cp(cfg, ("parallel", "parallel")),
    )(pos, pre, land)


def half_swap(cfg, sums):
    names = list(sums)
    n = len(names)

    def body(*refs):
        outs = refs[n:2 * n]
        ssem, rsem = refs[2 * n:]
        x, y, c = _me()
        cps = [_remote(_rows_half(outs[i], c), _rows_half(outs[i], c), ssem.at[i], rsem.at[i], (x, y, 1 - c))
               for i in range(n)]
        for cp in cps:
            cp.start()
        for i in range(n):
            theirs = _rows_half(outs[i], 1 - c)
            _remote(theirs, theirs, ssem.at[i], rsem.at[i], (x, y, 1 - c)).wait_recv()
        for cp in cps:
            cp.wait_send()

    outs = pl.pallas_call(
        body, name="half_swap_" + "_".join(names), in_specs=[ANY] * n, out_specs=[ANY] * n,
        out_shape=[jax.ShapeDtypeStruct(sums[k].shape, F32) for k in names],
        input_output_aliases={i: i for i in range(n)},
        scratch_shapes=[pltpu.SemaphoreType.DMA((n,))] * 2,
    )(*[sums[k] for k in names])
    return dict(zip(names, outs))


class MeshWeights:
    def __init__(self, cfg, w_sh):
        self.cfg = cfg
        self.pos = jnp.stack([2 * lax.axis_index("x") + lax.axis_index("y"), lax.axis_index("c")]).astype(jnp.int32)
        self.full = {k: cast_into(cfg, k, w_sh[k], self.pos) for k in ("w_in",) + REST}
        self.conv_w = pad_conv_w(cfg, w_sh["conv_w"])
        self.inflight = {}
        self.grads = {}

    def weights_first(self):
        w_in, conv_w, self.token = gather_first(self.cfg, self.full["w_in"], self.conv_w)
        return w_in, conv_w

    def start_rest(self):
        out = gather_start(self.cfg, [self.full[k] for k in REST], self.token)
        self.rest = out[:3]
        return out[3]

    def weights_rest(self, after):
        return gather_finish(self.cfg, gather_wait(self.cfg, *self.rest, after))

    def reduce_start(self, grads):
        theirs = pair_send(self.cfg, grads)
        pres = {k: pair_sum(self.cfg, k, grads[k], theirs[k], self.pos) for k in grads}
        out = scatter_start(self.cfg, pres, jnp.zeros((8, LANE), F32))
        self.inflight[tuple(grads)] = out[:4]
        return out[4]

    def reduce_wait(self, names, after):
        cfg = self.cfg
        pres, lands = scatter_wait(cfg, *self.inflight.pop(tuple(names)), after)
        sums = {k: sum_landed(cfg, k, pres[k], lands[k], self.pos) for k in names}
        self.grads.update(half_swap(cfg, sums))


def allreduce_small(cfg, vec):
    R = vec.shape[0]

    def body(v_ref, o_ref, buf, send_sems, recv_sems):
        x, y, c = _me()
        me = 4 * x + 2 * y + c
        buf[me] = v_ref[...]
        sends = []
        for k in range(1, N_DEV):
            px, py, pc = x ^ (k >> 2), y ^ ((k >> 1) & 1), c ^ (k & 1)
            sends.append(pltpu.make_async_remote_copy(
                src_ref=v_ref, dst_ref=buf.at[me], send_sem=send_sems.at[k], recv_sem=recv_sems.at[k],
                device_id=(px, py, pc), device_id_type=MESH))
        for cp in sends:
            cp.start()
        for k in range(1, N_DEV):
            px, py, pc = x ^ (k >> 2), y ^ ((k >> 1) & 1), c ^ (k & 1)
            pltpu.make_async_remote_copy(
                src_ref=v_ref, dst_ref=buf.at[4 * px + 2 * py + pc], send_sem=send_sems.at[k],
                recv_sem=recv_sems.at[k], device_id=(px, py, pc), device_id_type=MESH).wait_recv()
        for cp in sends:
            cp.wait_send()
        acc = buf[0]
        for j in range(1, N_DEV):
            acc = acc + buf[j]
        o_ref[...] = acc

    return pl.pallas_call(
        body, name="allreduce_small",
        in_specs=[pl.BlockSpec(memory_space=pltpu.VMEM)], out_specs=pl.BlockSpec(memory_space=pltpu.VMEM),
        out_shape=jax.ShapeDtypeStruct((R, LANE), F32),
        scratch_shapes=[pltpu.VMEM((N_DEV, R, LANE), F32), pltpu.SemaphoreType.DMA((N_DEV,)),
                        pltpu.SemaphoreType.DMA((N_DEV,))],
    )(vec)


def adamw(cfg, name, w, m, v, g_parts, tile):
    r, c = w.shape
    tm, tc = tile[0] or r, tile[1] or c
    assert tc == c or all(g.shape[1] == c for g in g_parts)
    n = len(g_parts)
    bc1 = 1.0 - ADAM_B1 ** ADAM_STEP
    bc2 = 1.0 - ADAM_B2 ** ADAM_STEP

    def body(*refs):
        w_ref, m_ref, v_ref = refs[:3]
        g_refs = refs[3:3 + n]
        g_out, d_out, m_out, v_out = refs[3 + n:]
        g = g_refs[0][:, :tc]
        for gr in g_refs[1:]:
            g = g + gr[:, :tc]
        m_new = ADAM_B1 * m_ref[...] + (1.0 - ADAM_B1) * g
        v_new = ADAM_B2 * v_ref[...] + (1.0 - ADAM_B2) * jnp.square(g)
        m_hat = m_new / bc1
        v_hat = v_new / bc2
        g_out[...] = g
        d_out[...] = -ADAM_LR * (m_hat / (jnp.sqrt(v_hat) + ADAM_EPS) + ADAM_WD * w_ref[...])
        m_out[...] = m_new
        v_out[...] = v_new

    blk = pl.BlockSpec((tm, tc), lambda i, j: (i, j))
    return pl.pallas_call(
        body, name=f"adamw_{name}", grid=(r // tm, c // tc),
        in_specs=[blk] * 3 + [pl.BlockSpec((tm, tc if tc < c else g.shape[1]), lambda i, j: (i, j)) for g in g_parts],
        out_specs=[blk] * 4, out_shape=[jax.ShapeDtypeStruct((r, c), F32)] * 4,
        compiler_params=_cp(cfg, ("parallel", "parallel")),
    )(w, m, v, *g_parts)


SMALL_ORDER = ("loss", "g1", "g2", "g3", "g4", "g_sb", "g_dl", "conv_b", "conv_w")


def pack_small(small):
    rows = []
    for k in SMALL_ORDER:
        a = small[k].reshape(-1, LANE)
        rows.append(a)
    flat = jnp.concatenate(rows, axis=0)
    pad = (-flat.shape[0]) % 8
    return jnp.pad(flat, ((0, pad), (0, 0))), [r.shape[0] for r in rows]


def unpack_small(red, small, counts):
    out, at = {}, 0
    for k, n in zip(SMALL_ORDER, counts):
        out[k] = red[at:at + n].reshape(small[k].shape)
        at += n
    return out


def pad_ff(cfg, a):
    r = a.shape[0]
    return jnp.pad(a.reshape(r, N_CHIPS, cfg.FSH), ((0, 0), (0, 0), (0, cfg.FSHP - cfg.FSH))).reshape(r, cfg.FF2P)


def step(cfg, x, target, gains, w_sh, conv_b, m_all, v_all):
    chip = 2 * lax.axis_index("x") + lax.axis_index("y")
    comm = MeshWeights(cfg, w_sh)
    grad_x, small = local_step(cfg, comm, x, target, gains["g1"], gains["g2"], gains["g3"], gains["g4"],
                               gains["g_sb"], gains["g_dl"], pad_ff(cfg, conv_b))

    packed, counts = pack_small(small)
    red = unpack_small(allreduce_small(cfg, packed), small, counts)

    names = ("w_in", "w_out", "w_up", "w_down")
    tms = dict(w_in=(cfg.TM, None), w_out=(cfg.TM, None), w_up=(cfg.TM // 2, None), w_down=(None, cfg.TN // 2))
    res = {}
    for n in names:
        res[n] = adamw(cfg, n, w_sh[n], m_all[n], v_all[n], [comm.grads[n]], tms[n])
    g_cw = lax.dynamic_slice_in_dim(red["conv_w"].reshape(3, N_CHIPS, cfg.FSHP), chip, 1, axis=1)[:, 0, :cfg.FSH]
    res["conv_w"] = adamw(cfg, "conv_w", w_sh["conv_w"], m_all["conv_w"], v_all["conv_w"], [g_cw], (None, None))
    g_cb = red["conv_b"].reshape(1, N_CHIPS, cfg.FSHP)[:, :, :cfg.FSH].reshape(1, N_CHIPS * cfg.FSH)
    res["conv_b"] = adamw(cfg, "conv_b", conv_b, m_all["conv_b"], v_all["conv_b"], [g_cb], (None, None))
    for k in ("g1", "g2", "g3", "g4", "g_sb", "g_dl"):
        res[k] = adamw(cfg, k, gains[k], m_all[k], v_all[k], [red[k]], (None, None))
    return red["loss"][0, 0], grad_x, res


PARAMS = ("pre_mix_gain", "post_mix_gain", "pre_ffn_gain", "post_ffn_gain", "w_in", "sb_out_gain", "dil_out_gain",
          "w_out", "w_up", "conv_w", "conv_b", "w_down")
SHORT = dict(pre_mix_gain="g1", post_mix_gain="g2", pre_ffn_gain="g3", post_ffn_gain="g4", sb_out_gain="g_sb",
             dil_out_gain="g_dl", w_in="w_in", w_out="w_out", w_up="w_up", conv_w="conv_w", conv_b="conv_b",
             w_down="w_down")


def kernel(x, pre_mix_gain, post_mix_gain, pre_ffn_gain, post_ffn_gain, w_in, sb_out_gain, dil_out_gain, w_out, w_up, conv_w, conv_b, w_down, loss_target, m_pre_mix_gain, m_post_mix_gain, m_pre_ffn_gain, m_post_ffn_gain, m_w_in, m_sb_out_gain, m_dil_out_gain, m_w_out, m_w_up, m_conv_w, m_conv_b, m_w_down, v_pre_mix_gain, v_post_mix_gain, v_pre_ffn_gain, v_post_ffn_gain, v_w_in, v_sb_out_gain, v_dil_out_gain, v_w_out, v_w_up, v_conv_w, v_conv_b, v_w_down):
    cfg = CFG
    w = dict(zip(PARAMS, (pre_mix_gain, post_mix_gain, pre_ffn_gain, post_ffn_gain, w_in, sb_out_gain, dil_out_gain,
                          w_out, w_up, conv_w, conv_b, w_down)))
    m = dict(zip(PARAMS, (m_pre_mix_gain, m_post_mix_gain, m_pre_ffn_gain, m_post_ffn_gain, m_w_in, m_sb_out_gain,
                          m_dil_out_gain, m_w_out, m_w_up, m_conv_w, m_conv_b, m_w_down)))
    v = dict(zip(PARAMS, (v_pre_mix_gain, v_post_mix_gain, v_pre_ffn_gain, v_post_ffn_gain, v_w_in, v_sb_out_gain,
                          v_dil_out_gain, v_w_out, v_w_up, v_conv_w, v_conv_b, v_w_down)))
    sq = lambda a: a.reshape(a.shape[1:])
    ws = {SHORT[k]: sq(a) if a.ndim == 3 else a for k, a in w.items()}
    ms = {SHORT[k]: sq(a) if a.ndim == 3 else a for k, a in m.items()}
    vs = {SHORT[k]: sq(a) if a.ndim == 3 else a for k, a in v.items()}
    gains = {k: ws[k] for k in ("g1", "g2", "g3", "g4", "g_sb", "g_dl")}
    w_sh = {k: ws[k] for k in ("w_in", "w_out", "w_up", "conv_w", "w_down")}
    loss, grad_x, res = step(cfg, sq(x), sq(loss_target), gains, w_sh, ws["conv_b"], ms, vs)
    outs = [loss, grad_x.reshape(x.shape)]
    for i in range(4):
        for k in PARAMS:
            outs.append(res[SHORT[k]][i].reshape(w[k].shape))
    return tuple(outs)
```

```python
import functools
import math
from typing import NamedTuple

import jax
import jax.numpy as jnp
from jax import lax
from jax.experimental import pallas as pl
from jax.experimental.pallas import tpu as pltpu

F32 = jnp.float32
BF16 = jnp.bfloat16
MESH = pl.DeviceIdType.MESH

ROPE_THETA = 10000.0
RMS_EPS = 1e-6
ADAM_LR = 0.001
ADAM_B1 = 0.9
ADAM_B2 = 0.999
ADAM_EPS = 1e-08
ADAM_WD = 0.01
ADAM_STEP = 10
GELU_C = math.sqrt(2.0 / math.pi)
NEG_BIG = -1e30
LANE = 128
N_CHIPS = 4
N_DEV = 8


class Cfg(NamedTuple):
    S: int = 2048
    D: int = 2048
    DH: int = 128
    HSB: int = 8
    HDL: int = 8
    QB: int = 128
    SBT: int = 256
    branches: tuple = ((128, 1), (512, 4), (2048, 16))
    FSH: int = 2752
    FSHP: int = 2816
    TM: int = 256
    TNF: int = 256
    FCH: int = 512
    TN: int = 512
    VMEM_MB: int = 56

    @property
    def DSB(self):
        return self.HSB * self.DH

    @property
    def DDL(self):
        return self.HDL * self.DH

    @property
    def DMIX(self):
        return self.DSB + self.DDL

    @property
    def FFP(self):
        return 2 * self.FSHP

    @property
    def FF2P(self):
        return 4 * self.FSHP


CFG = Cfg()


def _cp(cfg, sem=None):
    return pltpu.CompilerParams(dimension_semantics=sem, vmem_limit_bytes=cfg.VMEM_MB * 2**20)


def _dot(a, b):
    return jnp.dot(a, b, preferred_element_type=F32)


def _dot_nt(a, b):
    return lax.dot_general(a, b, (((1,), (1,)), ((), ())), preferred_element_type=F32)


def _dot_tn(a, b):
    return lax.dot_general(a, b, (((0,), (0,)), ((), ())), preferred_element_type=F32)


def _dot_split(x, u):
    hi = x.astype(BF16)
    lo = (x - hi.astype(F32)).astype(BF16)
    return _dot(hi, u) + _dot(lo, u)


def _rstd(x):
    return lax.rsqrt(jnp.mean(x * x, axis=-1, keepdims=True) + RMS_EPS)


def _rms_bwd(dy, x, g):
    r = _rstd(x)
    xh = x * r
    dxh = dy * g
    dx = r * (dxh - xh * jnp.mean(dxh * xh, axis=-1, keepdims=True))
    return dx, dy * xh


def _gelu(x):
    t = jnp.tanh(GELU_C * (x + 0.044715 * (x * x * x)))
    return 0.5 * x * (1.0 + t), t


def _gelu_grad(x, t):
    return 0.5 * (1.0 + t) + 0.5 * x * (1.0 - t * t) * (GELU_C * (1.0 + 3 * 0.044715 * (x * x)))


def _row(cfg, w):
    return pl.BlockSpec((cfg.TM, w), lambda i: (i, 0))


def _vec(w):
    return pl.BlockSpec((1, w), lambda i: (0, 0))


def rms_fwd(cfg, x, g):
    S, D = x.shape

    def body(x_ref, g_ref, h_ref):
        xv = x_ref[...]
        h_ref[...] = (xv * _rstd(xv) * g_ref[...]).astype(BF16)

    return pl.pallas_call(
        body, name="rms_fwd", grid=(S // cfg.TM,),
        in_specs=[_row(cfg, D), _vec(D)], out_specs=_row(cfg, D),
        out_shape=jax.ShapeDtypeStruct((S, D), BF16), compiler_params=_cp(cfg, ("parallel",)),
    )(x, g)


def mid_fwd(cfg, x, mo, g_post, g_pre):
    S, D = x.shape

    def body(x_ref, mo_ref, gp_ref, gn_ref, x1_ref, h2_ref):
        mo_v = mo_ref[...]
        x1 = x_ref[...] + mo_v * _rstd(mo_v) * gp_ref[...]
        x1_ref[...] = x1
        h2_ref[...] = (x1 * _rstd(x1) * gn_ref[...]).astype(BF16)

    return pl.pallas_call(
        body, name="mid_fwd", grid=(S // cfg.TM,),
        in_specs=[_row(cfg, D), _row(cfg, D), _vec(D), _vec(D)],
        out_specs=[_row(cfg, D), _row(cfg, D)],
        out_shape=[jax.ShapeDtypeStruct((S, D), F32), jax.ShapeDtypeStruct((S, D), BF16)],
        compiler_params=_cp(cfg, ("parallel",)),
    )(x, mo, g_post, g_pre)


def final_fwd_bwd(cfg, x1, f, g_post, target):
    S, D = x1.shape

    def body(x1_ref, f_ref, g_ref, t_ref, dout_ref, df_ref, dg_ref, loss_ref):
        @pl.when(pl.program_id(0) == 0)
        def _():
            dg_ref[...] = jnp.zeros_like(dg_ref)
            loss_ref[...] = jnp.zeros_like(loss_ref)

        fv = f_ref[...]
        g = g_ref[...]
        out = x1_ref[...] + fv * _rstd(fv) * g
        err = out - t_ref[...]
        loss_ref[...] += 0.5 * jnp.sum(jnp.mean(err * err, axis=-1, keepdims=True), axis=0, keepdims=True)
        dout = err * (1.0 / D)
        dout_ref[...] = dout
        df, dgx = _rms_bwd(dout, fv, g)
        df_ref[...] = df.astype(BF16)
        dg_ref[...] += jnp.sum(dgx, axis=0, keepdims=True)

    return pl.pallas_call(
        body, name="final_fwd_bwd", grid=(S // cfg.TM,),
        in_specs=[_row(cfg, D), _row(cfg, D), _vec(D), _row(cfg, D)],
        out_specs=[_row(cfg, D), _row(cfg, D), _vec(D), _vec(LANE)],
        out_shape=[jax.ShapeDtypeStruct((S, D), F32), jax.ShapeDtypeStruct((S, D), BF16),
                   jax.ShapeDtypeStruct((1, D), F32), jax.ShapeDtypeStruct((1, LANE), F32)],
        compiler_params=_cp(cfg, ("arbitrary",)),
    )(x1, f, g_post, target)


def mid_bwd(cfg, dh2, x1, g_pre, dout, mo, g_post):
    S, D = x1.shape

    def body(dh_ref, x1_ref, gn_ref, do_ref, mo_ref, gp_ref, dx1_ref, dmo_ref, dgn_ref, dgp_ref):
        @pl.when(pl.program_id(0) == 0)
        def _():
            dgn_ref[...] = jnp.zeros_like(dgn_ref)
            dgp_ref[...] = jnp.zeros_like(dgp_ref)

        dx, dgx = _rms_bwd(dh_ref[...], x1_ref[...], gn_ref[...])
        dx1 = do_ref[...] + dx
        dx1_ref[...] = dx1
        dgn_ref[...] += jnp.sum(dgx, axis=0, keepdims=True)
        dmo, dgy = _rms_bwd(dx1, mo_ref[...], gp_ref[...])
        dmo_ref[...] = dmo.astype(BF16)
        dgp_ref[...] += jnp.sum(dgy, axis=0, keepdims=True)

    return pl.pallas_call(
        body, name="mid_bwd", grid=(S // cfg.TM,),
        in_specs=[_row(cfg, D), _row(cfg, D), _vec(D), _row(cfg, D), _row(cfg, D), _vec(D)],
        out_specs=[_row(cfg, D), _row(cfg, D), _vec(D), _vec(D)],
        out_shape=[jax.ShapeDtypeStruct((S, D), F32), jax.ShapeDtypeStruct((S, D), BF16),
                   jax.ShapeDtypeStruct((1, D), F32), jax.ShapeDtypeStruct((1, D), F32)],
        compiler_params=_cp(cfg, ("arbitrary",)),
    )(dh2, x1, g_pre, dout, mo, g_post)


def first_bwd(cfg, dh1, x, g_pre, dx1):
    S, D = x.shape

    def body(dh_ref, x_ref, g_ref, r_ref, dx_ref, dg_ref):
        @pl.when(pl.program_id(0) == 0)
        def _():
            dg_ref[...] = jnp.zeros_like(dg_ref)

        dx, dgx = _rms_bwd(dh_ref[...], x_ref[...], g_ref[...])
        dx_ref[...] = r_ref[...] + dx
        dg_ref[...] += jnp.sum(dgx, axis=0, keepdims=True)

    return pl.pallas_call(
        body, name="first_bwd", grid=(S // cfg.TM,),
        in_specs=[_row(cfg, D), _row(cfg, D), _vec(D), _row(cfg, D)],
        out_specs=[_row(cfg, D), _vec(D)],
        out_shape=[jax.ShapeDtypeStruct((S, D), F32), jax.ShapeDtypeStruct((1, D), F32)],
        compiler_params=_cp(cfg, ("arbitrary",)),
    )(dh1, x, g_pre, dx1)


def _mm(cfg, name, a, b, *, nt, a_spec, b_spec, o_spec, grid, out_shape, acc_shape, dep=None):
    nk = grid[-1]
    dot = _dot_nt if nt else _dot
    deps = [] if dep is None else [dep]

    def body(a_ref, b_ref, *rest):
        o_ref, acc_ref = rest[-2:]
        k = pl.program_id(len(grid) - 1)
        part = dot(a_ref[...], b_ref[...])
        if deps:
            part = part + rest[0][0:1, 0:1]
        if nk == 1:
            o_ref[...] = part.astype(o_ref.dtype)
            return

        @pl.when(k == 0)
        def _():
            acc_ref[...] = part

        @pl.when(k > 0)
        def _():
            acc_ref[...] += part

        @pl.when(k == nk - 1)
        def _():
            o_ref[...] = acc_ref[...].astype(o_ref.dtype)

    sem = ("parallel",) * (len(grid) - 1) + ("arbitrary",)
    dep_specs = [pl.BlockSpec((8, LANE), lambda *_: (0, 0))] * len(deps)
    return pl.pallas_call(
        body, name=name, grid=grid, in_specs=[a_spec, b_spec] + dep_specs, out_specs=o_spec, out_shape=out_shape,
        scratch_shapes=[pltpu.VMEM(acc_shape, F32)], compiler_params=_cp(cfg, sem),
    )(a, b, *deps)


def _mm_tn(cfg, name, a, b, *, a_spec, b_spec, o_spec, grid, out_shape):
    def body(a_ref, b_ref, o_ref):
        o_ref[...] = _dot_tn(a_ref[...], b_ref[...]).astype(o_ref.dtype)

    return pl.pallas_call(
        body, name=name, grid=grid, in_specs=[a_spec, b_spec], out_specs=o_spec, out_shape=out_shape,
        compiler_params=_cp(cfg, ("parallel",) * len(grid)),
    )(a, b)


def qkv_proj(cfg, h1, w_in, cos2, sin2):
    S, D = h1.shape
    tn = 2 * cfg.DH
    per = cfg.DSB // tn
    assert cfg.DSB == cfg.DDL
    nblk = 6 * per

    def body(a_ref, b_ref, c_ref, s_ref, o_ref):
        j = pl.program_id(0)
        acc = _dot(a_ref[...], b_ref[...])
        rope = jnp.logical_and(j >= 3 * per, j < 5 * per)

        @pl.when(rope)
        def _():
            for c in range(tn // cfg.DH):
                xh = acc[:, c * cfg.DH:(c + 1) * cfg.DH]
                o_ref[:, c * cfg.DH:(c + 1) * cfg.DH] = (
                    xh * c_ref[...] + pltpu.roll(xh, cfg.DH // 2, 1) * s_ref[...]).astype(BF16)

        @pl.when(jnp.logical_not(rope))
        def _():
            o_ref[...] = acc.astype(BF16)

    return pl.pallas_call(
        body, name="qkv_proj", grid=(nblk,),
        in_specs=[pl.BlockSpec((S, D), lambda j: (0, 0)), pl.BlockSpec((D, tn), lambda j: (0, j)),
                  pl.BlockSpec((S, cfg.DH), lambda j: (0, 0)), pl.BlockSpec((S, cfg.DH), lambda j: (0, 0))],
        out_specs=pl.BlockSpec((None, S, tn), lambda j: (j // per, 0, j % per)),
        out_shape=jax.ShapeDtypeStruct((6, S, cfg.DSB), BF16),
        compiler_params=_cp(cfg, ("parallel",)),
    )(h1, w_in, cos2, sin2)


def _sb_tile(cfg, q, k, kb, qb):
    QB = cfg.SBT
    z = _dot_nt(q, k) * (cfg.DH ** -0.5)
    t1 = jnp.log1p(jnp.exp(-jnp.abs(z)))
    lb = jnp.minimum(z, 0.0) - t1
    row = lax.broadcasted_iota(jnp.int32, (QB, QB), 0)
    col = lax.broadcasted_iota(jnp.int32, (QB, QB), 1)
    valid = jnp.logical_or(kb < qb, col < row)
    lk = jnp.where(valid, jnp.minimum(-z, 0.0) - t1, 0.0)
    return lb, lk, valid


def sb_fwd(cfg, qkv3):
    S, QB, DH = cfg.S, cfg.SBT, cfg.DH

    def body(q_ref, k_ref, v_ref, o_ref, t_ref):
        row = lax.broadcasted_iota(jnp.int32, (QB, QB), 0)
        col = lax.broadcasted_iota(jnp.int32, (QB, QB), 1)
        u_after = (row > col).astype(BF16)

        def q_loop(qb, _):
            rows = pl.ds(pl.multiple_of(qb * QB, QB), QB)
            q = q_ref[rows, :]

            def k_loop(i, carry):
                o_acc, c = carry
                kb = qb - i
                krows = pl.ds(pl.multiple_of(kb * QB, QB), QB)
                lb, lk, valid = _sb_tile(cfg, q, k_ref[krows, :], kb, qb)
                rem = _dot_split(lk, u_after) + c
                a = jnp.where(valid, jnp.exp(lb + rem), 0.0)
                o_acc = o_acc + _dot(a.astype(BF16), v_ref[krows, :])
                return o_acc, c + jnp.sum(lk, axis=1, keepdims=True)

            o_acc, c = lax.fori_loop(0, qb + 1, k_loop, (jnp.zeros((QB, DH), F32), jnp.zeros((QB, 1), F32)))
            o_ref[rows, :] = o_acc
            t_ref[rows, :] = jnp.broadcast_to(c, (QB, DH))
            return 0

        lax.fori_loop(0, S // QB, q_loop, 0)

    def spec(i):
        return pl.BlockSpec((None, S, DH), lambda h: (i, 0, h))

    return pl.pallas_call(
        body, name="sb_fwd", grid=(cfg.HSB,),
        in_specs=[spec(0), spec(1), spec(2)],
        out_specs=[pl.BlockSpec((S, DH), lambda h: (0, h))] * 2,
        out_shape=[jax.ShapeDtypeStruct((S, cfg.DSB), F32)] * 2,
        compiler_params=_cp(cfg, ("parallel",)),
    )(qkv3, qkv3, qkv3)


def sb_bwd(cfg, qkv3, do_sb, tsum):
    S, QB, DH = cfg.S, cfg.SBT, cfg.DH
    scale = DH ** -0.5

    def body(q_ref, k_ref, v_ref, do_ref, t_ref, d_ref, dk_acc, dv_acc):
        dk_acc[...] = jnp.zeros_like(dk_acc)
        dv_acc[...] = jnp.zeros_like(dv_acc)
        row = lax.broadcasted_iota(jnp.int32, (QB, QB), 0)
        col = lax.broadcasted_iota(jnp.int32, (QB, QB), 1)
        u_upto = (row <= col).astype(BF16)
        u_before = (row < col).astype(BF16)

        def q_loop(qb, _):
            rows = pl.ds(pl.multiple_of(qb * QB, QB), QB)
            q = q_ref[rows, :]
            do = do_ref[rows, :]
            total = t_ref[rows, 0:1]

            def k_loop(kb, carry):
                dq_acc, pc, gc = carry
                krows = pl.ds(pl.multiple_of(kb * QB, QB), QB)
                k = k_ref[krows, :]
                v = v_ref[krows, :]
                lb, lk, valid = _sb_tile(cfg, q, k, kb, qb)
                rem = total - pc - _dot_split(lk, u_upto)
                a = jnp.where(valid, jnp.exp(lb + rem), 0.0)
                g = a * _dot_nt(do, v)
                dv_acc[krows, :] += _dot_tn(a.astype(BF16), do)
                cum = gc + _dot(g.astype(BF16), u_before)
                sig = jnp.exp(lb)
                dz = (jnp.where(valid, g * (1.0 - sig) - cum * sig, 0.0) * scale).astype(BF16)
                dq_acc = dq_acc + _dot(dz, k)
                dk_acc[krows, :] += _dot_tn(dz, q)
                return dq_acc, pc + jnp.sum(lk, axis=1, keepdims=True), gc + jnp.sum(g, axis=1, keepdims=True)

            z1 = jnp.zeros((QB, 1), F32)
            dq_acc, _, _ = lax.fori_loop(0, qb + 1, k_loop, (jnp.zeros((QB, DH), F32), z1, z1))
            d_ref[0, rows, :] = dq_acc.astype(BF16)
            return 0

        lax.fori_loop(0, S // QB, q_loop, 0)
        d_ref[1, :, :] = dk_acc[...].astype(BF16)
        d_ref[2, :, :] = dv_acc[...].astype(BF16)

    def spec(i):
        return pl.BlockSpec((None, S, DH), lambda h: (i, 0, h))

    return pl.pallas_call(
        body, name="sb_bwd", grid=(cfg.HSB,),
        in_specs=[spec(0), spec(1), spec(2), pl.BlockSpec((S, DH), lambda h: (0, h)),
                  pl.BlockSpec((S, DH), lambda h: (0, h))],
        out_specs=pl.BlockSpec((3, S, DH), lambda h: (0, 0, h)),
        out_shape=jax.ShapeDtypeStruct((6, S, cfg.DSB), BF16),
        scratch_shapes=[pltpu.VMEM((S, DH), F32), pltpu.VMEM((S, DH), F32)],
        compiler_params=_cp(cfg, ("parallel",)),
    )(qkv3, qkv3, qkv3, do_sb, tsum)


def _band_mask(cfg, n, n_back):
    QB = cfg.QB
    qi = lax.broadcasted_iota(jnp.int32, (QB, 2 * QB), 0)
    kj = lax.broadcasted_iota(jnp.int32, (QB, 2 * QB), 1)
    dist = QB + qi - kj
    return (dist >= 0) & (dist <= n_back) & jnp.logical_or(n > 0, kj >= QB)


def _sub_rows(start, n, dil):
    if dil > 1:
        return pl.ds(start, n, stride=dil)
    return pl.ds(start if isinstance(start, int) else pl.multiple_of(start, 8), n)


def _stage_residues(cfg, dil, pairs):
    QB, L = cfg.QB, cfg.S // dil
    for src, dst in pairs:
        for r in range(dil):
            dst[pl.ds(r * (QB + L), QB), :] = jnp.zeros((QB, cfg.DH), BF16)
            dst[pl.ds(r * (QB + L) + QB, L), :] = src[_sub_rows(r, L, dil), :].astype(BF16)


def _staged_rows(cfg):
    return cfg.S + cfg.QB * max(d for _, d in cfg.branches)


def _lane_value(x):
    return jnp.max(x, axis=1, keepdims=True)


def dil_fwd(cfg, qkv3):
    S, QB, DH = cfg.S, cfg.QB, cfg.DH
    scale = DH ** -0.5
    nb = len(cfg.branches)
    mix_rows = min(256, S)

    def body(q_ref, k_ref, v_ref, o_ref, lt_ref, qf, kf, vf, kp, vp, *obl):
        obs, lbs = obl[:nb], obl[nb:]
        qf[...] = q_ref[...].astype(F32)
        kf[...] = k_ref[...].astype(F32)
        vf[...] = v_ref[...].astype(F32)
        for b, (window, dil) in enumerate(cfg.branches):
            L, n_back = S // dil, window // dil
            assert n_back <= QB and L % QB == 0
            _stage_residues(cfg, dil, [(kf, kp), (vf, vp)])
            for r in range(dil):
                for n in range(L // QB):
                    rows = _sub_rows(r + n * (QB * dil), QB, dil)
                    band = pl.ds(r * (QB + L) + n * QB, 2 * QB)
                    s = _dot_nt(qf[rows, :].astype(BF16), kp[band, :]) * scale
                    s = jnp.where(_band_mask(cfg, n, n_back), s, NEG_BIG)
                    m = jnp.max(s, axis=1, keepdims=True)
                    p = jnp.exp(s - m)
                    den = jnp.sum(p, axis=1, keepdims=True)
                    obs[b][rows, :] = _dot(p.astype(BF16), vp[band, :]) / den
                    lbs[b][rows, :] = jnp.broadcast_to(m + jnp.log(den), (QB, DH))

        def mix(i, _):
            rows = pl.ds(pl.multiple_of(i * mix_rows, mix_rows), mix_rows)
            ls = [r[rows, :] for r in lbs]
            m = functools.reduce(jnp.maximum, ls)
            es = [jnp.exp(l - m) for l in ls]
            tot = functools.reduce(jnp.add, es)
            o_ref[rows, :] = functools.reduce(jnp.add, [(e / tot) * r[rows, :] for e, r in zip(es, obs)])
            lt_ref[rows, :] = m + jnp.log(tot)
            return 0

        lax.fori_loop(0, S // mix_rows, mix, 0)

    def spec(i):
        return pl.BlockSpec((None, S, DH), lambda h: (i, 0, h))

    o_spec = pl.BlockSpec((S, DH), lambda h: (0, h))
    return pl.pallas_call(
        body, name="dil_fwd", grid=(cfg.HDL,),
        in_specs=[spec(3), spec(4), spec(5)], out_specs=[o_spec, o_spec],
        out_shape=[jax.ShapeDtypeStruct((S, cfg.DDL), F32)] * 2,
        scratch_shapes=[pltpu.VMEM((S, DH), F32)] * 3 + [pltpu.VMEM((_staged_rows(cfg), DH), BF16)] * 2
        + [pltpu.VMEM((S, DH), F32)] * (2 * nb),
        compiler_params=_cp(cfg, ("parallel",)),
    )(qkv3, qkv3, qkv3)


def dil_bwd(cfg, qkv3, do_dl, delta, lse_tot, cos2, sin2, d_sb3):
    S, QB, DH = cfg.S, cfg.QB, cfg.DH
    scale = DH ** -0.5
    out_rows = min(256, S)

    def body(q_ref, k_ref, v_ref, do_ref, dl_ref, lt_ref, c_ref, s_ref, base_ref, d_ref,
             qf, kf, vf, dof, kp, vp, dkp, dvp, dqn, dkn, dvn):
        qf[...] = q_ref[...].astype(F32)
        kf[...] = k_ref[...].astype(F32)
        vf[...] = v_ref[...].astype(F32)
        dof[...] = do_ref[...].astype(F32)
        for acc in (dqn, dkn, dvn):
            acc[...] = jnp.zeros_like(acc)
        for window, dil in cfg.branches:
            L, n_back = S // dil, window // dil
            reg = QB + L
            _stage_residues(cfg, dil, [(kf, kp), (vf, vp)])
            dkp[pl.ds(0, dil * reg), :] = jnp.zeros((dil * reg, DH), F32)
            dvp[pl.ds(0, dil * reg), :] = jnp.zeros((dil * reg, DH), F32)
            for r in range(dil):
                for n in range(L // QB):
                    rows = _sub_rows(r + n * (QB * dil), QB, dil)
                    band = pl.ds(r * reg + n * QB, 2 * QB)
                    q = qf[rows, :].astype(BF16)
                    do = dof[rows, :].astype(BF16)
                    kb = kp[band, :]
                    s = _dot_nt(q, kb) * scale
                    s = jnp.where(_band_mask(cfg, n, n_back), s, NEG_BIG)
                    p = jnp.exp(s - _lane_value(lt_ref[rows, :]))
                    ds = (p * (_dot_nt(do, vp[band, :]) - _lane_value(dl_ref[rows, :])) * scale).astype(BF16)
                    dqn[rows, :] += _dot(ds, kb)
                    dkp[band, :] += _dot_tn(ds, q)
                    dvp[band, :] += _dot_tn(p.astype(BF16), do)
            for r in range(dil):
                sub = _sub_rows(r, L, dil)
                dkn[sub, :] += dkp[pl.ds(r * reg + QB, L), :]
                dvn[sub, :] += dvp[pl.ds(r * reg + QB, L), :]

        def finish(i, _):
            rows = pl.ds(pl.multiple_of(i * out_rows, out_rows), out_rows)
            c, sn = c_ref[rows, :], s_ref[rows, :]
            for j, acc in enumerate((dqn, dkn)):
                d = acc[rows, :]
                d_ref[j, rows, :] = (d * c + pltpu.roll(d * sn, DH // 2, 1)).astype(BF16)
            d_ref[2, rows, :] = dvn[rows, :].astype(BF16)
            return 0

        lax.fori_loop(0, S // out_rows, finish, 0)

    def spec(i):
        return pl.BlockSpec((None, S, DH), lambda h: (i, 0, h))

    hd = pl.BlockSpec((S, DH), lambda h: (0, h))
    tab = pl.BlockSpec((S, DH), lambda h: (0, 0))
    ns = _staged_rows(cfg)
    return pl.pallas_call(
        body, name="dil_bwd", grid=(cfg.HDL,),
        in_specs=[spec(3), spec(4), spec(5), hd, hd, hd, tab, tab, ANY],
        out_specs=pl.BlockSpec((3, S, DH), lambda h: (1, 0, h)),
        out_shape=jax.ShapeDtypeStruct((6, S, cfg.DDL), BF16),
        input_output_aliases={8: 0},
        scratch_shapes=[pltpu.VMEM((S, DH), F32)] * 4 + [pltpu.VMEM((ns, DH), BF16)] * 2
        + [pltpu.VMEM((ns, DH), F32)] * 2 + [pltpu.VMEM((S, DH), F32)] * 3,
        compiler_params=_cp(cfg, ("parallel",)),
    )(qkv3, qkv3, qkv3, do_dl, delta, lse_tot, cos2, sin2, d_sb3)


def combine_fwd(cfg, o_sb, o_dl, g_sb, g_dl):
    S, DH = cfg.S, cfg.DH

    def head_norm(o, g):
        return o * lax.rsqrt(jnp.mean(o * o, axis=-1, keepdims=True) + RMS_EPS) * g

    def body(osb_ref, odl_ref, gsb_ref, gdl_ref, mix_ref):
        for h in range(cfg.HSB):
            c = slice(h * DH, (h + 1) * DH)
            mix_ref[:, c] = head_norm(osb_ref[:, c], gsb_ref[:, c]).astype(BF16)
        for h in range(cfg.HDL):
            c = slice(h * DH, (h + 1) * DH)
            mix_ref[:, cfg.DSB + h * DH:cfg.DSB + (h + 1) * DH] = head_norm(odl_ref[:, c], gdl_ref[:, c]).astype(BF16)

    return pl.pallas_call(
        body, name="combine_fwd", grid=(S // cfg.TM,),
        in_specs=[_row(cfg, cfg.DSB), _row(cfg, cfg.DDL), _vec(cfg.DSB), _vec(cfg.DDL)],
        out_specs=_row(cfg, cfg.DMIX), out_shape=jax.ShapeDtypeStruct((S, cfg.DMIX), BF16),
        compiler_params=_cp(cfg, ("parallel",)),
    )(o_sb, o_dl, g_sb, g_dl)


def combine_bwd(cfg, dmix, o_sb, o_dl, g_sb, g_dl):
    S, DH = cfg.S, cfg.DH

    def body(dm_ref, osb_ref, odl_ref, gsb_ref, gdl_ref, dsb_ref, ddl_ref, dl_ref, dgsb_ref, dgdl_ref):
        @pl.when(pl.program_id(0) == 0)
        def _():
            dgsb_ref[...] = jnp.zeros_like(dgsb_ref)
            dgdl_ref[...] = jnp.zeros_like(dgdl_ref)

        for h in range(cfg.HSB):
            c = slice(h * DH, (h + 1) * DH)
            dx, dgx = _rms_bwd(dm_ref[:, c], osb_ref[:, c], gsb_ref[:, c])
            dsb_ref[:, c] = dx.astype(BF16)
            dgsb_ref[:, c] += jnp.sum(dgx, axis=0, keepdims=True)
        for h in range(cfg.HDL):
            c = slice(h * DH, (h + 1) * DH)
            o = odl_ref[:, c]
            dx, dgx = _rms_bwd(dm_ref[:, cfg.DSB + h * DH:cfg.DSB + (h + 1) * DH], o, gdl_ref[:, c])
            ddl_ref[:, c] = dx.astype(BF16)
            dl_ref[:, c] = jnp.broadcast_to(jnp.sum(dx * o, axis=-1, keepdims=True), dx.shape)
            dgdl_ref[:, c] += jnp.sum(dgx, axis=0, keepdims=True)

    return pl.pallas_call(
        body, name="combine_bwd", grid=(S // cfg.TM,),
        in_specs=[_row(cfg, cfg.DMIX), _row(cfg, cfg.DSB), _row(cfg, cfg.DDL), _vec(cfg.DSB), _vec(cfg.DDL)],
        out_specs=[_row(cfg, cfg.DSB), _row(cfg, cfg.DDL), _row(cfg, cfg.DDL), _vec(cfg.DSB), _vec(cfg.DDL)],
        out_shape=[jax.ShapeDtypeStruct((S, cfg.DSB), BF16), jax.ShapeDtypeStruct((S, cfg.DDL), BF16),
                   jax.ShapeDtypeStruct((S, cfg.DDL), F32), jax.ShapeDtypeStruct((1, cfg.DSB), F32),
                   jax.ShapeDtypeStruct((1, cfg.DDL), F32)],
        compiler_params=_cp(cfg, ("arbitrary",)),
    )(dmix, o_sb, o_dl, g_sb, g_dl)


SUB = 8


def _shift_down(u, prev, j):
    rolled = pltpu.roll(u, j, 0)
    row = lax.broadcasted_iota(jnp.int32, (SUB, u.shape[1]), 0)
    head = jnp.where(row >= j, rolled[:SUB], pltpu.roll(prev, j, 0))
    return jnp.concatenate([head, rolled[SUB:]], axis=0)


def _shift_up(u, nxt, j):
    n = u.shape[0]
    rolled = pltpu.roll(u, n - j, 0)
    row = lax.broadcasted_iota(jnp.int32, (SUB, u.shape[1]), 0)
    tail = jnp.where(row < SUB - j, rolled[n - SUB:], pltpu.roll(nxt, SUB - j, 0))
    return jnp.concatenate([rolled[:n - SUB], tail], axis=0)


def _conv(u, s1, s2, cw, cb):
    return u * cw[2:3, :] + s1 * cw[1:2, :] + s2 * cw[0:1, :] + cb


def _chunk_rows(cfg):
    ch = min(cfg.FCH, cfg.S)
    return ch, cfg.S // ch


def ffn_fwd(cfg, h2, w_up, conv_w, conv_b):
    S, D = h2.shape
    tn, nt = cfg.TNF, cfg.FFP // cfg.TNF
    ch, nch = _chunk_rows(cfg)

    def body(h_ref, wg_ref, wv_ref, cwg_ref, cwv_ref, cbg_ref, cbv_ref, u_ref, y_ref):
        prev = [jnp.zeros((SUB, tn), F32)] * 2
        for ci in range(nch):
            rows = pl.ds(ci * ch, ch)
            h = h_ref[rows, :]
            us = [_dot_nt(h, wg_ref[...]), _dot_nt(h, wv_ref[...])]
            cs = []
            for i, (cw_ref, cb_ref) in enumerate(((cwg_ref, cbg_ref), (cwv_ref, cbv_ref))):
                u_ref[i, rows, :] = us[i]
                cs.append(_conv(us[i], _shift_down(us[i], prev[i], 1), _shift_down(us[i], prev[i], 2),
                                cw_ref[...], cb_ref[...]))
            y_ref[rows, :] = (_gelu(cs[0])[0] * cs[1]).astype(BF16)
            prev = [u[ch - SUB:] for u in us]

    return pl.pallas_call(
        body, name="ffn_fwd", grid=(nt,),
        in_specs=[pl.BlockSpec((S, D), lambda n: (0, 0)),
                  pl.BlockSpec((tn, D), lambda n: (n, 0)), pl.BlockSpec((tn, D), lambda n: (n + nt, 0)),
                  pl.BlockSpec((3, tn), lambda n: (0, n)), pl.BlockSpec((3, tn), lambda n: (0, n + nt)),
                  pl.BlockSpec((1, tn), lambda n: (0, n)), pl.BlockSpec((1, tn), lambda n: (0, n + nt))],
        out_specs=[pl.BlockSpec((2, S, tn), lambda n: (0, 0, n)), pl.BlockSpec((S, tn), lambda n: (0, n))],
        out_shape=[jax.ShapeDtypeStruct((2, S, cfg.FFP), F32), jax.ShapeDtypeStruct((S, cfg.FFP), BF16)],
        compiler_params=_cp(cfg, ("parallel",)),
    )(h2, w_up, w_up, conv_w, conv_w, conv_b, conv_b)


def ffn_bwd(cfg, df, h2, w_down, u, conv_w, conv_b):
    S, D = df.shape
    tn, nt = cfg.TNF, cfg.FFP // cfg.TNF

    ch, nch = _chunk_rows(cfg)

    def body(df_ref, h_ref, wd_ref, u_ref, cwg_ref, cwv_ref, cbg_ref, cbv_ref,
             du_ref, dwd_ref, dwu_ref, dcw_ref, dcb_ref):
        cws = (cwg_ref[...], cwv_ref[...])
        cbs = (cbg_ref[...], cbv_ref[...])
        zero = jnp.zeros((SUB, tn), F32)
        nxt = [zero, zero]
        dws = [[jnp.zeros((1, tn), F32)] * 4 for _ in range(2)]
        dwd = jnp.zeros((tn, D), F32)
        dwu = [jnp.zeros((tn, D), F32)] * 2
        for ci in reversed(range(nch)):
            rows = pl.ds(ci * ch, ch)
            dfv = df_ref[rows, :]
            hv = h_ref[rows, :]
            dy = _dot_nt(dfv, wd_ref[...])
            us, s1, s2, cs = [], [], [], []
            for i in range(2):
                u = u_ref[i, rows, :]
                prev = u_ref[i, pl.ds(ci * ch - SUB, SUB), :] if ci else zero
                us.append(u)
                s1.append(_shift_down(u, prev, 1))
                s2.append(_shift_down(u, prev, 2))
                cs.append(_conv(u, s1[i], s2[i], cws[i], cbs[i]))
            gl, t = _gelu(cs[0])
            dwd = dwd + _dot_tn((gl * cs[1]).astype(BF16), dfv)
            dcs = (dy * cs[1] * _gelu_grad(cs[0], t), dy * gl)
            for i, dc in enumerate(dcs):
                du = dc * cws[i][2:3, :] + _shift_up(dc, nxt[i], 1) * cws[i][1:2, :] + _shift_up(dc, nxt[i], 2) * cws[i][0:1, :]
                du = du.astype(BF16)
                du_ref[i, rows, :] = du
                dwu[i] = dwu[i] + _dot_tn(du, hv)
                for j, tap in enumerate((s2[i], s1[i], us[i])):
                    dws[i][j] = dws[i][j] + jnp.sum(dc * tap, axis=0, keepdims=True)
                dws[i][3] = dws[i][3] + jnp.sum(dc, axis=0, keepdims=True)
            nxt = [dc[:SUB] for dc in dcs]
        dwd_ref[...] = dwd.astype(BF16)
        for i in range(2):
            dwu_ref[i] = dwu[i].astype(BF16)
            for j in range(3):
                dcw_ref[i, j:j + 1, :] = dws[i][j]
            dcb_ref[i] = dws[i][3]

    whole = pl.BlockSpec((S, D), lambda n: (0, 0), pipeline_mode=pl.Buffered(1))
    du, dwd, dwu, dcw, dcb = pl.pallas_call(
        body, name="ffn_bwd", grid=(nt,),
        in_specs=[whole, whole, pl.BlockSpec((tn, D), lambda n: (n, 0)),
                  pl.BlockSpec((2, S, tn), lambda n: (0, 0, n)),
                  pl.BlockSpec((3, tn), lambda n: (0, n)), pl.BlockSpec((3, tn), lambda n: (0, n + nt)),
                  pl.BlockSpec((1, tn), lambda n: (0, n)), pl.BlockSpec((1, tn), lambda n: (0, n + nt))],
        out_specs=[pl.BlockSpec((2, S, tn), lambda n: (0, 0, n)), pl.BlockSpec((tn, D), lambda n: (n, 0)),
                   pl.BlockSpec((2, tn, D), lambda n: (0, n, 0)),
                   pl.BlockSpec((2, 3, tn), lambda n: (0, 0, n)), pl.BlockSpec((2, 1, tn), lambda n: (0, 0, n))],
        out_shape=[jax.ShapeDtypeStruct((2, S, cfg.FFP), BF16), jax.ShapeDtypeStruct((cfg.FFP, D), BF16),
                   jax.ShapeDtypeStruct((2, cfg.FFP, D), BF16),
                   jax.ShapeDtypeStruct((2, 3, cfg.FFP), F32), jax.ShapeDtypeStruct((2, 1, cfg.FFP), F32)],
        compiler_params=_cp(cfg, ("parallel",)),
    )(df, h2, w_down, u, conv_w, conv_w, conv_b, conv_b)
    return du, dwd, dwu.reshape(cfg.FF2P, D), dcw, dcb


def rope_tables(cfg):
    inv_freq = ROPE_THETA ** (-jnp.arange(0, cfg.DH, 2, dtype=F32) / cfg.DH)
    ang = jnp.arange(cfg.S, dtype=F32)[:, None] * inv_freq[None, :]
    cos, sin = jnp.cos(ang), jnp.sin(ang)
    return jnp.concatenate([cos, cos], axis=1), jnp.concatenate([-sin, sin], axis=1)


class LocalWeights:
    def __init__(self, w_in, w_out, w_up, conv_w, w_down):
        self.w = (w_in, w_out, w_up, conv_w, w_down)
        self.grads = {}

    def weights_first(self):
        return self.w[0], self.w[3]

    def start_rest(self):
        return None

    def weights_rest(self, after):
        return self.w[1], self.w[2], self.w[4]

    def reduce_start(self, grads):
        self.grads.update(grads)
        return None

    def reduce_wait(self, names, after):
        pass


def _after(a, token):
    return a if token is None else a + token[0, 0].astype(a.dtype)


def local_step(cfg, comm, x, target, g1, g2, g3, g4, g_sb, g_dl, conv_b):
    S, D = cfg.S, cfg.D
    cos2, sin2 = rope_tables(cfg)
    full = lambda r, c: pl.BlockSpec((r, c), lambda j, k: (0, 0))

    w_in, conv_w = comm.weights_first()
    h1 = rms_fwd(cfg, x, g1)
    qkv3 = qkv_proj(cfg, h1, w_in, _after(cos2, comm.start_rest()), sin2)
    o_sb, tsum = sb_fwd(cfg, qkv3)
    o_dl, lse_tot = dil_fwd(cfg, qkv3)
    mixed = combine_fwd(cfg, o_sb, o_dl, g_sb, g_dl)
    w_out, w_up, w_down = comm.weights_rest(after=mixed)
    tn = cfg.TN
    mo = _mm(cfg, "mix_out", mixed, w_out, nt=False, grid=(D // tn, 1),
             a_spec=full(S, cfg.DMIX), b_spec=pl.BlockSpec((cfg.DMIX, tn), lambda j, k: (0, j)),
             o_spec=pl.BlockSpec((S, tn), lambda j, k: (0, j)),
             out_shape=jax.ShapeDtypeStruct((S, D), F32), acc_shape=(8, LANE))
    x1, h2 = mid_fwd(cfg, x, mo, g2, g3)
    u, y = ffn_fwd(cfg, h2, w_up, conv_w, conv_b)
    tk = cfg.FFP // 2
    f = _mm(cfg, "ffn_down", y, w_down, nt=False, grid=(D // tn, cfg.FFP // tk),
            a_spec=pl.BlockSpec((S, tk), lambda j, k: (0, k)), b_spec=pl.BlockSpec((tk, tn), lambda j, k: (k, j)),
            o_spec=pl.BlockSpec((S, tn), lambda j, k: (0, j)),
            out_shape=jax.ShapeDtypeStruct((S, D), F32), acc_shape=(S, tn))
    dout, df, dg4, loss = final_fwd_bwd(cfg, x1, f, g4, target)

    du, dw_down, dw_up, dconv_w, dconv_b = ffn_bwd(cfg, df, h2, w_down, u, conv_w, conv_b)
    kt = cfg.FFP // tk
    dh2 = _mm(cfg, "d_h2", du, w_up, nt=False, grid=(D // tn, 2 * kt),
              a_spec=pl.BlockSpec((None, S, tk), lambda j, k: (k // kt, 0, k % kt)),
              b_spec=pl.BlockSpec((tk, tn), lambda j, k: (k, j)),
              o_spec=pl.BlockSpec((S, tn), lambda j, k: (0, j)),
              out_shape=jax.ShapeDtypeStruct((S, D), F32), acc_shape=(S, tn))
    dx1, dmo, dg3, dg2 = mid_bwd(cfg, dh2, x1, g3, dout, mo, g2)

    dmix = _mm(cfg, "d_mixed", dmo, w_out, nt=True, grid=(cfg.DMIX // tn, 1),
               a_spec=full(S, D), b_spec=pl.BlockSpec((tn, D), lambda j, k: (j, 0)),
               o_spec=pl.BlockSpec((S, tn), lambda j, k: (0, j)),
               out_shape=jax.ShapeDtypeStruct((S, cfg.DMIX), F32), acc_shape=(8, LANE))
    dw_out = _mm_tn(cfg, "d_w_out", mixed, dmo, grid=(D // tn,),
                    a_spec=pl.BlockSpec((S, cfg.DMIX), lambda j: (0, 0)),
                    b_spec=pl.BlockSpec((S, tn), lambda j: (0, j)),
                    o_spec=pl.BlockSpec((cfg.DMIX, tn), lambda j: (0, j)),
                    out_shape=jax.ShapeDtypeStruct((cfg.DMIX, D), BF16))
    token = comm.reduce_start(dict(w_out=dw_out, w_up=dw_up, w_down=dw_down))
    do_sb, do_dl, delta, dg_sb, dg_dl = combine_bwd(cfg, dmix, o_sb, o_dl, _after(g_sb, token), g_dl)
    d_sb3 = sb_bwd(cfg, qkv3, do_sb, tsum)
    dqkv3 = dil_bwd(cfg, qkv3, do_dl, delta, lse_tot, cos2, sin2, d_sb3)
    comm.reduce_wait(("w_out", "w_up", "w_down"), after=dqkv3)
    tkq = min(tn, cfg.DSB)
    kq = cfg.DSB // tkq
    dw_in = _mm_tn(cfg, "d_w_in", h1, dqkv3, grid=(6 * kq,),
                   a_spec=pl.BlockSpec((S, D), lambda j: (0, 0)),
                   b_spec=pl.BlockSpec((None, S, tkq), lambda j: (j // kq, 0, j % kq)),
                   o_spec=pl.BlockSpec((D, tkq), lambda j: (0, j)),
                   out_shape=jax.ShapeDtypeStruct((D, 6 * cfg.DSB), BF16))
    token = comm.reduce_start(dict(w_in=dw_in))
    dh1 = _mm(cfg, "d_h1", dqkv3, w_in, nt=True, grid=(D // tn, 6),
              a_spec=pl.BlockSpec((None, S, cfg.DSB), lambda j, k: (k, 0, 0)),
              b_spec=pl.BlockSpec((tn, cfg.DSB), lambda j, k: (j, k)),
              o_spec=pl.BlockSpec((S, tn), lambda j, k: (0, j)),
              out_shape=jax.ShapeDtypeStruct((S, D), F32), acc_shape=(S, tn), dep=token)
    grad_x, dg1 = first_bwd(cfg, dh1, x, g1, dx1)
    comm.reduce_wait(("w_in",), after=grad_x)
    small = dict(loss=loss, g1=dg1, g2=dg2, g3=dg3, g4=dg4, g_sb=dg_sb, g_dl=dg_dl,
                 conv_b=dconv_b.reshape(1, cfg.FF2P), conv_w=dconv_w.transpose(1, 0, 2).reshape(3, cfg.FF2P))
    return grad_x, small


ANY = pl.BlockSpec(memory_space=pl.ANY)


def _me():
    return lax.axis_index("x"), lax.axis_index("y"), lax.axis_index("c")


def _other_chips(x, y):
    return [(1 - x, y), (x, 1 - y), (1 - x, 1 - y)]


def pad_conv_w(cfg, conv_w):
    r, c = conv_w.shape

    def body(w_ref, o_ref):
        o_ref[:, :c] = w_ref[...]
        o_ref[:, c:] = jnp.zeros((r, cfg.FSHP - c), F32)

    return pl.pallas_call(body, name="pad_conv_w", out_shape=jax.ShapeDtypeStruct((r, cfg.FSHP), F32))(conv_w)


def _tile2(r, c):
    return (256, c) if r % 256 == 0 else (r, 512 if c % 512 == 0 else c)


def cast_into(cfg, name, w, pos):
    r, c = w.shape
    _, nr, _, nc = _slab(cfg, name, 0)
    tm, tc = _tile2(r, c)
    wr = nr if tm == r else tm
    assert nc == c and (nr == r or tm == r)

    def body(pos_ref, w_ref, full_ref, scr, sem):
        scr[pl.ds(0, tm), :] = w_ref[...].astype(BF16)
        if wr > tm:
            scr[pl.ds(tm, wr - tm), :] = jnp.zeros((wr - tm, tc), BF16)
        r0, _, c0, _ = _slab(cfg, name, pos_ref[0])
        rows = pl.ds(pl.multiple_of(r0 + pl.program_id(0) * tm, 16), wr)
        cols = pl.ds(pl.multiple_of(c0 + pl.program_id(1) * tc, LANE), tc)
        cp = pltpu.make_async_copy(scr, full_ref.at[rows, cols], sem)
        cp.start()
        cp.wait()

    return pl.pallas_call(
        body, name=f"cast_{name}",
        grid_spec=pltpu.PrefetchScalarGridSpec(
            num_scalar_prefetch=1, grid=(r // tm, c // tc),
            in_specs=[pl.BlockSpec((tm, tc), lambda i, j, p: (i, j))], out_specs=ANY,
            scratch_shapes=[pltpu.VMEM((wr, tc), BF16), pltpu.SemaphoreType.DMA]),
        out_shape=jax.ShapeDtypeStruct(_full_shape(cfg, name), BF16),
        compiler_params=_cp(cfg, ("arbitrary", "arbitrary")),
    )(pos, w)


HBM = pl.BlockSpec(memory_space=pltpu.HBM)
SEM = pl.BlockSpec(memory_space=pltpu.SEMAPHORE)
TOKEN = pl.BlockSpec(memory_space=pltpu.VMEM)
EFFECT = pltpu.SideEffectType.DATAFLOW_SIDE_EFFECTING


def _slab(cfg, name, k):
    D = cfg.D
    if name == "w_in":
        cin = 6 * cfg.DSB // N_CHIPS
        return 0, D, k * cin, cin
    if name == "w_out":
        rout = cfg.DMIX // N_CHIPS
        return k * rout, rout, 0, D
    if name == "w_up":
        return k * cfg.FSHP, cfg.FSHP, 0, D
    rdn = cfg.FSH // 2
    return (k // 2) * cfg.FSHP + (k % 2) * rdn, rdn, 0, D


def _full_shape(cfg, name):
    return dict(w_in=(cfg.D, 6 * cfg.DSB), w_out=(cfg.DMIX, cfg.D), w_up=(cfg.FF2P, cfg.D), w_down=(cfg.FFP, cfg.D))[name]


def _half(cfg, name, ref, k, h):
    r0, nr, c0, nc = _slab(cfg, name, k)
    return ref.at[pl.ds(r0 + h * (nr // 2), nr // 2), pl.ds(c0, nc)]


def _rows_half(ref, h):
    nr = ref.shape[0] // 2
    return ref.at[pl.ds(h * nr, nr), :]


def _remote(src, dst, send_sem, recv_sem, dev):
    return pltpu.make_async_remote_copy(src_ref=src, dst_ref=dst, send_sem=send_sem, recv_sem=recv_sem,
                                        device_id=dev, device_id_type=MESH)


def gather_first(cfg, g_in, sh_cw):
    def body(in_ref, cw_ref, g_in, g_cw, token, ssem, rsem, fssem, frsem, lsem):
        x, y, c = _me()
        me, sib = 2 * x + y, (x, y, 1 - c)
        cw_slot = lambda k: g_cw.at[:, pl.ds(k * cfg.FSHP, cfg.FSHP)]
        local = [pltpu.make_async_copy(cw_ref, cw_slot(me), lsem.at[0])]
        sends = []
        for j, (px, py) in enumerate(_other_chips(x, y)):
            mine = _half(cfg, "w_in", g_in, me, c)
            sends.append(_remote(mine, mine, ssem.at[j], rsem.at[j], (px, py, c)))
            sends.append(_remote(cw_ref, cw_slot(me), ssem.at[3 + j], rsem.at[3 + j], (px, py, c)))
        for cp in local + sends:
            cp.start()
        for j, (px, py) in enumerate(_other_chips(x, y)):
            k = 2 * px + py
            landed = _half(cfg, "w_in", g_in, k, c)
            _remote(landed, landed, ssem.at[j], rsem.at[j], (px, py, c)).wait_recv()
            fwd = _remote(landed, landed, fssem.at[j], frsem.at[j], sib)
            fwd.start()
            sends.append(fwd)
        for j, (px, py) in enumerate(_other_chips(x, y)):
            k = 2 * px + py
            passed = _half(cfg, "w_in", g_in, k, 1 - c)
            _remote(passed, passed, fssem.at[j], frsem.at[j], sib).wait_recv()
            _remote(cw_ref, cw_slot(k), ssem.at[3 + j], rsem.at[3 + j], (px, py, c)).wait_recv()
        for cp in sends:
            cp.wait_send()
        for cp in local:
            cp.wait()
        token[...] = jnp.zeros_like(token)

    return pl.pallas_call(
        body, name="gather_first", in_specs=[ANY, ANY], out_specs=[ANY, ANY, TOKEN],
        out_shape=[jax.ShapeDtypeStruct(_full_shape(cfg, "w_in"), BF16), jax.ShapeDtypeStruct((3, cfg.FF2P), F32),
                   jax.ShapeDtypeStruct((8, LANE), F32)],
        input_output_aliases={0: 0},
        scratch_shapes=[pltpu.SemaphoreType.DMA((6,)), pltpu.SemaphoreType.DMA((6,)), pltpu.SemaphoreType.DMA((3,)),
                        pltpu.SemaphoreType.DMA((3,)), pltpu.SemaphoreType.DMA((1,))],
    )(g_in, sh_cw)


REST = ("w_out", "w_up", "w_down")


def _hbm(a):
    return pltpu.with_memory_space_constraint(a, pltpu.HBM)


def gather_start(cfg, fulls, after):
    n = len(REST)

    def body(*refs):
        lands = refs[:n]
        ssem, rsem = refs[n + 1], refs[n + 2]
        token = refs[-1]
        x, y, c = _me()
        me = 2 * x + y
        for i, name in enumerate(REST):
            mine = _half(cfg, name, lands[i], me, c)
            for j, (px, py) in enumerate(_other_chips(x, y)):
                _remote(mine, mine, ssem.at[3 * i + j], rsem.at[3 * i + j], (px, py, c)).start()
        token[...] = jnp.zeros_like(token)

    ops = [_hbm(a) for a in fulls]
    outs = pl.pallas_call(
        body, name="gather_start",
        in_specs=[HBM] * n + [ANY],
        out_specs=[SEM, SEM] + [HBM] * n + [TOKEN],
        out_shape=[pltpu.SemaphoreType.DMA((3 * n,)), pltpu.SemaphoreType.DMA((3 * n,))]
        + [pltpu.HBM(a.shape, a.dtype) for a in ops] + [jax.ShapeDtypeStruct((8, LANE), F32)],
        input_output_aliases={i: 2 + i for i in range(n)},
        compiler_params=pltpu.CompilerParams(has_side_effects=EFFECT),
    )(*ops, after)
    return outs[0], outs[1], outs[2:2 + n], outs[-1]


def gather_wait(cfg, ssem, rsem, lands, after):
    n = len(REST)

    def body(*refs):
        lands_ = refs[:n]
        ssem_, rsem_ = refs[n], refs[n + 1]
        x, y, c = _me()
        me = 2 * x + y
        for i, name in enumerate(REST):
            for j, (px, py) in enumerate(_other_chips(x, y)):
                cp = _remote(_half(cfg, name, lands_[i], me, c), _half(cfg, name, lands_[i], 2 * px + py, c),
                             ssem_.at[3 * i + j], rsem_.at[3 * i + j], (px, py, c))
                cp.wait_send()
                cp.wait_recv()

    return pl.pallas_call(
        body, name="gather_wait",
        in_specs=[HBM] * n + [SEM, SEM, ANY], out_specs=[HBM] * n,
        out_shape=[pltpu.HBM(a.shape, a.dtype) for a in lands],
        input_output_aliases={i: i for i in range(n)},
        compiler_params=pltpu.CompilerParams(has_side_effects=EFFECT),
    )(*lands, ssem, rsem, after)


def gather_finish(cfg, lands):
    n = len(REST)
    rdn = cfg.FSH // 2
    zpad = jnp.zeros((cfg.FSHP - cfg.FSH, cfg.D), BF16)

    def body(*refs):
        z_ref, outs = refs[0], refs[n + 1:2 * n + 1]
        ssem, rsem, lsem = refs[2 * n + 1:]
        x, y, c = _me()
        sib = (x, y, 1 - c)
        local = [pltpu.make_async_copy(z_ref, outs[2].at[pl.ds(h * cfg.FSHP + 2 * rdn, cfg.FSHP - cfg.FSH), :],
                                       lsem.at[h]) for h in range(2)]
        fwds = []
        for i, name in enumerate(REST):
            for j, (px, py) in enumerate(_other_chips(x, y)):
                landed = _half(cfg, name, outs[i], 2 * px + py, c)
                fwds.append(_remote(landed, landed, ssem.at[3 * i + j], rsem.at[3 * i + j], sib))
        for cp in local + fwds:
            cp.start()
        for i, name in enumerate(REST):
            for j, (px, py) in enumerate(_other_chips(x, y)):
                passed = _half(cfg, name, outs[i], 2 * px + py, 1 - c)
                _remote(passed, passed, ssem.at[3 * i + j], rsem.at[3 * i + j], sib).wait_recv()
        for cp in fwds:
            cp.wait_send()
        for cp in local:
            cp.wait()

    return pl.pallas_call(
        body, name="gather_finish", in_specs=[ANY] * (n + 1), out_specs=[ANY] * n,
        out_shape=[jax.ShapeDtypeStruct(a.shape, a.dtype) for a in lands],
        input_output_aliases={1 + i: i for i in range(n)},
        scratch_shapes=[pltpu.SemaphoreType.DMA((3 * n,)), pltpu.SemaphoreType.DMA((3 * n,)),
                        pltpu.SemaphoreType.DMA((2,))],
    )(zpad, *lands)


def pair_send(cfg, grads):
    names = list(grads)
    n = len(names)

    def half_shape(name):
        _, nr, _, nc = _slab(cfg, name, 0)
        return (N_CHIPS, nr // 2, nc)

    def body(*refs):
        srcs, theirs = refs[:n], refs[n:2 * n]
        ssem, rsem = refs[2 * n:]
        x, y, c = _me()
        cps = []
        for i, name in enumerate(names):
            for k in range(N_CHIPS):
                cps.append(_remote(_half(cfg, name, srcs[i], k, 1 - c), theirs[i].at[k],
                                   ssem.at[N_CHIPS * i + k], rsem.at[N_CHIPS * i + k], (x, y, 1 - c)))
        for cp in cps:
            cp.start()
        for cp in cps:
            cp.wait()

    outs = pl.pallas_call(
        body, name="pair_send_" + "_".join(names), in_specs=[ANY] * n, out_specs=[ANY] * n,
        out_shape=[jax.ShapeDtypeStruct(half_shape(name), BF16) for name in names],
        scratch_shapes=[pltpu.SemaphoreType.DMA((N_CHIPS * n,))] * 2,
    )(*[grads[k] for k in names])
    return dict(zip(names, outs))


def pair_sum(cfg, name, grad, theirs, pos):
    _, r, c = theirs.shape
    tm, tc = _tile2(r, c)

    ni, nj = r // tm, c // tc
    total = N_CHIPS * ni * nj

    def body(pos_ref, g_ref, t_ref, o_ref, scr, sem):
        step = (pl.program_id(0) * ni + pl.program_id(1)) * nj + pl.program_id(2)

        def fetch(flat, slot):
            k, rem = flat // (ni * nj), flat % (ni * nj)
            r0, nr, c0, _ = _slab(cfg, name, k)
            rows = pl.ds(pl.multiple_of(r0 + pos_ref[1] * (nr // 2) + (rem // nj) * tm, 16), tm)
            cols = pl.ds(pl.multiple_of(c0 + (rem % nj) * tc, LANE), tc)
            return pltpu.make_async_copy(g_ref.at[rows, cols], scr.at[slot], sem.at[slot])

        @pl.when(step == 0)
        def _():
            fetch(0, 0).start()

        @pl.when(step + 1 < total)
        def _():
            fetch(step + 1, (step + 1) % 2).start()

        fetch(step, step % 2).wait()
        o_ref[...] = (scr[step % 2].astype(F32) + t_ref[...].astype(F32)).astype(BF16)

    blk = pl.BlockSpec((None, tm, tc), lambda k, i, j, p: (k, i, j))
    return pl.pallas_call(
        body, name=f"pair_sum_{name}",
        grid_spec=pltpu.PrefetchScalarGridSpec(
            num_scalar_prefetch=1, grid=(N_CHIPS, ni, nj), in_specs=[ANY, blk], out_specs=blk,
            scratch_shapes=[pltpu.VMEM((2, tm, tc), BF16), pltpu.SemaphoreType.DMA((2,))]),
        out_shape=jax.ShapeDtypeStruct(theirs.shape, BF16),
        compiler_params=_cp(cfg, ("arbitrary",) * 3),
    )(pos, grad, theirs)


def scatter_start(cfg, pres, after):
    names = list(pres)
    n = len(names)

    def body(*refs):
        srcs, lands = refs[:n], refs[n:2 * n]
        ssem, rsem = refs[2 * n + 1], refs[2 * n + 2]
        token = refs[-1]
        x, y, c = _me()
        for i in range(n):
            for j, (px, py) in enumerate(_other_chips(x, y)):
                _remote(srcs[i].at[2 * px + py], lands[i].at[j], ssem.at[3 * i + j], rsem.at[3 * i + j], (px, py, c)).start()
        token[...] = jnp.zeros_like(token)

    lands = [lax.empty((3,) + pres[k].shape[1:], BF16) for k in names]
    ops = [_hbm(a) for a in [pres[k] for k in names] + lands]
    outs = pl.pallas_call(
        body, name="scatter_start_" + "_".join(names),
        in_specs=[HBM] * (2 * n) + [ANY],
        out_specs=[SEM, SEM] + [HBM] * (2 * n) + [TOKEN],
        out_shape=[pltpu.SemaphoreType.DMA((3 * n,)), pltpu.SemaphoreType.DMA((3 * n,))]
        + [pltpu.HBM(a.shape, a.dtype) for a in ops] + [jax.ShapeDtypeStruct((8, LANE), F32)],
        input_output_aliases={i: 2 + i for i in range(2 * n)},
        compiler_params=pltpu.CompilerParams(has_side_effects=EFFECT),
    )(*ops, after)
    return outs[0], outs[1], dict(zip(names, outs[2:2 + n])), dict(zip(names, outs[2 + n:2 + 2 * n])), outs[-1]


def scatter_wait(cfg, ssem, rsem, pres, lands, after):
    names = list(pres)
    n = len(names)

    def body(*refs):
        srcs, lands_ = refs[:n], refs[n:2 * n]
        ssem_, rsem_ = refs[2 * n], refs[2 * n + 1]
        x, y, c = _me()
        for i in range(n):
            for j, (px, py) in enumerate(_other_chips(x, y)):
                cp = _remote(srcs[i].at[2 * px + py], lands_[i].at[j], ssem_.at[3 * i + j], rsem_.at[3 * i + j], (px, py, c))
                cp.wait_send()
                cp.wait_recv()

    ops = [pres[k] for k in names] + [lands[k] for k in names]
    outs = pl.pallas_call(
        body, name="scatter_wait_" + "_".join(names),
        in_specs=[HBM] * (2 * n) + [SEM, SEM, ANY], out_specs=[HBM] * (2 * n),
        out_shape=[pltpu.HBM(a.shape, a.dtype) for a in ops],
        input_output_aliases={i: i for i in range(2 * n)},
        compiler_params=pltpu.CompilerParams(has_side_effects=EFFECT),
    )(*ops, ssem, rsem, after)
    return dict(zip(names, outs[:n])), dict(zip(names, outs[n:]))


def sum_landed(cfg, name, pre, land, pos):
    _, r, c = pre.shape
    tm, tc = _tile2(r, c)
    nrt = r // tm

    def body(pos_ref, p_ref, l_ref, o_ref):
        acc = p_ref[...].astype(F32)
        for j in range(3):
            acc = acc + l_ref[j].astype(F32)
        o_ref[...] = acc

    return pl.pallas_call(
        body, name=f"sum_landed_{name}",
        grid_spec=pltpu.PrefetchScalarGridSpec(
            num_scalar_prefetch=1, grid=(nrt, c // tc),
            in_specs=[pl.BlockSpec((None, tm, tc), lambda i, j, p: (p[0], i, j)),
                      pl.BlockSpec((3, tm, tc), lambda i, j, p: (0, i, j))],
            out_specs=pl.BlockSpec((tm, tc), lambda i, j, p: (p[1] * nrt + i, j))),
        out_shape=jax.ShapeDtypeStruct((2 * r, c), F32), compiler_params=_cp(cfg, ("parallel", "parallel")),
    )(pos, pre, land)


def half_swap(cfg, sums):
    names = list(sums)
    n = len(names)

    def body(*refs):
        outs = refs[n:2 * n]
        ssem, rsem = refs[2 * n:]
        x, y, c = _me()
        cps = [_remote(_rows_half(outs[i], c), _rows_half(outs[i], c), ssem.at[i], rsem.at[i], (x, y, 1 - c))
               for i in range(n)]
        for cp in cps:
            cp.start()
        for i in range(n):
            theirs = _rows_half(outs[i], 1 - c)
            _remote(theirs, theirs, ssem.at[i], rsem.at[i], (x, y, 1 - c)).wait_recv()
        for cp in cps:
            cp.wait_send()

    outs = pl.pallas_call(
        body, name="half_swap_" + "_".join(names), in_specs=[ANY] * n, out_specs=[ANY] * n,
        out_shape=[jax.ShapeDtypeStruct(sums[k].shape, F32) for k in names],
        input_output_aliases={i: i for i in range(n)},
        scratch_shapes=[pltpu.SemaphoreType.DMA((n,))] * 2,
    )(*[sums[k] for k in names])
    return dict(zip(names, outs))


class MeshWeights:
    def __init__(self, cfg, w_sh):
        self.cfg = cfg
        self.pos = jnp.stack([2 * lax.axis_index("x") + lax.axis_index("y"), lax.axis_index("c")]).astype(jnp.int32)
        self.full = {k: cast_into(cfg, k, w_sh[k], self.pos) for k in ("w_in",) + REST}
        self.conv_w = pad_conv_w(cfg, w_sh["conv_w"])
        self.inflight = {}
        self.grads = {}

    def weights_first(self):
        w_in, conv_w, self.token = gather_first(self.cfg, self.full["w_in"], self.conv_w)
        return w_in, conv_w

    def start_rest(self):
        out = gather_start(self.cfg, [self.full[k] for k in REST], self.token)
        self.rest = out[:3]
        return out[3]

    def weights_rest(self, after):
        return gather_finish(self.cfg, gather_wait(self.cfg, *self.rest, after))

    def reduce_start(self, grads):
        theirs = pair_send(self.cfg, grads)
        pres = {k: pair_sum(self.cfg, k, grads[k], theirs[k], self.pos) for k in grads}
        out = scatter_start(self.cfg, pres, jnp.zeros((8, LANE), F32))
        self.inflight[tuple(grads)] = out[:4]
        return out[4]

    def reduce_wait(self, names, after):
        cfg = self.cfg
        pres, lands = scatter_wait(cfg, *self.inflight.pop(tuple(names)), after)
        sums = {k: sum_landed(cfg, k, pres[k], lands[k], self.pos) for k in names}
        self.grads.update(half_swap(cfg, sums))


def allreduce_small(cfg, vec):
    R = vec.shape[0]

    def body(v_ref, o_ref, buf, send_sems, recv_sems):
        x, y, c = _me()
        me = 4 * x + 2 * y + c
        buf[me] = v_ref[...]
        sends = []
        for k in range(1, N_DEV):
            px, py, pc = x ^ (k >> 2), y ^ ((k >> 1) & 1), c ^ (k & 1)
            sends.append(pltpu.make_async_remote_copy(
                src_ref=v_ref, dst_ref=buf.at[me], send_sem=send_sems.at[k], recv_sem=recv_sems.at[k],
                device_id=(px, py, pc), device_id_type=MESH))
        for cp in sends:
            cp.start()
        for k in range(1, N_DEV):
            px, py, pc = x ^ (k >> 2), y ^ ((k >> 1) & 1), c ^ (k & 1)
            pltpu.make_async_remote_copy(
                src_ref=v_ref, dst_ref=buf.at[4 * px + 2 * py + pc], send_sem=send_sems.at[k],
                recv_sem=recv_sems.at[k], device_id=(px, py, pc), device_id_type=MESH).wait_recv()
        for cp in sends:
            cp.wait_send()
        acc = buf[0]
        for j in range(1, N_DEV):
            acc = acc + buf[j]
        o_ref[...] = acc

    return pl.pallas_call(
        body, name="allreduce_small",
        in_specs=[pl.BlockSpec(memory_space=pltpu.VMEM)], out_specs=pl.BlockSpec(memory_space=pltpu.VMEM),
        out_shape=jax.ShapeDtypeStruct((R, LANE), F32),
        scratch_shapes=[pltpu.VMEM((N_DEV, R, LANE), F32), pltpu.SemaphoreType.DMA((N_DEV,)),
                        pltpu.SemaphoreType.DMA((N_DEV,))],
    )(vec)


def adamw(cfg, name, w, m, v, g_parts, tile):
    r, c = w.shape
    tm, tc = tile[0] or r, tile[1] or c
    assert tc == c or all(g.shape[1] == c for g in g_parts)
    n = len(g_parts)
    bc1 = 1.0 - ADAM_B1 ** ADAM_STEP
    bc2 = 1.0 - ADAM_B2 ** ADAM_STEP

    def body(*refs):
        w_ref, m_ref, v_ref = refs[:3]
        g_refs = refs[3:3 + n]
        g_out, d_out, m_out, v_out = refs[3 + n:]
        g = g_refs[0][:, :tc]
        for gr in g_refs[1:]:
            g = g + gr[:, :tc]
        m_new = ADAM_B1 * m_ref[...] + (1.0 - ADAM_B1) * g
        v_new = ADAM_B2 * v_ref[...] + (1.0 - ADAM_B2) * jnp.square(g)
        m_hat = m_new / bc1
        v_hat = v_new / bc2
        g_out[...] = g
        d_out[...] = -ADAM_LR * (m_hat / (jnp.sqrt(v_hat) + ADAM_EPS) + ADAM_WD * w_ref[...])
        m_out[...] = m_new
        v_out[...] = v_new

    blk = pl.BlockSpec((tm, tc), lambda i, j: (i, j))
    return pl.pallas_call(
        body, name=f"adamw_{name}", grid=(r // tm, c // tc),
        in_specs=[blk] * 3 + [pl.BlockSpec((tm, tc if tc < c else g.shape[1]), lambda i, j: (i, j)) for g in g_parts],
        out_specs=[blk] * 4, out_shape=[jax.ShapeDtypeStruct((r, c), F32)] * 4,
        compiler_params=_cp(cfg, ("parallel", "parallel")),
    )(w, m, v, *g_parts)


SMALL_ORDER = ("loss", "g1", "g2", "g3", "g4", "g_sb", "g_dl", "conv_b", "conv_w")


def pack_small(small):
    rows = []
    for k in SMALL_ORDER:
        a = small[k].reshape(-1, LANE)
        rows.append(a)
    flat = jnp.concatenate(rows, axis=0)
    pad = (-flat.shape[0]) % 8
    return jnp.pad(flat, ((0, pad), (0, 0))), [r.shape[0] for r in rows]


def unpack_small(red, small, counts):
    out, at = {}, 0
    for k, n in zip(SMALL_ORDER, counts):
        out[k] = red[at:at + n].reshape(small[k].shape)
        at += n
    return out


def pad_ff(cfg, a):
    r = a.shape[0]
    return jnp.pad(a.reshape(r, N_CHIPS, cfg.FSH), ((0, 0), (0, 0), (0, cfg.FSHP - cfg.FSH))).reshape(r, cfg.FF2P)


def step(cfg, x, target, gains, w_sh, conv_b, m_all, v_all):
    chip = 2 * lax.axis_index("x") + lax.axis_index("y")
    comm = MeshWeights(cfg, w_sh)
    grad_x, small = local_step(cfg, comm, x, target, gains["g1"], gains["g2"], gains["g3"], gains["g4"],
                               gains["g_sb"], gains["g_dl"], pad_ff(cfg, conv_b))

    packed, counts = pack_small(small)
    red = unpack_small(allreduce_small(cfg, packed), small, counts)

    names = ("w_in", "w_out", "w_up", "w_down")
    up_rows = max(t for t in range(SUB, 513, SUB) if cfg.FSH % t == 0)
    tms = dict(w_in=(cfg.TM, None), w_out=(cfg.TM, None), w_up=(up_rows, None), w_down=(None, cfg.TN // 2))
    res = {}
    for n in names:
        res[n] = adamw(cfg, n, w_sh[n], m_all[n], v_all[n], [comm.grads[n]], tms[n])
    g_cw = lax.dynamic_slice_in_dim(red["conv_w"].reshape(3, N_CHIPS, cfg.FSHP), chip, 1, axis=1)[:, 0, :cfg.FSH]
    res["conv_w"] = adamw(cfg, "conv_w", w_sh["conv_w"], m_all["conv_w"], v_all["conv_w"], [g_cw], (None, None))
    g_cb = red["conv_b"].reshape(1, N_CHIPS, cfg.FSHP)[:, :, :cfg.FSH].reshape(1, N_CHIPS * cfg.FSH)
    res["conv_b"] = adamw(cfg, "conv_b", conv_b, m_all["conv_b"], v_all["conv_b"], [g_cb], (None, None))
    for k in ("g1", "g2", "g3", "g4", "g_sb", "g_dl"):
        res[k] = adamw(cfg, k, gains[k], m_all[k], v_all[k], [red[k]], (None, None))
    return red["loss"][0, 0], grad_x, res


PARAMS = ("pre_mix_gain", "post_mix_gain", "pre_ffn_gain", "post_ffn_gain", "w_in", "sb_out_gain", "dil_out_gain",
          "w_out", "w_up", "conv_w", "conv_b", "w_down")
SHORT = dict(pre_mix_gain="g1", post_mix_gain="g2", pre_ffn_gain="g3", post_ffn_gain="g4", sb_out_gain="g_sb",
             dil_out_gain="g_dl", w_in="w_in", w_out="w_out", w_up="w_up", conv_w="conv_w", conv_b="conv_b",
             w_down="w_down")


def kernel(x, pre_mix_gain, post_mix_gain, pre_ffn_gain, post_ffn_gain, w_in, sb_out_gain, dil_out_gain, w_out, w_up, conv_w, conv_b, w_down, loss_target, m_pre_mix_gain, m_post_mix_gain, m_pre_ffn_gain, m_post_ffn_gain, m_w_in, m_sb_out_gain, m_dil_out_gain, m_w_out, m_w_up, m_conv_w, m_conv_b, m_w_down, v_pre_mix_gain, v_post_mix_gain, v_pre_ffn_gain, v_post_ffn_gain, v_w_in, v_sb_out_gain, v_dil_out_gain, v_w_out, v_w_up, v_conv_w, v_conv_b, v_w_down):
    cfg = CFG
    w = dict(zip(PARAMS, (pre_mix_gain, post_mix_gain, pre_ffn_gain, post_ffn_gain, w_in, sb_out_gain, dil_out_gain,
                          w_out, w_up, conv_w, conv_b, w_down)))
    m = dict(zip(PARAMS, (m_pre_mix_gain, m_post_mix_gain, m_pre_ffn_gain, m_post_ffn_gain, m_w_in, m_sb_out_gain,
                          m_dil_out_gain, m_w_out, m_w_up, m_conv_w, m_conv_b, m_w_down)))
    v = dict(zip(PARAMS, (v_pre_mix_gain, v_post_mix_gain, v_pre_ffn_gain, v_post_ffn_gain, v_w_in, v_sb_out_gain,
                          v_dil_out_gain, v_w_out, v_w_up, v_conv_w, v_conv_b, v_w_down)))
    sq = lambda a: a.reshape(a.shape[1:])
    ws = {SHORT[k]: sq(a) if a.ndim == 3 else a for k, a in w.items()}
    ms = {SHORT[k]: sq(a) if a.ndim == 3 else a for k, a in m.items()}
    vs = {SHORT[k]: sq(a) if a.ndim == 3 else a for k, a in v.items()}
    for d in (ws, ms, vs):
        d["w_up"] = d["w_up"].T
    gains = {k: ws[k] for k in ("g1", "g2", "g3", "g4", "g_sb", "g_dl")}
    w_sh = {k: ws[k] for k in ("w_in", "w_out", "w_up", "conv_w", "w_down")}
    loss, grad_x, res = step(cfg, sq(x), sq(loss_target), gains, w_sh, ws["conv_b"], ms, vs)
    res["w_up"] = [a.T for a in res["w_up"]]
    outs = [loss, grad_x.reshape(x.shape)]
    for i in range(4):
        for k in PARAMS:
            outs.append(res[SHORT[k]][i].reshape(w[k].shape))
    return tuple(outs)
```

```python
import functools
import math
from typing import NamedTuple

import jax
import jax.numpy as jnp
from jax import lax
from jax.experimental import pallas as pl
from jax.experimental.pallas import tpu as pltpu

F32 = jnp.float32
BF16 = jnp.bfloat16
MESH = pl.DeviceIdType.MESH

ROPE_THETA = 10000.0
RMS_EPS = 1e-6
ADAM_LR = 0.001
ADAM_B1 = 0.9
ADAM_B2 = 0.999
ADAM_EPS = 1e-08
ADAM_WD = 0.01
ADAM_STEP = 10
GELU_C = math.sqrt(2.0 / math.pi)
NEG_BIG = -1e30
LANE = 128
N_CHIPS = 4
N_DEV = 8


class Cfg(NamedTuple):
    S: int = 2048
    D: int = 2048
    DH: int = 128
    HSB: int = 8
    HDL: int = 8
    QB: int = 128
    SBT: int = 256
    branches: tuple = ((128, 1), (512, 4), (2048, 16))
    FSH: int = 2752
    FSHP: int = 2816
    TM: int = 256
    TNF: int = 256
    FCH: int = 512
    TN: int = 512
    VMEM_MB: int = 56

    @property
    def DSB(self):
        return self.HSB * self.DH

    @property
    def DDL(self):
        return self.HDL * self.DH

    @property
    def DMIX(self):
        return self.DSB + self.DDL

    @property
    def FFP(self):
        return 2 * self.FSHP

    @property
    def FF2P(self):
        return 4 * self.FSHP


CFG = Cfg()


def _cp(cfg, sem=None):
    return pltpu.CompilerParams(dimension_semantics=sem, vmem_limit_bytes=cfg.VMEM_MB * 2**20)


def _dot(a, b):
    return jnp.dot(a, b, preferred_element_type=F32)


def _dot_nt(a, b):
    return lax.dot_general(a, b, (((1,), (1,)), ((), ())), preferred_element_type=F32)


def _dot_tn(a, b):
    return lax.dot_general(a, b, (((0,), (0,)), ((), ())), preferred_element_type=F32)


def _dot_split(x, u):
    hi = x.astype(BF16)
    lo = (x - hi.astype(F32)).astype(BF16)
    return _dot(hi, u) + _dot(lo, u)


def _rstd(x):
    return lax.rsqrt(jnp.mean(x * x, axis=-1, keepdims=True) + RMS_EPS)


def _rms_bwd(dy, x, g):
    r = _rstd(x)
    xh = x * r
    dxh = dy * g
    dx = r * (dxh - xh * jnp.mean(dxh * xh, axis=-1, keepdims=True))
    return dx, dy * xh


def _gelu(x):
    t = jnp.tanh(GELU_C * (x + 0.044715 * (x * x * x)))
    return 0.5 * x * (1.0 + t), t


def _gelu_grad(x, t):
    return 0.5 * (1.0 + t) + 0.5 * x * (1.0 - t * t) * (GELU_C * (1.0 + 3 * 0.044715 * (x * x)))


def _row(cfg, w):
    return pl.BlockSpec((cfg.TM, w), lambda i: (i, 0))


def _vec(w):
    return pl.BlockSpec((1, w), lambda i: (0, 0))


def rms_fwd(cfg, x, g):
    S, D = x.shape

    def body(x_ref, g_ref, h_ref):
        xv = x_ref[...]
        h_ref[...] = (xv * _rstd(xv) * g_ref[...]).astype(BF16)

    return pl.pallas_call(
        body, name="rms_fwd", grid=(S // cfg.TM,),
        in_specs=[_row(cfg, D), _vec(D)], out_specs=_row(cfg, D),
        out_shape=jax.ShapeDtypeStruct((S, D), BF16), compiler_params=_cp(cfg, ("parallel",)),
    )(x, g)


def mid_fwd(cfg, x, mo, g_post, g_pre):
    S, D = x.shape

    def body(x_ref, mo_ref, gp_ref, gn_ref, x1_ref, h2_ref):
        mo_v = mo_ref[...]
        x1 = x_ref[...] + mo_v * _rstd(mo_v) * gp_ref[...]
        x1_ref[...] = x1
        h2_ref[...] = (x1 * _rstd(x1) * gn_ref[...]).astype(BF16)

    return pl.pallas_call(
        body, name="mid_fwd", grid=(S // cfg.TM,),
        in_specs=[_row(cfg, D), _row(cfg, D), _vec(D), _vec(D)],
        out_specs=[_row(cfg, D), _row(cfg, D)],
        out_shape=[jax.ShapeDtypeStruct((S, D), F32), jax.ShapeDtypeStruct((S, D), BF16)],
        compiler_params=_cp(cfg, ("parallel",)),
    )(x, mo, g_post, g_pre)


def final_fwd_bwd(cfg, x1, f, g_post, target):
    S, D = x1.shape

    def body(x1_ref, f_ref, g_ref, t_ref, dout_ref, df_ref, dg_ref, loss_ref):
        @pl.when(pl.program_id(0) == 0)
        def _():
            dg_ref[...] = jnp.zeros_like(dg_ref)
            loss_ref[...] = jnp.zeros_like(loss_ref)

        fv = f_ref[...]
        g = g_ref[...]
        out = x1_ref[...] + fv * _rstd(fv) * g
        err = out - t_ref[...]
        loss_ref[...] += 0.5 * jnp.sum(jnp.mean(err * err, axis=-1, keepdims=True), axis=0, keepdims=True)
        dout = err * (1.0 / D)
        dout_ref[...] = dout
        df, dgx = _rms_bwd(dout, fv, g)
        df_ref[...] = df.astype(BF16)
        dg_ref[...] += jnp.sum(dgx, axis=0, keepdims=True)

    return pl.pallas_call(
        body, name="final_fwd_bwd", grid=(S // cfg.TM,),
        in_specs=[_row(cfg, D), _row(cfg, D), _vec(D), _row(cfg, D)],
        out_specs=[_row(cfg, D), _row(cfg, D), _vec(D), _vec(LANE)],
        out_shape=[jax.ShapeDtypeStruct((S, D), F32), jax.ShapeDtypeStruct((S, D), BF16),
                   jax.ShapeDtypeStruct((1, D), F32), jax.ShapeDtypeStruct((1, LANE), F32)],
        compiler_params=_cp(cfg, ("arbitrary",)),
    )(x1, f, g_post, target)


def mid_bwd(cfg, dh2, x1, g_pre, dout, mo, g_post):
    S, D = x1.shape

    def body(dh_ref, x1_ref, gn_ref, do_ref, mo_ref, gp_ref, dx1_ref, dmo_ref, dgn_ref, dgp_ref):
        @pl.when(pl.program_id(0) == 0)
        def _():
            dgn_ref[...] = jnp.zeros_like(dgn_ref)
            dgp_ref[...] = jnp.zeros_like(dgp_ref)

        dx, dgx = _rms_bwd(dh_ref[...], x1_ref[...], gn_ref[...])
        dx1 = do_ref[...] + dx
        dx1_ref[...] = dx1
        dgn_ref[...] += jnp.sum(dgx, axis=0, keepdims=True)
        dmo, dgy = _rms_bwd(dx1, mo_ref[...], gp_ref[...])
        dmo_ref[...] = dmo.astype(BF16)
        dgp_ref[...] += jnp.sum(dgy, axis=0, keepdims=True)

    return pl.pallas_call(
        body, name="mid_bwd", grid=(S // cfg.TM,),
        in_specs=[_row(cfg, D), _row(cfg, D), _vec(D), _row(cfg, D), _row(cfg, D), _vec(D)],
        out_specs=[_row(cfg, D), _row(cfg, D), _vec(D), _vec(D)],
        out_shape=[jax.ShapeDtypeStruct((S, D), F32), jax.ShapeDtypeStruct((S, D), BF16),
                   jax.ShapeDtypeStruct((1, D), F32), jax.ShapeDtypeStruct((1, D), F32)],
        compiler_params=_cp(cfg, ("arbitrary",)),
    )(dh2, x1, g_pre, dout, mo, g_post)


def first_bwd(cfg, dh1, x, g_pre, dx1):
    S, D = x.shape

    def body(dh_ref, x_ref, g_ref, r_ref, dx_ref, dg_ref):
        @pl.when(pl.program_id(0) == 0)
        def _():
            dg_ref[...] = jnp.zeros_like(dg_ref)

        dx, dgx = _rms_bwd(dh_ref[...], x_ref[...], g_ref[...])
        dx_ref[...] = r_ref[...] + dx
        dg_ref[...] += jnp.sum(dgx, axis=0, keepdims=True)

    return pl.pallas_call(
        body, name="first_bwd", grid=(S // cfg.TM,),
        in_specs=[_row(cfg, D), _row(cfg, D), _vec(D), _row(cfg, D)],
        out_specs=[_row(cfg, D), _vec(D)],
        out_shape=[jax.ShapeDtypeStruct((S, D), F32), jax.ShapeDtypeStruct((1, D), F32)],
        compiler_params=_cp(cfg, ("arbitrary",)),
    )(dh1, x, g_pre, dx1)


def _mm(cfg, name, a, b, *, nt, a_spec, b_spec, o_spec, grid, out_shape, acc_shape, dep=None):
    nk = grid[-1]
    dot = _dot_nt if nt else _dot
    deps = [] if dep is None else [dep]

    def body(a_ref, b_ref, *rest):
        o_ref, acc_ref = rest[-2:]
        k = pl.program_id(len(grid) - 1)
        part = dot(a_ref[...], b_ref[...])
        if deps:
            part = part + rest[0][0:1, 0:1]
        if nk == 1:
            o_ref[...] = part.astype(o_ref.dtype)
            return

        @pl.when(k == 0)
        def _():
            acc_ref[...] = part

        @pl.when(k > 0)
        def _():
            acc_ref[...] += part

        @pl.when(k == nk - 1)
        def _():
            o_ref[...] = acc_ref[...].astype(o_ref.dtype)

    sem = ("parallel",) * (len(grid) - 1) + ("arbitrary",)
    dep_specs = [pl.BlockSpec((8, LANE), lambda *_: (0, 0))] * len(deps)
    return pl.pallas_call(
        body, name=name, grid=grid, in_specs=[a_spec, b_spec] + dep_specs, out_specs=o_spec, out_shape=out_shape,
        scratch_shapes=[pltpu.VMEM(acc_shape, F32)], compiler_params=_cp(cfg, sem),
    )(a, b, *deps)


def _mm_tn(cfg, name, a, b, *, a_spec, b_spec, o_spec, grid, out_shape):
    def body(a_ref, b_ref, o_ref):
        o_ref[...] = _dot_tn(a_ref[...], b_ref[...]).astype(o_ref.dtype)

    return pl.pallas_call(
        body, name=name, grid=grid, in_specs=[a_spec, b_spec], out_specs=o_spec, out_shape=out_shape,
        compiler_params=_cp(cfg, ("parallel",) * len(grid)),
    )(a, b)


def qkv_proj(cfg, h1, w_in, cos2, sin2):
    S, D = h1.shape
    tn = 2 * cfg.DH
    per = cfg.DSB // tn
    assert cfg.DSB == cfg.DDL
    nblk = 6 * per

    def body(a_ref, b_ref, c_ref, s_ref, o_ref):
        j = pl.program_id(0)
        acc = _dot(a_ref[...], b_ref[...])
        rope = jnp.logical_and(j >= 3 * per, j < 5 * per)

        @pl.when(rope)
        def _():
            for c in range(tn // cfg.DH):
                xh = acc[:, c * cfg.DH:(c + 1) * cfg.DH]
                o_ref[:, c * cfg.DH:(c + 1) * cfg.DH] = (
                    xh * c_ref[...] + pltpu.roll(xh, cfg.DH // 2, 1) * s_ref[...]).astype(BF16)

        @pl.when(jnp.logical_not(rope))
        def _():
            o_ref[...] = acc.astype(BF16)

    return pl.pallas_call(
        body, name="qkv_proj", grid=(nblk,),
        in_specs=[pl.BlockSpec((S, D), lambda j: (0, 0)), pl.BlockSpec((D, tn), lambda j: (0, j)),
                  pl.BlockSpec((S, cfg.DH), lambda j: (0, 0)), pl.BlockSpec((S, cfg.DH), lambda j: (0, 0))],
        out_specs=pl.BlockSpec((None, S, tn), lambda j: (j // per, 0, j % per)),
        out_shape=jax.ShapeDtypeStruct((6, S, cfg.DSB), BF16),
        compiler_params=_cp(cfg, ("parallel",)),
    )(h1, w_in, cos2, sin2)


def _sb_tile(cfg, q, k, kb, qb):
    QB = cfg.SBT
    z = _dot_nt(q, k) * (cfg.DH ** -0.5)
    t1 = jnp.log1p(jnp.exp(-jnp.abs(z)))
    lb = jnp.minimum(z, 0.0) - t1
    row = lax.broadcasted_iota(jnp.int32, (QB, QB), 0)
    col = lax.broadcasted_iota(jnp.int32, (QB, QB), 1)
    valid = jnp.logical_or(kb < qb, col < row)
    lk = jnp.where(valid, jnp.minimum(-z, 0.0) - t1, 0.0)
    return lb, lk, valid


def sb_fwd(cfg, qkv3):
    S, QB, DH = cfg.S, cfg.SBT, cfg.DH

    def body(q_ref, k_ref, v_ref, o_ref, t_ref):
        row = lax.broadcasted_iota(jnp.int32, (QB, QB), 0)
        col = lax.broadcasted_iota(jnp.int32, (QB, QB), 1)
        u_after = (row > col).astype(BF16)

        def q_loop(qb, _):
            rows = pl.ds(pl.multiple_of(qb * QB, QB), QB)
            q = q_ref[rows, :]

            def k_loop(i, carry):
                o_acc, c = carry
                kb = qb - i
                krows = pl.ds(pl.multiple_of(kb * QB, QB), QB)
                lb, lk, valid = _sb_tile(cfg, q, k_ref[krows, :], kb, qb)
                rem = _dot_split(lk, u_after) + c
                a = jnp.where(valid, jnp.exp(lb + rem), 0.0)
                o_acc = o_acc + _dot(a.astype(BF16), v_ref[krows, :])
                return o_acc, c + jnp.sum(lk, axis=1, keepdims=True)

            o_acc, c = lax.fori_loop(0, qb + 1, k_loop, (jnp.zeros((QB, DH), F32), jnp.zeros((QB, 1), F32)))
            o_ref[rows, :] = o_acc
            t_ref[rows, :] = jnp.broadcast_to(c, (QB, DH))
            return 0

        lax.fori_loop(0, S // QB, q_loop, 0)

    def spec(i):
        return pl.BlockSpec((None, S, DH), lambda h: (i, 0, h))

    return pl.pallas_call(
        body, name="sb_fwd", grid=(cfg.HSB,),
        in_specs=[spec(0), spec(1), spec(2)],
        out_specs=[pl.BlockSpec((S, DH), lambda h: (0, h))] * 2,
        out_shape=[jax.ShapeDtypeStruct((S, cfg.DSB), F32)] * 2,
        compiler_params=_cp(cfg, ("parallel",)),
    )(qkv3, qkv3, qkv3)


def sb_bwd(cfg, qkv3, do_sb, tsum):
    S, QB, DH = cfg.S, cfg.SBT, cfg.DH
    scale = DH ** -0.5

    def body(q_ref, k_ref, v_ref, do_ref, t_ref, d_ref, dk_acc, dv_acc):
        dk_acc[...] = jnp.zeros_like(dk_acc)
        dv_acc[...] = jnp.zeros_like(dv_acc)
        row = lax.broadcasted_iota(jnp.int32, (QB, QB), 0)
        col = lax.broadcasted_iota(jnp.int32, (QB, QB), 1)
        u_upto = (row <= col).astype(BF16)
        u_before = (row < col).astype(BF16)

        def q_loop(qb, _):
            rows = pl.ds(pl.multiple_of(qb * QB, QB), QB)
            q = q_ref[rows, :]
            do = do_ref[rows, :]
            total = t_ref[rows, 0:1]

            def k_loop(kb, carry):
                dq_acc, pc, gc = carry
                krows = pl.ds(pl.multiple_of(kb * QB, QB), QB)
                k = k_ref[krows, :]
                v = v_ref[krows, :]
                lb, lk, valid = _sb_tile(cfg, q, k, kb, qb)
                rem = total - pc - _dot_split(lk, u_upto)
                a = jnp.where(valid, jnp.exp(lb + rem), 0.0)
                g = a * _dot_nt(do, v)
                dv_acc[krows, :] += _dot_tn(a.astype(BF16), do)
                cum = gc + _dot(g.astype(BF16), u_before)
                sig = jnp.exp(lb)
                dz = (jnp.where(valid, g * (1.0 - sig) - cum * sig, 0.0) * scale).astype(BF16)
                dq_acc = dq_acc + _dot(dz, k)
                dk_acc[krows, :] += _dot_tn(dz, q)
                return dq_acc, pc + jnp.sum(lk, axis=1, keepdims=True), gc + jnp.sum(g, axis=1, keepdims=True)

            z1 = jnp.zeros((QB, 1), F32)
            dq_acc, _, _ = lax.fori_loop(0, qb + 1, k_loop, (jnp.zeros((QB, DH), F32), z1, z1))
            d_ref[0, rows, :] = dq_acc.astype(BF16)
            return 0

        lax.fori_loop(0, S // QB, q_loop, 0)
        d_ref[1, :, :] = dk_acc[...].astype(BF16)
        d_ref[2, :, :] = dv_acc[...].astype(BF16)

    def spec(i):
        return pl.BlockSpec((None, S, DH), lambda h: (i, 0, h))

    return pl.pallas_call(
        body, name="sb_bwd", grid=(cfg.HSB,),
        in_specs=[spec(0), spec(1), spec(2), pl.BlockSpec((S, DH), lambda h: (0, h)),
                  pl.BlockSpec((S, DH), lambda h: (0, h))],
        out_specs=pl.BlockSpec((3, S, DH), lambda h: (0, 0, h)),
        out_shape=jax.ShapeDtypeStruct((6, S, cfg.DSB), BF16),
        scratch_shapes=[pltpu.VMEM((S, DH), F32), pltpu.VMEM((S, DH), F32)],
        compiler_params=_cp(cfg, ("parallel",)),
    )(qkv3, qkv3, qkv3, do_sb, tsum)


def _band_mask(cfg, n, n_back):
    QB = cfg.QB
    qi = lax.broadcasted_iota(jnp.int32, (QB, 2 * QB), 0)
    kj = lax.broadcasted_iota(jnp.int32, (QB, 2 * QB), 1)
    dist = QB + qi - kj
    return (dist >= 0) & (dist <= n_back) & jnp.logical_or(n > 0, kj >= QB)


def _sub_rows(start, n, dil):
    if dil > 1:
        return pl.ds(start, n, stride=dil)
    return pl.ds(start if isinstance(start, int) else pl.multiple_of(start, 8), n)


def _stage_residues(cfg, dil, pairs):
    QB, L = cfg.QB, cfg.S // dil
    for src, dst in pairs:
        for r in range(dil):
            dst[pl.ds(r * (QB + L), QB), :] = jnp.zeros((QB, cfg.DH), BF16)
            dst[pl.ds(r * (QB + L) + QB, L), :] = src[_sub_rows(r, L, dil), :].astype(BF16)


def _staged_rows(cfg):
    return cfg.S + cfg.QB * max(d for _, d in cfg.branches)


def _lane_value(x):
    return jnp.max(x, axis=1, keepdims=True)


def dil_fwd(cfg, qkv3):
    S, QB, DH = cfg.S, cfg.QB, cfg.DH
    scale = DH ** -0.5
    nb = len(cfg.branches)
    mix_rows = min(256, S)

    def body(q_ref, k_ref, v_ref, o_ref, lt_ref, qf, kf, vf, kp, vp, *obl):
        obs, lbs = obl[:nb], obl[nb:]
        qf[...] = q_ref[...].astype(F32)
        kf[...] = k_ref[...].astype(F32)
        vf[...] = v_ref[...].astype(F32)
        for b, (window, dil) in enumerate(cfg.branches):
            L, n_back = S // dil, window // dil
            assert n_back <= QB and L % QB == 0
            _stage_residues(cfg, dil, [(kf, kp), (vf, vp)])
            for r in range(dil):
                for n in range(L // QB):
                    rows = _sub_rows(r + n * (QB * dil), QB, dil)
                    band = pl.ds(r * (QB + L) + n * QB, 2 * QB)
                    s = _dot_nt(qf[rows, :].astype(BF16), kp[band, :]) * scale
                    s = jnp.where(_band_mask(cfg, n, n_back), s, NEG_BIG)
                    m = jnp.max(s, axis=1, keepdims=True)
                    p = jnp.exp(s - m)
                    den = jnp.sum(p, axis=1, keepdims=True)
                    obs[b][rows, :] = _dot(p.astype(BF16), vp[band, :]) / den
                    lbs[b][rows, :] = jnp.broadcast_to(m + jnp.log(den), (QB, DH))

        def mix(i, _):
            rows = pl.ds(pl.multiple_of(i * mix_rows, mix_rows), mix_rows)
            ls = [r[rows, :] for r in lbs]
            m = functools.reduce(jnp.maximum, ls)
            es = [jnp.exp(l - m) for l in ls]
            tot = functools.reduce(jnp.add, es)
            o_ref[rows, :] = functools.reduce(jnp.add, [(e / tot) * r[rows, :] for e, r in zip(es, obs)])
            lt_ref[rows, :] = m + jnp.log(tot)
            return 0

        lax.fori_loop(0, S // mix_rows, mix, 0)

    def spec(i):
        return pl.BlockSpec((None, S, DH), lambda h: (i, 0, h))

    o_spec = pl.BlockSpec((S, DH), lambda h: (0, h))
    return pl.pallas_call(
        body, name="dil_fwd", grid=(cfg.HDL,),
        in_specs=[spec(3), spec(4), spec(5)], out_specs=[o_spec, o_spec],
        out_shape=[jax.ShapeDtypeStruct((S, cfg.DDL), F32)] * 2,
        scratch_shapes=[pltpu.VMEM((S, DH), F32)] * 3 + [pltpu.VMEM((_staged_rows(cfg), DH), BF16)] * 2
        + [pltpu.VMEM((S, DH), F32)] * (2 * nb),
        compiler_params=_cp(cfg, ("parallel",)),
    )(qkv3, qkv3, qkv3)


def dil_bwd(cfg, qkv3, do_dl, delta, lse_tot, cos2, sin2, d_sb3):
    S, QB, DH = cfg.S, cfg.QB, cfg.DH
    scale = DH ** -0.5
    out_rows = min(256, S)

    def body(q_ref, k_ref, v_ref, do_ref, dl_ref, lt_ref, c_ref, s_ref, base_ref, d_ref,
             qf, kf, vf, dof, kp, vp, dkp, dvp, dqn, dkn, dvn):
        qf[...] = q_ref[...].astype(F32)
        kf[...] = k_ref[...].astype(F32)
        vf[...] = v_ref[...].astype(F32)
        dof[...] = do_ref[...].astype(F32)
        for acc in (dqn, dkn, dvn):
            acc[...] = jnp.zeros_like(acc)
        for window, dil in cfg.branches:
            L, n_back = S // dil, window // dil
            reg = QB + L
            _stage_residues(cfg, dil, [(kf, kp), (vf, vp)])
            dkp[pl.ds(0, dil * reg), :] = jnp.zeros((dil * reg, DH), F32)
            dvp[pl.ds(0, dil * reg), :] = jnp.zeros((dil * reg, DH), F32)
            for r in range(dil):
                for n in range(L // QB):
                    rows = _sub_rows(r + n * (QB * dil), QB, dil)
                    band = pl.ds(r * reg + n * QB, 2 * QB)
                    q = qf[rows, :].astype(BF16)
                    do = dof[rows, :].astype(BF16)
                    kb = kp[band, :]
                    s = _dot_nt(q, kb) * scale
                    s = jnp.where(_band_mask(cfg, n, n_back), s, NEG_BIG)
                    p = jnp.exp(s - _lane_value(lt_ref[rows, :]))
                    ds = (p * (_dot_nt(do, vp[band, :]) - _lane_value(dl_ref[rows, :])) * scale).astype(BF16)
                    dqn[rows, :] += _dot(ds, kb)
                    dkp[band, :] += _dot_tn(ds, q)
                    dvp[band, :] += _dot_tn(p.astype(BF16), do)
            for r in range(dil):
                sub = _sub_rows(r, L, dil)
                dkn[sub, :] += dkp[pl.ds(r * reg + QB, L), :]
                dvn[sub, :] += dvp[pl.ds(r * reg + QB, L), :]

        def finish(i, _):
            rows = pl.ds(pl.multiple_of(i * out_rows, out_rows), out_rows)
            c, sn = c_ref[rows, :], s_ref[rows, :]
            for j, acc in enumerate((dqn, dkn)):
                d = acc[rows, :]
                d_ref[j, rows, :] = (d * c + pltpu.roll(d * sn, DH // 2, 1)).astype(BF16)
            d_ref[2, rows, :] = dvn[rows, :].astype(BF16)
            return 0

        lax.fori_loop(0, S // out_rows, finish, 0)

    def spec(i):
        return pl.BlockSpec((None, S, DH), lambda h: (i, 0, h))

    hd = pl.BlockSpec((S, DH), lambda h: (0, h))
    tab = pl.BlockSpec((S, DH), lambda h: (0, 0))
    ns = _staged_rows(cfg)
    return pl.pallas_call(
        body, name="dil_bwd", grid=(cfg.HDL,),
        in_specs=[spec(3), spec(4), spec(5), hd, hd, hd, tab, tab, ANY],
        out_specs=pl.BlockSpec((3, S, DH), lambda h: (1, 0, h)),
        out_shape=jax.ShapeDtypeStruct((6, S, cfg.DDL), BF16),
        input_output_aliases={8: 0},
        scratch_shapes=[pltpu.VMEM((S, DH), F32)] * 4 + [pltpu.VMEM((ns, DH), BF16)] * 2
        + [pltpu.VMEM((ns, DH), F32)] * 2 + [pltpu.VMEM((S, DH), F32)] * 3,
        compiler_params=_cp(cfg, ("parallel",)),
    )(qkv3, qkv3, qkv3, do_dl, delta, lse_tot, cos2, sin2, d_sb3)


def combine_fwd(cfg, o_sb, o_dl, g_sb, g_dl):
    S, DH = cfg.S, cfg.DH

    def head_norm(o, g):
        return o * lax.rsqrt(jnp.mean(o * o, axis=-1, keepdims=True) + RMS_EPS) * g

    def body(osb_ref, odl_ref, gsb_ref, gdl_ref, mix_ref):
        for h in range(cfg.HSB):
            c = slice(h * DH, (h + 1) * DH)
            mix_ref[:, c] = head_norm(osb_ref[:, c], gsb_ref[:, c]).astype(BF16)
        for h in range(cfg.HDL):
            c = slice(h * DH, (h + 1) * DH)
            mix_ref[:, cfg.DSB + h * DH:cfg.DSB + (h + 1) * DH] = head_norm(odl_ref[:, c], gdl_ref[:, c]).astype(BF16)

    return pl.pallas_call(
        body, name="combine_fwd", grid=(S // cfg.TM,),
        in_specs=[_row(cfg, cfg.DSB), _row(cfg, cfg.DDL), _vec(cfg.DSB), _vec(cfg.DDL)],
        out_specs=_row(cfg, cfg.DMIX), out_shape=jax.ShapeDtypeStruct((S, cfg.DMIX), BF16),
        compiler_params=_cp(cfg, ("parallel",)),
    )(o_sb, o_dl, g_sb, g_dl)


def combine_bwd(cfg, dmix, o_sb, o_dl, g_sb, g_dl):
    S, DH = cfg.S, cfg.DH

    def body(dm_ref, osb_ref, odl_ref, gsb_ref, gdl_ref, dsb_ref, ddl_ref, dl_ref, dgsb_ref, dgdl_ref):
        @pl.when(pl.program_id(0) == 0)
        def _():
            dgsb_ref[...] = jnp.zeros_like(dgsb_ref)
            dgdl_ref[...] = jnp.zeros_like(dgdl_ref)

        for h in range(cfg.HSB):
            c = slice(h * DH, (h + 1) * DH)
            dx, dgx = _rms_bwd(dm_ref[:, c], osb_ref[:, c], gsb_ref[:, c])
            dsb_ref[:, c] = dx.astype(BF16)
            dgsb_ref[:, c] += jnp.sum(dgx, axis=0, keepdims=True)
        for h in range(cfg.HDL):
            c = slice(h * DH, (h + 1) * DH)
            o = odl_ref[:, c]
            dx, dgx = _rms_bwd(dm_ref[:, cfg.DSB + h * DH:cfg.DSB + (h + 1) * DH], o, gdl_ref[:, c])
            ddl_ref[:, c] = dx.astype(BF16)
            dl_ref[:, c] = jnp.broadcast_to(jnp.sum(dx * o, axis=-1, keepdims=True), dx.shape)
            dgdl_ref[:, c] += jnp.sum(dgx, axis=0, keepdims=True)

    return pl.pallas_call(
        body, name="combine_bwd", grid=(S // cfg.TM,),
        in_specs=[_row(cfg, cfg.DMIX), _row(cfg, cfg.DSB), _row(cfg, cfg.DDL), _vec(cfg.DSB), _vec(cfg.DDL)],
        out_specs=[_row(cfg, cfg.DSB), _row(cfg, cfg.DDL), _row(cfg, cfg.DDL), _vec(cfg.DSB), _vec(cfg.DDL)],
        out_shape=[jax.ShapeDtypeStruct((S, cfg.DSB), BF16), jax.ShapeDtypeStruct((S, cfg.DDL), BF16),
                   jax.ShapeDtypeStruct((S, cfg.DDL), F32), jax.ShapeDtypeStruct((1, cfg.DSB), F32),
                   jax.ShapeDtypeStruct((1, cfg.DDL), F32)],
        compiler_params=_cp(cfg, ("arbitrary",)),
    )(dmix, o_sb, o_dl, g_sb, g_dl)


SUB = 8


def _shift_down(u, prev, j):
    rolled = pltpu.roll(u, j, 0)
    row = lax.broadcasted_iota(jnp.int32, (SUB, u.shape[1]), 0)
    head = jnp.where(row >= j, rolled[:SUB], pltpu.roll(prev, j, 0))
    return jnp.concatenate([head, rolled[SUB:]], axis=0)


def _shift_up(u, nxt, j):
    n = u.shape[0]
    rolled = pltpu.roll(u, n - j, 0)
    row = lax.broadcasted_iota(jnp.int32, (SUB, u.shape[1]), 0)
    tail = jnp.where(row < SUB - j, rolled[n - SUB:], pltpu.roll(nxt, SUB - j, 0))
    return jnp.concatenate([rolled[:n - SUB], tail], axis=0)


def _conv(u, s1, s2, cw, cb):
    return u * cw[2:3, :] + s1 * cw[1:2, :] + s2 * cw[0:1, :] + cb


def _chunk_rows(cfg):
    ch = min(cfg.FCH, cfg.S)
    return ch, cfg.S // ch


def ffn_fwd(cfg, h2, w_up, conv_w, conv_b):
    S, D = h2.shape
    tn, nt = cfg.TNF, cfg.FFP // cfg.TNF
    ch, nch = _chunk_rows(cfg)

    def body(h_ref, wg_ref, wv_ref, cwg_ref, cwv_ref, cbg_ref, cbv_ref, u_ref, y_ref):
        prev = [jnp.zeros((SUB, tn), F32)] * 2
        for ci in range(nch):
            rows = pl.ds(ci * ch, ch)
            h = h_ref[rows, :]
            us = [_dot_nt(h, wg_ref[...]), _dot_nt(h, wv_ref[...])]
            cs = []
            for i, (cw_ref, cb_ref) in enumerate(((cwg_ref, cbg_ref), (cwv_ref, cbv_ref))):
                u_ref[i, rows, :] = us[i]
                cs.append(_conv(us[i], _shift_down(us[i], prev[i], 1), _shift_down(us[i], prev[i], 2),
                                cw_ref[...], cb_ref[...]))
            y_ref[rows, :] = (_gelu(cs[0])[0] * cs[1]).astype(BF16)
            prev = [u[ch - SUB:] for u in us]

    return pl.pallas_call(
        body, name="ffn_fwd", grid=(nt,),
        in_specs=[pl.BlockSpec((S, D), lambda n: (0, 0)),
                  pl.BlockSpec((tn, D), lambda n: (n, 0)), pl.BlockSpec((tn, D), lambda n: (n + nt, 0)),
                  pl.BlockSpec((3, tn), lambda n: (0, n)), pl.BlockSpec((3, tn), lambda n: (0, n + nt)),
                  pl.BlockSpec((1, tn), lambda n: (0, n)), pl.BlockSpec((1, tn), lambda n: (0, n + nt))],
        out_specs=[pl.BlockSpec((2, S, tn), lambda n: (0, 0, n)), pl.BlockSpec((S, tn), lambda n: (0, n))],
        out_shape=[jax.ShapeDtypeStruct((2, S, cfg.FFP), F32), jax.ShapeDtypeStruct((S, cfg.FFP), BF16)],
        compiler_params=_cp(cfg, ("parallel",)),
    )(h2, w_up, w_up, conv_w, conv_w, conv_b, conv_b)


def ffn_bwd(cfg, df, h2, w_down, u, conv_w, conv_b):
    S, D = df.shape
    tn, nt = cfg.TNF, cfg.FFP // cfg.TNF

    ch, nch = _chunk_rows(cfg)

    def body(df_ref, h_ref, wd_ref, u_ref, cwg_ref, cwv_ref, cbg_ref, cbv_ref,
             du_ref, dwd_ref, dwu_ref, dcw_ref, dcb_ref):
        cws = (cwg_ref[...], cwv_ref[...])
        cbs = (cbg_ref[...], cbv_ref[...])
        zero = jnp.zeros((SUB, tn), F32)
        nxt = [zero, zero]
        dws = [[jnp.zeros((1, tn), F32)] * 4 for _ in range(2)]
        dwd = jnp.zeros((tn, D), F32)
        dwu = [jnp.zeros((tn, D), F32)] * 2
        for ci in reversed(range(nch)):
            rows = pl.ds(ci * ch, ch)
            dfv = df_ref[rows, :]
            hv = h_ref[rows, :]
            dy = _dot_nt(dfv, wd_ref[...])
            us, s1, s2, cs = [], [], [], []
            for i in range(2):
                u = u_ref[i, rows, :]
                prev = u_ref[i, pl.ds(ci * ch - SUB, SUB), :] if ci else zero
                us.append(u)
                s1.append(_shift_down(u, prev, 1))
                s2.append(_shift_down(u, prev, 2))
                cs.append(_conv(u, s1[i], s2[i], cws[i], cbs[i]))
            gl, t = _gelu(cs[0])
            dwd = dwd + _dot_tn((gl * cs[1]).astype(BF16), dfv)
            dcs = (dy * cs[1] * _gelu_grad(cs[0], t), dy * gl)
            for i, dc in enumerate(dcs):
                du = dc * cws[i][2:3, :] + _shift_up(dc, nxt[i], 1) * cws[i][1:2, :] + _shift_up(dc, nxt[i], 2) * cws[i][0:1, :]
                du = du.astype(BF16)
                du_ref[i, rows, :] = du
                dwu[i] = dwu[i] + _dot_tn(du, hv)
                for j, tap in enumerate((s2[i], s1[i], us[i])):
                    dws[i][j] = dws[i][j] + jnp.sum(dc * tap, axis=0, keepdims=True)
                dws[i][3] = dws[i][3] + jnp.sum(dc, axis=0, keepdims=True)
            nxt = [dc[:SUB] for dc in dcs]
        dwd_ref[...] = dwd.astype(BF16)
        for i in range(2):
            dwu_ref[i] = dwu[i].astype(BF16)
            for j in range(3):
                dcw_ref[i, j:j + 1, :] = dws[i][j]
            dcb_ref[i] = dws[i][3]

    whole = pl.BlockSpec((S, D), lambda n: (0, 0), pipeline_mode=pl.Buffered(1))
    du, dwd, dwu, dcw, dcb = pl.pallas_call(
        body, name="ffn_bwd", grid=(nt,),
        in_specs=[whole, whole, pl.BlockSpec((tn, D), lambda n: (n, 0)),
                  pl.BlockSpec((2, S, tn), lambda n: (0, 0, n)),
                  pl.BlockSpec((3, tn), lambda n: (0, n)), pl.BlockSpec((3, tn), lambda n: (0, n + nt)),
                  pl.BlockSpec((1, tn), lambda n: (0, n)), pl.BlockSpec((1, tn), lambda n: (0, n + nt))],
        out_specs=[pl.BlockSpec((2, S, tn), lambda n: (0, 0, n)), pl.BlockSpec((tn, D), lambda n: (n, 0)),
                   pl.BlockSpec((2, tn, D), lambda n: (0, n, 0)),
                   pl.BlockSpec((2, 3, tn), lambda n: (0, 0, n)), pl.BlockSpec((2, 1, tn), lambda n: (0, 0, n))],
        out_shape=[jax.ShapeDtypeStruct((2, S, cfg.FFP), BF16), jax.ShapeDtypeStruct((cfg.FFP, D), BF16),
                   jax.ShapeDtypeStruct((2, cfg.FFP, D), BF16),
                   jax.ShapeDtypeStruct((2, 3, cfg.FFP), F32), jax.ShapeDtypeStruct((2, 1, cfg.FFP), F32)],
        compiler_params=_cp(cfg, ("parallel",)),
    )(df, h2, w_down, u, conv_w, conv_w, conv_b, conv_b)
    return du, dwd, dwu.reshape(cfg.FF2P, D), dcw, dcb


def rope_tables(cfg):
    inv_freq = ROPE_THETA ** (-jnp.arange(0, cfg.DH, 2, dtype=F32) / cfg.DH)
    ang = jnp.arange(cfg.S, dtype=F32)[:, None] * inv_freq[None, :]
    cos, sin = jnp.cos(ang), jnp.sin(ang)
    return jnp.concatenate([cos, cos], axis=1), jnp.concatenate([-sin, sin], axis=1)


class LocalWeights:
    def __init__(self, w_in, w_out, w_up, conv_w, w_down):
        self.w = (w_in, w_out, w_up, conv_w, w_down)
        self.grads = {}

    def weights_first(self):
        return self.w[0], self.w[3]

    def start_rest(self):
        return None

    def weights_rest(self, group, after):
        return (self.w[1], self.w[2]) if group == 0 else (self.w[4],)

    def pair_start(self, grads):
        self.grads.update(grads)
        return None

    def reduce_start(self, grads, after=None):
        self.grads.update(grads)
        return None

    def reduce_wait(self, names, after):
        pass


def _after(a, token):
    return a if token is None else a + token[0, 0].astype(a.dtype)


def local_step(cfg, comm, x, target, g1, g2, g3, g4, g_sb, g_dl, conv_b):
    S, D = cfg.S, cfg.D
    cos2, sin2 = rope_tables(cfg)
    full = lambda r, c: pl.BlockSpec((r, c), lambda j, k: (0, 0))

    w_in, conv_w = comm.weights_first()
    h1 = rms_fwd(cfg, x, g1)
    qkv3 = qkv_proj(cfg, h1, w_in, _after(cos2, comm.start_rest()), sin2)
    o_sb, tsum = sb_fwd(cfg, qkv3)
    o_dl, lse_tot = dil_fwd(cfg, qkv3)
    mixed = combine_fwd(cfg, o_sb, o_dl, g_sb, g_dl)
    w_out, w_up = comm.weights_rest(0, after=mixed)
    tn = cfg.TN
    mo = _mm(cfg, "mix_out", mixed, w_out, nt=False, grid=(D // tn, 1),
             a_spec=full(S, cfg.DMIX), b_spec=pl.BlockSpec((cfg.DMIX, tn), lambda j, k: (0, j)),
             o_spec=pl.BlockSpec((S, tn), lambda j, k: (0, j)),
             out_shape=jax.ShapeDtypeStruct((S, D), F32), acc_shape=(8, LANE))
    x1, h2 = mid_fwd(cfg, x, mo, g2, g3)
    u, y = ffn_fwd(cfg, h2, w_up, conv_w, conv_b)
    w_down, = comm.weights_rest(1, after=y)
    tk = cfg.FFP // 2
    f = _mm(cfg, "ffn_down", y, w_down, nt=False, grid=(D // tn, cfg.FFP // tk),
            a_spec=pl.BlockSpec((S, tk), lambda j, k: (0, k)), b_spec=pl.BlockSpec((tk, tn), lambda j, k: (k, j)),
            o_spec=pl.BlockSpec((S, tn), lambda j, k: (0, j)),
            out_shape=jax.ShapeDtypeStruct((S, D), F32), acc_shape=(S, tn))
    dout, df, dg4, loss = final_fwd_bwd(cfg, x1, f, g4, target)

    du, dw_down, dw_up, dconv_w, dconv_b = ffn_bwd(cfg, df, h2, w_down, u, conv_w, conv_b)
    kt = cfg.FFP // tk
    dh2 = _mm(cfg, "d_h2", du, w_up, nt=False, grid=(D // tn, 2 * kt),
              a_spec=pl.BlockSpec((None, S, tk), lambda j, k: (k // kt, 0, k % kt)),
              b_spec=pl.BlockSpec((tk, tn), lambda j, k: (k, j)),
              o_spec=pl.BlockSpec((S, tn), lambda j, k: (0, j)),
              out_shape=jax.ShapeDtypeStruct((S, D), F32), acc_shape=(S, tn),
              dep=comm.pair_start(dict(w_down=dw_down, w_up=dw_up)))
    dx1, dmo, dg3, dg2 = mid_bwd(cfg, dh2, x1, g3, dout, mo, g2)

    dmix = _mm(cfg, "d_mixed", dmo, w_out, nt=True, grid=(cfg.DMIX // tn, 1),
               a_spec=full(S, D), b_spec=pl.BlockSpec((tn, D), lambda j, k: (j, 0)),
               o_spec=pl.BlockSpec((S, tn), lambda j, k: (0, j)),
               out_shape=jax.ShapeDtypeStruct((S, cfg.DMIX), F32), acc_shape=(8, LANE))
    dw_out = _mm_tn(cfg, "d_w_out", mixed, dmo, grid=(D // tn,),
                    a_spec=pl.BlockSpec((S, cfg.DMIX), lambda j: (0, 0)),
                    b_spec=pl.BlockSpec((S, tn), lambda j: (0, j)),
                    o_spec=pl.BlockSpec((cfg.DMIX, tn), lambda j: (0, j)),
                    out_shape=jax.ShapeDtypeStruct((cfg.DMIX, D), BF16))
    token = comm.reduce_start(dict(w_out=dw_out), after=dw_out)
    do_sb, do_dl, delta, dg_sb, dg_dl = combine_bwd(cfg, dmix, o_sb, o_dl, _after(g_sb, token), g_dl)
    d_sb3 = sb_bwd(cfg, qkv3, do_sb, tsum)
    dqkv3 = dil_bwd(cfg, qkv3, do_dl, delta, lse_tot, cos2, sin2, d_sb3)
    comm.reduce_wait(("w_out", "w_up", "w_down"), after=dqkv3)
    tkq = min(tn, cfg.DSB)
    kq = cfg.DSB // tkq
    dw_in = _mm_tn(cfg, "d_w_in", h1, dqkv3, grid=(6 * kq,),
                   a_spec=pl.BlockSpec((S, D), lambda j: (0, 0)),
                   b_spec=pl.BlockSpec((None, S, tkq), lambda j: (j // kq, 0, j % kq)),
                   o_spec=pl.BlockSpec((D, tkq), lambda j: (0, j)),
                   out_shape=jax.ShapeDtypeStruct((D, 6 * cfg.DSB), BF16))
    token = comm.reduce_start(dict(w_in=dw_in))
    dh1 = _mm(cfg, "d_h1", dqkv3, w_in, nt=True, grid=(D // tn, 6),
              a_spec=pl.BlockSpec((None, S, cfg.DSB), lambda j, k: (k, 0, 0)),
              b_spec=pl.BlockSpec((tn, cfg.DSB), lambda j, k: (j, k)),
              o_spec=pl.BlockSpec((S, tn), lambda j, k: (0, j)),
              out_shape=jax.ShapeDtypeStruct((S, D), F32), acc_shape=(S, tn), dep=token)
    grad_x, dg1 = first_bwd(cfg, dh1, x, g1, dx1)
    comm.reduce_wait(("w_in",), after=grad_x)
    small = dict(loss=loss, g1=dg1, g2=dg2, g3=dg3, g4=dg4, g_sb=dg_sb, g_dl=dg_dl,
                 conv_b=dconv_b.reshape(1, cfg.FF2P), conv_w=dconv_w.transpose(1, 0, 2).reshape(3, cfg.FF2P))
    return grad_x, small


ANY = pl.BlockSpec(memory_space=pl.ANY)


def _me():
    return lax.axis_index("x"), lax.axis_index("y"), lax.axis_index("c")


def _other_chips(x, y):
    return [(1 - x, y), (x, 1 - y), (1 - x, 1 - y)]


def pad_conv_w(cfg, conv_w):
    r, c = conv_w.shape

    def body(w_ref, o_ref):
        o_ref[:, :c] = w_ref[...]
        o_ref[:, c:] = jnp.zeros((r, cfg.FSHP - c), F32)

    return pl.pallas_call(body, name="pad_conv_w", out_shape=jax.ShapeDtypeStruct((r, cfg.FSHP), F32))(conv_w)


def _tile2(r, c):
    return (256, c) if r % 256 == 0 else (r, 512 if c % 512 == 0 else c)


def cast_into(cfg, name, w, pos):
    r, c = w.shape
    _, nr, _, nc = _slab(cfg, name, 0)
    tm, tc = _tile2(r, c)
    wr = nr if tm == r else tm
    assert nc == c and (nr == r or tm == r)

    def body(pos_ref, w_ref, full_ref, scr, sem):
        scr[pl.ds(0, tm), :] = w_ref[...].astype(BF16)
        if wr > tm:
            scr[pl.ds(tm, wr - tm), :] = jnp.zeros((wr - tm, tc), BF16)
        r0, _, c0, _ = _slab(cfg, name, pos_ref[0])
        rows = pl.ds(pl.multiple_of(r0 + pl.program_id(0) * tm, 16), wr)
        cols = pl.ds(pl.multiple_of(c0 + pl.program_id(1) * tc, LANE), tc)
        cp = pltpu.make_async_copy(scr, full_ref.at[rows, cols], sem)
        cp.start()
        cp.wait()

    return pl.pallas_call(
        body, name=f"cast_{name}",
        grid_spec=pltpu.PrefetchScalarGridSpec(
            num_scalar_prefetch=1, grid=(r // tm, c // tc),
            in_specs=[pl.BlockSpec((tm, tc), lambda i, j, p: (i, j))], out_specs=ANY,
            scratch_shapes=[pltpu.VMEM((wr, tc), BF16), pltpu.SemaphoreType.DMA]),
        out_shape=jax.ShapeDtypeStruct(_full_shape(cfg, name), BF16),
        compiler_params=_cp(cfg, ("arbitrary", "arbitrary")),
    )(pos, w)


HBM = pl.BlockSpec(memory_space=pltpu.HBM)
SEM = pl.BlockSpec(memory_space=pltpu.SEMAPHORE)
TOKEN = pl.BlockSpec(memory_space=pltpu.VMEM)
EFFECT = pltpu.SideEffectType.DATAFLOW_SIDE_EFFECTING


def _slab(cfg, name, k):
    D = cfg.D
    if name == "w_in":
        cin = 6 * cfg.DSB // N_CHIPS
        return 0, D, k * cin, cin
    if name == "w_out":
        rout = cfg.DMIX // N_CHIPS
        return k * rout, rout, 0, D
    if name == "w_up":
        return k * cfg.FSHP, cfg.FSHP, 0, D
    rdn = cfg.FSH // 2
    return (k // 2) * cfg.FSHP + (k % 2) * rdn, rdn, 0, D


def _full_shape(cfg, name):
    return dict(w_in=(cfg.D, 6 * cfg.DSB), w_out=(cfg.DMIX, cfg.D), w_up=(cfg.FF2P, cfg.D), w_down=(cfg.FFP, cfg.D))[name]


def _half(cfg, name, ref, k, h):
    r0, nr, c0, nc = _slab(cfg, name, k)
    return ref.at[pl.ds(r0 + h * (nr // 2), nr // 2), pl.ds(c0, nc)]


def _rows_half(ref, h):
    nr = ref.shape[0] // 2
    return ref.at[pl.ds(h * nr, nr), :]


def _remote(src, dst, send_sem, recv_sem, dev):
    return pltpu.make_async_remote_copy(src_ref=src, dst_ref=dst, send_sem=send_sem, recv_sem=recv_sem,
                                        device_id=dev, device_id_type=MESH)


def gather_first(cfg, g_in, sh_cw):
    def body(in_ref, cw_ref, g_in, g_cw, token, ssem, rsem, fssem, frsem, lsem):
        x, y, c = _me()
        me, sib = 2 * x + y, (x, y, 1 - c)
        cw_slot = lambda k: g_cw.at[:, pl.ds(k * cfg.FSHP, cfg.FSHP)]
        local = [pltpu.make_async_copy(cw_ref, cw_slot(me), lsem.at[0])]
        sends = []
        for j, (px, py) in enumerate(_other_chips(x, y)):
            mine = _half(cfg, "w_in", g_in, me, c)
            sends.append(_remote(mine, mine, ssem.at[j], rsem.at[j], (px, py, c)))
            sends.append(_remote(cw_ref, cw_slot(me), ssem.at[3 + j], rsem.at[3 + j], (px, py, c)))
        for cp in local + sends:
            cp.start()
        for j, (px, py) in enumerate(_other_chips(x, y)):
            k = 2 * px + py
            landed = _half(cfg, "w_in", g_in, k, c)
            _remote(landed, landed, ssem.at[j], rsem.at[j], (px, py, c)).wait_recv()
            fwd = _remote(landed, landed, fssem.at[j], frsem.at[j], sib)
            fwd.start()
            sends.append(fwd)
        for j, (px, py) in enumerate(_other_chips(x, y)):
            k = 2 * px + py
            passed = _half(cfg, "w_in", g_in, k, 1 - c)
            _remote(passed, passed, fssem.at[j], frsem.at[j], sib).wait_recv()
            _remote(cw_ref, cw_slot(k), ssem.at[3 + j], rsem.at[3 + j], (px, py, c)).wait_recv()
        for cp in sends:
            cp.wait_send()
        for cp in local:
            cp.wait()
        token[...] = jnp.zeros_like(token)

    return pl.pallas_call(
        body, name="gather_first", in_specs=[ANY, ANY], out_specs=[ANY, ANY, TOKEN],
        out_shape=[jax.ShapeDtypeStruct(_full_shape(cfg, "w_in"), BF16), jax.ShapeDtypeStruct((3, cfg.FF2P), F32),
                   jax.ShapeDtypeStruct((8, LANE), F32)],
        input_output_aliases={0: 0},
        scratch_shapes=[pltpu.SemaphoreType.DMA((6,)), pltpu.SemaphoreType.DMA((6,)), pltpu.SemaphoreType.DMA((3,)),
                        pltpu.SemaphoreType.DMA((3,)), pltpu.SemaphoreType.DMA((1,))],
    )(g_in, sh_cw)


REST = ("w_out", "w_up", "w_down")


def _hbm(a):
    return pltpu.with_memory_space_constraint(a, pltpu.HBM)


GROUPS = (("w_out", "w_up"), ("w_down",))


def gather_start(cfg, fulls, after):
    n, ng = len(REST), len(GROUPS)

    def body(*refs):
        lands = dict(zip(REST, refs[:n]))
        sems = refs[n + 1:n + 1 + 2 * ng]
        token = refs[-1]
        x, y, c = _me()
        me = 2 * x + y
        for g, names in enumerate(GROUPS):
            for i, name in enumerate(names):
                mine = _half(cfg, name, lands[name], me, c)
                for j, (px, py) in enumerate(_other_chips(x, y)):
                    _remote(mine, mine, sems[2 * g].at[3 * i + j], sems[2 * g + 1].at[3 * i + j], (px, py, c)).start()
        token[...] = jnp.zeros_like(token)

    ops = [_hbm(fulls[k]) for k in REST]
    sem_shapes = [pltpu.SemaphoreType.DMA((3 * len(names),)) for names in GROUPS for _ in range(2)]
    outs = pl.pallas_call(
        body, name="gather_start",
        in_specs=[HBM] * n + [ANY],
        out_specs=[SEM] * (2 * ng) + [HBM] * n + [TOKEN],
        out_shape=sem_shapes + [pltpu.HBM(a.shape, a.dtype) for a in ops] + [jax.ShapeDtypeStruct((8, LANE), F32)],
        input_output_aliases={i: 2 * ng + i for i in range(n)},
        compiler_params=pltpu.CompilerParams(has_side_effects=EFFECT),
    )(*ops, after)
    thru = dict(zip(REST, outs[2 * ng:2 * ng + n]))
    groups = [(outs[2 * g], outs[2 * g + 1], [thru[k] for k in names]) for g, names in enumerate(GROUPS)]
    return groups, outs[-1]


def gather_wait(cfg, names, ssem, rsem, lands, after):
    n = len(names)

    def body(*refs):
        lands_ = refs[:n]
        ssem_, rsem_ = refs[n], refs[n + 1]
        x, y, c = _me()
        me = 2 * x + y
        for i, name in enumerate(names):
            for j, (px, py) in enumerate(_other_chips(x, y)):
                cp = _remote(_half(cfg, name, lands_[i], me, c), _half(cfg, name, lands_[i], 2 * px + py, c),
                             ssem_.at[3 * i + j], rsem_.at[3 * i + j], (px, py, c))
                cp.wait_send()
                cp.wait_recv()

    return pl.pallas_call(
        body, name="gather_wait_" + "_".join(names),
        in_specs=[HBM] * n + [SEM, SEM, ANY], out_specs=[HBM] * n,
        out_shape=[pltpu.HBM(a.shape, a.dtype) for a in lands],
        input_output_aliases={i: i for i in range(n)},
        compiler_params=pltpu.CompilerParams(has_side_effects=EFFECT),
    )(*lands, ssem, rsem, after)


def gather_finish(cfg, names, lands):
    n = len(names)
    rdn = cfg.FSH // 2
    zpad = jnp.zeros((cfg.FSHP - cfg.FSH, cfg.D), BF16)

    def body(*refs):
        z_ref, outs = refs[0], refs[n + 1:2 * n + 1]
        ssem, rsem, lsem = refs[2 * n + 1:]
        x, y, c = _me()
        sib = (x, y, 1 - c)
        local = []
        if "w_down" in names:
            dn = outs[names.index("w_down")]
            local = [pltpu.make_async_copy(z_ref, dn.at[pl.ds(h * cfg.FSHP + 2 * rdn, cfg.FSHP - cfg.FSH), :],
                                           lsem.at[h]) for h in range(2)]
        fwds = []
        for i, name in enumerate(names):
            for j, (px, py) in enumerate(_other_chips(x, y)):
                landed = _half(cfg, name, outs[i], 2 * px + py, c)
                fwds.append(_remote(landed, landed, ssem.at[3 * i + j], rsem.at[3 * i + j], sib))
        for cp in local + fwds:
            cp.start()
        for i, name in enumerate(names):
            for j, (px, py) in enumerate(_other_chips(x, y)):
                passed = _half(cfg, name, outs[i], 2 * px + py, 1 - c)
                _remote(passed, passed, ssem.at[3 * i + j], rsem.at[3 * i + j], sib).wait_recv()
        for cp in fwds:
            cp.wait_send()
        for cp in local:
            cp.wait()

    return pl.pallas_call(
        body, name="gather_finish_" + "_".join(names), in_specs=[ANY] * (n + 1), out_specs=[ANY] * n,
        out_shape=[jax.ShapeDtypeStruct(a.shape, a.dtype) for a in lands],
        input_output_aliases={1 + i: i for i in range(n)},
        scratch_shapes=[pltpu.SemaphoreType.DMA((3 * n,)), pltpu.SemaphoreType.DMA((3 * n,)),
                        pltpu.SemaphoreType.DMA((2,))],
    )(zpad, *lands)


def pair_send(cfg, grads):
    names = list(grads)
    n = len(names)

    def half_shape(name):
        _, nr, _, nc = _slab(cfg, name, 0)
        return (N_CHIPS, nr // 2, nc)

    def body(*refs):
        srcs, theirs = refs[:n], refs[n:2 * n]
        ssem, rsem = refs[2 * n:]
        x, y, c = _me()
        cps = []
        for i, name in enumerate(names):
            for k in range(N_CHIPS):
                cps.append(_remote(_half(cfg, name, srcs[i], k, 1 - c), theirs[i].at[k],
                                   ssem.at[N_CHIPS * i + k], rsem.at[N_CHIPS * i + k], (x, y, 1 - c)))
        for cp in cps:
            cp.start()
        for cp in cps:
            cp.wait()

    outs = pl.pallas_call(
        body, name="pair_send_" + "_".join(names), in_specs=[ANY] * n, out_specs=[ANY] * n,
        out_shape=[jax.ShapeDtypeStruct(half_shape(name), BF16) for name in names],
        scratch_shapes=[pltpu.SemaphoreType.DMA((N_CHIPS * n,))] * 2,
    )(*[grads[k] for k in names])
    return dict(zip(names, outs))


def pair_start(cfg, grads, after):
    names = list(grads)
    n = len(names)

    def body(*refs):
        srcs, theirs = refs[:n], refs[n:2 * n]
        ssem, rsem = refs[2 * n + 1], refs[2 * n + 2]
        token = refs[-1]
        x, y, c = _me()
        for i, name in enumerate(names):
            for k in range(N_CHIPS):
                _remote(_half(cfg, name, srcs[i], k, 1 - c), theirs[i].at[k],
                        ssem.at[N_CHIPS * i + k], rsem.at[N_CHIPS * i + k], (x, y, 1 - c)).start()
        token[...] = jnp.zeros_like(token)

    def half_shape(name):
        _, nr, _, nc = _slab(cfg, name, 0)
        return (N_CHIPS, nr // 2, nc)

    ops = [_hbm(grads[k]) for k in names] + [_hbm(lax.empty(half_shape(k), BF16)) for k in names]
    outs = pl.pallas_call(
        body, name="pair_start_" + "_".join(names),
        in_specs=[HBM] * (2 * n) + [ANY],
        out_specs=[SEM, SEM] + [HBM] * (2 * n) + [TOKEN],
        out_shape=[pltpu.SemaphoreType.DMA((N_CHIPS * n,)), pltpu.SemaphoreType.DMA((N_CHIPS * n,))]
        + [pltpu.HBM(a.shape, a.dtype) for a in ops] + [jax.ShapeDtypeStruct((8, LANE), F32)],
        input_output_aliases={i: 2 + i for i in range(2 * n)},
        compiler_params=pltpu.CompilerParams(has_side_effects=EFFECT),
    )(*ops, after)
    return outs[0], outs[1], dict(zip(names, outs[2:2 + n])), dict(zip(names, outs[2 + n:2 + 2 * n])), outs[-1]


def pair_wait(cfg, ssem, rsem, grads, theirs, after):
    names = list(grads)
    n = len(names)

    def body(*refs):
        srcs, theirs_ = refs[:n], refs[n:2 * n]
        ssem_, rsem_ = refs[2 * n], refs[2 * n + 1]
        x, y, c = _me()
        for i, name in enumerate(names):
            for k in range(N_CHIPS):
                cp = _remote(_half(cfg, name, srcs[i], k, 1 - c), theirs_[i].at[k],
                             ssem_.at[N_CHIPS * i + k], rsem_.at[N_CHIPS * i + k], (x, y, 1 - c))
                cp.wait_send()
                cp.wait_recv()

    ops = [grads[k] for k in names] + [theirs[k] for k in names]
    outs = pl.pallas_call(
        body, name="pair_wait_" + "_".join(names),
        in_specs=[HBM] * (2 * n) + [SEM, SEM, ANY], out_specs=[HBM] * (2 * n),
        out_shape=[pltpu.HBM(a.shape, a.dtype) for a in ops],
        input_output_aliases={i: i for i in range(2 * n)},
        compiler_params=pltpu.CompilerParams(has_side_effects=EFFECT),
    )(*ops, ssem, rsem, after)
    return dict(zip(names, outs[:n])), dict(zip(names, outs[n:]))


def pair_sum(cfg, name, grad, theirs, pos):
    _, r, c = theirs.shape
    tm, tc = _tile2(r, c)

    ni, nj = r // tm, c // tc
    total = N_CHIPS * ni * nj

    def body(pos_ref, g_ref, t_ref, o_ref, scr, sem):
        step = (pl.program_id(0) * ni + pl.program_id(1)) * nj + pl.program_id(2)

        def fetch(flat, slot):
            k, rem = flat // (ni * nj), flat % (ni * nj)
            r0, nr, c0, _ = _slab(cfg, name, k)
            rows = pl.ds(pl.multiple_of(r0 + pos_ref[1] * (nr // 2) + (rem // nj) * tm, 16), tm)
            cols = pl.ds(pl.multiple_of(c0 + (rem % nj) * tc, LANE), tc)
            return pltpu.make_async_copy(g_ref.at[rows, cols], scr.at[slot], sem.at[slot])

        @pl.when(step == 0)
        def _():
            fetch(0, 0).start()

        @pl.when(step + 1 < total)
        def _():
            fetch(step + 1, (step + 1) % 2).start()

        fetch(step, step % 2).wait()
        o_ref[...] = (scr[step % 2].astype(F32) + t_ref[...].astype(F32)).astype(BF16)

    blk = pl.BlockSpec((None, tm, tc), lambda k, i, j, p: (k, i, j))
    return pl.pallas_call(
        body, name=f"pair_sum_{name}",
        grid_spec=pltpu.PrefetchScalarGridSpec(
            num_scalar_prefetch=1, grid=(N_CHIPS, ni, nj), in_specs=[ANY, blk], out_specs=blk,
            scratch_shapes=[pltpu.VMEM((2, tm, tc), BF16), pltpu.SemaphoreType.DMA((2,))]),
        out_shape=jax.ShapeDtypeStruct(theirs.shape, BF16),
        compiler_params=_cp(cfg, ("arbitrary",) * 3),
    )(pos, grad, theirs)


def scatter_start(cfg, pres, after):
    names = list(pres)
    n = len(names)

    def body(*refs):
        srcs, lands = refs[:n], refs[n:2 * n]
        ssem, rsem = refs[2 * n + 1], refs[2 * n + 2]
        token = refs[-1]
        x, y, c = _me()
        for i in range(n):
            for j, (px, py) in enumerate(_other_chips(x, y)):
                _remote(srcs[i].at[2 * px + py], lands[i].at[j], ssem.at[3 * i + j], rsem.at[3 * i + j], (px, py, c)).start()
        token[...] = jnp.zeros_like(token)

    lands = [lax.empty((3,) + pres[k].shape[1:], BF16) for k in names]
    ops = [_hbm(a) for a in [pres[k] for k in names] + lands]
    outs = pl.pallas_call(
        body, name="scatter_start_" + "_".join(names),
        in_specs=[HBM] * (2 * n) + [ANY],
        out_specs=[SEM, SEM] + [HBM] * (2 * n) + [TOKEN],
        out_shape=[pltpu.SemaphoreType.DMA((3 * n,)), pltpu.SemaphoreType.DMA((3 * n,))]
        + [pltpu.HBM(a.shape, a.dtype) for a in ops] + [jax.ShapeDtypeStruct((8, LANE), F32)],
        input_output_aliases={i: 2 + i for i in range(2 * n)},
        compiler_params=pltpu.CompilerParams(has_side_effects=EFFECT),
    )(*ops, after)
    return outs[0], outs[1], dict(zip(names, outs[2:2 + n])), dict(zip(names, outs[2 + n:2 + 2 * n])), outs[-1]


def scatter_wait(cfg, ssem, rsem, pres, lands, after):
    names = list(pres)
    n = len(names)

    def body(*refs):
        srcs, lands_ = refs[:n], refs[n:2 * n]
        ssem_, rsem_ = refs[2 * n], refs[2 * n + 1]
        x, y, c = _me()
        for i in range(n):
            for j, (px, py) in enumerate(_other_chips(x, y)):
                cp = _remote(srcs[i].at[2 * px + py], lands_[i].at[j], ssem_.at[3 * i + j], rsem_.at[3 * i + j], (px, py, c))
                cp.wait_send()
                cp.wait_recv()

    ops = [pres[k] for k in names] + [lands[k] for k in names]
    outs = pl.pallas_call(
        body, name="scatter_wait_" + "_".join(names),
        in_specs=[HBM] * (2 * n) + [SEM, SEM, ANY], out_specs=[HBM] * (2 * n),
        out_shape=[pltpu.HBM(a.shape, a.dtype) for a in ops],
        input_output_aliases={i: i for i in range(2 * n)},
        compiler_params=pltpu.CompilerParams(has_side_effects=EFFECT),
    )(*ops, ssem, rsem, after)
    return dict(zip(names, outs[:n])), dict(zip(names, outs[n:]))


def sum_landed(cfg, name, pre, land, pos):
    _, r, c = pre.shape
    tm, tc = _tile2(r, c)
    nrt = r // tm

    def body(pos_ref, p_ref, l_ref, o_ref):
        acc = p_ref[...].astype(F32)
        for j in range(3):
            acc = acc + l_ref[j].astype(F32)
        o_ref[...] = acc

    return pl.pallas_call(
        body, name=f"sum_landed_{name}",
        grid_spec=pltpu.PrefetchScalarGridSpec(
            num_scalar_prefetch=1, grid=(nrt, c // tc),
            in_specs=[pl.BlockSpec((None, tm, tc), lambda i, j, p: (p[0], i, j)),
                      pl.BlockSpec((3, tm, tc), lambda i, j, p: (0, i, j))],
            out_specs=pl.BlockSpec((tm, tc), lambda i, j, p: (p[1] * nrt + i, j))),
        out_shape=jax.ShapeDtypeStruct((2 * r, c), F32), compiler_params=_cp(cfg, ("parallel", "parallel")),
    )(pos, pre, land)


def half_swap(cfg, sums):
    names = list(sums)
    n = len(names)

    def body(*refs):
        outs = refs[n:2 * n]
        ssem, rsem = refs[2 * n:]
        x, y, c = _me()
        cps = [_remote(_rows_half(outs[i], c), _rows_half(outs[i], c), ssem.at[i], rsem.at[i], (x, y, 1 - c))
               for i in range(n)]
        for cp in cps:
            cp.start()
        for i in range(n):
            theirs = _rows_half(outs[i], 1 - c)
            _remote(theirs, theirs, ssem.at[i], rsem.at[i], (x, y, 1 - c)).wait_recv()
        for cp in cps:
            cp.wait_send()

    outs = pl.pallas_call(
        body, name="half_swap_" + "_".join(names), in_specs=[ANY] * n, out_specs=[ANY] * n,
        out_shape=[jax.ShapeDtypeStruct(sums[k].shape, F32) for k in names],
        input_output_aliases={i: i for i in range(n)},
        scratch_shapes=[pltpu.SemaphoreType.DMA((n,))] * 2,
    )(*[sums[k] for k in names])
    return dict(zip(names, outs))


class MeshWeights:
    def __init__(self, cfg, w_sh):
        self.cfg = cfg
        self.pos = jnp.stack([2 * lax.axis_index("x") + lax.axis_index("y"), lax.axis_index("c")]).astype(jnp.int32)
        self.full = {k: cast_into(cfg, k, w_sh[k], self.pos) for k in ("w_in",) + REST}
        self.conv_w = pad_conv_w(cfg, w_sh["conv_w"])
        self.inflight = {}
        self.grads = {}

    def weights_first(self):
        w_in, conv_w, self.token = gather_first(self.cfg, self.full["w_in"], self.conv_w)
        return w_in, conv_w

    def start_rest(self):
        self.rest, token = gather_start(self.cfg, self.full, self.token)
        return token

    def weights_rest(self, group, after):
        names = GROUPS[group]
        ssem, rsem, lands = self.rest[group]
        return gather_finish(self.cfg, names, gather_wait(self.cfg, names, ssem, rsem, lands, after))

    def pair_start(self, grads):
        out = pair_start(self.cfg, grads, jnp.zeros((8, LANE), F32))
        self.pairs = out[:4]
        return out[4]

    def reduce_start(self, grads, after=None):
        theirs = pair_send(self.cfg, grads)
        if after is not None:
            early, early_theirs = pair_wait(self.cfg, *self.pairs, after)
            grads, theirs = {**early, **grads}, {**early_theirs, **theirs}
        pres = {k: pair_sum(self.cfg, k, grads[k], theirs[k], self.pos) for k in grads}
        out = scatter_start(self.cfg, pres, jnp.zeros((8, LANE), F32))
        self.inflight[tuple(sorted(grads))] = out[:4]
        return out[4]

    def reduce_wait(self, names, after):
        cfg = self.cfg
        pres, lands = scatter_wait(cfg, *self.inflight.pop(tuple(sorted(names))), after)
        sums = {k: sum_landed(cfg, k, pres[k], lands[k], self.pos) for k in names}
        self.grads.update(half_swap(cfg, sums))


def allreduce_small(cfg, vec):
    R = vec.shape[0]

    def body(v_ref, o_ref, buf, send_sems, recv_sems):
        x, y, c = _me()
        me = 4 * x + 2 * y + c
        buf[me] = v_ref[...]
        sends = []
        for k in range(1, N_DEV):
            px, py, pc = x ^ (k >> 2), y ^ ((k >> 1) & 1), c ^ (k & 1)
            sends.append(pltpu.make_async_remote_copy(
                src_ref=v_ref, dst_ref=buf.at[me], send_sem=send_sems.at[k], recv_sem=recv_sems.at[k],
                device_id=(px, py, pc), device_id_type=MESH))
        for cp in sends:
            cp.start()
        for k in range(1, N_DEV):
            px, py, pc = x ^ (k >> 2), y ^ ((k >> 1) & 1), c ^ (k & 1)
            pltpu.make_async_remote_copy(
                src_ref=v_ref, dst_ref=buf.at[4 * px + 2 * py + pc], send_sem=send_sems.at[k],
                recv_sem=recv_sems.at[k], device_id=(px, py, pc), device_id_type=MESH).wait_recv()
        for cp in sends:
            cp.wait_send()
        acc = buf[0]
        for j in range(1, N_DEV):
            acc = acc + buf[j]
        o_ref[...] = acc

    return pl.pallas_call(
        body, name="allreduce_small",
        in_specs=[pl.BlockSpec(memory_space=pltpu.VMEM)], out_specs=pl.BlockSpec(memory_space=pltpu.VMEM),
        out_shape=jax.ShapeDtypeStruct((R, LANE), F32),
        scratch_shapes=[pltpu.VMEM((N_DEV, R, LANE), F32), pltpu.SemaphoreType.DMA((N_DEV,)),
                        pltpu.SemaphoreType.DMA((N_DEV,))],
    )(vec)


def adamw(cfg, name, w, m, v, g_parts, tile):
    r, c = w.shape
    tm, tc = tile[0] or r, tile[1] or c
    assert tc == c or all(g.shape[1] == c for g in g_parts)
    n = len(g_parts)
    bc1 = 1.0 - ADAM_B1 ** ADAM_STEP
    bc2 = 1.0 - ADAM_B2 ** ADAM_STEP

    def body(*refs):
        w_ref, m_ref, v_ref = refs[:3]
        g_refs = refs[3:3 + n]
        g_out, d_out, m_out, v_out = refs[3 + n:]
        g = g_refs[0][:, :tc]
        for gr in g_refs[1:]:
            g = g + gr[:, :tc]
        m_new = ADAM_B1 * m_ref[...] + (1.0 - ADAM_B1) * g
        v_new = ADAM_B2 * v_ref[...] + (1.0 - ADAM_B2) * jnp.square(g)
        m_hat = m_new / bc1
        v_hat = v_new / bc2
        g_out[...] = g
        d_out[...] = -ADAM_LR * (m_hat / (jnp.sqrt(v_hat) + ADAM_EPS) + ADAM_WD * w_ref[...])
        m_out[...] = m_new
        v_out[...] = v_new

    blk = pl.BlockSpec((tm, tc), lambda i, j: (i, j))
    return pl.pallas_call(
        body, name=f"adamw_{name}", grid=(r // tm, c // tc),
        in_specs=[blk] * 3 + [pl.BlockSpec((tm, tc if tc < c else g.shape[1]), lambda i, j: (i, j)) for g in g_parts],
        out_specs=[blk] * 4, out_shape=[jax.ShapeDtypeStruct((r, c), F32)] * 4,
        compiler_params=_cp(cfg, ("parallel", "parallel")),
    )(w, m, v, *g_parts)


SMALL_ORDER = ("loss", "g1", "g2", "g3", "g4", "g_sb", "g_dl", "conv_b", "conv_w")


def pack_small(small):
    rows = []
    for k in SMALL_ORDER:
        a = small[k].reshape(-1, LANE)
        rows.append(a)
    flat = jnp.concatenate(rows, axis=0)
    pad = (-flat.shape[0]) % 8
    return jnp.pad(flat, ((0, pad), (0, 0))), [r.shape[0] for r in rows]


def unpack_small(red, small, counts):
    out, at = {}, 0
    for k, n in zip(SMALL_ORDER, counts):
        out[k] = red[at:at + n].reshape(small[k].shape)
        at += n
    return out


def pad_ff(cfg, a):
    r = a.shape[0]
    return jnp.pad(a.reshape(r, N_CHIPS, cfg.FSH), ((0, 0), (0, 0), (0, cfg.FSHP - cfg.FSH))).reshape(r, cfg.FF2P)


def step(cfg, x, target, gains, w_sh, conv_b, m_all, v_all):
    chip = 2 * lax.axis_index("x") + lax.axis_index("y")
    comm = MeshWeights(cfg, w_sh)
    grad_x, small = local_step(cfg, comm, x, target, gains["g1"], gains["g2"], gains["g3"], gains["g4"],
                               gains["g_sb"], gains["g_dl"], pad_ff(cfg, conv_b))

    packed, counts = pack_small(small)
    red = unpack_small(allreduce_small(cfg, packed), small, counts)

    names = ("w_in", "w_out", "w_up", "w_down")
    up_rows = max(t for t in range(SUB, 513, SUB) if cfg.FSH % t == 0)
    tms = dict(w_in=(cfg.TM, None), w_out=(cfg.TM, None), w_up=(up_rows, None), w_down=(None, cfg.TN // 2))
    res = {}
    for n in names:
        res[n] = adamw(cfg, n, w_sh[n], m_all[n], v_all[n], [comm.grads[n]], tms[n])
    g_cw = lax.dynamic_slice_in_dim(red["conv_w"].reshape(3, N_CHIPS, cfg.FSHP), chip, 1, axis=1)[:, 0, :cfg.FSH]
    res["conv_w"] = adamw(cfg, "conv_w", w_sh["conv_w"], m_all["conv_w"], v_all["conv_w"], [g_cw], (None, None))
    g_cb = red["conv_b"].reshape(1, N_CHIPS, cfg.FSHP)[:, :, :cfg.FSH].reshape(1, N_CHIPS * cfg.FSH)
    res["conv_b"] = adamw(cfg, "conv_b", conv_b, m_all["conv_b"], v_all["conv_b"], [g_cb], (None, None))
    for k in ("g1", "g2", "g3", "g4", "g_sb", "g_dl"):
        res[k] = adamw(cfg, k, gains[k], m_all[k], v_all[k], [red[k]], (None, None))
    return red["loss"][0, 0], grad_x, res


PARAMS = ("pre_mix_gain", "post_mix_gain", "pre_ffn_gain", "post_ffn_gain", "w_in", "sb_out_gain", "dil_out_gain",
          "w_out", "w_up", "conv_w", "conv_b", "w_down")
SHORT = dict(pre_mix_gain="g1", post_mix_gain="g2", pre_ffn_gain="g3", post_ffn_gain="g4", sb_out_gain="g_sb",
             dil_out_gain="g_dl", w_in="w_in", w_out="w_out", w_up="w_up", conv_w="conv_w", conv_b="conv_b",
             w_down="w_down")


def kernel(x, pre_mix_gain, post_mix_gain, pre_ffn_gain, post_ffn_gain, w_in, sb_out_gain, dil_out_gain, w_out, w_up, conv_w, conv_b, w_down, loss_target, m_pre_mix_gain, m_post_mix_gain, m_pre_ffn_gain, m_post_ffn_gain, m_w_in, m_sb_out_gain, m_dil_out_gain, m_w_out, m_w_up, m_conv_w, m_conv_b, m_w_down, v_pre_mix_gain, v_post_mix_gain, v_pre_ffn_gain, v_post_ffn_gain, v_w_in, v_sb_out_gain, v_dil_out_gain, v_w_out, v_w_up, v_conv_w, v_conv_b, v_w_down):
    cfg = CFG
    w = dict(zip(PARAMS, (pre_mix_gain, post_mix_gain, pre_ffn_gain, post_ffn_gain, w_in, sb_out_gain, dil_out_gain,
                          w_out, w_up, conv_w, conv_b, w_down)))
    m = dict(zip(PARAMS, (m_pre_mix_gain, m_post_mix_gain, m_pre_ffn_gain, m_post_ffn_gain, m_w_in, m_sb_out_gain,
                          m_dil_out_gain, m_w_out, m_w_up, m_conv_w, m_conv_b, m_w_down)))
    v = dict(zip(PARAMS, (v_pre_mix_gain, v_post_mix_gain, v_pre_ffn_gain, v_post_ffn_gain, v_w_in, v_sb_out_gain,
                          v_dil_out_gain, v_w_out, v_w_up, v_conv_w, v_conv_b, v_w_down)))
    sq = lambda a: a.reshape(a.shape[1:])
    ws = {SHORT[k]: sq(a) if a.ndim == 3 else a for k, a in w.items()}
    ms = {SHORT[k]: sq(a) if a.ndim == 3 else a for k, a in m.items()}
    vs = {SHORT[k]: sq(a) if a.ndim == 3 else a for k, a in v.items()}
    for d in (ws, ms, vs):
        d["w_up"] = d["w_up"].T
    gains = {k: ws[k] for k in ("g1", "g2", "g3", "g4", "g_sb", "g_dl")}
    w_sh = {k: ws[k] for k in ("w_in", "w_out", "w_up", "conv_w", "w_down")}
    loss, grad_x, res = step(cfg, sq(x), sq(loss_target), gains, w_sh, ws["conv_b"], ms, vs)
    res["w_up"] = [a.T for a in res["w_up"]]
    outs = [loss, grad_x.reshape(x.shape)]
    for i in range(4):
        for k in PARAMS:
            outs.append(res[SHORT[k]][i].reshape(w[k].shape))
    return tuple(outs)
```

```python
import functools
import math
from typing import NamedTuple

import jax
import jax.numpy as jnp
from jax import lax
from jax.experimental import pallas as pl
from jax.experimental.pallas import tpu as pltpu

F32 = jnp.float32
BF16 = jnp.bfloat16
MESH = pl.DeviceIdType.MESH

ROPE_THETA = 10000.0
RMS_EPS = 1e-6
ADAM_LR = 0.001
ADAM_B1 = 0.9
ADAM_B2 = 0.999
ADAM_EPS = 1e-08
ADAM_WD = 0.01
ADAM_STEP = 10
GELU_C = math.sqrt(2.0 / math.pi)
NEG_BIG = -1e30
LANE = 128
N_CHIPS = 4
N_DEV = 8


class Cfg(NamedTuple):
    S: int = 2048
    D: int = 2048
    DH: int = 128
    HSB: int = 8
    HDL: int = 8
    QB: int = 128
    SBT: int = 256
    SBH: int = 4
    SBHB: int = 4
    branches: tuple = ((128, 1), (512, 4), (2048, 16))
    FSH: int = 2752
    FSHP: int = 2816
    TM: int = 256
    TNF: int = 256
    FCH: int = 512
    TN: int = 512
    VMEM_MB: int = 56

    @property
    def DSB(self):
        return self.HSB * self.DH

    @property
    def DDL(self):
        return self.HDL * self.DH

    @property
    def DMIX(self):
        return self.DSB + self.DDL

    @property
    def FFP(self):
        return 2 * self.FSHP

    @property
    def FF2P(self):
        return 4 * self.FSHP


CFG = Cfg()


def _cp(cfg, sem=None):
    return pltpu.CompilerParams(dimension_semantics=sem, vmem_limit_bytes=cfg.VMEM_MB * 2**20)


def _dot(a, b):
    return jnp.dot(a, b, preferred_element_type=F32)


def _dot_nt(a, b):
    return lax.dot_general(a, b, (((1,), (1,)), ((), ())), preferred_element_type=F32)


def _dot_tn(a, b):
    return lax.dot_general(a, b, (((0,), (0,)), ((), ())), preferred_element_type=F32)


def _dot_split(x, u):
    hi = x.astype(BF16)
    lo = (x - hi.astype(F32)).astype(BF16)
    return _dot(hi, u) + _dot(lo, u)


def _rstd(x):
    return lax.rsqrt(jnp.mean(x * x, axis=-1, keepdims=True) + RMS_EPS)


def _rms_bwd(dy, x, g):
    r = _rstd(x)
    xh = x * r
    dxh = dy * g
    dx = r * (dxh - xh * jnp.mean(dxh * xh, axis=-1, keepdims=True))
    return dx, dy * xh


def _gelu(x):
    t = jnp.tanh(GELU_C * (x + 0.044715 * (x * x * x)))
    return 0.5 * x * (1.0 + t), t


def _gelu_grad(x, t):
    return 0.5 * (1.0 + t) + 0.5 * x * (1.0 - t * t) * (GELU_C * (1.0 + 3 * 0.044715 * (x * x)))


def _row(cfg, w):
    return pl.BlockSpec((cfg.TM, w), lambda i: (i, 0))


def _vec(w):
    return pl.BlockSpec((1, w), lambda i: (0, 0))


def rms_fwd(cfg, x, g):
    S, D = x.shape

    def body(x_ref, g_ref, h_ref):
        xv = x_ref[...]
        h_ref[...] = (xv * _rstd(xv) * g_ref[...]).astype(BF16)

    return pl.pallas_call(
        body, name="rms_fwd", grid=(S // cfg.TM,),
        in_specs=[_row(cfg, D), _vec(D)], out_specs=_row(cfg, D),
        out_shape=jax.ShapeDtypeStruct((S, D), BF16), compiler_params=_cp(cfg, ("parallel",)),
    )(x, g)


def mid_fwd(cfg, x, mo, g_post, g_pre):
    S, D = x.shape

    def body(x_ref, mo_ref, gp_ref, gn_ref, x1_ref, h2_ref):
        mo_v = mo_ref[...]
        x1 = x_ref[...] + mo_v * _rstd(mo_v) * gp_ref[...]
        x1_ref[...] = x1
        h2_ref[...] = (x1 * _rstd(x1) * gn_ref[...]).astype(BF16)

    return pl.pallas_call(
        body, name="mid_fwd", grid=(S // cfg.TM,),
        in_specs=[_row(cfg, D), _row(cfg, D), _vec(D), _vec(D)],
        out_specs=[_row(cfg, D), _row(cfg, D)],
        out_shape=[jax.ShapeDtypeStruct((S, D), F32), jax.ShapeDtypeStruct((S, D), BF16)],
        compiler_params=_cp(cfg, ("parallel",)),
    )(x, mo, g_post, g_pre)


def final_fwd_bwd(cfg, x1, f, g_post, target):
    S, D = x1.shape

    def body(x1_ref, f_ref, g_ref, t_ref, dout_ref, df_ref, dg_ref, loss_ref):
        @pl.when(pl.program_id(0) == 0)
        def _():
            dg_ref[...] = jnp.zeros_like(dg_ref)
            loss_ref[...] = jnp.zeros_like(loss_ref)

        fv = f_ref[...]
        g = g_ref[...]
        out = x1_ref[...] + fv * _rstd(fv) * g
        err = out - t_ref[...]
        loss_ref[...] += 0.5 * jnp.sum(jnp.mean(err * err, axis=-1, keepdims=True), axis=0, keepdims=True)
        dout = err * (1.0 / D)
        dout_ref[...] = dout
        df, dgx = _rms_bwd(dout, fv, g)
        df_ref[...] = df.astype(BF16)
        dg_ref[...] += jnp.sum(dgx, axis=0, keepdims=True)

    return pl.pallas_call(
        body, name="final_fwd_bwd", grid=(S // cfg.TM,),
        in_specs=[_row(cfg, D), _row(cfg, D), _vec(D), _row(cfg, D)],
        out_specs=[_row(cfg, D), _row(cfg, D), _vec(D), _vec(LANE)],
        out_shape=[jax.ShapeDtypeStruct((S, D), F32), jax.ShapeDtypeStruct((S, D), BF16),
                   jax.ShapeDtypeStruct((1, D), F32), jax.ShapeDtypeStruct((1, LANE), F32)],
        compiler_params=_cp(cfg, ("arbitrary",)),
    )(x1, f, g_post, target)


def mid_bwd(cfg, dh2, x1, g_pre, dout, mo, g_post):
    S, D = x1.shape

    def body(dh_ref, x1_ref, gn_ref, do_ref, mo_ref, gp_ref, dx1_ref, dmo_ref, dgn_ref, dgp_ref):
        @pl.when(pl.program_id(0) == 0)
        def _():
            dgn_ref[...] = jnp.zeros_like(dgn_ref)
            dgp_ref[...] = jnp.zeros_like(dgp_ref)

        dx, dgx = _rms_bwd(dh_ref[...], x1_ref[...], gn_ref[...])
        dx1 = do_ref[...] + dx
        dx1_ref[...] = dx1
        dgn_ref[...] += jnp.sum(dgx, axis=0, keepdims=True)
        dmo, dgy = _rms_bwd(dx1, mo_ref[...], gp_ref[...])
        dmo_ref[...] = dmo.astype(BF16)
        dgp_ref[...] += jnp.sum(dgy, axis=0, keepdims=True)

    return pl.pallas_call(
        body, name="mid_bwd", grid=(S // cfg.TM,),
        in_specs=[_row(cfg, D), _row(cfg, D), _vec(D), _row(cfg, D), _row(cfg, D), _vec(D)],
        out_specs=[_row(cfg, D), _row(cfg, D), _vec(D), _vec(D)],
        out_shape=[jax.ShapeDtypeStruct((S, D), F32), jax.ShapeDtypeStruct((S, D), BF16),
                   jax.ShapeDtypeStruct((1, D), F32), jax.ShapeDtypeStruct((1, D), F32)],
        compiler_params=_cp(cfg, ("arbitrary",)),
    )(dh2, x1, g_pre, dout, mo, g_post)


def first_bwd(cfg, dh1, x, g_pre, dx1):
    S, D = x.shape

    def body(dh_ref, x_ref, g_ref, r_ref, dx_ref, dg_ref):
        @pl.when(pl.program_id(0) == 0)
        def _():
            dg_ref[...] = jnp.zeros_like(dg_ref)

        dx, dgx = _rms_bwd(dh_ref[...], x_ref[...], g_ref[...])
        dx_ref[...] = r_ref[...] + dx
        dg_ref[...] += jnp.sum(dgx, axis=0, keepdims=True)

    return pl.pallas_call(
        body, name="first_bwd", grid=(S // cfg.TM,),
        in_specs=[_row(cfg, D), _row(cfg, D), _vec(D), _row(cfg, D)],
        out_specs=[_row(cfg, D), _vec(D)],
        out_shape=[jax.ShapeDtypeStruct((S, D), F32), jax.ShapeDtypeStruct((1, D), F32)],
        compiler_params=_cp(cfg, ("arbitrary",)),
    )(dh1, x, g_pre, dx1)


def _mm(cfg, name, a, b, *, nt, a_spec, b_spec, o_spec, grid, out_shape, acc_shape, dep=None):
    nk = grid[-1]
    dot = _dot_nt if nt else _dot
    deps = [] if dep is None else [dep]

    def body(a_ref, b_ref, *rest):
        o_ref, acc_ref = rest[-2:]
        k = pl.program_id(len(grid) - 1)
        part = dot(a_ref[...], b_ref[...])
        if deps:
            part = part + rest[0][0:1, 0:1]
        if nk == 1:
            o_ref[...] = part.astype(o_ref.dtype)
            return

        @pl.when(k == 0)
        def _():
            acc_ref[...] = part

        @pl.when(k > 0)
        def _():
            acc_ref[...] += part

        @pl.when(k == nk - 1)
        def _():
            o_ref[...] = acc_ref[...].astype(o_ref.dtype)

    sem = ("parallel",) * (len(grid) - 1) + ("arbitrary",)
    dep_specs = [pl.BlockSpec((8, LANE), lambda *_: (0, 0))] * len(deps)
    return pl.pallas_call(
        body, name=name, grid=grid, in_specs=[a_spec, b_spec] + dep_specs, out_specs=o_spec, out_shape=out_shape,
        scratch_shapes=[pltpu.VMEM(acc_shape, F32)], compiler_params=_cp(cfg, sem),
    )(a, b, *deps)


def _mm_tn(cfg, name, a, b, *, a_spec, b_spec, o_spec, grid, out_shape):
    def body(a_ref, b_ref, o_ref):
        o_ref[...] = _dot_tn(a_ref[...], b_ref[...]).astype(o_ref.dtype)

    return pl.pallas_call(
        body, name=name, grid=grid, in_specs=[a_spec, b_spec], out_specs=o_spec, out_shape=out_shape,
        compiler_params=_cp(cfg, ("parallel",) * len(grid)),
    )(a, b)


def qkv_proj(cfg, h1, w_in, cos2, sin2):
    S, D = h1.shape
    tn = 2 * cfg.DH
    per = cfg.DSB // tn
    assert cfg.DSB == cfg.DDL
    nblk = 6 * per

    def body(a_ref, b_ref, c_ref, s_ref, o_ref):
        j = pl.program_id(0)
        acc = _dot(a_ref[...], b_ref[...])
        rope = jnp.logical_and(j >= 3 * per, j < 5 * per)

        @pl.when(rope)
        def _():
            for c in range(tn // cfg.DH):
                xh = acc[:, c * cfg.DH:(c + 1) * cfg.DH]
                o_ref[:, c * cfg.DH:(c + 1) * cfg.DH] = (
                    xh * c_ref[...] + pltpu.roll(xh, cfg.DH // 2, 1) * s_ref[...]).astype(BF16)

        @pl.when(jnp.logical_not(rope))
        def _():
            o_ref[...] = acc.astype(BF16)

    return pl.pallas_call(
        body, name="qkv_proj", grid=(nblk,),
        in_specs=[pl.BlockSpec((S, D), lambda j: (0, 0)), pl.BlockSpec((D, tn), lambda j: (0, j)),
                  pl.BlockSpec((S, cfg.DH), lambda j: (0, 0)), pl.BlockSpec((S, cfg.DH), lambda j: (0, 0))],
        out_specs=pl.BlockSpec((None, S, tn), lambda j: (j // per, 0, j % per)),
        out_shape=jax.ShapeDtypeStruct((6, S, cfg.DSB), BF16),
        compiler_params=_cp(cfg, ("parallel",)),
    )(h1, w_in, cos2, sin2)


def _sb_tile(cfg, q, k, valid):
    z = _dot_nt(q, k) * (cfg.DH ** -0.5)
    lb = jnp.minimum(z, 0.0) - jnp.log1p(jnp.exp(-jnp.abs(z)))
    lk = lb - z
    return lb, (lk if valid is None else jnp.where(valid, lk, 0.0))


def _masked(valid, x):
    return x if valid is None else jnp.where(valid, x, 0.0)


def sb_fwd(cfg, qkv3):
    S, QB, DH, NH = cfg.S, cfg.SBT, cfg.DH, cfg.SBH

    def body(q_ref, k_ref, v_ref, o_ref, t_ref):
        row = lax.broadcasted_iota(jnp.int32, (QB, QB), 0)
        col = lax.broadcasted_iota(jnp.int32, (QB, QB), 1)
        u_after = (row > col).astype(BF16)
        causal = col < row
        heads = [slice(h * DH, (h + 1) * DH) for h in range(NH)]

        def q_loop(qb, _):
            rows = pl.ds(pl.multiple_of(qb * QB, QB), QB)
            qs = [q_ref[rows, hd] for hd in heads]

            def tile(kb, carry, valid):
                krows = pl.ds(pl.multiple_of(kb * QB, QB), QB)
                lbk = [_sb_tile(cfg, q, k_ref[krows, hd], valid) for q, hd in zip(qs, heads)]
                rems = [_dot_split(lk, u_after) for _, lk in lbk]
                aa = [_masked(valid, jnp.exp(lb + rem + c)).astype(BF16) for (lb, _), rem, (_, c) in zip(lbk, rems, carry)]
                return tuple((o_acc + _dot(a, v_ref[krows, hd]), c + jnp.sum(lk, axis=1, keepdims=True))
                             for a, hd, (_, lk), (o_acc, c) in zip(aa, heads, lbk, carry))

            carry = tile(qb, ((jnp.zeros((QB, DH), F32), jnp.zeros((QB, 1), F32)),) * NH, causal)
            carry = lax.fori_loop(0, qb, lambda i, cr: tile(qb - 1 - i, cr, None), carry)
            for hd, (o_acc, c) in zip(heads, carry):
                o_ref[rows, hd] = o_acc
                t_ref[rows, hd] = jnp.broadcast_to(c, (QB, DH))
            return 0

        lax.fori_loop(0, S // QB, q_loop, 0)

    def spec(i):
        return pl.BlockSpec((None, S, NH * DH), lambda h: (i, 0, h))

    return pl.pallas_call(
        body, name="sb_fwd", grid=(cfg.HSB // NH,),
        in_specs=[spec(0), spec(1), spec(2)],
        out_specs=[pl.BlockSpec((S, NH * DH), lambda h: (0, h))] * 2,
        out_shape=[jax.ShapeDtypeStruct((S, cfg.DSB), F32)] * 2,
        compiler_params=_cp(cfg, ("parallel",)),
    )(qkv3, qkv3, qkv3)


def sb_bwd(cfg, qkv3, do_sb, tsum):
    S, QB, DH, NH = cfg.S, cfg.SBT, cfg.DH, cfg.SBHB
    scale = DH ** -0.5

    def body(q_ref, k_ref, v_ref, do_ref, t_ref, d_ref, dk_acc, dv_acc):
        dk_acc[...] = jnp.zeros_like(dk_acc)
        dv_acc[...] = jnp.zeros_like(dv_acc)
        row = lax.broadcasted_iota(jnp.int32, (QB, QB), 0)
        col = lax.broadcasted_iota(jnp.int32, (QB, QB), 1)
        u_upto = (row <= col).astype(BF16)
        u_before = (row < col).astype(BF16)
        causal = col < row
        heads = [slice(h * DH, (h + 1) * DH) for h in range(NH)]

        def q_loop(qb, _):
            rows = pl.ds(pl.multiple_of(qb * QB, QB), QB)
            qs = [q_ref[rows, hd] for hd in heads]
            dos = [do_ref[rows, hd] for hd in heads]
            totals = [t_ref[rows, hd.start:hd.start + 1] for hd in heads]

            def tile(kb, carry, valid):
                krows = pl.ds(pl.multiple_of(kb * QB, QB), QB)
                ks = [k_ref[krows, hd] for hd in heads]
                lbk = [_sb_tile(cfg, q, k, valid) for q, k in zip(qs, ks)]
                das = [_dot_nt(do, v_ref[krows, hd]) for do, hd in zip(dos, heads)]
                pins = [_dot_split(lk, u_upto) for _, lk in lbk]
                aa = [_masked(valid, jnp.exp(lb + (tot - pc - pin)))
                      for (lb, _), tot, (_, pc, _), pin in zip(lbk, totals, carry, pins)]
                gs = [a * da for a, da in zip(aa, das)]
                for a, do, hd in zip(aa, dos, heads):
                    dv_acc[krows, hd] += _dot_tn(a.astype(BF16), do)
                cums = [gc + _dot(g.astype(BF16), u_before) for g, (_, _, gc) in zip(gs, carry)]
                dzs = [(_masked(valid, g - jnp.exp(lb) * (g + cum)) * scale).astype(BF16)
                       for g, (lb, _), cum in zip(gs, lbk, cums)]
                for dz, q, hd in zip(dzs, qs, heads):
                    dk_acc[krows, hd] += _dot_tn(dz, q)
                return tuple((dq + _dot(dz, k), pc + jnp.sum(lk, axis=1, keepdims=True), gc + jnp.sum(g, axis=1, keepdims=True))
                             for dz, k, (_, lk), g, (dq, pc, gc) in zip(dzs, ks, lbk, gs, carry))

            z1 = jnp.zeros((QB, 1), F32)
            carry = lax.fori_loop(0, qb, lambda kb, cr: tile(kb, cr, None), ((jnp.zeros((QB, DH), F32), z1, z1),) * NH)
            for hd, (dq_acc, _, _) in zip(heads, tile(qb, carry, causal)):
                d_ref[0, rows, hd] = dq_acc.astype(BF16)
            return 0

        lax.fori_loop(0, S // QB, q_loop, 0)
        d_ref[1, :, :] = dk_acc[...].astype(BF16)
        d_ref[2, :, :] = dv_acc[...].astype(BF16)

    def spec(i):
        return pl.BlockSpec((None, S, NH * DH), lambda h: (i, 0, h))

    hd_spec = pl.BlockSpec((S, NH * DH), lambda h: (0, h))
    return pl.pallas_call(
        body, name="sb_bwd", grid=(cfg.HSB // NH,),
        in_specs=[spec(0), spec(1), spec(2), hd_spec, hd_spec],
        out_specs=pl.BlockSpec((3, S, NH * DH), lambda h: (0, 0, h)),
        out_shape=jax.ShapeDtypeStruct((6, S, cfg.DSB), BF16),
        scratch_shapes=[pltpu.VMEM((S, NH * DH), F32), pltpu.VMEM((S, NH * DH), F32)],
        compiler_params=_cp(cfg, ("parallel",)),
    )(qkv3, qkv3, qkv3, do_sb, tsum)


def _band_mask(cfg, n, n_back):
    QB = cfg.QB
    qi = lax.broadcasted_iota(jnp.int32, (QB, 2 * QB), 0)
    kj = lax.broadcasted_iota(jnp.int32, (QB, 2 * QB), 1)
    dist = QB + qi - kj
    return (dist >= 0) & (dist <= n_back) & jnp.logical_or(n > 0, kj >= QB)


def _sub_rows(start, n, dil):
    if dil > 1:
        return pl.ds(start, n, stride=dil)
    return pl.ds(start if isinstance(start, int) else pl.multiple_of(start, 8), n)


def _stage_residues(cfg, dil, pairs):
    QB, L = cfg.QB, cfg.S // dil
    for src, dst in pairs:
        for r in range(dil):
            dst[pl.ds(r * (QB + L), QB), :] = jnp.zeros((QB, cfg.DH), BF16)
            dst[pl.ds(r * (QB + L) + QB, L), :] = src[_sub_rows(r, L, dil), :].astype(BF16)


def _staged_rows(cfg):
    return cfg.S + cfg.QB * max(d for _, d in cfg.branches)


def _lane_value(x):
    return jnp.max(x, axis=1, keepdims=True)


def dil_fwd(cfg, qkv3):
    S, QB, DH = cfg.S, cfg.QB, cfg.DH
    scale = DH ** -0.5
    nb = len(cfg.branches)
    mix_rows = min(256, S)

    def body(q_ref, k_ref, v_ref, o_ref, lt_ref, qf, kf, vf, kp, vp, *obl):
        obs, lbs = obl[:nb], obl[nb:]
        qf[...] = q_ref[...].astype(F32)
        kf[...] = k_ref[...].astype(F32)
        vf[...] = v_ref[...].astype(F32)
        for b, (window, dil) in enumerate(cfg.branches):
            L, n_back = S // dil, window // dil
            assert n_back <= QB and L % QB == 0
            _stage_residues(cfg, dil, [(kf, kp), (vf, vp)])
            for r in range(dil):
                for n in range(L // QB):
                    rows = _sub_rows(r + n * (QB * dil), QB, dil)
                    band = pl.ds(r * (QB + L) + n * QB, 2 * QB)
                    s = _dot_nt(qf[rows, :].astype(BF16), kp[band, :]) * scale
                    s = jnp.where(_band_mask(cfg, n, n_back), s, NEG_BIG)
                    m = jnp.max(s, axis=1, keepdims=True)
                    p = jnp.exp(s - m)
                    den = jnp.sum(p, axis=1, keepdims=True)
                    obs[b][rows, :] = _dot(p.astype(BF16), vp[band, :]) / den
                    lbs[b][rows, :] = jnp.broadcast_to(m + jnp.log(den), (QB, DH))

        def mix(i, _):
            rows = pl.ds(pl.multiple_of(i * mix_rows, mix_rows), mix_rows)
            ls = [r[rows, :] for r in lbs]
            m = functools.reduce(jnp.maximum, ls)
            es = [jnp.exp(l - m) for l in ls]
            tot = functools.reduce(jnp.add, es)
            o_ref[rows, :] = functools.reduce(jnp.add, [(e / tot) * r[rows, :] for e, r in zip(es, obs)])
            lt_ref[rows, :] = m + jnp.log(tot)
            return 0

        lax.fori_loop(0, S // mix_rows, mix, 0)

    def spec(i):
        return pl.BlockSpec((None, S, DH), lambda h: (i, 0, h))

    o_spec = pl.BlockSpec((S, DH), lambda h: (0, h))
    return pl.pallas_call(
        body, name="dil_fwd", grid=(cfg.HDL,),
        in_specs=[spec(3), spec(4), spec(5)], out_specs=[o_spec, o_spec],
        out_shape=[jax.ShapeDtypeStruct((S, cfg.DDL), F32)] * 2,
        scratch_shapes=[pltpu.VMEM((S, DH), F32)] * 3 + [pltpu.VMEM((_staged_rows(cfg), DH), BF16)] * 2
        + [pltpu.VMEM((S, DH), F32)] * (2 * nb),
        compiler_params=_cp(cfg, ("parallel",)),
    )(qkv3, qkv3, qkv3)


def dil_bwd(cfg, qkv3, do_dl, delta, lse_tot, cos2, sin2, d_sb3):
    S, QB, DH = cfg.S, cfg.QB, cfg.DH
    scale = DH ** -0.5
    out_rows = min(256, S)

    def body(q_ref, k_ref, v_ref, do_ref, dl_ref, lt_ref, c_ref, s_ref, base_ref, d_ref,
             qf, kf, vf, dof, kp, vp, dkp, dvp, dqn, dkn, dvn):
        qf[...] = q_ref[...].astype(F32)
        kf[...] = k_ref[...].astype(F32)
        vf[...] = v_ref[...].astype(F32)
        dof[...] = do_ref[...].astype(F32)
        for acc in (dqn, dkn, dvn):
            acc[...] = jnp.zeros_like(acc)
        for window, dil in cfg.branches:
            L, n_back = S // dil, window // dil
            reg = QB + L
            _stage_residues(cfg, dil, [(kf, kp), (vf, vp)])
            dkp[pl.ds(0, dil * reg), :] = jnp.zeros((dil * reg, DH), F32)
            dvp[pl.ds(0, dil * reg), :] = jnp.zeros((dil * reg, DH), F32)
            for r in range(dil):
                for n in range(L // QB):
                    rows = _sub_rows(r + n * (QB * dil), QB, dil)
                    band = pl.ds(r * reg + n * QB, 2 * QB)
                    q = qf[rows, :].astype(BF16)
                    do = dof[rows, :].astype(BF16)
                    kb = kp[band, :]
                    s = _dot_nt(q, kb) * scale
                    s = jnp.where(_band_mask(cfg, n, n_back), s, NEG_BIG)
                    p = jnp.exp(s - _lane_value(lt_ref[rows, :]))
                    ds = (p * (_dot_nt(do, vp[band, :]) - _lane_value(dl_ref[rows, :])) * scale).astype(BF16)
                    dqn[rows, :] += _dot(ds, kb)
                    dkp[band, :] += _dot_tn(ds, q)
                    dvp[band, :] += _dot_tn(p.astype(BF16), do)
            for r in range(dil):
                sub = _sub_rows(r, L, dil)
                dkn[sub, :] += dkp[pl.ds(r * reg + QB, L), :]
                dvn[sub, :] += dvp[pl.ds(r * reg + QB, L), :]

        def finish(i, _):
            rows = pl.ds(pl.multiple_of(i * out_rows, out_rows), out_rows)
            c, sn = c_ref[rows, :], s_ref[rows, :]
            for j, acc in enumerate((dqn, dkn)):
                d = acc[rows, :]
                d_ref[j, rows, :] = (d * c + pltpu.roll(d * sn, DH // 2, 1)).astype(BF16)
            d_ref[2, rows, :] = dvn[rows, :].astype(BF16)
            return 0

        lax.fori_loop(0, S // out_rows, finish, 0)

    def spec(i):
        return pl.BlockSpec((None, S, DH), lambda h: (i, 0, h))

    hd = pl.BlockSpec((S, DH), lambda h: (0, h))
    tab = pl.BlockSpec((S, DH), lambda h: (0, 0))
    ns = _staged_rows(cfg)
    return pl.pallas_call(
        body, name="dil_bwd", grid=(cfg.HDL,),
        in_specs=[spec(3), spec(4), spec(5), hd, hd, hd, tab, tab, ANY],
        out_specs=pl.BlockSpec((3, S, DH), lambda h: (1, 0, h)),
        out_shape=jax.ShapeDtypeStruct((6, S, cfg.DDL), BF16),
        input_output_aliases={8: 0},
        scratch_shapes=[pltpu.VMEM((S, DH), F32)] * 4 + [pltpu.VMEM((ns, DH), BF16)] * 2
        + [pltpu.VMEM((ns, DH), F32)] * 2 + [pltpu.VMEM((S, DH), F32)] * 3,
        compiler_params=_cp(cfg, ("parallel",)),
    )(qkv3, qkv3, qkv3, do_dl, delta, lse_tot, cos2, sin2, d_sb3)


def combine_fwd(cfg, o_sb, o_dl, g_sb, g_dl):
    S, DH = cfg.S, cfg.DH

    def head_norm(o, g):
        return o * lax.rsqrt(jnp.mean(o * o, axis=-1, keepdims=True) + RMS_EPS) * g

    def body(osb_ref, odl_ref, gsb_ref, gdl_ref, mix_ref):
        for h in range(cfg.HSB):
            c = slice(h * DH, (h + 1) * DH)
            mix_ref[:, c] = head_norm(osb_ref[:, c], gsb_ref[:, c]).astype(BF16)
        for h in range(cfg.HDL):
            c = slice(h * DH, (h + 1) * DH)
            mix_ref[:, cfg.DSB + h * DH:cfg.DSB + (h + 1) * DH] = head_norm(odl_ref[:, c], gdl_ref[:, c]).astype(BF16)

    return pl.pallas_call(
        body, name="combine_fwd", grid=(S // cfg.TM,),
        in_specs=[_row(cfg, cfg.DSB), _row(cfg, cfg.DDL), _vec(cfg.DSB), _vec(cfg.DDL)],
        out_specs=_row(cfg, cfg.DMIX), out_shape=jax.ShapeDtypeStruct((S, cfg.DMIX), BF16),
        compiler_params=_cp(cfg, ("parallel",)),
    )(o_sb, o_dl, g_sb, g_dl)


def combine_bwd(cfg, dmix, o_sb, o_dl, g_sb, g_dl):
    S, DH = cfg.S, cfg.DH

    def body(dm_ref, osb_ref, odl_ref, gsb_ref, gdl_ref, dsb_ref, ddl_ref, dl_ref, dgsb_ref, dgdl_ref):
        @pl.when(pl.program_id(0) == 0)
        def _():
            dgsb_ref[...] = jnp.zeros_like(dgsb_ref)
            dgdl_ref[...] = jnp.zeros_like(dgdl_ref)

        for h in range(cfg.HSB):
            c = slice(h * DH, (h + 1) * DH)
            dx, dgx = _rms_bwd(dm_ref[:, c], osb_ref[:, c], gsb_ref[:, c])
            dsb_ref[:, c] = dx.astype(BF16)
            dgsb_ref[:, c] += jnp.sum(dgx, axis=0, keepdims=True)
        for h in range(cfg.HDL):
            c = slice(h * DH, (h + 1) * DH)
            o = odl_ref[:, c]
            dx, dgx = _rms_bwd(dm_ref[:, cfg.DSB + h * DH:cfg.DSB + (h + 1) * DH], o, gdl_ref[:, c])
            ddl_ref[:, c] = dx.astype(BF16)
            dl_ref[:, c] = jnp.broadcast_to(jnp.sum(dx * o, axis=-1, keepdims=True), dx.shape)
            dgdl_ref[:, c] += jnp.sum(dgx, axis=0, keepdims=True)

    return pl.pallas_call(
        body, name="combine_bwd", grid=(S // cfg.TM,),
        in_specs=[_row(cfg, cfg.DMIX), _row(cfg, cfg.DSB), _row(cfg, cfg.DDL), _vec(cfg.DSB), _vec(cfg.DDL)],
        out_specs=[_row(cfg, cfg.DSB), _row(cfg, cfg.DDL), _row(cfg, cfg.DDL), _vec(cfg.DSB), _vec(cfg.DDL)],
        out_shape=[jax.ShapeDtypeStruct((S, cfg.DSB), BF16), jax.ShapeDtypeStruct((S, cfg.DDL), BF16),
                   jax.ShapeDtypeStruct((S, cfg.DDL), F32), jax.ShapeDtypeStruct((1, cfg.DSB), F32),
                   jax.ShapeDtypeStruct((1, cfg.DDL), F32)],
        compiler_params=_cp(cfg, ("arbitrary",)),
    )(dmix, o_sb, o_dl, g_sb, g_dl)


SUB = 8


def _shift_down(u, prev, j):
    rolled = pltpu.roll(u, j, 0)
    row = lax.broadcasted_iota(jnp.int32, (SUB, u.shape[1]), 0)
    head = jnp.where(row >= j, rolled[:SUB], pltpu.roll(prev, j, 0))
    return jnp.concatenate([head, rolled[SUB:]], axis=0)


def _shift_up(u, nxt, j):
    n = u.shape[0]
    rolled = pltpu.roll(u, n - j, 0)
    row = lax.broadcasted_iota(jnp.int32, (SUB, u.shape[1]), 0)
    tail = jnp.where(row < SUB - j, rolled[n - SUB:], pltpu.roll(nxt, SUB - j, 0))
    return jnp.concatenate([rolled[:n - SUB], tail], axis=0)


def _conv(u, s1, s2, cw, cb):
    return u * cw[2:3, :] + s1 * cw[1:2, :] + s2 * cw[0:1, :] + cb


def _chunk_rows(cfg):
    ch = min(cfg.FCH, cfg.S)
    return ch, cfg.S // ch


def ffn_fwd(cfg, h2, w_up, conv_w, conv_b):
    S, D = h2.shape
    tn, nt = cfg.TNF, cfg.FFP // cfg.TNF
    ch, nch = _chunk_rows(cfg)

    def body(h_ref, wg_ref, wv_ref, cwg_ref, cwv_ref, cbg_ref, cbv_ref, u_ref, y_ref):
        prev = [jnp.zeros((SUB, tn), F32)] * 2
        for ci in range(nch):
            rows = pl.ds(ci * ch, ch)
            h = h_ref[rows, :]
            us = [_dot_nt(h, wg_ref[...]), _dot_nt(h, wv_ref[...])]
            cs = []
            for i, (cw_ref, cb_ref) in enumerate(((cwg_ref, cbg_ref), (cwv_ref, cbv_ref))):
                u_ref[i, rows, :] = us[i]
                cs.append(_conv(us[i], _shift_down(us[i], prev[i], 1), _shift_down(us[i], prev[i], 2),
                                cw_ref[...], cb_ref[...]))
            y_ref[rows, :] = (_gelu(cs[0])[0] * cs[1]).astype(BF16)
            prev = [u[ch - SUB:] for u in us]

    return pl.pallas_call(
        body, name="ffn_fwd", grid=(nt,),
        in_specs=[pl.BlockSpec((S, D), lambda n: (0, 0)),
                  pl.BlockSpec((tn, D), lambda n: (n, 0)), pl.BlockSpec((tn, D), lambda n: (n + nt, 0)),
                  pl.BlockSpec((3, tn), lambda n: (0, n)), pl.BlockSpec((3, tn), lambda n: (0, n + nt)),
                  pl.BlockSpec((1, tn), lambda n: (0, n)), pl.BlockSpec((1, tn), lambda n: (0, n + nt))],
        out_specs=[pl.BlockSpec((2, S, tn), lambda n: (0, 0, n)), pl.BlockSpec((S, tn), lambda n: (0, n))],
        out_shape=[jax.ShapeDtypeStruct((2, S, cfg.FFP), F32), jax.ShapeDtypeStruct((S, cfg.FFP), BF16)],
        compiler_params=_cp(cfg, ("parallel",)),
    )(h2, w_up, w_up, conv_w, conv_w, conv_b, conv_b)


def ffn_bwd(cfg, df, h2, w_down, u, conv_w, conv_b):
    S, D = df.shape
    tn, nt = cfg.TNF, cfg.FFP // cfg.TNF

    ch, nch = _chunk_rows(cfg)

    def body(df_ref, h_ref, wd_ref, u_ref, cwg_ref, cwv_ref, cbg_ref, cbv_ref,
             du_ref, dwd_ref, dwu_ref, dcw_ref, dcb_ref):
        cws = (cwg_ref[...], cwv_ref[...])
        cbs = (cbg_ref[...], cbv_ref[...])
        zero = jnp.zeros((SUB, tn), F32)
        nxt = [zero, zero]
        dws = [[jnp.zeros((1, tn), F32)] * 4 for _ in range(2)]
        dwd = jnp.zeros((tn, D), F32)
        dwu = [jnp.zeros((tn, D), F32)] * 2
        for ci in reversed(range(nch)):
            rows = pl.ds(ci * ch, ch)
            dfv = df_ref[rows, :]
            hv = h_ref[rows, :]
            dy = _dot_nt(dfv, wd_ref[...])
            us, s1, s2, cs = [], [], [], []
            for i in range(2):
                u = u_ref[i, rows, :]
                prev = u_ref[i, pl.ds(ci * ch - SUB, SUB), :] if ci else zero
                us.append(u)
                s1.append(_shift_down(u, prev, 1))
                s2.append(_shift_down(u, prev, 2))
                cs.append(_conv(u, s1[i], s2[i], cws[i], cbs[i]))
            gl, t = _gelu(cs[0])
            dwd = dwd + _dot_tn((gl * cs[1]).astype(BF16), dfv)
            dcs = (dy * cs[1] * _gelu_grad(cs[0], t), dy * gl)
            for i, dc in enumerate(dcs):
                du = dc * cws[i][2:3, :] + _shift_up(dc, nxt[i], 1) * cws[i][1:2, :] + _shift_up(dc, nxt[i], 2) * cws[i][0:1, :]
                du = du.astype(BF16)
                du_ref[i, rows, :] = du
                dwu[i] = dwu[i] + _dot_tn(du, hv)
                for j, tap in enumerate((s2[i], s1[i], us[i])):
                    dws[i][j] = dws[i][j] + jnp.sum(dc * tap, axis=0, keepdims=True)
                dws[i][3] = dws[i][3] + jnp.sum(dc, axis=0, keepdims=True)
            nxt = [dc[:SUB] for dc in dcs]
        dwd_ref[...] = dwd.astype(BF16)
        for i in range(2):
            dwu_ref[i] = dwu[i].astype(BF16)
            for j in range(3):
                dcw_ref[i, j:j + 1, :] = dws[i][j]
            dcb_ref[i] = dws[i][3]

    whole = pl.BlockSpec((S, D), lambda n: (0, 0), pipeline_mode=pl.Buffered(1))
    du, dwd, dwu, dcw, dcb = pl.pallas_call(
        body, name="ffn_bwd", grid=(nt,),
        in_specs=[whole, whole, pl.BlockSpec((tn, D), lambda n: (n, 0)),
                  pl.BlockSpec((2, S, tn), lambda n: (0, 0, n)),
                  pl.BlockSpec((3, tn), lambda n: (0, n)), pl.BlockSpec((3, tn), lambda n: (0, n + nt)),
                  pl.BlockSpec((1, tn), lambda n: (0, n)), pl.BlockSpec((1, tn), lambda n: (0, n + nt))],
        out_specs=[pl.BlockSpec((2, S, tn), lambda n: (0, 0, n)), pl.BlockSpec((tn, D), lambda n: (n, 0)),
                   pl.BlockSpec((2, tn, D), lambda n: (0, n, 0)),
                   pl.BlockSpec((2, 3, tn), lambda n: (0, 0, n)), pl.BlockSpec((2, 1, tn), lambda n: (0, 0, n))],
        out_shape=[jax.ShapeDtypeStruct((2, S, cfg.FFP), BF16), jax.ShapeDtypeStruct((cfg.FFP, D), BF16),
                   jax.ShapeDtypeStruct((2, cfg.FFP, D), BF16),
                   jax.ShapeDtypeStruct((2, 3, cfg.FFP), F32), jax.ShapeDtypeStruct((2, 1, cfg.FFP), F32)],
        compiler_params=_cp(cfg, ("parallel",)),
    )(df, h2, w_down, u, conv_w, conv_w, conv_b, conv_b)
    return du, dwd, dwu.reshape(cfg.FF2P, D), dcw, dcb


def rope_tables(cfg):
    inv_freq = ROPE_THETA ** (-jnp.arange(0, cfg.DH, 2, dtype=F32) / cfg.DH)
    ang = jnp.arange(cfg.S, dtype=F32)[:, None] * inv_freq[None, :]
    cos, sin = jnp.cos(ang), jnp.sin(ang)
    return jnp.concatenate([cos, cos], axis=1), jnp.concatenate([-sin, sin], axis=1)


class LocalWeights:
    def __init__(self, w_in, w_out, w_up, conv_w, w_down):
        self.w = (w_in, w_out, w_up, conv_w, w_down)
        self.grads = {}

    def weights_first(self):
        return self.w[0], self.w[3]

    def start_rest(self):
        return None

    def weights_rest(self, group, after):
        return (self.w[1], self.w[2]) if group == 0 else (self.w[4],)

    def pair_start(self, grads):
        self.grads.update(grads)
        return None

    def reduce_start(self, grads, after=None):
        self.grads.update(grads)
        return None

    def reduce_wait(self, names, after):
        pass


def _after(a, token):
    return a if token is None else a + token[0, 0].astype(a.dtype)


def local_step(cfg, comm, x, target, g1, g2, g3, g4, g_sb, g_dl, conv_b):
    S, D = cfg.S, cfg.D
    cos2, sin2 = rope_tables(cfg)
    full = lambda r, c: pl.BlockSpec((r, c), lambda j, k: (0, 0))

    w_in, conv_w = comm.weights_first()
    h1 = rms_fwd(cfg, x, g1)
    qkv3 = qkv_proj(cfg, h1, w_in, _after(cos2, comm.start_rest()), sin2)
    o_sb, tsum = sb_fwd(cfg, qkv3)
    o_dl, lse_tot = dil_fwd(cfg, qkv3)
    mixed = combine_fwd(cfg, o_sb, o_dl, g_sb, g_dl)
    w_out, w_up = comm.weights_rest(0, after=mixed)
    tn = cfg.TN
    mo = _mm(cfg, "mix_out", mixed, w_out, nt=False, grid=(D // tn, 1),
             a_spec=full(S, cfg.DMIX), b_spec=pl.BlockSpec((cfg.DMIX, tn), lambda j, k: (0, j)),
             o_spec=pl.BlockSpec((S, tn), lambda j, k: (0, j)),
             out_shape=jax.ShapeDtypeStruct((S, D), F32), acc_shape=(8, LANE))
    x1, h2 = mid_fwd(cfg, x, mo, g2, g3)
    u, y = ffn_fwd(cfg, h2, w_up, conv_w, conv_b)
    w_down, = comm.weights_rest(1, after=y)
    tk = cfg.FFP // 2
    f = _mm(cfg, "ffn_down", y, w_down, nt=False, grid=(D // tn, cfg.FFP // tk),
            a_spec=pl.BlockSpec((S, tk), lambda j, k: (0, k)), b_spec=pl.BlockSpec((tk, tn), lambda j, k: (k, j)),
            o_spec=pl.BlockSpec((S, tn), lambda j, k: (0, j)),
            out_shape=jax.ShapeDtypeStruct((S, D), F32), acc_shape=(S, tn))
    dout, df, dg4, loss = final_fwd_bwd(cfg, x1, f, g4, target)

    du, dw_down, dw_up, dconv_w, dconv_b = ffn_bwd(cfg, df, h2, w_down, u, conv_w, conv_b)
    kt = cfg.FFP // tk
    dh2 = _mm(cfg, "d_h2", du, w_up, nt=False, grid=(D // tn, 2 * kt),
              a_spec=pl.BlockSpec((None, S, tk), lambda j, k: (k // kt, 0, k % kt)),
              b_spec=pl.BlockSpec((tk, tn), lambda j, k: (k, j)),
              o_spec=pl.BlockSpec((S, tn), lambda j, k: (0, j)),
              out_shape=jax.ShapeDtypeStruct((S, D), F32), acc_shape=(S, tn),
              dep=comm.pair_start(dict(w_down=dw_down, w_up=dw_up)))
    dx1, dmo, dg3, dg2 = mid_bwd(cfg, dh2, x1, g3, dout, mo, g2)

    dmix = _mm(cfg, "d_mixed", dmo, w_out, nt=True, grid=(cfg.DMIX // tn, 1),
               a_spec=full(S, D), b_spec=pl.BlockSpec((tn, D), lambda j, k: (j, 0)),
               o_spec=pl.BlockSpec((S, tn), lambda j, k: (0, j)),
               out_shape=jax.ShapeDtypeStruct((S, cfg.DMIX), F32), acc_shape=(8, LANE))
    dw_out = _mm_tn(cfg, "d_w_out", mixed, dmo, grid=(D // tn,),
                    a_spec=pl.BlockSpec((S, cfg.DMIX), lambda j: (0, 0)),
                    b_spec=pl.BlockSpec((S, tn), lambda j: (0, j)),
                    o_spec=pl.BlockSpec((cfg.DMIX, tn), lambda j: (0, j)),
                    out_shape=jax.ShapeDtypeStruct((cfg.DMIX, D), BF16))
    token = comm.reduce_start(dict(w_out=dw_out), after=dw_out)
    do_sb, do_dl, delta, dg_sb, dg_dl = combine_bwd(cfg, dmix, o_sb, o_dl, _after(g_sb, token), g_dl)
    d_sb3 = sb_bwd(cfg, qkv3, do_sb, tsum)
    dqkv3 = dil_bwd(cfg, qkv3, do_dl, delta, lse_tot, cos2, sin2, d_sb3)
    comm.reduce_wait(("w_out", "w_up", "w_down"), after=dqkv3)
    tkq = min(tn, cfg.DSB)
    kq = cfg.DSB // tkq
    dw_in = _mm_tn(cfg, "d_w_in", h1, dqkv3, grid=(6 * kq,),
                   a_spec=pl.BlockSpec((S, D), lambda j: (0, 0)),
                   b_spec=pl.BlockSpec((None, S, tkq), lambda j: (j // kq, 0, j % kq)),
                   o_spec=pl.BlockSpec((D, tkq), lambda j: (0, j)),
                   out_shape=jax.ShapeDtypeStruct((D, 6 * cfg.DSB), BF16))
    token = comm.reduce_start(dict(w_in=dw_in))
    dh1 = _mm(cfg, "d_h1", dqkv3, w_in, nt=True, grid=(D // tn, 6),
              a_spec=pl.BlockSpec((None, S, cfg.DSB), lambda j, k: (k, 0, 0)),
              b_spec=pl.BlockSpec((tn, cfg.DSB), lambda j, k: (j, k)),
              o_spec=pl.BlockSpec((S, tn), lambda j, k: (0, j)),
              out_shape=jax.ShapeDtypeStruct((S, D), F32), acc_shape=(S, tn), dep=token)
    grad_x, dg1 = first_bwd(cfg, dh1, x, g1, dx1)
    comm.reduce_wait(("w_in",), after=grad_x)
    small = dict(loss=loss, g1=dg1, g2=dg2, g3=dg3, g4=dg4, g_sb=dg_sb, g_dl=dg_dl,
                 conv_b=dconv_b.reshape(1, cfg.FF2P), conv_w=dconv_w.transpose(1, 0, 2).reshape(3, cfg.FF2P))
    return grad_x, small


ANY = pl.BlockSpec(memory_space=pl.ANY)


def _me():
    return lax.axis_index("x"), lax.axis_index("y"), lax.axis_index("c")


def _other_chips(x, y):
    return [(1 - x, y), (x, 1 - y), (1 - x, 1 - y)]


def pad_conv_w(cfg, conv_w):
    r, c = conv_w.shape

    def body(w_ref, o_ref):
        o_ref[:, :c] = w_ref[...]
        o_ref[:, c:] = jnp.zeros((r, cfg.FSHP - c), F32)

    return pl.pallas_call(body, name="pad_conv_w", out_shape=jax.ShapeDtypeStruct((r, cfg.FSHP), F32))(conv_w)


def _tile2(r, c):
    return (256, c) if r % 256 == 0 else (r, 512 if c % 512 == 0 else c)


def cast_into(cfg, name, w, pos):
    r, c = w.shape
    _, nr, _, nc = _slab(cfg, name, 0)
    tm, tc = _tile2(r, c)
    wr = nr if tm == r else tm
    assert nc == c and (nr == r or tm == r)

    def body(pos_ref, w_ref, full_ref, scr, sem):
        scr[pl.ds(0, tm), :] = w_ref[...].astype(BF16)
        if wr > tm:
            scr[pl.ds(tm, wr - tm), :] = jnp.zeros((wr - tm, tc), BF16)
        r0, _, c0, _ = _slab(cfg, name, pos_ref[0])
        rows = pl.ds(pl.multiple_of(r0 + pl.program_id(0) * tm, 16), wr)
        cols = pl.ds(pl.multiple_of(c0 + pl.program_id(1) * tc, LANE), tc)
        cp = pltpu.make_async_copy(scr, full_ref.at[rows, cols], sem)
        cp.start()
        cp.wait()

    return pl.pallas_call(
        body, name=f"cast_{name}",
        grid_spec=pltpu.PrefetchScalarGridSpec(
            num_scalar_prefetch=1, grid=(r // tm, c // tc),
            in_specs=[pl.BlockSpec((tm, tc), lambda i, j, p: (i, j))], out_specs=ANY,
            scratch_shapes=[pltpu.VMEM((wr, tc), BF16), pltpu.SemaphoreType.DMA]),
        out_shape=jax.ShapeDtypeStruct(_full_shape(cfg, name), BF16),
        compiler_params=_cp(cfg, ("arbitrary", "arbitrary")),
    )(pos, w)


HBM = pl.BlockSpec(memory_space=pltpu.HBM)
SEM = pl.BlockSpec(memory_space=pltpu.SEMAPHORE)
TOKEN = pl.BlockSpec(memory_space=pltpu.VMEM)
EFFECT = pltpu.SideEffectType.DATAFLOW_SIDE_EFFECTING


def _slab(cfg, name, k):
    D = cfg.D
    if name == "w_in":
        cin = 6 * cfg.DSB // N_CHIPS
        return 0, D, k * cin, cin
    if name == "w_out":
        rout = cfg.DMIX // N_CHIPS
        return k * rout, rout, 0, D
    if name == "w_up":
        return k * cfg.FSHP, cfg.FSHP, 0, D
    rdn = cfg.FSH // 2
    return (k // 2) * cfg.FSHP + (k % 2) * rdn, rdn, 0, D


def _full_shape(cfg, name):
    return dict(w_in=(cfg.D, 6 * cfg.DSB), w_out=(cfg.DMIX, cfg.D), w_up=(cfg.FF2P, cfg.D), w_down=(cfg.FFP, cfg.D))[name]


def _half(cfg, name, ref, k, h):
    r0, nr, c0, nc = _slab(cfg, name, k)
    return ref.at[pl.ds(r0 + h * (nr // 2), nr // 2), pl.ds(c0, nc)]


def _rows_half(ref, h):
    nr = ref.shape[0] // 2
    return ref.at[pl.ds(h * nr, nr), :]


def _remote(src, dst, send_sem, recv_sem, dev):
    return pltpu.make_async_remote_copy(src_ref=src, dst_ref=dst, send_sem=send_sem, recv_sem=recv_sem,
                                        device_id=dev, device_id_type=MESH)


def gather_first(cfg, g_in, sh_cw):
    def body(in_ref, cw_ref, g_in, g_cw, token, ssem, rsem, fssem, frsem, lsem):
        x, y, c = _me()
        me, sib = 2 * x + y, (x, y, 1 - c)
        cw_slot = lambda k: g_cw.at[:, pl.ds(k * cfg.FSHP, cfg.FSHP)]
        local = [pltpu.make_async_copy(cw_ref, cw_slot(me), lsem.at[0])]
        sends = []
        for j, (px, py) in enumerate(_other_chips(x, y)):
            mine = _half(cfg, "w_in", g_in, me, c)
            sends.append(_remote(mine, mine, ssem.at[j], rsem.at[j], (px, py, c)))
            sends.append(_remote(cw_ref, cw_slot(me), ssem.at[3 + j], rsem.at[3 + j], (px, py, c)))
        for cp in local + sends:
            cp.start()
        for j, (px, py) in enumerate(_other_chips(x, y)):
            k = 2 * px + py
            landed = _half(cfg, "w_in", g_in, k, c)
            _remote(landed, landed, ssem.at[j], rsem.at[j], (px, py, c)).wait_recv()
            fwd = _remote(landed, landed, fssem.at[j], frsem.at[j], sib)
            fwd.start()
            sends.append(fwd)
        for j, (px, py) in enumerate(_other_chips(x, y)):
            k = 2 * px + py
            passed = _half(cfg, "w_in", g_in, k, 1 - c)
            _remote(passed, passed, fssem.at[j], frsem.at[j], sib).wait_recv()
            _remote(cw_ref, cw_slot(k), ssem.at[3 + j], rsem.at[3 + j], (px, py, c)).wait_recv()
        for cp in sends:
            cp.wait_send()
        for cp in local:
            cp.wait()
        token[...] = jnp.zeros_like(token)

    return pl.pallas_call(
        body, name="gather_first", in_specs=[ANY, ANY], out_specs=[ANY, ANY, TOKEN],
        out_shape=[jax.ShapeDtypeStruct(_full_shape(cfg, "w_in"), BF16), jax.ShapeDtypeStruct((3, cfg.FF2P), F32),
                   jax.ShapeDtypeStruct((8, LANE), F32)],
        input_output_aliases={0: 0},
        scratch_shapes=[pltpu.SemaphoreType.DMA((6,)), pltpu.SemaphoreType.DMA((6,)), pltpu.SemaphoreType.DMA((3,)),
                        pltpu.SemaphoreType.DMA((3,)), pltpu.SemaphoreType.DMA((1,))],
    )(g_in, sh_cw)


REST = ("w_out", "w_up", "w_down")


def _hbm(a):
    return pltpu.with_memory_space_constraint(a, pltpu.HBM)


GROUPS = (("w_out", "w_up"), ("w_down",))


def gather_start(cfg, fulls, after):
    n, ng = len(REST), len(GROUPS)

    def body(*refs):
        lands = dict(zip(REST, refs[:n]))
        sems = refs[n + 1:n + 1 + 2 * ng]
        token = refs[-1]
        x, y, c = _me()
        me = 2 * x + y
        for g, names in enumerate(GROUPS):
            for i, name in enumerate(names):
                mine = _half(cfg, name, lands[name], me, c)
                for j, (px, py) in enumerate(_other_chips(x, y)):
                    _remote(mine, mine, sems[2 * g].at[3 * i + j], sems[2 * g + 1].at[3 * i + j], (px, py, c)).start()
        token[...] = jnp.zeros_like(token)

    ops = [_hbm(fulls[k]) for k in REST]
    sem_shapes = [pltpu.SemaphoreType.DMA((3 * len(names),)) for names in GROUPS for _ in range(2)]
    outs = pl.pallas_call(
        body, name="gather_start",
        in_specs=[HBM] * n + [ANY],
        out_specs=[SEM] * (2 * ng) + [HBM] * n + [TOKEN],
        out_shape=sem_shapes + [pltpu.HBM(a.shape, a.dtype) for a in ops] + [jax.ShapeDtypeStruct((8, LANE), F32)],
        input_output_aliases={i: 2 * ng + i for i in range(n)},
        compiler_params=pltpu.CompilerParams(has_side_effects=EFFECT),
    )(*ops, after)
    thru = dict(zip(REST, outs[2 * ng:2 * ng + n]))
    groups = [(outs[2 * g], outs[2 * g + 1], [thru[k] for k in names]) for g, names in enumerate(GROUPS)]
    return groups, outs[-1]


def gather_wait(cfg, names, ssem, rsem, lands, after):
    n = len(names)

    def body(*refs):
        lands_ = refs[:n]
        ssem_, rsem_ = refs[n], refs[n + 1]
        x, y, c = _me()
        me = 2 * x + y
        for i, name in enumerate(names):
            for j, (px, py) in enumerate(_other_chips(x, y)):
                cp = _remote(_half(cfg, name, lands_[i], me, c), _half(cfg, name, lands_[i], 2 * px + py, c),
                             ssem_.at[3 * i + j], rsem_.at[3 * i + j], (px, py, c))
                cp.wait_send()
                cp.wait_recv()

    return pl.pallas_call(
        body, name="gather_wait_" + "_".join(names),
        in_specs=[HBM] * n + [SEM, SEM, ANY], out_specs=[HBM] * n,
        out_shape=[pltpu.HBM(a.shape, a.dtype) for a in lands],
        input_output_aliases={i: i for i in range(n)},
        compiler_params=pltpu.CompilerParams(has_side_effects=EFFECT),
    )(*lands, ssem, rsem, after)


def gather_finish(cfg, names, lands):
    n = len(names)
    rdn = cfg.FSH // 2
    zpad = jnp.zeros((cfg.FSHP - cfg.FSH, cfg.D), BF16)

    def body(*refs):
        z_ref, outs = refs[0], refs[n + 1:2 * n + 1]
        ssem, rsem, lsem = refs[2 * n + 1:]
        x, y, c = _me()
        sib = (x, y, 1 - c)
        local = []
        if "w_down" in names:
            dn = outs[names.index("w_down")]
            local = [pltpu.make_async_copy(z_ref, dn.at[pl.ds(h * cfg.FSHP + 2 * rdn, cfg.FSHP - cfg.FSH), :],
                                           lsem.at[h]) for h in range(2)]
        fwds = []
        for i, name in enumerate(names):
            for j, (px, py) in enumerate(_other_chips(x, y)):
                landed = _half(cfg, name, outs[i], 2 * px + py, c)
                fwds.append(_remote(landed, landed, ssem.at[3 * i + j], rsem.at[3 * i + j], sib))
        for cp in local + fwds:
            cp.start()
        for i, name in enumerate(names):
            for j, (px, py) in enumerate(_other_chips(x, y)):
                passed = _half(cfg, name, outs[i], 2 * px + py, 1 - c)
                _remote(passed, passed, ssem.at[3 * i + j], rsem.at[3 * i + j], sib).wait_recv()
        for cp in fwds:
            cp.wait_send()
        for cp in local:
            cp.wait()

    return pl.pallas_call(
        body, name="gather_finish_" + "_".join(names), in_specs=[ANY] * (n + 1), out_specs=[ANY] * n,
        out_shape=[jax.ShapeDtypeStruct(a.shape, a.dtype) for a in lands],
        input_output_aliases={1 + i: i for i in range(n)},
        scratch_shapes=[pltpu.SemaphoreType.DMA((3 * n,)), pltpu.SemaphoreType.DMA((3 * n,)),
                        pltpu.SemaphoreType.DMA((2,))],
    )(zpad, *lands)


def pair_send(cfg, grads):
    names = list(grads)
    n = len(names)

    def half_shape(name):
        _, nr, _, nc = _slab(cfg, name, 0)
        return (N_CHIPS, nr // 2, nc)

    def body(*refs):
        srcs, theirs = refs[:n], refs[n:2 * n]
        ssem, rsem = refs[2 * n:]
        x, y, c = _me()
        cps = []
        for i, name in enumerate(names):
            for k in range(N_CHIPS):
                cps.append(_remote(_half(cfg, name, srcs[i], k, 1 - c), theirs[i].at[k],
                                   ssem.at[N_CHIPS * i + k], rsem.at[N_CHIPS * i + k], (x, y, 1 - c)))
        for cp in cps:
            cp.start()
        for cp in cps:
            cp.wait()

    outs = pl.pallas_call(
        body, name="pair_send_" + "_".join(names), in_specs=[ANY] * n, out_specs=[ANY] * n,
        out_shape=[jax.ShapeDtypeStruct(half_shape(name), BF16) for name in names],
        scratch_shapes=[pltpu.SemaphoreType.DMA((N_CHIPS * n,))] * 2,
    )(*[grads[k] for k in names])
    return dict(zip(names, outs))


def pair_start(cfg, grads, after):
    names = list(grads)
    n = len(names)

    def body(*refs):
        srcs, theirs = refs[:n], refs[n:2 * n]
        ssem, rsem = refs[2 * n + 1], refs[2 * n + 2]
        token = refs[-1]
        x, y, c = _me()
        for i, name in enumerate(names):
            for k in range(N_CHIPS):
                _remote(_half(cfg, name, srcs[i], k, 1 - c), theirs[i].at[k],
                        ssem.at[N_CHIPS * i + k], rsem.at[N_CHIPS * i + k], (x, y, 1 - c)).start()
        token[...] = jnp.zeros_like(token)

    def half_shape(name):
        _, nr, _, nc = _slab(cfg, name, 0)
        return (N_CHIPS, nr // 2, nc)

    ops = [_hbm(grads[k]) for k in names] + [_hbm(lax.empty(half_shape(k), BF16)) for k in names]
    outs = pl.pallas_call(
        body, name="pair_start_" + "_".join(names),
        in_specs=[HBM] * (2 * n) + [ANY],
        out_specs=[SEM, SEM] + [HBM] * (2 * n) + [TOKEN],
        out_shape=[pltpu.SemaphoreType.DMA((N_CHIPS * n,)), pltpu.SemaphoreType.DMA((N_CHIPS * n,))]
        + [pltpu.HBM(a.shape, a.dtype) for a in ops] + [jax.ShapeDtypeStruct((8, LANE), F32)],
        input_output_aliases={i: 2 + i for i in range(2 * n)},
        compiler_params=pltpu.CompilerParams(has_side_effects=EFFECT),
    )(*ops, after)
    return outs[0], outs[1], dict(zip(names, outs[2:2 + n])), dict(zip(names, outs[2 + n:2 + 2 * n])), outs[-1]


def pair_wait(cfg, ssem, rsem, grads, theirs, after):
    names = list(grads)
    n = len(names)

    def body(*refs):
        srcs, theirs_ = refs[:n], refs[n:2 * n]
        ssem_, rsem_ = refs[2 * n], refs[2 * n + 1]
        x, y, c = _me()
        for i, name in enumerate(names):
            for k in range(N_CHIPS):
                cp = _remote(_half(cfg, name, srcs[i], k, 1 - c), theirs_[i].at[k],
                             ssem_.at[N_CHIPS * i + k], rsem_.at[N_CHIPS * i + k], (x, y, 1 - c))
                cp.wait_send()
                cp.wait_recv()

    ops = [grads[k] for k in names] + [theirs[k] for k in names]
    outs = pl.pallas_call(
        body, name="pair_wait_" + "_".join(names),
        in_specs=[HBM] * (2 * n) + [SEM, SEM, ANY], out_specs=[HBM] * (2 * n),
        out_shape=[pltpu.HBM(a.shape, a.dtype) for a in ops],
        input_output_aliases={i: i for i in range(2 * n)},
        compiler_params=pltpu.CompilerParams(has_side_effects=EFFECT),
    )(*ops, ssem, rsem, after)
    return dict(zip(names, outs[:n])), dict(zip(names, outs[n:]))


def pair_sum(cfg, name, grad, theirs, pos):
    _, r, c = theirs.shape
    tm, tc = _tile2(r, c)

    ni, nj = r // tm, c // tc
    total = N_CHIPS * ni * nj

    def body(pos_ref, g_ref, t_ref, o_ref, scr, sem):
        step = (pl.program_id(0) * ni + pl.program_id(1)) * nj + pl.program_id(2)

        def fetch(flat, slot):
            k, rem = flat // (ni * nj), flat % (ni * nj)
            r0, nr, c0, _ = _slab(cfg, name, k)
            rows = pl.ds(pl.multiple_of(r0 + pos_ref[1] * (nr // 2) + (rem // nj) * tm, 16), tm)
            cols = pl.ds(pl.multiple_of(c0 + (rem % nj) * tc, LANE), tc)
            return pltpu.make_async_copy(g_ref.at[rows, cols], scr.at[slot], sem.at[slot])

        @pl.when(step == 0)
        def _():
            fetch(0, 0).start()

        @pl.when(step + 1 < total)
        def _():
            fetch(step + 1, (step + 1) % 2).start()

        fetch(step, step % 2).wait()
        o_ref[...] = (scr[step % 2].astype(F32) + t_ref[...].astype(F32)).astype(BF16)

    blk = pl.BlockSpec((None, tm, tc), lambda k, i, j, p: (k, i, j))
    return pl.pallas_call(
        body, name=f"pair_sum_{name}",
        grid_spec=pltpu.PrefetchScalarGridSpec(
            num_scalar_prefetch=1, grid=(N_CHIPS, ni, nj), in_specs=[ANY, blk], out_specs=blk,
            scratch_shapes=[pltpu.VMEM((2, tm, tc), BF16), pltpu.SemaphoreType.DMA((2,))]),
        out_shape=jax.ShapeDtypeStruct(theirs.shape, BF16),
        compiler_params=_cp(cfg, ("arbitrary",) * 3),
    )(pos, grad, theirs)


def scatter_start(cfg, pres, after):
    names = list(pres)
    n = len(names)

    def body(*refs):
        srcs, lands = refs[:n], refs[n:2 * n]
        ssem, rsem = refs[2 * n + 1], refs[2 * n + 2]
        token = refs[-1]
        x, y, c = _me()
        for i in range(n):
            for j, (px, py) in enumerate(_other_chips(x, y)):
                _remote(srcs[i].at[2 * px + py], lands[i].at[j], ssem.at[3 * i + j], rsem.at[3 * i + j], (px, py, c)).start()
        token[...] = jnp.zeros_like(token)

    lands = [lax.empty((3,) + pres[k].shape[1:], BF16) for k in names]
    ops = [_hbm(a) for a in [pres[k] for k in names] + lands]
    outs = pl.pallas_call(
        body, name="scatter_start_" + "_".join(names),
        in_specs=[HBM] * (2 * n) + [ANY],
        out_specs=[SEM, SEM] + [HBM] * (2 * n) + [TOKEN],
        out_shape=[pltpu.SemaphoreType.DMA((3 * n,)), pltpu.SemaphoreType.DMA((3 * n,))]
        + [pltpu.HBM(a.shape, a.dtype) for a in ops] + [jax.ShapeDtypeStruct((8, LANE), F32)],
        input_output_aliases={i: 2 + i for i in range(2 * n)},
        compiler_params=pltpu.CompilerParams(has_side_effects=EFFECT),
    )(*ops, after)
    return outs[0], outs[1], dict(zip(names, outs[2:2 + n])), dict(zip(names, outs[2 + n:2 + 2 * n])), outs[-1]


def scatter_wait(cfg, ssem, rsem, pres, lands, after):
    names = list(pres)
    n = len(names)

    def body(*refs):
        srcs, lands_ = refs[:n], refs[n:2 * n]
        ssem_, rsem_ = refs[2 * n], refs[2 * n + 1]
        x, y, c = _me()
        for i in range(n):
            for j, (px, py) in enumerate(_other_chips(x, y)):
                cp = _remote(srcs[i].at[2 * px + py], lands_[i].at[j], ssem_.at[3 * i + j], rsem_.at[3 * i + j], (px, py, c))
                cp.wait_send()
                cp.wait_recv()

    ops = [pres[k] for k in names] + [lands[k] for k in names]
    outs = pl.pallas_call(
        body, name="scatter_wait_" + "_".join(names),
        in_specs=[HBM] * (2 * n) + [SEM, SEM, ANY], out_specs=[HBM] * (2 * n),
        out_shape=[pltpu.HBM(a.shape, a.dtype) for a in ops],
        input_output_aliases={i: i for i in range(2 * n)},
        compiler_params=pltpu.CompilerParams(has_side_effects=EFFECT),
    )(*ops, ssem, rsem, after)
    return dict(zip(names, outs[:n])), dict(zip(names, outs[n:]))


def sum_landed(cfg, name, pre, land, pos):
    _, r, c = pre.shape
    tm, tc = _tile2(r, c)
    nrt = r // tm

    def body(pos_ref, p_ref, l_ref, o_ref):
        acc = p_ref[...].astype(F32)
        for j in range(3):
            acc = acc + l_ref[j].astype(F32)
        o_ref[...] = acc

    return pl.pallas_call(
        body, name=f"sum_landed_{name}",
        grid_spec=pltpu.PrefetchScalarGridSpec(
            num_scalar_prefetch=1, grid=(nrt, c // tc),
            in_specs=[pl.BlockSpec((None, tm, tc), lambda i, j, p: (p[0], i, j)),
                      pl.BlockSpec((3, tm, tc), lambda i, j, p: (0, i, j))],
            out_specs=pl.BlockSpec((tm, tc), lambda i, j, p: (p[1] * nrt + i, j))),
        out_shape=jax.ShapeDtypeStruct((2 * r, c), F32), compiler_params=_cp(cfg, ("parallel", "parallel")),
    )(pos, pre, land)


def half_swap(cfg, sums):
    names = list(sums)
    n = len(names)

    def body(*refs):
        outs = refs[n:2 * n]
        ssem, rsem = refs[2 * n:]
        x, y, c = _me()
        cps = [_remote(_rows_half(outs[i], c), _rows_half(outs[i], c), ssem.at[i], rsem.at[i], (x, y, 1 - c))
               for i in range(n)]
        for cp in cps:
            cp.start()
        for i in range(n):
            theirs = _rows_half(outs[i], 1 - c)
            _remote(theirs, theirs, ssem.at[i], rsem.at[i], (x, y, 1 - c)).wait_recv()
        for cp in cps:
            cp.wait_send()

    outs = pl.pallas_call(
        body, name="half_swap_" + "_".join(names), in_specs=[ANY] * n, out_specs=[ANY] * n,
        out_shape=[jax.ShapeDtypeStruct(sums[k].shape, F32) for k in names],
        input_output_aliases={i: i for i in range(n)},
        scratch_shapes=[pltpu.SemaphoreType.DMA((n,))] * 2,
    )(*[sums[k] for k in names])
    return dict(zip(names, outs))


class MeshWeights:
    def __init__(self, cfg, w_sh):
        self.cfg = cfg
        self.pos = jnp.stack([2 * lax.axis_index("x") + lax.axis_index("y"), lax.axis_index("c")]).astype(jnp.int32)
        self.full = {k: cast_into(cfg, k, w_sh[k], self.pos) for k in ("w_in",) + REST}
        self.conv_w = pad_conv_w(cfg, w_sh["conv_w"])
        self.inflight = {}
        self.grads = {}

    def weights_first(self):
        w_in, conv_w, self.token = gather_first(self.cfg, self.full["w_in"], self.conv_w)
        return w_in, conv_w

    def start_rest(self):
        self.rest, token = gather_start(self.cfg, self.full, self.token)
        return token

    def weights_rest(self, group, after):
        names = GROUPS[group]
        ssem, rsem, lands = self.rest[group]
        return gather_finish(self.cfg, names, gather_wait(self.cfg, names, ssem, rsem, lands, after))

    def pair_start(self, grads):
        out = pair_start(self.cfg, grads, jnp.zeros((8, LANE), F32))
        self.pairs = out[:4]
        return out[4]

    def reduce_start(self, grads, after=None):
        theirs = pair_send(self.cfg, grads)
        if after is not None:
            early, early_theirs = pair_wait(self.cfg, *self.pairs, after)
            grads, theirs = {**early, **grads}, {**early_theirs, **theirs}
        pres = {k: pair_sum(self.cfg, k, grads[k], theirs[k], self.pos) for k in grads}
        out = scatter_start(self.cfg, pres, jnp.zeros((8, LANE), F32))
        self.inflight[tuple(sorted(grads))] = out[:4]
        return out[4]

    def reduce_wait(self, names, after):
        cfg = self.cfg
        pres, lands = scatter_wait(cfg, *self.inflight.pop(tuple(sorted(names))), after)
        sums = {k: sum_landed(cfg, k, pres[k], lands[k], self.pos) for k in names}
        self.grads.update(half_swap(cfg, sums))


def allreduce_small(cfg, vec):
    R = vec.shape[0]

    def body(v_ref, o_ref, buf, send_sems, recv_sems):
        x, y, c = _me()
        me = 4 * x + 2 * y + c
        buf[me] = v_ref[...]
        sends = []
        for k in range(1, N_DEV):
            px, py, pc = x ^ (k >> 2), y ^ ((k >> 1) & 1), c ^ (k & 1)
            sends.append(pltpu.make_async_remote_copy(
                src_ref=v_ref, dst_ref=buf.at[me], send_sem=send_sems.at[k], recv_sem=recv_sems.at[k],
                device_id=(px, py, pc), device_id_type=MESH))
        for cp in sends:
            cp.start()
        for k in range(1, N_DEV):
            px, py, pc = x ^ (k >> 2), y ^ ((k >> 1) & 1), c ^ (k & 1)
            pltpu.make_async_remote_copy(
                src_ref=v_ref, dst_ref=buf.at[4 * px + 2 * py + pc], send_sem=send_sems.at[k],
                recv_sem=recv_sems.at[k], device_id=(px, py, pc), device_id_type=MESH).wait_recv()
        for cp in sends:
            cp.wait_send()
        acc = buf[0]
        for j in range(1, N_DEV):
            acc = acc + buf[j]
        o_ref[...] = acc

    return pl.pallas_call(
        body, name="allreduce_small",
        in_specs=[pl.BlockSpec(memory_space=pltpu.VMEM)], out_specs=pl.BlockSpec(memory_space=pltpu.VMEM),
        out_shape=jax.ShapeDtypeStruct((R, LANE), F32),
        scratch_shapes=[pltpu.VMEM((N_DEV, R, LANE), F32), pltpu.SemaphoreType.DMA((N_DEV,)),
                        pltpu.SemaphoreType.DMA((N_DEV,))],
    )(vec)


def adamw(cfg, name, w, m, v, g_parts, tile):
    r, c = w.shape
    tm, tc = tile[0] or r, tile[1] or c
    assert tc == c or all(g.shape[1] == c for g in g_parts)
    n = len(g_parts)
    bc1 = 1.0 - ADAM_B1 ** ADAM_STEP
    bc2 = 1.0 - ADAM_B2 ** ADAM_STEP

    def body(*refs):
        w_ref, m_ref, v_ref = refs[:3]
        g_refs = refs[3:3 + n]
        g_out, d_out, m_out, v_out = refs[3 + n:]
        g = g_refs[0][:, :tc]
        for gr in g_refs[1:]:
            g = g + gr[:, :tc]
        m_new = ADAM_B1 * m_ref[...] + (1.0 - ADAM_B1) * g
        v_new = ADAM_B2 * v_ref[...] + (1.0 - ADAM_B2) * jnp.square(g)
        m_hat = m_new / bc1
        v_hat = v_new / bc2
        g_out[...] = g
        d_out[...] = -ADAM_LR * (m_hat / (jnp.sqrt(v_hat) + ADAM_EPS) + ADAM_WD * w_ref[...])
        m_out[...] = m_new
        v_out[...] = v_new

    blk = pl.BlockSpec((tm, tc), lambda i, j: (i, j))
    return pl.pallas_call(
        body, name=f"adamw_{name}", grid=(r // tm, c // tc),
        in_specs=[blk] * 3 + [pl.BlockSpec((tm, tc if tc < c else g.shape[1]), lambda i, j: (i, j)) for g in g_parts],
        out_specs=[blk] * 4, out_shape=[jax.ShapeDtypeStruct((r, c), F32)] * 4,
        compiler_params=_cp(cfg, ("parallel", "parallel")),
    )(w, m, v, *g_parts)


SMALL_ORDER = ("loss", "g1", "g2", "g3", "g4", "g_sb", "g_dl", "conv_b", "conv_w")


def pack_small(small):
    rows = []
    for k in SMALL_ORDER:
        a = small[k].reshape(-1, LANE)
        rows.append(a)
    flat = jnp.concatenate(rows, axis=0)
    pad = (-flat.shape[0]) % 8
    return jnp.pad(flat, ((0, pad), (0, 0))), [r.shape[0] for r in rows]


def unpack_small(red, small, counts):
    out, at = {}, 0
    for k, n in zip(SMALL_ORDER, counts):
        out[k] = red[at:at + n].reshape(small[k].shape)
        at += n
    return out


def pad_ff(cfg, a):
    r = a.shape[0]
    return jnp.pad(a.reshape(r, N_CHIPS, cfg.FSH), ((0, 0), (0, 0), (0, cfg.FSHP - cfg.FSH))).reshape(r, cfg.FF2P)


def step(cfg, x, target, gains, w_sh, conv_b, m_all, v_all):
    chip = 2 * lax.axis_index("x") + lax.axis_index("y")
    comm = MeshWeights(cfg, w_sh)
    grad_x, small = local_step(cfg, comm, x, target, gains["g1"], gains["g2"], gains["g3"], gains["g4"],
                               gains["g_sb"], gains["g_dl"], pad_ff(cfg, conv_b))

    packed, counts = pack_small(small)
    red = unpack_small(allreduce_small(cfg, packed), small, counts)

    names = ("w_in", "w_out", "w_up", "w_down")
    up_rows = max(t for t in range(SUB, 513, SUB) if cfg.FSH % t == 0)
    tms = dict(w_in=(cfg.TM, None), w_out=(cfg.TM, None), w_up=(up_rows, None), w_down=(None, cfg.TN // 2))
    res = {}
    for n in names:
        res[n] = adamw(cfg, n, w_sh[n], m_all[n], v_all[n], [comm.grads[n]], tms[n])
    g_cw = lax.dynamic_slice_in_dim(red["conv_w"].reshape(3, N_CHIPS, cfg.FSHP), chip, 1, axis=1)[:, 0, :cfg.FSH]
    res["conv_w"] = adamw(cfg, "conv_w", w_sh["conv_w"], m_all["conv_w"], v_all["conv_w"], [g_cw], (None, None))
    g_cb = red["conv_b"].reshape(1, N_CHIPS, cfg.FSHP)[:, :, :cfg.FSH].reshape(1, N_CHIPS * cfg.FSH)
    res["conv_b"] = adamw(cfg, "conv_b", conv_b, m_all["conv_b"], v_all["conv_b"], [g_cb], (None, None))
    for k in ("g1", "g2", "g3", "g4", "g_sb", "g_dl"):
        res[k] = adamw(cfg, k, gains[k], m_all[k], v_all[k], [red[k]], (None, None))
    return red["loss"][0, 0], grad_x, res


PARAMS = ("pre_mix_gain", "post_mix_gain", "pre_ffn_gain", "post_ffn_gain", "w_in", "sb_out_gain", "dil_out_gain",
          "w_out", "w_up", "conv_w", "conv_b", "w_down")
SHORT = dict(pre_mix_gain="g1", post_mix_gain="g2", pre_ffn_gain="g3", post_ffn_gain="g4", sb_out_gain="g_sb",
             dil_out_gain="g_dl", w_in="w_in", w_out="w_out", w_up="w_up", conv_w="conv_w", conv_b="conv_b",
             w_down="w_down")


def kernel(x, pre_mix_gain, post_mix_gain, pre_ffn_gain, post_ffn_gain, w_in, sb_out_gain, dil_out_gain, w_out, w_up, conv_w, conv_b, w_down, loss_target, m_pre_mix_gain, m_post_mix_gain, m_pre_ffn_gain, m_post_ffn_gain, m_w_in, m_sb_out_gain, m_dil_out_gain, m_w_out, m_w_up, m_conv_w, m_conv_b, m_w_down, v_pre_mix_gain, v_post_mix_gain, v_pre_ffn_gain, v_post_ffn_gain, v_w_in, v_sb_out_gain, v_dil_out_gain, v_w_out, v_w_up, v_conv_w, v_conv_b, v_w_down):
    cfg = CFG
    w = dict(zip(PARAMS, (pre_mix_gain, post_mix_gain, pre_ffn_gain, post_ffn_gain, w_in, sb_out_gain, dil_out_gain,
                          w_out, w_up, conv_w, conv_b, w_down)))
    m = dict(zip(PARAMS, (m_pre_mix_gain, m_post_mix_gain, m_pre_ffn_gain, m_post_ffn_gain, m_w_in, m_sb_out_gain,
                          m_dil_out_gain, m_w_out, m_w_up, m_conv_w, m_conv_b, m_w_down)))
    v = dict(zip(PARAMS, (v_pre_mix_gain, v_post_mix_gain, v_pre_ffn_gain, v_post_ffn_gain, v_w_in, v_sb_out_gain,
                          v_dil_out_gain, v_w_out, v_w_up, v_conv_w, v_conv_b, v_w_down)))
    sq = lambda a: a.reshape(a.shape[1:])
    ws = {SHORT[k]: sq(a) if a.ndim == 3 else a for k, a in w.items()}
    ms = {SHORT[k]: sq(a) if a.ndim == 3 else a for k, a in m.items()}
    vs = {SHORT[k]: sq(a) if a.ndim == 3 else a for k, a in v.items()}
    for d in (ws, ms, vs):
        d["w_up"] = d["w_up"].T
    gains = {k: ws[k] for k in ("g1", "g2", "g3", "g4", "g_sb", "g_dl")}
    w_sh = {k: ws[k] for k in ("w_in", "w_out", "w_up", "conv_w", "w_down")}
    loss, grad_x, res = step(cfg, sq(x), sq(loss_target), gains, w_sh, ws["conv_b"], ms, vs)
    res["w_up"] = [a.T for a in res["w_up"]]
    outs = [loss, grad_x.reshape(x.shape)]
    for i in range(4):
        for k in PARAMS:
            outs.append(res[SHORT[k]][i].reshape(w[k].shape))
    return tuple(outs)
```

```python
import functools
import math
from typing import NamedTuple

import jax
import jax.numpy as jnp
from jax import lax
from jax.experimental import pallas as pl
from jax.experimental.pallas import tpu as pltpu

F32 = jnp.float32
BF16 = jnp.bfloat16
MESH = pl.DeviceIdType.MESH

ROPE_THETA = 10000.0
RMS_EPS = 1e-6
ADAM_LR = 0.001
ADAM_B1 = 0.9
ADAM_B2 = 0.999
ADAM_EPS = 1e-08
ADAM_WD = 0.01
ADAM_STEP = 10
GELU_C = math.sqrt(2.0 / math.pi)
NEG_BIG = -1e30
LANE = 128
N_CHIPS = 4
N_DEV = 8


class Cfg(NamedTuple):
    S: int = 2048
    D: int = 2048
    DH: int = 128
    HSB: int = 8
    HDL: int = 8
    QB: int = 128
    SBT: int = 256
    SBH: int = 4
    SBHB: int = 4
    branches: tuple = ((128, 1), (512, 4), (2048, 16))
    FSH: int = 2752
    FSHP: int = 2816
    TM: int = 256
    TNF: int = 256
    FCH: int = 512
    TN: int = 512
    VMEM_MB: int = 56

    @property
    def DSB(self):
        return self.HSB * self.DH

    @property
    def DDL(self):
        return self.HDL * self.DH

    @property
    def DMIX(self):
        return self.DSB + self.DDL

    @property
    def FFP(self):
        return 2 * self.FSHP

    @property
    def FF2P(self):
        return 4 * self.FSHP


CFG = Cfg()


def _cp(cfg, sem=None):
    return pltpu.CompilerParams(dimension_semantics=sem, vmem_limit_bytes=cfg.VMEM_MB * 2**20)


def _dot(a, b):
    return jnp.dot(a, b, preferred_element_type=F32)


def _dot_nt(a, b):
    return lax.dot_general(a, b, (((1,), (1,)), ((), ())), preferred_element_type=F32)


def _dot_tn(a, b):
    return lax.dot_general(a, b, (((0,), (0,)), ((), ())), preferred_element_type=F32)


def _dot_split(x, u):
    hi = x.astype(BF16)
    lo = (x - hi.astype(F32)).astype(BF16)
    return _dot(hi, u) + _dot(lo, u)


def _rstd(x):
    return lax.rsqrt(jnp.mean(x * x, axis=-1, keepdims=True) + RMS_EPS)


def _rms_bwd(dy, x, g):
    r = _rstd(x)
    xh = x * r
    dxh = dy * g
    dx = r * (dxh - xh * jnp.mean(dxh * xh, axis=-1, keepdims=True))
    return dx, dy * xh


def _gelu(x):
    t = jnp.tanh(GELU_C * (x + 0.044715 * (x * x * x)))
    return 0.5 * x * (1.0 + t), t


def _gelu_grad(x, t):
    return 0.5 * (1.0 + t) + 0.5 * x * (1.0 - t * t) * (GELU_C * (1.0 + 3 * 0.044715 * (x * x)))


def _row(cfg, w):
    return pl.BlockSpec((cfg.TM, w), lambda i: (i, 0))


def _vec(w):
    return pl.BlockSpec((1, w), lambda i: (0, 0))


def rms_fwd(cfg, x, g):
    S, D = x.shape

    def body(x_ref, g_ref, h_ref):
        xv = x_ref[...]
        h_ref[...] = (xv * _rstd(xv) * g_ref[...]).astype(BF16)

    return pl.pallas_call(
        body, name="rms_fwd", grid=(S // cfg.TM,),
        in_specs=[_row(cfg, D), _vec(D)], out_specs=_row(cfg, D),
        out_shape=jax.ShapeDtypeStruct((S, D), BF16), compiler_params=_cp(cfg, ("parallel",)),
    )(x, g)


def mid_fwd(cfg, x, mo, g_post, g_pre):
    S, D = x.shape

    def body(x_ref, mo_ref, gp_ref, gn_ref, x1_ref, h2_ref):
        mo_v = mo_ref[...]
        x1 = x_ref[...] + mo_v * _rstd(mo_v) * gp_ref[...]
        x1_ref[...] = x1
        h2_ref[...] = (x1 * _rstd(x1) * gn_ref[...]).astype(BF16)

    return pl.pallas_call(
        body, name="mid_fwd", grid=(S // cfg.TM,),
        in_specs=[_row(cfg, D), _row(cfg, D), _vec(D), _vec(D)],
        out_specs=[_row(cfg, D), _row(cfg, D)],
        out_shape=[jax.ShapeDtypeStruct((S, D), F32), jax.ShapeDtypeStruct((S, D), BF16)],
        compiler_params=_cp(cfg, ("parallel",)),
    )(x, mo, g_post, g_pre)


def final_fwd_bwd(cfg, x1, f, g_post, target):
    S, D = x1.shape

    def body(x1_ref, f_ref, g_ref, t_ref, dout_ref, df_ref, dg_ref, loss_ref):
        @pl.when(pl.program_id(0) == 0)
        def _():
            dg_ref[...] = jnp.zeros_like(dg_ref)
            loss_ref[...] = jnp.zeros_like(loss_ref)

        fv = f_ref[...]
        g = g_ref[...]
        out = x1_ref[...] + fv * _rstd(fv) * g
        err = out - t_ref[...]
        loss_ref[...] += 0.5 * jnp.sum(jnp.mean(err * err, axis=-1, keepdims=True), axis=0, keepdims=True)
        dout = err * (1.0 / D)
        dout_ref[...] = dout
        df, dgx = _rms_bwd(dout, fv, g)
        df_ref[...] = df.astype(BF16)
        dg_ref[...] += jnp.sum(dgx, axis=0, keepdims=True)

    return pl.pallas_call(
        body, name="final_fwd_bwd", grid=(S // cfg.TM,),
        in_specs=[_row(cfg, D), _row(cfg, D), _vec(D), _row(cfg, D)],
        out_specs=[_row(cfg, D), _row(cfg, D), _vec(D), _vec(LANE)],
        out_shape=[jax.ShapeDtypeStruct((S, D), F32), jax.ShapeDtypeStruct((S, D), BF16),
                   jax.ShapeDtypeStruct((1, D), F32), jax.ShapeDtypeStruct((1, LANE), F32)],
        compiler_params=_cp(cfg, ("arbitrary",)),
    )(x1, f, g_post, target)


def mid_bwd(cfg, dh2, x1, g_pre, dout, mo, g_post):
    S, D = x1.shape

    def body(dh_ref, x1_ref, gn_ref, do_ref, mo_ref, gp_ref, dx1_ref, dmo_ref, dgn_ref, dgp_ref):
        @pl.when(pl.program_id(0) == 0)
        def _():
            dgn_ref[...] = jnp.zeros_like(dgn_ref)
            dgp_ref[...] = jnp.zeros_like(dgp_ref)

        dx, dgx = _rms_bwd(dh_ref[...], x1_ref[...], gn_ref[...])
        dx1 = do_ref[...] + dx
        dx1_ref[...] = dx1
        dgn_ref[...] += jnp.sum(dgx, axis=0, keepdims=True)
        dmo, dgy = _rms_bwd(dx1, mo_ref[...], gp_ref[...])
        dmo_ref[...] = dmo.astype(BF16)
        dgp_ref[...] += jnp.sum(dgy, axis=0, keepdims=True)

    return pl.pallas_call(
        body, name="mid_bwd", grid=(S // cfg.TM,),
        in_specs=[_row(cfg, D), _row(cfg, D), _vec(D), _row(cfg, D), _row(cfg, D), _vec(D)],
        out_specs=[_row(cfg, D), _row(cfg, D), _vec(D), _vec(D)],
        out_shape=[jax.ShapeDtypeStruct((S, D), F32), jax.ShapeDtypeStruct((S, D), BF16),
                   jax.ShapeDtypeStruct((1, D), F32), jax.ShapeDtypeStruct((1, D), F32)],
        compiler_params=_cp(cfg, ("arbitrary",)),
    )(dh2, x1, g_pre, dout, mo, g_post)


def first_bwd(cfg, dh1, x, g_pre, dx1):
    S, D = x.shape

    def body(dh_ref, x_ref, g_ref, r_ref, dx_ref, dg_ref):
        @pl.when(pl.program_id(0) == 0)
        def _():
            dg_ref[...] = jnp.zeros_like(dg_ref)

        dx, dgx = _rms_bwd(dh_ref[...], x_ref[...], g_ref[...])
        dx_ref[...] = r_ref[...] + dx
        dg_ref[...] += jnp.sum(dgx, axis=0, keepdims=True)

    return pl.pallas_call(
        body, name="first_bwd", grid=(S // cfg.TM,),
        in_specs=[_row(cfg, D), _row(cfg, D), _vec(D), _row(cfg, D)],
        out_specs=[_row(cfg, D), _vec(D)],
        out_shape=[jax.ShapeDtypeStruct((S, D), F32), jax.ShapeDtypeStruct((1, D), F32)],
        compiler_params=_cp(cfg, ("arbitrary",)),
    )(dh1, x, g_pre, dx1)


def _mm(cfg, name, a, b, *, nt, a_spec, b_spec, o_spec, grid, out_shape, acc_shape, dep=None):
    nk = grid[-1]
    dot = _dot_nt if nt else _dot
    deps = [] if dep is None else [dep]

    def body(a_ref, b_ref, *rest):
        o_ref, acc_ref = rest[-2:]
        k = pl.program_id(len(grid) - 1)
        part = dot(a_ref[...], b_ref[...])
        if deps:
            part = part + rest[0][0:1, 0:1]
        if nk == 1:
            o_ref[...] = part.astype(o_ref.dtype)
            return

        @pl.when(k == 0)
        def _():
            acc_ref[...] = part

        @pl.when(k > 0)
        def _():
            acc_ref[...] += part

        @pl.when(k == nk - 1)
        def _():
            o_ref[...] = acc_ref[...].astype(o_ref.dtype)

    sem = ("parallel",) * (len(grid) - 1) + ("arbitrary",)
    dep_specs = [pl.BlockSpec((8, LANE), lambda *_: (0, 0))] * len(deps)
    return pl.pallas_call(
        body, name=name, grid=grid, in_specs=[a_spec, b_spec] + dep_specs, out_specs=o_spec, out_shape=out_shape,
        scratch_shapes=[pltpu.VMEM(acc_shape, F32)], compiler_params=_cp(cfg, sem),
    )(a, b, *deps)


def _mm_tn(cfg, name, a, b, *, a_spec, b_spec, o_spec, grid, out_shape):
    def body(a_ref, b_ref, o_ref):
        o_ref[...] = _dot_tn(a_ref[...], b_ref[...]).astype(o_ref.dtype)

    return pl.pallas_call(
        body, name=name, grid=grid, in_specs=[a_spec, b_spec], out_specs=o_spec, out_shape=out_shape,
        compiler_params=_cp(cfg, ("parallel",) * len(grid)),
    )(a, b)


def qkv_proj(cfg, h1, w_in, cos2, sin2):
    S, D = h1.shape
    tn = 2 * cfg.DH
    per = cfg.DSB // tn
    assert cfg.DSB == cfg.DDL
    nblk = 6 * per

    def body(a_ref, b_ref, c_ref, s_ref, o_ref):
        j = pl.program_id(0)
        acc = _dot(a_ref[...], b_ref[...])
        rope = jnp.logical_and(j >= 3 * per, j < 5 * per)

        @pl.when(rope)
        def _():
            for c in range(tn // cfg.DH):
                xh = acc[:, c * cfg.DH:(c + 1) * cfg.DH]
                o_ref[:, c * cfg.DH:(c + 1) * cfg.DH] = (
                    xh * c_ref[...] + pltpu.roll(xh, cfg.DH // 2, 1) * s_ref[...]).astype(BF16)

        @pl.when(jnp.logical_not(rope))
        def _():
            o_ref[...] = acc.astype(BF16)

    return pl.pallas_call(
        body, name="qkv_proj", grid=(nblk,),
        in_specs=[pl.BlockSpec((S, D), lambda j: (0, 0)), pl.BlockSpec((D, tn), lambda j: (0, j)),
                  pl.BlockSpec((S, cfg.DH), lambda j: (0, 0)), pl.BlockSpec((S, cfg.DH), lambda j: (0, 0))],
        out_specs=pl.BlockSpec((None, S, tn), lambda j: (j // per, 0, j % per)),
        out_shape=jax.ShapeDtypeStruct((6, S, cfg.DSB), BF16),
        compiler_params=_cp(cfg, ("parallel",)),
    )(h1, w_in, cos2, sin2)


def _sb_tile(cfg, q, k, valid):
    z = _dot_nt(q, k) * (cfg.DH ** -0.5)
    lb = jnp.minimum(z, 0.0) - jnp.log1p(jnp.exp(-jnp.abs(z)))
    lk = lb - z
    return lb, (lk if valid is None else jnp.where(valid, lk, 0.0))


def _masked(valid, x):
    return x if valid is None else jnp.where(valid, x, 0.0)


def sb_fwd(cfg, qkv3):
    S, QB, DH, NH = cfg.S, cfg.SBT, cfg.DH, cfg.SBH

    def body(q_ref, k_ref, v_ref, o_ref, t_ref):
        row = lax.broadcasted_iota(jnp.int32, (QB, QB), 0)
        col = lax.broadcasted_iota(jnp.int32, (QB, QB), 1)
        u_after = (row > col).astype(BF16)
        causal = col < row
        heads = [slice(h * DH, (h + 1) * DH) for h in range(NH)]

        def q_loop(qb, _):
            rows = pl.ds(pl.multiple_of(qb * QB, QB), QB)
            qs = [q_ref[rows, hd] for hd in heads]

            def tile(kb, carry, valid):
                krows = pl.ds(pl.multiple_of(kb * QB, QB), QB)
                lbk = [_sb_tile(cfg, q, k_ref[krows, hd], valid) for q, hd in zip(qs, heads)]
                rems = [_dot_split(lk, u_after) for _, lk in lbk]
                aa = [_masked(valid, jnp.exp(lb + rem + c)).astype(BF16) for (lb, _), rem, (_, c) in zip(lbk, rems, carry)]
                return tuple((o_acc + _dot(a, v_ref[krows, hd]), c + jnp.sum(lk, axis=1, keepdims=True))
                             for a, hd, (_, lk), (o_acc, c) in zip(aa, heads, lbk, carry))

            carry = tile(qb, ((jnp.zeros((QB, DH), F32), jnp.zeros((QB, 1), F32)),) * NH, causal)
            carry = lax.fori_loop(0, qb, lambda i, cr: tile(qb - 1 - i, cr, None), carry)
            for hd, (o_acc, c) in zip(heads, carry):
                o_ref[rows, hd] = o_acc
                t_ref[rows, hd] = jnp.broadcast_to(c, (QB, DH))
            return 0

        lax.fori_loop(0, S // QB, q_loop, 0)

    def spec(i):
        return pl.BlockSpec((None, S, NH * DH), lambda h: (i, 0, h))

    return pl.pallas_call(
        body, name="sb_fwd", grid=(cfg.HSB // NH,),
        in_specs=[spec(0), spec(1), spec(2)],
        out_specs=[pl.BlockSpec((S, NH * DH), lambda h: (0, h))] * 2,
        out_shape=[jax.ShapeDtypeStruct((S, cfg.DSB), F32)] * 2,
        compiler_params=_cp(cfg, ("parallel",)),
    )(qkv3, qkv3, qkv3)


def sb_bwd(cfg, qkv3, do_sb, tsum):
    S, QB, DH, NH = cfg.S, cfg.SBT, cfg.DH, cfg.SBHB
    scale = DH ** -0.5

    def body(q_ref, k_ref, v_ref, do_ref, t_ref, d_ref, dk_acc, dv_acc):
        dk_acc[...] = jnp.zeros_like(dk_acc)
        dv_acc[...] = jnp.zeros_like(dv_acc)
        row = lax.broadcasted_iota(jnp.int32, (QB, QB), 0)
        col = lax.broadcasted_iota(jnp.int32, (QB, QB), 1)
        u_upto = (row <= col).astype(BF16)
        u_before = (row < col).astype(BF16)
        causal = col < row
        heads = [slice(h * DH, (h + 1) * DH) for h in range(NH)]

        def q_loop(qb, _):
            rows = pl.ds(pl.multiple_of(qb * QB, QB), QB)
            qs = [q_ref[rows, hd] for hd in heads]
            dos = [do_ref[rows, hd] for hd in heads]
            totals = [t_ref[rows, hd.start:hd.start + 1] for hd in heads]

            def tile(kb, carry, valid):
                krows = pl.ds(pl.multiple_of(kb * QB, QB), QB)
                ks = [k_ref[krows, hd] for hd in heads]
                lbk = [_sb_tile(cfg, q, k, valid) for q, k in zip(qs, ks)]
                das = [_dot_nt(do, v_ref[krows, hd]) for do, hd in zip(dos, heads)]
                pins = [_dot_split(lk, u_upto) for _, lk in lbk]
                aa = [_masked(valid, jnp.exp(lb + (tot - pc - pin)))
                      for (lb, _), tot, (_, pc, _), pin in zip(lbk, totals, carry, pins)]
                gs = [a * da for a, da in zip(aa, das)]
                for a, do, hd in zip(aa, dos, heads):
                    dv_acc[krows, hd] += _dot_tn(a.astype(BF16), do)
                cums = [gc + _dot(g.astype(BF16), u_before) for g, (_, _, gc) in zip(gs, carry)]
                dzs = [(_masked(valid, g - jnp.exp(lb) * (g + cum)) * scale).astype(BF16)
                       for g, (lb, _), cum in zip(gs, lbk, cums)]
                for dz, q, hd in zip(dzs, qs, heads):
                    dk_acc[krows, hd] += _dot_tn(dz, q)
                return tuple((dq + _dot(dz, k), pc + jnp.sum(lk, axis=1, keepdims=True), gc + jnp.sum(g, axis=1, keepdims=True))
                             for dz, k, (_, lk), g, (dq, pc, gc) in zip(dzs, ks, lbk, gs, carry))

            z1 = jnp.zeros((QB, 1), F32)
            carry = lax.fori_loop(0, qb, lambda kb, cr: tile(kb, cr, None), ((jnp.zeros((QB, DH), F32), z1, z1),) * NH)
            for hd, (dq_acc, _, _) in zip(heads, tile(qb, carry, causal)):
                d_ref[0, rows, hd] = dq_acc.astype(BF16)
            return 0

        lax.fori_loop(0, S // QB, q_loop, 0)
        d_ref[1, :, :] = dk_acc[...].astype(BF16)
        d_ref[2, :, :] = dv_acc[...].astype(BF16)

    def spec(i):
        return pl.BlockSpec((None, S, NH * DH), lambda h: (i, 0, h))

    hd_spec = pl.BlockSpec((S, NH * DH), lambda h: (0, h))
    return pl.pallas_call(
        body, name="sb_bwd", grid=(cfg.HSB // NH,),
        in_specs=[spec(0), spec(1), spec(2), hd_spec, hd_spec],
        out_specs=pl.BlockSpec((3, S, NH * DH), lambda h: (0, 0, h)),
        out_shape=jax.ShapeDtypeStruct((6, S, cfg.DSB), BF16),
        scratch_shapes=[pltpu.VMEM((S, NH * DH), F32), pltpu.VMEM((S, NH * DH), F32)],
        compiler_params=_cp(cfg, ("parallel",)),
    )(qkv3, qkv3, qkv3, do_sb, tsum)


def _band_mask(cfg, n, n_back):
    QB = cfg.QB
    qi = lax.broadcasted_iota(jnp.int32, (QB, 2 * QB), 0)
    kj = lax.broadcasted_iota(jnp.int32, (QB, 2 * QB), 1)
    dist = QB + qi - kj
    return (dist >= 0) & (dist <= n_back) & jnp.logical_or(n > 0, kj >= QB)


def _sub_rows(start, n, dil):
    if dil > 1:
        return pl.ds(start, n, stride=dil)
    return pl.ds(start if isinstance(start, int) else pl.multiple_of(start, 8), n)


def _stage_residues(cfg, dil, pairs):
    QB, L = cfg.QB, cfg.S // dil
    for src, dst in pairs:
        for r in range(dil):
            dst[pl.ds(r * (QB + L), QB), :] = jnp.zeros((QB, cfg.DH), BF16)
            dst[pl.ds(r * (QB + L) + QB, L), :] = src[_sub_rows(r, L, dil), :].astype(BF16)


def _staged_rows(cfg):
    return cfg.S + cfg.QB * max(d for _, d in cfg.branches)


def _lane_value(x):
    return jnp.max(x, axis=1, keepdims=True)


def dil_fwd(cfg, qkv3):
    S, QB, DH = cfg.S, cfg.QB, cfg.DH
    scale = DH ** -0.5
    nb = len(cfg.branches)
    mix_rows = min(256, S)

    def body(q_ref, k_ref, v_ref, o_ref, lt_ref, qf, kf, vf, kp, vp, *obl):
        obs, lbs = obl[:nb], obl[nb:]
        qf[...] = q_ref[...].astype(F32)
        kf[...] = k_ref[...].astype(F32)
        vf[...] = v_ref[...].astype(F32)
        for b, (window, dil) in enumerate(cfg.branches):
            L, n_back = S // dil, window // dil
            assert n_back <= QB and L % QB == 0
            _stage_residues(cfg, dil, [(kf, kp), (vf, vp)])
            for r in range(dil):
                for n in range(L // QB):
                    rows = _sub_rows(r + n * (QB * dil), QB, dil)
                    band = pl.ds(r * (QB + L) + n * QB, 2 * QB)
                    s = _dot_nt(qf[rows, :].astype(BF16), kp[band, :]) * scale
                    s = jnp.where(_band_mask(cfg, n, n_back), s, NEG_BIG)
                    m = jnp.max(s, axis=1, keepdims=True)
                    p = jnp.exp(s - m)
                    den = jnp.sum(p, axis=1, keepdims=True)
                    obs[b][rows, :] = _dot(p.astype(BF16), vp[band, :]) / den
                    lbs[b][rows, :] = jnp.broadcast_to(m + jnp.log(den), (QB, DH))

        def mix(i, _):
            rows = pl.ds(pl.multiple_of(i * mix_rows, mix_rows), mix_rows)
            ls = [r[rows, :] for r in lbs]
            m = functools.reduce(jnp.maximum, ls)
            es = [jnp.exp(l - m) for l in ls]
            tot = functools.reduce(jnp.add, es)
            o_ref[rows, :] = functools.reduce(jnp.add, [(e / tot) * r[rows, :] for e, r in zip(es, obs)])
            lt_ref[rows, :] = m + jnp.log(tot)
            return 0

        lax.fori_loop(0, S // mix_rows, mix, 0)

    def spec(i):
        return pl.BlockSpec((None, S, DH), lambda h: (i, 0, h))

    o_spec = pl.BlockSpec((S, DH), lambda h: (0, h))
    return pl.pallas_call(
        body, name="dil_fwd", grid=(cfg.HDL,),
        in_specs=[spec(3), spec(4), spec(5)], out_specs=[o_spec, o_spec],
        out_shape=[jax.ShapeDtypeStruct((S, cfg.DDL), F32)] * 2,
        scratch_shapes=[pltpu.VMEM((S, DH), F32)] * 3 + [pltpu.VMEM((_staged_rows(cfg), DH), BF16)] * 2
        + [pltpu.VMEM((S, DH), F32)] * (2 * nb),
        compiler_params=_cp(cfg, ("parallel",)),
    )(qkv3, qkv3, qkv3)


def dil_bwd(cfg, qkv3, do_dl, delta, lse_tot, cos2, sin2, d_sb3):
    S, QB, DH = cfg.S, cfg.QB, cfg.DH
    scale = DH ** -0.5
    out_rows = min(256, S)

    def body(q_ref, k_ref, v_ref, do_ref, dl_ref, lt_ref, c_ref, s_ref, base_ref, d_ref,
             qf, kf, vf, dof, kp, vp, dkp, dvp, dqn, dkn, dvn):
        qf[...] = q_ref[...].astype(F32)
        kf[...] = k_ref[...].astype(F32)
        vf[...] = v_ref[...].astype(F32)
        dof[...] = do_ref[...].astype(F32)
        for acc in (dqn, dkn, dvn):
            acc[...] = jnp.zeros_like(acc)
        for window, dil in cfg.branches:
            L, n_back = S // dil, window // dil
            reg = QB + L
            _stage_residues(cfg, dil, [(kf, kp), (vf, vp)])
            dkp[pl.ds(0, dil * reg), :] = jnp.zeros((dil * reg, DH), F32)
            dvp[pl.ds(0, dil * reg), :] = jnp.zeros((dil * reg, DH), F32)
            for r in range(dil):
                for n in range(L // QB):
                    rows = _sub_rows(r + n * (QB * dil), QB, dil)
                    band = pl.ds(r * reg + n * QB, 2 * QB)
                    q = qf[rows, :].astype(BF16)
                    do = dof[rows, :].astype(BF16)
                    kb = kp[band, :]
                    s = _dot_nt(q, kb) * scale
                    s = jnp.where(_band_mask(cfg, n, n_back), s, NEG_BIG)
                    p = jnp.exp(s - _lane_value(lt_ref[rows, :]))
                    ds = (p * (_dot_nt(do, vp[band, :]) - _lane_value(dl_ref[rows, :])) * scale).astype(BF16)
                    dqn[rows, :] += _dot(ds, kb)
                    dkp[band, :] += _dot_tn(ds, q)
                    dvp[band, :] += _dot_tn(p.astype(BF16), do)
            for r in range(dil):
                sub = _sub_rows(r, L, dil)
                dkn[sub, :] += dkp[pl.ds(r * reg + QB, L), :]
                dvn[sub, :] += dvp[pl.ds(r * reg + QB, L), :]

        def finish(i, _):
            rows = pl.ds(pl.multiple_of(i * out_rows, out_rows), out_rows)
            c, sn = c_ref[rows, :], s_ref[rows, :]
            for j, acc in enumerate((dqn, dkn)):
                d = acc[rows, :]
                d_ref[j, rows, :] = (d * c + pltpu.roll(d * sn, DH // 2, 1)).astype(BF16)
            d_ref[2, rows, :] = dvn[rows, :].astype(BF16)
            return 0

        lax.fori_loop(0, S // out_rows, finish, 0)

    def spec(i):
        return pl.BlockSpec((None, S, DH), lambda h: (i, 0, h))

    hd = pl.BlockSpec((S, DH), lambda h: (0, h))
    tab = pl.BlockSpec((S, DH), lambda h: (0, 0))
    ns = _staged_rows(cfg)
    return pl.pallas_call(
        body, name="dil_bwd", grid=(cfg.HDL,),
        in_specs=[spec(3), spec(4), spec(5), hd, hd, hd, tab, tab, ANY],
        out_specs=pl.BlockSpec((3, S, DH), lambda h: (1, 0, h)),
        out_shape=jax.ShapeDtypeStruct((6, S, cfg.DDL), BF16),
        input_output_aliases={8: 0},
        scratch_shapes=[pltpu.VMEM((S, DH), F32)] * 4 + [pltpu.VMEM((ns, DH), BF16)] * 2
        + [pltpu.VMEM((ns, DH), F32)] * 2 + [pltpu.VMEM((S, DH), F32)] * 3,
        compiler_params=_cp(cfg, ("parallel",)),
    )(qkv3, qkv3, qkv3, do_dl, delta, lse_tot, cos2, sin2, d_sb3)


def combine_fwd(cfg, o_sb, o_dl, g_sb, g_dl):
    S, DH = cfg.S, cfg.DH

    def head_norm(o, g):
        return o * lax.rsqrt(jnp.mean(o * o, axis=-1, keepdims=True) + RMS_EPS) * g

    def body(osb_ref, odl_ref, gsb_ref, gdl_ref, mix_ref):
        for h in range(cfg.HSB):
            c = slice(h * DH, (h + 1) * DH)
            mix_ref[:, c] = head_norm(osb_ref[:, c], gsb_ref[:, c]).astype(BF16)
        for h in range(cfg.HDL):
            c = slice(h * DH, (h + 1) * DH)
            mix_ref[:, cfg.DSB + h * DH:cfg.DSB + (h + 1) * DH] = head_norm(odl_ref[:, c], gdl_ref[:, c]).astype(BF16)

    return pl.pallas_call(
        body, name="combine_fwd", grid=(S // cfg.TM,),
        in_specs=[_row(cfg, cfg.DSB), _row(cfg, cfg.DDL), _vec(cfg.DSB), _vec(cfg.DDL)],
        out_specs=_row(cfg, cfg.DMIX), out_shape=jax.ShapeDtypeStruct((S, cfg.DMIX), BF16),
        compiler_params=_cp(cfg, ("parallel",)),
    )(o_sb, o_dl, g_sb, g_dl)


def combine_bwd(cfg, dmix, o_sb, o_dl, g_sb, g_dl):
    S, DH = cfg.S, cfg.DH

    def body(dm_ref, osb_ref, odl_ref, gsb_ref, gdl_ref, dsb_ref, ddl_ref, dl_ref, dgsb_ref, dgdl_ref):
        @pl.when(pl.program_id(0) == 0)
        def _():
            dgsb_ref[...] = jnp.zeros_like(dgsb_ref)
            dgdl_ref[...] = jnp.zeros_like(dgdl_ref)

        for h in range(cfg.HSB):
            c = slice(h * DH, (h + 1) * DH)
            dx, dgx = _rms_bwd(dm_ref[:, c], osb_ref[:, c], gsb_ref[:, c])
            dsb_ref[:, c] = dx.astype(BF16)
            dgsb_ref[:, c] += jnp.sum(dgx, axis=0, keepdims=True)
        for h in range(cfg.HDL):
            c = slice(h * DH, (h + 1) * DH)
            o = odl_ref[:, c]
            dx, dgx = _rms_bwd(dm_ref[:, cfg.DSB + h * DH:cfg.DSB + (h + 1) * DH], o, gdl_ref[:, c])
            ddl_ref[:, c] = dx.astype(BF16)
            dl_ref[:, c] = jnp.broadcast_to(jnp.sum(dx * o, axis=-1, keepdims=True), dx.shape)
            dgdl_ref[:, c] += jnp.sum(dgx, axis=0, keepdims=True)

    return pl.pallas_call(
        body, name="combine_bwd", grid=(S // cfg.TM,),
        in_specs=[_row(cfg, cfg.DMIX), _row(cfg, cfg.DSB), _row(cfg, cfg.DDL), _vec(cfg.DSB), _vec(cfg.DDL)],
        out_specs=[_row(cfg, cfg.DSB), _row(cfg, cfg.DDL), _row(cfg, cfg.DDL), _vec(cfg.DSB), _vec(cfg.DDL)],
        out_shape=[jax.ShapeDtypeStruct((S, cfg.DSB), BF16), jax.ShapeDtypeStruct((S, cfg.DDL), BF16),
                   jax.ShapeDtypeStruct((S, cfg.DDL), F32), jax.ShapeDtypeStruct((1, cfg.DSB), F32),
                   jax.ShapeDtypeStruct((1, cfg.DDL), F32)],
        compiler_params=_cp(cfg, ("arbitrary",)),
    )(dmix, o_sb, o_dl, g_sb, g_dl)


SUB = 8


def _shift_down(u, prev, j):
    rolled = pltpu.roll(u, j, 0)
    row = lax.broadcasted_iota(jnp.int32, (SUB, u.shape[1]), 0)
    head = jnp.where(row >= j, rolled[:SUB], pltpu.roll(prev, j, 0))
    return jnp.concatenate([head, rolled[SUB:]], axis=0)


def _shift_up(u, nxt, j):
    n = u.shape[0]
    rolled = pltpu.roll(u, n - j, 0)
    row = lax.broadcasted_iota(jnp.int32, (SUB, u.shape[1]), 0)
    tail = jnp.where(row < SUB - j, rolled[n - SUB:], pltpu.roll(nxt, SUB - j, 0))
    return jnp.concatenate([rolled[:n - SUB], tail], axis=0)


def _conv(u, s1, s2, cw, cb):
    return u * cw[2:3, :] + s1 * cw[1:2, :] + s2 * cw[0:1, :] + cb


def _chunk_rows(cfg):
    ch = min(cfg.FCH, cfg.S)
    return ch, cfg.S // ch


def ffn_fwd(cfg, h2, w_up, conv_w, conv_b):
    S, D = h2.shape
    tn, nt = cfg.TNF, cfg.FFP // cfg.TNF
    ch, nch = _chunk_rows(cfg)

    def body(h_ref, wg_ref, wv_ref, cwg_ref, cwv_ref, cbg_ref, cbv_ref, u_ref, y_ref):
        prev = [jnp.zeros((SUB, tn), F32)] * 2
        pending = None
        for ci in range(nch + 1):
            if ci < nch:
                h = h_ref[pl.ds(ci * ch, ch), :]
                us_next = [_dot_nt(h, wg_ref[...]), _dot_nt(h, wv_ref[...])]
            if pending is not None:
                rows, us = pending
                cs = []
                for i, (cw_ref, cb_ref) in enumerate(((cwg_ref, cbg_ref), (cwv_ref, cbv_ref))):
                    u_ref[i, rows, :] = us[i]
                    cs.append(_conv(us[i], _shift_down(us[i], prev[i], 1), _shift_down(us[i], prev[i], 2),
                                    cw_ref[...], cb_ref[...]))
                y_ref[rows, :] = (_gelu(cs[0])[0] * cs[1]).astype(BF16)
                prev = [u[ch - SUB:] for u in us]
            pending = (pl.ds(ci * ch, ch), us_next) if ci < nch else None

    return pl.pallas_call(
        body, name="ffn_fwd", grid=(nt,),
        in_specs=[pl.BlockSpec((S, D), lambda n: (0, 0)),
                  pl.BlockSpec((tn, D), lambda n: (n, 0)), pl.BlockSpec((tn, D), lambda n: (n + nt, 0)),
                  pl.BlockSpec((3, tn), lambda n: (0, n)), pl.BlockSpec((3, tn), lambda n: (0, n + nt)),
                  pl.BlockSpec((1, tn), lambda n: (0, n)), pl.BlockSpec((1, tn), lambda n: (0, n + nt))],
        out_specs=[pl.BlockSpec((2, S, tn), lambda n: (0, 0, n)), pl.BlockSpec((S, tn), lambda n: (0, n))],
        out_shape=[jax.ShapeDtypeStruct((2, S, cfg.FFP), F32), jax.ShapeDtypeStruct((S, cfg.FFP), BF16)],
        compiler_params=_cp(cfg, ("parallel",)),
    )(h2, w_up, w_up, conv_w, conv_w, conv_b, conv_b)


def ffn_bwd(cfg, df, h2, w_down, u, conv_w, conv_b):
    S, D = df.shape
    tn, nt = cfg.TNF, cfg.FFP // cfg.TNF

    ch, nch = _chunk_rows(cfg)

    def body(df_ref, h_ref, wd_ref, u_ref, cwg_ref, cwv_ref, cbg_ref, cbv_ref,
             du_ref, dwd_ref, dwu_ref, dcw_ref, dcb_ref):
        cws = (cwg_ref[...], cwv_ref[...])
        cbs = (cbg_ref[...], cbv_ref[...])
        zero = jnp.zeros((SUB, tn), F32)
        nxt = [zero, zero]
        dws = [[jnp.zeros((1, tn), F32)] * 4 for _ in range(2)]
        dwd = jnp.zeros((tn, D), F32)
        dwu = [jnp.zeros((tn, D), F32)] * 2
        order = list(reversed(range(nch)))
        dys, done = {}, {}
        for step in range(nch + 2):
            if step < nch:
                ci = order[step]
                dys[ci] = _dot_nt(df_ref[pl.ds(ci * ch, ch), :], wd_ref[...])
            if 1 <= step <= nch:
                ci = order[step - 1]
                rows = pl.ds(ci * ch, ch)
                dy = dys.pop(ci)
                us, s1, s2, cs = [], [], [], []
                for i in range(2):
                    u = u_ref[i, rows, :]
                    prev = u_ref[i, pl.ds(ci * ch - SUB, SUB), :] if ci else zero
                    us.append(u)
                    s1.append(_shift_down(u, prev, 1))
                    s2.append(_shift_down(u, prev, 2))
                    cs.append(_conv(u, s1[i], s2[i], cws[i], cbs[i]))
                gl, t = _gelu(cs[0])
                dcs = (dy * cs[1] * _gelu_grad(cs[0], t), dy * gl)
                dus = []
                for i, dc in enumerate(dcs):
                    du = dc * cws[i][2:3, :] + _shift_up(dc, nxt[i], 1) * cws[i][1:2, :] + _shift_up(dc, nxt[i], 2) * cws[i][0:1, :]
                    dus.append(du.astype(BF16))
                    du_ref[i, rows, :] = dus[i]
                    for j, tap in enumerate((s2[i], s1[i], us[i])):
                        dws[i][j] = dws[i][j] + jnp.sum(dc * tap, axis=0, keepdims=True)
                    dws[i][3] = dws[i][3] + jnp.sum(dc, axis=0, keepdims=True)
                nxt = [dc[:SUB] for dc in dcs]
                done[ci] = ((gl * cs[1]).astype(BF16), dus)
            if step >= 2:
                ci = order[step - 2]
                rows = pl.ds(ci * ch, ch)
                yv, dus = done.pop(ci)
                dwd = dwd + _dot_tn(yv, df_ref[rows, :])
                hv = h_ref[rows, :]
                dwu = [acc + _dot_tn(du, hv) for acc, du in zip(dwu, dus)]
        dwd_ref[...] = dwd.astype(BF16)
        for i in range(2):
            dwu_ref[i] = dwu[i].astype(BF16)
            for j in range(3):
                dcw_ref[i, j:j + 1, :] = dws[i][j]
            dcb_ref[i] = dws[i][3]

    whole = pl.BlockSpec((S, D), lambda n: (0, 0), pipeline_mode=pl.Buffered(1))
    du, dwd, dwu, dcw, dcb = pl.pallas_call(
        body, name="ffn_bwd", grid=(nt,),
        in_specs=[whole, whole, pl.BlockSpec((tn, D), lambda n: (n, 0)),
                  pl.BlockSpec((2, S, tn), lambda n: (0, 0, n)),
                  pl.BlockSpec((3, tn), lambda n: (0, n)), pl.BlockSpec((3, tn), lambda n: (0, n + nt)),
                  pl.BlockSpec((1, tn), lambda n: (0, n)), pl.BlockSpec((1, tn), lambda n: (0, n + nt))],
        out_specs=[pl.BlockSpec((2, S, tn), lambda n: (0, 0, n)), pl.BlockSpec((tn, D), lambda n: (n, 0)),
                   pl.BlockSpec((2, tn, D), lambda n: (0, n, 0)),
                   pl.BlockSpec((2, 3, tn), lambda n: (0, 0, n)), pl.BlockSpec((2, 1, tn), lambda n: (0, 0, n))],
        out_shape=[jax.ShapeDtypeStruct((2, S, cfg.FFP), BF16), jax.ShapeDtypeStruct((cfg.FFP, D), BF16),
                   jax.ShapeDtypeStruct((2, cfg.FFP, D), BF16),
                   jax.ShapeDtypeStruct((2, 3, cfg.FFP), F32), jax.ShapeDtypeStruct((2, 1, cfg.FFP), F32)],
        compiler_params=_cp(cfg, ("parallel",)),
    )(df, h2, w_down, u, conv_w, conv_w, conv_b, conv_b)
    return du, dwd, dwu.reshape(cfg.FF2P, D), dcw, dcb


def rope_tables(cfg):
    inv_freq = ROPE_THETA ** (-jnp.arange(0, cfg.DH, 2, dtype=F32) / cfg.DH)
    ang = jnp.arange(cfg.S, dtype=F32)[:, None] * inv_freq[None, :]
    cos, sin = jnp.cos(ang), jnp.sin(ang)
    return jnp.concatenate([cos, cos], axis=1), jnp.concatenate([-sin, sin], axis=1)


class LocalWeights:
    def __init__(self, w_in, w_out, w_up, conv_w, w_down):
        self.w = (w_in, w_out, w_up, conv_w, w_down)
        self.grads = {}

    def first_start(self):
        return None

    def weights_first(self, after):
        return self.w[0], self.w[3]

    def start_rest(self, after):
        return None

    def weights_rest(self, group, after):
        return ((self.w[1],), None) if group == 0 else ((), None)

    def forwarded(self, group, after):
        return (self.w[2],) if group == 0 else (self.w[4],)

    def pair_start(self, grads):
        self.grads.update(grads)
        return None

    def reduce_start(self, grads, after=None):
        self.grads.update(grads)
        return None

    def reduce_wait(self, names, after):
        pass


def _after(a, token):
    return a if token is None else a + token[0, 0].astype(a.dtype)


def local_step(cfg, comm, x, target, g1, g2, g3, g4, g_sb, g_dl, conv_b):
    S, D = cfg.S, cfg.D
    cos2, sin2 = rope_tables(cfg)
    full = lambda r, c: pl.BlockSpec((r, c), lambda j, k: (0, 0))

    h1 = rms_fwd(cfg, x, _after(g1, comm.first_start()))
    w_in, conv_w = comm.weights_first(after=h1)
    qkv3 = qkv_proj(cfg, h1, w_in, _after(cos2, comm.start_rest(after=w_in)), sin2)
    o_sb, tsum = sb_fwd(cfg, qkv3)
    o_dl, lse_tot = dil_fwd(cfg, qkv3)
    mixed = combine_fwd(cfg, o_sb, o_dl, g_sb, g_dl)
    (w_out,), token = comm.weights_rest(0, after=mixed)
    tn = cfg.TN
    mo = _mm(cfg, "mix_out", mixed, w_out, nt=False, grid=(D // tn, 1),
             a_spec=full(S, cfg.DMIX), b_spec=pl.BlockSpec((cfg.DMIX, tn), lambda j, k: (0, j)),
             o_spec=pl.BlockSpec((S, tn), lambda j, k: (0, j)),
             out_shape=jax.ShapeDtypeStruct((S, D), F32), acc_shape=(8, LANE), dep=token)
    x1, h2 = mid_fwd(cfg, x, mo, g2, g3)
    w_up, = comm.forwarded(0, after=h2)
    _, token = comm.weights_rest(1, after=h2)
    u, y = ffn_fwd(cfg, h2, w_up, conv_w, _after(conv_b, token))
    w_down, = comm.forwarded(1, after=y)
    tk = cfg.FFP // 2
    f = _mm(cfg, "ffn_down", y, w_down, nt=False, grid=(D // tn, cfg.FFP // tk),
            a_spec=pl.BlockSpec((S, tk), lambda j, k: (0, k)), b_spec=pl.BlockSpec((tk, tn), lambda j, k: (k, j)),
            o_spec=pl.BlockSpec((S, tn), lambda j, k: (0, j)),
            out_shape=jax.ShapeDtypeStruct((S, D), F32), acc_shape=(S, tn))
    dout, df, dg4, loss = final_fwd_bwd(cfg, x1, f, g4, target)

    du, dw_down, dw_up, dconv_w, dconv_b = ffn_bwd(cfg, df, h2, w_down, u, conv_w, conv_b)
    kt = cfg.FFP // tk
    dh2 = _mm(cfg, "d_h2", du, w_up, nt=False, grid=(D // tn, 2 * kt),
              a_spec=pl.BlockSpec((None, S, tk), lambda j, k: (k // kt, 0, k % kt)),
              b_spec=pl.BlockSpec((tk, tn), lambda j, k: (k, j)),
              o_spec=pl.BlockSpec((S, tn), lambda j, k: (0, j)),
              out_shape=jax.ShapeDtypeStruct((S, D), F32), acc_shape=(S, tn),
              dep=comm.pair_start(dict(w_down=dw_down, w_up=dw_up)))
    dx1, dmo, dg3, dg2 = mid_bwd(cfg, dh2, x1, g3, dout, mo, g2)

    dmix = _mm(cfg, "d_mixed", dmo, w_out, nt=True, grid=(cfg.DMIX // tn, 1),
               a_spec=full(S, D), b_spec=pl.BlockSpec((tn, D), lambda j, k: (j, 0)),
               o_spec=pl.BlockSpec((S, tn), lambda j, k: (0, j)),
               out_shape=jax.ShapeDtypeStruct((S, cfg.DMIX), F32), acc_shape=(8, LANE))
    dw_out = _mm_tn(cfg, "d_w_out", mixed, dmo, grid=(D // tn,),
                    a_spec=pl.BlockSpec((S, cfg.DMIX), lambda j: (0, 0)),
                    b_spec=pl.BlockSpec((S, tn), lambda j: (0, j)),
                    o_spec=pl.BlockSpec((cfg.DMIX, tn), lambda j: (0, j)),
                    out_shape=jax.ShapeDtypeStruct((cfg.DMIX, D), BF16))
    token = comm.reduce_start(dict(w_out=dw_out), after=dw_out)
    do_sb, do_dl, delta, dg_sb, dg_dl = combine_bwd(cfg, dmix, o_sb, o_dl, _after(g_sb, token), g_dl)
    d_sb3 = sb_bwd(cfg, qkv3, do_sb, tsum)
    dqkv3 = dil_bwd(cfg, qkv3, do_dl, delta, lse_tot, cos2, sin2, d_sb3)
    comm.reduce_wait(("w_out", "w_up", "w_down"), after=dqkv3)
    tkq = min(tn, cfg.DSB)
    kq = cfg.DSB // tkq
    dw_in = _mm_tn(cfg, "d_w_in", h1, dqkv3, grid=(6 * kq,),
                   a_spec=pl.BlockSpec((S, D), lambda j: (0, 0)),
                   b_spec=pl.BlockSpec((None, S, tkq), lambda j: (j // kq, 0, j % kq)),
                   o_spec=pl.BlockSpec((D, tkq), lambda j: (0, j)),
                   out_shape=jax.ShapeDtypeStruct((D, 6 * cfg.DSB), BF16))
    token = comm.reduce_start(dict(w_in=dw_in))
    dh1 = _mm(cfg, "d_h1", dqkv3, w_in, nt=True, grid=(D // tn, 6),
              a_spec=pl.BlockSpec((None, S, cfg.DSB), lambda j, k: (k, 0, 0)),
              b_spec=pl.BlockSpec((tn, cfg.DSB), lambda j, k: (j, k)),
              o_spec=pl.BlockSpec((S, tn), lambda j, k: (0, j)),
              out_shape=jax.ShapeDtypeStruct((S, D), F32), acc_shape=(S, tn), dep=token)
    grad_x, dg1 = first_bwd(cfg, dh1, x, g1, dx1)
    small = dict(loss=loss, g1=dg1, g2=dg2, g3=dg3, g4=dg4, g_sb=dg_sb, g_dl=dg_dl,
                 conv_b=dconv_b.reshape(1, cfg.FF2P), conv_w=dconv_w.transpose(1, 0, 2).reshape(3, cfg.FF2P))
    return grad_x, small


ANY = pl.BlockSpec(memory_space=pl.ANY)


def _me():
    return lax.axis_index("x"), lax.axis_index("y"), lax.axis_index("c")


def _other_chips(x, y):
    return [(1 - x, y), (x, 1 - y), (1 - x, 1 - y)]


def pad_conv_w(cfg, conv_w, pos):
    r, c = conv_w.shape

    def body(pos_ref, w_ref, full_ref, scr, sem):
        scr[:, :c] = w_ref[...]
        scr[:, c:] = jnp.zeros((r, cfg.FSHP - c), F32)
        cols = pl.ds(pl.multiple_of(pos_ref[0] * cfg.FSHP, LANE), cfg.FSHP)
        cp = pltpu.make_async_copy(scr, full_ref.at[:, cols], sem)
        cp.start()
        cp.wait()

    return pl.pallas_call(
        body, name="pad_conv_w",
        grid_spec=pltpu.PrefetchScalarGridSpec(
            num_scalar_prefetch=1, grid=(1,), in_specs=[pl.BlockSpec((r, c), lambda i, p: (0, 0))], out_specs=ANY,
            scratch_shapes=[pltpu.VMEM((r, cfg.FSHP), F32), pltpu.SemaphoreType.DMA]),
        out_shape=jax.ShapeDtypeStruct(_full_shape(cfg, "conv_w"), F32),
    )(pos, conv_w)


def _tile2(r, c):
    return (256, c) if r % 256 == 0 else (r, 512 if c % 512 == 0 else c)


def cast_into(cfg, name, w, pos, dep=None):
    r, c = w.shape
    _, nr, _, nc = _slab(cfg, name, 0)
    tm, tc = _tile2(r, c)
    wr = nr if tm == r else tm
    assert nc == c and (nr == r or tm == r)
    gap = cfg.FSHP - cfg.FSH if name == "w_down" else 0
    deps = [] if dep is None else [dep]

    def body(pos_ref, w_ref, *rest):
        full_ref, scr, sem = rest[len(deps):]
        tile = w_ref[...]
        if deps:
            tile = tile + rest[0][0:1, 0:1]
        scr[pl.ds(0, tm), :] = tile.astype(BF16)
        if wr > tm:
            scr[pl.ds(tm, wr - tm), :] = jnp.zeros((wr - tm, tc), BF16)
        r0, _, c0, _ = _slab(cfg, name, pos_ref[0])
        rows = pl.ds(pl.multiple_of(r0 + pl.program_id(0) * tm, 16), wr)
        cols = pl.ds(pl.multiple_of(c0 + pl.program_id(1) * tc, LANE), tc)
        cps = [pltpu.make_async_copy(scr.at[pl.ds(0, wr), :], full_ref.at[rows, cols], sem.at[0])]
        if gap:
            scr[pl.ds(wr, gap), :] = jnp.zeros((gap, tc), BF16)
            for h in range(2):
                pad_rows = pl.ds(h * cfg.FSHP + cfg.FSH, gap)
                cps.append(pltpu.make_async_copy(scr.at[pl.ds(wr, gap), :], full_ref.at[pad_rows, cols], sem.at[1 + h]))
        for cp in cps:
            cp.start()
        for cp in cps:
            cp.wait()

    return pl.pallas_call(
        body, name=f"cast_{name}",
        grid_spec=pltpu.PrefetchScalarGridSpec(
            num_scalar_prefetch=1, grid=(r // tm, c // tc),
            in_specs=[pl.BlockSpec((tm, tc), lambda i, j, p: (i, j))]
            + [pl.BlockSpec((8, LANE), lambda i, j, p: (0, 0))] * len(deps), out_specs=ANY,
            scratch_shapes=[pltpu.VMEM((wr + gap, tc), BF16), pltpu.SemaphoreType.DMA((3,))]),
        out_shape=jax.ShapeDtypeStruct(_full_shape(cfg, name), BF16),
        compiler_params=_cp(cfg, ("arbitrary", "arbitrary")),
    )(pos, w, *deps)


HBM = pl.BlockSpec(memory_space=pltpu.HBM)
SEM = pl.BlockSpec(memory_space=pltpu.SEMAPHORE)
TOKEN = pl.BlockSpec(memory_space=pltpu.VMEM)
EFFECT = pltpu.SideEffectType.DATAFLOW_SIDE_EFFECTING


def _slab(cfg, name, k):
    D = cfg.D
    if name == "w_in":
        cin = 6 * cfg.DSB // N_CHIPS
        return 0, D, k * cin, cin
    if name == "w_out":
        rout = cfg.DMIX // N_CHIPS
        return k * rout, rout, 0, D
    if name == "w_up":
        return k * cfg.FSHP, cfg.FSHP, 0, D
    if name == "conv_w":
        return 0, 3, k * cfg.FSHP, cfg.FSHP
    rdn = cfg.FSH // 2
    return (k // 2) * cfg.FSHP + (k % 2) * rdn, rdn, 0, D


def _full_shape(cfg, name):
    return dict(w_in=(cfg.D, 6 * cfg.DSB), w_out=(cfg.DMIX, cfg.D), w_up=(cfg.FF2P, cfg.D), w_down=(cfg.FFP, cfg.D),
                conv_w=(3, cfg.FF2P))[name]


def _half(cfg, name, ref, k, h):
    r0, nr, c0, nc = _slab(cfg, name, k)
    if name == "conv_w":
        return ref.at[:, pl.ds(c0, nc)]
    return ref.at[pl.ds(r0 + h * (nr // 2), nr // 2), pl.ds(c0, nc)]


def _rows_half(ref, h):
    nr = ref.shape[0] // 2
    return ref.at[pl.ds(h * nr, nr), :]


def _remote(src, dst, send_sem, recv_sem, dev):
    return pltpu.make_async_remote_copy(src_ref=src, dst_ref=dst, send_sem=send_sem, recv_sem=recv_sem,
                                        device_id=dev, device_id_type=MESH)


REST = ("w_out", "w_up", "w_down")
FIRST = (("w_in", "conv_w"),)
GROUPS = (("w_out", "w_up"), ("w_down",))


def _hbm(a):
    return pltpu.with_memory_space_constraint(a, pltpu.HBM)


def gather_start(cfg, tag, groups, fulls, after):
    order = [k for names in groups for k in names]
    n, ng = len(order), len(groups)

    def body(*refs):
        lands = dict(zip(order, refs[:n]))
        sems = refs[n + 1:n + 1 + 2 * ng]
        token = refs[-1]
        x, y, c = _me()
        me = 2 * x + y
        for g, names in enumerate(groups):
            for i, name in enumerate(names):
                mine = _half(cfg, name, lands[name], me, c)
                for j, (px, py) in enumerate(_other_chips(x, y)):
                    _remote(mine, mine, sems[2 * g].at[3 * i + j], sems[2 * g + 1].at[3 * i + j], (px, py, c)).start()
        token[...] = jnp.zeros_like(token)

    ops = [_hbm(fulls[k]) for k in order]
    sem_shapes = [pltpu.SemaphoreType.DMA((3 * len(names),)) for names in groups for _ in range(2)]
    outs = pl.pallas_call(
        body, name=f"gather_start_{tag}",
        in_specs=[HBM] * n + [ANY],
        out_specs=[SEM] * (2 * ng) + [HBM] * n + [TOKEN],
        out_shape=sem_shapes + [pltpu.HBM(a.shape, a.dtype) for a in ops] + [jax.ShapeDtypeStruct((8, LANE), F32)],
        input_output_aliases={i: 2 * ng + i for i in range(n)},
        compiler_params=pltpu.CompilerParams(has_side_effects=EFFECT),
    )(*ops, after)
    thru = dict(zip(order, outs[2 * ng:2 * ng + n]))
    return [(outs[2 * g], outs[2 * g + 1], [thru[k] for k in names]) for g, names in enumerate(groups)], outs[-1]


def gather_wait(cfg, names, ssem, rsem, lands, after):
    n = len(names)

    def body(*refs):
        lands_ = refs[:n]
        ssem_, rsem_ = refs[n], refs[n + 1]
        x, y, c = _me()
        me = 2 * x + y
        for i, name in enumerate(names):
            for j, (px, py) in enumerate(_other_chips(x, y)):
                cp = _remote(_half(cfg, name, lands_[i], me, c), _half(cfg, name, lands_[i], 2 * px + py, c),
                             ssem_.at[3 * i + j], rsem_.at[3 * i + j], (px, py, c))
                cp.wait_send()
                cp.wait_recv()

    return pl.pallas_call(
        body, name="gather_wait_" + "_".join(names),
        in_specs=[HBM] * n + [SEM, SEM, ANY], out_specs=[HBM] * n,
        out_shape=[pltpu.HBM(a.shape, a.dtype) for a in lands],
        input_output_aliases={i: i for i in range(n)},
        compiler_params=pltpu.CompilerParams(has_side_effects=EFFECT),
    )(*lands, ssem, rsem, after)


def gather_finish(cfg, names, lands):
    n = len(names)

    def body(*refs):
        outs = refs[n:2 * n]
        ssem, rsem = refs[2 * n:]
        x, y, c = _me()
        sib = (x, y, 1 - c)
        fwds = []
        for i, name in enumerate(names):
            for j, (px, py) in enumerate(_other_chips(x, y)):
                landed = _half(cfg, name, outs[i], 2 * px + py, c)
                fwds.append(_remote(landed, landed, ssem.at[3 * i + j], rsem.at[3 * i + j], sib))
        for cp in fwds:
            cp.start()
        for i, name in enumerate(names):
            for j, (px, py) in enumerate(_other_chips(x, y)):
                passed = _half(cfg, name, outs[i], 2 * px + py, 1 - c)
                _remote(passed, passed, ssem.at[3 * i + j], rsem.at[3 * i + j], sib).wait_recv()
        for cp in fwds:
            cp.wait_send()

    return pl.pallas_call(
        body, name="gather_finish_" + "_".join(names), in_specs=[ANY] * n, out_specs=[ANY] * n,
        out_shape=[jax.ShapeDtypeStruct(a.shape, a.dtype) for a in lands],
        input_output_aliases={i: i for i in range(n)},
        scratch_shapes=[pltpu.SemaphoreType.DMA((3 * n,)), pltpu.SemaphoreType.DMA((3 * n,))],
    )(*lands)


def forward_start(cfg, names, lands, after):
    n = len(names)

    def body(*refs):
        outs = refs[:n]
        ssem, rsem = refs[n + 1], refs[n + 2]
        token = refs[-1]
        x, y, c = _me()
        for i, name in enumerate(names):
            for j, (px, py) in enumerate(_other_chips(x, y)):
                landed = _half(cfg, name, outs[i], 2 * px + py, c)
                _remote(landed, landed, ssem.at[3 * i + j], rsem.at[3 * i + j], (x, y, 1 - c)).start()
        token[...] = jnp.zeros_like(token)

    ops = [_hbm(a) for a in lands]
    outs = pl.pallas_call(
        body, name="forward_start_" + "_".join(names),
        in_specs=[HBM] * n + [ANY], out_specs=[SEM, SEM] + [HBM] * n + [TOKEN],
        out_shape=[pltpu.SemaphoreType.DMA((3 * n,)), pltpu.SemaphoreType.DMA((3 * n,))]
        + [pltpu.HBM(a.shape, a.dtype) for a in ops] + [jax.ShapeDtypeStruct((8, LANE), F32)],
        input_output_aliases={i: 2 + i for i in range(n)},
        compiler_params=pltpu.CompilerParams(has_side_effects=EFFECT),
    )(*ops, after)
    return outs[0], outs[1], outs[2:2 + n], outs[-1]


def forward_wait(cfg, names, ssem, rsem, lands, after):
    n = len(names)

    def body(*refs):
        outs = refs[:n]
        ssem_, rsem_ = refs[n], refs[n + 1]
        x, y, c = _me()
        for i, name in enumerate(names):
            for j, (px, py) in enumerate(_other_chips(x, y)):
                cp = _remote(_half(cfg, name, outs[i], 2 * px + py, c), _half(cfg, name, outs[i], 2 * px + py, 1 - c),
                             ssem_.at[3 * i + j], rsem_.at[3 * i + j], (x, y, 1 - c))
                cp.wait_send()
                cp.wait_recv()

    return pl.pallas_call(
        body, name="forward_wait_" + "_".join(names),
        in_specs=[HBM] * n + [SEM, SEM, ANY], out_specs=[HBM] * n,
        out_shape=[pltpu.HBM(a.shape, a.dtype) for a in lands],
        input_output_aliases={i: i for i in range(n)},
        compiler_params=pltpu.CompilerParams(has_side_effects=EFFECT),
    )(*lands, ssem, rsem, after)


def pair_send(cfg, grads):
    names = list(grads)
    n = len(names)

    def half_shape(name):
        _, nr, _, nc = _slab(cfg, name, 0)
        return (N_CHIPS, nr // 2, nc)

    def body(*refs):
        srcs, theirs = refs[:n], refs[n:2 * n]
        ssem, rsem = refs[2 * n:]
        x, y, c = _me()
        cps = []
        for i, name in enumerate(names):
            for k in range(N_CHIPS):
                cps.append(_remote(_half(cfg, name, srcs[i], k, 1 - c), theirs[i].at[k],
                                   ssem.at[N_CHIPS * i + k], rsem.at[N_CHIPS * i + k], (x, y, 1 - c)))
        for cp in cps:
            cp.start()
        for cp in cps:
            cp.wait()

    outs = pl.pallas_call(
        body, name="pair_send_" + "_".join(names), in_specs=[ANY] * n, out_specs=[ANY] * n,
        out_shape=[jax.ShapeDtypeStruct(half_shape(name), BF16) for name in names],
        scratch_shapes=[pltpu.SemaphoreType.DMA((N_CHIPS * n,))] * 2,
    )(*[grads[k] for k in names])
    return dict(zip(names, outs))


def pair_start(cfg, grads, after):
    names = list(grads)
    n = len(names)

    def body(*refs):
        srcs, theirs = refs[:n], refs[n:2 * n]
        ssem, rsem = refs[2 * n + 1], refs[2 * n + 2]
        token = refs[-1]
        x, y, c = _me()
        for i, name in enumerate(names):
            for k in range(N_CHIPS):
                _remote(_half(cfg, name, srcs[i], k, 1 - c), theirs[i].at[k],
                        ssem.at[N_CHIPS * i + k], rsem.at[N_CHIPS * i + k], (x, y, 1 - c)).start()
        token[...] = jnp.zeros_like(token)

    def half_shape(name):
        _, nr, _, nc = _slab(cfg, name, 0)
        return (N_CHIPS, nr // 2, nc)

    ops = [_hbm(grads[k]) for k in names] + [_hbm(lax.empty(half_shape(k), BF16)) for k in names]
    outs = pl.pallas_call(
        body, name="pair_start_" + "_".join(names),
        in_specs=[HBM] * (2 * n) + [ANY],
        out_specs=[SEM, SEM] + [HBM] * (2 * n) + [TOKEN],
        out_shape=[pltpu.SemaphoreType.DMA((N_CHIPS * n,)), pltpu.SemaphoreType.DMA((N_CHIPS * n,))]
        + [pltpu.HBM(a.shape, a.dtype) for a in ops] + [jax.ShapeDtypeStruct((8, LANE), F32)],
        input_output_aliases={i: 2 + i for i in range(2 * n)},
        compiler_params=pltpu.CompilerParams(has_side_effects=EFFECT),
    )(*ops, after)
    return outs[0], outs[1], dict(zip(names, outs[2:2 + n])), dict(zip(names, outs[2 + n:2 + 2 * n])), outs[-1]


def pair_wait(cfg, ssem, rsem, grads, theirs, after):
    names = list(grads)
    n = len(names)

    def body(*refs):
        srcs, theirs_ = refs[:n], refs[n:2 * n]
        ssem_, rsem_ = refs[2 * n], refs[2 * n + 1]
        x, y, c = _me()
        for i, name in enumerate(names):
            for k in range(N_CHIPS):
                cp = _remote(_half(cfg, name, srcs[i], k, 1 - c), theirs_[i].at[k],
                             ssem_.at[N_CHIPS * i + k], rsem_.at[N_CHIPS * i + k], (x, y, 1 - c))
                cp.wait_send()
                cp.wait_recv()

    ops = [grads[k] for k in names] + [theirs[k] for k in names]
    outs = pl.pallas_call(
        body, name="pair_wait_" + "_".join(names),
        in_specs=[HBM] * (2 * n) + [SEM, SEM, ANY], out_specs=[HBM] * (2 * n),
        out_shape=[pltpu.HBM(a.shape, a.dtype) for a in ops],
        input_output_aliases={i: i for i in range(2 * n)},
        compiler_params=pltpu.CompilerParams(has_side_effects=EFFECT),
    )(*ops, ssem, rsem, after)
    return dict(zip(names, outs[:n])), dict(zip(names, outs[n:]))


def pair_sum(cfg, name, grad, theirs, pos):
    _, r, c = theirs.shape
    tm, tc = _tile2(r, c)

    ni, nj = r // tm, c // tc
    total = N_CHIPS * ni * nj

    def body(pos_ref, g_ref, t_ref, o_ref, scr, sem):
        step = (pl.program_id(0) * ni + pl.program_id(1)) * nj + pl.program_id(2)

        def fetch(flat, slot):
            k, rem = flat // (ni * nj), flat % (ni * nj)
            r0, nr, c0, _ = _slab(cfg, name, k)
            rows = pl.ds(pl.multiple_of(r0 + pos_ref[1] * (nr // 2) + (rem // nj) * tm, 16), tm)
            cols = pl.ds(pl.multiple_of(c0 + (rem % nj) * tc, LANE), tc)
            return pltpu.make_async_copy(g_ref.at[rows, cols], scr.at[slot], sem.at[slot])

        @pl.when(step == 0)
        def _():
            fetch(0, 0).start()

        @pl.when(step + 1 < total)
        def _():
            fetch(step + 1, (step + 1) % 2).start()

        fetch(step, step % 2).wait()
        o_ref[...] = (scr[step % 2].astype(F32) + t_ref[...].astype(F32)).astype(BF16)

    blk = pl.BlockSpec((None, tm, tc), lambda k, i, j, p: (k, i, j))
    return pl.pallas_call(
        body, name=f"pair_sum_{name}",
        grid_spec=pltpu.PrefetchScalarGridSpec(
            num_scalar_prefetch=1, grid=(N_CHIPS, ni, nj), in_specs=[ANY, blk], out_specs=blk,
            scratch_shapes=[pltpu.VMEM((2, tm, tc), BF16), pltpu.SemaphoreType.DMA((2,))]),
        out_shape=jax.ShapeDtypeStruct(theirs.shape, BF16),
        compiler_params=_cp(cfg, ("arbitrary",) * 3),
    )(pos, grad, theirs)


def scatter_start(cfg, pres, after):
    names = list(pres)
    n = len(names)

    def body(*refs):
        srcs, lands = refs[:n], refs[n:2 * n]
        ssem, rsem = refs[2 * n + 1], refs[2 * n + 2]
        token = refs[-1]
        x, y, c = _me()
        for i in range(n):
            for j, (px, py) in enumerate(_other_chips(x, y)):
                _remote(srcs[i].at[2 * px + py], lands[i].at[j], ssem.at[3 * i + j], rsem.at[3 * i + j], (px, py, c)).start()
        token[...] = jnp.zeros_like(token)

    lands = [lax.empty((3,) + pres[k].shape[1:], BF16) for k in names]
    ops = [_hbm(a) for a in [pres[k] for k in names] + lands]
    outs = pl.pallas_call(
        body, name="scatter_start_" + "_".join(names),
        in_specs=[HBM] * (2 * n) + [ANY],
        out_specs=[SEM, SEM] + [HBM] * (2 * n) + [TOKEN],
        out_shape=[pltpu.SemaphoreType.DMA((3 * n,)), pltpu.SemaphoreType.DMA((3 * n,))]
        + [pltpu.HBM(a.shape, a.dtype) for a in ops] + [jax.ShapeDtypeStruct((8, LANE), F32)],
        input_output_aliases={i: 2 + i for i in range(2 * n)},
        compiler_params=pltpu.CompilerParams(has_side_effects=EFFECT),
    )(*ops, after)
    return outs[0], outs[1], dict(zip(names, outs[2:2 + n])), dict(zip(names, outs[2 + n:2 + 2 * n])), outs[-1]


def scatter_wait(cfg, ssem, rsem, pres, lands, after):
    names = list(pres)
    n = len(names)

    def body(*refs):
        srcs, lands_ = refs[:n], refs[n:2 * n]
        ssem_, rsem_ = refs[2 * n], refs[2 * n + 1]
        x, y, c = _me()
        for i in range(n):
            for j, (px, py) in enumerate(_other_chips(x, y)):
                cp = _remote(srcs[i].at[2 * px + py], lands_[i].at[j], ssem_.at[3 * i + j], rsem_.at[3 * i + j], (px, py, c))
                cp.wait_send()
                cp.wait_recv()

    ops = [pres[k] for k in names] + [lands[k] for k in names]
    outs = pl.pallas_call(
        body, name="scatter_wait_" + "_".join(names),
        in_specs=[HBM] * (2 * n) + [SEM, SEM, ANY], out_specs=[HBM] * (2 * n),
        out_shape=[pltpu.HBM(a.shape, a.dtype) for a in ops],
        input_output_aliases={i: i for i in range(2 * n)},
        compiler_params=pltpu.CompilerParams(has_side_effects=EFFECT),
    )(*ops, ssem, rsem, after)
    return dict(zip(names, outs[:n])), dict(zip(names, outs[n:]))


def sum_landed(cfg, name, pre, land, pos):
    _, r, c = pre.shape
    tm, tc = _tile2(r, c)
    nrt = r // tm

    def body(pos_ref, p_ref, l_ref, o_ref):
        acc = p_ref[...].astype(F32)
        for j in range(3):
            acc = acc + l_ref[j].astype(F32)
        o_ref[...] = acc

    return pl.pallas_call(
        body, name=f"sum_landed_{name}",
        grid_spec=pltpu.PrefetchScalarGridSpec(
            num_scalar_prefetch=1, grid=(nrt, c // tc),
            in_specs=[pl.BlockSpec((None, tm, tc), lambda i, j, p: (p[0], i, j)),
                      pl.BlockSpec((3, tm, tc), lambda i, j, p: (0, i, j))],
            out_specs=pl.BlockSpec((tm, tc), lambda i, j, p: (p[1] * nrt + i, j))),
        out_shape=jax.ShapeDtypeStruct((2 * r, c), F32), compiler_params=_cp(cfg, ("parallel", "parallel")),
    )(pos, pre, land)


def half_swap(cfg, sums):
    names = list(sums)
    n = len(names)

    def body(*refs):
        outs = refs[n:2 * n]
        ssem, rsem = refs[2 * n:]
        x, y, c = _me()
        cps = [_remote(_rows_half(outs[i], c), _rows_half(outs[i], c), ssem.at[i], rsem.at[i], (x, y, 1 - c))
               for i in range(n)]
        for cp in cps:
            cp.start()
        for i in range(n):
            theirs = _rows_half(outs[i], 1 - c)
            _remote(theirs, theirs, ssem.at[i], rsem.at[i], (x, y, 1 - c)).wait_recv()
        for cp in cps:
            cp.wait_send()

    outs = pl.pallas_call(
        body, name="half_swap_" + "_".join(names), in_specs=[ANY] * n, out_specs=[ANY] * n,
        out_shape=[jax.ShapeDtypeStruct(sums[k].shape, F32) for k in names],
        input_output_aliases={i: i for i in range(n)},
        scratch_shapes=[pltpu.SemaphoreType.DMA((n,))] * 2,
    )(*[sums[k] for k in names])
    return dict(zip(names, outs))


class MeshWeights:
    def __init__(self, cfg, w_sh):
        self.cfg = cfg
        self.pos = jnp.stack([2 * lax.axis_index("x") + lax.axis_index("y"), lax.axis_index("c")]).astype(jnp.int32)
        self.w_sh = w_sh
        self.full = {"w_in": cast_into(cfg, "w_in", w_sh["w_in"], self.pos),
                     "conv_w": pad_conv_w(cfg, w_sh["conv_w"], self.pos)}
        self.inflight = {}
        self.forwards = {}
        self.grads = {}

    def first_start(self):
        cfg = self.cfg
        self.first, token = gather_start(cfg, "first", FIRST, self.full, jnp.zeros((8, LANE), F32))
        for k in REST:
            self.full[k] = cast_into(cfg, k, self.w_sh[k], self.pos, dep=token)
        return token

    def weights_first(self, after):
        cfg = self.cfg
        ssem, rsem, lands = self.first[0]
        w_in, conv_w = gather_wait(cfg, FIRST[0], ssem, rsem, lands, after)
        return gather_finish(cfg, ("w_in",), [w_in])[0], conv_w

    def start_rest(self, after):
        self.rest, token = gather_start(self.cfg, "rest", GROUPS, self.full, after)
        return token

    def weights_rest(self, group, after):
        cfg = self.cfg
        names = GROUPS[group]
        ssem, rsem, lands = self.rest[group]
        lands = dict(zip(names, gather_wait(cfg, names, ssem, rsem, lands, after)))
        now = [k for k in names if k == "w_out"]
        later = [k for k in names if k != "w_out"]
        ready = gather_finish(cfg, tuple(now), [lands[k] for k in now]) if now else []
        out = forward_start(cfg, tuple(later), [lands[k] for k in later], ready[0] if ready else jnp.zeros((8, LANE), F32))
        self.forwards[group] = (tuple(later),) + tuple(out[:3])
        return tuple(ready), out[3]

    def forwarded(self, group, after):
        names, ssem, rsem, lands = self.forwards.pop(group)
        return tuple(forward_wait(self.cfg, names, ssem, rsem, lands, after))

    def pair_start(self, grads):
        out = pair_start(self.cfg, grads, jnp.zeros((8, LANE), F32))
        self.pairs = out[:4]
        return out[4]

    def reduce_start(self, grads, after=None):
        theirs = pair_send(self.cfg, grads)
        if after is not None:
            early, early_theirs = pair_wait(self.cfg, *self.pairs, after)
            grads, theirs = {**early, **grads}, {**early_theirs, **theirs}
        pres = {k: pair_sum(self.cfg, k, grads[k], theirs[k], self.pos) for k in grads}
        out = scatter_start(self.cfg, pres, jnp.zeros((8, LANE), F32))
        self.inflight[tuple(sorted(grads))] = out[:4]
        return out[4]

    def reduce_wait(self, names, after):
        cfg = self.cfg
        pres, lands = scatter_wait(cfg, *self.inflight.pop(tuple(sorted(names))), after)
        sums = {k: sum_landed(cfg, k, pres[k], lands[k], self.pos) for k in names}
        self.grads.update(half_swap(cfg, sums))


def allreduce_small(cfg, vec):
    R = vec.shape[0]

    def body(v_ref, o_ref, buf, send_sems, recv_sems):
        x, y, c = _me()
        me = 4 * x + 2 * y + c
        buf[me] = v_ref[...]
        sends = []
        for k in range(1, N_DEV):
            px, py, pc = x ^ (k >> 2), y ^ ((k >> 1) & 1), c ^ (k & 1)
            sends.append(pltpu.make_async_remote_copy(
                src_ref=v_ref, dst_ref=buf.at[me], send_sem=send_sems.at[k], recv_sem=recv_sems.at[k],
                device_id=(px, py, pc), device_id_type=MESH))
        for cp in sends:
            cp.start()
        for k in range(1, N_DEV):
            px, py, pc = x ^ (k >> 2), y ^ ((k >> 1) & 1), c ^ (k & 1)
            pltpu.make_async_remote_copy(
                src_ref=v_ref, dst_ref=buf.at[4 * px + 2 * py + pc], send_sem=send_sems.at[k],
                recv_sem=recv_sems.at[k], device_id=(px, py, pc), device_id_type=MESH).wait_recv()
        for cp in sends:
            cp.wait_send()
        acc = buf[0]
        for j in range(1, N_DEV):
            acc = acc + buf[j]
        o_ref[...] = acc

    return pl.pallas_call(
        body, name="allreduce_small",
        in_specs=[pl.BlockSpec(memory_space=pltpu.VMEM)], out_specs=pl.BlockSpec(memory_space=pltpu.VMEM),
        out_shape=jax.ShapeDtypeStruct((R, LANE), F32),
        scratch_shapes=[pltpu.VMEM((N_DEV, R, LANE), F32), pltpu.SemaphoreType.DMA((N_DEV,)),
                        pltpu.SemaphoreType.DMA((N_DEV,))],
    )(vec)


def adamw(cfg, name, w, m, v, g_parts, tile):
    r, c = w.shape
    tm, tc = tile[0] or r, tile[1] or c
    assert tc == c or all(g.shape[1] == c for g in g_parts)
    n = len(g_parts)
    bc1 = 1.0 - ADAM_B1 ** ADAM_STEP
    bc2 = 1.0 - ADAM_B2 ** ADAM_STEP

    def body(*refs):
        w_ref, m_ref, v_ref = refs[:3]
        g_refs = refs[3:3 + n]
        g_out, d_out, m_out, v_out = refs[3 + n:]
        g = g_refs[0][:, :tc]
        for gr in g_refs[1:]:
            g = g + gr[:, :tc]
        m_new = ADAM_B1 * m_ref[...] + (1.0 - ADAM_B1) * g
        v_new = ADAM_B2 * v_ref[...] + (1.0 - ADAM_B2) * jnp.square(g)
        m_hat = m_new / bc1
        v_hat = v_new / bc2
        g_out[...] = g
        d_out[...] = -ADAM_LR * (m_hat / (jnp.sqrt(v_hat) + ADAM_EPS) + ADAM_WD * w_ref[...])
        m_out[...] = m_new
        v_out[...] = v_new

    blk = pl.BlockSpec((tm, tc), lambda i, j: (i, j))
    return pl.pallas_call(
        body, name=f"adamw_{name}", grid=(r // tm, c // tc),
        in_specs=[blk] * 3 + [pl.BlockSpec((tm, tc if tc < c else g.shape[1]), lambda i, j: (i, j)) for g in g_parts],
        out_specs=[blk] * 4, out_shape=[jax.ShapeDtypeStruct((r, c), F32)] * 4,
        compiler_params=_cp(cfg, ("parallel", "parallel")),
    )(w, m, v, *g_parts)


SMALL_ORDER = ("loss", "g1", "g2", "g3", "g4", "g_sb", "g_dl", "conv_b", "conv_w")


def pack_small(small):
    rows = []
    for k in SMALL_ORDER:
        a = small[k].reshape(-1, LANE)
        rows.append(a)
    flat = jnp.concatenate(rows, axis=0)
    pad = (-flat.shape[0]) % 8
    return jnp.pad(flat, ((0, pad), (0, 0))), [r.shape[0] for r in rows]


def unpack_small(red, small, counts):
    out, at = {}, 0
    for k, n in zip(SMALL_ORDER, counts):
        out[k] = red[at:at + n].reshape(small[k].shape)
        at += n
    return out


def pad_ff(cfg, a):
    r = a.shape[0]
    return jnp.pad(a.reshape(r, N_CHIPS, cfg.FSH), ((0, 0), (0, 0), (0, cfg.FSHP - cfg.FSH))).reshape(r, cfg.FF2P)


def step(cfg, x, target, gains, w_sh, conv_b, m_all, v_all):
    chip = 2 * lax.axis_index("x") + lax.axis_index("y")
    comm = MeshWeights(cfg, w_sh)
    grad_x, small = local_step(cfg, comm, x, target, gains["g1"], gains["g2"], gains["g3"], gains["g4"],
                               gains["g_sb"], gains["g_dl"], pad_ff(cfg, conv_b))

    packed, counts = pack_small(small)
    summed = allreduce_small(cfg, packed)
    comm.reduce_wait(("w_in",), after=summed)
    red = unpack_small(summed, small, counts)

    names = ("w_in", "w_out", "w_up", "w_down")
    up_rows = max(t for t in range(SUB, 513, SUB) if cfg.FSH % t == 0)
    tms = dict(w_in=(cfg.TM, None), w_out=(cfg.TM, None), w_up=(up_rows, None), w_down=(None, cfg.TN // 2))
    res = {}
    for n in names:
        res[n] = adamw(cfg, n, w_sh[n], m_all[n], v_all[n], [comm.grads[n]], tms[n])
    g_cw = lax.dynamic_slice_in_dim(red["conv_w"].reshape(3, N_CHIPS, cfg.FSHP), chip, 1, axis=1)[:, 0, :cfg.FSH]
    res["conv_w"] = adamw(cfg, "conv_w", w_sh["conv_w"], m_all["conv_w"], v_all["conv_w"], [g_cw], (None, None))
    g_cb = red["conv_b"].reshape(1, N_CHIPS, cfg.FSHP)[:, :, :cfg.FSH].reshape(1, N_CHIPS * cfg.FSH)
    res["conv_b"] = adamw(cfg, "conv_b", conv_b, m_all["conv_b"], v_all["conv_b"], [g_cb], (None, None))
    for k in ("g1", "g2", "g3", "g4", "g_sb", "g_dl"):
        res[k] = adamw(cfg, k, gains[k], m_all[k], v_all[k], [red[k]], (None, None))
    return red["loss"][0, 0], grad_x, res


PARAMS = ("pre_mix_gain", "post_mix_gain", "pre_ffn_gain", "post_ffn_gain", "w_in", "sb_out_gain", "dil_out_gain",
          "w_out", "w_up", "conv_w", "conv_b", "w_down")
SHORT = dict(pre_mix_gain="g1", post_mix_gain="g2", pre_ffn_gain="g3", post_ffn_gain="g4", sb_out_gain="g_sb",
             dil_out_gain="g_dl", w_in="w_in", w_out="w_out", w_up="w_up", conv_w="conv_w", conv_b="conv_b",
             w_down="w_down")


def kernel(x, pre_mix_gain, post_mix_gain, pre_ffn_gain, post_ffn_gain, w_in, sb_out_gain, dil_out_gain, w_out, w_up, conv_w, conv_b, w_down, loss_target, m_pre_mix_gain, m_post_mix_gain, m_pre_ffn_gain, m_post_ffn_gain, m_w_in, m_sb_out_gain, m_dil_out_gain, m_w_out, m_w_up, m_conv_w, m_conv_b, m_w_down, v_pre_mix_gain, v_post_mix_gain, v_pre_ffn_gain, v_post_ffn_gain, v_w_in, v_sb_out_gain, v_dil_out_gain, v_w_out, v_w_up, v_conv_w, v_conv_b, v_w_down):
    cfg = CFG
    w = dict(zip(PARAMS, (pre_mix_gain, post_mix_gain, pre_ffn_gain, post_ffn_gain, w_in, sb_out_gain, dil_out_gain,
                          w_out, w_up, conv_w, conv_b, w_down)))
    m = dict(zip(PARAMS, (m_pre_mix_gain, m_post_mix_gain, m_pre_ffn_gain, m_post_ffn_gain, m_w_in, m_sb_out_gain,
                          m_dil_out_gain, m_w_out, m_w_up, m_conv_w, m_conv_b, m_w_down)))
    v = dict(zip(PARAMS, (v_pre_mix_gain, v_post_mix_gain, v_pre_ffn_gain, v_post_ffn_gain, v_w_in, v_sb_out_gain,
                          v_dil_out_gain, v_w_out, v_w_up, v_conv_w, v_conv_b, v_w_down)))
    sq = lambda a: a.reshape(a.shape[1:])
    ws = {SHORT[k]: sq(a) if a.ndim == 3 else a for k, a in w.items()}
    ms = {SHORT[k]: sq(a) if a.ndim == 3 else a for k, a in m.items()}
    vs = {SHORT[k]: sq(a) if a.ndim == 3 else a for k, a in v.items()}
    for d in (ws, ms, vs):
        d["w_up"] = d["w_up"].T
    gains = {k: ws[k] for k in ("g1", "g2", "g3", "g4", "g_sb", "g_dl")}
    w_sh = {k: ws[k] for k in ("w_in", "w_out", "w_up", "conv_w", "w_down")}
    loss, grad_x, res = step(cfg, sq(x), sq(loss_target), gains, w_sh, ws["conv_b"], ms, vs)
    res["w_up"] = [a.T for a in res["w_up"]]
    outs = [loss, grad_x.reshape(x.shape)]
    for i in range(4):
        for k in PARAMS:
            outs.append(res[SHORT[k]][i].reshape(w[k].shape))
    return tuple(outs)
```

```python
import functools
import math
from typing import NamedTuple

import jax
import jax.numpy as jnp
from jax import lax
from jax.experimental import pallas as pl
from jax.experimental.pallas import tpu as pltpu

F32 = jnp.float32
BF16 = jnp.bfloat16
MESH = pl.DeviceIdType.MESH

ROPE_THETA = 10000.0
RMS_EPS = 1e-6
ADAM_LR = 0.001
ADAM_B1 = 0.9
ADAM_B2 = 0.999
ADAM_EPS = 1e-08
ADAM_WD = 0.01
ADAM_STEP = 10
GELU_C = math.sqrt(2.0 / math.pi)
NEG_BIG = -1e30
LANE = 128
N_CHIPS = 4
N_DEV = 8


class Cfg(NamedTuple):
    S: int = 2048
    D: int = 2048
    DH: int = 128
    HSB: int = 8
    HDL: int = 8
    QB: int = 128
    SBT: int = 256
    SBH: int = 4
    SBHB: int = 4
    branches: tuple = ((128, 1), (512, 4), (2048, 16))
    FSH: int = 2752
    FSHP: int = 2816
    TM: int = 256
    TNF: int = 256
    FCH: int = 512
    TN: int = 512
    VMEM_MB: int = 56

    @property
    def DSB(self):
        return self.HSB * self.DH

    @property
    def DDL(self):
        return self.HDL * self.DH

    @property
    def DMIX(self):
        return self.DSB + self.DDL

    @property
    def FFP(self):
        return 2 * self.FSHP

    @property
    def FF2P(self):
        return 4 * self.FSHP


CFG = Cfg()


def _cp(cfg, sem=None):
    return pltpu.CompilerParams(dimension_semantics=sem, vmem_limit_bytes=cfg.VMEM_MB * 2**20)


def _dot(a, b):
    return jnp.dot(a, b, preferred_element_type=F32)


def _dot_nt(a, b):
    return lax.dot_general(a, b, (((1,), (1,)), ((), ())), preferred_element_type=F32)


def _dot_tn(a, b):
    return lax.dot_general(a, b, (((0,), (0,)), ((), ())), preferred_element_type=F32)


def _dot_split(x, u):
    hi = x.astype(BF16)
    lo = (x - hi.astype(F32)).astype(BF16)
    return _dot(hi, u) + _dot(lo, u)


def _rstd(x):
    return lax.rsqrt(jnp.mean(x * x, axis=-1, keepdims=True) + RMS_EPS)


def _rms_bwd(dy, x, g):
    r = _rstd(x)
    xh = x * r
    dxh = dy * g
    dx = r * (dxh - xh * jnp.mean(dxh * xh, axis=-1, keepdims=True))
    return dx, dy * xh


def _gelu(x):
    t = jnp.tanh(GELU_C * (x + 0.044715 * (x * x * x)))
    return 0.5 * x * (1.0 + t), t


def _gelu_grad(x, t):
    return 0.5 * (1.0 + t) + 0.5 * x * (1.0 - t * t) * (GELU_C * (1.0 + 3 * 0.044715 * (x * x)))


def _row(cfg, w):
    return pl.BlockSpec((cfg.TM, w), lambda i: (i, 0))


def _vec(w):
    return pl.BlockSpec((1, w), lambda i: (0, 0))


def rms_fwd(cfg, x, g):
    S, D = x.shape

    def body(x_ref, g_ref, h_ref):
        xv = x_ref[...]
        h_ref[...] = (xv * _rstd(xv) * g_ref[...]).astype(BF16)

    return pl.pallas_call(
        body, name="rms_fwd", grid=(S // cfg.TM,),
        in_specs=[_row(cfg, D), _vec(D)], out_specs=_row(cfg, D),
        out_shape=jax.ShapeDtypeStruct((S, D), BF16), compiler_params=_cp(cfg, ("parallel",)),
    )(x, g)


def mid_fwd(cfg, x, mo, g_post, g_pre):
    S, D = x.shape

    def body(x_ref, mo_ref, gp_ref, gn_ref, x1_ref, h2_ref):
        mo_v = mo_ref[...]
        x1 = x_ref[...] + mo_v * _rstd(mo_v) * gp_ref[...]
        x1_ref[...] = x1
        h2_ref[...] = (x1 * _rstd(x1) * gn_ref[...]).astype(BF16)

    return pl.pallas_call(
        body, name="mid_fwd", grid=(S // cfg.TM,),
        in_specs=[_row(cfg, D), _row(cfg, D), _vec(D), _vec(D)],
        out_specs=[_row(cfg, D), _row(cfg, D)],
        out_shape=[jax.ShapeDtypeStruct((S, D), F32), jax.ShapeDtypeStruct((S, D), BF16)],
        compiler_params=_cp(cfg, ("parallel",)),
    )(x, mo, g_post, g_pre)


def final_fwd_bwd(cfg, x1, f, g_post, target):
    S, D = x1.shape

    def body(x1_ref, f_ref, g_ref, t_ref, dout_ref, df_ref, dg_ref, loss_ref):
        @pl.when(pl.program_id(0) == 0)
        def _():
            dg_ref[...] = jnp.zeros_like(dg_ref)
            loss_ref[...] = jnp.zeros_like(loss_ref)

        fv = f_ref[...]
        g = g_ref[...]
        out = x1_ref[...] + fv * _rstd(fv) * g
        err = out - t_ref[...]
        loss_ref[...] += 0.5 * jnp.sum(jnp.mean(err * err, axis=-1, keepdims=True), axis=0, keepdims=True)
        dout = err * (1.0 / D)
        dout_ref[...] = dout
        df, dgx = _rms_bwd(dout, fv, g)
        df_ref[...] = df.astype(BF16)
        dg_ref[...] += jnp.sum(dgx, axis=0, keepdims=True)

    return pl.pallas_call(
        body, name="final_fwd_bwd", grid=(S // cfg.TM,),
        in_specs=[_row(cfg, D), _row(cfg, D), _vec(D), _row(cfg, D)],
        out_specs=[_row(cfg, D), _row(cfg, D), _vec(D), _vec(LANE)],
        out_shape=[jax.ShapeDtypeStruct((S, D), F32), jax.ShapeDtypeStruct((S, D), BF16),
                   jax.ShapeDtypeStruct((1, D), F32), jax.ShapeDtypeStruct((1, LANE), F32)],
        compiler_params=_cp(cfg, ("arbitrary",)),
    )(x1, f, g_post, target)


def mid_bwd(cfg, dh2, x1, g_pre, dout, mo, g_post):
    S, D = x1.shape

    def body(dh_ref, x1_ref, gn_ref, do_ref, mo_ref, gp_ref, dx1_ref, dmo_ref, dgn_ref, dgp_ref):
        @pl.when(pl.program_id(0) == 0)
        def _():
            dgn_ref[...] = jnp.zeros_like(dgn_ref)
            dgp_ref[...] = jnp.zeros_like(dgp_ref)

        dx, dgx = _rms_bwd(dh_ref[...], x1_ref[...], gn_ref[...])
        dx1 = do_ref[...] + dx
        dx1_ref[...] = dx1
        dgn_ref[...] += jnp.sum(dgx, axis=0, keepdims=True)
        dmo, dgy = _rms_bwd(dx1, mo_ref[...], gp_ref[...])
        dmo_ref[...] = dmo.astype(BF16)
        dgp_ref[...] += jnp.sum(dgy, axis=0, keepdims=True)

    return pl.pallas_call(
        body, name="mid_bwd", grid=(S // cfg.TM,),
        in_specs=[_row(cfg, D), _row(cfg, D), _vec(D), _row(cfg, D), _row(cfg, D), _vec(D)],
        out_specs=[_row(cfg, D), _row(cfg, D), _vec(D), _vec(D)],
        out_shape=[jax.ShapeDtypeStruct((S, D), F32), jax.ShapeDtypeStruct((S, D), BF16),
                   jax.ShapeDtypeStruct((1, D), F32), jax.ShapeDtypeStruct((1, D), F32)],
        compiler_params=_cp(cfg, ("arbitrary",)),
    )(dh2, x1, g_pre, dout, mo, g_post)


def first_bwd(cfg, dh1, x, g_pre, dx1):
    S, D = x.shape

    def body(dh_ref, x_ref, g_ref, r_ref, dx_ref, dg_ref):
        @pl.when(pl.program_id(0) == 0)
        def _():
            dg_ref[...] = jnp.zeros_like(dg_ref)

        dx, dgx = _rms_bwd(dh_ref[...], x_ref[...], g_ref[...])
        dx_ref[...] = r_ref[...] + dx
        dg_ref[...] += jnp.sum(dgx, axis=0, keepdims=True)

    return pl.pallas_call(
        body, name="first_bwd", grid=(S // cfg.TM,),
        in_specs=[_row(cfg, D), _row(cfg, D), _vec(D), _row(cfg, D)],
        out_specs=[_row(cfg, D), _vec(D)],
        out_shape=[jax.ShapeDtypeStruct((S, D), F32), jax.ShapeDtypeStruct((1, D), F32)],
        compiler_params=_cp(cfg, ("arbitrary",)),
    )(dh1, x, g_pre, dx1)


def _mm(cfg, name, a, b, *, nt, a_spec, b_spec, o_spec, grid, out_shape, acc_shape, dep=None):
    nk = grid[-1]
    dot = _dot_nt if nt else _dot
    deps = [] if dep is None else [dep]

    def body(a_ref, b_ref, *rest):
        o_ref, acc_ref = rest[-2:]
        k = pl.program_id(len(grid) - 1)
        part = dot(a_ref[...], b_ref[...])
        if deps:
            part = part + rest[0][0:1, 0:1]
        if nk == 1:
            o_ref[...] = part.astype(o_ref.dtype)
            return

        @pl.when(k == 0)
        def _():
            acc_ref[...] = part

        @pl.when(k > 0)
        def _():
            acc_ref[...] += part

        @pl.when(k == nk - 1)
        def _():
            o_ref[...] = acc_ref[...].astype(o_ref.dtype)

    sem = ("parallel",) * (len(grid) - 1) + ("arbitrary",)
    dep_specs = [pl.BlockSpec((8, LANE), lambda *_: (0, 0))] * len(deps)
    return pl.pallas_call(
        body, name=name, grid=grid, in_specs=[a_spec, b_spec] + dep_specs, out_specs=o_spec, out_shape=out_shape,
        scratch_shapes=[pltpu.VMEM(acc_shape, F32)], compiler_params=_cp(cfg, sem),
    )(a, b, *deps)


def _mm_tn(cfg, name, a, b, *, a_spec, b_spec, o_spec, grid, out_shape):
    def body(a_ref, b_ref, o_ref):
        o_ref[...] = _dot_tn(a_ref[...], b_ref[...]).astype(o_ref.dtype)

    return pl.pallas_call(
        body, name=name, grid=grid, in_specs=[a_spec, b_spec], out_specs=o_spec, out_shape=out_shape,
        compiler_params=_cp(cfg, ("parallel",) * len(grid)),
    )(a, b)


def qkv_proj(cfg, h1, w_in, cos2, sin2):
    S, D = h1.shape
    tn = 2 * cfg.DH
    per = cfg.DSB // tn
    assert cfg.DSB == cfg.DDL
    nblk = 6 * per

    def body(a_ref, b_ref, c_ref, s_ref, o_ref):
        j = pl.program_id(0)
        acc = _dot(a_ref[...], b_ref[...])
        rope = jnp.logical_and(j >= 3 * per, j < 5 * per)

        @pl.when(rope)
        def _():
            for c in range(tn // cfg.DH):
                xh = acc[:, c * cfg.DH:(c + 1) * cfg.DH]
                o_ref[:, c * cfg.DH:(c + 1) * cfg.DH] = (
                    xh * c_ref[...] + pltpu.roll(xh, cfg.DH // 2, 1) * s_ref[...]).astype(BF16)

        @pl.when(jnp.logical_not(rope))
        def _():
            o_ref[...] = acc.astype(BF16)

    return pl.pallas_call(
        body, name="qkv_proj", grid=(nblk,),
        in_specs=[pl.BlockSpec((S, D), lambda j: (0, 0)), pl.BlockSpec((D, tn), lambda j: (0, j)),
                  pl.BlockSpec((S, cfg.DH), lambda j: (0, 0)), pl.BlockSpec((S, cfg.DH), lambda j: (0, 0))],
        out_specs=pl.BlockSpec((None, S, tn), lambda j: (j // per, 0, j % per)),
        out_shape=jax.ShapeDtypeStruct((6, S, cfg.DSB), BF16),
        compiler_params=_cp(cfg, ("parallel",)),
    )(h1, w_in, cos2, sin2)


def _sb_tile(cfg, q, k, valid):
    z = _dot_nt(q, k) * (cfg.DH ** -0.5)
    lb = jnp.minimum(z, 0.0) - jnp.log1p(jnp.exp(-jnp.abs(z)))
    lk = lb - z
    return lb, (lk if valid is None else jnp.where(valid, lk, 0.0))


def _masked(valid, x):
    return x if valid is None else jnp.where(valid, x, 0.0)


def sb_fwd(cfg, qkv3):
    S, QB, DH, NH = cfg.S, cfg.SBT, cfg.DH, cfg.SBH

    def body(q_ref, k_ref, v_ref, o_ref, t_ref):
        row = lax.broadcasted_iota(jnp.int32, (QB, QB), 0)
        col = lax.broadcasted_iota(jnp.int32, (QB, QB), 1)
        u_after = (row > col).astype(BF16)
        causal = col < row
        heads = [slice(h * DH, (h + 1) * DH) for h in range(NH)]

        def q_loop(qb, _):
            rows = pl.ds(pl.multiple_of(qb * QB, QB), QB)
            qs = [q_ref[rows, hd] for hd in heads]

            def tile(kb, carry, valid):
                krows = pl.ds(pl.multiple_of(kb * QB, QB), QB)
                lbk = [_sb_tile(cfg, q, k_ref[krows, hd], valid) for q, hd in zip(qs, heads)]
                rems = [_dot_split(lk, u_after) for _, lk in lbk]
                aa = [_masked(valid, jnp.exp(lb + rem + c)).astype(BF16) for (lb, _), rem, (_, c) in zip(lbk, rems, carry)]
                return tuple((o_acc + _dot(a, v_ref[krows, hd]), c + jnp.sum(lk, axis=1, keepdims=True))
                             for a, hd, (_, lk), (o_acc, c) in zip(aa, heads, lbk, carry))

            carry = tile(qb, ((jnp.zeros((QB, DH), F32), jnp.zeros((QB, 1), F32)),) * NH, causal)
            carry = lax.fori_loop(0, qb, lambda i, cr: tile(qb - 1 - i, cr, None), carry)
            for hd, (o_acc, c) in zip(heads, carry):
                o_ref[rows, hd] = o_acc
                t_ref[rows, hd] = jnp.broadcast_to(c, (QB, DH))
            return 0

        lax.fori_loop(0, S // QB, q_loop, 0)

    def spec(i):
        return pl.BlockSpec((None, S, NH * DH), lambda h: (i, 0, h))

    return pl.pallas_call(
        body, name="sb_fwd", grid=(cfg.HSB // NH,),
        in_specs=[spec(0), spec(1), spec(2)],
        out_specs=[pl.BlockSpec((S, NH * DH), lambda h: (0, h))] * 2,
        out_shape=[jax.ShapeDtypeStruct((S, cfg.DSB), F32)] * 2,
        compiler_params=_cp(cfg, ("parallel",)),
    )(qkv3, qkv3, qkv3)


def sb_bwd(cfg, qkv3, do_sb, tsum):
    S, QB, DH, NH = cfg.S, cfg.SBT, cfg.DH, cfg.SBHB
    scale = DH ** -0.5

    def body(q_ref, k_ref, v_ref, do_ref, t_ref, d_ref, dk_acc, dv_acc):
        dk_acc[...] = jnp.zeros_like(dk_acc)
        dv_acc[...] = jnp.zeros_like(dv_acc)
        row = lax.broadcasted_iota(jnp.int32, (QB, QB), 0)
        col = lax.broadcasted_iota(jnp.int32, (QB, QB), 1)
        u_upto = (row <= col).astype(BF16)
        u_before = (row < col).astype(BF16)
        causal = col < row
        heads = [slice(h * DH, (h + 1) * DH) for h in range(NH)]

        def q_loop(qb, _):
            rows = pl.ds(pl.multiple_of(qb * QB, QB), QB)
            qs = [q_ref[rows, hd] for hd in heads]
            dos = [do_ref[rows, hd] for hd in heads]
            totals = [t_ref[rows, hd.start:hd.start + 1] for hd in heads]

            def tile(kb, carry, valid):
                krows = pl.ds(pl.multiple_of(kb * QB, QB), QB)
                ks = [k_ref[krows, hd] for hd in heads]
                lbk = [_sb_tile(cfg, q, k, valid) for q, k in zip(qs, ks)]
                das = [_dot_nt(do, v_ref[krows, hd]) for do, hd in zip(dos, heads)]
                pins = [_dot_split(lk, u_upto) for _, lk in lbk]
                aa = [_masked(valid, jnp.exp(lb + (tot - pc - pin)))
                      for (lb, _), tot, (_, pc, _), pin in zip(lbk, totals, carry, pins)]
                gs = [a * da for a, da in zip(aa, das)]
                for a, do, hd in zip(aa, dos, heads):
                    dv_acc[krows, hd] += _dot_tn(a.astype(BF16), do)
                cums = [gc + _dot(g.astype(BF16), u_before) for g, (_, _, gc) in zip(gs, carry)]
                dzs = [(_masked(valid, g - jnp.exp(lb) * (g + cum)) * scale).astype(BF16)
                       for g, (lb, _), cum in zip(gs, lbk, cums)]
                for dz, q, hd in zip(dzs, qs, heads):
                    dk_acc[krows, hd] += _dot_tn(dz, q)
                return tuple((dq + _dot(dz, k), pc + jnp.sum(lk, axis=1, keepdims=True), gc + jnp.sum(g, axis=1, keepdims=True))
                             for dz, k, (_, lk), g, (dq, pc, gc) in zip(dzs, ks, lbk, gs, carry))

            z1 = jnp.zeros((QB, 1), F32)
            carry = lax.fori_loop(0, qb, lambda kb, cr: tile(kb, cr, None), ((jnp.zeros((QB, DH), F32), z1, z1),) * NH)
            for hd, (dq_acc, _, _) in zip(heads, tile(qb, carry, causal)):
                d_ref[0, rows, hd] = dq_acc.astype(BF16)
            return 0

        lax.fori_loop(0, S // QB, q_loop, 0)
        d_ref[1, :, :] = dk_acc[...].astype(BF16)
        d_ref[2, :, :] = dv_acc[...].astype(BF16)

    def spec(i):
        return pl.BlockSpec((None, S, NH * DH), lambda h: (i, 0, h))

    hd_spec = pl.BlockSpec((S, NH * DH), lambda h: (0, h))
    return pl.pallas_call(
        body, name="sb_bwd", grid=(cfg.HSB // NH,),
        in_specs=[spec(0), spec(1), spec(2), hd_spec, hd_spec],
        out_specs=pl.BlockSpec((3, S, NH * DH), lambda h: (0, 0, h)),
        out_shape=jax.ShapeDtypeStruct((6, S, cfg.DSB), BF16),
        scratch_shapes=[pltpu.VMEM((S, NH * DH), F32), pltpu.VMEM((S, NH * DH), F32)],
        compiler_params=_cp(cfg, ("parallel",)),
    )(qkv3, qkv3, qkv3, do_sb, tsum)


def _band_mask(cfg, n, n_back):
    QB = cfg.QB
    qi = lax.broadcasted_iota(jnp.int32, (QB, 2 * QB), 0)
    kj = lax.broadcasted_iota(jnp.int32, (QB, 2 * QB), 1)
    dist = QB + qi - kj
    return (dist >= 0) & (dist <= n_back) & jnp.logical_or(n > 0, kj >= QB)


def _sub_rows(start, n, dil):
    if dil > 1:
        return pl.ds(start, n, stride=dil)
    return pl.ds(start if isinstance(start, int) else pl.multiple_of(start, 8), n)


def _stage_residues(cfg, dil, pairs):
    QB, L = cfg.QB, cfg.S // dil
    for src, dst in pairs:
        for r in range(dil):
            dst[pl.ds(r * (QB + L), QB), :] = jnp.zeros((QB, cfg.DH), BF16)
            dst[pl.ds(r * (QB + L) + QB, L), :] = src[_sub_rows(r, L, dil), :].astype(BF16)


def _staged_rows(cfg):
    return cfg.S + cfg.QB * max(d for _, d in cfg.branches)


def _lane_value(x):
    return jnp.max(x, axis=1, keepdims=True)


def dil_fwd(cfg, qkv3):
    S, QB, DH = cfg.S, cfg.QB, cfg.DH
    scale = DH ** -0.5
    nb = len(cfg.branches)
    mix_rows = min(256, S)

    def body(q_ref, k_ref, v_ref, o_ref, lt_ref, qf, kf, vf, kp, vp, *obl):
        obs, lbs = obl[:nb], obl[nb:]
        qf[...] = q_ref[...].astype(F32)
        kf[...] = k_ref[...].astype(F32)
        vf[...] = v_ref[...].astype(F32)
        for b, (window, dil) in enumerate(cfg.branches):
            L, n_back = S // dil, window // dil
            assert n_back <= QB and L % QB == 0
            _stage_residues(cfg, dil, [(kf, kp), (vf, vp)])
            for r in range(dil):
                for n in range(L // QB):
                    rows = _sub_rows(r + n * (QB * dil), QB, dil)
                    band = pl.ds(r * (QB + L) + n * QB, 2 * QB)
                    s = _dot_nt(qf[rows, :].astype(BF16), kp[band, :]) * scale
                    s = jnp.where(_band_mask(cfg, n, n_back), s, NEG_BIG)
                    m = jnp.max(s, axis=1, keepdims=True)
                    p = jnp.exp(s - m)
                    den = jnp.sum(p, axis=1, keepdims=True)
                    obs[b][rows, :] = _dot(p.astype(BF16), vp[band, :]) / den
                    lbs[b][rows, :] = jnp.broadcast_to(m + jnp.log(den), (QB, DH))

        def mix(i, _):
            rows = pl.ds(pl.multiple_of(i * mix_rows, mix_rows), mix_rows)
            ls = [r[rows, :] for r in lbs]
            m = functools.reduce(jnp.maximum, ls)
            es = [jnp.exp(l - m) for l in ls]
            tot = functools.reduce(jnp.add, es)
            o_ref[rows, :] = functools.reduce(jnp.add, [(e / tot) * r[rows, :] for e, r in zip(es, obs)])
            lt_ref[rows, :] = m + jnp.log(tot)
            return 0

        lax.fori_loop(0, S // mix_rows, mix, 0)

    def spec(i):
        return pl.BlockSpec((None, S, DH), lambda h: (i, 0, h))

    o_spec = pl.BlockSpec((S, DH), lambda h: (0, h))
    return pl.pallas_call(
        body, name="dil_fwd", grid=(cfg.HDL,),
        in_specs=[spec(3), spec(4), spec(5)], out_specs=[o_spec, o_spec],
        out_shape=[jax.ShapeDtypeStruct((S, cfg.DDL), F32)] * 2,
        scratch_shapes=[pltpu.VMEM((S, DH), F32)] * 3 + [pltpu.VMEM((_staged_rows(cfg), DH), BF16)] * 2
        + [pltpu.VMEM((S, DH), F32)] * (2 * nb),
        compiler_params=_cp(cfg, ("parallel",)),
    )(qkv3, qkv3, qkv3)


def dil_bwd(cfg, qkv3, do_dl, delta, lse_tot, cos2, sin2, d_sb3):
    S, QB, DH = cfg.S, cfg.QB, cfg.DH
    scale = DH ** -0.5
    out_rows = min(256, S)

    def body(q_ref, k_ref, v_ref, do_ref, dl_ref, lt_ref, c_ref, s_ref, base_ref, d_ref,
             qf, kf, vf, dof, kp, vp, dkp, dvp, dqn, dkn, dvn):
        qf[...] = q_ref[...].astype(F32)
        kf[...] = k_ref[...].astype(F32)
        vf[...] = v_ref[...].astype(F32)
        dof[...] = do_ref[...].astype(F32)
        for acc in (dqn, dkn, dvn):
            acc[...] = jnp.zeros_like(acc)
        for window, dil in cfg.branches:
            L, n_back = S // dil, window // dil
            reg = QB + L
            _stage_residues(cfg, dil, [(kf, kp), (vf, vp)])
            dkp[pl.ds(0, dil * reg), :] = jnp.zeros((dil * reg, DH), F32)
            dvp[pl.ds(0, dil * reg), :] = jnp.zeros((dil * reg, DH), F32)
            for r in range(dil):
                for n in range(L // QB):
                    rows = _sub_rows(r + n * (QB * dil), QB, dil)
                    band = pl.ds(r * reg + n * QB, 2 * QB)
                    q = qf[rows, :].astype(BF16)
                    do = dof[rows, :].astype(BF16)
                    kb = kp[band, :]
                    s = _dot_nt(q, kb) * scale
                    s = jnp.where(_band_mask(cfg, n, n_back), s, NEG_BIG)
                    p = jnp.exp(s - _lane_value(lt_ref[rows, :]))
                    ds = (p * (_dot_nt(do, vp[band, :]) - _lane_value(dl_ref[rows, :])) * scale).astype(BF16)
                    dqn[rows, :] += _dot(ds, kb)
                    dkp[band, :] += _dot_tn(ds, q)
                    dvp[band, :] += _dot_tn(p.astype(BF16), do)
            for r in range(dil):
                sub = _sub_rows(r, L, dil)
                dkn[sub, :] += dkp[pl.ds(r * reg + QB, L), :]
                dvn[sub, :] += dvp[pl.ds(r * reg + QB, L), :]

        def finish(i, _):
            rows = pl.ds(pl.multiple_of(i * out_rows, out_rows), out_rows)
            c, sn = c_ref[rows, :], s_ref[rows, :]
            for j, acc in enumerate((dqn, dkn)):
                d = acc[rows, :]
                d_ref[j, rows, :] = (d * c + pltpu.roll(d * sn, DH // 2, 1)).astype(BF16)
            d_ref[2, rows, :] = dvn[rows, :].astype(BF16)
            return 0

        lax.fori_loop(0, S // out_rows, finish, 0)

    def spec(i):
        return pl.BlockSpec((None, S, DH), lambda h: (i, 0, h))

    hd = pl.BlockSpec((S, DH), lambda h: (0, h))
    tab = pl.BlockSpec((S, DH), lambda h: (0, 0))
    ns = _staged_rows(cfg)
    return pl.pallas_call(
        body, name="dil_bwd", grid=(cfg.HDL,),
        in_specs=[spec(3), spec(4), spec(5), hd, hd, hd, tab, tab, ANY],
        out_specs=pl.BlockSpec((3, S, DH), lambda h: (1, 0, h)),
        out_shape=jax.ShapeDtypeStruct((6, S, cfg.DDL), BF16),
        input_output_aliases={8: 0},
        scratch_shapes=[pltpu.VMEM((S, DH), F32)] * 4 + [pltpu.VMEM((ns, DH), BF16)] * 2
        + [pltpu.VMEM((ns, DH), F32)] * 2 + [pltpu.VMEM((S, DH), F32)] * 3,
        compiler_params=_cp(cfg, ("parallel",)),
    )(qkv3, qkv3, qkv3, do_dl, delta, lse_tot, cos2, sin2, d_sb3)


def combine_fwd(cfg, o_sb, o_dl, g_sb, g_dl):
    S, DH = cfg.S, cfg.DH

    def head_norm(o, g):
        return o * lax.rsqrt(jnp.mean(o * o, axis=-1, keepdims=True) + RMS_EPS) * g

    def body(osb_ref, odl_ref, gsb_ref, gdl_ref, mix_ref):
        for h in range(cfg.HSB):
            c = slice(h * DH, (h + 1) * DH)
            mix_ref[:, c] = head_norm(osb_ref[:, c], gsb_ref[:, c]).astype(BF16)
        for h in range(cfg.HDL):
            c = slice(h * DH, (h + 1) * DH)
            mix_ref[:, cfg.DSB + h * DH:cfg.DSB + (h + 1) * DH] = head_norm(odl_ref[:, c], gdl_ref[:, c]).astype(BF16)

    return pl.pallas_call(
        body, name="combine_fwd", grid=(S // cfg.TM,),
        in_specs=[_row(cfg, cfg.DSB), _row(cfg, cfg.DDL), _vec(cfg.DSB), _vec(cfg.DDL)],
        out_specs=_row(cfg, cfg.DMIX), out_shape=jax.ShapeDtypeStruct((S, cfg.DMIX), BF16),
        compiler_params=_cp(cfg, ("parallel",)),
    )(o_sb, o_dl, g_sb, g_dl)


def combine_bwd(cfg, dmix, o_sb, o_dl, g_sb, g_dl):
    S, DH = cfg.S, cfg.DH

    def body(dm_ref, osb_ref, odl_ref, gsb_ref, gdl_ref, dsb_ref, ddl_ref, dl_ref, dgsb_ref, dgdl_ref):
        @pl.when(pl.program_id(0) == 0)
        def _():
            dgsb_ref[...] = jnp.zeros_like(dgsb_ref)
            dgdl_ref[...] = jnp.zeros_like(dgdl_ref)

        for h in range(cfg.HSB):
            c = slice(h * DH, (h + 1) * DH)
            dx, dgx = _rms_bwd(dm_ref[:, c], osb_ref[:, c], gsb_ref[:, c])
            dsb_ref[:, c] = dx.astype(BF16)
            dgsb_ref[:, c] += jnp.sum(dgx, axis=0, keepdims=True)
        for h in range(cfg.HDL):
            c = slice(h * DH, (h + 1) * DH)
            o = odl_ref[:, c]
            dx, dgx = _rms_bwd(dm_ref[:, cfg.DSB + h * DH:cfg.DSB + (h + 1) * DH], o, gdl_ref[:, c])
            ddl_ref[:, c] = dx.astype(BF16)
            dl_ref[:, c] = jnp.broadcast_to(jnp.sum(dx * o, axis=-1, keepdims=True), dx.shape)
            dgdl_ref[:, c] += jnp.sum(dgx, axis=0, keepdims=True)

    return pl.pallas_call(
        body, name="combine_bwd", grid=(S // cfg.TM,),
        in_specs=[_row(cfg, cfg.DMIX), _row(cfg, cfg.DSB), _row(cfg, cfg.DDL), _vec(cfg.DSB), _vec(cfg.DDL)],
        out_specs=[_row(cfg, cfg.DSB), _row(cfg, cfg.DDL), _row(cfg, cfg.DDL), _vec(cfg.DSB), _vec(cfg.DDL)],
        out_shape=[jax.ShapeDtypeStruct((S, cfg.DSB), BF16), jax.ShapeDtypeStruct((S, cfg.DDL), BF16),
                   jax.ShapeDtypeStruct((S, cfg.DDL), F32), jax.ShapeDtypeStruct((1, cfg.DSB), F32),
                   jax.ShapeDtypeStruct((1, cfg.DDL), F32)],
        compiler_params=_cp(cfg, ("arbitrary",)),
    )(dmix, o_sb, o_dl, g_sb, g_dl)


SUB = 8


def _shift_down(u, prev, j):
    rolled = pltpu.roll(u, j, 0)
    row = lax.broadcasted_iota(jnp.int32, (SUB, u.shape[1]), 0)
    head = jnp.where(row >= j, rolled[:SUB], pltpu.roll(prev, j, 0))
    return jnp.concatenate([head, rolled[SUB:]], axis=0)


def _shift_up(u, nxt, j):
    n = u.shape[0]
    rolled = pltpu.roll(u, n - j, 0)
    row = lax.broadcasted_iota(jnp.int32, (SUB, u.shape[1]), 0)
    tail = jnp.where(row < SUB - j, rolled[n - SUB:], pltpu.roll(nxt, SUB - j, 0))
    return jnp.concatenate([rolled[:n - SUB], tail], axis=0)


def _conv(u, s1, s2, cw, cb):
    return u * cw[2:3, :] + s1 * cw[1:2, :] + s2 * cw[0:1, :] + cb


def _chunk_rows(cfg):
    ch = min(cfg.FCH, cfg.S)
    return ch, cfg.S // ch


def ffn_fwd(cfg, h2, w_up, conv_w, conv_b):
    S, D = h2.shape
    tn, nt = cfg.TNF, cfg.FFP // cfg.TNF
    ch, nch = _chunk_rows(cfg)

    def body(h_ref, wg_ref, wv_ref, cwg_ref, cwv_ref, cbg_ref, cbv_ref, u_ref, y_ref):
        prev = [jnp.zeros((SUB, tn), F32)] * 2
        pending = None
        for ci in range(nch + 1):
            if ci < nch:
                h = h_ref[pl.ds(ci * ch, ch), :]
                us_next = [_dot_nt(h, wg_ref[...]), _dot_nt(h, wv_ref[...])]
            if pending is not None:
                rows, us = pending
                cs = []
                for i, (cw_ref, cb_ref) in enumerate(((cwg_ref, cbg_ref), (cwv_ref, cbv_ref))):
                    u_ref[i, rows, :] = us[i]
                    cs.append(_conv(us[i], _shift_down(us[i], prev[i], 1), _shift_down(us[i], prev[i], 2),
                                    cw_ref[...], cb_ref[...]))
                y_ref[rows, :] = (_gelu(cs[0])[0] * cs[1]).astype(BF16)
                prev = [u[ch - SUB:] for u in us]
            pending = (pl.ds(ci * ch, ch), us_next) if ci < nch else None

    return pl.pallas_call(
        body, name="ffn_fwd", grid=(nt,),
        in_specs=[pl.BlockSpec((S, D), lambda n: (0, 0)),
                  pl.BlockSpec((tn, D), lambda n: (n, 0)), pl.BlockSpec((tn, D), lambda n: (n + nt, 0)),
                  pl.BlockSpec((3, tn), lambda n: (0, n)), pl.BlockSpec((3, tn), lambda n: (0, n + nt)),
                  pl.BlockSpec((1, tn), lambda n: (0, n)), pl.BlockSpec((1, tn), lambda n: (0, n + nt))],
        out_specs=[pl.BlockSpec((2, S, tn), lambda n: (0, 0, n)), pl.BlockSpec((S, tn), lambda n: (0, n))],
        out_shape=[jax.ShapeDtypeStruct((2, S, cfg.FFP), F32), jax.ShapeDtypeStruct((S, cfg.FFP), BF16)],
        compiler_params=_cp(cfg, ("parallel",)),
    )(h2, w_up, w_up, conv_w, conv_w, conv_b, conv_b)


def ffn_bwd(cfg, df, h2, w_down, u, conv_w, conv_b):
    S, D = df.shape
    tn, nt = cfg.TNF, cfg.FFP // cfg.TNF

    ch, nch = _chunk_rows(cfg)

    def body(df_ref, h_ref, wd_ref, u_ref, cwg_ref, cwv_ref, cbg_ref, cbv_ref,
             du_ref, dwd_ref, dwu_ref, dcw_ref, dcb_ref):
        cws = (cwg_ref[...], cwv_ref[...])
        cbs = (cbg_ref[...], cbv_ref[...])
        zero = jnp.zeros((SUB, tn), F32)
        nxt = [zero, zero]
        dws = [[jnp.zeros((1, tn), F32)] * 4 for _ in range(2)]
        dwd = jnp.zeros((tn, D), F32)
        dwu = [jnp.zeros((tn, D), F32)] * 2
        order = list(reversed(range(nch)))
        dys, done = {}, {}
        for step in range(nch + 2):
            if step < nch:
                ci = order[step]
                dys[ci] = _dot_nt(df_ref[pl.ds(ci * ch, ch), :], wd_ref[...])
            if 1 <= step <= nch:
                ci = order[step - 1]
                rows = pl.ds(ci * ch, ch)
                dy = dys.pop(ci)
                us, s1, s2, cs = [], [], [], []
                for i in range(2):
                    u = u_ref[i, rows, :]
                    prev = u_ref[i, pl.ds(ci * ch - SUB, SUB), :] if ci else zero
                    us.append(u)
                    s1.append(_shift_down(u, prev, 1))
                    s2.append(_shift_down(u, prev, 2))
                    cs.append(_conv(u, s1[i], s2[i], cws[i], cbs[i]))
                gl, t = _gelu(cs[0])
                dcs = (dy * cs[1] * _gelu_grad(cs[0], t), dy * gl)
                dus = []
                for i, dc in enumerate(dcs):
                    du = dc * cws[i][2:3, :] + _shift_up(dc, nxt[i], 1) * cws[i][1:2, :] + _shift_up(dc, nxt[i], 2) * cws[i][0:1, :]
                    dus.append(du.astype(BF16))
                    du_ref[i, rows, :] = dus[i]
                    for j, tap in enumerate((s2[i], s1[i], us[i])):
                        dws[i][j] = dws[i][j] + jnp.sum(dc * tap, axis=0, keepdims=True)
                    dws[i][3] = dws[i][3] + jnp.sum(dc, axis=0, keepdims=True)
                nxt = [dc[:SUB] for dc in dcs]
                done[ci] = ((gl * cs[1]).astype(BF16), dus)
            if step >= 2:
                ci = order[step - 2]
                rows = pl.ds(ci * ch, ch)
                yv, dus = done.pop(ci)
                dwd = dwd + _dot_tn(yv, df_ref[rows, :])
                hv = h_ref[rows, :]
                dwu = [acc + _dot_tn(du, hv) for acc, du in zip(dwu, dus)]
        dwd_ref[...] = dwd.astype(BF16)
        for i in range(2):
            dwu_ref[i] = dwu[i].astype(BF16)
            for j in range(3):
                dcw_ref[i, j:j + 1, :] = dws[i][j]
            dcb_ref[i] = dws[i][3]

    whole = pl.BlockSpec((S, D), lambda n: (0, 0), pipeline_mode=pl.Buffered(1))
    du, dwd, dwu, dcw, dcb = pl.pallas_call(
        body, name="ffn_bwd", grid=(nt,),
        in_specs=[whole, whole, pl.BlockSpec((tn, D), lambda n: (n, 0)),
                  pl.BlockSpec((2, S, tn), lambda n: (0, 0, n)),
                  pl.BlockSpec((3, tn), lambda n: (0, n)), pl.BlockSpec((3, tn), lambda n: (0, n + nt)),
                  pl.BlockSpec((1, tn), lambda n: (0, n)), pl.BlockSpec((1, tn), lambda n: (0, n + nt))],
        out_specs=[pl.BlockSpec((2, S, tn), lambda n: (0, 0, n)), pl.BlockSpec((tn, D), lambda n: (n, 0)),
                   pl.BlockSpec((2, tn, D), lambda n: (0, n, 0)),
                   pl.BlockSpec((2, 3, tn), lambda n: (0, 0, n)), pl.BlockSpec((2, 1, tn), lambda n: (0, 0, n))],
        out_shape=[jax.ShapeDtypeStruct((2, S, cfg.FFP), BF16), jax.ShapeDtypeStruct((cfg.FFP, D), BF16),
                   jax.ShapeDtypeStruct((2, cfg.FFP, D), BF16),
                   jax.ShapeDtypeStruct((2, 3, cfg.FFP), F32), jax.ShapeDtypeStruct((2, 1, cfg.FFP), F32)],
        compiler_params=_cp(cfg, ("parallel",)),
    )(df, h2, w_down, u, conv_w, conv_w, conv_b, conv_b)
    return du, dwd, dwu.reshape(cfg.FF2P, D), dcw, dcb


def rope_tables(cfg):
    inv_freq = ROPE_THETA ** (-jnp.arange(0, cfg.DH, 2, dtype=F32) / cfg.DH)
    ang = jnp.arange(cfg.S, dtype=F32)[:, None] * inv_freq[None, :]
    cos, sin = jnp.cos(ang), jnp.sin(ang)
    return jnp.concatenate([cos, cos], axis=1), jnp.concatenate([-sin, sin], axis=1)


class LocalWeights:
    def __init__(self, w_in, w_out, w_up, conv_w, w_down):
        self.w = (w_in, w_out, w_up, conv_w, w_down)
        self.grads = {}

    def first_start(self):
        return None

    def weights_first(self, after):
        return self.w[0], self.w[3]

    def start_rest(self, after):
        return None

    def weights_rest(self, group, after):
        return ((self.w[1],), None) if group == 0 else ((self.w[4],), None)

    def forwarded(self, group, after):
        return (self.w[2],)

    def pair_start(self, grads):
        self.grads.update(grads)
        return None

    def reduce_start(self, grads, after=None):
        self.grads.update(grads)
        return None

    def reduce_wait(self, names, after):
        pass


def _after(a, token):
    return a if token is None else a + token[0, 0].astype(a.dtype)


def local_step(cfg, comm, x, target, g1, g2, g3, g4, g_sb, g_dl, conv_b):
    S, D = cfg.S, cfg.D
    cos2, sin2 = rope_tables(cfg)
    full = lambda r, c: pl.BlockSpec((r, c), lambda j, k: (0, 0))

    h1 = rms_fwd(cfg, x, _after(g1, comm.first_start()))
    w_in, conv_w = comm.weights_first(after=h1)
    qkv3 = qkv_proj(cfg, h1, w_in, _after(cos2, comm.start_rest(after=w_in)), sin2)
    o_sb, tsum = sb_fwd(cfg, qkv3)
    o_dl, lse_tot = dil_fwd(cfg, qkv3)
    mixed = combine_fwd(cfg, o_sb, o_dl, g_sb, g_dl)
    (w_out,), token = comm.weights_rest(0, after=mixed)
    tn = cfg.TN
    mo = _mm(cfg, "mix_out", mixed, w_out, nt=False, grid=(D // tn, 1),
             a_spec=full(S, cfg.DMIX), b_spec=pl.BlockSpec((cfg.DMIX, tn), lambda j, k: (0, j)),
             o_spec=pl.BlockSpec((S, tn), lambda j, k: (0, j)),
             out_shape=jax.ShapeDtypeStruct((S, D), F32), acc_shape=(8, LANE), dep=token)
    x1, h2 = mid_fwd(cfg, x, mo, g2, g3)
    w_up, = comm.forwarded(0, after=h2)
    u, y = ffn_fwd(cfg, h2, w_up, conv_w, conv_b)
    (w_down,), _ = comm.weights_rest(1, after=y)
    tk = cfg.FFP // 2
    f = _mm(cfg, "ffn_down", y, w_down, nt=False, grid=(D // tn, cfg.FFP // tk),
            a_spec=pl.BlockSpec((S, tk), lambda j, k: (0, k)), b_spec=pl.BlockSpec((tk, tn), lambda j, k: (k, j)),
            o_spec=pl.BlockSpec((S, tn), lambda j, k: (0, j)),
            out_shape=jax.ShapeDtypeStruct((S, D), F32), acc_shape=(S, tn))
    dout, df, dg4, loss = final_fwd_bwd(cfg, x1, f, g4, target)

    du, dw_down, dw_up, dconv_w, dconv_b = ffn_bwd(cfg, df, h2, w_down, u, conv_w, conv_b)
    kt = cfg.FFP // tk
    dh2 = _mm(cfg, "d_h2", du, w_up, nt=False, grid=(D // tn, 2 * kt),
              a_spec=pl.BlockSpec((None, S, tk), lambda j, k: (k // kt, 0, k % kt)),
              b_spec=pl.BlockSpec((tk, tn), lambda j, k: (k, j)),
              o_spec=pl.BlockSpec((S, tn), lambda j, k: (0, j)),
              out_shape=jax.ShapeDtypeStruct((S, D), F32), acc_shape=(S, tn),
              dep=comm.pair_start(dict(w_down=dw_down, w_up=dw_up)))
    dx1, dmo, dg3, dg2 = mid_bwd(cfg, dh2, x1, g3, dout, mo, g2)

    dmix = _mm(cfg, "d_mixed", dmo, w_out, nt=True, grid=(cfg.DMIX // tn, 1),
               a_spec=full(S, D), b_spec=pl.BlockSpec((tn, D), lambda j, k: (j, 0)),
               o_spec=pl.BlockSpec((S, tn), lambda j, k: (0, j)),
               out_shape=jax.ShapeDtypeStruct((S, cfg.DMIX), F32), acc_shape=(8, LANE))
    dw_out = _mm_tn(cfg, "d_w_out", mixed, dmo, grid=(D // tn,),
                    a_spec=pl.BlockSpec((S, cfg.DMIX), lambda j: (0, 0)),
                    b_spec=pl.BlockSpec((S, tn), lambda j: (0, j)),
                    o_spec=pl.BlockSpec((cfg.DMIX, tn), lambda j: (0, j)),
                    out_shape=jax.ShapeDtypeStruct((cfg.DMIX, D), BF16))
    token = comm.reduce_start(dict(w_out=dw_out), after=dw_out)
    do_sb, do_dl, delta, dg_sb, dg_dl = combine_bwd(cfg, dmix, o_sb, o_dl, _after(g_sb, token), g_dl)
    d_sb3 = sb_bwd(cfg, qkv3, do_sb, tsum)
    dqkv3 = dil_bwd(cfg, qkv3, do_dl, delta, lse_tot, cos2, sin2, d_sb3)
    comm.reduce_wait(("w_out", "w_up", "w_down"), after=dqkv3)
    tkq = min(tn, cfg.DSB)
    kq = cfg.DSB // tkq
    dw_in = _mm_tn(cfg, "d_w_in", h1, dqkv3, grid=(6 * kq,),
                   a_spec=pl.BlockSpec((S, D), lambda j: (0, 0)),
                   b_spec=pl.BlockSpec((None, S, tkq), lambda j: (j // kq, 0, j % kq)),
                   o_spec=pl.BlockSpec((D, tkq), lambda j: (0, j)),
                   out_shape=jax.ShapeDtypeStruct((D, 6 * cfg.DSB), BF16))
    token = comm.reduce_start(dict(w_in=dw_in))
    dh1 = _mm(cfg, "d_h1", dqkv3, w_in, nt=True, grid=(D // tn, 6),
              a_spec=pl.BlockSpec((None, S, cfg.DSB), lambda j, k: (k, 0, 0)),
              b_spec=pl.BlockSpec((tn, cfg.DSB), lambda j, k: (j, k)),
              o_spec=pl.BlockSpec((S, tn), lambda j, k: (0, j)),
              out_shape=jax.ShapeDtypeStruct((S, D), F32), acc_shape=(S, tn), dep=token)
    grad_x, dg1 = first_bwd(cfg, dh1, x, g1, dx1)
    small = dict(loss=loss, g1=dg1, g2=dg2, g3=dg3, g4=dg4, g_sb=dg_sb, g_dl=dg_dl,
                 conv_b=dconv_b.reshape(1, cfg.FF2P), conv_w=dconv_w.transpose(1, 0, 2).reshape(3, cfg.FF2P))
    return grad_x, small


ANY = pl.BlockSpec(memory_space=pl.ANY)


def _me():
    return lax.axis_index("x"), lax.axis_index("y"), lax.axis_index("c")


def _other_chips(x, y):
    return [(1 - x, y), (x, 1 - y), (1 - x, 1 - y)]


def pad_conv_w(cfg, conv_w, pos):
    r, c = conv_w.shape

    def body(pos_ref, w_ref, full_ref, scr, sem):
        scr[:, :c] = w_ref[...]
        scr[:, c:] = jnp.zeros((r, cfg.FSHP - c), F32)
        cols = pl.ds(pl.multiple_of(pos_ref[0] * cfg.FSHP, LANE), cfg.FSHP)
        cp = pltpu.make_async_copy(scr, full_ref.at[:, cols], sem)
        cp.start()
        cp.wait()

    return pl.pallas_call(
        body, name="pad_conv_w",
        grid_spec=pltpu.PrefetchScalarGridSpec(
            num_scalar_prefetch=1, grid=(1,), in_specs=[pl.BlockSpec((r, c), lambda i, p: (0, 0))], out_specs=ANY,
            scratch_shapes=[pltpu.VMEM((r, cfg.FSHP), F32), pltpu.SemaphoreType.DMA]),
        out_shape=jax.ShapeDtypeStruct(_full_shape(cfg, "conv_w"), F32),
    )(pos, conv_w)


def _tile2(r, c):
    return (256, c) if r % 256 == 0 else (r, 512 if c % 512 == 0 else c)


def cast_into(cfg, name, w, pos, dep=None):
    r, c = w.shape
    _, nr, _, nc = _slab(cfg, name, 0)
    tm, tc = _tile2(r, c)
    wr = nr if tm == r else tm
    assert nc == c and (nr == r or tm == r)
    gap = cfg.FSHP - cfg.FSH if name == "w_down" else 0
    deps = [] if dep is None else [dep]

    def body(pos_ref, w_ref, *rest):
        full_ref, token, scr, sem = rest[len(deps):]
        token[...] = jnp.zeros_like(token)
        tile = w_ref[...]
        if deps:
            tile = tile + rest[0][0:1, 0:1]
        scr[pl.ds(0, tm), :] = tile.astype(BF16)
        if wr > tm:
            scr[pl.ds(tm, wr - tm), :] = jnp.zeros((wr - tm, tc), BF16)
        r0, _, c0, _ = _slab(cfg, name, pos_ref[0])
        rows = pl.ds(pl.multiple_of(r0 + pl.program_id(0) * tm, 16), wr)
        cols = pl.ds(pl.multiple_of(c0 + pl.program_id(1) * tc, LANE), tc)
        cps = [pltpu.make_async_copy(scr.at[pl.ds(0, wr), :], full_ref.at[rows, cols], sem.at[0])]
        if gap:
            scr[pl.ds(wr, gap), :] = jnp.zeros((gap, tc), BF16)
            for h in range(2):
                pad_rows = pl.ds(h * cfg.FSHP + cfg.FSH, gap)
                cps.append(pltpu.make_async_copy(scr.at[pl.ds(wr, gap), :], full_ref.at[pad_rows, cols], sem.at[1 + h]))
        for cp in cps:
            cp.start()
        for cp in cps:
            cp.wait()

    return pl.pallas_call(
        body, name=f"cast_{name}",
        grid_spec=pltpu.PrefetchScalarGridSpec(
            num_scalar_prefetch=1, grid=(r // tm, c // tc),
            in_specs=[pl.BlockSpec((tm, tc), lambda i, j, p: (i, j))]
            + [pl.BlockSpec((8, LANE), lambda i, j, p: (0, 0))] * len(deps),
            out_specs=[ANY, pl.BlockSpec((8, LANE), lambda i, j, p: (0, 0))],
            scratch_shapes=[pltpu.VMEM((wr + gap, tc), BF16), pltpu.SemaphoreType.DMA((3,))]),
        out_shape=[jax.ShapeDtypeStruct(_full_shape(cfg, name), BF16), jax.ShapeDtypeStruct((8, LANE), F32)],
        compiler_params=_cp(cfg, ("arbitrary", "arbitrary")),
    )(pos, w, *deps)


HBM = pl.BlockSpec(memory_space=pltpu.HBM)
SEM = pl.BlockSpec(memory_space=pltpu.SEMAPHORE)
TOKEN = pl.BlockSpec(memory_space=pltpu.VMEM)
EFFECT = pltpu.SideEffectType.DATAFLOW_SIDE_EFFECTING


def _slab(cfg, name, k):
    D = cfg.D
    if name == "w_in":
        cin = 6 * cfg.DSB // N_CHIPS
        return 0, D, k * cin, cin
    if name == "w_out":
        rout = cfg.DMIX // N_CHIPS
        return k * rout, rout, 0, D
    if name == "w_up":
        return k * cfg.FSHP, cfg.FSHP, 0, D
    if name == "conv_w":
        return 0, 3, k * cfg.FSHP, cfg.FSHP
    rdn = cfg.FSH // 2
    return (k // 2) * cfg.FSHP + (k % 2) * rdn, rdn, 0, D


def _full_shape(cfg, name):
    return dict(w_in=(cfg.D, 6 * cfg.DSB), w_out=(cfg.DMIX, cfg.D), w_up=(cfg.FF2P, cfg.D), w_down=(cfg.FFP, cfg.D),
                conv_w=(3, cfg.FF2P))[name]


def _half(cfg, name, ref, k, h):
    r0, nr, c0, nc = _slab(cfg, name, k)
    if name == "conv_w":
        return ref.at[:, pl.ds(c0, nc)]
    return ref.at[pl.ds(r0 + h * (nr // 2), nr // 2), pl.ds(c0, nc)]


def _rows_half(ref, h):
    nr = ref.shape[0] // 2
    return ref.at[pl.ds(h * nr, nr), :]


def _remote(src, dst, send_sem, recv_sem, dev):
    return pltpu.make_async_remote_copy(src_ref=src, dst_ref=dst, send_sem=send_sem, recv_sem=recv_sem,
                                        device_id=dev, device_id_type=MESH)


REST = ("w_out", "w_up", "w_down")
FIRST = (("w_in", "conv_w"),)
GROUPS = (("w_out", "w_up"), ("w_down",))


def _hbm(a):
    return pltpu.with_memory_space_constraint(a, pltpu.HBM)


def gather_start(cfg, tag, groups, fulls, after):
    order = [k for names in groups for k in names]
    n, ng = len(order), len(groups)

    def body(*refs):
        lands = dict(zip(order, refs[:n]))
        sems = refs[n + 1:n + 1 + 2 * ng]
        token = refs[-1]
        x, y, c = _me()
        me = 2 * x + y
        for g, names in enumerate(groups):
            for i, name in enumerate(names):
                mine = _half(cfg, name, lands[name], me, c)
                for j, (px, py) in enumerate(_other_chips(x, y)):
                    _remote(mine, mine, sems[2 * g].at[3 * i + j], sems[2 * g + 1].at[3 * i + j], (px, py, c)).start()
        token[...] = jnp.zeros_like(token)

    ops = [_hbm(fulls[k]) for k in order]
    sem_shapes = [pltpu.SemaphoreType.DMA((3 * len(names),)) for names in groups for _ in range(2)]
    outs = pl.pallas_call(
        body, name=f"gather_start_{tag}",
        in_specs=[HBM] * n + [ANY],
        out_specs=[SEM] * (2 * ng) + [HBM] * n + [TOKEN],
        out_shape=sem_shapes + [pltpu.HBM(a.shape, a.dtype) for a in ops] + [jax.ShapeDtypeStruct((8, LANE), F32)],
        input_output_aliases={i: 2 * ng + i for i in range(n)},
        compiler_params=pltpu.CompilerParams(has_side_effects=EFFECT),
    )(*ops, after)
    thru = dict(zip(order, outs[2 * ng:2 * ng + n]))
    return [(outs[2 * g], outs[2 * g + 1], [thru[k] for k in names]) for g, names in enumerate(groups)], outs[-1]


def gather_wait(cfg, names, ssem, rsem, lands, after):
    n = len(names)

    def body(*refs):
        lands_ = refs[:n]
        ssem_, rsem_ = refs[n], refs[n + 1]
        x, y, c = _me()
        me = 2 * x + y
        for i, name in enumerate(names):
            for j, (px, py) in enumerate(_other_chips(x, y)):
                cp = _remote(_half(cfg, name, lands_[i], me, c), _half(cfg, name, lands_[i], 2 * px + py, c),
                             ssem_.at[3 * i + j], rsem_.at[3 * i + j], (px, py, c))
                cp.wait_send()
                cp.wait_recv()

    return pl.pallas_call(
        body, name="gather_wait_" + "_".join(names),
        in_specs=[HBM] * n + [SEM, SEM, ANY], out_specs=[HBM] * n,
        out_shape=[pltpu.HBM(a.shape, a.dtype) for a in lands],
        input_output_aliases={i: i for i in range(n)},
        compiler_params=pltpu.CompilerParams(has_side_effects=EFFECT),
    )(*lands, ssem, rsem, after)


def gather_finish(cfg, names, lands):
    n = len(names)

    def body(*refs):
        outs = refs[n:2 * n]
        ssem, rsem = refs[2 * n:]
        x, y, c = _me()
        sib = (x, y, 1 - c)
        fwds = []
        for i, name in enumerate(names):
            for j, (px, py) in enumerate(_other_chips(x, y)):
                landed = _half(cfg, name, outs[i], 2 * px + py, c)
                fwds.append(_remote(landed, landed, ssem.at[3 * i + j], rsem.at[3 * i + j], sib))
        for cp in fwds:
            cp.start()
        for i, name in enumerate(names):
            for j, (px, py) in enumerate(_other_chips(x, y)):
                passed = _half(cfg, name, outs[i], 2 * px + py, 1 - c)
                _remote(passed, passed, ssem.at[3 * i + j], rsem.at[3 * i + j], sib).wait_recv()
        for cp in fwds:
            cp.wait_send()

    return pl.pallas_call(
        body, name="gather_finish_" + "_".join(names), in_specs=[ANY] * n, out_specs=[ANY] * n,
        out_shape=[jax.ShapeDtypeStruct(a.shape, a.dtype) for a in lands],
        input_output_aliases={i: i for i in range(n)},
        scratch_shapes=[pltpu.SemaphoreType.DMA((3 * n,)), pltpu.SemaphoreType.DMA((3 * n,))],
    )(*lands)


def forward_start(cfg, names, lands, after):
    n = len(names)

    def body(*refs):
        outs = refs[:n]
        ssem, rsem = refs[n + 1], refs[n + 2]
        token = refs[-1]
        x, y, c = _me()
        for i, name in enumerate(names):
            for j, (px, py) in enumerate(_other_chips(x, y)):
                landed = _half(cfg, name, outs[i], 2 * px + py, c)
                _remote(landed, landed, ssem.at[3 * i + j], rsem.at[3 * i + j], (x, y, 1 - c)).start()
        token[...] = jnp.zeros_like(token)

    ops = [_hbm(a) for a in lands]
    outs = pl.pallas_call(
        body, name="forward_start_" + "_".join(names),
        in_specs=[HBM] * n + [ANY], out_specs=[SEM, SEM] + [HBM] * n + [TOKEN],
        out_shape=[pltpu.SemaphoreType.DMA((3 * n,)), pltpu.SemaphoreType.DMA((3 * n,))]
        + [pltpu.HBM(a.shape, a.dtype) for a in ops] + [jax.ShapeDtypeStruct((8, LANE), F32)],
        input_output_aliases={i: 2 + i for i in range(n)},
        compiler_params=pltpu.CompilerParams(has_side_effects=EFFECT),
    )(*ops, after)
    return outs[0], outs[1], outs[2:2 + n], outs[-1]


def forward_wait(cfg, names, ssem, rsem, lands, after):
    n = len(names)

    def body(*refs):
        outs = refs[:n]
        ssem_, rsem_ = refs[n], refs[n + 1]
        x, y, c = _me()
        for i, name in enumerate(names):
            for j, (px, py) in enumerate(_other_chips(x, y)):
                cp = _remote(_half(cfg, name, outs[i], 2 * px + py, c), _half(cfg, name, outs[i], 2 * px + py, 1 - c),
                             ssem_.at[3 * i + j], rsem_.at[3 * i + j], (x, y, 1 - c))
                cp.wait_send()
                cp.wait_recv()

    return pl.pallas_call(
        body, name="forward_wait_" + "_".join(names),
        in_specs=[HBM] * n + [SEM, SEM, ANY], out_specs=[HBM] * n,
        out_shape=[pltpu.HBM(a.shape, a.dtype) for a in lands],
        input_output_aliases={i: i for i in range(n)},
        compiler_params=pltpu.CompilerParams(has_side_effects=EFFECT),
    )(*lands, ssem, rsem, after)


def pair_send(cfg, grads):
    names = list(grads)
    n = len(names)

    def half_shape(name):
        _, nr, _, nc = _slab(cfg, name, 0)
        return (N_CHIPS, nr // 2, nc)

    def body(*refs):
        srcs, theirs = refs[:n], refs[n:2 * n]
        ssem, rsem = refs[2 * n:]
        x, y, c = _me()
        cps = []
        for i, name in enumerate(names):
            for k in range(N_CHIPS):
                cps.append(_remote(_half(cfg, name, srcs[i], k, 1 - c), theirs[i].at[k],
                                   ssem.at[N_CHIPS * i + k], rsem.at[N_CHIPS * i + k], (x, y, 1 - c)))
        for cp in cps:
            cp.start()
        for cp in cps:
            cp.wait()

    outs = pl.pallas_call(
        body, name="pair_send_" + "_".join(names), in_specs=[ANY] * n, out_specs=[ANY] * n,
        out_shape=[jax.ShapeDtypeStruct(half_shape(name), BF16) for name in names],
        scratch_shapes=[pltpu.SemaphoreType.DMA((N_CHIPS * n,))] * 2,
    )(*[grads[k] for k in names])
    return dict(zip(names, outs))


def pair_start(cfg, grads, after):
    names = list(grads)
    n = len(names)

    def body(*refs):
        srcs, theirs = refs[:n], refs[n:2 * n]
        ssem, rsem = refs[2 * n + 1], refs[2 * n + 2]
        token = refs[-1]
        x, y, c = _me()
        for i, name in enumerate(names):
            for k in range(N_CHIPS):
                _remote(_half(cfg, name, srcs[i], k, 1 - c), theirs[i].at[k],
                        ssem.at[N_CHIPS * i + k], rsem.at[N_CHIPS * i + k], (x, y, 1 - c)).start()
        token[...] = jnp.zeros_like(token)

    def half_shape(name):
        _, nr, _, nc = _slab(cfg, name, 0)
        return (N_CHIPS, nr // 2, nc)

    ops = [_hbm(grads[k]) for k in names] + [_hbm(lax.empty(half_shape(k), BF16)) for k in names]
    outs = pl.pallas_call(
        body, name="pair_start_" + "_".join(names),
        in_specs=[HBM] * (2 * n) + [ANY],
        out_specs=[SEM, SEM] + [HBM] * (2 * n) + [TOKEN],
        out_shape=[pltpu.SemaphoreType.DMA((N_CHIPS * n,)), pltpu.SemaphoreType.DMA((N_CHIPS * n,))]
        + [pltpu.HBM(a.shape, a.dtype) for a in ops] + [jax.ShapeDtypeStruct((8, LANE), F32)],
        input_output_aliases={i: 2 + i for i in range(2 * n)},
        compiler_params=pltpu.CompilerParams(has_side_effects=EFFECT),
    )(*ops, after)
    return outs[0], outs[1], dict(zip(names, outs[2:2 + n])), dict(zip(names, outs[2 + n:2 + 2 * n])), outs[-1]


def pair_wait(cfg, ssem, rsem, grads, theirs, after):
    names = list(grads)
    n = len(names)

    def body(*refs):
        srcs, theirs_ = refs[:n], refs[n:2 * n]
        ssem_, rsem_ = refs[2 * n], refs[2 * n + 1]
        x, y, c = _me()
        for i, name in enumerate(names):
            for k in range(N_CHIPS):
                cp = _remote(_half(cfg, name, srcs[i], k, 1 - c), theirs_[i].at[k],
                             ssem_.at[N_CHIPS * i + k], rsem_.at[N_CHIPS * i + k], (x, y, 1 - c))
                cp.wait_send()
                cp.wait_recv()

    ops = [grads[k] for k in names] + [theirs[k] for k in names]
    outs = pl.pallas_call(
        body, name="pair_wait_" + "_".join(names),
        in_specs=[HBM] * (2 * n) + [SEM, SEM, ANY], out_specs=[HBM] * (2 * n),
        out_shape=[pltpu.HBM(a.shape, a.dtype) for a in ops],
        input_output_aliases={i: i for i in range(2 * n)},
        compiler_params=pltpu.CompilerParams(has_side_effects=EFFECT),
    )(*ops, ssem, rsem, after)
    return dict(zip(names, outs[:n])), dict(zip(names, outs[n:]))


def pair_sum(cfg, name, grad, theirs, pos):
    _, r, c = theirs.shape
    tm, tc = _tile2(r, c)

    ni, nj = r // tm, c // tc
    total = N_CHIPS * ni * nj

    def body(pos_ref, g_ref, t_ref, o_ref, scr, sem):
        step = (pl.program_id(0) * ni + pl.program_id(1)) * nj + pl.program_id(2)

        def fetch(flat, slot):
            k, rem = flat // (ni * nj), flat % (ni * nj)
            r0, nr, c0, _ = _slab(cfg, name, k)
            rows = pl.ds(pl.multiple_of(r0 + pos_ref[1] * (nr // 2) + (rem // nj) * tm, 16), tm)
            cols = pl.ds(pl.multiple_of(c0 + (rem % nj) * tc, LANE), tc)
            return pltpu.make_async_copy(g_ref.at[rows, cols], scr.at[slot], sem.at[slot])

        @pl.when(step == 0)
        def _():
            fetch(0, 0).start()

        @pl.when(step + 1 < total)
        def _():
            fetch(step + 1, (step + 1) % 2).start()

        fetch(step, step % 2).wait()
        o_ref[...] = (scr[step % 2].astype(F32) + t_ref[...].astype(F32)).astype(BF16)

    blk = pl.BlockSpec((None, tm, tc), lambda k, i, j, p: (k, i, j))
    return pl.pallas_call(
        body, name=f"pair_sum_{name}",
        grid_spec=pltpu.PrefetchScalarGridSpec(
            num_scalar_prefetch=1, grid=(N_CHIPS, ni, nj), in_specs=[ANY, blk], out_specs=blk,
            scratch_shapes=[pltpu.VMEM((2, tm, tc), BF16), pltpu.SemaphoreType.DMA((2,))]),
        out_shape=jax.ShapeDtypeStruct(theirs.shape, BF16),
        compiler_params=_cp(cfg, ("arbitrary",) * 3),
    )(pos, grad, theirs)


def scatter_start(cfg, pres, after):
    names = list(pres)
    n = len(names)

    def body(*refs):
        srcs, lands = refs[:n], refs[n:2 * n]
        ssem, rsem = refs[2 * n + 1], refs[2 * n + 2]
        token = refs[-1]
        x, y, c = _me()
        for i in range(n):
            for j, (px, py) in enumerate(_other_chips(x, y)):
                _remote(srcs[i].at[2 * px + py], lands[i].at[j], ssem.at[3 * i + j], rsem.at[3 * i + j], (px, py, c)).start()
        token[...] = jnp.zeros_like(token)

    lands = [lax.empty((3,) + pres[k].shape[1:], BF16) for k in names]
    ops = [_hbm(a) for a in [pres[k] for k in names] + lands]
    outs = pl.pallas_call(
        body, name="scatter_start_" + "_".join(names),
        in_specs=[HBM] * (2 * n) + [ANY],
        out_specs=[SEM, SEM] + [HBM] * (2 * n) + [TOKEN],
        out_shape=[pltpu.SemaphoreType.DMA((3 * n,)), pltpu.SemaphoreType.DMA((3 * n,))]
        + [pltpu.HBM(a.shape, a.dtype) for a in ops] + [jax.ShapeDtypeStruct((8, LANE), F32)],
        input_output_aliases={i: 2 + i for i in range(2 * n)},
        compiler_params=pltpu.CompilerParams(has_side_effects=EFFECT),
    )(*ops, after)
    return outs[0], outs[1], dict(zip(names, outs[2:2 + n])), dict(zip(names, outs[2 + n:2 + 2 * n])), outs[-1]


def scatter_wait(cfg, ssem, rsem, pres, lands, after):
    names = list(pres)
    n = len(names)

    def body(*refs):
        srcs, lands_ = refs[:n], refs[n:2 * n]
        ssem_, rsem_ = refs[2 * n], refs[2 * n + 1]
        x, y, c = _me()
        for i in range(n):
            for j, (px, py) in enumerate(_other_chips(x, y)):
                cp = _remote(srcs[i].at[2 * px + py], lands_[i].at[j], ssem_.at[3 * i + j], rsem_.at[3 * i + j], (px, py, c))
                cp.wait_send()
                cp.wait_recv()

    ops = [pres[k] for k in names] + [lands[k] for k in names]
    outs = pl.pallas_call(
        body, name="scatter_wait_" + "_".join(names),
        in_specs=[HBM] * (2 * n) + [SEM, SEM, ANY], out_specs=[HBM] * (2 * n),
        out_shape=[pltpu.HBM(a.shape, a.dtype) for a in ops],
        input_output_aliases={i: i for i in range(2 * n)},
        compiler_params=pltpu.CompilerParams(has_side_effects=EFFECT),
    )(*ops, ssem, rsem, after)
    return dict(zip(names, outs[:n])), dict(zip(names, outs[n:]))


def sum_landed(cfg, name, pre, land, pos):
    _, r, c = pre.shape
    tm, tc = _tile2(r, c)
    nrt = r // tm

    def body(pos_ref, p_ref, l_ref, o_ref):
        acc = p_ref[...].astype(F32)
        for j in range(3):
            acc = acc + l_ref[j].astype(F32)
        o_ref[...] = acc

    return pl.pallas_call(
        body, name=f"sum_landed_{name}",
        grid_spec=pltpu.PrefetchScalarGridSpec(
            num_scalar_prefetch=1, grid=(nrt, c // tc),
            in_specs=[pl.BlockSpec((None, tm, tc), lambda i, j, p: (p[0], i, j)),
                      pl.BlockSpec((3, tm, tc), lambda i, j, p: (0, i, j))],
            out_specs=pl.BlockSpec((tm, tc), lambda i, j, p: (p[1] * nrt + i, j))),
        out_shape=jax.ShapeDtypeStruct((2 * r, c), F32), compiler_params=_cp(cfg, ("parallel", "parallel")),
    )(pos, pre, land)


def half_swap(cfg, sums):
    names = list(sums)
    n = len(names)

    def body(*refs):
        outs = refs[n:2 * n]
        ssem, rsem = refs[2 * n:]
        x, y, c = _me()
        cps = [_remote(_rows_half(outs[i], c), _rows_half(outs[i], c), ssem.at[i], rsem.at[i], (x, y, 1 - c))
               for i in range(n)]
        for cp in cps:
            cp.start()
        for i in range(n):
            theirs = _rows_half(outs[i], 1 - c)
            _remote(theirs, theirs, ssem.at[i], rsem.at[i], (x, y, 1 - c)).wait_recv()
        for cp in cps:
            cp.wait_send()

    outs = pl.pallas_call(
        body, name="half_swap_" + "_".join(names), in_specs=[ANY] * n, out_specs=[ANY] * n,
        out_shape=[jax.ShapeDtypeStruct(sums[k].shape, F32) for k in names],
        input_output_aliases={i: i for i in range(n)},
        scratch_shapes=[pltpu.SemaphoreType.DMA((n,))] * 2,
    )(*[sums[k] for k in names])
    return dict(zip(names, outs))


class MeshWeights:
    def __init__(self, cfg, w_sh):
        self.cfg = cfg
        self.pos = jnp.stack([2 * lax.axis_index("x") + lax.axis_index("y"), lax.axis_index("c")]).astype(jnp.int32)
        self.w_sh = w_sh
        self.full = {"w_in": cast_into(cfg, "w_in", w_sh["w_in"], self.pos)[0],
                     "conv_w": pad_conv_w(cfg, w_sh["conv_w"], self.pos)}
        self.inflight = {}
        self.forwards = {}
        self.grads = {}

    def first_start(self):
        cfg = self.cfg
        self.first, token = gather_start(cfg, "first", FIRST, self.full, jnp.zeros((8, LANE), F32))
        for k in REST:
            self.full[k], token = cast_into(cfg, k, self.w_sh[k], self.pos, dep=token)
        return token

    def weights_first(self, after):
        cfg = self.cfg
        ssem, rsem, lands = self.first[0]
        w_in, conv_w = gather_wait(cfg, FIRST[0], ssem, rsem, lands, after)
        return gather_finish(cfg, ("w_in",), [w_in])[0], conv_w

    def start_rest(self, after):
        self.rest, token = gather_start(self.cfg, "rest", GROUPS, self.full, after)
        return token

    def weights_rest(self, group, after):
        cfg = self.cfg
        names = GROUPS[group]
        ssem, rsem, lands = self.rest[group]
        lands = dict(zip(names, gather_wait(cfg, names, ssem, rsem, lands, after)))
        now = [k for k in names if k != "w_up"]
        later = [k for k in names if k == "w_up"]
        ready = gather_finish(cfg, tuple(now), [lands[k] for k in now])
        if not later:
            return tuple(ready), None
        out = forward_start(cfg, tuple(later), [lands[k] for k in later], ready[0])
        self.forwards[group] = (tuple(later),) + tuple(out[:3])
        return tuple(ready), out[3]

    def forwarded(self, group, after):
        names, ssem, rsem, lands = self.forwards.pop(group)
        return tuple(forward_wait(self.cfg, names, ssem, rsem, lands, after))

    def pair_start(self, grads):
        out = pair_start(self.cfg, grads, jnp.zeros((8, LANE), F32))
        self.pairs = out[:4]
        return out[4]

    def reduce_start(self, grads, after=None):
        theirs = pair_send(self.cfg, grads)
        if after is not None:
            early, early_theirs = pair_wait(self.cfg, *self.pairs, after)
            grads, theirs = {**early, **grads}, {**early_theirs, **theirs}
        pres = {k: pair_sum(self.cfg, k, grads[k], theirs[k], self.pos) for k in grads}
        out = scatter_start(self.cfg, pres, jnp.zeros((8, LANE), F32))
        self.inflight[tuple(sorted(grads))] = out[:4]
        return out[4]

    def reduce_wait(self, names, after):
        cfg = self.cfg
        pres, lands = scatter_wait(cfg, *self.inflight.pop(tuple(sorted(names))), after)
        sums = {k: sum_landed(cfg, k, pres[k], lands[k], self.pos) for k in names}
        self.grads.update(half_swap(cfg, sums))


def allreduce_small(cfg, vec):
    R = vec.shape[0]

    def body(v_ref, o_ref, buf, send_sems, recv_sems):
        x, y, c = _me()
        me = 4 * x + 2 * y + c
        buf[me] = v_ref[...]
        sends = []
        for k in range(1, N_DEV):
            px, py, pc = x ^ (k >> 2), y ^ ((k >> 1) & 1), c ^ (k & 1)
            sends.append(pltpu.make_async_remote_copy(
                src_ref=v_ref, dst_ref=buf.at[me], send_sem=send_sems.at[k], recv_sem=recv_sems.at[k],
                device_id=(px, py, pc), device_id_type=MESH))
        for cp in sends:
            cp.start()
        for k in range(1, N_DEV):
            px, py, pc = x ^ (k >> 2), y ^ ((k >> 1) & 1), c ^ (k & 1)
            pltpu.make_async_remote_copy(
                src_ref=v_ref, dst_ref=buf.at[4 * px + 2 * py + pc], send_sem=send_sems.at[k],
                recv_sem=recv_sems.at[k], device_id=(px, py, pc), device_id_type=MESH).wait_recv()
        for cp in sends:
            cp.wait_send()
        acc = buf[0]
        for j in range(1, N_DEV):
            acc = acc + buf[j]
        o_ref[...] = acc

    return pl.pallas_call(
        body, name="allreduce_small",
        in_specs=[pl.BlockSpec(memory_space=pltpu.VMEM)], out_specs=pl.BlockSpec(memory_space=pltpu.VMEM),
        out_shape=jax.ShapeDtypeStruct((R, LANE), F32),
        scratch_shapes=[pltpu.VMEM((N_DEV, R, LANE), F32), pltpu.SemaphoreType.DMA((N_DEV,)),
                        pltpu.SemaphoreType.DMA((N_DEV,))],
    )(vec)


def adamw(cfg, name, w, m, v, g_parts, tile):
    r, c = w.shape
    tm, tc = tile[0] or r, tile[1] or c
    assert tc == c or all(g.shape[1] == c for g in g_parts)
    n = len(g_parts)
    bc1 = 1.0 - ADAM_B1 ** ADAM_STEP
    bc2 = 1.0 - ADAM_B2 ** ADAM_STEP

    def body(*refs):
        w_ref, m_ref, v_ref = refs[:3]
        g_refs = refs[3:3 + n]
        g_out, d_out, m_out, v_out = refs[3 + n:]
        g = g_refs[0][:, :tc]
        for gr in g_refs[1:]:
            g = g + gr[:, :tc]
        m_new = ADAM_B1 * m_ref[...] + (1.0 - ADAM_B1) * g
        v_new = ADAM_B2 * v_ref[...] + (1.0 - ADAM_B2) * jnp.square(g)
        m_hat = m_new / bc1
        v_hat = v_new / bc2
        g_out[...] = g
        d_out[...] = -ADAM_LR * (m_hat / (jnp.sqrt(v_hat) + ADAM_EPS) + ADAM_WD * w_ref[...])
        m_out[...] = m_new
        v_out[...] = v_new

    blk = pl.BlockSpec((tm, tc), lambda i, j: (i, j))
    return pl.pallas_call(
        body, name=f"adamw_{name}", grid=(r // tm, c // tc),
        in_specs=[blk] * 3 + [pl.BlockSpec((tm, tc if tc < c else g.shape[1]), lambda i, j: (i, j)) for g in g_parts],
        out_specs=[blk] * 4, out_shape=[jax.ShapeDtypeStruct((r, c), F32)] * 4,
        compiler_params=_cp(cfg, ("parallel", "parallel")),
    )(w, m, v, *g_parts)


SMALL_ORDER = ("loss", "g1", "g2", "g3", "g4", "g_sb", "g_dl", "conv_b", "conv_w")


def pack_small(small):
    rows = []
    for k in SMALL_ORDER:
        a = small[k].reshape(-1, LANE)
        rows.append(a)
    flat = jnp.concatenate(rows, axis=0)
    pad = (-flat.shape[0]) % 8
    return jnp.pad(flat, ((0, pad), (0, 0))), [r.shape[0] for r in rows]


def unpack_small(red, small, counts):
    out, at = {}, 0
    for k, n in zip(SMALL_ORDER, counts):
        out[k] = red[at:at + n].reshape(small[k].shape)
        at += n
    return out


def pad_ff(cfg, a):
    r = a.shape[0]
    return jnp.pad(a.reshape(r, N_CHIPS, cfg.FSH), ((0, 0), (0, 0), (0, cfg.FSHP - cfg.FSH))).reshape(r, cfg.FF2P)


def step(cfg, x, target, gains, w_sh, conv_b, m_all, v_all):
    chip = 2 * lax.axis_index("x") + lax.axis_index("y")
    comm = MeshWeights(cfg, w_sh)
    grad_x, small = local_step(cfg, comm, x, target, gains["g1"], gains["g2"], gains["g3"], gains["g4"],
                               gains["g_sb"], gains["g_dl"], pad_ff(cfg, conv_b))

    packed, counts = pack_small(small)
    summed = allreduce_small(cfg, packed)
    comm.reduce_wait(("w_in",), after=summed)
    red = unpack_small(summed, small, counts)

    names = ("w_in", "w_out", "w_up", "w_down")
    up_rows = max(t for t in range(SUB, 513, SUB) if cfg.FSH % t == 0)
    tms = dict(w_in=(cfg.TM, None), w_out=(cfg.TM, None), w_up=(up_rows, None), w_down=(None, cfg.TN // 2))
    res = {}
    for n in names:
        res[n] = adamw(cfg, n, w_sh[n], m_all[n], v_all[n], [comm.grads[n]], tms[n])
    g_cw = lax.dynamic_slice_in_dim(red["conv_w"].reshape(3, N_CHIPS, cfg.FSHP), chip, 1, axis=1)[:, 0, :cfg.FSH]
    res["conv_w"] = adamw(cfg, "conv_w", w_sh["conv_w"], m_all["conv_w"], v_all["conv_w"], [g_cw], (None, None))
    g_cb = red["conv_b"].reshape(1, N_CHIPS, cfg.FSHP)[:, :, :cfg.FSH].reshape(1, N_CHIPS * cfg.FSH)
    res["conv_b"] = adamw(cfg, "conv_b", conv_b, m_all["conv_b"], v_all["conv_b"], [g_cb], (None, None))
    for k in ("g1", "g2", "g3", "g4", "g_sb", "g_dl"):
        res[k] = adamw(cfg, k, gains[k], m_all[k], v_all[k], [red[k]], (None, None))
    return red["loss"][0, 0], grad_x, res


PARAMS = ("pre_mix_gain", "post_mix_gain", "pre_ffn_gain", "post_ffn_gain", "w_in", "sb_out_gain", "dil_out_gain",
          "w_out", "w_up", "conv_w", "conv_b", "w_down")
SHORT = dict(pre_mix_gain="g1", post_mix_gain="g2", pre_ffn_gain="g3", post_ffn_gain="g4", sb_out_gain="g_sb",
             dil_out_gain="g_dl", w_in="w_in", w_out="w_out", w_up="w_up", conv_w="conv_w", conv_b="conv_b",
             w_down="w_down")


def kernel(x, pre_mix_gain, post_mix_gain, pre_ffn_gain, post_ffn_gain, w_in, sb_out_gain, dil_out_gain, w_out, w_up, conv_w, conv_b, w_down, loss_target, m_pre_mix_gain, m_post_mix_gain, m_pre_ffn_gain, m_post_ffn_gain, m_w_in, m_sb_out_gain, m_dil_out_gain, m_w_out, m_w_up, m_conv_w, m_conv_b, m_w_down, v_pre_mix_gain, v_post_mix_gain, v_pre_ffn_gain, v_post_ffn_gain, v_w_in, v_sb_out_gain, v_dil_out_gain, v_w_out, v_w_up, v_conv_w, v_conv_b, v_w_down):
    cfg = CFG
    w = dict(zip(PARAMS, (pre_mix_gain, post_mix_gain, pre_ffn_gain, post_ffn_gain, w_in, sb_out_gain, dil_out_gain,
                          w_out, w_up, conv_w, conv_b, w_down)))
    m = dict(zip(PARAMS, (m_pre_mix_gain, m_post_mix_gain, m_pre_ffn_gain, m_post_ffn_gain, m_w_in, m_sb_out_gain,
                          m_dil_out_gain, m_w_out, m_w_up, m_conv_w, m_conv_b, m_w_down)))
    v = dict(zip(PARAMS, (v_pre_mix_gain, v_post_mix_gain, v_pre_ffn_gain, v_post_ffn_gain, v_w_in, v_sb_out_gain,
                          v_dil_out_gain, v_w_out, v_w_up, v_conv_w, v_conv_b, v_w_down)))
    sq = lambda a: a.reshape(a.shape[1:])
    ws = {SHORT[k]: sq(a) if a.ndim == 3 else a for k, a in w.items()}
    ms = {SHORT[k]: sq(a) if a.ndim == 3 else a for k, a in m.items()}
    vs = {SHORT[k]: sq(a) if a.ndim == 3 else a for k, a in v.items()}
    for d in (ws, ms, vs):
        d["w_up"] = d["w_up"].T
    gains = {k: ws[k] for k in ("g1", "g2", "g3", "g4", "g_sb", "g_dl")}
    w_sh = {k: ws[k] for k in ("w_in", "w_out", "w_up", "conv_w", "w_down")}
    loss, grad_x, res = step(cfg, sq(x), sq(loss_target), gains, w_sh, ws["conv_b"], ms, vs)
    res["w_up"] = [a.T for a in res["w_up"]]
    outs = [loss, grad_x.reshape(x.shape)]
    for i in range(4):
        for k in PARAMS:
            outs.append(res[SHORT[k]][i].reshape(w[k].shape))
    return tuple(outs)
```

```python
import functools
import math
from typing import NamedTuple

import jax
import jax.numpy as jnp
from jax import lax
from jax.experimental import pallas as pl
from jax.experimental.pallas import tpu as pltpu

F32 = jnp.float32
BF16 = jnp.bfloat16
MESH = pl.DeviceIdType.MESH

ROPE_THETA = 10000.0
RMS_EPS = 1e-6
ADAM_LR = 0.001
ADAM_B1 = 0.9
ADAM_B2 = 0.999
ADAM_EPS = 1e-08
ADAM_WD = 0.01
ADAM_STEP = 10
GELU_C = math.sqrt(2.0 / math.pi)
NEG_BIG = -1e30
LANE = 128
N_CHIPS = 4
N_DEV = 8


class Cfg(NamedTuple):
    S: int = 2048
    D: int = 2048
    DH: int = 128
    HSB: int = 8
    HDL: int = 8
    QB: int = 128
    SBT: int = 256
    SBH: int = 4
    SBHB: int = 4
    branches: tuple = ((128, 1), (512, 4), (2048, 16))
    FSH: int = 2752
    FSHP: int = 2816
    TM: int = 256
    TNF: int = 256
    FCH: int = 512
    TN: int = 512
    VMEM_MB: int = 56

    @property
    def DSB(self):
        return self.HSB * self.DH

    @property
    def DDL(self):
        return self.HDL * self.DH

    @property
    def DMIX(self):
        return self.DSB + self.DDL

    @property
    def FFP(self):
        return 2 * self.FSHP

    @property
    def FF2P(self):
        return 4 * self.FSHP


CFG = Cfg()


def _cp(cfg, sem=None):
    return pltpu.CompilerParams(dimension_semantics=sem, vmem_limit_bytes=cfg.VMEM_MB * 2**20)


def _dot(a, b):
    return jnp.dot(a, b, preferred_element_type=F32)


def _dot_nt(a, b):
    return lax.dot_general(a, b, (((1,), (1,)), ((), ())), preferred_element_type=F32)


def _dot_tn(a, b):
    return lax.dot_general(a, b, (((0,), (0,)), ((), ())), preferred_element_type=F32)


def _dot_split(x, u):
    hi = x.astype(BF16)
    lo = (x - hi.astype(F32)).astype(BF16)
    return _dot(hi, u) + _dot(lo, u)


def _rstd(x):
    return lax.rsqrt(jnp.mean(x * x, axis=-1, keepdims=True) + RMS_EPS)


def _rms_bwd(dy, x, g):
    r = _rstd(x)
    xh = x * r
    dxh = dy * g
    dx = r * (dxh - xh * jnp.mean(dxh * xh, axis=-1, keepdims=True))
    return dx, dy * xh


def _gelu(x):
    t = jnp.tanh(GELU_C * (x + 0.044715 * (x * x * x)))
    return 0.5 * x * (1.0 + t), t


def _gelu_grad(x, t):
    return 0.5 * (1.0 + t) + 0.5 * x * (1.0 - t * t) * (GELU_C * (1.0 + 3 * 0.044715 * (x * x)))


def _row(cfg, w):
    return pl.BlockSpec((cfg.TM, w), lambda i: (i, 0))


def _vec(w):
    return pl.BlockSpec((1, w), lambda i: (0, 0))


def rms_fwd(cfg, x, g):
    S, D = x.shape

    def body(x_ref, g_ref, h_ref):
        xv = x_ref[...]
        h_ref[...] = (xv * _rstd(xv) * g_ref[...]).astype(BF16)

    return pl.pallas_call(
        body, name="rms_fwd", grid=(S // cfg.TM,),
        in_specs=[_row(cfg, D), _vec(D)], out_specs=_row(cfg, D),
        out_shape=jax.ShapeDtypeStruct((S, D), BF16), compiler_params=_cp(cfg, ("parallel",)),
    )(x, g)


def mid_fwd(cfg, x, mo, g_post, g_pre):
    S, D = x.shape

    def body(x_ref, mo_ref, gp_ref, gn_ref, x1_ref, h2_ref):
        mo_v = mo_ref[...]
        x1 = x_ref[...] + mo_v * _rstd(mo_v) * gp_ref[...]
        x1_ref[...] = x1
        h2_ref[...] = (x1 * _rstd(x1) * gn_ref[...]).astype(BF16)

    return pl.pallas_call(
        body, name="mid_fwd", grid=(S // cfg.TM,),
        in_specs=[_row(cfg, D), _row(cfg, D), _vec(D), _vec(D)],
        out_specs=[_row(cfg, D), _row(cfg, D)],
        out_shape=[jax.ShapeDtypeStruct((S, D), F32), jax.ShapeDtypeStruct((S, D), BF16)],
        compiler_params=_cp(cfg, ("parallel",)),
    )(x, mo, g_post, g_pre)


def final_fwd_bwd(cfg, x1, f, g_post, target):
    S, D = x1.shape

    def body(x1_ref, f_ref, g_ref, t_ref, dout_ref, df_ref, dg_ref, loss_ref):
        @pl.when(pl.program_id(0) == 0)
        def _():
            dg_ref[...] = jnp.zeros_like(dg_ref)
            loss_ref[...] = jnp.zeros_like(loss_ref)

        fv = f_ref[...]
        g = g_ref[...]
        out = x1_ref[...] + fv * _rstd(fv) * g
        err = out - t_ref[...]
        loss_ref[...] += 0.5 * jnp.sum(jnp.mean(err * err, axis=-1, keepdims=True), axis=0, keepdims=True)
        dout = err * (1.0 / D)
        dout_ref[...] = dout
        df, dgx = _rms_bwd(dout, fv, g)
        df_ref[...] = df.astype(BF16)
        dg_ref[...] += jnp.sum(dgx, axis=0, keepdims=True)

    return pl.pallas_call(
        body, name="final_fwd_bwd", grid=(S // cfg.TM,),
        in_specs=[_row(cfg, D), _row(cfg, D), _vec(D), _row(cfg, D)],
        out_specs=[_row(cfg, D), _row(cfg, D), _vec(D), _vec(LANE)],
        out_shape=[jax.ShapeDtypeStruct((S, D), F32), jax.ShapeDtypeStruct((S, D), BF16),
                   jax.ShapeDtypeStruct((1, D), F32), jax.ShapeDtypeStruct((1, LANE), F32)],
        compiler_params=_cp(cfg, ("arbitrary",)),
    )(x1, f, g_post, target)


def mid_bwd(cfg, dh2, x1, g_pre, dout, mo, g_post):
    S, D = x1.shape

    def body(dh_ref, x1_ref, gn_ref, do_ref, mo_ref, gp_ref, dx1_ref, dmo_ref, dgn_ref, dgp_ref):
        @pl.when(pl.program_id(0) == 0)
        def _():
            dgn_ref[...] = jnp.zeros_like(dgn_ref)
            dgp_ref[...] = jnp.zeros_like(dgp_ref)

        dx, dgx = _rms_bwd(dh_ref[...], x1_ref[...], gn_ref[...])
        dx1 = do_ref[...] + dx
        dx1_ref[...] = dx1
        dgn_ref[...] += jnp.sum(dgx, axis=0, keepdims=True)
        dmo, dgy = _rms_bwd(dx1, mo_ref[...], gp_ref[...])
        dmo_ref[...] = dmo.astype(BF16)
        dgp_ref[...] += jnp.sum(dgy, axis=0, keepdims=True)

    return pl.pallas_call(
        body, name="mid_bwd", grid=(S // cfg.TM,),
        in_specs=[_row(cfg, D), _row(cfg, D), _vec(D), _row(cfg, D), _row(cfg, D), _vec(D)],
        out_specs=[_row(cfg, D), _row(cfg, D), _vec(D), _vec(D)],
        out_shape=[jax.ShapeDtypeStruct((S, D), F32), jax.ShapeDtypeStruct((S, D), BF16),
                   jax.ShapeDtypeStruct((1, D), F32), jax.ShapeDtypeStruct((1, D), F32)],
        compiler_params=_cp(cfg, ("arbitrary",)),
    )(dh2, x1, g_pre, dout, mo, g_post)


def first_bwd(cfg, dh1, x, g_pre, dx1):
    S, D = x.shape

    def body(dh_ref, x_ref, g_ref, r_ref, dx_ref, dg_ref):
        @pl.when(pl.program_id(0) == 0)
        def _():
            dg_ref[...] = jnp.zeros_like(dg_ref)

        dx, dgx = _rms_bwd(dh_ref[...], x_ref[...], g_ref[...])
        dx_ref[...] = r_ref[...] + dx
        dg_ref[...] += jnp.sum(dgx, axis=0, keepdims=True)

    return pl.pallas_call(
        body, name="first_bwd", grid=(S // cfg.TM,),
        in_specs=[_row(cfg, D), _row(cfg, D), _vec(D), _row(cfg, D)],
        out_specs=[_row(cfg, D), _vec(D)],
        out_shape=[jax.ShapeDtypeStruct((S, D), F32), jax.ShapeDtypeStruct((1, D), F32)],
        compiler_params=_cp(cfg, ("arbitrary",)),
    )(dh1, x, g_pre, dx1)


def _mm(cfg, name, a, b, *, nt, a_spec, b_spec, o_spec, grid, out_shape, acc_shape, dep=None):
    nk = grid[-1]
    dot = _dot_nt if nt else _dot
    deps = [] if dep is None else [dep]

    def body(a_ref, b_ref, *rest):
        o_ref, acc_ref = rest[-2:]
        k = pl.program_id(len(grid) - 1)
        part = dot(a_ref[...], b_ref[...])
        if deps:
            part = part + rest[0][0:1, 0:1]
        if nk == 1:
            o_ref[...] = part.astype(o_ref.dtype)
            return

        @pl.when(k == 0)
        def _():
            acc_ref[...] = part

        @pl.when(k > 0)
        def _():
            acc_ref[...] += part

        @pl.when(k == nk - 1)
        def _():
            o_ref[...] = acc_ref[...].astype(o_ref.dtype)

    sem = ("parallel",) * (len(grid) - 1) + ("arbitrary",)
    dep_specs = [pl.BlockSpec((8, LANE), lambda *_: (0, 0))] * len(deps)
    return pl.pallas_call(
        body, name=name, grid=grid, in_specs=[a_spec, b_spec] + dep_specs, out_specs=o_spec, out_shape=out_shape,
        scratch_shapes=[pltpu.VMEM(acc_shape, F32)], compiler_params=_cp(cfg, sem),
    )(a, b, *deps)


def _mm_tn(cfg, name, a, b, *, a_spec, b_spec, o_spec, grid, out_shape):
    def body(a_ref, b_ref, o_ref):
        o_ref[...] = _dot_tn(a_ref[...], b_ref[...]).astype(o_ref.dtype)

    return pl.pallas_call(
        body, name=name, grid=grid, in_specs=[a_spec, b_spec], out_specs=o_spec, out_shape=out_shape,
        compiler_params=_cp(cfg, ("parallel",) * len(grid)),
    )(a, b)


def qkv_proj(cfg, h1, w_in, cos2, sin2):
    S, D = h1.shape
    tn = 2 * cfg.DH
    per = cfg.DSB // tn
    assert cfg.DSB == cfg.DDL
    nblk = 6 * per

    def body(a_ref, b_ref, c_ref, s_ref, o_ref):
        j = pl.program_id(0)
        acc = _dot(a_ref[...], b_ref[...])
        rope = jnp.logical_and(j >= 3 * per, j < 5 * per)

        @pl.when(rope)
        def _():
            for c in range(tn // cfg.DH):
                xh = acc[:, c * cfg.DH:(c + 1) * cfg.DH]
                o_ref[:, c * cfg.DH:(c + 1) * cfg.DH] = (
                    xh * c_ref[...] + pltpu.roll(xh, cfg.DH // 2, 1) * s_ref[...]).astype(BF16)

        @pl.when(jnp.logical_not(rope))
        def _():
            o_ref[...] = acc.astype(BF16)

    return pl.pallas_call(
        body, name="qkv_proj", grid=(nblk,),
        in_specs=[pl.BlockSpec((S, D), lambda j: (0, 0)), pl.BlockSpec((D, tn), lambda j: (0, j)),
                  pl.BlockSpec((S, cfg.DH), lambda j: (0, 0)), pl.BlockSpec((S, cfg.DH), lambda j: (0, 0))],
        out_specs=pl.BlockSpec((None, S, tn), lambda j: (j // per, 0, j % per)),
        out_shape=jax.ShapeDtypeStruct((6, S, cfg.DSB), BF16),
        compiler_params=_cp(cfg, ("parallel",)),
    )(h1, w_in, cos2, sin2)


def _sb_tile(cfg, q, k, valid):
    z = _dot_nt(q, k) * (cfg.DH ** -0.5)
    lb = jnp.minimum(z, 0.0) - jnp.log1p(jnp.exp(-jnp.abs(z)))
    lk = lb - z
    return lb, (lk if valid is None else jnp.where(valid, lk, 0.0))


def _masked(valid, x):
    return x if valid is None else jnp.where(valid, x, 0.0)


def sb_fwd(cfg, qkv3):
    S, QB, DH, NH = cfg.S, cfg.SBT, cfg.DH, cfg.SBH

    def body(q_ref, k_ref, v_ref, o_ref, t_ref):
        row = lax.broadcasted_iota(jnp.int32, (QB, QB), 0)
        col = lax.broadcasted_iota(jnp.int32, (QB, QB), 1)
        u_after = (row > col).astype(BF16)
        causal = col < row
        heads = [slice(h * DH, (h + 1) * DH) for h in range(NH)]

        def q_loop(qb, _):
            rows = pl.ds(pl.multiple_of(qb * QB, QB), QB)
            qs = [q_ref[rows, hd] for hd in heads]

            def tile(kb, carry, valid):
                krows = pl.ds(pl.multiple_of(kb * QB, QB), QB)
                lbk = [_sb_tile(cfg, q, k_ref[krows, hd], valid) for q, hd in zip(qs, heads)]
                rems = [_dot_split(lk, u_after) for _, lk in lbk]
                aa = [_masked(valid, jnp.exp(lb + rem + c)).astype(BF16) for (lb, _), rem, (_, c) in zip(lbk, rems, carry)]
                return tuple((o_acc + _dot(a, v_ref[krows, hd]), c + jnp.sum(lk, axis=1, keepdims=True))
                             for a, hd, (_, lk), (o_acc, c) in zip(aa, heads, lbk, carry))

            carry = tile(qb, ((jnp.zeros((QB, DH), F32), jnp.zeros((QB, 1), F32)),) * NH, causal)
            carry = lax.fori_loop(0, qb, lambda i, cr: tile(qb - 1 - i, cr, None), carry)
            for hd, (o_acc, c) in zip(heads, carry):
                o_ref[rows, hd] = o_acc
                t_ref[rows, hd] = jnp.broadcast_to(c, (QB, DH))
            return 0

        lax.fori_loop(0, S // QB, q_loop, 0)

    def spec(i):
        return pl.BlockSpec((None, S, NH * DH), lambda h: (i, 0, h))

    return pl.pallas_call(
        body, name="sb_fwd", grid=(cfg.HSB // NH,),
        in_specs=[spec(0), spec(1), spec(2)],
        out_specs=[pl.BlockSpec((S, NH * DH), lambda h: (0, h))] * 2,
        out_shape=[jax.ShapeDtypeStruct((S, cfg.DSB), F32)] * 2,
        compiler_params=_cp(cfg, ("parallel",)),
    )(qkv3, qkv3, qkv3)


def sb_bwd(cfg, qkv3, do_sb, tsum):
    S, QB, DH, NH = cfg.S, cfg.SBT, cfg.DH, cfg.SBHB
    scale = DH ** -0.5

    def body(q_ref, k_ref, v_ref, do_ref, t_ref, d_ref, dk_acc, dv_acc):
        dk_acc[...] = jnp.zeros_like(dk_acc)
        dv_acc[...] = jnp.zeros_like(dv_acc)
        row = lax.broadcasted_iota(jnp.int32, (QB, QB), 0)
        col = lax.broadcasted_iota(jnp.int32, (QB, QB), 1)
        u_upto = (row <= col).astype(BF16)
        u_before = (row < col).astype(BF16)
        causal = col < row
        heads = [slice(h * DH, (h + 1) * DH) for h in range(NH)]

        def q_loop(qb, _):
            rows = pl.ds(pl.multiple_of(qb * QB, QB), QB)
            qs = [q_ref[rows, hd] for hd in heads]
            dos = [do_ref[rows, hd] for hd in heads]
            totals = [t_ref[rows, hd.start:hd.start + 1] for hd in heads]

            def tile(kb, carry, valid):
                krows = pl.ds(pl.multiple_of(kb * QB, QB), QB)
                ks = [k_ref[krows, hd] for hd in heads]
                lbk = [_sb_tile(cfg, q, k, valid) for q, k in zip(qs, ks)]
                das = [_dot_nt(do, v_ref[krows, hd]) for do, hd in zip(dos, heads)]
                pins = [_dot_split(lk, u_upto) for _, lk in lbk]
                aa = [_masked(valid, jnp.exp(lb + (tot - pc - pin)))
                      for (lb, _), tot, (_, pc, _), pin in zip(lbk, totals, carry, pins)]
                gs = [a * da for a, da in zip(aa, das)]
                for a, do, hd in zip(aa, dos, heads):
                    dv_acc[krows, hd] += _dot_tn(a.astype(BF16), do)
                cums = [gc + _dot(g.astype(BF16), u_before) for g, (_, _, gc) in zip(gs, carry)]
                dzs = [(_masked(valid, g - jnp.exp(lb) * (g + cum)) * scale).astype(BF16)
                       for g, (lb, _), cum in zip(gs, lbk, cums)]
                for dz, q, hd in zip(dzs, qs, heads):
                    dk_acc[krows, hd] += _dot_tn(dz, q)
                return tuple((dq + _dot(dz, k), pc + jnp.sum(lk, axis=1, keepdims=True), gc + jnp.sum(g, axis=1, keepdims=True))
                             for dz, k, (_, lk), g, (dq, pc, gc) in zip(dzs, ks, lbk, gs, carry))

            z1 = jnp.zeros((QB, 1), F32)
            carry = lax.fori_loop(0, qb, lambda kb, cr: tile(kb, cr, None), ((jnp.zeros((QB, DH), F32), z1, z1),) * NH)
            for hd, (dq_acc, _, _) in zip(heads, tile(qb, carry, causal)):
                d_ref[0, rows, hd] = dq_acc.astype(BF16)
            return 0

        lax.fori_loop(0, S // QB, q_loop, 0)
        d_ref[1, :, :] = dk_acc[...].astype(BF16)
        d_ref[2, :, :] = dv_acc[...].astype(BF16)

    def spec(i):
        return pl.BlockSpec((None, S, NH * DH), lambda h: (i, 0, h))

    hd_spec = pl.BlockSpec((S, NH * DH), lambda h: (0, h))
    return pl.pallas_call(
        body, name="sb_bwd", grid=(cfg.HSB // NH,),
        in_specs=[spec(0), spec(1), spec(2), hd_spec, hd_spec],
        out_specs=pl.BlockSpec((3, S, NH * DH), lambda h: (0, 0, h)),
        out_shape=jax.ShapeDtypeStruct((6, S, cfg.DSB), BF16),
        scratch_shapes=[pltpu.VMEM((S, NH * DH), F32), pltpu.VMEM((S, NH * DH), F32)],
        compiler_params=_cp(cfg, ("parallel",)),
    )(qkv3, qkv3, qkv3, do_sb, tsum)


def _band_mask(cfg, n, n_back):
    QB = cfg.QB
    qi = lax.broadcasted_iota(jnp.int32, (QB, 2 * QB), 0)
    kj = lax.broadcasted_iota(jnp.int32, (QB, 2 * QB), 1)
    dist = QB + qi - kj
    return (dist >= 0) & (dist <= n_back) & jnp.logical_or(n > 0, kj >= QB)


def _sub_rows(start, n, dil):
    if dil > 1:
        return pl.ds(start, n, stride=dil)
    return pl.ds(start if isinstance(start, int) else pl.multiple_of(start, 8), n)


def _stage_residues(cfg, dil, pairs):
    QB, L = cfg.QB, cfg.S // dil
    for src, dst in pairs:
        for r in range(dil):
            dst[pl.ds(r * (QB + L), QB), :] = jnp.zeros((QB, cfg.DH), BF16)
            dst[pl.ds(r * (QB + L) + QB, L), :] = src[_sub_rows(r, L, dil), :].astype(BF16)


def _staged_rows(cfg):
    return cfg.S + cfg.QB * max(d for _, d in cfg.branches)


def _lane_value(x):
    return jnp.max(x, axis=1, keepdims=True)


def dil_fwd(cfg, qkv3):
    S, QB, DH = cfg.S, cfg.QB, cfg.DH
    scale = DH ** -0.5
    nb = len(cfg.branches)
    mix_rows = min(256, S)

    def body(q_ref, k_ref, v_ref, o_ref, lt_ref, qf, kf, vf, kp, vp, *obl):
        obs, lbs = obl[:nb], obl[nb:]
        qf[...] = q_ref[...].astype(F32)
        kf[...] = k_ref[...].astype(F32)
        vf[...] = v_ref[...].astype(F32)
        for b, (window, dil) in enumerate(cfg.branches):
            L, n_back = S // dil, window // dil
            assert n_back <= QB and L % QB == 0
            _stage_residues(cfg, dil, [(kf, kp), (vf, vp)])
            for r in range(dil):
                for n in range(L // QB):
                    rows = _sub_rows(r + n * (QB * dil), QB, dil)
                    band = pl.ds(r * (QB + L) + n * QB, 2 * QB)
                    s = _dot_nt(qf[rows, :].astype(BF16), kp[band, :]) * scale
                    s = jnp.where(_band_mask(cfg, n, n_back), s, NEG_BIG)
                    m = jnp.max(s, axis=1, keepdims=True)
                    p = jnp.exp(s - m)
                    den = jnp.sum(p, axis=1, keepdims=True)
                    obs[b][rows, :] = _dot(p.astype(BF16), vp[band, :]) / den
                    lbs[b][rows, :] = jnp.broadcast_to(m + jnp.log(den), (QB, DH))

        def mix(i, _):
            rows = pl.ds(pl.multiple_of(i * mix_rows, mix_rows), mix_rows)
            ls = [r[rows, :] for r in lbs]
            m = functools.reduce(jnp.maximum, ls)
            es = [jnp.exp(l - m) for l in ls]
            tot = functools.reduce(jnp.add, es)
            o_ref[rows, :] = functools.reduce(jnp.add, [(e / tot) * r[rows, :] for e, r in zip(es, obs)])
            lt_ref[rows, :] = m + jnp.log(tot)
            return 0

        lax.fori_loop(0, S // mix_rows, mix, 0)

    def spec(i):
        return pl.BlockSpec((None, S, DH), lambda h: (i, 0, h))

    o_spec = pl.BlockSpec((S, DH), lambda h: (0, h))
    return pl.pallas_call(
        body, name="dil_fwd", grid=(cfg.HDL,),
        in_specs=[spec(3), spec(4), spec(5)], out_specs=[o_spec, o_spec],
        out_shape=[jax.ShapeDtypeStruct((S, cfg.DDL), F32)] * 2,
        scratch_shapes=[pltpu.VMEM((S, DH), F32)] * 3 + [pltpu.VMEM((_staged_rows(cfg), DH), BF16)] * 2
        + [pltpu.VMEM((S, DH), F32)] * (2 * nb),
        compiler_params=_cp(cfg, ("parallel",)),
    )(qkv3, qkv3, qkv3)


def dil_bwd(cfg, qkv3, do_dl, delta, lse_tot, cos2, sin2, d_sb3):
    S, QB, DH = cfg.S, cfg.QB, cfg.DH
    scale = DH ** -0.5
    out_rows = min(256, S)

    def body(q_ref, k_ref, v_ref, do_ref, dl_ref, lt_ref, c_ref, s_ref, base_ref, d_ref,
             qf, kf, vf, dof, kp, vp, dkp, dvp, dqn, dkn, dvn):
        qf[...] = q_ref[...].astype(F32)
        kf[...] = k_ref[...].astype(F32)
        vf[...] = v_ref[...].astype(F32)
        dof[...] = do_ref[...].astype(F32)
        for acc in (dqn, dkn, dvn):
            acc[...] = jnp.zeros_like(acc)
        for window, dil in cfg.branches:
            L, n_back = S // dil, window // dil
            reg = QB + L
            _stage_residues(cfg, dil, [(kf, kp), (vf, vp)])
            dkp[pl.ds(0, dil * reg), :] = jnp.zeros((dil * reg, DH), F32)
            dvp[pl.ds(0, dil * reg), :] = jnp.zeros((dil * reg, DH), F32)
            for r in range(dil):
                for n in range(L // QB):
                    rows = _sub_rows(r + n * (QB * dil), QB, dil)
                    band = pl.ds(r * reg + n * QB, 2 * QB)
                    q = qf[rows, :].astype(BF16)
                    do = dof[rows, :].astype(BF16)
                    kb = kp[band, :]
                    s = _dot_nt(q, kb) * scale
                    s = jnp.where(_band_mask(cfg, n, n_back), s, NEG_BIG)
                    p = jnp.exp(s - _lane_value(lt_ref[rows, :]))
                    ds = (p * (_dot_nt(do, vp[band, :]) - _lane_value(dl_ref[rows, :])) * scale).astype(BF16)
                    dqn[rows, :] += _dot(ds, kb)
                    dkp[band, :] += _dot_tn(ds, q)
                    dvp[band, :] += _dot_tn(p.astype(BF16), do)
            for r in range(dil):
                sub = _sub_rows(r, L, dil)
                dkn[sub, :] += dkp[pl.ds(r * reg + QB, L), :]
                dvn[sub, :] += dvp[pl.ds(r * reg + QB, L), :]

        def finish(i, _):
            rows = pl.ds(pl.multiple_of(i * out_rows, out_rows), out_rows)
            c, sn = c_ref[rows, :], s_ref[rows, :]
            for j, acc in enumerate((dqn, dkn)):
                d = acc[rows, :]
                d_ref[j, rows, :] = (d * c + pltpu.roll(d * sn, DH // 2, 1)).astype(BF16)
            d_ref[2, rows, :] = dvn[rows, :].astype(BF16)
            return 0

        lax.fori_loop(0, S // out_rows, finish, 0)

    def spec(i):
        return pl.BlockSpec((None, S, DH), lambda h: (i, 0, h))

    hd = pl.BlockSpec((S, DH), lambda h: (0, h))
    tab = pl.BlockSpec((S, DH), lambda h: (0, 0))
    ns = _staged_rows(cfg)
    return pl.pallas_call(
        body, name="dil_bwd", grid=(cfg.HDL,),
        in_specs=[spec(3), spec(4), spec(5), hd, hd, hd, tab, tab, ANY],
        out_specs=pl.BlockSpec((3, S, DH), lambda h: (1, 0, h)),
        out_shape=jax.ShapeDtypeStruct((6, S, cfg.DDL), BF16),
        input_output_aliases={8: 0},
        scratch_shapes=[pltpu.VMEM((S, DH), F32)] * 4 + [pltpu.VMEM((ns, DH), BF16)] * 2
        + [pltpu.VMEM((ns, DH), F32)] * 2 + [pltpu.VMEM((S, DH), F32)] * 3,
        compiler_params=_cp(cfg, ("parallel",)),
    )(qkv3, qkv3, qkv3, do_dl, delta, lse_tot, cos2, sin2, d_sb3)


def combine_fwd(cfg, o_sb, o_dl, g_sb, g_dl):
    S, DH = cfg.S, cfg.DH

    def head_norm(o, g):
        return o * lax.rsqrt(jnp.mean(o * o, axis=-1, keepdims=True) + RMS_EPS) * g

    def body(osb_ref, odl_ref, gsb_ref, gdl_ref, mix_ref):
        for h in range(cfg.HSB):
            c = slice(h * DH, (h + 1) * DH)
            mix_ref[:, c] = head_norm(osb_ref[:, c], gsb_ref[:, c]).astype(BF16)
        for h in range(cfg.HDL):
            c = slice(h * DH, (h + 1) * DH)
            mix_ref[:, cfg.DSB + h * DH:cfg.DSB + (h + 1) * DH] = head_norm(odl_ref[:, c], gdl_ref[:, c]).astype(BF16)

    return pl.pallas_call(
        body, name="combine_fwd", grid=(S // cfg.TM,),
        in_specs=[_row(cfg, cfg.DSB), _row(cfg, cfg.DDL), _vec(cfg.DSB), _vec(cfg.DDL)],
        out_specs=_row(cfg, cfg.DMIX), out_shape=jax.ShapeDtypeStruct((S, cfg.DMIX), BF16),
        compiler_params=_cp(cfg, ("parallel",)),
    )(o_sb, o_dl, g_sb, g_dl)


def combine_bwd(cfg, dmix, o_sb, o_dl, g_sb, g_dl):
    S, DH = cfg.S, cfg.DH

    def body(dm_ref, osb_ref, odl_ref, gsb_ref, gdl_ref, dsb_ref, ddl_ref, dl_ref, dgsb_ref, dgdl_ref):
        @pl.when(pl.program_id(0) == 0)
        def _():
            dgsb_ref[...] = jnp.zeros_like(dgsb_ref)
            dgdl_ref[...] = jnp.zeros_like(dgdl_ref)

        for h in range(cfg.HSB):
            c = slice(h * DH, (h + 1) * DH)
            dx, dgx = _rms_bwd(dm_ref[:, c], osb_ref[:, c], gsb_ref[:, c])
            dsb_ref[:, c] = dx.astype(BF16)
            dgsb_ref[:, c] += jnp.sum(dgx, axis=0, keepdims=True)
        for h in range(cfg.HDL):
            c = slice(h * DH, (h + 1) * DH)
            o = odl_ref[:, c]
            dx, dgx = _rms_bwd(dm_ref[:, cfg.DSB + h * DH:cfg.DSB + (h + 1) * DH], o, gdl_ref[:, c])
            ddl_ref[:, c] = dx.astype(BF16)
            dl_ref[:, c] = jnp.broadcast_to(jnp.sum(dx * o, axis=-1, keepdims=True), dx.shape)
            dgdl_ref[:, c] += jnp.sum(dgx, axis=0, keepdims=True)

    return pl.pallas_call(
        body, name="combine_bwd", grid=(S // cfg.TM,),
        in_specs=[_row(cfg, cfg.DMIX), _row(cfg, cfg.DSB), _row(cfg, cfg.DDL), _vec(cfg.DSB), _vec(cfg.DDL)],
        out_specs=[_row(cfg, cfg.DSB), _row(cfg, cfg.DDL), _row(cfg, cfg.DDL), _vec(cfg.DSB), _vec(cfg.DDL)],
        out_shape=[jax.ShapeDtypeStruct((S, cfg.DSB), BF16), jax.ShapeDtypeStruct((S, cfg.DDL), BF16),
                   jax.ShapeDtypeStruct((S, cfg.DDL), F32), jax.ShapeDtypeStruct((1, cfg.DSB), F32),
                   jax.ShapeDtypeStruct((1, cfg.DDL), F32)],
        compiler_params=_cp(cfg, ("arbitrary",)),
    )(dmix, o_sb, o_dl, g_sb, g_dl)


SUB = 8


def _shift_down(u, prev, j):
    rolled = pltpu.roll(u, j, 0)
    row = lax.broadcasted_iota(jnp.int32, (SUB, u.shape[1]), 0)
    head = jnp.where(row >= j, rolled[:SUB], pltpu.roll(prev, j, 0))
    return jnp.concatenate([head, rolled[SUB:]], axis=0)


def _shift_up(u, nxt, j):
    n = u.shape[0]
    rolled = pltpu.roll(u, n - j, 0)
    row = lax.broadcasted_iota(jnp.int32, (SUB, u.shape[1]), 0)
    tail = jnp.where(row < SUB - j, rolled[n - SUB:], pltpu.roll(nxt, SUB - j, 0))
    return jnp.concatenate([rolled[:n - SUB], tail], axis=0)


def _conv(u, s1, s2, cw, cb):
    return u * cw[2:3, :] + s1 * cw[1:2, :] + s2 * cw[0:1, :] + cb


def _chunk_rows(cfg):
    ch = min(cfg.FCH, cfg.S)
    return ch, cfg.S // ch


def ffn_fwd(cfg, h2, w_up, conv_w, conv_b):
    S, D = h2.shape
    tn, nt = cfg.TNF, cfg.FFP // cfg.TNF
    ch, nch = _chunk_rows(cfg)

    def body(h_ref, wg_ref, wv_ref, cwg_ref, cwv_ref, cbg_ref, cbv_ref, u_ref, y_ref):
        prev = [jnp.zeros((SUB, tn), F32)] * 2
        pending = None
        for ci in range(nch + 1):
            if ci < nch:
                h = h_ref[pl.ds(ci * ch, ch), :]
                us_next = [_dot_nt(h, wg_ref[...]), _dot_nt(h, wv_ref[...])]
            if pending is not None:
                rows, us = pending
                cs = []
                for i, (cw_ref, cb_ref) in enumerate(((cwg_ref, cbg_ref), (cwv_ref, cbv_ref))):
                    u_ref[i, rows, :] = us[i]
                    cs.append(_conv(us[i], _shift_down(us[i], prev[i], 1), _shift_down(us[i], prev[i], 2),
                                    cw_ref[...], cb_ref[...]))
                y_ref[rows, :] = (_gelu(cs[0])[0] * cs[1]).astype(BF16)
                prev = [u[ch - SUB:] for u in us]
            pending = (pl.ds(ci * ch, ch), us_next) if ci < nch else None

    return pl.pallas_call(
        body, name="ffn_fwd", grid=(nt,),
        in_specs=[pl.BlockSpec((S, D), lambda n: (0, 0)),
                  pl.BlockSpec((tn, D), lambda n: (n, 0)), pl.BlockSpec((tn, D), lambda n: (n + nt, 0)),
                  pl.BlockSpec((3, tn), lambda n: (0, n)), pl.BlockSpec((3, tn), lambda n: (0, n + nt)),
                  pl.BlockSpec((1, tn), lambda n: (0, n)), pl.BlockSpec((1, tn), lambda n: (0, n + nt))],
        out_specs=[pl.BlockSpec((2, S, tn), lambda n: (0, 0, n)), pl.BlockSpec((S, tn), lambda n: (0, n))],
        out_shape=[jax.ShapeDtypeStruct((2, S, cfg.FFP), F32), jax.ShapeDtypeStruct((S, cfg.FFP), BF16)],
        compiler_params=_cp(cfg, ("parallel",)),
    )(h2, w_up, w_up, conv_w, conv_w, conv_b, conv_b)


def ffn_bwd(cfg, df, h2, w_down, u, conv_w, conv_b):
    S, D = df.shape
    tn, nt = cfg.TNF, cfg.FFP // cfg.TNF

    ch, nch = _chunk_rows(cfg)

    def body(df_ref, h_ref, wd_ref, u_ref, cwg_ref, cwv_ref, cbg_ref, cbv_ref,
             du_ref, dwd_ref, dwu_ref, dcw_ref, dcb_ref):
        cws = (cwg_ref[...], cwv_ref[...])
        cbs = (cbg_ref[...], cbv_ref[...])
        zero = jnp.zeros((SUB, tn), F32)
        nxt = [zero, zero]
        dws = [[jnp.zeros((1, tn), F32)] * 4 for _ in range(2)]
        dwd = jnp.zeros((tn, D), F32)
        dwu = [jnp.zeros((tn, D), F32)] * 2
        order = list(reversed(range(nch)))
        dys, done = {}, {}
        for step in range(nch + 2):
            if step < nch:
                ci = order[step]
                dys[ci] = _dot_nt(df_ref[pl.ds(ci * ch, ch), :], wd_ref[...])
            if 1 <= step <= nch:
                ci = order[step - 1]
                rows = pl.ds(ci * ch, ch)
                dy = dys.pop(ci)
                us, s1, s2, cs = [], [], [], []
                for i in range(2):
                    u = u_ref[i, rows, :]
                    prev = u_ref[i, pl.ds(ci * ch - SUB, SUB), :] if ci else zero
                    us.append(u)
                    s1.append(_shift_down(u, prev, 1))
                    s2.append(_shift_down(u, prev, 2))
                    cs.append(_conv(u, s1[i], s2[i], cws[i], cbs[i]))
                gl, t = _gelu(cs[0])
                dcs = (dy * cs[1] * _gelu_grad(cs[0], t), dy * gl)
                dus = []
                for i, dc in enumerate(dcs):
                    du = dc * cws[i][2:3, :] + _shift_up(dc, nxt[i], 1) * cws[i][1:2, :] + _shift_up(dc, nxt[i], 2) * cws[i][0:1, :]
                    dus.append(du.astype(BF16))
                    du_ref[i, rows, :] = dus[i]
                    for j, tap in enumerate((s2[i], s1[i], us[i])):
                        dws[i][j] = dws[i][j] + jnp.sum(dc * tap, axis=0, keepdims=True)
                    dws[i][3] = dws[i][3] + jnp.sum(dc, axis=0, keepdims=True)
                nxt = [dc[:SUB] for dc in dcs]
                done[ci] = ((gl * cs[1]).astype(BF16), dus)
            if step >= 2:
                ci = order[step - 2]
                rows = pl.ds(ci * ch, ch)
                yv, dus = done.pop(ci)
                dwd = dwd + _dot_tn(yv, df_ref[rows, :])
                hv = h_ref[rows, :]
                dwu = [acc + _dot_tn(du, hv) for acc, du in zip(dwu, dus)]
        dwd_ref[...] = dwd.astype(BF16)
        for i in range(2):
            dwu_ref[i] = dwu[i].astype(BF16)
            for j in range(3):
                dcw_ref[i, j:j + 1, :] = dws[i][j]
            dcb_ref[i] = dws[i][3]

    whole = pl.BlockSpec((S, D), lambda n: (0, 0), pipeline_mode=pl.Buffered(1))
    du, dwd, dwu, dcw, dcb = pl.pallas_call(
        body, name="ffn_bwd", grid=(nt,),
        in_specs=[whole, whole, pl.BlockSpec((tn, D), lambda n: (n, 0)),
                  pl.BlockSpec((2, S, tn), lambda n: (0, 0, n)),
                  pl.BlockSpec((3, tn), lambda n: (0, n)), pl.BlockSpec((3, tn), lambda n: (0, n + nt)),
                  pl.BlockSpec((1, tn), lambda n: (0, n)), pl.BlockSpec((1, tn), lambda n: (0, n + nt))],
        out_specs=[pl.BlockSpec((2, S, tn), lambda n: (0, 0, n)), pl.BlockSpec((tn, D), lambda n: (n, 0)),
                   pl.BlockSpec((2, tn, D), lambda n: (0, n, 0)),
                   pl.BlockSpec((2, 3, tn), lambda n: (0, 0, n)), pl.BlockSpec((2, 1, tn), lambda n: (0, 0, n))],
        out_shape=[jax.ShapeDtypeStruct((2, S, cfg.FFP), BF16), jax.ShapeDtypeStruct((cfg.FFP, D), BF16),
                   jax.ShapeDtypeStruct((2, cfg.FFP, D), BF16),
                   jax.ShapeDtypeStruct((2, 3, cfg.FFP), F32), jax.ShapeDtypeStruct((2, 1, cfg.FFP), F32)],
        compiler_params=_cp(cfg, ("parallel",)),
    )(df, h2, w_down, u, conv_w, conv_w, conv_b, conv_b)
    return du, dwd, dwu.reshape(cfg.FF2P, D), dcw, dcb


def rope_tables(cfg):
    inv_freq = ROPE_THETA ** (-jnp.arange(0, cfg.DH, 2, dtype=F32) / cfg.DH)
    ang = jnp.arange(cfg.S, dtype=F32)[:, None] * inv_freq[None, :]
    cos, sin = jnp.cos(ang), jnp.sin(ang)
    return jnp.concatenate([cos, cos], axis=1), jnp.concatenate([-sin, sin], axis=1)


class LocalWeights:
    def __init__(self, w_in, w_out, w_up, conv_w, w_down):
        self.w = (w_in, w_out, w_up, conv_w, w_down)
        self.grads = {}

    def first_start(self):
        return None

    def weights_first(self, after):
        return self.w[0], self.w[3]

    def start_rest(self, after):
        return None

    def weights_rest(self, group, after):
        return ((self.w[1],), None) if group == 0 else ((self.w[4],), None)

    def forwarded(self, group, after):
        return (self.w[2],)

    def pair_start(self, grads):
        self.grads.update(grads)
        return None

    def reduce_start(self, grads, after=None):
        self.grads.update(grads)
        return None

    def reduce_wait(self, names, after):
        pass


def _after(a, token):
    return a if token is None else a + token[0, 0].astype(a.dtype)


def local_step(cfg, comm, x, target, g1, g2, g3, g4, g_sb, g_dl, conv_b):
    S, D = cfg.S, cfg.D
    cos2, sin2 = rope_tables(cfg)
    full = lambda r, c: pl.BlockSpec((r, c), lambda j, k: (0, 0))

    h1 = rms_fwd(cfg, x, _after(g1, comm.first_start()))
    w_in, conv_w = comm.weights_first(after=h1)
    qkv3 = qkv_proj(cfg, h1, w_in, _after(cos2, comm.start_rest(after=w_in)), sin2)
    o_sb, tsum = sb_fwd(cfg, qkv3)
    o_dl, lse_tot = dil_fwd(cfg, qkv3)
    mixed = combine_fwd(cfg, o_sb, o_dl, g_sb, g_dl)
    (w_out,), token = comm.weights_rest(0, after=mixed)
    tn = cfg.TN
    mo = _mm(cfg, "mix_out", mixed, w_out, nt=False, grid=(D // tn, 1),
             a_spec=full(S, cfg.DMIX), b_spec=pl.BlockSpec((cfg.DMIX, tn), lambda j, k: (0, j)),
             o_spec=pl.BlockSpec((S, tn), lambda j, k: (0, j)),
             out_shape=jax.ShapeDtypeStruct((S, D), F32), acc_shape=(8, LANE), dep=token)
    x1, h2 = mid_fwd(cfg, x, mo, g2, g3)
    w_up, = comm.forwarded(0, after=h2)
    u, y = ffn_fwd(cfg, h2, w_up, conv_w, conv_b)
    (w_down,), _ = comm.weights_rest(1, after=y)
    tk = cfg.FFP // 2
    f = _mm(cfg, "ffn_down", y, w_down, nt=False, grid=(D // tn, cfg.FFP // tk),
            a_spec=pl.BlockSpec((S, tk), lambda j, k: (0, k)), b_spec=pl.BlockSpec((tk, tn), lambda j, k: (k, j)),
            o_spec=pl.BlockSpec((S, tn), lambda j, k: (0, j)),
            out_shape=jax.ShapeDtypeStruct((S, D), F32), acc_shape=(S, tn))
    dout, df, dg4, loss = final_fwd_bwd(cfg, x1, f, g4, target)

    du, dw_down, dw_up, dconv_w, dconv_b = ffn_bwd(cfg, df, h2, w_down, u, conv_w, conv_b)
    kt = cfg.FFP // tk
    dh2 = _mm(cfg, "d_h2", du, w_up, nt=False, grid=(D // tn, 2 * kt),
              a_spec=pl.BlockSpec((None, S, tk), lambda j, k: (k // kt, 0, k % kt)),
              b_spec=pl.BlockSpec((tk, tn), lambda j, k: (k, j)),
              o_spec=pl.BlockSpec((S, tn), lambda j, k: (0, j)),
              out_shape=jax.ShapeDtypeStruct((S, D), F32), acc_shape=(S, tn),
              dep=comm.pair_start(dict(w_down=dw_down, w_up=dw_up)))
    token = comm.reduce_start({}, after=dh2)
    dx1, dmo, dg3, dg2 = mid_bwd(cfg, dh2, x1, _after(g3, token), dout, mo, g2)

    dmix = _mm(cfg, "d_mixed", dmo, w_out, nt=True, grid=(cfg.DMIX // tn, 1),
               a_spec=full(S, D), b_spec=pl.BlockSpec((tn, D), lambda j, k: (j, 0)),
               o_spec=pl.BlockSpec((S, tn), lambda j, k: (0, j)),
               out_shape=jax.ShapeDtypeStruct((S, cfg.DMIX), F32), acc_shape=(8, LANE))
    dw_out = _mm_tn(cfg, "d_w_out", mixed, dmo, grid=(D // tn,),
                    a_spec=pl.BlockSpec((S, cfg.DMIX), lambda j: (0, 0)),
                    b_spec=pl.BlockSpec((S, tn), lambda j: (0, j)),
                    o_spec=pl.BlockSpec((cfg.DMIX, tn), lambda j: (0, j)),
                    out_shape=jax.ShapeDtypeStruct((cfg.DMIX, D), BF16))
    token = comm.reduce_start(dict(w_out=dw_out))
    do_sb, do_dl, delta, dg_sb, dg_dl = combine_bwd(cfg, dmix, o_sb, o_dl, _after(g_sb, token), g_dl)
    d_sb3 = sb_bwd(cfg, qkv3, do_sb, tsum)
    dqkv3 = dil_bwd(cfg, qkv3, do_dl, delta, lse_tot, cos2, sin2, d_sb3)
    comm.reduce_wait(("w_up", "w_down"), after=dqkv3)
    comm.reduce_wait(("w_out",), after=dqkv3)
    tkq = min(tn, cfg.DSB)
    kq = cfg.DSB // tkq
    dw_in = _mm_tn(cfg, "d_w_in", h1, dqkv3, grid=(6 * kq,),
                   a_spec=pl.BlockSpec((S, D), lambda j: (0, 0)),
                   b_spec=pl.BlockSpec((None, S, tkq), lambda j: (j // kq, 0, j % kq)),
                   o_spec=pl.BlockSpec((D, tkq), lambda j: (0, j)),
                   out_shape=jax.ShapeDtypeStruct((D, 6 * cfg.DSB), BF16))
    token = comm.reduce_start(dict(w_in=dw_in))
    dh1 = _mm(cfg, "d_h1", dqkv3, w_in, nt=True, grid=(D // tn, 6),
              a_spec=pl.BlockSpec((None, S, cfg.DSB), lambda j, k: (k, 0, 0)),
              b_spec=pl.BlockSpec((tn, cfg.DSB), lambda j, k: (j, k)),
              o_spec=pl.BlockSpec((S, tn), lambda j, k: (0, j)),
              out_shape=jax.ShapeDtypeStruct((S, D), F32), acc_shape=(S, tn), dep=token)
    grad_x, dg1 = first_bwd(cfg, dh1, x, g1, dx1)
    small = dict(loss=loss, g1=dg1, g2=dg2, g3=dg3, g4=dg4, g_sb=dg_sb, g_dl=dg_dl,
                 conv_b=dconv_b.reshape(1, cfg.FF2P), conv_w=dconv_w.transpose(1, 0, 2).reshape(3, cfg.FF2P))
    return grad_x, small


ANY = pl.BlockSpec(memory_space=pl.ANY)


def _me():
    return lax.axis_index("x"), lax.axis_index("y"), lax.axis_index("c")


def _other_chips(x, y):
    return [(1 - x, y), (x, 1 - y), (1 - x, 1 - y)]


def pad_conv_w(cfg, conv_w, pos):
    r, c = conv_w.shape

    def body(pos_ref, w_ref, full_ref, scr, sem):
        scr[:, :c] = w_ref[...]
        scr[:, c:] = jnp.zeros((r, cfg.FSHP - c), F32)
        cols = pl.ds(pl.multiple_of(pos_ref[0] * cfg.FSHP, LANE), cfg.FSHP)
        cp = pltpu.make_async_copy(scr, full_ref.at[:, cols], sem)
        cp.start()
        cp.wait()

    return pl.pallas_call(
        body, name="pad_conv_w",
        grid_spec=pltpu.PrefetchScalarGridSpec(
            num_scalar_prefetch=1, grid=(1,), in_specs=[pl.BlockSpec((r, c), lambda i, p: (0, 0))], out_specs=ANY,
            scratch_shapes=[pltpu.VMEM((r, cfg.FSHP), F32), pltpu.SemaphoreType.DMA]),
        out_shape=jax.ShapeDtypeStruct(_full_shape(cfg, "conv_w"), F32),
    )(pos, conv_w)


def _tile2(r, c):
    return (256, c) if r % 256 == 0 else (r, 512 if c % 512 == 0 else c)


def cast_into(cfg, name, w, pos, dep=None):
    r, c = w.shape
    _, nr, _, nc = _slab(cfg, name, 0)
    tm, tc = _tile2(r, c)
    wr = nr if tm == r else tm
    assert nc == c and (nr == r or tm == r)
    gap = cfg.FSHP - cfg.FSH if name == "w_down" else 0
    deps = [] if dep is None else [dep]

    def body(pos_ref, w_ref, *rest):
        full_ref, token, scr, sem = rest[len(deps):]
        token[...] = jnp.zeros_like(token)
        tile = w_ref[...]
        if deps:
            tile = tile + rest[0][0:1, 0:1]
        scr[pl.ds(0, tm), :] = tile.astype(BF16)
        if wr > tm:
            scr[pl.ds(tm, wr - tm), :] = jnp.zeros((wr - tm, tc), BF16)
        r0, _, c0, _ = _slab(cfg, name, pos_ref[0])
        rows = pl.ds(pl.multiple_of(r0 + pl.program_id(0) * tm, 16), wr)
        cols = pl.ds(pl.multiple_of(c0 + pl.program_id(1) * tc, LANE), tc)
        cps = [pltpu.make_async_copy(scr.at[pl.ds(0, wr), :], full_ref.at[rows, cols], sem.at[0])]
        if gap:
            scr[pl.ds(wr, gap), :] = jnp.zeros((gap, tc), BF16)
            for h in range(2):
                pad_rows = pl.ds(h * cfg.FSHP + cfg.FSH, gap)
                cps.append(pltpu.make_async_copy(scr.at[pl.ds(wr, gap), :], full_ref.at[pad_rows, cols], sem.at[1 + h]))
        for cp in cps:
            cp.start()
        for cp in cps:
            cp.wait()

    return pl.pallas_call(
        body, name=f"cast_{name}",
        grid_spec=pltpu.PrefetchScalarGridSpec(
            num_scalar_prefetch=1, grid=(r // tm, c // tc),
            in_specs=[pl.BlockSpec((tm, tc), lambda i, j, p: (i, j))]
            + [pl.BlockSpec((8, LANE), lambda i, j, p: (0, 0))] * len(deps),
            out_specs=[ANY, pl.BlockSpec((8, LANE), lambda i, j, p: (0, 0))],
            scratch_shapes=[pltpu.VMEM((wr + gap, tc), BF16), pltpu.SemaphoreType.DMA((3,))]),
        out_shape=[jax.ShapeDtypeStruct(_full_shape(cfg, name), BF16), jax.ShapeDtypeStruct((8, LANE), F32)],
        compiler_params=_cp(cfg, ("arbitrary", "arbitrary")),
    )(pos, w, *deps)


HBM = pl.BlockSpec(memory_space=pltpu.HBM)
SEM = pl.BlockSpec(memory_space=pltpu.SEMAPHORE)
TOKEN = pl.BlockSpec(memory_space=pltpu.VMEM)
EFFECT = pltpu.SideEffectType.DATAFLOW_SIDE_EFFECTING


def _slab(cfg, name, k):
    D = cfg.D
    if name == "w_in":
        cin = 6 * cfg.DSB // N_CHIPS
        return 0, D, k * cin, cin
    if name == "w_out":
        rout = cfg.DMIX // N_CHIPS
        return k * rout, rout, 0, D
    if name == "w_up":
        return k * cfg.FSHP, cfg.FSHP, 0, D
    if name == "conv_w":
        return 0, 3, k * cfg.FSHP, cfg.FSHP
    rdn = cfg.FSH // 2
    return (k // 2) * cfg.FSHP + (k % 2) * rdn, rdn, 0, D


def _full_shape(cfg, name):
    return dict(w_in=(cfg.D, 6 * cfg.DSB), w_out=(cfg.DMIX, cfg.D), w_up=(cfg.FF2P, cfg.D), w_down=(cfg.FFP, cfg.D),
                conv_w=(3, cfg.FF2P))[name]


def _half(cfg, name, ref, k, h):
    r0, nr, c0, nc = _slab(cfg, name, k)
    if name == "conv_w":
        return ref.at[:, pl.ds(c0, nc)]
    return ref.at[pl.ds(r0 + h * (nr // 2), nr // 2), pl.ds(c0, nc)]


def _rows_half(ref, h):
    nr = ref.shape[0] // 2
    return ref.at[pl.ds(h * nr, nr), :]


def _remote(src, dst, send_sem, recv_sem, dev):
    return pltpu.make_async_remote_copy(src_ref=src, dst_ref=dst, send_sem=send_sem, recv_sem=recv_sem,
                                        device_id=dev, device_id_type=MESH)


REST = ("w_out", "w_up", "w_down")
FIRST = (("w_in", "conv_w"),)
GROUPS = (("w_out", "w_up"), ("w_down",))


def _hbm(a):
    return pltpu.with_memory_space_constraint(a, pltpu.HBM)


def gather_start(cfg, tag, groups, fulls, after):
    order = [k for names in groups for k in names]
    n, ng = len(order), len(groups)

    def body(*refs):
        lands = dict(zip(order, refs[:n]))
        sems = refs[n + 1:n + 1 + 2 * ng]
        token = refs[-1]
        x, y, c = _me()
        me = 2 * x + y
        for g, names in enumerate(groups):
            for i, name in enumerate(names):
                mine = _half(cfg, name, lands[name], me, c)
                for j, (px, py) in enumerate(_other_chips(x, y)):
                    _remote(mine, mine, sems[2 * g].at[3 * i + j], sems[2 * g + 1].at[3 * i + j], (px, py, c)).start()
        token[...] = jnp.zeros_like(token)

    ops = [_hbm(fulls[k]) for k in order]
    sem_shapes = [pltpu.SemaphoreType.DMA((3 * len(names),)) for names in groups for _ in range(2)]
    outs = pl.pallas_call(
        body, name=f"gather_start_{tag}",
        in_specs=[HBM] * n + [ANY],
        out_specs=[SEM] * (2 * ng) + [HBM] * n + [TOKEN],
        out_shape=sem_shapes + [pltpu.HBM(a.shape, a.dtype) for a in ops] + [jax.ShapeDtypeStruct((8, LANE), F32)],
        input_output_aliases={i: 2 * ng + i for i in range(n)},
        compiler_params=pltpu.CompilerParams(has_side_effects=EFFECT),
    )(*ops, after)
    thru = dict(zip(order, outs[2 * ng:2 * ng + n]))
    return [(outs[2 * g], outs[2 * g + 1], [thru[k] for k in names]) for g, names in enumerate(groups)], outs[-1]


def gather_wait(cfg, names, ssem, rsem, lands, after):
    n = len(names)

    def body(*refs):
        lands_ = refs[:n]
        ssem_, rsem_ = refs[n], refs[n + 1]
        x, y, c = _me()
        me = 2 * x + y
        for i, name in enumerate(names):
            for j, (px, py) in enumerate(_other_chips(x, y)):
                cp = _remote(_half(cfg, name, lands_[i], me, c), _half(cfg, name, lands_[i], 2 * px + py, c),
                             ssem_.at[3 * i + j], rsem_.at[3 * i + j], (px, py, c))
                cp.wait_send()
                cp.wait_recv()

    return pl.pallas_call(
        body, name="gather_wait_" + "_".join(names),
        in_specs=[HBM] * n + [SEM, SEM, ANY], out_specs=[HBM] * n,
        out_shape=[pltpu.HBM(a.shape, a.dtype) for a in lands],
        input_output_aliases={i: i for i in range(n)},
        compiler_params=pltpu.CompilerParams(has_side_effects=EFFECT),
    )(*lands, ssem, rsem, after)


def gather_finish(cfg, names, lands):
    n = len(names)

    def body(*refs):
        outs = refs[n:2 * n]
        ssem, rsem = refs[2 * n:]
        x, y, c = _me()
        sib = (x, y, 1 - c)
        fwds = []
        for i, name in enumerate(names):
            for j, (px, py) in enumerate(_other_chips(x, y)):
                landed = _half(cfg, name, outs[i], 2 * px + py, c)
                fwds.append(_remote(landed, landed, ssem.at[3 * i + j], rsem.at[3 * i + j], sib))
        for cp in fwds:
            cp.start()
        for i, name in enumerate(names):
            for j, (px, py) in enumerate(_other_chips(x, y)):
                passed = _half(cfg, name, outs[i], 2 * px + py, 1 - c)
                _remote(passed, passed, ssem.at[3 * i + j], rsem.at[3 * i + j], sib).wait_recv()
        for cp in fwds:
            cp.wait_send()

    return pl.pallas_call(
        body, name="gather_finish_" + "_".join(names), in_specs=[ANY] * n, out_specs=[ANY] * n,
        out_shape=[jax.ShapeDtypeStruct(a.shape, a.dtype) for a in lands],
        input_output_aliases={i: i for i in range(n)},
        scratch_shapes=[pltpu.SemaphoreType.DMA((3 * n,)), pltpu.SemaphoreType.DMA((3 * n,))],
    )(*lands)


def forward_start(cfg, names, lands, after):
    n = len(names)

    def body(*refs):
        outs = refs[:n]
        ssem, rsem = refs[n + 1], refs[n + 2]
        token = refs[-1]
        x, y, c = _me()
        for i, name in enumerate(names):
            for j, (px, py) in enumerate(_other_chips(x, y)):
                landed = _half(cfg, name, outs[i], 2 * px + py, c)
                _remote(landed, landed, ssem.at[3 * i + j], rsem.at[3 * i + j], (x, y, 1 - c)).start()
        token[...] = jnp.zeros_like(token)

    ops = [_hbm(a) for a in lands]
    outs = pl.pallas_call(
        body, name="forward_start_" + "_".join(names),
        in_specs=[HBM] * n + [ANY], out_specs=[SEM, SEM] + [HBM] * n + [TOKEN],
        out_shape=[pltpu.SemaphoreType.DMA((3 * n,)), pltpu.SemaphoreType.DMA((3 * n,))]
        + [pltpu.HBM(a.shape, a.dtype) for a in ops] + [jax.ShapeDtypeStruct((8, LANE), F32)],
        input_output_aliases={i: 2 + i for i in range(n)},
        compiler_params=pltpu.CompilerParams(has_side_effects=EFFECT),
    )(*ops, after)
    return outs[0], outs[1], outs[2:2 + n], outs[-1]


def forward_wait(cfg, names, ssem, rsem, lands, after):
    n = len(names)

    def body(*refs):
        outs = refs[:n]
        ssem_, rsem_ = refs[n], refs[n + 1]
        x, y, c = _me()
        for i, name in enumerate(names):
            for j, (px, py) in enumerate(_other_chips(x, y)):
                cp = _remote(_half(cfg, name, outs[i], 2 * px + py, c), _half(cfg, name, outs[i], 2 * px + py, 1 - c),
                             ssem_.at[3 * i + j], rsem_.at[3 * i + j], (x, y, 1 - c))
                cp.wait_send()
                cp.wait_recv()

    return pl.pallas_call(
        body, name="forward_wait_" + "_".join(names),
        in_specs=[HBM] * n + [SEM, SEM, ANY], out_specs=[HBM] * n,
        out_shape=[pltpu.HBM(a.shape, a.dtype) for a in lands],
        input_output_aliases={i: i for i in range(n)},
        compiler_params=pltpu.CompilerParams(has_side_effects=EFFECT),
    )(*lands, ssem, rsem, after)


def pair_send(cfg, grads):
    names = list(grads)
    n = len(names)

    def half_shape(name):
        _, nr, _, nc = _slab(cfg, name, 0)
        return (N_CHIPS, nr // 2, nc)

    def body(*refs):
        srcs, theirs = refs[:n], refs[n:2 * n]
        ssem, rsem = refs[2 * n:]
        x, y, c = _me()
        cps = []
        for i, name in enumerate(names):
            for k in range(N_CHIPS):
                cps.append(_remote(_half(cfg, name, srcs[i], k, 1 - c), theirs[i].at[k],
                                   ssem.at[N_CHIPS * i + k], rsem.at[N_CHIPS * i + k], (x, y, 1 - c)))
        for cp in cps:
            cp.start()
        for cp in cps:
            cp.wait()

    outs = pl.pallas_call(
        body, name="pair_send_" + "_".join(names), in_specs=[ANY] * n, out_specs=[ANY] * n,
        out_shape=[jax.ShapeDtypeStruct(half_shape(name), BF16) for name in names],
        scratch_shapes=[pltpu.SemaphoreType.DMA((N_CHIPS * n,))] * 2,
    )(*[grads[k] for k in names])
    return dict(zip(names, outs))


def pair_start(cfg, grads, after):
    names = list(grads)
    n = len(names)

    def body(*refs):
        srcs, theirs = refs[:n], refs[n:2 * n]
        ssem, rsem = refs[2 * n + 1], refs[2 * n + 2]
        token = refs[-1]
        x, y, c = _me()
        for i, name in enumerate(names):
            for k in range(N_CHIPS):
                _remote(_half(cfg, name, srcs[i], k, 1 - c), theirs[i].at[k],
                        ssem.at[N_CHIPS * i + k], rsem.at[N_CHIPS * i + k], (x, y, 1 - c)).start()
        token[...] = jnp.zeros_like(token)

    def half_shape(name):
        _, nr, _, nc = _slab(cfg, name, 0)
        return (N_CHIPS, nr // 2, nc)

    ops = [_hbm(grads[k]) for k in names] + [_hbm(lax.empty(half_shape(k), BF16)) for k in names]
    outs = pl.pallas_call(
        body, name="pair_start_" + "_".join(names),
        in_specs=[HBM] * (2 * n) + [ANY],
        out_specs=[SEM, SEM] + [HBM] * (2 * n) + [TOKEN],
        out_shape=[pltpu.SemaphoreType.DMA((N_CHIPS * n,)), pltpu.SemaphoreType.DMA((N_CHIPS * n,))]
        + [pltpu.HBM(a.shape, a.dtype) for a in ops] + [jax.ShapeDtypeStruct((8, LANE), F32)],
        input_output_aliases={i: 2 + i for i in range(2 * n)},
        compiler_params=pltpu.CompilerParams(has_side_effects=EFFECT),
    )(*ops, after)
    return outs[0], outs[1], dict(zip(names, outs[2:2 + n])), dict(zip(names, outs[2 + n:2 + 2 * n])), outs[-1]


def pair_wait(cfg, ssem, rsem, grads, theirs, after):
    names = list(grads)
    n = len(names)

    def body(*refs):
        srcs, theirs_ = refs[:n], refs[n:2 * n]
        ssem_, rsem_ = refs[2 * n], refs[2 * n + 1]
        x, y, c = _me()
        for i, name in enumerate(names):
            for k in range(N_CHIPS):
                cp = _remote(_half(cfg, name, srcs[i], k, 1 - c), theirs_[i].at[k],
                             ssem_.at[N_CHIPS * i + k], rsem_.at[N_CHIPS * i + k], (x, y, 1 - c))
                cp.wait_send()
                cp.wait_recv()

    ops = [grads[k] for k in names] + [theirs[k] for k in names]
    outs = pl.pallas_call(
        body, name="pair_wait_" + "_".join(names),
        in_specs=[HBM] * (2 * n) + [SEM, SEM, ANY], out_specs=[HBM] * (2 * n),
        out_shape=[pltpu.HBM(a.shape, a.dtype) for a in ops],
        input_output_aliases={i: i for i in range(2 * n)},
        compiler_params=pltpu.CompilerParams(has_side_effects=EFFECT),
    )(*ops, ssem, rsem, after)
    return dict(zip(names, outs[:n])), dict(zip(names, outs[n:]))


def pair_sum(cfg, name, grad, theirs, pos):
    _, r, c = theirs.shape
    tm, tc = _tile2(r, c)

    ni, nj = r // tm, c // tc
    total = N_CHIPS * ni * nj

    def body(pos_ref, g_ref, t_ref, o_ref, scr, sem):
        step = (pl.program_id(0) * ni + pl.program_id(1)) * nj + pl.program_id(2)

        def fetch(flat, slot):
            k, rem = flat // (ni * nj), flat % (ni * nj)
            r0, nr, c0, _ = _slab(cfg, name, k)
            rows = pl.ds(pl.multiple_of(r0 + pos_ref[1] * (nr // 2) + (rem // nj) * tm, 16), tm)
            cols = pl.ds(pl.multiple_of(c0 + (rem % nj) * tc, LANE), tc)
            return pltpu.make_async_copy(g_ref.at[rows, cols], scr.at[slot], sem.at[slot])

        @pl.when(step == 0)
        def _():
            fetch(0, 0).start()

        @pl.when(step + 1 < total)
        def _():
            fetch(step + 1, (step + 1) % 2).start()

        fetch(step, step % 2).wait()
        o_ref[...] = (scr[step % 2].astype(F32) + t_ref[...].astype(F32)).astype(BF16)

    blk = pl.BlockSpec((None, tm, tc), lambda k, i, j, p: (k, i, j))
    return pl.pallas_call(
        body, name=f"pair_sum_{name}",
        grid_spec=pltpu.PrefetchScalarGridSpec(
            num_scalar_prefetch=1, grid=(N_CHIPS, ni, nj), in_specs=[ANY, blk], out_specs=blk,
            scratch_shapes=[pltpu.VMEM((2, tm, tc), BF16), pltpu.SemaphoreType.DMA((2,))]),
        out_shape=jax.ShapeDtypeStruct(theirs.shape, BF16),
        compiler_params=_cp(cfg, ("arbitrary",) * 3),
    )(pos, grad, theirs)


def scatter_start(cfg, pres, after):
    names = list(pres)
    n = len(names)

    def body(*refs):
        srcs, lands = refs[:n], refs[n:2 * n]
        ssem, rsem = refs[2 * n + 1], refs[2 * n + 2]
        token = refs[-1]
        x, y, c = _me()
        for i in range(n):
            for j, (px, py) in enumerate(_other_chips(x, y)):
                _remote(srcs[i].at[2 * px + py], lands[i].at[j], ssem.at[3 * i + j], rsem.at[3 * i + j], (px, py, c)).start()
        token[...] = jnp.zeros_like(token)

    lands = [lax.empty((3,) + pres[k].shape[1:], BF16) for k in names]
    ops = [_hbm(a) for a in [pres[k] for k in names] + lands]
    outs = pl.pallas_call(
        body, name="scatter_start_" + "_".join(names),
        in_specs=[HBM] * (2 * n) + [ANY],
        out_specs=[SEM, SEM] + [HBM] * (2 * n) + [TOKEN],
        out_shape=[pltpu.SemaphoreType.DMA((3 * n,)), pltpu.SemaphoreType.DMA((3 * n,))]
        + [pltpu.HBM(a.shape, a.dtype) for a in ops] + [jax.ShapeDtypeStruct((8, LANE), F32)],
        input_output_aliases={i: 2 + i for i in range(2 * n)},
        compiler_params=pltpu.CompilerParams(has_side_effects=EFFECT),
    )(*ops, after)
    return outs[0], outs[1], dict(zip(names, outs[2:2 + n])), dict(zip(names, outs[2 + n:2 + 2 * n])), outs[-1]


def scatter_wait(cfg, ssem, rsem, pres, lands, after):
    names = list(pres)
    n = len(names)

    def body(*refs):
        srcs, lands_ = refs[:n], refs[n:2 * n]
        ssem_, rsem_ = refs[2 * n], refs[2 * n + 1]
        x, y, c = _me()
        for i in range(n):
            for j, (px, py) in enumerate(_other_chips(x, y)):
                cp = _remote(srcs[i].at[2 * px + py], lands_[i].at[j], ssem_.at[3 * i + j], rsem_.at[3 * i + j], (px, py, c))
                cp.wait_send()
                cp.wait_recv()

    ops = [pres[k] for k in names] + [lands[k] for k in names]
    outs = pl.pallas_call(
        body, name="scatter_wait_" + "_".join(names),
        in_specs=[HBM] * (2 * n) + [SEM, SEM, ANY], out_specs=[HBM] * (2 * n),
        out_shape=[pltpu.HBM(a.shape, a.dtype) for a in ops],
        input_output_aliases={i: i for i in range(2 * n)},
        compiler_params=pltpu.CompilerParams(has_side_effects=EFFECT),
    )(*ops, ssem, rsem, after)
    return dict(zip(names, outs[:n])), dict(zip(names, outs[n:]))


def sum_landed(cfg, name, pre, land, pos):
    _, r, c = pre.shape
    tm, tc = _tile2(r, c)
    nrt = r // tm

    def body(pos_ref, p_ref, l_ref, o_ref):
        acc = p_ref[...].astype(F32)
        for j in range(3):
            acc = acc + l_ref[j].astype(F32)
        o_ref[...] = acc

    return pl.pallas_call(
        body, name=f"sum_landed_{name}",
        grid_spec=pltpu.PrefetchScalarGridSpec(
            num_scalar_prefetch=1, grid=(nrt, c // tc),
            in_specs=[pl.BlockSpec((None, tm, tc), lambda i, j, p: (p[0], i, j)),
                      pl.BlockSpec((3, tm, tc), lambda i, j, p: (0, i, j))],
            out_specs=pl.BlockSpec((tm, tc), lambda i, j, p: (p[1] * nrt + i, j))),
        out_shape=jax.ShapeDtypeStruct((2 * r, c), F32), compiler_params=_cp(cfg, ("parallel", "parallel")),
    )(pos, pre, land)


def half_swap(cfg, sums):
    names = list(sums)
    n = len(names)

    def body(*refs):
        outs = refs[n:2 * n]
        ssem, rsem = refs[2 * n:]
        x, y, c = _me()
        cps = [_remote(_rows_half(outs[i], c), _rows_half(outs[i], c), ssem.at[i], rsem.at[i], (x, y, 1 - c))
               for i in range(n)]
        for cp in cps:
            cp.start()
        for i in range(n):
            theirs = _rows_half(outs[i], 1 - c)
            _remote(theirs, theirs, ssem.at[i], rsem.at[i], (x, y, 1 - c)).wait_recv()
        for cp in cps:
            cp.wait_send()

    outs = pl.pallas_call(
        body, name="half_swap_" + "_".join(names), in_specs=[ANY] * n, out_specs=[ANY] * n,
        out_shape=[jax.ShapeDtypeStruct(sums[k].shape, F32) for k in names],
        input_output_aliases={i: i for i in range(n)},
        scratch_shapes=[pltpu.SemaphoreType.DMA((n,))] * 2,
    )(*[sums[k] for k in names])
    return dict(zip(names, outs))


class MeshWeights:
    def __init__(self, cfg, w_sh):
        self.cfg = cfg
        self.pos = jnp.stack([2 * lax.axis_index("x") + lax.axis_index("y"), lax.axis_index("c")]).astype(jnp.int32)
        self.w_sh = w_sh
        self.full = {"w_in": cast_into(cfg, "w_in", w_sh["w_in"], self.pos)[0],
                     "conv_w": pad_conv_w(cfg, w_sh["conv_w"], self.pos)}
        self.inflight = {}
        self.forwards = {}
        self.grads = {}

    def first_start(self):
        cfg = self.cfg
        self.first, token = gather_start(cfg, "first", FIRST, self.full, jnp.zeros((8, LANE), F32))
        for k in REST:
            self.full[k], token = cast_into(cfg, k, self.w_sh[k], self.pos, dep=token)
        return token

    def weights_first(self, after):
        cfg = self.cfg
        ssem, rsem, lands = self.first[0]
        w_in, conv_w = gather_wait(cfg, FIRST[0], ssem, rsem, lands, after)
        return gather_finish(cfg, ("w_in",), [w_in])[0], conv_w

    def start_rest(self, after):
        self.rest, token = gather_start(self.cfg, "rest", GROUPS, self.full, after)
        return token

    def weights_rest(self, group, after):
        cfg = self.cfg
        names = GROUPS[group]
        ssem, rsem, lands = self.rest[group]
        lands = dict(zip(names, gather_wait(cfg, names, ssem, rsem, lands, after)))
        now = [k for k in names if k != "w_up"]
        later = [k for k in names if k == "w_up"]
        ready = gather_finish(cfg, tuple(now), [lands[k] for k in now])
        if not later:
            return tuple(ready), None
        out = forward_start(cfg, tuple(later), [lands[k] for k in later], ready[0])
        self.forwards[group] = (tuple(later),) + tuple(out[:3])
        return tuple(ready), out[3]

    def forwarded(self, group, after):
        names, ssem, rsem, lands = self.forwards.pop(group)
        return tuple(forward_wait(self.cfg, names, ssem, rsem, lands, after))

    def pair_start(self, grads):
        out = pair_start(self.cfg, grads, jnp.zeros((8, LANE), F32))
        self.pairs = out[:4]
        return out[4]

    def reduce_start(self, grads, after=None):
        theirs = pair_send(self.cfg, grads) if grads else {}
        if after is not None:
            early, early_theirs = pair_wait(self.cfg, *self.pairs, after)
            grads, theirs = {**early, **grads}, {**early_theirs, **theirs}
        pres = {k: pair_sum(self.cfg, k, grads[k], theirs[k], self.pos) for k in grads}
        out = scatter_start(self.cfg, pres, jnp.zeros((8, LANE), F32))
        self.inflight[tuple(sorted(grads))] = out[:4]
        return out[4]

    def reduce_wait(self, names, after):
        cfg = self.cfg
        pres, lands = scatter_wait(cfg, *self.inflight.pop(tuple(sorted(names))), after)
        sums = {k: sum_landed(cfg, k, pres[k], lands[k], self.pos) for k in names}
        self.grads.update(half_swap(cfg, sums))


def allreduce_small(cfg, vec):
    R = vec.shape[0]

    def body(v_ref, o_ref, buf, send_sems, recv_sems):
        x, y, c = _me()
        me = 4 * x + 2 * y + c
        buf[me] = v_ref[...]
        sends = []
        for k in range(1, N_DEV):
            px, py, pc = x ^ (k >> 2), y ^ ((k >> 1) & 1), c ^ (k & 1)
            sends.append(pltpu.make_async_remote_copy(
                src_ref=v_ref, dst_ref=buf.at[me], send_sem=send_sems.at[k], recv_sem=recv_sems.at[k],
                device_id=(px, py, pc), device_id_type=MESH))
        for cp in sends:
            cp.start()
        for k in range(1, N_DEV):
            px, py, pc = x ^ (k >> 2), y ^ ((k >> 1) & 1), c ^ (k & 1)
            pltpu.make_async_remote_copy(
                src_ref=v_ref, dst_ref=buf.at[4 * px + 2 * py + pc], send_sem=send_sems.at[k],
                recv_sem=recv_sems.at[k], device_id=(px, py, pc), device_id_type=MESH).wait_recv()
        for cp in sends:
            cp.wait_send()
        acc = buf[0]
        for j in range(1, N_DEV):
            acc = acc + buf[j]
        o_ref[...] = acc

    return pl.pallas_call(
        body, name="allreduce_small",
        in_specs=[pl.BlockSpec(memory_space=pltpu.VMEM)], out_specs=pl.BlockSpec(memory_space=pltpu.VMEM),
        out_shape=jax.ShapeDtypeStruct((R, LANE), F32),
        scratch_shapes=[pltpu.VMEM((N_DEV, R, LANE), F32), pltpu.SemaphoreType.DMA((N_DEV,)),
                        pltpu.SemaphoreType.DMA((N_DEV,))],
    )(vec)


def adamw(cfg, name, w, m, v, g_parts, tile):
    r, c = w.shape
    tm, tc = tile[0] or r, tile[1] or c
    assert tc == c or all(g.shape[1] == c for g in g_parts)
    n = len(g_parts)
    bc1 = 1.0 - ADAM_B1 ** ADAM_STEP
    bc2 = 1.0 - ADAM_B2 ** ADAM_STEP

    def body(*refs):
        w_ref, m_ref, v_ref = refs[:3]
        g_refs = refs[3:3 + n]
        g_out, d_out, m_out, v_out = refs[3 + n:]
        g = g_refs[0][:, :tc]
        for gr in g_refs[1:]:
            g = g + gr[:, :tc]
        m_new = ADAM_B1 * m_ref[...] + (1.0 - ADAM_B1) * g
        v_new = ADAM_B2 * v_ref[...] + (1.0 - ADAM_B2) * jnp.square(g)
        m_hat = m_new / bc1
        v_hat = v_new / bc2
        g_out[...] = g
        d_out[...] = -ADAM_LR * (m_hat / (jnp.sqrt(v_hat) + ADAM_EPS) + ADAM_WD * w_ref[...])
        m_out[...] = m_new
        v_out[...] = v_new

    blk = pl.BlockSpec((tm, tc), lambda i, j: (i, j))
    return pl.pallas_call(
        body, name=f"adamw_{name}", grid=(r // tm, c // tc),
        in_specs=[blk] * 3 + [pl.BlockSpec((tm, tc if tc < c else g.shape[1]), lambda i, j: (i, j)) for g in g_parts],
        out_specs=[blk] * 4, out_shape=[jax.ShapeDtypeStruct((r, c), F32)] * 4,
        compiler_params=_cp(cfg, ("parallel", "parallel")),
    )(w, m, v, *g_parts)


SMALL_ORDER = ("loss", "g1", "g2", "g3", "g4", "g_sb", "g_dl", "conv_b", "conv_w")


def pack_small(small):
    rows = []
    for k in SMALL_ORDER:
        a = small[k].reshape(-1, LANE)
        rows.append(a)
    flat = jnp.concatenate(rows, axis=0)
    pad = (-flat.shape[0]) % 8
    return jnp.pad(flat, ((0, pad), (0, 0))), [r.shape[0] for r in rows]


def unpack_small(red, small, counts):
    out, at = {}, 0
    for k, n in zip(SMALL_ORDER, counts):
        out[k] = red[at:at + n].reshape(small[k].shape)
        at += n
    return out


def pad_ff(cfg, a):
    r = a.shape[0]
    return jnp.pad(a.reshape(r, N_CHIPS, cfg.FSH), ((0, 0), (0, 0), (0, cfg.FSHP - cfg.FSH))).reshape(r, cfg.FF2P)


def step(cfg, x, target, gains, w_sh, conv_b, m_all, v_all):
    chip = 2 * lax.axis_index("x") + lax.axis_index("y")
    comm = MeshWeights(cfg, w_sh)
    grad_x, small = local_step(cfg, comm, x, target, gains["g1"], gains["g2"], gains["g3"], gains["g4"],
                               gains["g_sb"], gains["g_dl"], pad_ff(cfg, conv_b))

    packed, counts = pack_small(small)
    summed = allreduce_small(cfg, packed)
    comm.reduce_wait(("w_in",), after=summed)
    red = unpack_small(summed, small, counts)

    names = ("w_in", "w_out", "w_up", "w_down")
    up_rows = max(t for t in range(SUB, 513, SUB) if cfg.FSH % t == 0)
    tms = dict(w_in=(cfg.TM, None), w_out=(cfg.TM, None), w_up=(up_rows, None), w_down=(None, cfg.TN // 2))
    res = {}
    for n in names:
        res[n] = adamw(cfg, n, w_sh[n], m_all[n], v_all[n], [comm.grads[n]], tms[n])
    g_cw = lax.dynamic_slice_in_dim(red["conv_w"].reshape(3, N_CHIPS, cfg.FSHP), chip, 1, axis=1)[:, 0, :cfg.FSH]
    res["conv_w"] = adamw(cfg, "conv_w", w_sh["conv_w"], m_all["conv_w"], v_all["conv_w"], [g_cw], (None, None))
    g_cb = red["conv_b"].reshape(1, N_CHIPS, cfg.FSHP)[:, :, :cfg.FSH].reshape(1, N_CHIPS * cfg.FSH)
    res["conv_b"] = adamw(cfg, "conv_b", conv_b, m_all["conv_b"], v_all["conv_b"], [g_cb], (None, None))
    for k in ("g1", "g2", "g3", "g4", "g_sb", "g_dl"):
        res[k] = adamw(cfg, k, gains[k], m_all[k], v_all[k], [red[k]], (None, None))
    return red["loss"][0, 0], grad_x, res


PARAMS = ("pre_mix_gain", "post_mix_gain", "pre_ffn_gain", "post_ffn_gain", "w_in", "sb_out_gain", "dil_out_gain",
          "w_out", "w_up", "conv_w", "conv_b", "w_down")
SHORT = dict(pre_mix_gain="g1", post_mix_gain="g2", pre_ffn_gain="g3", post_ffn_gain="g4", sb_out_gain="g_sb",
             dil_out_gain="g_dl", w_in="w_in", w_out="w_out", w_up="w_up", conv_w="conv_w", conv_b="conv_b",
             w_down="w_down")


def kernel(x, pre_mix_gain, post_mix_gain, pre_ffn_gain, post_ffn_gain, w_in, sb_out_gain, dil_out_gain, w_out, w_up, conv_w, conv_b, w_down, loss_target, m_pre_mix_gain, m_post_mix_gain, m_pre_ffn_gain, m_post_ffn_gain, m_w_in, m_sb_out_gain, m_dil_out_gain, m_w_out, m_w_up, m_conv_w, m_conv_b, m_w_down, v_pre_mix_gain, v_post_mix_gain, v_pre_ffn_gain, v_post_ffn_gain, v_w_in, v_sb_out_gain, v_dil_out_gain, v_w_out, v_w_up, v_conv_w, v_conv_b, v_w_down):
    cfg = CFG
    w = dict(zip(PARAMS, (pre_mix_gain, post_mix_gain, pre_ffn_gain, post_ffn_gain, w_in, sb_out_gain, dil_out_gain,
                          w_out, w_up, conv_w, conv_b, w_down)))
    m = dict(zip(PARAMS, (m_pre_mix_gain, m_post_mix_gain, m_pre_ffn_gain, m_post_ffn_gain, m_w_in, m_sb_out_gain,
                          m_dil_out_gain, m_w_out, m_w_up, m_conv_w, m_conv_b, m_w_down)))
    v = dict(zip(PARAMS, (v_pre_mix_gain, v_post_mix_gain, v_pre_ffn_gain, v_post_ffn_gain, v_w_in, v_sb_out_gain,
                          v_dil_out_gain, v_w_out, v_w_up, v_conv_w, v_conv_b, v_w_down)))
    sq = lambda a: a.reshape(a.shape[1:])
    ws = {SHORT[k]: sq(a) if a.ndim == 3 else a for k, a in w.items()}
    ms = {SHORT[k]: sq(a) if a.ndim == 3 else a for k, a in m.items()}
    vs = {SHORT[k]: sq(a) if a.ndim == 3 else a for k, a in v.items()}
    for d in (ws, ms, vs):
        d["w_up"] = d["w_up"].T
    gains = {k: ws[k] for k in ("g1", "g2", "g3", "g4", "g_sb", "g_dl")}
    w_sh = {k: ws[k] for k in ("w_in", "w_out", "w_up", "conv_w", "w_down")}
    loss, grad_x, res = step(cfg, sq(x), sq(loss_target), gains, w_sh, ws["conv_b"], ms, vs)
    res["w_up"] = [a.T for a in res["w_up"]]
    outs = [loss, grad_x.reshape(x.shape)]
    for i in range(4):
        for k in PARAMS:
            outs.append(res[SHORT[k]][i].reshape(w[k].shape))
    return tuple(outs)
```

```python
import functools
import math
from typing import NamedTuple

import jax
import jax.numpy as jnp
from jax import lax
from jax.experimental import pallas as pl
from jax.experimental.pallas import tpu as pltpu

F32 = jnp.float32
BF16 = jnp.bfloat16
MESH = pl.DeviceIdType.MESH

ROPE_THETA = 10000.0
RMS_EPS = 1e-6
ADAM_LR = 0.001
ADAM_B1 = 0.9
ADAM_B2 = 0.999
ADAM_EPS = 1e-08
ADAM_WD = 0.01
ADAM_STEP = 10
GELU_C = math.sqrt(2.0 / math.pi)
NEG_BIG = -1e30
LANE = 128
N_CHIPS = 4
N_DEV = 8


class Cfg(NamedTuple):
    S: int = 2048
    D: int = 2048
    DH: int = 128
    HSB: int = 8
    HDL: int = 8
    QB: int = 128
    SBT: int = 256
    SBH: int = 4
    SBHB: int = 4
    branches: tuple = ((128, 1), (512, 4), (2048, 16))
    FSH: int = 2752
    FSHP: int = 2816
    TM: int = 256
    TNF: int = 256
    FCH: int = 512
    TN: int = 512
    VMEM_MB: int = 56

    @property
    def DSB(self):
        return self.HSB * self.DH

    @property
    def DDL(self):
        return self.HDL * self.DH

    @property
    def DMIX(self):
        return self.DSB + self.DDL

    @property
    def FFP(self):
        return 2 * self.FSHP

    @property
    def FF2P(self):
        return 4 * self.FSHP


CFG = Cfg()


def _cp(cfg, sem=None):
    return pltpu.CompilerParams(dimension_semantics=sem, vmem_limit_bytes=cfg.VMEM_MB * 2**20)


def _dot(a, b):
    return jnp.dot(a, b, preferred_element_type=F32)


def _dot_nt(a, b):
    return lax.dot_general(a, b, (((1,), (1,)), ((), ())), preferred_element_type=F32)


def _dot_tn(a, b):
    return lax.dot_general(a, b, (((0,), (0,)), ((), ())), preferred_element_type=F32)


def _dot_split(x, u):
    hi = x.astype(BF16)
    lo = (x - hi.astype(F32)).astype(BF16)
    return _dot(hi, u) + _dot(lo, u)


def _rstd(x):
    return lax.rsqrt(jnp.mean(x * x, axis=-1, keepdims=True) + RMS_EPS)


def _rms_bwd(dy, x, g):
    r = _rstd(x)
    xh = x * r
    dxh = dy * g
    dx = r * (dxh - xh * jnp.mean(dxh * xh, axis=-1, keepdims=True))
    return dx, dy * xh


def _gelu(x):
    t = jnp.tanh(GELU_C * (x + 0.044715 * (x * x * x)))
    return 0.5 * x * (1.0 + t), t


def _gelu_grad(x, t):
    return 0.5 * (1.0 + t) + 0.5 * x * (1.0 - t * t) * (GELU_C * (1.0 + 3 * 0.044715 * (x * x)))


def _row(cfg, w):
    return pl.BlockSpec((cfg.TM, w), lambda i: (i, 0))


def _vec(w):
    return pl.BlockSpec((1, w), lambda i: (0, 0))


def rms_fwd(cfg, x, g):
    S, D = x.shape

    def body(x_ref, g_ref, h_ref):
        xv = x_ref[...]
        h_ref[...] = (xv * _rstd(xv) * g_ref[...]).astype(BF16)

    return pl.pallas_call(
        body, name="rms_fwd", grid=(S // cfg.TM,),
        in_specs=[_row(cfg, D), _vec(D)], out_specs=_row(cfg, D),
        out_shape=jax.ShapeDtypeStruct((S, D), BF16), compiler_params=_cp(cfg, ("parallel",)),
    )(x, g)


def mid_fwd(cfg, x, mo, g_post, g_pre):
    S, D = x.shape

    def body(x_ref, mo_ref, gp_ref, gn_ref, x1_ref, h2_ref):
        mo_v = mo_ref[...]
        x1 = x_ref[...] + mo_v * _rstd(mo_v) * gp_ref[...]
        x1_ref[...] = x1
        h2_ref[...] = (x1 * _rstd(x1) * gn_ref[...]).astype(BF16)

    return pl.pallas_call(
        body, name="mid_fwd", grid=(S // cfg.TM,),
        in_specs=[_row(cfg, D), _row(cfg, D), _vec(D), _vec(D)],
        out_specs=[_row(cfg, D), _row(cfg, D)],
        out_shape=[jax.ShapeDtypeStruct((S, D), F32), jax.ShapeDtypeStruct((S, D), BF16)],
        compiler_params=_cp(cfg, ("parallel",)),
    )(x, mo, g_post, g_pre)


def final_fwd_bwd(cfg, x1, f, g_post, target):
    S, D = x1.shape

    def body(x1_ref, f_ref, g_ref, t_ref, dout_ref, df_ref, dg_ref, loss_ref):
        @pl.when(pl.program_id(0) == 0)
        def _():
            dg_ref[...] = jnp.zeros_like(dg_ref)
            loss_ref[...] = jnp.zeros_like(loss_ref)

        fv = f_ref[...]
        g = g_ref[...]
        out = x1_ref[...] + fv * _rstd(fv) * g
        err = out - t_ref[...]
        loss_ref[...] += 0.5 * jnp.sum(jnp.mean(err * err, axis=-1, keepdims=True), axis=0, keepdims=True)
        dout = err * (1.0 / D)
        dout_ref[...] = dout
        df, dgx = _rms_bwd(dout, fv, g)
        df_ref[...] = df.astype(BF16)
        dg_ref[...] += jnp.sum(dgx, axis=0, keepdims=True)

    return pl.pallas_call(
        body, name="final_fwd_bwd", grid=(S // cfg.TM,),
        in_specs=[_row(cfg, D), _row(cfg, D), _vec(D), _row(cfg, D)],
        out_specs=[_row(cfg, D), _row(cfg, D), _vec(D), _vec(LANE)],
        out_shape=[jax.ShapeDtypeStruct((S, D), F32), jax.ShapeDtypeStruct((S, D), BF16),
                   jax.ShapeDtypeStruct((1, D), F32), jax.ShapeDtypeStruct((1, LANE), F32)],
        compiler_params=_cp(cfg, ("arbitrary",)),
    )(x1, f, g_post, target)


def mid_bwd(cfg, dh2, x1, g_pre, dout, mo, g_post):
    S, D = x1.shape

    def body(dh_ref, x1_ref, gn_ref, do_ref, mo_ref, gp_ref, dx1_ref, dmo_ref, dgn_ref, dgp_ref):
        @pl.when(pl.program_id(0) == 0)
        def _():
            dgn_ref[...] = jnp.zeros_like(dgn_ref)
            dgp_ref[...] = jnp.zeros_like(dgp_ref)

        dx, dgx = _rms_bwd(dh_ref[...], x1_ref[...], gn_ref[...])
        dx1 = do_ref[...] + dx
        dx1_ref[...] = dx1
        dgn_ref[...] += jnp.sum(dgx, axis=0, keepdims=True)
        dmo, dgy = _rms_bwd(dx1, mo_ref[...], gp_ref[...])
        dmo_ref[...] = dmo.astype(BF16)
        dgp_ref[...] += jnp.sum(dgy, axis=0, keepdims=True)

    return pl.pallas_call(
        body, name="mid_bwd", grid=(S // cfg.TM,),
        in_specs=[_row(cfg, D), _row(cfg, D), _vec(D), _row(cfg, D), _row(cfg, D), _vec(D)],
        out_specs=[_row(cfg, D), _row(cfg, D), _vec(D), _vec(D)],
        out_shape=[jax.ShapeDtypeStruct((S, D), F32), jax.ShapeDtypeStruct((S, D), BF16),
                   jax.ShapeDtypeStruct((1, D), F32), jax.ShapeDtypeStruct((1, D), F32)],
        compiler_params=_cp(cfg, ("arbitrary",)),
    )(dh2, x1, g_pre, dout, mo, g_post)


def first_bwd(cfg, dh1, x, g_pre, dx1):
    S, D = x.shape

    def body(dh_ref, x_ref, g_ref, r_ref, dx_ref, dg_ref):
        @pl.when(pl.program_id(0) == 0)
        def _():
            dg_ref[...] = jnp.zeros_like(dg_ref)

        dx, dgx = _rms_bwd(dh_ref[...], x_ref[...], g_ref[...])
        dx_ref[...] = r_ref[...] + dx
        dg_ref[...] += jnp.sum(dgx, axis=0, keepdims=True)

    return pl.pallas_call(
        body, name="first_bwd", grid=(S // cfg.TM,),
        in_specs=[_row(cfg, D), _row(cfg, D), _vec(D), _row(cfg, D)],
        out_specs=[_row(cfg, D), _vec(D)],
        out_shape=[jax.ShapeDtypeStruct((S, D), F32), jax.ShapeDtypeStruct((1, D), F32)],
        compiler_params=_cp(cfg, ("arbitrary",)),
    )(dh1, x, g_pre, dx1)


def _mm(cfg, name, a, b, *, nt, a_spec, b_spec, o_spec, grid, out_shape, acc_shape, dep=None):
    nk = grid[-1]
    dot = _dot_nt if nt else _dot
    deps = [] if dep is None else [dep]

    def body(a_ref, b_ref, *rest):
        o_ref, acc_ref = rest[-2:]
        k = pl.program_id(len(grid) - 1)
        part = dot(a_ref[...], b_ref[...])
        if deps:
            part = part + rest[0][0:1, 0:1]
        if nk == 1:
            o_ref[...] = part.astype(o_ref.dtype)
            return

        @pl.when(k == 0)
        def _():
            acc_ref[...] = part

        @pl.when(k > 0)
        def _():
            acc_ref[...] += part

        @pl.when(k == nk - 1)
        def _():
            o_ref[...] = acc_ref[...].astype(o_ref.dtype)

    sem = ("parallel",) * (len(grid) - 1) + ("arbitrary",)
    dep_specs = [pl.BlockSpec((8, LANE), lambda *_: (0, 0))] * len(deps)
    return pl.pallas_call(
        body, name=name, grid=grid, in_specs=[a_spec, b_spec] + dep_specs, out_specs=o_spec, out_shape=out_shape,
        scratch_shapes=[pltpu.VMEM(acc_shape, F32)], compiler_params=_cp(cfg, sem),
    )(a, b, *deps)


def _mm_tn(cfg, name, a, b, *, a_spec, b_spec, o_spec, grid, out_shape):
    def body(a_ref, b_ref, o_ref):
        o_ref[...] = _dot_tn(a_ref[...], b_ref[...]).astype(o_ref.dtype)

    return pl.pallas_call(
        body, name=name, grid=grid, in_specs=[a_spec, b_spec], out_specs=o_spec, out_shape=out_shape,
        compiler_params=_cp(cfg, ("parallel",) * len(grid)),
    )(a, b)


def qkv_proj(cfg, h1, w_in, cos2, sin2):
    S, D = h1.shape
    tn = 2 * cfg.DH
    per = cfg.DSB // tn
    assert cfg.DSB == cfg.DDL
    nblk = 6 * per

    def body(a_ref, b_ref, c_ref, s_ref, o_ref):
        j = pl.program_id(0)
        acc = _dot(a_ref[...], b_ref[...])
        rope = jnp.logical_and(j >= 3 * per, j < 5 * per)

        @pl.when(rope)
        def _():
            for c in range(tn // cfg.DH):
                xh = acc[:, c * cfg.DH:(c + 1) * cfg.DH]
                o_ref[:, c * cfg.DH:(c + 1) * cfg.DH] = (
                    xh * c_ref[...] + pltpu.roll(xh, cfg.DH // 2, 1) * s_ref[...]).astype(BF16)

        @pl.when(jnp.logical_not(rope))
        def _():
            o_ref[...] = acc.astype(BF16)

    return pl.pallas_call(
        body, name="qkv_proj", grid=(nblk,),
        in_specs=[pl.BlockSpec((S, D), lambda j: (0, 0)), pl.BlockSpec((D, tn), lambda j: (0, j)),
                  pl.BlockSpec((S, cfg.DH), lambda j: (0, 0)), pl.BlockSpec((S, cfg.DH), lambda j: (0, 0))],
        out_specs=pl.BlockSpec((None, S, tn), lambda j: (j // per, 0, j % per)),
        out_shape=jax.ShapeDtypeStruct((6, S, cfg.DSB), BF16),
        compiler_params=_cp(cfg, ("parallel",)),
    )(h1, w_in, cos2, sin2)


def _sb_tile(cfg, q, k, valid):
    z = _dot_nt(q, k) * (cfg.DH ** -0.5)
    lb = jnp.minimum(z, 0.0) - jnp.log1p(jnp.exp(-jnp.abs(z)))
    lk = lb - z
    return lb, (lk if valid is None else jnp.where(valid, lk, 0.0))


def _masked(valid, x):
    return x if valid is None else jnp.where(valid, x, 0.0)


def sb_fwd(cfg, qkv3):
    S, QB, DH, NH = cfg.S, cfg.SBT, cfg.DH, cfg.SBH

    def body(q_ref, k_ref, v_ref, o_ref, t_ref):
        row = lax.broadcasted_iota(jnp.int32, (QB, QB), 0)
        col = lax.broadcasted_iota(jnp.int32, (QB, QB), 1)
        u_after = (row > col).astype(BF16)
        causal = col < row
        heads = [slice(h * DH, (h + 1) * DH) for h in range(NH)]

        def q_loop(qb, _):
            rows = pl.ds(pl.multiple_of(qb * QB, QB), QB)
            qs = [q_ref[rows, hd] for hd in heads]

            def tile(kb, carry, valid):
                krows = pl.ds(pl.multiple_of(kb * QB, QB), QB)
                lbk = [_sb_tile(cfg, q, k_ref[krows, hd], valid) for q, hd in zip(qs, heads)]
                rems = [_dot_split(lk, u_after) for _, lk in lbk]
                aa = [_masked(valid, jnp.exp(lb + rem + c)).astype(BF16) for (lb, _), rem, (_, c) in zip(lbk, rems, carry)]
                return tuple((o_acc + _dot(a, v_ref[krows, hd]), c + jnp.sum(lk, axis=1, keepdims=True))
                             for a, hd, (_, lk), (o_acc, c) in zip(aa, heads, lbk, carry))

            carry = tile(qb, ((jnp.zeros((QB, DH), F32), jnp.zeros((QB, 1), F32)),) * NH, causal)
            carry = lax.fori_loop(0, qb, lambda i, cr: tile(qb - 1 - i, cr, None), carry)
            for hd, (o_acc, c) in zip(heads, carry):
                o_ref[rows, hd] = o_acc
                t_ref[rows, hd] = jnp.broadcast_to(c, (QB, DH))
            return 0

        lax.fori_loop(0, S // QB, q_loop, 0)

    def spec(i):
        return pl.BlockSpec((None, S, NH * DH), lambda h: (i, 0, h))

    return pl.pallas_call(
        body, name="sb_fwd", grid=(cfg.HSB // NH,),
        in_specs=[spec(0), spec(1), spec(2)],
        out_specs=[pl.BlockSpec((S, NH * DH), lambda h: (0, h))] * 2,
        out_shape=[jax.ShapeDtypeStruct((S, cfg.DSB), F32)] * 2,
        compiler_params=_cp(cfg, ("parallel",)),
    )(qkv3, qkv3, qkv3)


def sb_bwd(cfg, qkv3, do_sb, tsum):
    S, QB, DH, NH = cfg.S, cfg.SBT, cfg.DH, cfg.SBHB
    scale = DH ** -0.5

    def body(q_ref, k_ref, v_ref, do_ref, t_ref, d_ref, dk_acc, dv_acc):
        dk_acc[...] = jnp.zeros_like(dk_acc)
        dv_acc[...] = jnp.zeros_like(dv_acc)
        row = lax.broadcasted_iota(jnp.int32, (QB, QB), 0)
        col = lax.broadcasted_iota(jnp.int32, (QB, QB), 1)
        u_upto = (row <= col).astype(BF16)
        u_before = (row < col).astype(BF16)
        causal = col < row
        heads = [slice(h * DH, (h + 1) * DH) for h in range(NH)]

        def q_loop(qb, _):
            rows = pl.ds(pl.multiple_of(qb * QB, QB), QB)
            qs = [q_ref[rows, hd] for hd in heads]
            dos = [do_ref[rows, hd] for hd in heads]
            totals = [t_ref[rows, hd.start:hd.start + 1] for hd in heads]

            def tile(kb, carry, valid):
                krows = pl.ds(pl.multiple_of(kb * QB, QB), QB)
                ks = [k_ref[krows, hd] for hd in heads]
                lbk = [_sb_tile(cfg, q, k, valid) for q, k in zip(qs, ks)]
                das = [_dot_nt(do, v_ref[krows, hd]) for do, hd in zip(dos, heads)]
                pins = [_dot_split(lk, u_upto) for _, lk in lbk]
                aa = [_masked(valid, jnp.exp(lb + (tot - pc - pin)))
                      for (lb, _), tot, (_, pc, _), pin in zip(lbk, totals, carry, pins)]
                gs = [a * da for a, da in zip(aa, das)]
                for a, do, hd in zip(aa, dos, heads):
                    dv_acc[krows, hd] += _dot_tn(a.astype(BF16), do)
                cums = [gc + _dot(g.astype(BF16), u_before) for g, (_, _, gc) in zip(gs, carry)]
                dzs = [(_masked(valid, g - jnp.exp(lb) * (g + cum)) * scale).astype(BF16)
                       for g, (lb, _), cum in zip(gs, lbk, cums)]
                for dz, q, hd in zip(dzs, qs, heads):
                    dk_acc[krows, hd] += _dot_tn(dz, q)
                return tuple((dq + _dot(dz, k), pc + jnp.sum(lk, axis=1, keepdims=True), gc + jnp.sum(g, axis=1, keepdims=True))
                             for dz, k, (_, lk), g, (dq, pc, gc) in zip(dzs, ks, lbk, gs, carry))

            z1 = jnp.zeros((QB, 1), F32)
            carry = lax.fori_loop(0, qb, lambda kb, cr: tile(kb, cr, None), ((jnp.zeros((QB, DH), F32), z1, z1),) * NH)
            for hd, (dq_acc, _, _) in zip(heads, tile(qb, carry, causal)):
                d_ref[0, rows, hd] = dq_acc.astype(BF16)
            return 0

        lax.fori_loop(0, S // QB, q_loop, 0)
        d_ref[1, :, :] = dk_acc[...].astype(BF16)
        d_ref[2, :, :] = dv_acc[...].astype(BF16)

    def spec(i):
        return pl.BlockSpec((None, S, NH * DH), lambda h: (i, 0, h))

    hd_spec = pl.BlockSpec((S, NH * DH), lambda h: (0, h))
    return pl.pallas_call(
        body, name="sb_bwd", grid=(cfg.HSB // NH,),
        in_specs=[spec(0), spec(1), spec(2), hd_spec, hd_spec],
        out_specs=pl.BlockSpec((3, S, NH * DH), lambda h: (0, 0, h)),
        out_shape=jax.ShapeDtypeStruct((6, S, cfg.DSB), BF16),
        scratch_shapes=[pltpu.VMEM((S, NH * DH), F32), pltpu.VMEM((S, NH * DH), F32)],
        compiler_params=_cp(cfg, ("parallel",)),
    )(qkv3, qkv3, qkv3, do_sb, tsum)


def _band_mask(cfg, n, n_back):
    QB = cfg.QB
    qi = lax.broadcasted_iota(jnp.int32, (QB, 2 * QB), 0)
    kj = lax.broadcasted_iota(jnp.int32, (QB, 2 * QB), 1)
    dist = QB + qi - kj
    return (dist >= 0) & (dist <= n_back) & jnp.logical_or(n > 0, kj >= QB)


def _sub_rows(start, n, dil):
    if dil > 1:
        return pl.ds(start, n, stride=dil)
    return pl.ds(start if isinstance(start, int) else pl.multiple_of(start, 8), n)


def _stage_residues(cfg, dil, pairs):
    QB, L = cfg.QB, cfg.S // dil
    for src, dst in pairs:
        for r in range(dil):
            dst[pl.ds(r * (QB + L), QB), :] = jnp.zeros((QB, cfg.DH), BF16)
            dst[pl.ds(r * (QB + L) + QB, L), :] = src[_sub_rows(r, L, dil), :].astype(BF16)


def _staged_rows(cfg):
    return cfg.S + cfg.QB * max(d for _, d in cfg.branches)


def _lane_value(x):
    return jnp.max(x, axis=1, keepdims=True)


def dil_fwd(cfg, qkv3):
    S, QB, DH = cfg.S, cfg.QB, cfg.DH
    scale = DH ** -0.5
    nb = len(cfg.branches)
    mix_rows = min(256, S)

    def body(q_ref, k_ref, v_ref, o_ref, lt_ref, qf, kf, vf, kp, vp, *obl):
        obs, lbs = obl[:nb], obl[nb:]
        qf[...] = q_ref[...].astype(F32)
        kf[...] = k_ref[...].astype(F32)
        vf[...] = v_ref[...].astype(F32)
        for b, (window, dil) in enumerate(cfg.branches):
            L, n_back = S // dil, window // dil
            assert n_back <= QB and L % QB == 0
            _stage_residues(cfg, dil, [(kf, kp), (vf, vp)])
            for r in range(dil):
                for n in range(L // QB):
                    rows = _sub_rows(r + n * (QB * dil), QB, dil)
                    band = pl.ds(r * (QB + L) + n * QB, 2 * QB)
                    s = _dot_nt(qf[rows, :].astype(BF16), kp[band, :]) * scale
                    s = jnp.where(_band_mask(cfg, n, n_back), s, NEG_BIG)
                    m = jnp.max(s, axis=1, keepdims=True)
                    p = jnp.exp(s - m)
                    den = jnp.sum(p, axis=1, keepdims=True)
                    obs[b][rows, :] = _dot(p.astype(BF16), vp[band, :]) / den
                    lbs[b][rows, :] = jnp.broadcast_to(m + jnp.log(den), (QB, DH))

        def mix(i, _):
            rows = pl.ds(pl.multiple_of(i * mix_rows, mix_rows), mix_rows)
            ls = [r[rows, :] for r in lbs]
            m = functools.reduce(jnp.maximum, ls)
            es = [jnp.exp(l - m) for l in ls]
            tot = functools.reduce(jnp.add, es)
            o_ref[rows, :] = functools.reduce(jnp.add, [(e / tot) * r[rows, :] for e, r in zip(es, obs)])
            lt_ref[rows, :] = m + jnp.log(tot)
            return 0

        lax.fori_loop(0, S // mix_rows, mix, 0)

    def spec(i):
        return pl.BlockSpec((None, S, DH), lambda h: (i, 0, h))

    o_spec = pl.BlockSpec((S, DH), lambda h: (0, h))
    return pl.pallas_call(
        body, name="dil_fwd", grid=(cfg.HDL,),
        in_specs=[spec(3), spec(4), spec(5)], out_specs=[o_spec, o_spec],
        out_shape=[jax.ShapeDtypeStruct((S, cfg.DDL), F32)] * 2,
        scratch_shapes=[pltpu.VMEM((S, DH), F32)] * 3 + [pltpu.VMEM((_staged_rows(cfg), DH), BF16)] * 2
        + [pltpu.VMEM((S, DH), F32)] * (2 * nb),
        compiler_params=_cp(cfg, ("parallel",)),
    )(qkv3, qkv3, qkv3)


def dil_bwd(cfg, qkv3, do_dl, delta, lse_tot, cos2, sin2, d_sb3):
    S, QB, DH = cfg.S, cfg.QB, cfg.DH
    scale = DH ** -0.5
    out_rows = min(256, S)
    GROUP = 4

    def body(q_ref, k_ref, v_ref, do_ref, dl_ref, lt_ref, c_ref, s_ref, base_ref, d_ref,
             qf, kf, vf, dof, kp, vp, dkp, dvp, dqn, dkn, dvn):
        qf[...] = q_ref[...].astype(F32)
        kf[...] = k_ref[...].astype(F32)
        vf[...] = v_ref[...].astype(F32)
        dof[...] = do_ref[...].astype(F32)
        for acc in (dqn, dkn, dvn):
            acc[...] = jnp.zeros_like(acc)
        for window, dil in cfg.branches:
            L, n_back = S // dil, window // dil
            reg = QB + L
            _stage_residues(cfg, dil, [(kf, kp), (vf, vp)])
            dkp[pl.ds(0, dil * reg), :] = jnp.zeros((dil * reg, DH), F32)
            dvp[pl.ds(0, dil * reg), :] = jnp.zeros((dil * reg, DH), F32)
            blocks = [(r, n) for n in range(L // QB) for r in range(dil)]
            for g0 in range(0, len(blocks), GROUP):
                grp = blocks[g0:g0 + GROUP]
                rows = [_sub_rows(r + n * (QB * dil), QB, dil) for r, n in grp]
                bands = [pl.ds(r * reg + n * QB, 2 * QB) for r, n in grp]
                qs = [qf[rw, :].astype(BF16) for rw in rows]
                dos = [dof[rw, :].astype(BF16) for rw in rows]
                kbs = [kp[bd, :] for bd in bands]
                ss = [_dot_nt(q, kb) * scale for q, kb in zip(qs, kbs)]
                dps = [_dot_nt(do, vp[bd, :]) for do, bd in zip(dos, bands)]
                ps = [jnp.exp(jnp.where(_band_mask(cfg, n, n_back), s, NEG_BIG) - _lane_value(lt_ref[rw, :]))
                      for s, rw, (_, n) in zip(ss, rows, grp)]
                dss = [(p * (dp - _lane_value(dl_ref[rw, :])) * scale).astype(BF16) for p, dp, rw in zip(ps, dps, rows)]
                for rw, bd, ds, p, q, do, kb in zip(rows, bands, dss, ps, qs, dos, kbs):
                    dqn[rw, :] += _dot(ds, kb)
                    dkp[bd, :] += _dot_tn(ds, q)
                    dvp[bd, :] += _dot_tn(p.astype(BF16), do)
            for r in range(dil):
                sub = _sub_rows(r, L, dil)
                dkn[sub, :] += dkp[pl.ds(r * reg + QB, L), :]
                dvn[sub, :] += dvp[pl.ds(r * reg + QB, L), :]

        def finish(i, _):
            rows = pl.ds(pl.multiple_of(i * out_rows, out_rows), out_rows)
            c, sn = c_ref[rows, :], s_ref[rows, :]
            for j, acc in enumerate((dqn, dkn)):
                d = acc[rows, :]
                d_ref[j, rows, :] = (d * c + pltpu.roll(d * sn, DH // 2, 1)).astype(BF16)
            d_ref[2, rows, :] = dvn[rows, :].astype(BF16)
            return 0

        lax.fori_loop(0, S // out_rows, finish, 0)

    def spec(i):
        return pl.BlockSpec((None, S, DH), lambda h: (i, 0, h))

    hd = pl.BlockSpec((S, DH), lambda h: (0, h))
    tab = pl.BlockSpec((S, DH), lambda h: (0, 0))
    ns = _staged_rows(cfg)
    return pl.pallas_call(
        body, name="dil_bwd", grid=(cfg.HDL,),
        in_specs=[spec(3), spec(4), spec(5), hd, hd, hd, tab, tab, ANY],
        out_specs=pl.BlockSpec((3, S, DH), lambda h: (1, 0, h)),
        out_shape=jax.ShapeDtypeStruct((6, S, cfg.DDL), BF16),
        input_output_aliases={8: 0},
        scratch_shapes=[pltpu.VMEM((S, DH), F32)] * 4 + [pltpu.VMEM((ns, DH), BF16)] * 2
        + [pltpu.VMEM((ns, DH), F32)] * 2 + [pltpu.VMEM((S, DH), F32)] * 3,
        compiler_params=_cp(cfg, ("parallel",)),
    )(qkv3, qkv3, qkv3, do_dl, delta, lse_tot, cos2, sin2, d_sb3)


def combine_fwd(cfg, o_sb, o_dl, g_sb, g_dl):
    S, DH = cfg.S, cfg.DH

    def head_norm(o, g):
        return o * lax.rsqrt(jnp.mean(o * o, axis=-1, keepdims=True) + RMS_EPS) * g

    def body(osb_ref, odl_ref, gsb_ref, gdl_ref, mix_ref):
        for h in range(cfg.HSB):
            c = slice(h * DH, (h + 1) * DH)
            mix_ref[:, c] = head_norm(osb_ref[:, c], gsb_ref[:, c]).astype(BF16)
        for h in range(cfg.HDL):
            c = slice(h * DH, (h + 1) * DH)
            mix_ref[:, cfg.DSB + h * DH:cfg.DSB + (h + 1) * DH] = head_norm(odl_ref[:, c], gdl_ref[:, c]).astype(BF16)

    return pl.pallas_call(
        body, name="combine_fwd", grid=(S // cfg.TM,),
        in_specs=[_row(cfg, cfg.DSB), _row(cfg, cfg.DDL), _vec(cfg.DSB), _vec(cfg.DDL)],
        out_specs=_row(cfg, cfg.DMIX), out_shape=jax.ShapeDtypeStruct((S, cfg.DMIX), BF16),
        compiler_params=_cp(cfg, ("parallel",)),
    )(o_sb, o_dl, g_sb, g_dl)


def combine_bwd(cfg, dmix, o_sb, o_dl, g_sb, g_dl):
    S, DH = cfg.S, cfg.DH

    def body(dm_ref, osb_ref, odl_ref, gsb_ref, gdl_ref, dsb_ref, ddl_ref, dl_ref, dgsb_ref, dgdl_ref):
        @pl.when(pl.program_id(0) == 0)
        def _():
            dgsb_ref[...] = jnp.zeros_like(dgsb_ref)
            dgdl_ref[...] = jnp.zeros_like(dgdl_ref)

        for h in range(cfg.HSB):
            c = slice(h * DH, (h + 1) * DH)
            dx, dgx = _rms_bwd(dm_ref[:, c], osb_ref[:, c], gsb_ref[:, c])
            dsb_ref[:, c] = dx.astype(BF16)
            dgsb_ref[:, c] += jnp.sum(dgx, axis=0, keepdims=True)
        for h in range(cfg.HDL):
            c = slice(h * DH, (h + 1) * DH)
            o = odl_ref[:, c]
            dx, dgx = _rms_bwd(dm_ref[:, cfg.DSB + h * DH:cfg.DSB + (h + 1) * DH], o, gdl_ref[:, c])
            ddl_ref[:, c] = dx.astype(BF16)
            dl_ref[:, c] = jnp.broadcast_to(jnp.sum(dx * o, axis=-1, keepdims=True), dx.shape)
            dgdl_ref[:, c] += jnp.sum(dgx, axis=0, keepdims=True)

    return pl.pallas_call(
        body, name="combine_bwd", grid=(S // cfg.TM,),
        in_specs=[_row(cfg, cfg.DMIX), _row(cfg, cfg.DSB), _row(cfg, cfg.DDL), _vec(cfg.DSB), _vec(cfg.DDL)],
        out_specs=[_row(cfg, cfg.DSB), _row(cfg, cfg.DDL), _row(cfg, cfg.DDL), _vec(cfg.DSB), _vec(cfg.DDL)],
        out_shape=[jax.ShapeDtypeStruct((S, cfg.DSB), BF16), jax.ShapeDtypeStruct((S, cfg.DDL), BF16),
                   jax.ShapeDtypeStruct((S, cfg.DDL), F32), jax.ShapeDtypeStruct((1, cfg.DSB), F32),
                   jax.ShapeDtypeStruct((1, cfg.DDL), F32)],
        compiler_params=_cp(cfg, ("arbitrary",)),
    )(dmix, o_sb, o_dl, g_sb, g_dl)


SUB = 8


def _shift_down(u, prev, j):
    rolled = pltpu.roll(u, j, 0)
    row = lax.broadcasted_iota(jnp.int32, (SUB, u.shape[1]), 0)
    head = jnp.where(row >= j, rolled[:SUB], pltpu.roll(prev, j, 0))
    return jnp.concatenate([head, rolled[SUB:]], axis=0)


def _shift_up(u, nxt, j):
    n = u.shape[0]
    rolled = pltpu.roll(u, n - j, 0)
    row = lax.broadcasted_iota(jnp.int32, (SUB, u.shape[1]), 0)
    tail = jnp.where(row < SUB - j, rolled[n - SUB:], pltpu.roll(nxt, SUB - j, 0))
    return jnp.concatenate([rolled[:n - SUB], tail], axis=0)


def _conv(u, s1, s2, cw, cb):
    return u * cw[2:3, :] + s1 * cw[1:2, :] + s2 * cw[0:1, :] + cb


def _chunk_rows(cfg):
    ch = min(cfg.FCH, cfg.S)
    return ch, cfg.S // ch


def ffn_fwd(cfg, h2, w_up, conv_w, conv_b):
    S, D = h2.shape
    tn, nt = cfg.TNF, cfg.FFP // cfg.TNF
    ch, nch = _chunk_rows(cfg)

    def body(h_ref, wg_ref, wv_ref, cwg_ref, cwv_ref, cbg_ref, cbv_ref, u_ref, y_ref):
        prev = [jnp.zeros((SUB, tn), F32)] * 2
        pending = None
        for ci in range(nch + 1):
            if ci < nch:
                h = h_ref[pl.ds(ci * ch, ch), :]
                us_next = [_dot_nt(h, wg_ref[...]), _dot_nt(h, wv_ref[...])]
            if pending is not None:
                rows, us = pending
                cs = []
                for i, (cw_ref, cb_ref) in enumerate(((cwg_ref, cbg_ref), (cwv_ref, cbv_ref))):
                    u_ref[i, rows, :] = us[i]
                    cs.append(_conv(us[i], _shift_down(us[i], prev[i], 1), _shift_down(us[i], prev[i], 2),
                                    cw_ref[...], cb_ref[...]))
                y_ref[rows, :] = (_gelu(cs[0])[0] * cs[1]).astype(BF16)
                prev = [u[ch - SUB:] for u in us]
            pending = (pl.ds(ci * ch, ch), us_next) if ci < nch else None

    return pl.pallas_call(
        body, name="ffn_fwd", grid=(nt,),
        in_specs=[pl.BlockSpec((S, D), lambda n: (0, 0)),
                  pl.BlockSpec((tn, D), lambda n: (n, 0)), pl.BlockSpec((tn, D), lambda n: (n + nt, 0)),
                  pl.BlockSpec((3, tn), lambda n: (0, n)), pl.BlockSpec((3, tn), lambda n: (0, n + nt)),
                  pl.BlockSpec((1, tn), lambda n: (0, n)), pl.BlockSpec((1, tn), lambda n: (0, n + nt))],
        out_specs=[pl.BlockSpec((2, S, tn), lambda n: (0, 0, n)), pl.BlockSpec((S, tn), lambda n: (0, n))],
        out_shape=[jax.ShapeDtypeStruct((2, S, cfg.FFP), F32), jax.ShapeDtypeStruct((S, cfg.FFP), BF16)],
        compiler_params=_cp(cfg, ("parallel",)),
    )(h2, w_up, w_up, conv_w, conv_w, conv_b, conv_b)


def ffn_bwd(cfg, df, h2, w_down, u, conv_w, conv_b):
    S, D = df.shape
    tn, nt = cfg.TNF, cfg.FFP // cfg.TNF

    ch, nch = _chunk_rows(cfg)

    def body(df_ref, h_ref, wd_ref, u_ref, cwg_ref, cwv_ref, cbg_ref, cbv_ref,
             du_ref, dwd_ref, dwu_ref, dcw_ref, dcb_ref):
        cws = (cwg_ref[...], cwv_ref[...])
        cbs = (cbg_ref[...], cbv_ref[...])
        zero = jnp.zeros((SUB, tn), F32)
        nxt = [zero, zero]
        dws = [[jnp.zeros((1, tn), F32)] * 4 for _ in range(2)]
        dwd = jnp.zeros((tn, D), F32)
        dwu = [jnp.zeros((tn, D), F32)] * 2
        order = list(reversed(range(nch)))
        dys, done = {}, {}
        for step in range(nch + 2):
            if step < nch:
                ci = order[step]
                dys[ci] = _dot_nt(df_ref[pl.ds(ci * ch, ch), :], wd_ref[...])
            if 1 <= step <= nch:
                ci = order[step - 1]
                rows = pl.ds(ci * ch, ch)
                dy = dys.pop(ci)
                us, s1, s2, cs = [], [], [], []
                for i in range(2):
                    u = u_ref[i, rows, :]
                    prev = u_ref[i, pl.ds(ci * ch - SUB, SUB), :] if ci else zero
                    us.append(u)
                    s1.append(_shift_down(u, prev, 1))
                    s2.append(_shift_down(u, prev, 2))
                    cs.append(_conv(u, s1[i], s2[i], cws[i], cbs[i]))
                gl, t = _gelu(cs[0])
                dcs = (dy * cs[1] * _gelu_grad(cs[0], t), dy * gl)
                dus = []
                for i, dc in enumerate(dcs):
                    du = dc * cws[i][2:3, :] + _shift_up(dc, nxt[i], 1) * cws[i][1:2, :] + _shift_up(dc, nxt[i], 2) * cws[i][0:1, :]
                    dus.append(du.astype(BF16))
                    du_ref[i, rows, :] = dus[i]
                    for j, tap in enumerate((s2[i], s1[i], us[i])):
                        dws[i][j] = dws[i][j] + jnp.sum(dc * tap, axis=0, keepdims=True)
                    dws[i][3] = dws[i][3] + jnp.sum(dc, axis=0, keepdims=True)
                nxt = [dc[:SUB] for dc in dcs]
                done[ci] = ((gl * cs[1]).astype(BF16), dus)
            if step >= 2:
                ci = order[step - 2]
                rows = pl.ds(ci * ch, ch)
                yv, dus = done.pop(ci)
                dwd = dwd + _dot_tn(yv, df_ref[rows, :])
                hv = h_ref[rows, :]
                dwu = [acc + _dot_tn(du, hv) for acc, du in zip(dwu, dus)]
        dwd_ref[...] = dwd.astype(BF16)
        for i in range(2):
            dwu_ref[i] = dwu[i].astype(BF16)
            for j in range(3):
                dcw_ref[i, j:j + 1, :] = dws[i][j]
            dcb_ref[i] = dws[i][3]

    whole = pl.BlockSpec((S, D), lambda n: (0, 0), pipeline_mode=pl.Buffered(1))
    du, dwd, dwu, dcw, dcb = pl.pallas_call(
        body, name="ffn_bwd", grid=(nt,),
        in_specs=[whole, whole, pl.BlockSpec((tn, D), lambda n: (n, 0)),
                  pl.BlockSpec((2, S, tn), lambda n: (0, 0, n)),
                  pl.BlockSpec((3, tn), lambda n: (0, n)), pl.BlockSpec((3, tn), lambda n: (0, n + nt)),
                  pl.BlockSpec((1, tn), lambda n: (0, n)), pl.BlockSpec((1, tn), lambda n: (0, n + nt))],
        out_specs=[pl.BlockSpec((2, S, tn), lambda n: (0, 0, n)), pl.BlockSpec((tn, D), lambda n: (n, 0)),
                   pl.BlockSpec((2, tn, D), lambda n: (0, n, 0)),
                   pl.BlockSpec((2, 3, tn), lambda n: (0, 0, n)), pl.BlockSpec((2, 1, tn), lambda n: (0, 0, n))],
        out_shape=[jax.ShapeDtypeStruct((2, S, cfg.FFP), BF16), jax.ShapeDtypeStruct((cfg.FFP, D), BF16),
                   jax.ShapeDtypeStruct((2, cfg.FFP, D), BF16),
                   jax.ShapeDtypeStruct((2, 3, cfg.FFP), F32), jax.ShapeDtypeStruct((2, 1, cfg.FFP), F32)],
        compiler_params=_cp(cfg, ("parallel",)),
    )(df, h2, w_down, u, conv_w, conv_w, conv_b, conv_b)
    return du, dwd, dwu.reshape(cfg.FF2P, D), dcw, dcb


def rope_tables(cfg):
    inv_freq = ROPE_THETA ** (-jnp.arange(0, cfg.DH, 2, dtype=F32) / cfg.DH)
    ang = jnp.arange(cfg.S, dtype=F32)[:, None] * inv_freq[None, :]
    cos, sin = jnp.cos(ang), jnp.sin(ang)
    return jnp.concatenate([cos, cos], axis=1), jnp.concatenate([-sin, sin], axis=1)


class LocalWeights:
    def __init__(self, w_in, w_out, w_up, conv_w, w_down):
        self.w = (w_in, w_out, w_up, conv_w, w_down)
        self.grads = {}

    def first_start(self):
        return None

    def weights_first(self, after):
        return self.w[0], self.w[3]

    def start_rest(self, after):
        return None

    def weights_rest(self, group, after):
        return ((self.w[1],), None) if group == 0 else ((self.w[4],), None)

    def forwarded(self, group, after):
        return (self.w[2],)

    def pair_start(self, grads):
        self.grads.update(grads)
        return None

    def reduce_start(self, grads, after=None):
        self.grads.update(grads)
        return None

    def reduce_wait(self, names, after):
        pass


def _after(a, token):
    return a if token is None else a + token[0, 0].astype(a.dtype)


def local_step(cfg, comm, x, target, g1, g2, g3, g4, g_sb, g_dl, conv_b):
    S, D = cfg.S, cfg.D
    cos2, sin2 = rope_tables(cfg)
    full = lambda r, c: pl.BlockSpec((r, c), lambda j, k: (0, 0))

    h1 = rms_fwd(cfg, x, _after(g1, comm.first_start()))
    w_in, conv_w = comm.weights_first(after=h1)
    qkv3 = qkv_proj(cfg, h1, w_in, _after(cos2, comm.start_rest(after=w_in)), sin2)
    o_sb, tsum = sb_fwd(cfg, qkv3)
    o_dl, lse_tot = dil_fwd(cfg, qkv3)
    mixed = combine_fwd(cfg, o_sb, o_dl, g_sb, g_dl)
    (w_out,), token = comm.weights_rest(0, after=mixed)
    tn = cfg.TN
    mo = _mm(cfg, "mix_out", mixed, w_out, nt=False, grid=(D // tn, 1),
             a_spec=full(S, cfg.DMIX), b_spec=pl.BlockSpec((cfg.DMIX, tn), lambda j, k: (0, j)),
             o_spec=pl.BlockSpec((S, tn), lambda j, k: (0, j)),
             out_shape=jax.ShapeDtypeStruct((S, D), F32), acc_shape=(8, LANE), dep=token)
    x1, h2 = mid_fwd(cfg, x, mo, g2, g3)
    w_up, = comm.forwarded(0, after=h2)
    u, y = ffn_fwd(cfg, h2, w_up, conv_w, conv_b)
    (w_down,), _ = comm.weights_rest(1, after=y)
    tk = cfg.FFP // 2
    f = _mm(cfg, "ffn_down", y, w_down, nt=False, grid=(D // tn, cfg.FFP // tk),
            a_spec=pl.BlockSpec((S, tk), lambda j, k: (0, k)), b_spec=pl.BlockSpec((tk, tn), lambda j, k: (k, j)),
            o_spec=pl.BlockSpec((S, tn), lambda j, k: (0, j)),
            out_shape=jax.ShapeDtypeStruct((S, D), F32), acc_shape=(S, tn))
    dout, df, dg4, loss = final_fwd_bwd(cfg, x1, f, g4, target)

    du, dw_down, dw_up, dconv_w, dconv_b = ffn_bwd(cfg, df, h2, w_down, u, conv_w, conv_b)
    kt = cfg.FFP // tk
    dh2 = _mm(cfg, "d_h2", du, w_up, nt=False, grid=(D // tn, 2 * kt),
              a_spec=pl.BlockSpec((None, S, tk), lambda j, k: (k // kt, 0, k % kt)),
              b_spec=pl.BlockSpec((tk, tn), lambda j, k: (k, j)),
              o_spec=pl.BlockSpec((S, tn), lambda j, k: (0, j)),
              out_shape=jax.ShapeDtypeStruct((S, D), F32), acc_shape=(S, tn),
              dep=comm.pair_start(dict(w_down=dw_down, w_up=dw_up)))
    token = comm.reduce_start({}, after=dh2)
    dx1, dmo, dg3, dg2 = mid_bwd(cfg, dh2, x1, _after(g3, token), dout, mo, g2)

    dmix = _mm(cfg, "d_mixed", dmo, w_out, nt=True, grid=(cfg.DMIX // tn, 1),
               a_spec=full(S, D), b_spec=pl.BlockSpec((tn, D), lambda j, k: (j, 0)),
               o_spec=pl.BlockSpec((S, tn), lambda j, k: (0, j)),
               out_shape=jax.ShapeDtypeStruct((S, cfg.DMIX), F32), acc_shape=(8, LANE))
    dw_out = _mm_tn(cfg, "d_w_out", mixed, dmo, grid=(D // tn,),
                    a_spec=pl.BlockSpec((S, cfg.DMIX), lambda j: (0, 0)),
                    b_spec=pl.BlockSpec((S, tn), lambda j: (0, j)),
                    o_spec=pl.BlockSpec((cfg.DMIX, tn), lambda j: (0, j)),
                    out_shape=jax.ShapeDtypeStruct((cfg.DMIX, D), BF16))
    token = comm.reduce_start(dict(w_out=dw_out))
    do_sb, do_dl, delta, dg_sb, dg_dl = combine_bwd(cfg, dmix, o_sb, o_dl, _after(g_sb, token), g_dl)
    d_sb3 = sb_bwd(cfg, qkv3, do_sb, tsum)
    dqkv3 = dil_bwd(cfg, qkv3, do_dl, delta, lse_tot, cos2, sin2, d_sb3)
    comm.reduce_wait(("w_up", "w_down"), after=dqkv3)
    comm.reduce_wait(("w_out",), after=dqkv3)
    tkq = min(tn, cfg.DSB)
    kq = cfg.DSB // tkq
    dw_in = _mm_tn(cfg, "d_w_in", h1, dqkv3, grid=(6 * kq,),
                   a_spec=pl.BlockSpec((S, D), lambda j: (0, 0)),
                   b_spec=pl.BlockSpec((None, S, tkq), lambda j: (j // kq, 0, j % kq)),
                   o_spec=pl.BlockSpec((D, tkq), lambda j: (0, j)),
                   out_shape=jax.ShapeDtypeStruct((D, 6 * cfg.DSB), BF16))
    token = comm.reduce_start(dict(w_in=dw_in))
    dh1 = _mm(cfg, "d_h1", dqkv3, w_in, nt=True, grid=(D // tn, 6),
              a_spec=pl.BlockSpec((None, S, cfg.DSB), lambda j, k: (k, 0, 0)),
              b_spec=pl.BlockSpec((tn, cfg.DSB), lambda j, k: (j, k)),
              o_spec=pl.BlockSpec((S, tn), lambda j, k: (0, j)),
              out_shape=jax.ShapeDtypeStruct((S, D), F32), acc_shape=(S, tn), dep=token)
    grad_x, dg1 = first_bwd(cfg, dh1, x, g1, dx1)
    small = dict(loss=loss, g1=dg1, g2=dg2, g3=dg3, g4=dg4, g_sb=dg_sb, g_dl=dg_dl,
                 conv_b=dconv_b.reshape(1, cfg.FF2P), conv_w=dconv_w.transpose(1, 0, 2).reshape(3, cfg.FF2P))
    return grad_x, small


ANY = pl.BlockSpec(memory_space=pl.ANY)


def _me():
    return lax.axis_index("x"), lax.axis_index("y"), lax.axis_index("c")


def _other_chips(x, y):
    return [(1 - x, y), (x, 1 - y), (1 - x, 1 - y)]


def pad_conv_w(cfg, conv_w, pos):
    r, c = conv_w.shape

    def body(pos_ref, w_ref, full_ref, scr, sem):
        scr[:, :c] = w_ref[...]
        scr[:, c:] = jnp.zeros((r, cfg.FSHP - c), F32)
        cols = pl.ds(pl.multiple_of(pos_ref[0] * cfg.FSHP, LANE), cfg.FSHP)
        cp = pltpu.make_async_copy(scr, full_ref.at[:, cols], sem)
        cp.start()
        cp.wait()

    return pl.pallas_call(
        body, name="pad_conv_w",
        grid_spec=pltpu.PrefetchScalarGridSpec(
            num_scalar_prefetch=1, grid=(1,), in_specs=[pl.BlockSpec((r, c), lambda i, p: (0, 0))], out_specs=ANY,
            scratch_shapes=[pltpu.VMEM((r, cfg.FSHP), F32), pltpu.SemaphoreType.DMA]),
        out_shape=jax.ShapeDtypeStruct(_full_shape(cfg, "conv_w"), F32),
    )(pos, conv_w)


def _tile2(r, c):
    return (256, c) if r % 256 == 0 else (r, 512 if c % 512 == 0 else c)


def cast_into(cfg, name, w, pos, dep=None):
    r, c = w.shape
    _, nr, _, nc = _slab(cfg, name, 0)
    tm, tc = _tile2(r, c)
    wr = nr if tm == r else tm
    assert nc == c and (nr == r or tm == r)
    gap = cfg.FSHP - cfg.FSH if name == "w_down" else 0
    deps = [] if dep is None else [dep]

    def body(pos_ref, w_ref, *rest):
        full_ref, token, scr, sem = rest[len(deps):]
        token[...] = jnp.zeros_like(token)
        tile = w_ref[...]
        if deps:
            tile = tile + rest[0][0:1, 0:1]
        scr[pl.ds(0, tm), :] = tile.astype(BF16)
        if wr > tm:
            scr[pl.ds(tm, wr - tm), :] = jnp.zeros((wr - tm, tc), BF16)
        r0, _, c0, _ = _slab(cfg, name, pos_ref[0])
        rows = pl.ds(pl.multiple_of(r0 + pl.program_id(0) * tm, 16), wr)
        cols = pl.ds(pl.multiple_of(c0 + pl.program_id(1) * tc, LANE), tc)
        cps = [pltpu.make_async_copy(scr.at[pl.ds(0, wr), :], full_ref.at[rows, cols], sem.at[0])]
        if gap:
            scr[pl.ds(wr, gap), :] = jnp.zeros((gap, tc), BF16)
            for h in range(2):
                pad_rows = pl.ds(h * cfg.FSHP + cfg.FSH, gap)
                cps.append(pltpu.make_async_copy(scr.at[pl.ds(wr, gap), :], full_ref.at[pad_rows, cols], sem.at[1 + h]))
        for cp in cps:
            cp.start()
        for cp in cps:
            cp.wait()

    return pl.pallas_call(
        body, name=f"cast_{name}",
        grid_spec=pltpu.PrefetchScalarGridSpec(
            num_scalar_prefetch=1, grid=(r // tm, c // tc),
            in_specs=[pl.BlockSpec((tm, tc), lambda i, j, p: (i, j))]
            + [pl.BlockSpec((8, LANE), lambda i, j, p: (0, 0))] * len(deps),
            out_specs=[ANY, pl.BlockSpec((8, LANE), lambda i, j, p: (0, 0))],
            scratch_shapes=[pltpu.VMEM((wr + gap, tc), BF16), pltpu.SemaphoreType.DMA((3,))]),
        out_shape=[jax.ShapeDtypeStruct(_full_shape(cfg, name), BF16), jax.ShapeDtypeStruct((8, LANE), F32)],
        compiler_params=_cp(cfg, ("arbitrary", "arbitrary")),
    )(pos, w, *deps)


HBM = pl.BlockSpec(memory_space=pltpu.HBM)
SEM = pl.BlockSpec(memory_space=pltpu.SEMAPHORE)
TOKEN = pl.BlockSpec(memory_space=pltpu.VMEM)
EFFECT = pltpu.SideEffectType.DATAFLOW_SIDE_EFFECTING


def _slab(cfg, name, k):
    D = cfg.D
    if name == "w_in":
        cin = 6 * cfg.DSB // N_CHIPS
        return 0, D, k * cin, cin
    if name == "w_out":
        rout = cfg.DMIX // N_CHIPS
        return k * rout, rout, 0, D
    if name == "w_up":
        return k * cfg.FSHP, cfg.FSHP, 0, D
    if name == "conv_w":
        return 0, 3, k * cfg.FSHP, cfg.FSHP
    rdn = cfg.FSH // 2
    return (k // 2) * cfg.FSHP + (k % 2) * rdn, rdn, 0, D


def _full_shape(cfg, name):
    return dict(w_in=(cfg.D, 6 * cfg.DSB), w_out=(cfg.DMIX, cfg.D), w_up=(cfg.FF2P, cfg.D), w_down=(cfg.FFP, cfg.D),
                conv_w=(3, cfg.FF2P))[name]


def _half(cfg, name, ref, k, h):
    r0, nr, c0, nc = _slab(cfg, name, k)
    if name == "conv_w":
        return ref.at[:, pl.ds(c0, nc)]
    return ref.at[pl.ds(r0 + h * (nr // 2), nr // 2), pl.ds(c0, nc)]


def _rows_half(ref, h):
    nr = ref.shape[0] // 2
    return ref.at[pl.ds(h * nr, nr), :]


def _remote(src, dst, send_sem, recv_sem, dev):
    return pltpu.make_async_remote_copy(src_ref=src, dst_ref=dst, send_sem=send_sem, recv_sem=recv_sem,
                                        device_id=dev, device_id_type=MESH)


REST = ("w_out", "w_up", "w_down")
FIRST = (("w_in", "conv_w"),)
GROUPS = (("w_out", "w_up"), ("w_down",))


def _hbm(a):
    return pltpu.with_memory_space_constraint(a, pltpu.HBM)


def gather_start(cfg, tag, groups, fulls, after):
    order = [k for names in groups for k in names]
    n, ng = len(order), len(groups)

    def body(*refs):
        lands = dict(zip(order, refs[:n]))
        sems = refs[n + 1:n + 1 + 2 * ng]
        token = refs[-1]
        x, y, c = _me()
        me = 2 * x + y
        for g, names in enumerate(groups):
            for i, name in enumerate(names):
                mine = _half(cfg, name, lands[name], me, c)
                for j, (px, py) in enumerate(_other_chips(x, y)):
                    _remote(mine, mine, sems[2 * g].at[3 * i + j], sems[2 * g + 1].at[3 * i + j], (px, py, c)).start()
        token[...] = jnp.zeros_like(token)

    ops = [_hbm(fulls[k]) for k in order]
    sem_shapes = [pltpu.SemaphoreType.DMA((3 * len(names),)) for names in groups for _ in range(2)]
    outs = pl.pallas_call(
        body, name=f"gather_start_{tag}",
        in_specs=[HBM] * n + [ANY],
        out_specs=[SEM] * (2 * ng) + [HBM] * n + [TOKEN],
        out_shape=sem_shapes + [pltpu.HBM(a.shape, a.dtype) for a in ops] + [jax.ShapeDtypeStruct((8, LANE), F32)],
        input_output_aliases={i: 2 * ng + i for i in range(n)},
        compiler_params=pltpu.CompilerParams(has_side_effects=EFFECT),
    )(*ops, after)
    thru = dict(zip(order, outs[2 * ng:2 * ng + n]))
    return [(outs[2 * g], outs[2 * g + 1], [thru[k] for k in names]) for g, names in enumerate(groups)], outs[-1]


def gather_wait(cfg, names, ssem, rsem, lands, after):
    n = len(names)

    def body(*refs):
        lands_ = refs[:n]
        ssem_, rsem_ = refs[n], refs[n + 1]
        x, y, c = _me()
        me = 2 * x + y
        for i, name in enumerate(names):
            for j, (px, py) in enumerate(_other_chips(x, y)):
                cp = _remote(_half(cfg, name, lands_[i], me, c), _half(cfg, name, lands_[i], 2 * px + py, c),
                             ssem_.at[3 * i + j], rsem_.at[3 * i + j], (px, py, c))
                cp.wait_send()
                cp.wait_recv()

    return pl.pallas_call(
        body, name="gather_wait_" + "_".join(names),
        in_specs=[HBM] * n + [SEM, SEM, ANY], out_specs=[HBM] * n,
        out_shape=[pltpu.HBM(a.shape, a.dtype) for a in lands],
        input_output_aliases={i: i for i in range(n)},
        compiler_params=pltpu.CompilerParams(has_side_effects=EFFECT),
    )(*lands, ssem, rsem, after)


def gather_finish(cfg, names, lands):
    n = len(names)

    def body(*refs):
        outs = refs[n:2 * n]
        ssem, rsem = refs[2 * n:]
        x, y, c = _me()
        sib = (x, y, 1 - c)
        fwds = []
        for i, name in enumerate(names):
            for j, (px, py) in enumerate(_other_chips(x, y)):
                landed = _half(cfg, name, outs[i], 2 * px + py, c)
                fwds.append(_remote(landed, landed, ssem.at[3 * i + j], rsem.at[3 * i + j], sib))
        for cp in fwds:
            cp.start()
        for i, name in enumerate(names):
            for j, (px, py) in enumerate(_other_chips(x, y)):
                passed = _half(cfg, name, outs[i], 2 * px + py, 1 - c)
                _remote(passed, passed, ssem.at[3 * i + j], rsem.at[3 * i + j], sib).wait_recv()
        for cp in fwds:
            cp.wait_send()

    return pl.pallas_call(
        body, name="gather_finish_" + "_".join(names), in_specs=[ANY] * n, out_specs=[ANY] * n,
        out_shape=[jax.ShapeDtypeStruct(a.shape, a.dtype) for a in lands],
        input_output_aliases={i: i for i in range(n)},
        scratch_shapes=[pltpu.SemaphoreType.DMA((3 * n,)), pltpu.SemaphoreType.DMA((3 * n,))],
    )(*lands)


def forward_start(cfg, names, lands, after):
    n = len(names)

    def body(*refs):
        outs = refs[:n]
        ssem, rsem = refs[n + 1], refs[n + 2]
        token = refs[-1]
        x, y, c = _me()
        for i, name in enumerate(names):
            for j, (px, py) in enumerate(_other_chips(x, y)):
                landed = _half(cfg, name, outs[i], 2 * px + py, c)
                _remote(landed, landed, ssem.at[3 * i + j], rsem.at[3 * i + j], (x, y, 1 - c)).start()
        token[...] = jnp.zeros_like(token)

    ops = [_hbm(a) for a in lands]
    outs = pl.pallas_call(
        body, name="forward_start_" + "_".join(names),
        in_specs=[HBM] * n + [ANY], out_specs=[SEM, SEM] + [HBM] * n + [TOKEN],
        out_shape=[pltpu.SemaphoreType.DMA((3 * n,)), pltpu.SemaphoreType.DMA((3 * n,))]
        + [pltpu.HBM(a.shape, a.dtype) for a in ops] + [jax.ShapeDtypeStruct((8, LANE), F32)],
        input_output_aliases={i: 2 + i for i in range(n)},
        compiler_params=pltpu.CompilerParams(has_side_effects=EFFECT),
    )(*ops, after)
    return outs[0], outs[1], outs[2:2 + n], outs[-1]


def forward_wait(cfg, names, ssem, rsem, lands, after):
    n = len(names)

    def body(*refs):
        outs = refs[:n]
        ssem_, rsem_ = refs[n], refs[n + 1]
        x, y, c = _me()
        for i, name in enumerate(names):
            for j, (px, py) in enumerate(_other_chips(x, y)):
                cp = _remote(_half(cfg, name, outs[i], 2 * px + py, c), _half(cfg, name, outs[i], 2 * px + py, 1 - c),
                             ssem_.at[3 * i + j], rsem_.at[3 * i + j], (x, y, 1 - c))
                cp.wait_send()
                cp.wait_recv()

    return pl.pallas_call(
        body, name="forward_wait_" + "_".join(names),
        in_specs=[HBM] * n + [SEM, SEM, ANY], out_specs=[HBM] * n,
        out_shape=[pltpu.HBM(a.shape, a.dtype) for a in lands],
        input_output_aliases={i: i for i in range(n)},
        compiler_params=pltpu.CompilerParams(has_side_effects=EFFECT),
    )(*lands, ssem, rsem, after)


def pair_send(cfg, grads):
    names = list(grads)
    n = len(names)

    def half_shape(name):
        _, nr, _, nc = _slab(cfg, name, 0)
        return (N_CHIPS, nr // 2, nc)

    def body(*refs):
        srcs, theirs = refs[:n], refs[n:2 * n]
        ssem, rsem = refs[2 * n:]
        x, y, c = _me()
        cps = []
        for i, name in enumerate(names):
            for k in range(N_CHIPS):
                cps.append(_remote(_half(cfg, name, srcs[i], k, 1 - c), theirs[i].at[k],
                                   ssem.at[N_CHIPS * i + k], rsem.at[N_CHIPS * i + k], (x, y, 1 - c)))
        for cp in cps:
            cp.start()
        for cp in cps:
            cp.wait()

    outs = pl.pallas_call(
        body, name="pair_send_" + "_".join(names), in_specs=[ANY] * n, out_specs=[ANY] * n,
        out_shape=[jax.ShapeDtypeStruct(half_shape(name), BF16) for name in names],
        scratch_shapes=[pltpu.SemaphoreType.DMA((N_CHIPS * n,))] * 2,
    )(*[grads[k] for k in names])
    return dict(zip(names, outs))


def pair_start(cfg, grads, after):
    names = list(grads)
    n = len(names)

    def body(*refs):
        srcs, theirs = refs[:n], refs[n:2 * n]
        ssem, rsem = refs[2 * n + 1], refs[2 * n + 2]
        token = refs[-1]
        x, y, c = _me()
        for i, name in enumerate(names):
            for k in range(N_CHIPS):
                _remote(_half(cfg, name, srcs[i], k, 1 - c), theirs[i].at[k],
                        ssem.at[N_CHIPS * i + k], rsem.at[N_CHIPS * i + k], (x, y, 1 - c)).start()
        token[...] = jnp.zeros_like(token)

    def half_shape(name):
        _, nr, _, nc = _slab(cfg, name, 0)
        return (N_CHIPS, nr // 2, nc)

    ops = [_hbm(grads[k]) for k in names] + [_hbm(lax.empty(half_shape(k), BF16)) for k in names]
    outs = pl.pallas_call(
        body, name="pair_start_" + "_".join(names),
        in_specs=[HBM] * (2 * n) + [ANY],
        out_specs=[SEM, SEM] + [HBM] * (2 * n) + [TOKEN],
        out_shape=[pltpu.SemaphoreType.DMA((N_CHIPS * n,)), pltpu.SemaphoreType.DMA((N_CHIPS * n,))]
        + [pltpu.HBM(a.shape, a.dtype) for a in ops] + [jax.ShapeDtypeStruct((8, LANE), F32)],
        input_output_aliases={i: 2 + i for i in range(2 * n)},
        compiler_params=pltpu.CompilerParams(has_side_effects=EFFECT),
    )(*ops, after)
    return outs[0], outs[1], dict(zip(names, outs[2:2 + n])), dict(zip(names, outs[2 + n:2 + 2 * n])), outs[-1]


def pair_wait(cfg, ssem, rsem, grads, theirs, after):
    names = list(grads)
    n = len(names)

    def body(*refs):
        srcs, theirs_ = refs[:n], refs[n:2 * n]
        ssem_, rsem_ = refs[2 * n], refs[2 * n + 1]
        x, y, c = _me()
        for i, name in enumerate(names):
            for k in range(N_CHIPS):
                cp = _remote(_half(cfg, name, srcs[i], k, 1 - c), theirs_[i].at[k],
                             ssem_.at[N_CHIPS * i + k], rsem_.at[N_CHIPS * i + k], (x, y, 1 - c))
                cp.wait_send()
                cp.wait_recv()

    ops = [grads[k] for k in names] + [theirs[k] for k in names]
    outs = pl.pallas_call(
        body, name="pair_wait_" + "_".join(names),
        in_specs=[HBM] * (2 * n) + [SEM, SEM, ANY], out_specs=[HBM] * (2 * n),
        out_shape=[pltpu.HBM(a.shape, a.dtype) for a in ops],
        input_output_aliases={i: i for i in range(2 * n)},
        compiler_params=pltpu.CompilerParams(has_side_effects=EFFECT),
    )(*ops, ssem, rsem, after)
    return dict(zip(names, outs[:n])), dict(zip(names, outs[n:]))


def pair_sum(cfg, name, grad, theirs, pos):
    _, r, c = theirs.shape
    tm, tc = _tile2(r, c)

    ni, nj = r // tm, c // tc
    total = N_CHIPS * ni * nj

    def body(pos_ref, g_ref, t_ref, o_ref, scr, sem):
        step = (pl.program_id(0) * ni + pl.program_id(1)) * nj + pl.program_id(2)

        def fetch(flat, slot):
            k, rem = flat // (ni * nj), flat % (ni * nj)
            r0, nr, c0, _ = _slab(cfg, name, k)
            rows = pl.ds(pl.multiple_of(r0 + pos_ref[1] * (nr // 2) + (rem // nj) * tm, 16), tm)
            cols = pl.ds(pl.multiple_of(c0 + (rem % nj) * tc, LANE), tc)
            return pltpu.make_async_copy(g_ref.at[rows, cols], scr.at[slot], sem.at[slot])

        @pl.when(step == 0)
        def _():
            fetch(0, 0).start()

        @pl.when(step + 1 < total)
        def _():
            fetch(step + 1, (step + 1) % 2).start()

        fetch(step, step % 2).wait()
        o_ref[...] = (scr[step % 2].astype(F32) + t_ref[...].astype(F32)).astype(BF16)

    blk = pl.BlockSpec((None, tm, tc), lambda k, i, j, p: (k, i, j))
    return pl.pallas_call(
        body, name=f"pair_sum_{name}",
        grid_spec=pltpu.PrefetchScalarGridSpec(
            num_scalar_prefetch=1, grid=(N_CHIPS, ni, nj), in_specs=[ANY, blk], out_specs=blk,
            scratch_shapes=[pltpu.VMEM((2, tm, tc), BF16), pltpu.SemaphoreType.DMA((2,))]),
        out_shape=jax.ShapeDtypeStruct(theirs.shape, BF16),
        compiler_params=_cp(cfg, ("arbitrary",) * 3),
    )(pos, grad, theirs)


def scatter_start(cfg, pres, after):
    names = list(pres)
    n = len(names)

    def body(*refs):
        srcs, lands = refs[:n], refs[n:2 * n]
        ssem, rsem = refs[2 * n + 1], refs[2 * n + 2]
        token = refs[-1]
        x, y, c = _me()
        for i in range(n):
            for j, (px, py) in enumerate(_other_chips(x, y)):
                _remote(srcs[i].at[2 * px + py], lands[i].at[j], ssem.at[3 * i + j], rsem.at[3 * i + j], (px, py, c)).start()
        token[...] = jnp.zeros_like(token)

    lands = [lax.empty((3,) + pres[k].shape[1:], BF16) for k in names]
    ops = [_hbm(a) for a in [pres[k] for k in names] + lands]
    outs = pl.pallas_call(
        body, name="scatter_start_" + "_".join(names),
        in_specs=[HBM] * (2 * n) + [ANY],
        out_specs=[SEM, SEM] + [HBM] * (2 * n) + [TOKEN],
        out_shape=[pltpu.SemaphoreType.DMA((3 * n,)), pltpu.SemaphoreType.DMA((3 * n,))]
        + [pltpu.HBM(a.shape, a.dtype) for a in ops] + [jax.ShapeDtypeStruct((8, LANE), F32)],
        input_output_aliases={i: 2 + i for i in range(2 * n)},
        compiler_params=pltpu.CompilerParams(has_side_effects=EFFECT),
    )(*ops, after)
    return outs[0], outs[1], dict(zip(names, outs[2:2 + n])), dict(zip(names, outs[2 + n:2 + 2 * n])), outs[-1]


def scatter_wait(cfg, ssem, rsem, pres, lands, after):
    names = list(pres)
    n = len(names)

    def body(*refs):
        srcs, lands_ = refs[:n], refs[n:2 * n]
        ssem_, rsem_ = refs[2 * n], refs[2 * n + 1]
        x, y, c = _me()
        for i in range(n):
            for j, (px, py) in enumerate(_other_chips(x, y)):
                cp = _remote(srcs[i].at[2 * px + py], lands_[i].at[j], ssem_.at[3 * i + j], rsem_.at[3 * i + j], (px, py, c))
                cp.wait_send()
                cp.wait_recv()

    ops = [pres[k] for k in names] + [lands[k] for k in names]
    outs = pl.pallas_call(
        body, name="scatter_wait_" + "_".join(names),
        in_specs=[HBM] * (2 * n) + [SEM, SEM, ANY], out_specs=[HBM] * (2 * n),
        out_shape=[pltpu.HBM(a.shape, a.dtype) for a in ops],
        input_output_aliases={i: i for i in range(2 * n)},
        compiler_params=pltpu.CompilerParams(has_side_effects=EFFECT),
    )(*ops, ssem, rsem, after)
    return dict(zip(names, outs[:n])), dict(zip(names, outs[n:]))


def sum_landed(cfg, name, pre, land, pos):
    _, r, c = pre.shape
    tm, tc = _tile2(r, c)
    nrt = r // tm

    def body(pos_ref, p_ref, l_ref, o_ref):
        acc = p_ref[...].astype(F32)
        for j in range(3):
            acc = acc + l_ref[j].astype(F32)
        o_ref[...] = acc

    return pl.pallas_call(
        body, name=f"sum_landed_{name}",
        grid_spec=pltpu.PrefetchScalarGridSpec(
            num_scalar_prefetch=1, grid=(nrt, c // tc),
            in_specs=[pl.BlockSpec((None, tm, tc), lambda i, j, p: (p[0], i, j)),
                      pl.BlockSpec((3, tm, tc), lambda i, j, p: (0, i, j))],
            out_specs=pl.BlockSpec((tm, tc), lambda i, j, p: (p[1] * nrt + i, j))),
        out_shape=jax.ShapeDtypeStruct((2 * r, c), F32), compiler_params=_cp(cfg, ("parallel", "parallel")),
    )(pos, pre, land)


def half_swap(cfg, sums):
    names = list(sums)
    n = len(names)

    def body(*refs):
        outs = refs[n:2 * n]
        ssem, rsem = refs[2 * n:]
        x, y, c = _me()
        cps = [_remote(_rows_half(outs[i], c), _rows_half(outs[i], c), ssem.at[i], rsem.at[i], (x, y, 1 - c))
               for i in range(n)]
        for cp in cps:
            cp.start()
        for i in range(n):
            theirs = _rows_half(outs[i], 1 - c)
            _remote(theirs, theirs, ssem.at[i], rsem.at[i], (x, y, 1 - c)).wait_recv()
        for cp in cps:
            cp.wait_send()

    outs = pl.pallas_call(
        body, name="half_swap_" + "_".join(names), in_specs=[ANY] * n, out_specs=[ANY] * n,
        out_shape=[jax.ShapeDtypeStruct(sums[k].shape, F32) for k in names],
        input_output_aliases={i: i for i in range(n)},
        scratch_shapes=[pltpu.SemaphoreType.DMA((n,))] * 2,
    )(*[sums[k] for k in names])
    return dict(zip(names, outs))


class MeshWeights:
    def __init__(self, cfg, w_sh):
        self.cfg = cfg
        self.pos = jnp.stack([2 * lax.axis_index("x") + lax.axis_index("y"), lax.axis_index("c")]).astype(jnp.int32)
        self.w_sh = w_sh
        self.full = {"w_in": cast_into(cfg, "w_in", w_sh["w_in"], self.pos)[0],
                     "conv_w": pad_conv_w(cfg, w_sh["conv_w"], self.pos)}
        self.inflight = {}
        self.forwards = {}
        self.grads = {}

    def first_start(self):
        cfg = self.cfg
        self.first, token = gather_start(cfg, "first", FIRST, self.full, jnp.zeros((8, LANE), F32))
        for k in REST:
            self.full[k], token = cast_into(cfg, k, self.w_sh[k], self.pos, dep=token)
        return token

    def weights_first(self, after):
        cfg = self.cfg
        ssem, rsem, lands = self.first[0]
        w_in, conv_w = gather_wait(cfg, FIRST[0], ssem, rsem, lands, after)
        return gather_finish(cfg, ("w_in",), [w_in])[0], conv_w

    def start_rest(self, after):
        self.rest, token = gather_start(self.cfg, "rest", GROUPS, self.full, after)
        return token

    def weights_rest(self, group, after):
        cfg = self.cfg
        names = GROUPS[group]
        ssem, rsem, lands = self.rest[group]
        lands = dict(zip(names, gather_wait(cfg, names, ssem, rsem, lands, after)))
        now = [k for k in names if k != "w_up"]
        later = [k for k in names if k == "w_up"]
        ready = gather_finish(cfg, tuple(now), [lands[k] for k in now])
        if not later:
            return tuple(ready), None
        out = forward_start(cfg, tuple(later), [lands[k] for k in later], ready[0])
        self.forwards[group] = (tuple(later),) + tuple(out[:3])
        return tuple(ready), out[3]

    def forwarded(self, group, after):
        names, ssem, rsem, lands = self.forwards.pop(group)
        return tuple(forward_wait(self.cfg, names, ssem, rsem, lands, after))

    def pair_start(self, grads):
        out = pair_start(self.cfg, grads, jnp.zeros((8, LANE), F32))
        self.pairs = out[:4]
        return out[4]

    def reduce_start(self, grads, after=None):
        theirs = pair_send(self.cfg, grads) if grads else {}
        if after is not None:
            early, early_theirs = pair_wait(self.cfg, *self.pairs, after)
            grads, theirs = {**early, **grads}, {**early_theirs, **theirs}
        pres = {k: pair_sum(self.cfg, k, grads[k], theirs[k], self.pos) for k in grads}
        out = scatter_start(self.cfg, pres, jnp.zeros((8, LANE), F32))
        self.inflight[tuple(sorted(grads))] = out[:4]
        return out[4]

    def reduce_wait(self, names, after):
        cfg = self.cfg
        pres, lands = scatter_wait(cfg, *self.inflight.pop(tuple(sorted(names))), after)
        sums = {k: sum_landed(cfg, k, pres[k], lands[k], self.pos) for k in names}
        self.grads.update(half_swap(cfg, sums))


def allreduce_small(cfg, vec):
    R = vec.shape[0]

    def body(v_ref, o_ref, buf, send_sems, recv_sems):
        x, y, c = _me()
        me = 4 * x + 2 * y + c
        buf[me] = v_ref[...]
        sends = []
        for k in range(1, N_DEV):
            px, py, pc = x ^ (k >> 2), y ^ ((k >> 1) & 1), c ^ (k & 1)
            sends.append(pltpu.make_async_remote_copy(
                src_ref=v_ref, dst_ref=buf.at[me], send_sem=send_sems.at[k], recv_sem=recv_sems.at[k],
                device_id=(px, py, pc), device_id_type=MESH))
        for cp in sends:
            cp.start()
        for k in range(1, N_DEV):
            px, py, pc = x ^ (k >> 2), y ^ ((k >> 1) & 1), c ^ (k & 1)
            pltpu.make_async_remote_copy(
                src_ref=v_ref, dst_ref=buf.at[4 * px + 2 * py + pc], send_sem=send_sems.at[k],
                recv_sem=recv_sems.at[k], device_id=(px, py, pc), device_id_type=MESH).wait_recv()
        for cp in sends:
            cp.wait_send()
        acc = buf[0]
        for j in range(1, N_DEV):
            acc = acc + buf[j]
        o_ref[...] = acc

    return pl.pallas_call(
        body, name="allreduce_small",
        in_specs=[pl.BlockSpec(memory_space=pltpu.VMEM)], out_specs=pl.BlockSpec(memory_space=pltpu.VMEM),
        out_shape=jax.ShapeDtypeStruct((R, LANE), F32),
        scratch_shapes=[pltpu.VMEM((N_DEV, R, LANE), F32), pltpu.SemaphoreType.DMA((N_DEV,)),
                        pltpu.SemaphoreType.DMA((N_DEV,))],
    )(vec)


def adamw(cfg, name, w, m, v, g_parts, tile):
    r, c = w.shape
    tm, tc = tile[0] or r, tile[1] or c
    assert tc == c or all(g.shape[1] == c for g in g_parts)
    n = len(g_parts)
    bc1 = 1.0 - ADAM_B1 ** ADAM_STEP
    bc2 = 1.0 - ADAM_B2 ** ADAM_STEP

    def body(*refs):
        w_ref, m_ref, v_ref = refs[:3]
        g_refs = refs[3:3 + n]
        g_out, d_out, m_out, v_out = refs[3 + n:]
        g = g_refs[0][:, :tc]
        for gr in g_refs[1:]:
            g = g + gr[:, :tc]
        m_new = ADAM_B1 * m_ref[...] + (1.0 - ADAM_B1) * g
        v_new = ADAM_B2 * v_ref[...] + (1.0 - ADAM_B2) * jnp.square(g)
        m_hat = m_new / bc1
        v_hat = v_new / bc2
        g_out[...] = g
        d_out[...] = -ADAM_LR * (m_hat / (jnp.sqrt(v_hat) + ADAM_EPS) + ADAM_WD * w_ref[...])
        m_out[...] = m_new
        v_out[...] = v_new

    blk = pl.BlockSpec((tm, tc), lambda i, j: (i, j))
    return pl.pallas_call(
        body, name=f"adamw_{name}", grid=(r // tm, c // tc),
        in_specs=[blk] * 3 + [pl.BlockSpec((tm, tc if tc < c else g.shape[1]), lambda i, j: (i, j)) for g in g_parts],
        out_specs=[blk] * 4, out_shape=[jax.ShapeDtypeStruct((r, c), F32)] * 4,
        compiler_params=_cp(cfg, ("parallel", "parallel")),
    )(w, m, v, *g_parts)


SMALL_ORDER = ("loss", "g1", "g2", "g3", "g4", "g_sb", "g_dl", "conv_b", "conv_w")


def pack_small(small):
    rows = []
    for k in SMALL_ORDER:
        a = small[k].reshape(-1, LANE)
        rows.append(a)
    flat = jnp.concatenate(rows, axis=0)
    pad = (-flat.shape[0]) % 8
    return jnp.pad(flat, ((0, pad), (0, 0))), [r.shape[0] for r in rows]


def unpack_small(red, small, counts):
    out, at = {}, 0
    for k, n in zip(SMALL_ORDER, counts):
        out[k] = red[at:at + n].reshape(small[k].shape)
        at += n
    return out


def pad_ff(cfg, a):
    r = a.shape[0]
    return jnp.pad(a.reshape(r, N_CHIPS, cfg.FSH), ((0, 0), (0, 0), (0, cfg.FSHP - cfg.FSH))).reshape(r, cfg.FF2P)


def step(cfg, x, target, gains, w_sh, conv_b, m_all, v_all):
    chip = 2 * lax.axis_index("x") + lax.axis_index("y")
    comm = MeshWeights(cfg, w_sh)
    grad_x, small = local_step(cfg, comm, x, target, gains["g1"], gains["g2"], gains["g3"], gains["g4"],
                               gains["g_sb"], gains["g_dl"], pad_ff(cfg, conv_b))

    packed, counts = pack_small(small)
    summed = allreduce_small(cfg, packed)
    comm.reduce_wait(("w_in",), after=summed)
    red = unpack_small(summed, small, counts)

    names = ("w_in", "w_out", "w_up", "w_down")
    up_rows = max(t for t in range(SUB, 513, SUB) if cfg.FSH % t == 0)
    tms = dict(w_in=(cfg.TM, None), w_out=(cfg.TM, None), w_up=(up_rows, None), w_down=(None, cfg.TN // 2))
    res = {}
    for n in names:
        res[n] = adamw(cfg, n, w_sh[n], m_all[n], v_all[n], [comm.grads[n]], tms[n])
    g_cw = lax.dynamic_slice_in_dim(red["conv_w"].reshape(3, N_CHIPS, cfg.FSHP), chip, 1, axis=1)[:, 0, :cfg.FSH]
    res["conv_w"] = adamw(cfg, "conv_w", w_sh["conv_w"], m_all["conv_w"], v_all["conv_w"], [g_cw], (None, None))
    g_cb = red["conv_b"].reshape(1, N_CHIPS, cfg.FSHP)[:, :, :cfg.FSH].reshape(1, N_CHIPS * cfg.FSH)
    res["conv_b"] = adamw(cfg, "conv_b", conv_b, m_all["conv_b"], v_all["conv_b"], [g_cb], (None, None))
    for k in ("g1", "g2", "g3", "g4", "g_sb", "g_dl"):
        res[k] = adamw(cfg, k, gains[k], m_all[k], v_all[k], [red[k]], (None, None))
    return red["loss"][0, 0], grad_x, res


PARAMS = ("pre_mix_gain", "post_mix_gain", "pre_ffn_gain", "post_ffn_gain", "w_in", "sb_out_gain", "dil_out_gain",
          "w_out", "w_up", "conv_w", "conv_b", "w_down")
SHORT = dict(pre_mix_gain="g1", post_mix_gain="g2", pre_ffn_gain="g3", post_ffn_gain="g4", sb_out_gain="g_sb",
             dil_out_gain="g_dl", w_in="w_in", w_out="w_out", w_up="w_up", conv_w="conv_w", conv_b="conv_b",
             w_down="w_down")


def kernel(x, pre_mix_gain, post_mix_gain, pre_ffn_gain, post_ffn_gain, w_in, sb_out_gain, dil_out_gain, w_out, w_up, conv_w, conv_b, w_down, loss_target, m_pre_mix_gain, m_post_mix_gain, m_pre_ffn_gain, m_post_ffn_gain, m_w_in, m_sb_out_gain, m_dil_out_gain, m_w_out, m_w_up, m_conv_w, m_conv_b, m_w_down, v_pre_mix_gain, v_post_mix_gain, v_pre_ffn_gain, v_post_ffn_gain, v_w_in, v_sb_out_gain, v_dil_out_gain, v_w_out, v_w_up, v_conv_w, v_conv_b, v_w_down):
    cfg = CFG
    w = dict(zip(PARAMS, (pre_mix_gain, post_mix_gain, pre_ffn_gain, post_ffn_gain, w_in, sb_out_gain, dil_out_gain,
                          w_out, w_up, conv_w, conv_b, w_down)))
    m = dict(zip(PARAMS, (m_pre_mix_gain, m_post_mix_gain, m_pre_ffn_gain, m_post_ffn_gain, m_w_in, m_sb_out_gain,
                          m_dil_out_gain, m_w_out, m_w_up, m_conv_w, m_conv_b, m_w_down)))
    v = dict(zip(PARAMS, (v_pre_mix_gain, v_post_mix_gain, v_pre_ffn_gain, v_post_ffn_gain, v_w_in, v_sb_out_gain,
                          v_dil_out_gain, v_w_out, v_w_up, v_conv_w, v_conv_b, v_w_down)))
    sq = lambda a: a.reshape(a.shape[1:])
    ws = {SHORT[k]: sq(a) if a.ndim == 3 else a for k, a in w.items()}
    ms = {SHORT[k]: sq(a) if a.ndim == 3 else a for k, a in m.items()}
    vs = {SHORT[k]: sq(a) if a.ndim == 3 else a for k, a in v.items()}
    for d in (ws, ms, vs):
        d["w_up"] = d["w_up"].T
    gains = {k: ws[k] for k in ("g1", "g2", "g3", "g4", "g_sb", "g_dl")}
    w_sh = {k: ws[k] for k in ("w_in", "w_out", "w_up", "conv_w", "w_down")}
    loss, grad_x, res = step(cfg, sq(x), sq(loss_target), gains, w_sh, ws["conv_b"], ms, vs)
    res["w_up"] = [a.T for a in res["w_up"]]
    outs = [loss, grad_x.reshape(x.shape)]
    for i in range(4):
        for k in PARAMS:
            outs.append(res[SHORT[k]][i].reshape(w[k].shape))
    return tuple(outs)
```

```python
import functools
import math
from typing import NamedTuple

import jax
import jax.numpy as jnp
from jax import lax
from jax.experimental import pallas as pl
from jax.experimental.pallas import tpu as pltpu

F32 = jnp.float32
BF16 = jnp.bfloat16
MESH = pl.DeviceIdType.MESH

ROPE_THETA = 10000.0
RMS_EPS = 1e-6
ADAM_LR = 0.001
ADAM_B1 = 0.9
ADAM_B2 = 0.999
ADAM_EPS = 1e-08
ADAM_WD = 0.01
ADAM_STEP = 10
GELU_C = math.sqrt(2.0 / math.pi)
NEG_BIG = -1e30
LANE = 128
N_CHIPS = 4
N_DEV = 8


class Cfg(NamedTuple):
    S: int = 2048
    D: int = 2048
    DH: int = 128
    HSB: int = 8
    HDL: int = 8
    QB: int = 128
    SBT: int = 256
    SBH: int = 4
    SBHB: int = 4
    branches: tuple = ((128, 1), (512, 4), (2048, 16))
    FSH: int = 2752
    FSHP: int = 2816
    TM: int = 256
    TNF: int = 256
    FCH: int = 512
    TN: int = 512
    VMEM_MB: int = 56

    @property
    def DSB(self):
        return self.HSB * self.DH

    @property
    def DDL(self):
        return self.HDL * self.DH

    @property
    def DMIX(self):
        return self.DSB + self.DDL

    @property
    def FFP(self):
        return 2 * self.FSHP

    @property
    def FF2P(self):
        return 4 * self.FSHP


CFG = Cfg()


def _cp(cfg, sem=None):
    return pltpu.CompilerParams(dimension_semantics=sem, vmem_limit_bytes=cfg.VMEM_MB * 2**20)


def _dot(a, b):
    return jnp.dot(a, b, preferred_element_type=F32)


def _dot_nt(a, b):
    return lax.dot_general(a, b, (((1,), (1,)), ((), ())), preferred_element_type=F32)


def _dot_tn(a, b):
    return lax.dot_general(a, b, (((0,), (0,)), ((), ())), preferred_element_type=F32)


def _dot_split(x, u):
    hi = x.astype(BF16)
    lo = (x - hi.astype(F32)).astype(BF16)
    return _dot(hi, u) + _dot(lo, u)


def _rstd(x):
    return lax.rsqrt(jnp.mean(x * x, axis=-1, keepdims=True) + RMS_EPS)


def _rms_bwd(dy, x, g):
    r = _rstd(x)
    xh = x * r
    dxh = dy * g
    dx = r * (dxh - xh * jnp.mean(dxh * xh, axis=-1, keepdims=True))
    return dx, dy * xh


def _gelu(x):
    t = jnp.tanh(GELU_C * (x + 0.044715 * (x * x * x)))
    return 0.5 * x * (1.0 + t), t


def _gelu_grad(x, t):
    return 0.5 * (1.0 + t) + 0.5 * x * (1.0 - t * t) * (GELU_C * (1.0 + 3 * 0.044715 * (x * x)))


def _row(cfg, w):
    return pl.BlockSpec((cfg.TM, w), lambda i: (i, 0))


def _vec(w):
    return pl.BlockSpec((1, w), lambda i: (0, 0))


def rms_fwd(cfg, x, g):
    S, D = x.shape

    def body(x_ref, g_ref, h_ref):
        xv = x_ref[...]
        h_ref[...] = (xv * _rstd(xv) * g_ref[...]).astype(BF16)

    return pl.pallas_call(
        body, name="rms_fwd", grid=(S // cfg.TM,),
        in_specs=[_row(cfg, D), _vec(D)], out_specs=_row(cfg, D),
        out_shape=jax.ShapeDtypeStruct((S, D), BF16), compiler_params=_cp(cfg, ("parallel",)),
    )(x, g)


def mid_fwd(cfg, x, mo, g_post, g_pre):
    S, D = x.shape

    def body(x_ref, mo_ref, gp_ref, gn_ref, x1_ref, h2_ref):
        mo_v = mo_ref[...]
        x1 = x_ref[...] + mo_v * _rstd(mo_v) * gp_ref[...]
        x1_ref[...] = x1
        h2_ref[...] = (x1 * _rstd(x1) * gn_ref[...]).astype(BF16)

    return pl.pallas_call(
        body, name="mid_fwd", grid=(S // cfg.TM,),
        in_specs=[_row(cfg, D), _row(cfg, D), _vec(D), _vec(D)],
        out_specs=[_row(cfg, D), _row(cfg, D)],
        out_shape=[jax.ShapeDtypeStruct((S, D), F32), jax.ShapeDtypeStruct((S, D), BF16)],
        compiler_params=_cp(cfg, ("parallel",)),
    )(x, mo, g_post, g_pre)


def final_fwd_bwd(cfg, x1, f, g_post, target):
    S, D = x1.shape

    def body(x1_ref, f_ref, g_ref, t_ref, dout_ref, df_ref, dg_ref, loss_ref):
        @pl.when(pl.program_id(0) == 0)
        def _():
            dg_ref[...] = jnp.zeros_like(dg_ref)
            loss_ref[...] = jnp.zeros_like(loss_ref)

        fv = f_ref[...]
        g = g_ref[...]
        out = x1_ref[...] + fv * _rstd(fv) * g
        err = out - t_ref[...]
        loss_ref[...] += 0.5 * jnp.sum(jnp.mean(err * err, axis=-1, keepdims=True), axis=0, keepdims=True)
        dout = err * (1.0 / D)
        dout_ref[...] = dout
        df, dgx = _rms_bwd(dout, fv, g)
        df_ref[...] = df.astype(BF16)
        dg_ref[...] += jnp.sum(dgx, axis=0, keepdims=True)

    return pl.pallas_call(
        body, name="final_fwd_bwd", grid=(S // cfg.TM,),
        in_specs=[_row(cfg, D), _row(cfg, D), _vec(D), _row(cfg, D)],
        out_specs=[_row(cfg, D), _row(cfg, D), _vec(D), _vec(LANE)],
        out_shape=[jax.ShapeDtypeStruct((S, D), F32), jax.ShapeDtypeStruct((S, D), BF16),
                   jax.ShapeDtypeStruct((1, D), F32), jax.ShapeDtypeStruct((1, LANE), F32)],
        compiler_params=_cp(cfg, ("arbitrary",)),
    )(x1, f, g_post, target)


def mid_bwd(cfg, dh2, x1, g_pre, dout, mo, g_post):
    S, D = x1.shape

    def body(dh_ref, x1_ref, gn_ref, do_ref, mo_ref, gp_ref, dx1_ref, dmo_ref, dgn_ref, dgp_ref):
        @pl.when(pl.program_id(0) == 0)
        def _():
            dgn_ref[...] = jnp.zeros_like(dgn_ref)
            dgp_ref[...] = jnp.zeros_like(dgp_ref)

        dx, dgx = _rms_bwd(dh_ref[...], x1_ref[...], gn_ref[...])
        dx1 = do_ref[...] + dx
        dx1_ref[...] = dx1
        dgn_ref[...] += jnp.sum(dgx, axis=0, keepdims=True)
        dmo, dgy = _rms_bwd(dx1, mo_ref[...], gp_ref[...])
        dmo_ref[...] = dmo.astype(BF16)
        dgp_ref[...] += jnp.sum(dgy, axis=0, keepdims=True)

    return pl.pallas_call(
        body, name="mid_bwd", grid=(S // cfg.TM,),
        in_specs=[_row(cfg, D), _row(cfg, D), _vec(D), _row(cfg, D), _row(cfg, D), _vec(D)],
        out_specs=[_row(cfg, D), _row(cfg, D), _vec(D), _vec(D)],
        out_shape=[jax.ShapeDtypeStruct((S, D), F32), jax.ShapeDtypeStruct((S, D), BF16),
                   jax.ShapeDtypeStruct((1, D), F32), jax.ShapeDtypeStruct((1, D), F32)],
        compiler_params=_cp(cfg, ("arbitrary",)),
    )(dh2, x1, g_pre, dout, mo, g_post)


def first_bwd(cfg, dh1, x, g_pre, dx1):
    S, D = x.shape

    def body(dh_ref, x_ref, g_ref, r_ref, dx_ref, dg_ref):
        @pl.when(pl.program_id(0) == 0)
        def _():
            dg_ref[...] = jnp.zeros_like(dg_ref)

        dx, dgx = _rms_bwd(dh_ref[...], x_ref[...], g_ref[...])
        dx_ref[...] = r_ref[...] + dx
        dg_ref[...] += jnp.sum(dgx, axis=0, keepdims=True)

    return pl.pallas_call(
        body, name="first_bwd", grid=(S // cfg.TM,),
        in_specs=[_row(cfg, D), _row(cfg, D), _vec(D), _row(cfg, D)],
        out_specs=[_row(cfg, D), _vec(D)],
        out_shape=[jax.ShapeDtypeStruct((S, D), F32), jax.ShapeDtypeStruct((1, D), F32)],
        compiler_params=_cp(cfg, ("arbitrary",)),
    )(dh1, x, g_pre, dx1)


def _mm(cfg, name, a, b, *, nt, a_spec, b_spec, o_spec, grid, out_shape, acc_shape, dep=None):
    nk = grid[-1]
    dot = _dot_nt if nt else _dot
    deps = [] if dep is None else [dep]

    def body(a_ref, b_ref, *rest):
        o_ref, acc_ref = rest[-2:]
        k = pl.program_id(len(grid) - 1)
        part = dot(a_ref[...], b_ref[...])
        if deps:
            part = part + rest[0][0:1, 0:1]
        if nk == 1:
            o_ref[...] = part.astype(o_ref.dtype)
            return

        @pl.when(k == 0)
        def _():
            acc_ref[...] = part

        @pl.when(k > 0)
        def _():
            acc_ref[...] += part

        @pl.when(k == nk - 1)
        def _():
            o_ref[...] = acc_ref[...].astype(o_ref.dtype)

    sem = ("parallel",) * (len(grid) - 1) + ("arbitrary",)
    dep_specs = [pl.BlockSpec((8, LANE), lambda *_: (0, 0))] * len(deps)
    return pl.pallas_call(
        body, name=name, grid=grid, in_specs=[a_spec, b_spec] + dep_specs, out_specs=o_spec, out_shape=out_shape,
        scratch_shapes=[pltpu.VMEM(acc_shape, F32)], compiler_params=_cp(cfg, sem),
    )(a, b, *deps)


def _mm_tn(cfg, name, a, b, *, a_spec, b_spec, o_spec, grid, out_shape):
    def body(a_ref, b_ref, o_ref):
        o_ref[...] = _dot_tn(a_ref[...], b_ref[...]).astype(o_ref.dtype)

    return pl.pallas_call(
        body, name=name, grid=grid, in_specs=[a_spec, b_spec], out_specs=o_spec, out_shape=out_shape,
        compiler_params=_cp(cfg, ("parallel",) * len(grid)),
    )(a, b)


def qkv_proj(cfg, h1, w_in, cos2, sin2):
    S, D = h1.shape
    tn = 2 * cfg.DH
    per = cfg.DSB // tn
    assert cfg.DSB == cfg.DDL
    nblk = 6 * per

    def body(a_ref, b_ref, c_ref, s_ref, o_ref):
        j = pl.program_id(0)
        acc = _dot(a_ref[...], b_ref[...])
        rope = jnp.logical_and(j >= 3 * per, j < 5 * per)

        @pl.when(rope)
        def _():
            for c in range(tn // cfg.DH):
                xh = acc[:, c * cfg.DH:(c + 1) * cfg.DH]
                o_ref[:, c * cfg.DH:(c + 1) * cfg.DH] = (
                    xh * c_ref[...] + pltpu.roll(xh, cfg.DH // 2, 1) * s_ref[...]).astype(BF16)

        @pl.when(jnp.logical_not(rope))
        def _():
            o_ref[...] = acc.astype(BF16)

    return pl.pallas_call(
        body, name="qkv_proj", grid=(nblk,),
        in_specs=[pl.BlockSpec((S, D), lambda j: (0, 0)), pl.BlockSpec((D, tn), lambda j: (0, j)),
                  pl.BlockSpec((S, cfg.DH), lambda j: (0, 0)), pl.BlockSpec((S, cfg.DH), lambda j: (0, 0))],
        out_specs=pl.BlockSpec((None, S, tn), lambda j: (j // per, 0, j % per)),
        out_shape=jax.ShapeDtypeStruct((6, S, cfg.DSB), BF16),
        compiler_params=_cp(cfg, ("parallel",)),
    )(h1, w_in, cos2, sin2)


def _sb_tile(cfg, q, k, valid):
    z = _dot_nt(q, k) * (cfg.DH ** -0.5)
    lb = jnp.minimum(z, 0.0) - jnp.log1p(jnp.exp(-jnp.abs(z)))
    lk = lb - z
    return lb, (lk if valid is None else jnp.where(valid, lk, 0.0))


def _masked(valid, x):
    return x if valid is None else jnp.where(valid, x, 0.0)


def sb_fwd(cfg, qkv3):
    S, QB, DH, NH = cfg.S, cfg.SBT, cfg.DH, cfg.SBH

    def body(q_ref, k_ref, v_ref, o_ref, t_ref):
        row = lax.broadcasted_iota(jnp.int32, (QB, QB), 0)
        col = lax.broadcasted_iota(jnp.int32, (QB, QB), 1)
        u_after = (row > col).astype(BF16)
        causal = col < row
        heads = [slice(h * DH, (h + 1) * DH) for h in range(NH)]

        def q_loop(qb, _):
            rows = pl.ds(pl.multiple_of(qb * QB, QB), QB)
            qs = [q_ref[rows, hd] for hd in heads]

            def tile(kb, carry, valid):
                krows = pl.ds(pl.multiple_of(kb * QB, QB), QB)
                lbk = [_sb_tile(cfg, q, k_ref[krows, hd], valid) for q, hd in zip(qs, heads)]
                rems = [_dot_split(lk, u_after) for _, lk in lbk]
                aa = [_masked(valid, jnp.exp(lb + rem + c)).astype(BF16) for (lb, _), rem, (_, c) in zip(lbk, rems, carry)]
                return tuple((o_acc + _dot(a, v_ref[krows, hd]), c + jnp.sum(lk, axis=1, keepdims=True))
                             for a, hd, (_, lk), (o_acc, c) in zip(aa, heads, lbk, carry))

            carry = tile(qb, ((jnp.zeros((QB, DH), F32), jnp.zeros((QB, 1), F32)),) * NH, causal)
            carry = lax.fori_loop(0, qb, lambda i, cr: tile(qb - 1 - i, cr, None), carry)
            for hd, (o_acc, c) in zip(heads, carry):
                o_ref[rows, hd] = o_acc
                t_ref[rows, hd] = jnp.broadcast_to(c, (QB, DH))
            return 0

        lax.fori_loop(0, S // QB, q_loop, 0)

    def spec(i):
        return pl.BlockSpec((None, S, NH * DH), lambda h: (i, 0, h))

    return pl.pallas_call(
        body, name="sb_fwd", grid=(cfg.HSB // NH,),
        in_specs=[spec(0), spec(1), spec(2)],
        out_specs=[pl.BlockSpec((S, NH * DH), lambda h: (0, h))] * 2,
        out_shape=[jax.ShapeDtypeStruct((S, cfg.DSB), F32)] * 2,
        compiler_params=_cp(cfg, ("parallel",)),
    )(qkv3, qkv3, qkv3)


def sb_bwd(cfg, qkv3, do_sb, tsum):
    S, QB, DH, NH = cfg.S, cfg.SBT, cfg.DH, cfg.SBHB
    scale = DH ** -0.5

    def body(q_ref, k_ref, v_ref, do_ref, t_ref, d_ref, dk_acc, dv_acc):
        dk_acc[...] = jnp.zeros_like(dk_acc)
        dv_acc[...] = jnp.zeros_like(dv_acc)
        row = lax.broadcasted_iota(jnp.int32, (QB, QB), 0)
        col = lax.broadcasted_iota(jnp.int32, (QB, QB), 1)
        u_upto = (row <= col).astype(BF16)
        u_before = (row < col).astype(BF16)
        causal = col < row
        heads = [slice(h * DH, (h + 1) * DH) for h in range(NH)]

        def q_loop(qb, _):
            rows = pl.ds(pl.multiple_of(qb * QB, QB), QB)
            qs = [q_ref[rows, hd] for hd in heads]
            dos = [do_ref[rows, hd] for hd in heads]
            totals = [t_ref[rows, hd.start:hd.start + 1] for hd in heads]

            def tile(kb, carry, valid):
                krows = pl.ds(pl.multiple_of(kb * QB, QB), QB)
                ks = [k_ref[krows, hd] for hd in heads]
                lbk = [_sb_tile(cfg, q, k, valid) for q, k in zip(qs, ks)]
                das = [_dot_nt(do, v_ref[krows, hd]) for do, hd in zip(dos, heads)]
                pins = [_dot_split(lk, u_upto) for _, lk in lbk]
                aa = [_masked(valid, jnp.exp(lb + (tot - pc - pin)))
                      for (lb, _), tot, (_, pc, _), pin in zip(lbk, totals, carry, pins)]
                gs = [a * da for a, da in zip(aa, das)]
                for a, do, hd in zip(aa, dos, heads):
                    dv_acc[krows, hd] += _dot_tn(a.astype(BF16), do)
                cums = [gc + _dot(g.astype(BF16), u_before) for g, (_, _, gc) in zip(gs, carry)]
                dzs = [(_masked(valid, g - jnp.exp(lb) * (g + cum)) * scale).astype(BF16)
                       for g, (lb, _), cum in zip(gs, lbk, cums)]
                for dz, q, hd in zip(dzs, qs, heads):
                    dk_acc[krows, hd] += _dot_tn(dz, q)
                return tuple((dq + _dot(dz, k), pc + jnp.sum(lk, axis=1, keepdims=True), gc + jnp.sum(g, axis=1, keepdims=True))
                             for dz, k, (_, lk), g, (dq, pc, gc) in zip(dzs, ks, lbk, gs, carry))

            z1 = jnp.zeros((QB, 1), F32)
            carry = lax.fori_loop(0, qb, lambda kb, cr: tile(kb, cr, None), ((jnp.zeros((QB, DH), F32), z1, z1),) * NH)
            for hd, (dq_acc, _, _) in zip(heads, tile(qb, carry, causal)):
                d_ref[0, rows, hd] = dq_acc.astype(BF16)
            return 0

        lax.fori_loop(0, S // QB, q_loop, 0)
        d_ref[1, :, :] = dk_acc[...].astype(BF16)
        d_ref[2, :, :] = dv_acc[...].astype(BF16)

    def spec(i):
        return pl.BlockSpec((None, S, NH * DH), lambda h: (i, 0, h))

    hd_spec = pl.BlockSpec((S, NH * DH), lambda h: (0, h))
    return pl.pallas_call(
        body, name="sb_bwd", grid=(cfg.HSB // NH,),
        in_specs=[spec(0), spec(1), spec(2), hd_spec, hd_spec],
        out_specs=pl.BlockSpec((3, S, NH * DH), lambda h: (0, 0, h)),
        out_shape=jax.ShapeDtypeStruct((6, S, cfg.DSB), BF16),
        scratch_shapes=[pltpu.VMEM((S, NH * DH), F32), pltpu.VMEM((S, NH * DH), F32)],
        compiler_params=_cp(cfg, ("parallel",)),
    )(qkv3, qkv3, qkv3, do_sb, tsum)


def _band_mask(cfg, n, n_back):
    QB = cfg.QB
    qi = lax.broadcasted_iota(jnp.int32, (QB, 2 * QB), 0)
    kj = lax.broadcasted_iota(jnp.int32, (QB, 2 * QB), 1)
    dist = QB + qi - kj
    return (dist >= 0) & (dist <= n_back) & jnp.logical_or(n > 0, kj >= QB)


def _sub_rows(start, n, dil):
    if dil > 1:
        return pl.ds(start, n, stride=dil)
    return pl.ds(start if isinstance(start, int) else pl.multiple_of(start, 8), n)


def _stage_residues(cfg, dil, pairs):
    QB, L = cfg.QB, cfg.S // dil
    for src, dst in pairs:
        for r in range(dil):
            dst[pl.ds(r * (QB + L), QB), :] = jnp.zeros((QB, cfg.DH), BF16)
            dst[pl.ds(r * (QB + L) + QB, L), :] = src[_sub_rows(r, L, dil), :].astype(BF16)


def _staged_rows(cfg):
    return cfg.S + cfg.QB * max(d for _, d in cfg.branches)


def _lane_value(x):
    return jnp.max(x, axis=1, keepdims=True)


def dil_fwd(cfg, qkv3):
    S, QB, DH = cfg.S, cfg.QB, cfg.DH
    scale = DH ** -0.5
    nb = len(cfg.branches)
    mix_rows = min(256, S)

    def body(q_ref, k_ref, v_ref, o_ref, lt_ref, qf, kf, vf, kp, vp, *obl):
        obs, lbs = obl[:nb], obl[nb:]
        qf[...] = q_ref[...].astype(F32)
        kf[...] = k_ref[...].astype(F32)
        vf[...] = v_ref[...].astype(F32)
        for b, (window, dil) in enumerate(cfg.branches):
            L, n_back = S // dil, window // dil
            assert n_back <= QB and L % QB == 0
            _stage_residues(cfg, dil, [(kf, kp), (vf, vp)])
            for r in range(dil):
                for n in range(L // QB):
                    rows = _sub_rows(r + n * (QB * dil), QB, dil)
                    band = pl.ds(r * (QB + L) + n * QB, 2 * QB)
                    s = _dot_nt(qf[rows, :].astype(BF16), kp[band, :]) * scale
                    s = jnp.where(_band_mask(cfg, n, n_back), s, NEG_BIG)
                    m = jnp.max(s, axis=1, keepdims=True)
                    p = jnp.exp(s - m)
                    den = jnp.sum(p, axis=1, keepdims=True)
                    obs[b][rows, :] = _dot(p.astype(BF16), vp[band, :]) / den
                    lbs[b][rows, :] = jnp.broadcast_to(m + jnp.log(den), (QB, DH))

        def mix(i, _):
            rows = pl.ds(pl.multiple_of(i * mix_rows, mix_rows), mix_rows)
            ls = [r[rows, :] for r in lbs]
            m = functools.reduce(jnp.maximum, ls)
            es = [jnp.exp(l - m) for l in ls]
            tot = functools.reduce(jnp.add, es)
            o_ref[rows, :] = functools.reduce(jnp.add, [(e / tot) * r[rows, :] for e, r in zip(es, obs)])
            lt_ref[rows, :] = m + jnp.log(tot)
            return 0

        lax.fori_loop(0, S // mix_rows, mix, 0)

    def spec(i):
        return pl.BlockSpec((None, S, DH), lambda h: (i, 0, h))

    o_spec = pl.BlockSpec((S, DH), lambda h: (0, h))
    return pl.pallas_call(
        body, name="dil_fwd", grid=(cfg.HDL,),
        in_specs=[spec(3), spec(4), spec(5)], out_specs=[o_spec, o_spec],
        out_shape=[jax.ShapeDtypeStruct((S, cfg.DDL), F32)] * 2,
        scratch_shapes=[pltpu.VMEM((S, DH), F32)] * 3 + [pltpu.VMEM((_staged_rows(cfg), DH), BF16)] * 2
        + [pltpu.VMEM((S, DH), F32)] * (2 * nb),
        compiler_params=_cp(cfg, ("parallel",)),
    )(qkv3, qkv3, qkv3)


def dil_bwd(cfg, qkv3, do_dl, delta, lse_tot, cos2, sin2, d_sb3):
    S, QB, DH = cfg.S, cfg.QB, cfg.DH
    scale = DH ** -0.5
    out_rows = min(256, S)
    GROUP = 4

    def body(q_ref, k_ref, v_ref, do_ref, dl_ref, lt_ref, c_ref, s_ref, base_ref, d_ref,
             qf, kf, vf, dof, kp, vp, dkp, dvp, dqn, dkn, dvn):
        qf[...] = q_ref[...].astype(F32)
        kf[...] = k_ref[...].astype(F32)
        vf[...] = v_ref[...].astype(F32)
        dof[...] = do_ref[...].astype(F32)
        for acc in (dqn, dkn, dvn):
            acc[...] = jnp.zeros_like(acc)
        for window, dil in cfg.branches:
            L, n_back = S // dil, window // dil
            reg = QB + L
            _stage_residues(cfg, dil, [(kf, kp), (vf, vp)])
            dkp[pl.ds(0, dil * reg), :] = jnp.zeros((dil * reg, DH), F32)
            dvp[pl.ds(0, dil * reg), :] = jnp.zeros((dil * reg, DH), F32)
            blocks = [(r, n) for n in range(L // QB) for r in range(dil)]
            for g0 in range(0, len(blocks), GROUP):
                grp = blocks[g0:g0 + GROUP]
                rows = [_sub_rows(r + n * (QB * dil), QB, dil) for r, n in grp]
                bands = [pl.ds(r * reg + n * QB, 2 * QB) for r, n in grp]
                qs = [qf[rw, :].astype(BF16) for rw in rows]
                dos = [dof[rw, :].astype(BF16) for rw in rows]
                kbs = [kp[bd, :] for bd in bands]
                ss = [_dot_nt(q, kb) * scale for q, kb in zip(qs, kbs)]
                dps = [_dot_nt(do, vp[bd, :]) for do, bd in zip(dos, bands)]
                ps = [jnp.exp(jnp.where(_band_mask(cfg, n, n_back), s, NEG_BIG) - _lane_value(lt_ref[rw, :]))
                      for s, rw, (_, n) in zip(ss, rows, grp)]
                dss = [(p * (dp - _lane_value(dl_ref[rw, :])) * scale).astype(BF16) for p, dp, rw in zip(ps, dps, rows)]
                for rw, bd, ds, p, q, do, kb in zip(rows, bands, dss, ps, qs, dos, kbs):
                    dqn[rw, :] += _dot(ds, kb)
                    dkp[bd, :] += _dot_tn(ds, q)
                    dvp[bd, :] += _dot_tn(p.astype(BF16), do)
            for r in range(dil):
                sub = _sub_rows(r, L, dil)
                dkn[sub, :] += dkp[pl.ds(r * reg + QB, L), :]
                dvn[sub, :] += dvp[pl.ds(r * reg + QB, L), :]

        def finish(i, _):
            rows = pl.ds(pl.multiple_of(i * out_rows, out_rows), out_rows)
            c, sn = c_ref[rows, :], s_ref[rows, :]
            for j, acc in enumerate((dqn, dkn)):
                d = acc[rows, :]
                d_ref[j, rows, :] = (d * c + pltpu.roll(d * sn, DH // 2, 1)).astype(BF16)
            d_ref[2, rows, :] = dvn[rows, :].astype(BF16)
            return 0

        lax.fori_loop(0, S // out_rows, finish, 0)

    def spec(i):
        return pl.BlockSpec((None, S, DH), lambda h: (i, 0, h))

    hd = pl.BlockSpec((S, DH), lambda h: (0, h))
    tab = pl.BlockSpec((S, DH), lambda h: (0, 0))
    ns = _staged_rows(cfg)
    return pl.pallas_call(
        body, name="dil_bwd", grid=(cfg.HDL,),
        in_specs=[spec(3), spec(4), spec(5), hd, hd, hd, tab, tab, ANY],
        out_specs=pl.BlockSpec((3, S, DH), lambda h: (1, 0, h)),
        out_shape=jax.ShapeDtypeStruct((6, S, cfg.DDL), BF16),
        input_output_aliases={8: 0},
        scratch_shapes=[pltpu.VMEM((S, DH), F32)] * 4 + [pltpu.VMEM((ns, DH), BF16)] * 2
        + [pltpu.VMEM((ns, DH), F32)] * 2 + [pltpu.VMEM((S, DH), F32)] * 3,
        compiler_params=_cp(cfg, ("parallel",)),
    )(qkv3, qkv3, qkv3, do_dl, delta, lse_tot, cos2, sin2, d_sb3)


def combine_fwd(cfg, o_sb, o_dl, g_sb, g_dl):
    S, DH = cfg.S, cfg.DH

    def head_norm(o, g):
        return o * lax.rsqrt(jnp.mean(o * o, axis=-1, keepdims=True) + RMS_EPS) * g

    def body(osb_ref, odl_ref, gsb_ref, gdl_ref, mix_ref):
        for h in range(cfg.HSB):
            c = slice(h * DH, (h + 1) * DH)
            mix_ref[:, c] = head_norm(osb_ref[:, c], gsb_ref[:, c]).astype(BF16)
        for h in range(cfg.HDL):
            c = slice(h * DH, (h + 1) * DH)
            mix_ref[:, cfg.DSB + h * DH:cfg.DSB + (h + 1) * DH] = head_norm(odl_ref[:, c], gdl_ref[:, c]).astype(BF16)

    return pl.pallas_call(
        body, name="combine_fwd", grid=(S // cfg.TM,),
        in_specs=[_row(cfg, cfg.DSB), _row(cfg, cfg.DDL), _vec(cfg.DSB), _vec(cfg.DDL)],
        out_specs=_row(cfg, cfg.DMIX), out_shape=jax.ShapeDtypeStruct((S, cfg.DMIX), BF16),
        compiler_params=_cp(cfg, ("parallel",)),
    )(o_sb, o_dl, g_sb, g_dl)


def combine_bwd(cfg, dmix, o_sb, o_dl, g_sb, g_dl):
    S, DH = cfg.S, cfg.DH

    def body(dm_ref, osb_ref, odl_ref, gsb_ref, gdl_ref, dsb_ref, ddl_ref, dl_ref, dgsb_ref, dgdl_ref):
        @pl.when(pl.program_id(0) == 0)
        def _():
            dgsb_ref[...] = jnp.zeros_like(dgsb_ref)
            dgdl_ref[...] = jnp.zeros_like(dgdl_ref)

        for h in range(cfg.HSB):
            c = slice(h * DH, (h + 1) * DH)
            dx, dgx = _rms_bwd(dm_ref[:, c], osb_ref[:, c], gsb_ref[:, c])
            dsb_ref[:, c] = dx.astype(BF16)
            dgsb_ref[:, c] += jnp.sum(dgx, axis=0, keepdims=True)
        for h in range(cfg.HDL):
            c = slice(h * DH, (h + 1) * DH)
            o = odl_ref[:, c]
            dx, dgx = _rms_bwd(dm_ref[:, cfg.DSB + h * DH:cfg.DSB + (h + 1) * DH], o, gdl_ref[:, c])
            ddl_ref[:, c] = dx.astype(BF16)
            dl_ref[:, c] = jnp.broadcast_to(jnp.sum(dx * o, axis=-1, keepdims=True), dx.shape)
            dgdl_ref[:, c] += jnp.sum(dgx, axis=0, keepdims=True)

    return pl.pallas_call(
        body, name="combine_bwd", grid=(S // cfg.TM,),
        in_specs=[_row(cfg, cfg.DMIX), _row(cfg, cfg.DSB), _row(cfg, cfg.DDL), _vec(cfg.DSB), _vec(cfg.DDL)],
        out_specs=[_row(cfg, cfg.DSB), _row(cfg, cfg.DDL), _row(cfg, cfg.DDL), _vec(cfg.DSB), _vec(cfg.DDL)],
        out_shape=[jax.ShapeDtypeStruct((S, cfg.DSB), BF16), jax.ShapeDtypeStruct((S, cfg.DDL), BF16),
                   jax.ShapeDtypeStruct((S, cfg.DDL), F32), jax.ShapeDtypeStruct((1, cfg.DSB), F32),
                   jax.ShapeDtypeStruct((1, cfg.DDL), F32)],
        compiler_params=_cp(cfg, ("arbitrary",)),
    )(dmix, o_sb, o_dl, g_sb, g_dl)


SUB = 8


def _shift_down(u, prev, j):
    rolled = pltpu.roll(u, j, 0)
    row = lax.broadcasted_iota(jnp.int32, (SUB, u.shape[1]), 0)
    head = jnp.where(row >= j, rolled[:SUB], pltpu.roll(prev, j, 0))
    return jnp.concatenate([head, rolled[SUB:]], axis=0)


def _shift_up(u, nxt, j):
    n = u.shape[0]
    rolled = pltpu.roll(u, n - j, 0)
    row = lax.broadcasted_iota(jnp.int32, (SUB, u.shape[1]), 0)
    tail = jnp.where(row < SUB - j, rolled[n - SUB:], pltpu.roll(nxt, SUB - j, 0))
    return jnp.concatenate([rolled[:n - SUB], tail], axis=0)


def _conv(u, s1, s2, cw, cb):
    return u * cw[2:3, :] + s1 * cw[1:2, :] + s2 * cw[0:1, :] + cb


def _chunk_rows(cfg):
    ch = min(cfg.FCH, cfg.S)
    return ch, cfg.S // ch


def ffn_fwd(cfg, h2, w_up, conv_w, conv_b):
    S, D = h2.shape
    tn, nt = cfg.TNF, cfg.FFP // cfg.TNF
    ch, nch = _chunk_rows(cfg)

    def body(h_ref, wg_ref, wv_ref, cwg_ref, cwv_ref, cbg_ref, cbv_ref, u_ref, y_ref):
        prev = [jnp.zeros((SUB, tn), F32)] * 2
        pending = None
        for ci in range(nch + 1):
            if ci < nch:
                h = h_ref[pl.ds(ci * ch, ch), :]
                us_next = [_dot_nt(h, wg_ref[...]), _dot_nt(h, wv_ref[...])]
            if pending is not None:
                rows, us = pending
                cs = []
                for i, (cw_ref, cb_ref) in enumerate(((cwg_ref, cbg_ref), (cwv_ref, cbv_ref))):
                    u_ref[i, rows, :] = us[i]
                    cs.append(_conv(us[i], _shift_down(us[i], prev[i], 1), _shift_down(us[i], prev[i], 2),
                                    cw_ref[...], cb_ref[...]))
                y_ref[rows, :] = (_gelu(cs[0])[0] * cs[1]).astype(BF16)
                prev = [u[ch - SUB:] for u in us]
            pending = (pl.ds(ci * ch, ch), us_next) if ci < nch else None

    return pl.pallas_call(
        body, name="ffn_fwd", grid=(nt,),
        in_specs=[pl.BlockSpec((S, D), lambda n: (0, 0)),
                  pl.BlockSpec((tn, D), lambda n: (n, 0)), pl.BlockSpec((tn, D), lambda n: (n + nt, 0)),
                  pl.BlockSpec((3, tn), lambda n: (0, n)), pl.BlockSpec((3, tn), lambda n: (0, n + nt)),
                  pl.BlockSpec((1, tn), lambda n: (0, n)), pl.BlockSpec((1, tn), lambda n: (0, n + nt))],
        out_specs=[pl.BlockSpec((2, S, tn), lambda n: (0, 0, n)), pl.BlockSpec((S, tn), lambda n: (0, n))],
        out_shape=[jax.ShapeDtypeStruct((2, S, cfg.FFP), F32), jax.ShapeDtypeStruct((S, cfg.FFP), BF16)],
        compiler_params=_cp(cfg, ("parallel",)),
    )(h2, w_up, w_up, conv_w, conv_w, conv_b, conv_b)


def ffn_bwd(cfg, df, h2, w_down, u, conv_w, conv_b):
    S, D = df.shape
    tn, nt = cfg.TNF, cfg.FFP // cfg.TNF

    ch, nch = _chunk_rows(cfg)

    def body(df_ref, h_ref, wd_ref, u_ref, cwg_ref, cwv_ref, cbg_ref, cbv_ref,
             du_ref, dwd_ref, dwu_ref, dcw_ref, dcb_ref):
        cws = (cwg_ref[...], cwv_ref[...])
        cbs = (cbg_ref[...], cbv_ref[...])
        zero = jnp.zeros((SUB, tn), F32)
        nxt = [zero, zero]
        dws = [[jnp.zeros((1, tn), F32)] * 4 for _ in range(2)]
        dwd = jnp.zeros((tn, D), F32)
        dwu = [jnp.zeros((tn, D), F32)] * 2
        order = list(reversed(range(nch)))
        dys, done = {}, {}
        for step in range(nch + 2):
            if step < nch:
                ci = order[step]
                dys[ci] = _dot_nt(df_ref[pl.ds(ci * ch, ch), :], wd_ref[...])
            if 1 <= step <= nch:
                ci = order[step - 1]
                rows = pl.ds(ci * ch, ch)
                dy = dys.pop(ci)
                us, s1, s2, cs = [], [], [], []
                for i in range(2):
                    u = u_ref[i, rows, :]
                    prev = u_ref[i, pl.ds(ci * ch - SUB, SUB), :] if ci else zero
                    us.append(u)
                    s1.append(_shift_down(u, prev, 1))
                    s2.append(_shift_down(u, prev, 2))
                    cs.append(_conv(u, s1[i], s2[i], cws[i], cbs[i]))
                gl, t = _gelu(cs[0])
                dcs = (dy * cs[1] * _gelu_grad(cs[0], t), dy * gl)
                dus = []
                for i, dc in enumerate(dcs):
                    du = dc * cws[i][2:3, :] + _shift_up(dc, nxt[i], 1) * cws[i][1:2, :] + _shift_up(dc, nxt[i], 2) * cws[i][0:1, :]
                    dus.append(du.astype(BF16))
                    du_ref[i, rows, :] = dus[i]
                    for j, tap in enumerate((s2[i], s1[i], us[i])):
                        dws[i][j] = dws[i][j] + jnp.sum(dc * tap, axis=0, keepdims=True)
                    dws[i][3] = dws[i][3] + jnp.sum(dc, axis=0, keepdims=True)
                nxt = [dc[:SUB] for dc in dcs]
                done[ci] = ((gl * cs[1]).astype(BF16), dus)
            if step >= 2:
                ci = order[step - 2]
                rows = pl.ds(ci * ch, ch)
                yv, dus = done.pop(ci)
                dwd = dwd + _dot_tn(yv, df_ref[rows, :])
                hv = h_ref[rows, :]
                dwu = [acc + _dot_tn(du, hv) for acc, du in zip(dwu, dus)]
        dwd_ref[...] = dwd.astype(BF16)
        for i in range(2):
            dwu_ref[i] = dwu[i].astype(BF16)
            for j in range(3):
                dcw_ref[i, j:j + 1, :] = dws[i][j]
            dcb_ref[i] = dws[i][3]

    whole = pl.BlockSpec((S, D), lambda n: (0, 0), pipeline_mode=pl.Buffered(1))
    du, dwd, dwu, dcw, dcb = pl.pallas_call(
        body, name="ffn_bwd", grid=(nt,),
        in_specs=[whole, whole, pl.BlockSpec((tn, D), lambda n: (n, 0)),
                  pl.BlockSpec((2, S, tn), lambda n: (0, 0, n)),
                  pl.BlockSpec((3, tn), lambda n: (0, n)), pl.BlockSpec((3, tn), lambda n: (0, n + nt)),
                  pl.BlockSpec((1, tn), lambda n: (0, n)), pl.BlockSpec((1, tn), lambda n: (0, n + nt))],
        out_specs=[pl.BlockSpec((2, S, tn), lambda n: (0, 0, n)), pl.BlockSpec((tn, D), lambda n: (n, 0)),
                   pl.BlockSpec((2, tn, D), lambda n: (0, n, 0)),
                   pl.BlockSpec((2, 3, tn), lambda n: (0, 0, n)), pl.BlockSpec((2, 1, tn), lambda n: (0, 0, n))],
        out_shape=[jax.ShapeDtypeStruct((2, S, cfg.FFP), BF16), jax.ShapeDtypeStruct((cfg.FFP, D), BF16),
                   jax.ShapeDtypeStruct((2, cfg.FFP, D), BF16),
                   jax.ShapeDtypeStruct((2, 3, cfg.FFP), F32), jax.ShapeDtypeStruct((2, 1, cfg.FFP), F32)],
        compiler_params=_cp(cfg, ("parallel",)),
    )(df, h2, w_down, u, conv_w, conv_w, conv_b, conv_b)
    return du, dwd, dwu.reshape(cfg.FF2P, D), dcw, dcb


def rope_tables(cfg):
    inv_freq = ROPE_THETA ** (-jnp.arange(0, cfg.DH, 2, dtype=F32) / cfg.DH)
    ang = jnp.arange(cfg.S, dtype=F32)[:, None] * inv_freq[None, :]
    cos, sin = jnp.cos(ang), jnp.sin(ang)
    return jnp.concatenate([cos, cos], axis=1), jnp.concatenate([-sin, sin], axis=1)


class LocalWeights:
    def __init__(self, w_in, w_out, w_up, conv_w, w_down):
        self.w = (w_in, w_out, w_up, conv_w, w_down)
        self.grads = {}

    def first_start(self):
        return None

    def weights_first(self, after):
        return self.w[0], self.w[3]

    def start_rest(self, after):
        return None

    def weights_rest(self, group, after):
        return ((self.w[1],), None) if group == 0 else ((self.w[4],), None)

    def forwarded(self, group, after):
        return (self.w[2],)

    def pair_start(self, grads):
        self.grads.update(grads)
        return None

    def reduce_start(self, grads, after=None):
        self.grads.update(grads)
        return None

    def reduce_wait(self, names, after):
        pass


def _after(a, token):
    return a if token is None else a + token[0, 0].astype(a.dtype)


def local_step(cfg, comm, x, target, g1, g2, g3, g4, g_sb, g_dl, conv_b):
    S, D = cfg.S, cfg.D
    cos2, sin2 = rope_tables(cfg)
    full = lambda r, c: pl.BlockSpec((r, c), lambda j, k: (0, 0))

    h1 = rms_fwd(cfg, x, _after(g1, comm.first_start()))
    w_in, conv_w = comm.weights_first(after=h1)
    qkv3 = qkv_proj(cfg, h1, w_in, _after(cos2, comm.start_rest(after=w_in)), sin2)
    o_sb, tsum = sb_fwd(cfg, qkv3)
    o_dl, lse_tot = dil_fwd(cfg, qkv3)
    mixed = combine_fwd(cfg, o_sb, o_dl, g_sb, g_dl)
    (w_out,), token = comm.weights_rest(0, after=mixed)
    tn = cfg.TN
    mo = _mm(cfg, "mix_out", mixed, w_out, nt=False, grid=(D // tn, 1),
             a_spec=full(S, cfg.DMIX), b_spec=pl.BlockSpec((cfg.DMIX, tn), lambda j, k: (0, j)),
             o_spec=pl.BlockSpec((S, tn), lambda j, k: (0, j)),
             out_shape=jax.ShapeDtypeStruct((S, D), F32), acc_shape=(8, LANE), dep=token)
    x1, h2 = mid_fwd(cfg, x, mo, g2, g3)
    w_up, = comm.forwarded(0, after=h2)
    u, y = ffn_fwd(cfg, h2, w_up, conv_w, conv_b)
    (w_down,), _ = comm.weights_rest(1, after=y)
    tk = cfg.FFP // 2
    f = _mm(cfg, "ffn_down", y, w_down, nt=False, grid=(D // tn, cfg.FFP // tk),
            a_spec=pl.BlockSpec((S, tk), lambda j, k: (0, k)), b_spec=pl.BlockSpec((tk, tn), lambda j, k: (k, j)),
            o_spec=pl.BlockSpec((S, tn), lambda j, k: (0, j)),
            out_shape=jax.ShapeDtypeStruct((S, D), F32), acc_shape=(S, tn))
    dout, df, dg4, loss = final_fwd_bwd(cfg, x1, f, g4, target)

    du, dw_down, dw_up, dconv_w, dconv_b = ffn_bwd(cfg, df, h2, w_down, u, conv_w, conv_b)
    kt = cfg.FFP // tk
    dh2 = _mm(cfg, "d_h2", du, w_up, nt=False, grid=(D // tn, 2 * kt),
              a_spec=pl.BlockSpec((None, S, tk), lambda j, k: (k // kt, 0, k % kt)),
              b_spec=pl.BlockSpec((tk, tn), lambda j, k: (k, j)),
              o_spec=pl.BlockSpec((S, tn), lambda j, k: (0, j)),
              out_shape=jax.ShapeDtypeStruct((S, D), F32), acc_shape=(S, tn),
              dep=comm.pair_start(dict(w_down=dw_down, w_up=dw_up)))
    token = comm.reduce_start({}, after=dh2)
    dx1, dmo, dg3, dg2 = mid_bwd(cfg, dh2, x1, _after(g3, token), dout, mo, g2)

    dmix = _mm(cfg, "d_mixed", dmo, w_out, nt=True, grid=(cfg.DMIX // tn, 1),
               a_spec=full(S, D), b_spec=pl.BlockSpec((tn, D), lambda j, k: (j, 0)),
               o_spec=pl.BlockSpec((S, tn), lambda j, k: (0, j)),
               out_shape=jax.ShapeDtypeStruct((S, cfg.DMIX), F32), acc_shape=(8, LANE))
    dw_out = _mm_tn(cfg, "d_w_out", mixed, dmo, grid=(D // tn,),
                    a_spec=pl.BlockSpec((S, cfg.DMIX), lambda j: (0, 0)),
                    b_spec=pl.BlockSpec((S, tn), lambda j: (0, j)),
                    o_spec=pl.BlockSpec((cfg.DMIX, tn), lambda j: (0, j)),
                    out_shape=jax.ShapeDtypeStruct((cfg.DMIX, D), BF16))
    token = comm.reduce_start(dict(w_out=dw_out))
    do_sb, do_dl, delta, dg_sb, dg_dl = combine_bwd(cfg, dmix, o_sb, o_dl, _after(g_sb, token), g_dl)
    d_sb3 = sb_bwd(cfg, qkv3, do_sb, tsum)
    dqkv3 = dil_bwd(cfg, qkv3, do_dl, delta, lse_tot, cos2, sin2, d_sb3)
    comm.reduce_wait(("w_up", "w_down"), after=dqkv3)
    comm.reduce_wait(("w_out",), after=dqkv3)
    tkq = min(tn, cfg.DSB)
    kq = cfg.DSB // tkq
    dw_in = _mm_tn(cfg, "d_w_in", h1, dqkv3, grid=(6 * kq,),
                   a_spec=pl.BlockSpec((S, D), lambda j: (0, 0)),
                   b_spec=pl.BlockSpec((None, S, tkq), lambda j: (j // kq, 0, j % kq)),
                   o_spec=pl.BlockSpec((D, tkq), lambda j: (0, j)),
                   out_shape=jax.ShapeDtypeStruct((D, 6 * cfg.DSB), BF16))
    token = comm.reduce_start(dict(w_in=dw_in))
    dh1 = _mm(cfg, "d_h1", dqkv3, w_in, nt=True, grid=(D // tn, 6),
              a_spec=pl.BlockSpec((None, S, cfg.DSB), lambda j, k: (k, 0, 0)),
              b_spec=pl.BlockSpec((tn, cfg.DSB), lambda j, k: (j, k)),
              o_spec=pl.BlockSpec((S, tn), lambda j, k: (0, j)),
              out_shape=jax.ShapeDtypeStruct((S, D), F32), acc_shape=(S, tn), dep=token)
    grad_x, dg1 = first_bwd(cfg, dh1, x, g1, dx1)
    small = dict(loss=loss, g1=dg1, g2=dg2, g3=dg3, g4=dg4, g_sb=dg_sb, g_dl=dg_dl,
                 conv_b=dconv_b.reshape(1, cfg.FF2P), conv_w=dconv_w.transpose(1, 0, 2).reshape(3, cfg.FF2P))
    return grad_x, small


ANY = pl.BlockSpec(memory_space=pl.ANY)


def _me():
    return lax.axis_index("x"), lax.axis_index("y"), lax.axis_index("c")


def _other_chips(x, y):
    return [(1 - x, y), (x, 1 - y), (1 - x, 1 - y)]


def pad_conv_w(cfg, conv_w, pos):
    r, c = conv_w.shape

    def body(pos_ref, w_ref, full_ref, scr, sem):
        scr[:, :c] = w_ref[...]
        scr[:, c:] = jnp.zeros((r, cfg.FSHP - c), F32)
        cols = pl.ds(pl.multiple_of(pos_ref[0] * cfg.FSHP, LANE), cfg.FSHP)
        cp = pltpu.make_async_copy(scr, full_ref.at[:, cols], sem)
        cp.start()
        cp.wait()

    return pl.pallas_call(
        body, name="pad_conv_w",
        grid_spec=pltpu.PrefetchScalarGridSpec(
            num_scalar_prefetch=1, grid=(1,), in_specs=[pl.BlockSpec((r, c), lambda i, p: (0, 0))], out_specs=ANY,
            scratch_shapes=[pltpu.VMEM((r, cfg.FSHP), F32), pltpu.SemaphoreType.DMA]),
        out_shape=jax.ShapeDtypeStruct(_full_shape(cfg, "conv_w"), F32),
    )(pos, conv_w)


def _tile2(r, c):
    return (256, c) if r % 256 == 0 else (r, 512 if c % 512 == 0 else c)


def cast_into(cfg, name, w, pos, dep=None):
    r, c = w.shape
    _, nr, _, nc = _slab(cfg, name, 0)
    tm, tc = _tile2(r, c)
    wr = nr if tm == r else tm
    assert nc == c and (nr == r or tm == r)
    gap = cfg.FSHP - cfg.FSH if name == "w_down" else 0
    deps = [] if dep is None else [dep]

    def body(pos_ref, w_ref, *rest):
        full_ref, token, scr, sem = rest[len(deps):]
        token[...] = jnp.zeros_like(token)
        tile = w_ref[...]
        if deps:
            tile = tile + rest[0][0:1, 0:1]
        scr[pl.ds(0, tm), :] = tile.astype(BF16)
        if wr > tm:
            scr[pl.ds(tm, wr - tm), :] = jnp.zeros((wr - tm, tc), BF16)
        r0, _, c0, _ = _slab(cfg, name, pos_ref[0])
        rows = pl.ds(pl.multiple_of(r0 + pl.program_id(0) * tm, 16), wr)
        cols = pl.ds(pl.multiple_of(c0 + pl.program_id(1) * tc, LANE), tc)
        cps = [pltpu.make_async_copy(scr.at[pl.ds(0, wr), :], full_ref.at[rows, cols], sem.at[0])]
        if gap:
            scr[pl.ds(wr, gap), :] = jnp.zeros((gap, tc), BF16)
            for h in range(2):
                pad_rows = pl.ds(h * cfg.FSHP + cfg.FSH, gap)
                cps.append(pltpu.make_async_copy(scr.at[pl.ds(wr, gap), :], full_ref.at[pad_rows, cols], sem.at[1 + h]))
        for cp in cps:
            cp.start()
        for cp in cps:
            cp.wait()

    return pl.pallas_call(
        body, name=f"cast_{name}",
        grid_spec=pltpu.PrefetchScalarGridSpec(
            num_scalar_prefetch=1, grid=(r // tm, c // tc),
            in_specs=[pl.BlockSpec((tm, tc), lambda i, j, p: (i, j))]
            + [pl.BlockSpec((8, LANE), lambda i, j, p: (0, 0))] * len(deps),
            out_specs=[ANY, pl.BlockSpec((8, LANE), lambda i, j, p: (0, 0))],
            scratch_shapes=[pltpu.VMEM((wr + gap, tc), BF16), pltpu.SemaphoreType.DMA((3,))]),
        out_shape=[jax.ShapeDtypeStruct(_full_shape(cfg, name), BF16), jax.ShapeDtypeStruct((8, LANE), F32)],
        compiler_params=_cp(cfg, ("arbitrary", "arbitrary")),
    )(pos, w, *deps)


HBM = pl.BlockSpec(memory_space=pltpu.HBM)
SEM = pl.BlockSpec(memory_space=pltpu.SEMAPHORE)
TOKEN = pl.BlockSpec(memory_space=pltpu.VMEM)
EFFECT = pltpu.SideEffectType.DATAFLOW_SIDE_EFFECTING


def _slab(cfg, name, k):
    D = cfg.D
    if name == "w_in":
        cin = 6 * cfg.DSB // N_CHIPS
        return 0, D, k * cin, cin
    if name == "w_out":
        rout = cfg.DMIX // N_CHIPS
        return k * rout, rout, 0, D
    if name == "w_up":
        return k * cfg.FSHP, cfg.FSHP, 0, D
    if name == "conv_w":
        return 0, 3, k * cfg.FSHP, cfg.FSHP
    rdn = cfg.FSH // 2
    return (k // 2) * cfg.FSHP + (k % 2) * rdn, rdn, 0, D


def _full_shape(cfg, name):
    return dict(w_in=(cfg.D, 6 * cfg.DSB), w_out=(cfg.DMIX, cfg.D), w_up=(cfg.FF2P, cfg.D), w_down=(cfg.FFP, cfg.D),
                conv_w=(3, cfg.FF2P))[name]


def _half(cfg, name, ref, k, h):
    r0, nr, c0, nc = _slab(cfg, name, k)
    if name == "conv_w":
        return ref.at[:, pl.ds(c0, nc)]
    return ref.at[pl.ds(r0 + h * (nr // 2), nr // 2), pl.ds(c0, nc)]


def _rows_half(ref, h):
    nr = ref.shape[0] // 2
    return ref.at[pl.ds(h * nr, nr), :]


def _remote(src, dst, send_sem, recv_sem, dev):
    return pltpu.make_async_remote_copy(src_ref=src, dst_ref=dst, send_sem=send_sem, recv_sem=recv_sem,
                                        device_id=dev, device_id_type=MESH)


REST = ("w_out", "w_up", "w_down")
FIRST = (("w_in", "conv_w"),)
GROUPS = (("w_out", "w_up"), ("w_down",))


def _hbm(a):
    return pltpu.with_memory_space_constraint(a, pltpu.HBM)


def gather_start(cfg, tag, groups, fulls, after):
    order = [k for names in groups for k in names]
    n, ng = len(order), len(groups)

    def body(*refs):
        lands = dict(zip(order, refs[:n]))
        sems = refs[n + 1:n + 1 + 2 * ng]
        token = refs[-1]
        x, y, c = _me()
        me = 2 * x + y
        for g, names in enumerate(groups):
            for i, name in enumerate(names):
                mine = _half(cfg, name, lands[name], me, c)
                for j, (px, py) in enumerate(_other_chips(x, y)):
                    _remote(mine, mine, sems[2 * g].at[3 * i + j], sems[2 * g + 1].at[3 * i + j], (px, py, c)).start()
        token[...] = jnp.zeros_like(token)

    ops = [_hbm(fulls[k]) for k in order]
    sem_shapes = [pltpu.SemaphoreType.DMA((3 * len(names),)) for names in groups for _ in range(2)]
    outs = pl.pallas_call(
        body, name=f"gather_start_{tag}",
        in_specs=[HBM] * n + [ANY],
        out_specs=[SEM] * (2 * ng) + [HBM] * n + [TOKEN],
        out_shape=sem_shapes + [pltpu.HBM(a.shape, a.dtype) for a in ops] + [jax.ShapeDtypeStruct((8, LANE), F32)],
        input_output_aliases={i: 2 * ng + i for i in range(n)},
        compiler_params=pltpu.CompilerParams(has_side_effects=EFFECT),
    )(*ops, after)
    thru = dict(zip(order, outs[2 * ng:2 * ng + n]))
    return [(outs[2 * g], outs[2 * g + 1], [thru[k] for k in names]) for g, names in enumerate(groups)], outs[-1]


def gather_wait(cfg, names, ssem, rsem, lands, after):
    n = len(names)

    def body(*refs):
        lands_ = refs[:n]
        ssem_, rsem_ = refs[n], refs[n + 1]
        x, y, c = _me()
        me = 2 * x + y
        for i, name in enumerate(names):
            for j, (px, py) in enumerate(_other_chips(x, y)):
                cp = _remote(_half(cfg, name, lands_[i], me, c), _half(cfg, name, lands_[i], 2 * px + py, c),
                             ssem_.at[3 * i + j], rsem_.at[3 * i + j], (px, py, c))
                cp.wait_send()
                cp.wait_recv()

    return pl.pallas_call(
        body, name="gather_wait_" + "_".join(names),
        in_specs=[HBM] * n + [SEM, SEM, ANY], out_specs=[HBM] * n,
        out_shape=[pltpu.HBM(a.shape, a.dtype) for a in lands],
        input_output_aliases={i: i for i in range(n)},
        compiler_params=pltpu.CompilerParams(has_side_effects=EFFECT),
    )(*lands, ssem, rsem, after)


def gather_finish(cfg, names, lands):
    n = len(names)

    def body(*refs):
        outs = refs[n:2 * n]
        ssem, rsem = refs[2 * n:]
        x, y, c = _me()
        sib = (x, y, 1 - c)
        fwds = []
        for i, name in enumerate(names):
            for j, (px, py) in enumerate(_other_chips(x, y)):
                landed = _half(cfg, name, outs[i], 2 * px + py, c)
                fwds.append(_remote(landed, landed, ssem.at[3 * i + j], rsem.at[3 * i + j], sib))
        for cp in fwds:
            cp.start()
        for i, name in enumerate(names):
            for j, (px, py) in enumerate(_other_chips(x, y)):
                passed = _half(cfg, name, outs[i], 2 * px + py, 1 - c)
                _remote(passed, passed, ssem.at[3 * i + j], rsem.at[3 * i + j], sib).wait_recv()
        for cp in fwds:
            cp.wait_send()

    return pl.pallas_call(
        body, name="gather_finish_" + "_".join(names), in_specs=[ANY] * n, out_specs=[ANY] * n,
        out_shape=[jax.ShapeDtypeStruct(a.shape, a.dtype) for a in lands],
        input_output_aliases={i: i for i in range(n)},
        scratch_shapes=[pltpu.SemaphoreType.DMA((3 * n,)), pltpu.SemaphoreType.DMA((3 * n,))],
    )(*lands)


def forward_start(cfg, names, lands, after):
    n = len(names)

    def body(*refs):
        outs = refs[:n]
        ssem, rsem = refs[n + 1], refs[n + 2]
        token = refs[-1]
        x, y, c = _me()
        for i, name in enumerate(names):
            for j, (px, py) in enumerate(_other_chips(x, y)):
                landed = _half(cfg, name, outs[i], 2 * px + py, c)
                _remote(landed, landed, ssem.at[3 * i + j], rsem.at[3 * i + j], (x, y, 1 - c)).start()
        token[...] = jnp.zeros_like(token)

    ops = [_hbm(a) for a in lands]
    outs = pl.pallas_call(
        body, name="forward_start_" + "_".join(names),
        in_specs=[HBM] * n + [ANY], out_specs=[SEM, SEM] + [HBM] * n + [TOKEN],
        out_shape=[pltpu.SemaphoreType.DMA((3 * n,)), pltpu.SemaphoreType.DMA((3 * n,))]
        + [pltpu.HBM(a.shape, a.dtype) for a in ops] + [jax.ShapeDtypeStruct((8, LANE), F32)],
        input_output_aliases={i: 2 + i for i in range(n)},
        compiler_params=pltpu.CompilerParams(has_side_effects=EFFECT),
    )(*ops, after)
    return outs[0], outs[1], outs[2:2 + n], outs[-1]


def forward_wait(cfg, names, ssem, rsem, lands, after):
    n = len(names)

    def body(*refs):
        outs = refs[:n]
        ssem_, rsem_ = refs[n], refs[n + 1]
        x, y, c = _me()
        for i, name in enumerate(names):
            for j, (px, py) in enumerate(_other_chips(x, y)):
                cp = _remote(_half(cfg, name, outs[i], 2 * px + py, c), _half(cfg, name, outs[i], 2 * px + py, 1 - c),
                             ssem_.at[3 * i + j], rsem_.at[3 * i + j], (x, y, 1 - c))
                cp.wait_send()
                cp.wait_recv()

    return pl.pallas_call(
        body, name="forward_wait_" + "_".join(names),
        in_specs=[HBM] * n + [SEM, SEM, ANY], out_specs=[HBM] * n,
        out_shape=[pltpu.HBM(a.shape, a.dtype) for a in lands],
        input_output_aliases={i: i for i in range(n)},
        compiler_params=pltpu.CompilerParams(has_side_effects=EFFECT),
    )(*lands, ssem, rsem, after)


def pair_send(cfg, grads):
    names = list(grads)
    n = len(names)

    def half_shape(name):
        _, nr, _, nc = _slab(cfg, name, 0)
        return (N_CHIPS, nr // 2, nc)

    def body(*refs):
        srcs, theirs = refs[:n], refs[n:2 * n]
        ssem, rsem = refs[2 * n:]
        x, y, c = _me()
        cps = []
        for i, name in enumerate(names):
            for k in range(N_CHIPS):
                cps.append(_remote(_half(cfg, name, srcs[i], k, 1 - c), theirs[i].at[k],
                                   ssem.at[N_CHIPS * i + k], rsem.at[N_CHIPS * i + k], (x, y, 1 - c)))
        for cp in cps:
            cp.start()
        for cp in cps:
            cp.wait()

    outs = pl.pallas_call(
        body, name="pair_send_" + "_".join(names), in_specs=[ANY] * n, out_specs=[ANY] * n,
        out_shape=[jax.ShapeDtypeStruct(half_shape(name), BF16) for name in names],
        scratch_shapes=[pltpu.SemaphoreType.DMA((N_CHIPS * n,))] * 2,
    )(*[grads[k] for k in names])
    return dict(zip(names, outs))


def pair_start(cfg, grads, after):
    names = list(grads)
    n = len(names)

    def body(*refs):
        srcs, theirs = refs[:n], refs[n:2 * n]
        ssem, rsem = refs[2 * n + 1], refs[2 * n + 2]
        token = refs[-1]
        x, y, c = _me()
        for i, name in enumerate(names):
            for k in range(N_CHIPS):
                _remote(_half(cfg, name, srcs[i], k, 1 - c), theirs[i].at[k],
                        ssem.at[N_CHIPS * i + k], rsem.at[N_CHIPS * i + k], (x, y, 1 - c)).start()
        token[...] = jnp.zeros_like(token)

    def half_shape(name):
        _, nr, _, nc = _slab(cfg, name, 0)
        return (N_CHIPS, nr // 2, nc)

    ops = [_hbm(grads[k]) for k in names] + [_hbm(lax.empty(half_shape(k), BF16)) for k in names]
    outs = pl.pallas_call(
        body, name="pair_start_" + "_".join(names),
        in_specs=[HBM] * (2 * n) + [ANY],
        out_specs=[SEM, SEM] + [HBM] * (2 * n) + [TOKEN],
        out_shape=[pltpu.SemaphoreType.DMA((N_CHIPS * n,)), pltpu.SemaphoreType.DMA((N_CHIPS * n,))]
        + [pltpu.HBM(a.shape, a.dtype) for a in ops] + [jax.ShapeDtypeStruct((8, LANE), F32)],
        input_output_aliases={i: 2 + i for i in range(2 * n)},
        compiler_params=pltpu.CompilerParams(has_side_effects=EFFECT),
    )(*ops, after)
    return outs[0], outs[1], dict(zip(names, outs[2:2 + n])), dict(zip(names, outs[2 + n:2 + 2 * n])), outs[-1]


def pair_wait(cfg, ssem, rsem, grads, theirs, after):
    names = list(grads)
    n = len(names)

    def body(*refs):
        srcs, theirs_ = refs[:n], refs[n:2 * n]
        ssem_, rsem_ = refs[2 * n], refs[2 * n + 1]
        x, y, c = _me()
        for i, name in enumerate(names):
            for k in range(N_CHIPS):
                cp = _remote(_half(cfg, name, srcs[i], k, 1 - c), theirs_[i].at[k],
                             ssem_.at[N_CHIPS * i + k], rsem_.at[N_CHIPS * i + k], (x, y, 1 - c))
                cp.wait_send()
                cp.wait_recv()

    ops = [grads[k] for k in names] + [theirs[k] for k in names]
    outs = pl.pallas_call(
        body, name="pair_wait_" + "_".join(names),
        in_specs=[HBM] * (2 * n) + [SEM, SEM, ANY], out_specs=[HBM] * (2 * n),
        out_shape=[pltpu.HBM(a.shape, a.dtype) for a in ops],
        input_output_aliases={i: i for i in range(2 * n)},
        compiler_params=pltpu.CompilerParams(has_side_effects=EFFECT),
    )(*ops, ssem, rsem, after)
    return dict(zip(names, outs[:n])), dict(zip(names, outs[n:]))


def pair_sum(cfg, name, grad, theirs, pos):
    _, r, c = theirs.shape
    tm, tc = _tile2(r, c)

    ni, nj = r // tm, c // tc
    total = N_CHIPS * ni * nj

    def body(pos_ref, g_ref, t_ref, o_ref, scr, sem):
        step = (pl.program_id(0) * ni + pl.program_id(1)) * nj + pl.program_id(2)

        def fetch(flat, slot):
            k, rem = flat // (ni * nj), flat % (ni * nj)
            r0, nr, c0, _ = _slab(cfg, name, k)
            rows = pl.ds(pl.multiple_of(r0 + pos_ref[1] * (nr // 2) + (rem // nj) * tm, 16), tm)
            cols = pl.ds(pl.multiple_of(c0 + (rem % nj) * tc, LANE), tc)
            return pltpu.make_async_copy(g_ref.at[rows, cols], scr.at[slot], sem.at[slot])

        @pl.when(step == 0)
        def _():
            fetch(0, 0).start()

        @pl.when(step + 1 < total)
        def _():
            fetch(step + 1, (step + 1) % 2).start()

        fetch(step, step % 2).wait()
        o_ref[...] = (scr[step % 2].astype(F32) + t_ref[...].astype(F32)).astype(BF16)

    blk = pl.BlockSpec((None, tm, tc), lambda k, i, j, p: (k, i, j))
    return pl.pallas_call(
        body, name=f"pair_sum_{name}",
        grid_spec=pltpu.PrefetchScalarGridSpec(
            num_scalar_prefetch=1, grid=(N_CHIPS, ni, nj), in_specs=[ANY, blk], out_specs=blk,
            scratch_shapes=[pltpu.VMEM((2, tm, tc), BF16), pltpu.SemaphoreType.DMA((2,))]),
        out_shape=jax.ShapeDtypeStruct(theirs.shape, BF16),
        compiler_params=_cp(cfg, ("arbitrary",) * 3),
    )(pos, grad, theirs)


def scatter_start(cfg, pres, after):
    names = list(pres)
    n = len(names)

    def body(*refs):
        srcs, lands = refs[:n], refs[n:2 * n]
        ssem, rsem = refs[2 * n + 1], refs[2 * n + 2]
        token = refs[-1]
        x, y, c = _me()
        for i in range(n):
            for j, (px, py) in enumerate(_other_chips(x, y)):
                _remote(srcs[i].at[2 * px + py], lands[i].at[j], ssem.at[3 * i + j], rsem.at[3 * i + j], (px, py, c)).start()
        token[...] = jnp.zeros_like(token)

    lands = [lax.empty((3,) + pres[k].shape[1:], BF16) for k in names]
    ops = [_hbm(a) for a in [pres[k] for k in names] + lands]
    outs = pl.pallas_call(
        body, name="scatter_start_" + "_".join(names),
        in_specs=[HBM] * (2 * n) + [ANY],
        out_specs=[SEM, SEM] + [HBM] * (2 * n) + [TOKEN],
        out_shape=[pltpu.SemaphoreType.DMA((3 * n,)), pltpu.SemaphoreType.DMA((3 * n,))]
        + [pltpu.HBM(a.shape, a.dtype) for a in ops] + [jax.ShapeDtypeStruct((8, LANE), F32)],
        input_output_aliases={i: 2 + i for i in range(2 * n)},
        compiler_params=pltpu.CompilerParams(has_side_effects=EFFECT),
    )(*ops, after)
    return outs[0], outs[1], dict(zip(names, outs[2:2 + n])), dict(zip(names, outs[2 + n:2 + 2 * n])), outs[-1]


def scatter_wait(cfg, ssem, rsem, pres, lands, after):
    names = list(pres)
    n = len(names)

    def body(*refs):
        srcs, lands_ = refs[:n], refs[n:2 * n]
        ssem_, rsem_ = refs[2 * n], refs[2 * n + 1]
        x, y, c = _me()
        for i in range(n):
            for j, (px, py) in enumerate(_other_chips(x, y)):
                cp = _remote(srcs[i].at[2 * px + py], lands_[i].at[j], ssem_.at[3 * i + j], rsem_.at[3 * i + j], (px, py, c))
                cp.wait_send()
                cp.wait_recv()

    ops = [pres[k] for k in names] + [lands[k] for k in names]
    outs = pl.pallas_call(
        body, name="scatter_wait_" + "_".join(names),
        in_specs=[HBM] * (2 * n) + [SEM, SEM, ANY], out_specs=[HBM] * (2 * n),
        out_shape=[pltpu.HBM(a.shape, a.dtype) for a in ops],
        input_output_aliases={i: i for i in range(2 * n)},
        compiler_params=pltpu.CompilerParams(has_side_effects=EFFECT),
    )(*ops, ssem, rsem, after)
    return dict(zip(names, outs[:n])), dict(zip(names, outs[n:]))


def sum_landed(cfg, name, pre, land, pos):
    _, r, c = pre.shape
    tm, tc = _tile2(r, c)
    nrt = r // tm

    def body(pos_ref, p_ref, l_ref, o_ref):
        acc = p_ref[...].astype(F32)
        for j in range(3):
            acc = acc + l_ref[j].astype(F32)
        o_ref[...] = acc

    return pl.pallas_call(
        body, name=f"sum_landed_{name}",
        grid_spec=pltpu.PrefetchScalarGridSpec(
            num_scalar_prefetch=1, grid=(nrt, c // tc),
            in_specs=[pl.BlockSpec((None, tm, tc), lambda i, j, p: (p[0], i, j)),
                      pl.BlockSpec((3, tm, tc), lambda i, j, p: (0, i, j))],
            out_specs=pl.BlockSpec((tm, tc), lambda i, j, p: (p[1] * nrt + i, j))),
        out_shape=jax.ShapeDtypeStruct((2 * r, c), F32), compiler_params=_cp(cfg, ("parallel", "parallel")),
    )(pos, pre, land)


def half_swap(cfg, sums):
    names = list(sums)
    n = len(names)

    def body(*refs):
        outs = refs[n:2 * n]
        ssem, rsem = refs[2 * n:]
        x, y, c = _me()
        cps = [_remote(_rows_half(outs[i], c), _rows_half(outs[i], c), ssem.at[i], rsem.at[i], (x, y, 1 - c))
               for i in range(n)]
        for cp in cps:
            cp.start()
        for i in range(n):
            theirs = _rows_half(outs[i], 1 - c)
            _remote(theirs, theirs, ssem.at[i], rsem.at[i], (x, y, 1 - c)).wait_recv()
        for cp in cps:
            cp.wait_send()

    outs = pl.pallas_call(
        body, name="half_swap_" + "_".join(names), in_specs=[ANY] * n, out_specs=[ANY] * n,
        out_shape=[jax.ShapeDtypeStruct(sums[k].shape, F32) for k in names],
        input_output_aliases={i: i for i in range(n)},
        scratch_shapes=[pltpu.SemaphoreType.DMA((n,))] * 2,
    )(*[sums[k] for k in names])
    return dict(zip(names, outs))


class MeshWeights:
    def __init__(self, cfg, w_sh):
        self.cfg = cfg
        self.pos = jnp.stack([2 * lax.axis_index("x") + lax.axis_index("y"), lax.axis_index("c")]).astype(jnp.int32)
        self.w_sh = w_sh
        self.full = {"w_in": cast_into(cfg, "w_in", w_sh["w_in"], self.pos)[0],
                     "conv_w": pad_conv_w(cfg, w_sh["conv_w"], self.pos)}
        self.inflight = {}
        self.forwards = {}
        self.grads = {}

    def first_start(self):
        cfg = self.cfg
        self.first, token = gather_start(cfg, "first", FIRST, self.full, jnp.zeros((8, LANE), F32))
        for k in REST:
            self.full[k], token = cast_into(cfg, k, self.w_sh[k], self.pos, dep=token)
        return token

    def weights_first(self, after):
        cfg = self.cfg
        ssem, rsem, lands = self.first[0]
        w_in, conv_w = gather_wait(cfg, FIRST[0], ssem, rsem, lands, after)
        return gather_finish(cfg, ("w_in",), [w_in])[0], conv_w

    def start_rest(self, after):
        self.rest, token = gather_start(self.cfg, "rest", GROUPS, self.full, after)
        return token

    def weights_rest(self, group, after):
        cfg = self.cfg
        names = GROUPS[group]
        ssem, rsem, lands = self.rest[group]
        lands = dict(zip(names, gather_wait(cfg, names, ssem, rsem, lands, after)))
        now = [k for k in names if k != "w_up"]
        later = [k for k in names if k == "w_up"]
        ready = gather_finish(cfg, tuple(now), [lands[k] for k in now])
        if not later:
            return tuple(ready), None
        out = forward_start(cfg, tuple(later), [lands[k] for k in later], ready[0])
        self.forwards[group] = (tuple(later),) + tuple(out[:3])
        return tuple(ready), out[3]

    def forwarded(self, group, after):
        names, ssem, rsem, lands = self.forwards.pop(group)
        return tuple(forward_wait(self.cfg, names, ssem, rsem, lands, after))

    def pair_start(self, grads):
        out = pair_start(self.cfg, grads, jnp.zeros((8, LANE), F32))
        self.pairs = out[:4]
        return out[4]

    def reduce_start(self, grads, after=None):
        theirs = pair_send(self.cfg, grads) if grads else {}
        if after is not None:
            early, early_theirs = pair_wait(self.cfg, *self.pairs, after)
            grads, theirs = {**early, **grads}, {**early_theirs, **theirs}
        pres = {k: pair_sum(self.cfg, k, grads[k], theirs[k], self.pos) for k in grads}
        out = scatter_start(self.cfg, pres, jnp.zeros((8, LANE), F32))
        self.inflight[tuple(sorted(grads))] = out[:4]
        return out[4]

    def reduce_wait(self, names, after):
        cfg = self.cfg
        pres, lands = scatter_wait(cfg, *self.inflight.pop(tuple(sorted(names))), after)
        sums = {k: sum_landed(cfg, k, pres[k], lands[k], self.pos) for k in names}
        self.grads.update(half_swap(cfg, sums))


def allreduce_small(cfg, vec):
    R = vec.shape[0]

    def body(v_ref, o_ref, buf, send_sems, recv_sems):
        x, y, c = _me()
        me = 4 * x + 2 * y + c
        buf[me] = v_ref[...]
        sends = []
        for k in range(1, N_DEV):
            px, py, pc = x ^ (k >> 2), y ^ ((k >> 1) & 1), c ^ (k & 1)
            sends.append(pltpu.make_async_remote_copy(
                src_ref=v_ref, dst_ref=buf.at[me], send_sem=send_sems.at[k], recv_sem=recv_sems.at[k],
                device_id=(px, py, pc), device_id_type=MESH))
        for cp in sends:
            cp.start()
        for k in range(1, N_DEV):
            px, py, pc = x ^ (k >> 2), y ^ ((k >> 1) & 1), c ^ (k & 1)
            pltpu.make_async_remote_copy(
                src_ref=v_ref, dst_ref=buf.at[4 * px + 2 * py + pc], send_sem=send_sems.at[k],
                recv_sem=recv_sems.at[k], device_id=(px, py, pc), device_id_type=MESH).wait_recv()
        for cp in sends:
            cp.wait_send()
        acc = buf[0]
        for j in range(1, N_DEV):
            acc = acc + buf[j]
        o_ref[...] = acc

    return pl.pallas_call(
        body, name="allreduce_small",
        in_specs=[pl.BlockSpec(memory_space=pltpu.VMEM)], out_specs=pl.BlockSpec(memory_space=pltpu.VMEM),
        out_shape=jax.ShapeDtypeStruct((R, LANE), F32),
        scratch_shapes=[pltpu.VMEM((N_DEV, R, LANE), F32), pltpu.SemaphoreType.DMA((N_DEV,)),
                        pltpu.SemaphoreType.DMA((N_DEV,))],
    )(vec)


def _adamw_update(w_ref, m_ref, v_ref, g, outs):
    g_out, d_out, m_out, v_out = outs
    m_new = ADAM_B1 * m_ref[...] + (1.0 - ADAM_B1) * g
    v_new = ADAM_B2 * v_ref[...] + (1.0 - ADAM_B2) * jnp.square(g)
    m_hat = m_new / (1.0 - ADAM_B1 ** ADAM_STEP)
    v_hat = v_new / (1.0 - ADAM_B2 ** ADAM_STEP)
    g_out[...] = g
    d_out[...] = -ADAM_LR * (m_hat / (jnp.sqrt(v_hat) + ADAM_EPS) + ADAM_WD * w_ref[...])
    m_out[...] = m_new
    v_out[...] = v_new


def adamw_small(cfg, params):
    names = list(params)
    n = len(names)

    def body(*refs):
        for i in range(n):
            w_ref, m_ref, v_ref, g_ref = refs[4 * i:4 * i + 4]
            _adamw_update(w_ref, m_ref, v_ref, g_ref[...], refs[4 * n + 4 * i:4 * n + 4 * i + 4])

    outs = pl.pallas_call(
        body, name="adamw_small",
        out_shape=[jax.ShapeDtypeStruct(params[k][0].shape, F32) for k in names for _ in range(4)],
    )(*[a for k in names for a in params[k]])
    return {k: list(outs[4 * i:4 * i + 4]) for i, k in enumerate(names)}


def adamw(cfg, name, w, m, v, g_parts, tile):
    r, c = w.shape
    tm, tc = tile[0] or r, tile[1] or c
    assert tc == c or all(g.shape[1] == c for g in g_parts)
    n = len(g_parts)

    def body(*refs):
        w_ref, m_ref, v_ref = refs[:3]
        g_refs = refs[3:3 + n]
        g = g_refs[0][:, :tc]
        for gr in g_refs[1:]:
            g = g + gr[:, :tc]
        _adamw_update(w_ref, m_ref, v_ref, g, refs[3 + n:])

    blk = pl.BlockSpec((tm, tc), lambda i, j: (i, j))
    return pl.pallas_call(
        body, name=f"adamw_{name}", grid=(r // tm, c // tc),
        in_specs=[blk] * 3 + [pl.BlockSpec((tm, tc if tc < c else g.shape[1]), lambda i, j: (i, j)) for g in g_parts],
        out_specs=[blk] * 4, out_shape=[jax.ShapeDtypeStruct((r, c), F32)] * 4,
        compiler_params=_cp(cfg, ("parallel", "parallel")),
    )(w, m, v, *g_parts)


SMALL_ORDER = ("loss", "g1", "g2", "g3", "g4", "g_sb", "g_dl", "conv_b", "conv_w")


def pack_small(small):
    rows = []
    for k in SMALL_ORDER:
        a = small[k].reshape(-1, LANE)
        rows.append(a)
    flat = jnp.concatenate(rows, axis=0)
    pad = (-flat.shape[0]) % 8
    return jnp.pad(flat, ((0, pad), (0, 0))), [r.shape[0] for r in rows]


def unpack_small(red, small, counts):
    out, at = {}, 0
    for k, n in zip(SMALL_ORDER, counts):
        out[k] = red[at:at + n].reshape(small[k].shape)
        at += n
    return out


def pad_ff(cfg, a):
    r = a.shape[0]
    return jnp.pad(a.reshape(r, N_CHIPS, cfg.FSH), ((0, 0), (0, 0), (0, cfg.FSHP - cfg.FSH))).reshape(r, cfg.FF2P)


def step(cfg, x, target, gains, w_sh, conv_b, m_all, v_all):
    chip = 2 * lax.axis_index("x") + lax.axis_index("y")
    comm = MeshWeights(cfg, w_sh)
    grad_x, small = local_step(cfg, comm, x, target, gains["g1"], gains["g2"], gains["g3"], gains["g4"],
                               gains["g_sb"], gains["g_dl"], pad_ff(cfg, conv_b))

    packed, counts = pack_small(small)
    summed = allreduce_small(cfg, packed)
    comm.reduce_wait(("w_in",), after=summed)
    red = unpack_small(summed, small, counts)

    names = ("w_in", "w_out", "w_up", "w_down")
    up_rows = max(t for t in range(SUB, 513, SUB) if cfg.FSH % t == 0)
    tms = dict(w_in=(cfg.TM, None), w_out=(cfg.TM, None), w_up=(up_rows, None), w_down=(None, cfg.TN // 2))
    res = {}
    for n in names:
        res[n] = adamw(cfg, n, w_sh[n], m_all[n], v_all[n], [comm.grads[n]], tms[n])
    g_cw = lax.dynamic_slice_in_dim(red["conv_w"].reshape(3, N_CHIPS, cfg.FSHP), chip, 1, axis=1)[:, 0, :cfg.FSH]
    g_cb = red["conv_b"].reshape(1, N_CHIPS, cfg.FSHP)[:, :, :cfg.FSH].reshape(1, N_CHIPS * cfg.FSH)
    smalls = {"conv_w": (w_sh["conv_w"], g_cw), "conv_b": (conv_b, g_cb)}
    smalls.update({k: (gains[k], red[k]) for k in ("g1", "g2", "g3", "g4", "g_sb", "g_dl")})
    res.update(adamw_small(cfg, {k: (w, m_all[k], v_all[k], g) for k, (w, g) in smalls.items()}))
    return red["loss"][0, 0], grad_x, res


PARAMS = ("pre_mix_gain", "post_mix_gain", "pre_ffn_gain", "post_ffn_gain", "w_in", "sb_out_gain", "dil_out_gain",
          "w_out", "w_up", "conv_w", "conv_b", "w_down")
SHORT = dict(pre_mix_gain="g1", post_mix_gain="g2", pre_ffn_gain="g3", post_ffn_gain="g4", sb_out_gain="g_sb",
             dil_out_gain="g_dl", w_in="w_in", w_out="w_out", w_up="w_up", conv_w="conv_w", conv_b="conv_b",
             w_down="w_down")


def kernel(x, pre_mix_gain, post_mix_gain, pre_ffn_gain, post_ffn_gain, w_in, sb_out_gain, dil_out_gain, w_out, w_up, conv_w, conv_b, w_down, loss_target, m_pre_mix_gain, m_post_mix_gain, m_pre_ffn_gain, m_post_ffn_gain, m_w_in, m_sb_out_gain, m_dil_out_gain, m_w_out, m_w_up, m_conv_w, m_conv_b, m_w_down, v_pre_mix_gain, v_post_mix_gain, v_pre_ffn_gain, v_post_ffn_gain, v_w_in, v_sb_out_gain, v_dil_out_gain, v_w_out, v_w_up, v_conv_w, v_conv_b, v_w_down):
    cfg = CFG
    w = dict(zip(PARAMS, (pre_mix_gain, post_mix_gain, pre_ffn_gain, post_ffn_gain, w_in, sb_out_gain, dil_out_gain,
                          w_out, w_up, conv_w, conv_b, w_down)))
    m = dict(zip(PARAMS, (m_pre_mix_gain, m_post_mix_gain, m_pre_ffn_gain, m_post_ffn_gain, m_w_in, m_sb_out_gain,
                          m_dil_out_gain, m_w_out, m_w_up, m_conv_w, m_conv_b, m_w_down)))
    v = dict(zip(PARAMS, (v_pre_mix_gain, v_post_mix_gain, v_pre_ffn_gain, v_post_ffn_gain, v_w_in, v_sb_out_gain,
                          v_dil_out_gain, v_w_out, v_w_up, v_conv_w, v_conv_b, v_w_down)))
    sq = lambda a: a.reshape(a.shape[1:])
    ws = {SHORT[k]: sq(a) if a.ndim == 3 else a for k, a in w.items()}
    ms = {SHORT[k]: sq(a) if a.ndim == 3 else a for k, a in m.items()}
    vs = {SHORT[k]: sq(a) if a.ndim == 3 else a for k, a in v.items()}
    for d in (ws, ms, vs):
        d["w_up"] = d["w_up"].T
    gains = {k: ws[k] for k in ("g1", "g2", "g3", "g4", "g_sb", "g_dl")}
    w_sh = {k: ws[k] for k in ("w_in", "w_out", "w_up", "conv_w", "w_down")}
    loss, grad_x, res = step(cfg, sq(x), sq(loss_target), gains, w_sh, ws["conv_b"], ms, vs)
    res["w_up"] = [a.T for a in res["w_up"]]
    outs = [loss, grad_x.reshape(x.shape)]
    for i in range(4):
        for k in PARAMS:
            outs.append(res[SHORT[k]][i].reshape(w[k].shape))
    return tuple(outs)
```

```python
import functools
import math
from typing import NamedTuple

import jax
import jax.numpy as jnp
from jax import lax
from jax.experimental import pallas as pl
from jax.experimental.pallas import tpu as pltpu

F32 = jnp.float32
BF16 = jnp.bfloat16
MESH = pl.DeviceIdType.MESH

ROPE_THETA = 10000.0
RMS_EPS = 1e-6
ADAM_LR = 0.001
ADAM_B1 = 0.9
ADAM_B2 = 0.999
ADAM_EPS = 1e-08
ADAM_WD = 0.01
ADAM_STEP = 10
GELU_C = math.sqrt(2.0 / math.pi)
NEG_BIG = -1e30
LANE = 128
N_CHIPS = 4
N_DEV = 8


class Cfg(NamedTuple):
    S: int = 2048
    D: int = 2048
    DH: int = 128
    HSB: int = 8
    HDL: int = 8
    QB: int = 128
    SBT: int = 256
    SBH: int = 4
    SBHB: int = 4
    branches: tuple = ((128, 1), (512, 4), (2048, 16))
    FSH: int = 2752
    FSHP: int = 2816
    TM: int = 256
    TNF: int = 256
    FCH: int = 512
    TN: int = 512
    VMEM_MB: int = 56

    @property
    def DSB(self):
        return self.HSB * self.DH

    @property
    def DDL(self):
        return self.HDL * self.DH

    @property
    def DMIX(self):
        return self.DSB + self.DDL

    @property
    def FFP(self):
        return 2 * self.FSHP

    @property
    def FF2P(self):
        return 4 * self.FSHP


CFG = Cfg()


def _cp(cfg, sem=None):
    return pltpu.CompilerParams(dimension_semantics=sem, vmem_limit_bytes=cfg.VMEM_MB * 2**20)


def _dot(a, b):
    return jnp.dot(a, b, preferred_element_type=F32)


def _dot_nt(a, b):
    return lax.dot_general(a, b, (((1,), (1,)), ((), ())), preferred_element_type=F32)


def _dot_tn(a, b):
    return lax.dot_general(a, b, (((0,), (0,)), ((), ())), preferred_element_type=F32)


def _dot_split(x, u):
    hi = x.astype(BF16)
    lo = (x - hi.astype(F32)).astype(BF16)
    return _dot(hi, u) + _dot(lo, u)


def _rstd(x):
    return lax.rsqrt(jnp.mean(x * x, axis=-1, keepdims=True) + RMS_EPS)


def _rms_bwd(dy, x, g):
    r = _rstd(x)
    xh = x * r
    dxh = dy * g
    dx = r * (dxh - xh * jnp.mean(dxh * xh, axis=-1, keepdims=True))
    return dx, dy * xh


def _gelu(x):
    t = jnp.tanh(GELU_C * (x + 0.044715 * (x * x * x)))
    return 0.5 * x * (1.0 + t), t


def _gelu_grad(x, t):
    return 0.5 * (1.0 + t) + 0.5 * x * (1.0 - t * t) * (GELU_C * (1.0 + 3 * 0.044715 * (x * x)))


def _row(cfg, w):
    return pl.BlockSpec((cfg.TM, w), lambda i: (i, 0))


def _vec(w):
    return pl.BlockSpec((1, w), lambda i: (0, 0))


def rms_fwd(cfg, x, g):
    S, D = x.shape

    def body(x_ref, g_ref, h_ref):
        xv = x_ref[...]
        h_ref[...] = (xv * _rstd(xv) * g_ref[...]).astype(BF16)

    return pl.pallas_call(
        body, name="rms_fwd", grid=(S // cfg.TM,),
        in_specs=[_row(cfg, D), _vec(D)], out_specs=_row(cfg, D),
        out_shape=jax.ShapeDtypeStruct((S, D), BF16), compiler_params=_cp(cfg, ("parallel",)),
    )(x, g)


def mid_fwd(cfg, x, mo, g_post, g_pre):
    S, D = x.shape

    def body(x_ref, mo_ref, gp_ref, gn_ref, x1_ref, h2_ref):
        mo_v = mo_ref[...]
        x1 = x_ref[...] + mo_v * _rstd(mo_v) * gp_ref[...]
        x1_ref[...] = x1
        h2_ref[...] = (x1 * _rstd(x1) * gn_ref[...]).astype(BF16)

    return pl.pallas_call(
        body, name="mid_fwd", grid=(S // cfg.TM,),
        in_specs=[_row(cfg, D), _row(cfg, D), _vec(D), _vec(D)],
        out_specs=[_row(cfg, D), _row(cfg, D)],
        out_shape=[jax.ShapeDtypeStruct((S, D), F32), jax.ShapeDtypeStruct((S, D), BF16)],
        compiler_params=_cp(cfg, ("parallel",)),
    )(x, mo, g_post, g_pre)


def final_fwd_bwd(cfg, x1, f, g_post, target):
    S, D = x1.shape

    def body(x1_ref, f_ref, g_ref, t_ref, dout_ref, df_ref, dg_ref, loss_ref):
        @pl.when(pl.program_id(0) == 0)
        def _():
            dg_ref[...] = jnp.zeros_like(dg_ref)
            loss_ref[...] = jnp.zeros_like(loss_ref)

        fv = f_ref[...]
        g = g_ref[...]
        out = x1_ref[...] + fv * _rstd(fv) * g
        err = out - t_ref[...]
        loss_ref[...] += 0.5 * jnp.sum(jnp.mean(err * err, axis=-1, keepdims=True), axis=0, keepdims=True)
        dout = err * (1.0 / D)
        dout_ref[...] = dout
        df, dgx = _rms_bwd(dout, fv, g)
        df_ref[...] = df.astype(BF16)
        dg_ref[...] += jnp.sum(dgx, axis=0, keepdims=True)

    return pl.pallas_call(
        body, name="final_fwd_bwd", grid=(S // cfg.TM,),
        in_specs=[_row(cfg, D), _row(cfg, D), _vec(D), _row(cfg, D)],
        out_specs=[_row(cfg, D), _row(cfg, D), _vec(D), _vec(LANE)],
        out_shape=[jax.ShapeDtypeStruct((S, D), F32), jax.ShapeDtypeStruct((S, D), BF16),
                   jax.ShapeDtypeStruct((1, D), F32), jax.ShapeDtypeStruct((1, LANE), F32)],
        compiler_params=_cp(cfg, ("arbitrary",)),
    )(x1, f, g_post, target)


def mid_bwd(cfg, dh2, x1, g_pre, dout, mo, g_post):
    S, D = x1.shape

    def body(dh_ref, x1_ref, gn_ref, do_ref, mo_ref, gp_ref, dx1_ref, dmo_ref, dgn_ref, dgp_ref):
        @pl.when(pl.program_id(0) == 0)
        def _():
            dgn_ref[...] = jnp.zeros_like(dgn_ref)
            dgp_ref[...] = jnp.zeros_like(dgp_ref)

        dx, dgx = _rms_bwd(dh_ref[...], x1_ref[...], gn_ref[...])
        dx1 = do_ref[...] + dx
        dx1_ref[...] = dx1
        dgn_ref[...] += jnp.sum(dgx, axis=0, keepdims=True)
        dmo, dgy = _rms_bwd(dx1, mo_ref[...], gp_ref[...])
        dmo_ref[...] = dmo.astype(BF16)
        dgp_ref[...] += jnp.sum(dgy, axis=0, keepdims=True)

    return pl.pallas_call(
        body, name="mid_bwd", grid=(S // cfg.TM,),
        in_specs=[_row(cfg, D), _row(cfg, D), _vec(D), _row(cfg, D), _row(cfg, D), _vec(D)],
        out_specs=[_row(cfg, D), _row(cfg, D), _vec(D), _vec(D)],
        out_shape=[jax.ShapeDtypeStruct((S, D), F32), jax.ShapeDtypeStruct((S, D), BF16),
                   jax.ShapeDtypeStruct((1, D), F32), jax.ShapeDtypeStruct((1, D), F32)],
        compiler_params=_cp(cfg, ("arbitrary",)),
    )(dh2, x1, g_pre, dout, mo, g_post)


def first_bwd(cfg, dh1, x, g_pre, dx1):
    S, D = x.shape

    def body(dh_ref, x_ref, g_ref, r_ref, dx_ref, dg_ref):
        @pl.when(pl.program_id(0) == 0)
        def _():
            dg_ref[...] = jnp.zeros_like(dg_ref)

        dx, dgx = _rms_bwd(dh_ref[...], x_ref[...], g_ref[...])
        dx_ref[...] = r_ref[...] + dx
        dg_ref[...] += jnp.sum(dgx, axis=0, keepdims=True)

    return pl.pallas_call(
        body, name="first_bwd", grid=(S // cfg.TM,),
        in_specs=[_row(cfg, D), _row(cfg, D), _vec(D), _row(cfg, D)],
        out_specs=[_row(cfg, D), _vec(D)],
        out_shape=[jax.ShapeDtypeStruct((S, D), F32), jax.ShapeDtypeStruct((1, D), F32)],
        compiler_params=_cp(cfg, ("arbitrary",)),
    )(dh1, x, g_pre, dx1)


def _mm(cfg, name, a, b, *, nt, a_spec, b_spec, o_spec, grid, out_shape, acc_shape, dep=None):
    nk = grid[-1]
    dot = _dot_nt if nt else _dot
    deps = [] if dep is None else [dep]

    def body(a_ref, b_ref, *rest):
        o_ref, acc_ref = rest[-2:]
        k = pl.program_id(len(grid) - 1)
        part = dot(a_ref[...], b_ref[...])
        if deps:
            part = part + rest[0][0:1, 0:1]
        if nk == 1:
            o_ref[...] = part.astype(o_ref.dtype)
            return

        @pl.when(k == 0)
        def _():
            acc_ref[...] = part

        @pl.when(k > 0)
        def _():
            acc_ref[...] += part

        @pl.when(k == nk - 1)
        def _():
            o_ref[...] = acc_ref[...].astype(o_ref.dtype)

    sem = ("parallel",) * (len(grid) - 1) + ("arbitrary",)
    dep_specs = [pl.BlockSpec((8, LANE), lambda *_: (0, 0))] * len(deps)
    return pl.pallas_call(
        body, name=name, grid=grid, in_specs=[a_spec, b_spec] + dep_specs, out_specs=o_spec, out_shape=out_shape,
        scratch_shapes=[pltpu.VMEM(acc_shape, F32)], compiler_params=_cp(cfg, sem),
    )(a, b, *deps)


def _mm_tn(cfg, name, a, b, *, a_spec, b_spec, o_spec, grid, out_shape, dep=None):
    deps = [] if dep is None else [dep]

    def body(a_ref, b_ref, *rest):
        part = _dot_tn(a_ref[...], b_ref[...])
        if deps:
            part = part + rest[0][0:1, 0:1]
        rest[-1][...] = part.astype(rest[-1].dtype)

    dep_specs = [pl.BlockSpec((8, LANE), lambda *_: (0, 0))] * len(deps)
    return pl.pallas_call(
        body, name=name, grid=grid, in_specs=[a_spec, b_spec] + dep_specs, out_specs=o_spec, out_shape=out_shape,
        compiler_params=_cp(cfg, ("parallel",) * len(grid)),
    )(a, b, *deps)


def qkv_proj(cfg, h1, w_in, cos2, sin2):
    S, D = h1.shape
    tn = 2 * cfg.DH
    per = cfg.DSB // tn
    assert cfg.DSB == cfg.DDL
    nblk = 6 * per

    def body(a_ref, b_ref, c_ref, s_ref, o_ref):
        j = pl.program_id(0)
        acc = _dot(a_ref[...], b_ref[...])
        rope = jnp.logical_and(j >= 3 * per, j < 5 * per)

        @pl.when(rope)
        def _():
            for c in range(tn // cfg.DH):
                xh = acc[:, c * cfg.DH:(c + 1) * cfg.DH]
                o_ref[:, c * cfg.DH:(c + 1) * cfg.DH] = (
                    xh * c_ref[...] + pltpu.roll(xh, cfg.DH // 2, 1) * s_ref[...]).astype(BF16)

        @pl.when(jnp.logical_not(rope))
        def _():
            o_ref[...] = acc.astype(BF16)

    return pl.pallas_call(
        body, name="qkv_proj", grid=(nblk,),
        in_specs=[pl.BlockSpec((S, D), lambda j: (0, 0)), pl.BlockSpec((D, tn), lambda j: (0, j)),
                  pl.BlockSpec((S, cfg.DH), lambda j: (0, 0)), pl.BlockSpec((S, cfg.DH), lambda j: (0, 0))],
        out_specs=pl.BlockSpec((None, S, tn), lambda j: (j // per, 0, j % per)),
        out_shape=jax.ShapeDtypeStruct((6, S, cfg.DSB), BF16),
        compiler_params=_cp(cfg, ("parallel",)),
    )(h1, w_in, cos2, sin2)


def _sb_tile(cfg, q, k, valid):
    z = _dot_nt(q, k) * (cfg.DH ** -0.5)
    lb = jnp.minimum(z, 0.0) - jnp.log1p(jnp.exp(-jnp.abs(z)))
    lk = lb - z
    return lb, (lk if valid is None else jnp.where(valid, lk, 0.0))


def _masked(valid, x):
    return x if valid is None else jnp.where(valid, x, 0.0)


def sb_fwd(cfg, qkv3):
    S, QB, DH, NH = cfg.S, cfg.SBT, cfg.DH, cfg.SBH

    def body(q_ref, k_ref, v_ref, o_ref, t_ref):
        row = lax.broadcasted_iota(jnp.int32, (QB, QB), 0)
        col = lax.broadcasted_iota(jnp.int32, (QB, QB), 1)
        u_after = (row > col).astype(BF16)
        causal = col < row
        heads = [slice(h * DH, (h + 1) * DH) for h in range(NH)]

        def q_loop(qb, _):
            rows = pl.ds(pl.multiple_of(qb * QB, QB), QB)
            qs = [q_ref[rows, hd] for hd in heads]

            def tile(kb, carry, valid):
                krows = pl.ds(pl.multiple_of(kb * QB, QB), QB)
                lbk = [_sb_tile(cfg, q, k_ref[krows, hd], valid) for q, hd in zip(qs, heads)]
                rems = [_dot_split(lk, u_after) for _, lk in lbk]
                aa = [_masked(valid, jnp.exp(lb + rem + c)).astype(BF16) for (lb, _), rem, (_, c) in zip(lbk, rems, carry)]
                return tuple((o_acc + _dot(a, v_ref[krows, hd]), c + jnp.sum(lk, axis=1, keepdims=True))
                             for a, hd, (_, lk), (o_acc, c) in zip(aa, heads, lbk, carry))

            carry = tile(qb, ((jnp.zeros((QB, DH), F32), jnp.zeros((QB, 1), F32)),) * NH, causal)
            carry = lax.fori_loop(0, qb, lambda i, cr: tile(qb - 1 - i, cr, None), carry)
            for hd, (o_acc, c) in zip(heads, carry):
                o_ref[rows, hd] = o_acc
                t_ref[rows, hd] = jnp.broadcast_to(c, (QB, DH))
            return 0

        lax.fori_loop(0, S // QB, q_loop, 0)

    def spec(i):
        return pl.BlockSpec((None, S, NH * DH), lambda h: (i, 0, h))

    return pl.pallas_call(
        body, name="sb_fwd", grid=(cfg.HSB // NH,),
        in_specs=[spec(0), spec(1), spec(2)],
        out_specs=[pl.BlockSpec((S, NH * DH), lambda h: (0, h))] * 2,
        out_shape=[jax.ShapeDtypeStruct((S, cfg.DSB), F32)] * 2,
        compiler_params=_cp(cfg, ("parallel",)),
    )(qkv3, qkv3, qkv3)


def sb_bwd(cfg, qkv3, do_sb, tsum):
    S, QB, DH, NH = cfg.S, cfg.SBT, cfg.DH, cfg.SBHB
    scale = DH ** -0.5

    def body(q_ref, k_ref, v_ref, do_ref, t_ref, d_ref, dk_acc, dv_acc):
        dk_acc[...] = jnp.zeros_like(dk_acc)
        dv_acc[...] = jnp.zeros_like(dv_acc)
        row = lax.broadcasted_iota(jnp.int32, (QB, QB), 0)
        col = lax.broadcasted_iota(jnp.int32, (QB, QB), 1)
        u_upto = (row <= col).astype(BF16)
        u_before = (row < col).astype(BF16)
        causal = col < row
        heads = [slice(h * DH, (h + 1) * DH) for h in range(NH)]

        def q_loop(qb, _):
            rows = pl.ds(pl.multiple_of(qb * QB, QB), QB)
            qs = [q_ref[rows, hd] for hd in heads]
            dos = [do_ref[rows, hd] for hd in heads]
            totals = [t_ref[rows, hd.start:hd.start + 1] for hd in heads]

            def tile(kb, carry, valid):
                krows = pl.ds(pl.multiple_of(kb * QB, QB), QB)
                ks = [k_ref[krows, hd] for hd in heads]
                lbk = [_sb_tile(cfg, q, k, valid) for q, k in zip(qs, ks)]
                das = [_dot_nt(do, v_ref[krows, hd]) for do, hd in zip(dos, heads)]
                pins = [_dot_split(lk, u_upto) for _, lk in lbk]
                aa = [_masked(valid, jnp.exp(lb + (tot - pc - pin)))
                      for (lb, _), tot, (_, pc, _), pin in zip(lbk, totals, carry, pins)]
                gs = [a * da for a, da in zip(aa, das)]
                for a, do, hd in zip(aa, dos, heads):
                    dv_acc[krows, hd] += _dot_tn(a.astype(BF16), do)
                cums = [gc + _dot(g.astype(BF16), u_before) for g, (_, _, gc) in zip(gs, carry)]
                dzs = [(_masked(valid, g - jnp.exp(lb) * (g + cum)) * scale).astype(BF16)
                       for g, (lb, _), cum in zip(gs, lbk, cums)]
                for dz, q, hd in zip(dzs, qs, heads):
                    dk_acc[krows, hd] += _dot_tn(dz, q)
                return tuple((dq + _dot(dz, k), pc + jnp.sum(lk, axis=1, keepdims=True), gc + jnp.sum(g, axis=1, keepdims=True))
                             for dz, k, (_, lk), g, (dq, pc, gc) in zip(dzs, ks, lbk, gs, carry))

            z1 = jnp.zeros((QB, 1), F32)
            carry = lax.fori_loop(0, qb, lambda kb, cr: tile(kb, cr, None), ((jnp.zeros((QB, DH), F32), z1, z1),) * NH)
            for hd, (dq_acc, _, _) in zip(heads, tile(qb, carry, causal)):
                d_ref[0, rows, hd] = dq_acc.astype(BF16)
            return 0

        lax.fori_loop(0, S // QB, q_loop, 0)
        d_ref[1, :, :] = dk_acc[...].astype(BF16)
        d_ref[2, :, :] = dv_acc[...].astype(BF16)

    def spec(i):
        return pl.BlockSpec((None, S, NH * DH), lambda h: (i, 0, h))

    hd_spec = pl.BlockSpec((S, NH * DH), lambda h: (0, h))
    return pl.pallas_call(
        body, name="sb_bwd", grid=(cfg.HSB // NH,),
        in_specs=[spec(0), spec(1), spec(2), hd_spec, hd_spec],
        out_specs=pl.BlockSpec((3, S, NH * DH), lambda h: (0, 0, h)),
        out_shape=jax.ShapeDtypeStruct((6, S, cfg.DSB), BF16),
        scratch_shapes=[pltpu.VMEM((S, NH * DH), F32), pltpu.VMEM((S, NH * DH), F32)],
        compiler_params=_cp(cfg, ("parallel",)),
    )(qkv3, qkv3, qkv3, do_sb, tsum)


def _band_mask(cfg, n, n_back):
    QB = cfg.QB
    qi = lax.broadcasted_iota(jnp.int32, (QB, 2 * QB), 0)
    kj = lax.broadcasted_iota(jnp.int32, (QB, 2 * QB), 1)
    dist = QB + qi - kj
    return (dist >= 0) & (dist <= n_back) & jnp.logical_or(n > 0, kj >= QB)


def _sub_rows(start, n, dil):
    if dil > 1:
        return pl.ds(start, n, stride=dil)
    return pl.ds(start if isinstance(start, int) else pl.multiple_of(start, 8), n)


def _stage_residues(cfg, dil, pairs):
    QB, L = cfg.QB, cfg.S // dil
    for src, dst in pairs:
        for r in range(dil):
            dst[pl.ds(r * (QB + L), QB), :] = jnp.zeros((QB, cfg.DH), BF16)
            dst[pl.ds(r * (QB + L) + QB, L), :] = src[_sub_rows(r, L, dil), :].astype(BF16)


def _staged_rows(cfg):
    return cfg.S + cfg.QB * max(d for _, d in cfg.branches)


def _lane_value(x):
    return jnp.max(x, axis=1, keepdims=True)


def dil_fwd(cfg, qkv3):
    S, QB, DH = cfg.S, cfg.QB, cfg.DH
    scale = DH ** -0.5
    nb = len(cfg.branches)
    mix_rows = min(256, S)

    def body(q_ref, k_ref, v_ref, o_ref, lt_ref, qf, kf, vf, kp, vp, *obl):
        obs, lbs = obl[:nb], obl[nb:]
        qf[...] = q_ref[...].astype(F32)
        kf[...] = k_ref[...].astype(F32)
        vf[...] = v_ref[...].astype(F32)
        for b, (window, dil) in enumerate(cfg.branches):
            L, n_back = S // dil, window // dil
            assert n_back <= QB and L % QB == 0
            _stage_residues(cfg, dil, [(kf, kp), (vf, vp)])
            for r in range(dil):
                for n in range(L // QB):
                    rows = _sub_rows(r + n * (QB * dil), QB, dil)
                    band = pl.ds(r * (QB + L) + n * QB, 2 * QB)
                    s = _dot_nt(qf[rows, :].astype(BF16), kp[band, :]) * scale
                    s = jnp.where(_band_mask(cfg, n, n_back), s, NEG_BIG)
                    m = jnp.max(s, axis=1, keepdims=True)
                    p = jnp.exp(s - m)
                    den = jnp.sum(p, axis=1, keepdims=True)
                    obs[b][rows, :] = _dot(p.astype(BF16), vp[band, :]) / den
                    lbs[b][rows, :] = jnp.broadcast_to(m + jnp.log(den), (QB, DH))

        def mix(i, _):
            rows = pl.ds(pl.multiple_of(i * mix_rows, mix_rows), mix_rows)
            ls = [r[rows, :] for r in lbs]
            m = functools.reduce(jnp.maximum, ls)
            es = [jnp.exp(l - m) for l in ls]
            tot = functools.reduce(jnp.add, es)
            o_ref[rows, :] = functools.reduce(jnp.add, [(e / tot) * r[rows, :] for e, r in zip(es, obs)])
            lt_ref[rows, :] = m + jnp.log(tot)
            return 0

        lax.fori_loop(0, S // mix_rows, mix, 0)

    def spec(i):
        return pl.BlockSpec((None, S, DH), lambda h: (i, 0, h))

    o_spec = pl.BlockSpec((S, DH), lambda h: (0, h))
    return pl.pallas_call(
        body, name="dil_fwd", grid=(cfg.HDL,),
        in_specs=[spec(3), spec(4), spec(5)], out_specs=[o_spec, o_spec],
        out_shape=[jax.ShapeDtypeStruct((S, cfg.DDL), F32)] * 2,
        scratch_shapes=[pltpu.VMEM((S, DH), F32)] * 3 + [pltpu.VMEM((_staged_rows(cfg), DH), BF16)] * 2
        + [pltpu.VMEM((S, DH), F32)] * (2 * nb),
        compiler_params=_cp(cfg, ("parallel",)),
    )(qkv3, qkv3, qkv3)


def dil_bwd(cfg, qkv3, do_dl, delta, lse_tot, cos2, sin2, d_sb3):
    S, QB, DH = cfg.S, cfg.QB, cfg.DH
    scale = DH ** -0.5
    out_rows = min(256, S)
    GROUP = 4

    def body(q_ref, k_ref, v_ref, do_ref, dl_ref, lt_ref, c_ref, s_ref, base_ref, d_ref,
             qf, kf, vf, dof, kp, vp, dkp, dvp, dqn, dkn, dvn):
        qf[...] = q_ref[...].astype(F32)
        kf[...] = k_ref[...].astype(F32)
        vf[...] = v_ref[...].astype(F32)
        dof[...] = do_ref[...].astype(F32)
        for acc in (dqn, dkn, dvn):
            acc[...] = jnp.zeros_like(acc)
        for window, dil in cfg.branches:
            L, n_back = S // dil, window // dil
            reg = QB + L
            _stage_residues(cfg, dil, [(kf, kp), (vf, vp)])
            dkp[pl.ds(0, dil * reg), :] = jnp.zeros((dil * reg, DH), F32)
            dvp[pl.ds(0, dil * reg), :] = jnp.zeros((dil * reg, DH), F32)
            blocks = [(r, n) for n in range(L // QB) for r in range(dil)]
            for g0 in range(0, len(blocks), GROUP):
                grp = blocks[g0:g0 + GROUP]
                rows = [_sub_rows(r + n * (QB * dil), QB, dil) for r, n in grp]
                bands = [pl.ds(r * reg + n * QB, 2 * QB) for r, n in grp]
                qs = [qf[rw, :].astype(BF16) for rw in rows]
                dos = [dof[rw, :].astype(BF16) for rw in rows]
                kbs = [kp[bd, :] for bd in bands]
                ss = [_dot_nt(q, kb) * scale for q, kb in zip(qs, kbs)]
                dps = [_dot_nt(do, vp[bd, :]) for do, bd in zip(dos, bands)]
                ps = [jnp.exp(jnp.where(_band_mask(cfg, n, n_back), s, NEG_BIG) - _lane_value(lt_ref[rw, :]))
                      for s, rw, (_, n) in zip(ss, rows, grp)]
                dss = [(p * (dp - _lane_value(dl_ref[rw, :])) * scale).astype(BF16) for p, dp, rw in zip(ps, dps, rows)]
                for rw, bd, ds, p, q, do, kb in zip(rows, bands, dss, ps, qs, dos, kbs):
                    dqn[rw, :] += _dot(ds, kb)
                    dkp[bd, :] += _dot_tn(ds, q)
                    dvp[bd, :] += _dot_tn(p.astype(BF16), do)
            for r in range(dil):
                sub = _sub_rows(r, L, dil)
                dkn[sub, :] += dkp[pl.ds(r * reg + QB, L), :]
                dvn[sub, :] += dvp[pl.ds(r * reg + QB, L), :]

        def finish(i, _):
            rows = pl.ds(pl.multiple_of(i * out_rows, out_rows), out_rows)
            c, sn = c_ref[rows, :], s_ref[rows, :]
            for j, acc in enumerate((dqn, dkn)):
                d = acc[rows, :]
                d_ref[j, rows, :] = (d * c + pltpu.roll(d * sn, DH // 2, 1)).astype(BF16)
            d_ref[2, rows, :] = dvn[rows, :].astype(BF16)
            return 0

        lax.fori_loop(0, S // out_rows, finish, 0)

    def spec(i):
        return pl.BlockSpec((None, S, DH), lambda h: (i, 0, h))

    hd = pl.BlockSpec((S, DH), lambda h: (0, h))
    tab = pl.BlockSpec((S, DH), lambda h: (0, 0))
    ns = _staged_rows(cfg)
    return pl.pallas_call(
        body, name="dil_bwd", grid=(cfg.HDL,),
        in_specs=[spec(3), spec(4), spec(5), hd, hd, hd, tab, tab, ANY],
        out_specs=pl.BlockSpec((3, S, DH), lambda h: (1, 0, h)),
        out_shape=jax.ShapeDtypeStruct((6, S, cfg.DDL), BF16),
        input_output_aliases={8: 0},
        scratch_shapes=[pltpu.VMEM((S, DH), F32)] * 4 + [pltpu.VMEM((ns, DH), BF16)] * 2
        + [pltpu.VMEM((ns, DH), F32)] * 2 + [pltpu.VMEM((S, DH), F32)] * 3,
        compiler_params=_cp(cfg, ("parallel",)),
    )(qkv3, qkv3, qkv3, do_dl, delta, lse_tot, cos2, sin2, d_sb3)


def combine_fwd(cfg, o_sb, o_dl, g_sb, g_dl):
    S, DH = cfg.S, cfg.DH

    def head_norm(o, g):
        return o * lax.rsqrt(jnp.mean(o * o, axis=-1, keepdims=True) + RMS_EPS) * g

    def body(osb_ref, odl_ref, gsb_ref, gdl_ref, mix_ref):
        for h in range(cfg.HSB):
            c = slice(h * DH, (h + 1) * DH)
            mix_ref[:, c] = head_norm(osb_ref[:, c], gsb_ref[:, c]).astype(BF16)
        for h in range(cfg.HDL):
            c = slice(h * DH, (h + 1) * DH)
            mix_ref[:, cfg.DSB + h * DH:cfg.DSB + (h + 1) * DH] = head_norm(odl_ref[:, c], gdl_ref[:, c]).astype(BF16)

    return pl.pallas_call(
        body, name="combine_fwd", grid=(S // cfg.TM,),
        in_specs=[_row(cfg, cfg.DSB), _row(cfg, cfg.DDL), _vec(cfg.DSB), _vec(cfg.DDL)],
        out_specs=_row(cfg, cfg.DMIX), out_shape=jax.ShapeDtypeStruct((S, cfg.DMIX), BF16),
        compiler_params=_cp(cfg, ("parallel",)),
    )(o_sb, o_dl, g_sb, g_dl)


def combine_bwd(cfg, dmix, o_sb, o_dl, g_sb, g_dl):
    S, DH = cfg.S, cfg.DH

    def body(dm_ref, osb_ref, odl_ref, gsb_ref, gdl_ref, dsb_ref, ddl_ref, dl_ref, dgsb_ref, dgdl_ref):
        @pl.when(pl.program_id(0) == 0)
        def _():
            dgsb_ref[...] = jnp.zeros_like(dgsb_ref)
            dgdl_ref[...] = jnp.zeros_like(dgdl_ref)

        for h in range(cfg.HSB):
            c = slice(h * DH, (h + 1) * DH)
            dx, dgx = _rms_bwd(dm_ref[:, c], osb_ref[:, c], gsb_ref[:, c])
            dsb_ref[:, c] = dx.astype(BF16)
            dgsb_ref[:, c] += jnp.sum(dgx, axis=0, keepdims=True)
        for h in range(cfg.HDL):
            c = slice(h * DH, (h + 1) * DH)
            o = odl_ref[:, c]
            dx, dgx = _rms_bwd(dm_ref[:, cfg.DSB + h * DH:cfg.DSB + (h + 1) * DH], o, gdl_ref[:, c])
            ddl_ref[:, c] = dx.astype(BF16)
            dl_ref[:, c] = jnp.broadcast_to(jnp.sum(dx * o, axis=-1, keepdims=True), dx.shape)
            dgdl_ref[:, c] += jnp.sum(dgx, axis=0, keepdims=True)

    return pl.pallas_call(
        body, name="combine_bwd", grid=(S // cfg.TM,),
        in_specs=[_row(cfg, cfg.DMIX), _row(cfg, cfg.DSB), _row(cfg, cfg.DDL), _vec(cfg.DSB), _vec(cfg.DDL)],
        out_specs=[_row(cfg, cfg.DSB), _row(cfg, cfg.DDL), _row(cfg, cfg.DDL), _vec(cfg.DSB), _vec(cfg.DDL)],
        out_shape=[jax.ShapeDtypeStruct((S, cfg.DSB), BF16), jax.ShapeDtypeStruct((S, cfg.DDL), BF16),
                   jax.ShapeDtypeStruct((S, cfg.DDL), F32), jax.ShapeDtypeStruct((1, cfg.DSB), F32),
                   jax.ShapeDtypeStruct((1, cfg.DDL), F32)],
        compiler_params=_cp(cfg, ("arbitrary",)),
    )(dmix, o_sb, o_dl, g_sb, g_dl)


SUB = 8


def _shift_down(u, prev, j):
    rolled = pltpu.roll(u, j, 0)
    row = lax.broadcasted_iota(jnp.int32, (SUB, u.shape[1]), 0)
    head = jnp.where(row >= j, rolled[:SUB], pltpu.roll(prev, j, 0))
    return jnp.concatenate([head, rolled[SUB:]], axis=0)


def _shift_up(u, nxt, j):
    n = u.shape[0]
    rolled = pltpu.roll(u, n - j, 0)
    row = lax.broadcasted_iota(jnp.int32, (SUB, u.shape[1]), 0)
    tail = jnp.where(row < SUB - j, rolled[n - SUB:], pltpu.roll(nxt, SUB - j, 0))
    return jnp.concatenate([rolled[:n - SUB], tail], axis=0)


def _conv(u, s1, s2, cw, cb):
    return u * cw[2:3, :] + s1 * cw[1:2, :] + s2 * cw[0:1, :] + cb


def _chunk_rows(cfg):
    ch = min(cfg.FCH, cfg.S)
    return ch, cfg.S // ch


def ffn_fwd(cfg, h2, w_up, conv_w, conv_b):
    S, D = h2.shape
    tn, nt = cfg.TNF, cfg.FFP // cfg.TNF
    ch, nch = _chunk_rows(cfg)

    def body(h_ref, wg_ref, wv_ref, cwg_ref, cwv_ref, cbg_ref, cbv_ref, u_ref, y_ref):
        prev = [jnp.zeros((SUB, tn), F32)] * 2
        pending = None
        for ci in range(nch + 1):
            if ci < nch:
                h = h_ref[pl.ds(ci * ch, ch), :]
                us_next = [_dot_nt(h, wg_ref[...]), _dot_nt(h, wv_ref[...])]
            if pending is not None:
                rows, us = pending
                cs = []
                for i, (cw_ref, cb_ref) in enumerate(((cwg_ref, cbg_ref), (cwv_ref, cbv_ref))):
                    u_ref[i, rows, :] = us[i]
                    cs.append(_conv(us[i], _shift_down(us[i], prev[i], 1), _shift_down(us[i], prev[i], 2),
                                    cw_ref[...], cb_ref[...]))
                y_ref[rows, :] = (_gelu(cs[0])[0] * cs[1]).astype(BF16)
                prev = [u[ch - SUB:] for u in us]
            pending = (pl.ds(ci * ch, ch), us_next) if ci < nch else None

    return pl.pallas_call(
        body, name="ffn_fwd", grid=(nt,),
        in_specs=[pl.BlockSpec((S, D), lambda n: (0, 0)),
                  pl.BlockSpec((tn, D), lambda n: (n, 0)), pl.BlockSpec((tn, D), lambda n: (n + nt, 0)),
                  pl.BlockSpec((3, tn), lambda n: (0, n)), pl.BlockSpec((3, tn), lambda n: (0, n + nt)),
                  pl.BlockSpec((1, tn), lambda n: (0, n)), pl.BlockSpec((1, tn), lambda n: (0, n + nt))],
        out_specs=[pl.BlockSpec((2, S, tn), lambda n: (0, 0, n)), pl.BlockSpec((S, tn), lambda n: (0, n))],
        out_shape=[jax.ShapeDtypeStruct((2, S, cfg.FFP), F32), jax.ShapeDtypeStruct((S, cfg.FFP), BF16)],
        compiler_params=_cp(cfg, ("parallel",)),
    )(h2, w_up, w_up, conv_w, conv_w, conv_b, conv_b)


def ffn_bwd(cfg, df, h2, w_down, u, conv_w, conv_b):
    S, D = df.shape
    tn, nt = cfg.TNF, cfg.FFP // cfg.TNF

    ch, nch = _chunk_rows(cfg)

    def body(df_ref, h_ref, wd_ref, u_ref, cwg_ref, cwv_ref, cbg_ref, cbv_ref,
             du_ref, dwd_ref, dwu_ref, dcw_ref, dcb_ref):
        cws = (cwg_ref[...], cwv_ref[...])
        cbs = (cbg_ref[...], cbv_ref[...])
        zero = jnp.zeros((SUB, tn), F32)
        nxt = [zero, zero]
        dws = [[jnp.zeros((1, tn), F32)] * 4 for _ in range(2)]
        dwd = jnp.zeros((tn, D), F32)
        dwu = [jnp.zeros((tn, D), F32)] * 2
        order = list(reversed(range(nch)))
        dys, done = {}, {}
        for step in range(nch + 2):
            if step < nch:
                ci = order[step]
                dys[ci] = _dot_nt(df_ref[pl.ds(ci * ch, ch), :], wd_ref[...])
            if 1 <= step <= nch:
                ci = order[step - 1]
                rows = pl.ds(ci * ch, ch)
                dy = dys.pop(ci)
                us, s1, s2, cs = [], [], [], []
                for i in range(2):
                    u = u_ref[i, rows, :]
                    prev = u_ref[i, pl.ds(ci * ch - SUB, SUB), :] if ci else zero
                    us.append(u)
                    s1.append(_shift_down(u, prev, 1))
                    s2.append(_shift_down(u, prev, 2))
                    cs.append(_conv(u, s1[i], s2[i], cws[i], cbs[i]))
                gl, t = _gelu(cs[0])
                dcs = (dy * cs[1] * _gelu_grad(cs[0], t), dy * gl)
                dus = []
                for i, dc in enumerate(dcs):
                    du = dc * cws[i][2:3, :] + _shift_up(dc, nxt[i], 1) * cws[i][1:2, :] + _shift_up(dc, nxt[i], 2) * cws[i][0:1, :]
                    dus.append(du.astype(BF16))
                    du_ref[i, rows, :] = dus[i]
                    for j, tap in enumerate((s2[i], s1[i], us[i])):
                        dws[i][j] = dws[i][j] + jnp.sum(dc * tap, axis=0, keepdims=True)
                    dws[i][3] = dws[i][3] + jnp.sum(dc, axis=0, keepdims=True)
                nxt = [dc[:SUB] for dc in dcs]
                done[ci] = ((gl * cs[1]).astype(BF16), dus)
            if step >= 2:
                ci = order[step - 2]
                rows = pl.ds(ci * ch, ch)
                yv, dus = done.pop(ci)
                dwd = dwd + _dot_tn(yv, df_ref[rows, :])
                hv = h_ref[rows, :]
                dwu = [acc + _dot_tn(du, hv) for acc, du in zip(dwu, dus)]
        dwd_ref[...] = dwd.astype(BF16)
        for i in range(2):
            dwu_ref[i] = dwu[i].astype(BF16)
            for j in range(3):
                dcw_ref[i, j:j + 1, :] = dws[i][j]
            dcb_ref[i] = dws[i][3]

    whole = pl.BlockSpec((S, D), lambda n: (0, 0), pipeline_mode=pl.Buffered(1))
    du, dwd, dwu, dcw, dcb = pl.pallas_call(
        body, name="ffn_bwd", grid=(nt,),
        in_specs=[whole, whole, pl.BlockSpec((tn, D), lambda n: (n, 0)),
                  pl.BlockSpec((2, S, tn), lambda n: (0, 0, n)),
                  pl.BlockSpec((3, tn), lambda n: (0, n)), pl.BlockSpec((3, tn), lambda n: (0, n + nt)),
                  pl.BlockSpec((1, tn), lambda n: (0, n)), pl.BlockSpec((1, tn), lambda n: (0, n + nt))],
        out_specs=[pl.BlockSpec((2, S, tn), lambda n: (0, 0, n)), pl.BlockSpec((tn, D), lambda n: (n, 0)),
                   pl.BlockSpec((2, tn, D), lambda n: (0, n, 0)),
                   pl.BlockSpec((2, 3, tn), lambda n: (0, 0, n)), pl.BlockSpec((2, 1, tn), lambda n: (0, 0, n))],
        out_shape=[jax.ShapeDtypeStruct((2, S, cfg.FFP), BF16), jax.ShapeDtypeStruct((cfg.FFP, D), BF16),
                   jax.ShapeDtypeStruct((2, cfg.FFP, D), BF16),
                   jax.ShapeDtypeStruct((2, 3, cfg.FFP), F32), jax.ShapeDtypeStruct((2, 1, cfg.FFP), F32)],
        compiler_params=_cp(cfg, ("parallel",)),
    )(df, h2, w_down, u, conv_w, conv_w, conv_b, conv_b)
    return du, dwd, dwu.reshape(cfg.FF2P, D), dcw, dcb


def rope_tables(cfg):
    inv_freq = ROPE_THETA ** (-jnp.arange(0, cfg.DH, 2, dtype=F32) / cfg.DH)
    ang = jnp.arange(cfg.S, dtype=F32)[:, None] * inv_freq[None, :]
    cos, sin = jnp.cos(ang), jnp.sin(ang)
    return jnp.concatenate([cos, cos], axis=1), jnp.concatenate([-sin, sin], axis=1)


class LocalWeights:
    def __init__(self, w_in, w_out, w_up, conv_w, w_down):
        self.w = (w_in, w_out, w_up, conv_w, w_down)
        self.grads = {}

    def first_start(self):
        return None

    def weights_first(self, after):
        return self.w[0], self.w[3]

    def start_rest(self, after):
        return None

    def weights_rest(self, group, after):
        return ((self.w[1],), None) if group == 0 else ((self.w[4],), None)

    def forwarded(self, group, after):
        return (self.w[2],)

    def pair_start(self, grads):
        self.grads.update(grads)
        return None

    def reduce_start(self, grads, after=None):
        self.grads.update(grads)
        return None

    def reduce_wait(self, names, after, finish_later=False):
        return None

    def reduce_finish(self, after):
        pass


def _after(a, token):
    return a if token is None else a + token[0, 0].astype(a.dtype)


def local_step(cfg, comm, x, target, g1, g2, g3, g4, g_sb, g_dl, conv_b):
    S, D = cfg.S, cfg.D
    cos2, sin2 = rope_tables(cfg)
    full = lambda r, c: pl.BlockSpec((r, c), lambda j, k: (0, 0))

    h1 = rms_fwd(cfg, x, _after(g1, comm.first_start()))
    w_in, conv_w = comm.weights_first(after=h1)
    qkv3 = qkv_proj(cfg, h1, w_in, _after(cos2, comm.start_rest(after=w_in)), sin2)
    o_sb, tsum = sb_fwd(cfg, qkv3)
    o_dl, lse_tot = dil_fwd(cfg, qkv3)
    mixed = combine_fwd(cfg, o_sb, o_dl, g_sb, g_dl)
    (w_out,), token = comm.weights_rest(0, after=mixed)
    tn = cfg.TN
    mo = _mm(cfg, "mix_out", mixed, w_out, nt=False, grid=(D // tn, 1),
             a_spec=full(S, cfg.DMIX), b_spec=pl.BlockSpec((cfg.DMIX, tn), lambda j, k: (0, j)),
             o_spec=pl.BlockSpec((S, tn), lambda j, k: (0, j)),
             out_shape=jax.ShapeDtypeStruct((S, D), F32), acc_shape=(8, LANE), dep=token)
    x1, h2 = mid_fwd(cfg, x, mo, g2, g3)
    w_up, = comm.forwarded(0, after=h2)
    u, y = ffn_fwd(cfg, h2, w_up, conv_w, conv_b)
    (w_down,), _ = comm.weights_rest(1, after=y)
    tk = cfg.FFP // 2
    f = _mm(cfg, "ffn_down", y, w_down, nt=False, grid=(D // tn, cfg.FFP // tk),
            a_spec=pl.BlockSpec((S, tk), lambda j, k: (0, k)), b_spec=pl.BlockSpec((tk, tn), lambda j, k: (k, j)),
            o_spec=pl.BlockSpec((S, tn), lambda j, k: (0, j)),
            out_shape=jax.ShapeDtypeStruct((S, D), F32), acc_shape=(S, tn))
    dout, df, dg4, loss = final_fwd_bwd(cfg, x1, f, g4, target)

    du, dw_down, dw_up, dconv_w, dconv_b = ffn_bwd(cfg, df, h2, w_down, u, conv_w, conv_b)
    kt = cfg.FFP // tk
    dh2 = _mm(cfg, "d_h2", du, w_up, nt=False, grid=(D // tn, 2 * kt),
              a_spec=pl.BlockSpec((None, S, tk), lambda j, k: (k // kt, 0, k % kt)),
              b_spec=pl.BlockSpec((tk, tn), lambda j, k: (k, j)),
              o_spec=pl.BlockSpec((S, tn), lambda j, k: (0, j)),
              out_shape=jax.ShapeDtypeStruct((S, D), F32), acc_shape=(S, tn),
              dep=comm.pair_start(dict(w_down=dw_down, w_up=dw_up)))
    token = comm.reduce_start({}, after=dh2)
    dx1, dmo, dg3, dg2 = mid_bwd(cfg, dh2, x1, _after(g3, token), dout, mo, g2)

    dmix = _mm(cfg, "d_mixed", dmo, w_out, nt=True, grid=(cfg.DMIX // tn, 1),
               a_spec=full(S, D), b_spec=pl.BlockSpec((tn, D), lambda j, k: (j, 0)),
               o_spec=pl.BlockSpec((S, tn), lambda j, k: (0, j)),
               out_shape=jax.ShapeDtypeStruct((S, cfg.DMIX), F32), acc_shape=(8, LANE))
    dw_out = _mm_tn(cfg, "d_w_out", mixed, dmo, grid=(D // tn,),
                    a_spec=pl.BlockSpec((S, cfg.DMIX), lambda j: (0, 0)),
                    b_spec=pl.BlockSpec((S, tn), lambda j: (0, j)),
                    o_spec=pl.BlockSpec((cfg.DMIX, tn), lambda j: (0, j)),
                    out_shape=jax.ShapeDtypeStruct((cfg.DMIX, D), BF16))
    token = comm.reduce_start(dict(w_out=dw_out))
    do_sb, do_dl, delta, dg_sb, dg_dl = combine_bwd(cfg, dmix, o_sb, o_dl, _after(g_sb, token), g_dl)
    d_sb3 = sb_bwd(cfg, qkv3, do_sb, tsum)
    dqkv3 = dil_bwd(cfg, qkv3, do_dl, delta, lse_tot, cos2, sin2, d_sb3)
    token = comm.reduce_wait(("w_up", "w_down"), after=dqkv3, finish_later=True)
    tkq = min(tn, cfg.DSB)
    kq = cfg.DSB // tkq
    dw_in = _mm_tn(cfg, "d_w_in", h1, dqkv3, grid=(6 * kq,),
                   a_spec=pl.BlockSpec((S, D), lambda j: (0, 0)),
                   b_spec=pl.BlockSpec((None, S, tkq), lambda j: (j // kq, 0, j % kq)),
                   o_spec=pl.BlockSpec((D, tkq), lambda j: (0, j)),
                   out_shape=jax.ShapeDtypeStruct((D, 6 * cfg.DSB), BF16), dep=token)
    comm.reduce_finish(after=dw_in)
    comm.reduce_wait(("w_out",), after=dw_in)
    token = comm.reduce_start(dict(w_in=dw_in))
    dh1 = _mm(cfg, "d_h1", dqkv3, w_in, nt=True, grid=(D // tn, 6),
              a_spec=pl.BlockSpec((None, S, cfg.DSB), lambda j, k: (k, 0, 0)),
              b_spec=pl.BlockSpec((tn, cfg.DSB), lambda j, k: (j, k)),
              o_spec=pl.BlockSpec((S, tn), lambda j, k: (0, j)),
              out_shape=jax.ShapeDtypeStruct((S, D), F32), acc_shape=(S, tn), dep=token)
    grad_x, dg1 = first_bwd(cfg, dh1, x, g1, dx1)
    small = dict(loss=loss, g1=dg1, g2=dg2, g3=dg3, g4=dg4, g_sb=dg_sb, g_dl=dg_dl,
                 conv_b=dconv_b.reshape(1, cfg.FF2P), conv_w=dconv_w.transpose(1, 0, 2).reshape(3, cfg.FF2P))
    return grad_x, small


ANY = pl.BlockSpec(memory_space=pl.ANY)


def _me():
    return lax.axis_index("x"), lax.axis_index("y"), lax.axis_index("c")


def _other_chips(x, y):
    return [(1 - x, y), (x, 1 - y), (1 - x, 1 - y)]


def pad_conv_w(cfg, conv_w, pos):
    r, c = conv_w.shape

    def body(pos_ref, w_ref, full_ref, scr, sem):
        scr[:, :c] = w_ref[...]
        scr[:, c:] = jnp.zeros((r, cfg.FSHP - c), F32)
        cols = pl.ds(pl.multiple_of(pos_ref[0] * cfg.FSHP, LANE), cfg.FSHP)
        cp = pltpu.make_async_copy(scr, full_ref.at[:, cols], sem)
        cp.start()
        cp.wait()

    return pl.pallas_call(
        body, name="pad_conv_w",
        grid_spec=pltpu.PrefetchScalarGridSpec(
            num_scalar_prefetch=1, grid=(1,), in_specs=[pl.BlockSpec((r, c), lambda i, p: (0, 0))], out_specs=ANY,
            scratch_shapes=[pltpu.VMEM((r, cfg.FSHP), F32), pltpu.SemaphoreType.DMA]),
        out_shape=jax.ShapeDtypeStruct(_full_shape(cfg, "conv_w"), F32),
    )(pos, conv_w)


def _tile2(r, c):
    return (256, c) if r % 256 == 0 else (r, 512 if c % 512 == 0 else c)


def cast_into(cfg, name, w, pos, dep=None):
    r, c = w.shape
    _, nr, _, nc = _slab(cfg, name, 0)
    tm, tc = _tile2(r, c)
    wr = nr if tm == r else tm
    assert nc == c and (nr == r or tm == r)
    gap = cfg.FSHP - cfg.FSH if name == "w_down" else 0
    deps = [] if dep is None else [dep]

    def body(pos_ref, w_ref, *rest):
        full_ref, token, scr, sem = rest[len(deps):]
        token[...] = jnp.zeros_like(token)
        tile = w_ref[...]
        if deps:
            tile = tile + rest[0][0:1, 0:1]
        scr[pl.ds(0, tm), :] = tile.astype(BF16)
        if wr > tm:
            scr[pl.ds(tm, wr - tm), :] = jnp.zeros((wr - tm, tc), BF16)
        r0, _, c0, _ = _slab(cfg, name, pos_ref[0])
        rows = pl.ds(pl.multiple_of(r0 + pl.program_id(0) * tm, 16), wr)
        cols = pl.ds(pl.multiple_of(c0 + pl.program_id(1) * tc, LANE), tc)
        cps = [pltpu.make_async_copy(scr.at[pl.ds(0, wr), :], full_ref.at[rows, cols], sem.at[0])]
        if gap:
            scr[pl.ds(wr, gap), :] = jnp.zeros((gap, tc), BF16)
            for h in range(2):
                pad_rows = pl.ds(h * cfg.FSHP + cfg.FSH, gap)
                cps.append(pltpu.make_async_copy(scr.at[pl.ds(wr, gap), :], full_ref.at[pad_rows, cols], sem.at[1 + h]))
        for cp in cps:
            cp.start()
        for cp in cps:
            cp.wait()

    return pl.pallas_call(
        body, name=f"cast_{name}",
        grid_spec=pltpu.PrefetchScalarGridSpec(
            num_scalar_prefetch=1, grid=(r // tm, c // tc),
            in_specs=[pl.BlockSpec((tm, tc), lambda i, j, p: (i, j))]
            + [pl.BlockSpec((8, LANE), lambda i, j, p: (0, 0))] * len(deps),
            out_specs=[ANY, pl.BlockSpec((8, LANE), lambda i, j, p: (0, 0))],
            scratch_shapes=[pltpu.VMEM((wr + gap, tc), BF16), pltpu.SemaphoreType.DMA((3,))]),
        out_shape=[jax.ShapeDtypeStruct(_full_shape(cfg, name), BF16), jax.ShapeDtypeStruct((8, LANE), F32)],
        compiler_params=_cp(cfg, ("arbitrary", "arbitrary")),
    )(pos, w, *deps)


HBM = pl.BlockSpec(memory_space=pltpu.HBM)
SEM = pl.BlockSpec(memory_space=pltpu.SEMAPHORE)
TOKEN = pl.BlockSpec(memory_space=pltpu.VMEM)
EFFECT = pltpu.SideEffectType.DATAFLOW_SIDE_EFFECTING


def _slab(cfg, name, k):
    D = cfg.D
    if name == "w_in":
        cin = 6 * cfg.DSB // N_CHIPS
        return 0, D, k * cin, cin
    if name == "w_out":
        rout = cfg.DMIX // N_CHIPS
        return k * rout, rout, 0, D
    if name == "w_up":
        return k * cfg.FSHP, cfg.FSHP, 0, D
    if name == "conv_w":
        return 0, 3, k * cfg.FSHP, cfg.FSHP
    rdn = cfg.FSH // 2
    return (k // 2) * cfg.FSHP + (k % 2) * rdn, rdn, 0, D


def _full_shape(cfg, name):
    return dict(w_in=(cfg.D, 6 * cfg.DSB), w_out=(cfg.DMIX, cfg.D), w_up=(cfg.FF2P, cfg.D), w_down=(cfg.FFP, cfg.D),
                conv_w=(3, cfg.FF2P))[name]


def _half(cfg, name, ref, k, h):
    r0, nr, c0, nc = _slab(cfg, name, k)
    if name == "conv_w":
        return ref.at[:, pl.ds(c0, nc)]
    return ref.at[pl.ds(r0 + h * (nr // 2), nr // 2), pl.ds(c0, nc)]


def _rows_half(ref, h):
    nr = ref.shape[0] // 2
    return ref.at[pl.ds(h * nr, nr), :]


def _remote(src, dst, send_sem, recv_sem, dev):
    return pltpu.make_async_remote_copy(src_ref=src, dst_ref=dst, send_sem=send_sem, recv_sem=recv_sem,
                                        device_id=dev, device_id_type=MESH)


REST = ("w_out", "w_up", "w_down")
FIRST = (("w_in", "conv_w"),)
GROUPS = (("w_out", "w_up"), ("w_down",))


def _hbm(a):
    return pltpu.with_memory_space_constraint(a, pltpu.HBM)


def gather_start(cfg, tag, groups, fulls, after):
    order = [k for names in groups for k in names]
    n, ng = len(order), len(groups)

    def body(*refs):
        lands = dict(zip(order, refs[:n]))
        sems = refs[n + 1:n + 1 + 2 * ng]
        token = refs[-1]
        x, y, c = _me()
        me = 2 * x + y
        for g, names in enumerate(groups):
            for i, name in enumerate(names):
                mine = _half(cfg, name, lands[name], me, c)
                for j, (px, py) in enumerate(_other_chips(x, y)):
                    _remote(mine, mine, sems[2 * g].at[3 * i + j], sems[2 * g + 1].at[3 * i + j], (px, py, c)).start()
        token[...] = jnp.zeros_like(token)

    ops = [_hbm(fulls[k]) for k in order]
    sem_shapes = [pltpu.SemaphoreType.DMA((3 * len(names),)) for names in groups for _ in range(2)]
    outs = pl.pallas_call(
        body, name=f"gather_start_{tag}",
        in_specs=[HBM] * n + [ANY],
        out_specs=[SEM] * (2 * ng) + [HBM] * n + [TOKEN],
        out_shape=sem_shapes + [pltpu.HBM(a.shape, a.dtype) for a in ops] + [jax.ShapeDtypeStruct((8, LANE), F32)],
        input_output_aliases={i: 2 * ng + i for i in range(n)},
        compiler_params=pltpu.CompilerParams(has_side_effects=EFFECT),
    )(*ops, after)
    thru = dict(zip(order, outs[2 * ng:2 * ng + n]))
    return [(outs[2 * g], outs[2 * g + 1], [thru[k] for k in names]) for g, names in enumerate(groups)], outs[-1]


def gather_wait(cfg, names, ssem, rsem, lands, after):
    n = len(names)

    def body(*refs):
        lands_ = refs[:n]
        ssem_, rsem_ = refs[n], refs[n + 1]
        x, y, c = _me()
        me = 2 * x + y
        for i, name in enumerate(names):
            for j, (px, py) in enumerate(_other_chips(x, y)):
                cp = _remote(_half(cfg, name, lands_[i], me, c), _half(cfg, name, lands_[i], 2 * px + py, c),
                             ssem_.at[3 * i + j], rsem_.at[3 * i + j], (px, py, c))
                cp.wait_send()
                cp.wait_recv()

    return pl.pallas_call(
        body, name="gather_wait_" + "_".join(names),
        in_specs=[HBM] * n + [SEM, SEM, ANY], out_specs=[HBM] * n,
        out_shape=[pltpu.HBM(a.shape, a.dtype) for a in lands],
        input_output_aliases={i: i for i in range(n)},
        compiler_params=pltpu.CompilerParams(has_side_effects=EFFECT),
    )(*lands, ssem, rsem, after)


def gather_finish(cfg, names, lands):
    n = len(names)

    def body(*refs):
        outs = refs[n:2 * n]
        ssem, rsem = refs[2 * n:]
        x, y, c = _me()
        sib = (x, y, 1 - c)
        fwds = []
        for i, name in enumerate(names):
            for j, (px, py) in enumerate(_other_chips(x, y)):
                landed = _half(cfg, name, outs[i], 2 * px + py, c)
                fwds.append(_remote(landed, landed, ssem.at[3 * i + j], rsem.at[3 * i + j], sib))
        for cp in fwds:
            cp.start()
        for i, name in enumerate(names):
            for j, (px, py) in enumerate(_other_chips(x, y)):
                passed = _half(cfg, name, outs[i], 2 * px + py, 1 - c)
                _remote(passed, passed, ssem.at[3 * i + j], rsem.at[3 * i + j], sib).wait_recv()
        for cp in fwds:
            cp.wait_send()

    return pl.pallas_call(
        body, name="gather_finish_" + "_".join(names), in_specs=[ANY] * n, out_specs=[ANY] * n,
        out_shape=[jax.ShapeDtypeStruct(a.shape, a.dtype) for a in lands],
        input_output_aliases={i: i for i in range(n)},
        scratch_shapes=[pltpu.SemaphoreType.DMA((3 * n,)), pltpu.SemaphoreType.DMA((3 * n,))],
    )(*lands)


def forward_start(cfg, names, lands, after):
    n = len(names)

    def body(*refs):
        outs = refs[:n]
        ssem, rsem = refs[n + 1], refs[n + 2]
        token = refs[-1]
        x, y, c = _me()
        for i, name in enumerate(names):
            for j, (px, py) in enumerate(_other_chips(x, y)):
                landed = _half(cfg, name, outs[i], 2 * px + py, c)
                _remote(landed, landed, ssem.at[3 * i + j], rsem.at[3 * i + j], (x, y, 1 - c)).start()
        token[...] = jnp.zeros_like(token)

    ops = [_hbm(a) for a in lands]
    outs = pl.pallas_call(
        body, name="forward_start_" + "_".join(names),
        in_specs=[HBM] * n + [ANY], out_specs=[SEM, SEM] + [HBM] * n + [TOKEN],
        out_shape=[pltpu.SemaphoreType.DMA((3 * n,)), pltpu.SemaphoreType.DMA((3 * n,))]
        + [pltpu.HBM(a.shape, a.dtype) for a in ops] + [jax.ShapeDtypeStruct((8, LANE), F32)],
        input_output_aliases={i: 2 + i for i in range(n)},
        compiler_params=pltpu.CompilerParams(has_side_effects=EFFECT),
    )(*ops, after)
    return outs[0], outs[1], outs[2:2 + n], outs[-1]


def forward_wait(cfg, names, ssem, rsem, lands, after):
    n = len(names)

    def body(*refs):
        outs = refs[:n]
        ssem_, rsem_ = refs[n], refs[n + 1]
        x, y, c = _me()
        for i, name in enumerate(names):
            for j, (px, py) in enumerate(_other_chips(x, y)):
                cp = _remote(_half(cfg, name, outs[i], 2 * px + py, c), _half(cfg, name, outs[i], 2 * px + py, 1 - c),
                             ssem_.at[3 * i + j], rsem_.at[3 * i + j], (x, y, 1 - c))
                cp.wait_send()
                cp.wait_recv()

    return pl.pallas_call(
        body, name="forward_wait_" + "_".join(names),
        in_specs=[HBM] * n + [SEM, SEM, ANY], out_specs=[HBM] * n,
        out_shape=[pltpu.HBM(a.shape, a.dtype) for a in lands],
        input_output_aliases={i: i for i in range(n)},
        compiler_params=pltpu.CompilerParams(has_side_effects=EFFECT),
    )(*lands, ssem, rsem, after)


def pair_send(cfg, grads):
    names = list(grads)
    n = len(names)

    def half_shape(name):
        _, nr, _, nc = _slab(cfg, name, 0)
        return (N_CHIPS, nr // 2, nc)

    def body(*refs):
        srcs, theirs = refs[:n], refs[n:2 * n]
        ssem, rsem = refs[2 * n:]
        x, y, c = _me()
        cps = []
        for i, name in enumerate(names):
            for k in range(N_CHIPS):
                cps.append(_remote(_half(cfg, name, srcs[i], k, 1 - c), theirs[i].at[k],
                                   ssem.at[N_CHIPS * i + k], rsem.at[N_CHIPS * i + k], (x, y, 1 - c)))
        for cp in cps:
            cp.start()
        for cp in cps:
            cp.wait()

    outs = pl.pallas_call(
        body, name="pair_send_" + "_".join(names), in_specs=[ANY] * n, out_specs=[ANY] * n,
        out_shape=[jax.ShapeDtypeStruct(half_shape(name), BF16) for name in names],
        scratch_shapes=[pltpu.SemaphoreType.DMA((N_CHIPS * n,))] * 2,
    )(*[grads[k] for k in names])
    return dict(zip(names, outs))


def pair_start(cfg, grads, after):
    names = list(grads)
    n = len(names)

    def body(*refs):
        srcs, theirs = refs[:n], refs[n:2 * n]
        ssem, rsem = refs[2 * n + 1], refs[2 * n + 2]
        token = refs[-1]
        x, y, c = _me()
        for i, name in enumerate(names):
            for k in range(N_CHIPS):
                _remote(_half(cfg, name, srcs[i], k, 1 - c), theirs[i].at[k],
                        ssem.at[N_CHIPS * i + k], rsem.at[N_CHIPS * i + k], (x, y, 1 - c)).start()
        token[...] = jnp.zeros_like(token)

    def half_shape(name):
        _, nr, _, nc = _slab(cfg, name, 0)
        return (N_CHIPS, nr // 2, nc)

    ops = [_hbm(grads[k]) for k in names] + [_hbm(lax.empty(half_shape(k), BF16)) for k in names]
    outs = pl.pallas_call(
        body, name="pair_start_" + "_".join(names),
        in_specs=[HBM] * (2 * n) + [ANY],
        out_specs=[SEM, SEM] + [HBM] * (2 * n) + [TOKEN],
        out_shape=[pltpu.SemaphoreType.DMA((N_CHIPS * n,)), pltpu.SemaphoreType.DMA((N_CHIPS * n,))]
        + [pltpu.HBM(a.shape, a.dtype) for a in ops] + [jax.ShapeDtypeStruct((8, LANE), F32)],
        input_output_aliases={i: 2 + i for i in range(2 * n)},
        compiler_params=pltpu.CompilerParams(has_side_effects=EFFECT),
    )(*ops, after)
    return outs[0], outs[1], dict(zip(names, outs[2:2 + n])), dict(zip(names, outs[2 + n:2 + 2 * n])), outs[-1]


def pair_wait(cfg, ssem, rsem, grads, theirs, after):
    names = list(grads)
    n = len(names)

    def body(*refs):
        srcs, theirs_ = refs[:n], refs[n:2 * n]
        ssem_, rsem_ = refs[2 * n], refs[2 * n + 1]
        x, y, c = _me()
        for i, name in enumerate(names):
            for k in range(N_CHIPS):
                cp = _remote(_half(cfg, name, srcs[i], k, 1 - c), theirs_[i].at[k],
                             ssem_.at[N_CHIPS * i + k], rsem_.at[N_CHIPS * i + k], (x, y, 1 - c))
                cp.wait_send()
                cp.wait_recv()

    ops = [grads[k] for k in names] + [theirs[k] for k in names]
    outs = pl.pallas_call(
        body, name="pair_wait_" + "_".join(names),
        in_specs=[HBM] * (2 * n) + [SEM, SEM, ANY], out_specs=[HBM] * (2 * n),
        out_shape=[pltpu.HBM(a.shape, a.dtype) for a in ops],
        input_output_aliases={i: i for i in range(2 * n)},
        compiler_params=pltpu.CompilerParams(has_side_effects=EFFECT),
    )(*ops, ssem, rsem, after)
    return dict(zip(names, outs[:n])), dict(zip(names, outs[n:]))


def pair_sum(cfg, name, grad, theirs, pos):
    _, r, c = theirs.shape
    tm, tc = _tile2(r, c)

    ni, nj = r // tm, c // tc
    total = N_CHIPS * ni * nj

    def body(pos_ref, g_ref, t_ref, o_ref, scr, sem):
        step = (pl.program_id(0) * ni + pl.program_id(1)) * nj + pl.program_id(2)

        def fetch(flat, slot):
            k, rem = flat // (ni * nj), flat % (ni * nj)
            r0, nr, c0, _ = _slab(cfg, name, k)
            rows = pl.ds(pl.multiple_of(r0 + pos_ref[1] * (nr // 2) + (rem // nj) * tm, 16), tm)
            cols = pl.ds(pl.multiple_of(c0 + (rem % nj) * tc, LANE), tc)
            return pltpu.make_async_copy(g_ref.at[rows, cols], scr.at[slot], sem.at[slot])

        @pl.when(step == 0)
        def _():
            fetch(0, 0).start()

        @pl.when(step + 1 < total)
        def _():
            fetch(step + 1, (step + 1) % 2).start()

        fetch(step, step % 2).wait()
        o_ref[...] = (scr[step % 2].astype(F32) + t_ref[...].astype(F32)).astype(BF16)

    blk = pl.BlockSpec((None, tm, tc), lambda k, i, j, p: (k, i, j))
    return pl.pallas_call(
        body, name=f"pair_sum_{name}",
        grid_spec=pltpu.PrefetchScalarGridSpec(
            num_scalar_prefetch=1, grid=(N_CHIPS, ni, nj), in_specs=[ANY, blk], out_specs=blk,
            scratch_shapes=[pltpu.VMEM((2, tm, tc), BF16), pltpu.SemaphoreType.DMA((2,))]),
        out_shape=jax.ShapeDtypeStruct(theirs.shape, BF16),
        compiler_params=_cp(cfg, ("arbitrary",) * 3),
    )(pos, grad, theirs)


def scatter_start(cfg, pres, after):
    names = list(pres)
    n = len(names)

    def body(*refs):
        srcs, lands = refs[:n], refs[n:2 * n]
        ssem, rsem = refs[2 * n + 1], refs[2 * n + 2]
        token = refs[-1]
        x, y, c = _me()
        for i in range(n):
            for j, (px, py) in enumerate(_other_chips(x, y)):
                _remote(srcs[i].at[2 * px + py], lands[i].at[j], ssem.at[3 * i + j], rsem.at[3 * i + j], (px, py, c)).start()
        token[...] = jnp.zeros_like(token)

    lands = [lax.empty((3,) + pres[k].shape[1:], BF16) for k in names]
    ops = [_hbm(a) for a in [pres[k] for k in names] + lands]
    outs = pl.pallas_call(
        body, name="scatter_start_" + "_".join(names),
        in_specs=[HBM] * (2 * n) + [ANY],
        out_specs=[SEM, SEM] + [HBM] * (2 * n) + [TOKEN],
        out_shape=[pltpu.SemaphoreType.DMA((3 * n,)), pltpu.SemaphoreType.DMA((3 * n,))]
        + [pltpu.HBM(a.shape, a.dtype) for a in ops] + [jax.ShapeDtypeStruct((8, LANE), F32)],
        input_output_aliases={i: 2 + i for i in range(2 * n)},
        compiler_params=pltpu.CompilerParams(has_side_effects=EFFECT),
    )(*ops, after)
    return outs[0], outs[1], dict(zip(names, outs[2:2 + n])), dict(zip(names, outs[2 + n:2 + 2 * n])), outs[-1]


def scatter_wait(cfg, ssem, rsem, pres, lands, after):
    names = list(pres)
    n = len(names)

    def body(*refs):
        srcs, lands_ = refs[:n], refs[n:2 * n]
        ssem_, rsem_ = refs[2 * n], refs[2 * n + 1]
        x, y, c = _me()
        for i in range(n):
            for j, (px, py) in enumerate(_other_chips(x, y)):
                cp = _remote(srcs[i].at[2 * px + py], lands_[i].at[j], ssem_.at[3 * i + j], rsem_.at[3 * i + j], (px, py, c))
                cp.wait_send()
                cp.wait_recv()

    ops = [pres[k] for k in names] + [lands[k] for k in names]
    outs = pl.pallas_call(
        body, name="scatter_wait_" + "_".join(names),
        in_specs=[HBM] * (2 * n) + [SEM, SEM, ANY], out_specs=[HBM] * (2 * n),
        out_shape=[pltpu.HBM(a.shape, a.dtype) for a in ops],
        input_output_aliases={i: i for i in range(2 * n)},
        compiler_params=pltpu.CompilerParams(has_side_effects=EFFECT),
    )(*ops, ssem, rsem, after)
    return dict(zip(names, outs[:n])), dict(zip(names, outs[n:]))


def sum_landed(cfg, name, pre, land, pos):
    _, r, c = pre.shape
    tm, tc = _tile2(r, c)
    nrt = r // tm

    def body(pos_ref, p_ref, l_ref, o_ref):
        acc = p_ref[...].astype(F32)
        for j in range(3):
            acc = acc + l_ref[j].astype(F32)
        o_ref[...] = acc

    return pl.pallas_call(
        body, name=f"sum_landed_{name}",
        grid_spec=pltpu.PrefetchScalarGridSpec(
            num_scalar_prefetch=1, grid=(nrt, c // tc),
            in_specs=[pl.BlockSpec((None, tm, tc), lambda i, j, p: (p[0], i, j)),
                      pl.BlockSpec((3, tm, tc), lambda i, j, p: (0, i, j))],
            out_specs=pl.BlockSpec((tm, tc), lambda i, j, p: (p[1] * nrt + i, j))),
        out_shape=jax.ShapeDtypeStruct((2 * r, c), F32), compiler_params=_cp(cfg, ("parallel", "parallel")),
    )(pos, pre, land)


def half_swap(cfg, sums):
    names = list(sums)
    n = len(names)

    def body(*refs):
        outs = refs[n:2 * n]
        ssem, rsem = refs[2 * n:]
        x, y, c = _me()
        cps = [_remote(_rows_half(outs[i], c), _rows_half(outs[i], c), ssem.at[i], rsem.at[i], (x, y, 1 - c))
               for i in range(n)]
        for cp in cps:
            cp.start()
        for i in range(n):
            theirs = _rows_half(outs[i], 1 - c)
            _remote(theirs, theirs, ssem.at[i], rsem.at[i], (x, y, 1 - c)).wait_recv()
        for cp in cps:
            cp.wait_send()

    outs = pl.pallas_call(
        body, name="half_swap_" + "_".join(names), in_specs=[ANY] * n, out_specs=[ANY] * n,
        out_shape=[jax.ShapeDtypeStruct(sums[k].shape, F32) for k in names],
        input_output_aliases={i: i for i in range(n)},
        scratch_shapes=[pltpu.SemaphoreType.DMA((n,))] * 2,
    )(*[sums[k] for k in names])
    return dict(zip(names, outs))


def half_swap_start(cfg, sums):
    names = list(sums)
    n = len(names)

    def body(*refs):
        outs = refs[:n]
        ssem, rsem = refs[n + 1], refs[n + 2]
        token = refs[-1]
        x, y, c = _me()
        for i in range(n):
            mine = _rows_half(outs[i], c)
            _remote(mine, mine, ssem.at[i], rsem.at[i], (x, y, 1 - c)).start()
        token[...] = jnp.zeros_like(token)

    ops = [_hbm(sums[k]) for k in names]
    outs = pl.pallas_call(
        body, name="half_swap_start_" + "_".join(names),
        in_specs=[HBM] * n + [ANY], out_specs=[SEM, SEM] + [HBM] * n + [TOKEN],
        out_shape=[pltpu.SemaphoreType.DMA((n,)), pltpu.SemaphoreType.DMA((n,))]
        + [pltpu.HBM(a.shape, a.dtype) for a in ops] + [jax.ShapeDtypeStruct((8, LANE), F32)],
        input_output_aliases={i: 2 + i for i in range(n)},
        compiler_params=pltpu.CompilerParams(has_side_effects=EFFECT),
    )(*ops, jnp.zeros((8, LANE), F32))
    return outs[0], outs[1], dict(zip(names, outs[2:2 + n])), outs[-1]


def half_swap_wait(cfg, ssem, rsem, sums, after):
    names = list(sums)
    n = len(names)

    def body(*refs):
        outs = refs[:n]
        ssem_, rsem_ = refs[n], refs[n + 1]
        x, y, c = _me()
        for i in range(n):
            cp = _remote(_rows_half(outs[i], c), _rows_half(outs[i], 1 - c), ssem_.at[i], rsem_.at[i], (x, y, 1 - c))
            cp.wait_send()
            cp.wait_recv()

    outs = pl.pallas_call(
        body, name="half_swap_wait_" + "_".join(names),
        in_specs=[HBM] * n + [SEM, SEM, ANY], out_specs=[HBM] * n,
        out_shape=[pltpu.HBM(sums[k].shape, sums[k].dtype) for k in names],
        input_output_aliases={i: i for i in range(n)},
        compiler_params=pltpu.CompilerParams(has_side_effects=EFFECT),
    )(*[sums[k] for k in names], ssem, rsem, after)
    return dict(zip(names, outs))


class MeshWeights:
    def __init__(self, cfg, w_sh):
        self.cfg = cfg
        self.pos = jnp.stack([2 * lax.axis_index("x") + lax.axis_index("y"), lax.axis_index("c")]).astype(jnp.int32)
        self.w_sh = w_sh
        self.full = {"w_in": cast_into(cfg, "w_in", w_sh["w_in"], self.pos)[0],
                     "conv_w": pad_conv_w(cfg, w_sh["conv_w"], self.pos)}
        self.inflight = {}
        self.forwards = {}
        self.grads = {}

    def first_start(self):
        cfg = self.cfg
        self.first, token = gather_start(cfg, "first", FIRST, self.full, jnp.zeros((8, LANE), F32))
        for k in REST:
            self.full[k], token = cast_into(cfg, k, self.w_sh[k], self.pos, dep=token)
        return token

    def weights_first(self, after):
        cfg = self.cfg
        ssem, rsem, lands = self.first[0]
        w_in, conv_w = gather_wait(cfg, FIRST[0], ssem, rsem, lands, after)
        return gather_finish(cfg, ("w_in",), [w_in])[0], conv_w

    def start_rest(self, after):
        self.rest, token = gather_start(self.cfg, "rest", GROUPS, self.full, after)
        return token

    def weights_rest(self, group, after):
        cfg = self.cfg
        names = GROUPS[group]
        ssem, rsem, lands = self.rest[group]
        lands = dict(zip(names, gather_wait(cfg, names, ssem, rsem, lands, after)))
        now = [k for k in names if k != "w_up"]
        later = [k for k in names if k == "w_up"]
        ready = gather_finish(cfg, tuple(now), [lands[k] for k in now])
        if not later:
            return tuple(ready), None
        out = forward_start(cfg, tuple(later), [lands[k] for k in later], ready[0])
        self.forwards[group] = (tuple(later),) + tuple(out[:3])
        return tuple(ready), out[3]

    def forwarded(self, group, after):
        names, ssem, rsem, lands = self.forwards.pop(group)
        return tuple(forward_wait(self.cfg, names, ssem, rsem, lands, after))

    def pair_start(self, grads):
        out = pair_start(self.cfg, grads, jnp.zeros((8, LANE), F32))
        self.pairs = out[:4]
        return out[4]

    def reduce_start(self, grads, after=None):
        theirs = pair_send(self.cfg, grads) if grads else {}
        if after is not None:
            early, early_theirs = pair_wait(self.cfg, *self.pairs, after)
            grads, theirs = {**early, **grads}, {**early_theirs, **theirs}
        pres = {k: pair_sum(self.cfg, k, grads[k], theirs[k], self.pos) for k in grads}
        out = scatter_start(self.cfg, pres, jnp.zeros((8, LANE), F32))
        self.inflight[tuple(sorted(grads))] = out[:4]
        return out[4]

    def reduce_wait(self, names, after, finish_later=False):
        cfg = self.cfg
        pres, lands = scatter_wait(cfg, *self.inflight.pop(tuple(sorted(names))), after)
        sums = {k: sum_landed(cfg, k, pres[k], lands[k], self.pos) for k in names}
        if not finish_later:
            self.grads.update(half_swap(cfg, sums))
            return None
        out = half_swap_start(cfg, sums)
        self.swaps = out[:3]
        return out[3]

    def reduce_finish(self, after):
        self.grads.update(half_swap_wait(self.cfg, *self.swaps, after))


def allreduce_small(cfg, vec):
    R = vec.shape[0]

    def body(v_ref, o_ref, buf, send_sems, recv_sems):
        x, y, c = _me()
        me = 4 * x + 2 * y + c
        buf[me] = v_ref[...]
        sends = []
        for k in range(1, N_DEV):
            px, py, pc = x ^ (k >> 2), y ^ ((k >> 1) & 1), c ^ (k & 1)
            sends.append(pltpu.make_async_remote_copy(
                src_ref=v_ref, dst_ref=buf.at[me], send_sem=send_sems.at[k], recv_sem=recv_sems.at[k],
                device_id=(px, py, pc), device_id_type=MESH))
        for cp in sends:
            cp.start()
        for k in range(1, N_DEV):
            px, py, pc = x ^ (k >> 2), y ^ ((k >> 1) & 1), c ^ (k & 1)
            pltpu.make_async_remote_copy(
                src_ref=v_ref, dst_ref=buf.at[4 * px + 2 * py + pc], send_sem=send_sems.at[k],
                recv_sem=recv_sems.at[k], device_id=(px, py, pc), device_id_type=MESH).wait_recv()
        for cp in sends:
            cp.wait_send()
        acc = buf[0]
        for j in range(1, N_DEV):
            acc = acc + buf[j]
        o_ref[...] = acc

    return pl.pallas_call(
        body, name="allreduce_small",
        in_specs=[pl.BlockSpec(memory_space=pltpu.VMEM)], out_specs=pl.BlockSpec(memory_space=pltpu.VMEM),
        out_shape=jax.ShapeDtypeStruct((R, LANE), F32),
        scratch_shapes=[pltpu.VMEM((N_DEV, R, LANE), F32), pltpu.SemaphoreType.DMA((N_DEV,)),
                        pltpu.SemaphoreType.DMA((N_DEV,))],
    )(vec)


def _adamw_update(w_ref, m_ref, v_ref, g, outs):
    g_out, d_out, m_out, v_out = outs
    m_new = ADAM_B1 * m_ref[...] + (1.0 - ADAM_B1) * g
    v_new = ADAM_B2 * v_ref[...] + (1.0 - ADAM_B2) * jnp.square(g)
    m_hat = m_new / (1.0 - ADAM_B1 ** ADAM_STEP)
    v_hat = v_new / (1.0 - ADAM_B2 ** ADAM_STEP)
    g_out[...] = g
    d_out[...] = -ADAM_LR * (m_hat / (jnp.sqrt(v_hat) + ADAM_EPS) + ADAM_WD * w_ref[...])
    m_out[...] = m_new
    v_out[...] = v_new


def adamw_small(cfg, params):
    names = list(params)
    n = len(names)

    def body(*refs):
        for i in range(n):
            w_ref, m_ref, v_ref, g_ref = refs[4 * i:4 * i + 4]
            _adamw_update(w_ref, m_ref, v_ref, g_ref[...], refs[4 * n + 4 * i:4 * n + 4 * i + 4])

    outs = pl.pallas_call(
        body, name="adamw_small",
        out_shape=[jax.ShapeDtypeStruct(params[k][0].shape, F32) for k in names for _ in range(4)],
    )(*[a for k in names for a in params[k]])
    return {k: list(outs[4 * i:4 * i + 4]) for i, k in enumerate(names)}


def adamw(cfg, name, w, m, v, g_parts, tile):
    r, c = w.shape
    tm, tc = tile[0] or r, tile[1] or c
    assert tc == c or all(g.shape[1] == c for g in g_parts)
    n = len(g_parts)

    def body(*refs):
        w_ref, m_ref, v_ref = refs[:3]
        g_refs = refs[3:3 + n]
        g = g_refs[0][:, :tc]
        for gr in g_refs[1:]:
            g = g + gr[:, :tc]
        _adamw_update(w_ref, m_ref, v_ref, g, refs[3 + n:])

    blk = pl.BlockSpec((tm, tc), lambda i, j: (i, j))
    return pl.pallas_call(
        body, name=f"adamw_{name}", grid=(r // tm, c // tc),
        in_specs=[blk] * 3 + [pl.BlockSpec((tm, tc if tc < c else g.shape[1]), lambda i, j: (i, j)) for g in g_parts],
        out_specs=[blk] * 4, out_shape=[jax.ShapeDtypeStruct((r, c), F32)] * 4,
        compiler_params=_cp(cfg, ("parallel", "parallel")),
    )(w, m, v, *g_parts)


SMALL_ORDER = ("loss", "g1", "g2", "g3", "g4", "g_sb", "g_dl", "conv_b", "conv_w")


def pack_small(small):
    rows = []
    for k in SMALL_ORDER:
        a = small[k].reshape(-1, LANE)
        rows.append(a)
    flat = jnp.concatenate(rows, axis=0)
    pad = (-flat.shape[0]) % 8
    return jnp.pad(flat, ((0, pad), (0, 0))), [r.shape[0] for r in rows]


def unpack_small(red, small, counts):
    out, at = {}, 0
    for k, n in zip(SMALL_ORDER, counts):
        out[k] = red[at:at + n].reshape(small[k].shape)
        at += n
    return out


def pad_ff(cfg, a):
    r = a.shape[0]
    return jnp.pad(a.reshape(r, N_CHIPS, cfg.FSH), ((0, 0), (0, 0), (0, cfg.FSHP - cfg.FSH))).reshape(r, cfg.FF2P)


def step(cfg, x, target, gains, w_sh, conv_b, m_all, v_all):
    chip = 2 * lax.axis_index("x") + lax.axis_index("y")
    comm = MeshWeights(cfg, w_sh)
    grad_x, small = local_step(cfg, comm, x, target, gains["g1"], gains["g2"], gains["g3"], gains["g4"],
                               gains["g_sb"], gains["g_dl"], pad_ff(cfg, conv_b))

    packed, counts = pack_small(small)
    summed = allreduce_small(cfg, packed)
    comm.reduce_wait(("w_in",), after=summed)
    red = unpack_small(summed, small, counts)

    names = ("w_in", "w_out", "w_up", "w_down")
    up_rows = max(t for t in range(SUB, 513, SUB) if cfg.FSH % t == 0)
    tms = dict(w_in=(cfg.TM, None), w_out=(cfg.TM, None), w_up=(up_rows, None), w_down=(None, cfg.TN // 2))
    res = {}
    for n in names:
        res[n] = adamw(cfg, n, w_sh[n], m_all[n], v_all[n], [comm.grads[n]], tms[n])
    g_cw = lax.dynamic_slice_in_dim(red["conv_w"].reshape(3, N_CHIPS, cfg.FSHP), chip, 1, axis=1)[:, 0, :cfg.FSH]
    g_cb = red["conv_b"].reshape(1, N_CHIPS, cfg.FSHP)[:, :, :cfg.FSH].reshape(1, N_CHIPS * cfg.FSH)
    smalls = {"conv_w": (w_sh["conv_w"], g_cw), "conv_b": (conv_b, g_cb)}
    smalls.update({k: (gains[k], red[k]) for k in ("g1", "g2", "g3", "g4", "g_sb", "g_dl")})
    res.update(adamw_small(cfg, {k: (w, m_all[k], v_all[k], g) for k, (w, g) in smalls.items()}))
    return red["loss"][0, 0], grad_x, res


PARAMS = ("pre_mix_gain", "post_mix_gain", "pre_ffn_gain", "post_ffn_gain", "w_in", "sb_out_gain", "dil_out_gain",
          "w_out", "w_up", "conv_w", "conv_b", "w_down")
SHORT = dict(pre_mix_gain="g1", post_mix_gain="g2", pre_ffn_gain="g3", post_ffn_gain="g4", sb_out_gain="g_sb",
             dil_out_gain="g_dl", w_in="w_in", w_out="w_out", w_up="w_up", conv_w="conv_w", conv_b="conv_b",
             w_down="w_down")


def kernel(x, pre_mix_gain, post_mix_gain, pre_ffn_gain, post_ffn_gain, w_in, sb_out_gain, dil_out_gain, w_out, w_up, conv_w, conv_b, w_down, loss_target, m_pre_mix_gain, m_post_mix_gain, m_pre_ffn_gain, m_post_ffn_gain, m_w_in, m_sb_out_gain, m_dil_out_gain, m_w_out, m_w_up, m_conv_w, m_conv_b, m_w_down, v_pre_mix_gain, v_post_mix_gain, v_pre_ffn_gain, v_post_ffn_gain, v_w_in, v_sb_out_gain, v_dil_out_gain, v_w_out, v_w_up, v_conv_w, v_conv_b, v_w_down):
    cfg = CFG
    w = dict(zip(PARAMS, (pre_mix_gain, post_mix_gain, pre_ffn_gain, post_ffn_gain, w_in, sb_out_gain, dil_out_gain,
                          w_out, w_up, conv_w, conv_b, w_down)))
    m = dict(zip(PARAMS, (m_pre_mix_gain, m_post_mix_gain, m_pre_ffn_gain, m_post_ffn_gain, m_w_in, m_sb_out_gain,
                          m_dil_out_gain, m_w_out, m_w_up, m_conv_w, m_conv_b, m_w_down)))
    v = dict(zip(PARAMS, (v_pre_mix_gain, v_post_mix_gain, v_pre_ffn_gain, v_post_ffn_gain, v_w_in, v_sb_out_gain,
                          v_dil_out_gain, v_w_out, v_w_up, v_conv_w, v_conv_b, v_w_down)))
    sq = lambda a: a.reshape(a.shape[1:])
    ws = {SHORT[k]: sq(a) if a.ndim == 3 else a for k, a in w.items()}
    ms = {SHORT[k]: sq(a) if a.ndim == 3 else a for k, a in m.items()}
    vs = {SHORT[k]: sq(a) if a.ndim == 3 else a for k, a in v.items()}
    for d in (ws, ms, vs):
        d["w_up"] = d["w_up"].T
    gains = {k: ws[k] for k in ("g1", "g2", "g3", "g4", "g_sb", "g_dl")}
    w_sh = {k: ws[k] for k in ("w_in", "w_out", "w_up", "conv_w", "w_down")}
    loss, grad_x, res = step(cfg, sq(x), sq(loss_target), gains, w_sh, ws["conv_b"], ms, vs)
    res["w_up"] = [a.T for a in res["w_up"]]
    outs = [loss, grad_x.reshape(x.shape)]
    for i in range(4):
        for k in PARAMS:
            outs.append(res[SHORT[k]][i].reshape(w[k].shape))
    return tuple(outs)
```

```python
import functools
import math
from typing import NamedTuple

import jax
import jax.numpy as jnp
from jax import lax
from jax.experimental import pallas as pl
from jax.experimental.pallas import tpu as pltpu

F32 = jnp.float32
BF16 = jnp.bfloat16
MESH = pl.DeviceIdType.MESH

ROPE_THETA = 10000.0
RMS_EPS = 1e-6
ADAM_LR = 0.001
ADAM_B1 = 0.9
ADAM_B2 = 0.999
ADAM_EPS = 1e-08
ADAM_WD = 0.01
ADAM_STEP = 10
GELU_C = math.sqrt(2.0 / math.pi)
NEG_BIG = -1e30
LANE = 128
N_CHIPS = 4
N_DEV = 8


class Cfg(NamedTuple):
    S: int = 2048
    D: int = 2048
    DH: int = 128
    HSB: int = 8
    HDL: int = 8
    QB: int = 128
    SBT: int = 256
    SBH: int = 4
    SBHB: int = 4
    branches: tuple = ((128, 1), (512, 4), (2048, 16))
    FSH: int = 2752
    FSHP: int = 2816
    TM: int = 256
    TNF: int = 256
    FCH: int = 512
    TN: int = 512
    VMEM_MB: int = 56

    @property
    def DSB(self):
        return self.HSB * self.DH

    @property
    def DDL(self):
        return self.HDL * self.DH

    @property
    def DMIX(self):
        return self.DSB + self.DDL

    @property
    def FFP(self):
        return 2 * self.FSHP

    @property
    def FF2P(self):
        return 4 * self.FSHP


CFG = Cfg()


def _cp(cfg, sem=None):
    return pltpu.CompilerParams(dimension_semantics=sem, vmem_limit_bytes=cfg.VMEM_MB * 2**20)


def _dot(a, b):
    return jnp.dot(a, b, preferred_element_type=F32)


def _dot_nt(a, b):
    return lax.dot_general(a, b, (((1,), (1,)), ((), ())), preferred_element_type=F32)


def _dot_tn(a, b):
    return lax.dot_general(a, b, (((0,), (0,)), ((), ())), preferred_element_type=F32)


def _dot_split(x, u):
    hi = x.astype(BF16)
    lo = (x - hi.astype(F32)).astype(BF16)
    return _dot(hi, u) + _dot(lo, u)


def _rstd(x):
    return lax.rsqrt(jnp.mean(x * x, axis=-1, keepdims=True) + RMS_EPS)


def _rms_bwd(dy, x, g):
    r = _rstd(x)
    xh = x * r
    dxh = dy * g
    dx = r * (dxh - xh * jnp.mean(dxh * xh, axis=-1, keepdims=True))
    return dx, dy * xh


def _gelu(x):
    t = jnp.tanh(GELU_C * (x + 0.044715 * (x * x * x)))
    return 0.5 * x * (1.0 + t), t


def _gelu_grad(x, t):
    return 0.5 * (1.0 + t) + 0.5 * x * (1.0 - t * t) * (GELU_C * (1.0 + 3 * 0.044715 * (x * x)))


def _row(cfg, w):
    return pl.BlockSpec((cfg.TM, w), lambda i: (i, 0))


def _vec(w):
    return pl.BlockSpec((1, w), lambda i: (0, 0))


def rms_fwd(cfg, x, g):
    S, D = x.shape

    def body(x_ref, g_ref, h_ref):
        xv = x_ref[...]
        h_ref[...] = (xv * _rstd(xv) * g_ref[...]).astype(BF16)

    return pl.pallas_call(
        body, name="rms_fwd", grid=(S // cfg.TM,),
        in_specs=[_row(cfg, D), _vec(D)], out_specs=_row(cfg, D),
        out_shape=jax.ShapeDtypeStruct((S, D), BF16), compiler_params=_cp(cfg, ("parallel",)),
    )(x, g)


def mid_fwd(cfg, x, mo, g_post, g_pre):
    S, D = x.shape

    def body(x_ref, mo_ref, gp_ref, gn_ref, x1_ref, h2_ref):
        mo_v = mo_ref[...]
        x1 = x_ref[...] + mo_v * _rstd(mo_v) * gp_ref[...]
        x1_ref[...] = x1
        h2_ref[...] = (x1 * _rstd(x1) * gn_ref[...]).astype(BF16)

    return pl.pallas_call(
        body, name="mid_fwd", grid=(S // cfg.TM,),
        in_specs=[_row(cfg, D), _row(cfg, D), _vec(D), _vec(D)],
        out_specs=[_row(cfg, D), _row(cfg, D)],
        out_shape=[jax.ShapeDtypeStruct((S, D), F32), jax.ShapeDtypeStruct((S, D), BF16)],
        compiler_params=_cp(cfg, ("parallel",)),
    )(x, mo, g_post, g_pre)


def final_fwd_bwd(cfg, x1, f, g_post, target):
    S, D = x1.shape

    def body(x1_ref, f_ref, g_ref, t_ref, dout_ref, df_ref, dg_ref, loss_ref):
        @pl.when(pl.program_id(0) == 0)
        def _():
            dg_ref[...] = jnp.zeros_like(dg_ref)
            loss_ref[...] = jnp.zeros_like(loss_ref)

        fv = f_ref[...]
        g = g_ref[...]
        out = x1_ref[...] + fv * _rstd(fv) * g
        err = out - t_ref[...]
        loss_ref[...] += 0.5 * jnp.sum(jnp.mean(err * err, axis=-1, keepdims=True), axis=0, keepdims=True)
        dout = err * (1.0 / D)
        dout_ref[...] = dout
        df, dgx = _rms_bwd(dout, fv, g)
        df_ref[...] = df.astype(BF16)
        dg_ref[...] += jnp.sum(dgx, axis=0, keepdims=True)

    return pl.pallas_call(
        body, name="final_fwd_bwd", grid=(S // cfg.TM,),
        in_specs=[_row(cfg, D), _row(cfg, D), _vec(D), _row(cfg, D)],
        out_specs=[_row(cfg, D), _row(cfg, D), _vec(D), _vec(LANE)],
        out_shape=[jax.ShapeDtypeStruct((S, D), F32), jax.ShapeDtypeStruct((S, D), BF16),
                   jax.ShapeDtypeStruct((1, D), F32), jax.ShapeDtypeStruct((1, LANE), F32)],
        compiler_params=_cp(cfg, ("arbitrary",)),
    )(x1, f, g_post, target)


def mid_bwd(cfg, dh2, x1, g_pre, dout, mo, g_post):
    S, D = x1.shape

    def body(dh_ref, x1_ref, gn_ref, do_ref, mo_ref, gp_ref, dx1_ref, dmo_ref, dgn_ref, dgp_ref):
        @pl.when(pl.program_id(0) == 0)
        def _():
            dgn_ref[...] = jnp.zeros_like(dgn_ref)
            dgp_ref[...] = jnp.zeros_like(dgp_ref)

        dx, dgx = _rms_bwd(dh_ref[...], x1_ref[...], gn_ref[...])
        dx1 = do_ref[...] + dx
        dx1_ref[...] = dx1
        dgn_ref[...] += jnp.sum(dgx, axis=0, keepdims=True)
        dmo, dgy = _rms_bwd(dx1, mo_ref[...], gp_ref[...])
        dmo_ref[...] = dmo.astype(BF16)
        dgp_ref[...] += jnp.sum(dgy, axis=0, keepdims=True)

    return pl.pallas_call(
        body, name="mid_bwd", grid=(S // cfg.TM,),
        in_specs=[_row(cfg, D), _row(cfg, D), _vec(D), _row(cfg, D), _row(cfg, D), _vec(D)],
        out_specs=[_row(cfg, D), _row(cfg, D), _vec(D), _vec(D)],
        out_shape=[jax.ShapeDtypeStruct((S, D), F32), jax.ShapeDtypeStruct((S, D), BF16),
                   jax.ShapeDtypeStruct((1, D), F32), jax.ShapeDtypeStruct((1, D), F32)],
        compiler_params=_cp(cfg, ("arbitrary",)),
    )(dh2, x1, g_pre, dout, mo, g_post)


def first_bwd(cfg, dh1, x, g_pre, dx1):
    S, D = x.shape

    def body(dh_ref, x_ref, g_ref, r_ref, dx_ref, dg_ref):
        @pl.when(pl.program_id(0) == 0)
        def _():
            dg_ref[...] = jnp.zeros_like(dg_ref)

        dx, dgx = _rms_bwd(dh_ref[...], x_ref[...], g_ref[...])
        dx_ref[...] = r_ref[...] + dx
        dg_ref[...] += jnp.sum(dgx, axis=0, keepdims=True)

    return pl.pallas_call(
        body, name="first_bwd", grid=(S // cfg.TM,),
        in_specs=[_row(cfg, D), _row(cfg, D), _vec(D), _row(cfg, D)],
        out_specs=[_row(cfg, D), _vec(D)],
        out_shape=[jax.ShapeDtypeStruct((S, D), F32), jax.ShapeDtypeStruct((1, D), F32)],
        compiler_params=_cp(cfg, ("arbitrary",)),
    )(dh1, x, g_pre, dx1)


def _mm(cfg, name, a, b, *, nt, a_spec, b_spec, o_spec, grid, out_shape, acc_shape, dep=None):
    nk = grid[-1]
    dot = _dot_nt if nt else _dot
    deps = [] if dep is None else [dep]

    def body(a_ref, b_ref, *rest):
        o_ref, acc_ref = rest[-2:]
        k = pl.program_id(len(grid) - 1)
        part = dot(a_ref[...], b_ref[...])
        if deps:
            part = part + rest[0][0:1, 0:1]
        if nk == 1:
            o_ref[...] = part.astype(o_ref.dtype)
            return

        @pl.when(k == 0)
        def _():
            acc_ref[...] = part

        @pl.when(k > 0)
        def _():
            acc_ref[...] += part

        @pl.when(k == nk - 1)
        def _():
            o_ref[...] = acc_ref[...].astype(o_ref.dtype)

    sem = ("parallel",) * (len(grid) - 1) + ("arbitrary",)
    dep_specs = [pl.BlockSpec((8, LANE), lambda *_: (0, 0))] * len(deps)
    return pl.pallas_call(
        body, name=name, grid=grid, in_specs=[a_spec, b_spec] + dep_specs, out_specs=o_spec, out_shape=out_shape,
        scratch_shapes=[pltpu.VMEM(acc_shape, F32)], compiler_params=_cp(cfg, sem),
    )(a, b, *deps)


def _mm_tn(cfg, name, a, b, *, a_spec, b_spec, o_spec, grid, out_shape, dep=None):
    deps = [] if dep is None else [dep]

    def body(a_ref, b_ref, *rest):
        part = _dot_tn(a_ref[...], b_ref[...])
        if deps:
            part = part + rest[0][0:1, 0:1]
        rest[-1][...] = part.astype(rest[-1].dtype)

    dep_specs = [pl.BlockSpec((8, LANE), lambda *_: (0, 0))] * len(deps)
    return pl.pallas_call(
        body, name=name, grid=grid, in_specs=[a_spec, b_spec] + dep_specs, out_specs=o_spec, out_shape=out_shape,
        compiler_params=_cp(cfg, ("parallel",) * len(grid)),
    )(a, b, *deps)


def qkv_proj(cfg, h1, w_in, cos2, sin2):
    S, D = h1.shape
    tn = 2 * cfg.DH
    per = cfg.DSB // tn
    assert cfg.DSB == cfg.DDL
    nblk = 6 * per

    def body(a_ref, b_ref, c_ref, s_ref, o_ref):
        j = pl.program_id(0)
        acc = _dot(a_ref[...], b_ref[...])
        rope = jnp.logical_and(j >= 3 * per, j < 5 * per)

        @pl.when(rope)
        def _():
            for c in range(tn // cfg.DH):
                xh = acc[:, c * cfg.DH:(c + 1) * cfg.DH]
                o_ref[:, c * cfg.DH:(c + 1) * cfg.DH] = (
                    xh * c_ref[...] + pltpu.roll(xh, cfg.DH // 2, 1) * s_ref[...]).astype(BF16)

        @pl.when(jnp.logical_not(rope))
        def _():
            o_ref[...] = acc.astype(BF16)

    return pl.pallas_call(
        body, name="qkv_proj", grid=(nblk,),
        in_specs=[pl.BlockSpec((S, D), lambda j: (0, 0)), pl.BlockSpec((D, tn), lambda j: (0, j)),
                  pl.BlockSpec((S, cfg.DH), lambda j: (0, 0)), pl.BlockSpec((S, cfg.DH), lambda j: (0, 0))],
        out_specs=pl.BlockSpec((None, S, tn), lambda j: (j // per, 0, j % per)),
        out_shape=jax.ShapeDtypeStruct((6, S, cfg.DSB), BF16),
        compiler_params=_cp(cfg, ("parallel",)),
    )(h1, w_in, cos2, sin2)


def _sb_tile(cfg, q, k, valid):
    z = _dot_nt(q, k) * (cfg.DH ** -0.5)
    lb = jnp.minimum(z, 0.0) - jnp.log1p(jnp.exp(-jnp.abs(z)))
    lk = lb - z
    return lb, (lk if valid is None else jnp.where(valid, lk, 0.0))


def _masked(valid, x):
    return x if valid is None else jnp.where(valid, x, 0.0)


def sb_fwd(cfg, qkv3):
    S, QB, DH, NH = cfg.S, cfg.SBT, cfg.DH, cfg.SBH

    def body(q_ref, k_ref, v_ref, o_ref, t_ref):
        row = lax.broadcasted_iota(jnp.int32, (QB, QB), 0)
        col = lax.broadcasted_iota(jnp.int32, (QB, QB), 1)
        u_after = (row > col).astype(BF16)
        causal = col < row
        heads = [slice(h * DH, (h + 1) * DH) for h in range(NH)]

        def q_loop(qb, _):
            rows = pl.ds(pl.multiple_of(qb * QB, QB), QB)
            qs = [q_ref[rows, hd] for hd in heads]

            def tile(kb, carry, valid):
                krows = pl.ds(pl.multiple_of(kb * QB, QB), QB)
                lbk = [_sb_tile(cfg, q, k_ref[krows, hd], valid) for q, hd in zip(qs, heads)]
                rems = [_dot_split(lk, u_after) for _, lk in lbk]
                aa = [_masked(valid, jnp.exp(lb + rem + c)).astype(BF16) for (lb, _), rem, (_, c) in zip(lbk, rems, carry)]
                return tuple((o_acc + _dot(a, v_ref[krows, hd]), c + jnp.sum(lk, axis=1, keepdims=True))
                             for a, hd, (_, lk), (o_acc, c) in zip(aa, heads, lbk, carry))

            carry = tile(qb, ((jnp.zeros((QB, DH), F32), jnp.zeros((QB, 1), F32)),) * NH, causal)
            carry = lax.fori_loop(0, qb, lambda i, cr: tile(qb - 1 - i, cr, None), carry)
            for hd, (o_acc, c) in zip(heads, carry):
                o_ref[rows, hd] = o_acc
                t_ref[rows, hd] = jnp.broadcast_to(c, (QB, DH))
            return 0

        lax.fori_loop(0, S // QB, q_loop, 0)

    def spec(i):
        return pl.BlockSpec((None, S, NH * DH), lambda h: (i, 0, h))

    return pl.pallas_call(
        body, name="sb_fwd", grid=(cfg.HSB // NH,),
        in_specs=[spec(0), spec(1), spec(2)],
        out_specs=[pl.BlockSpec((S, NH * DH), lambda h: (0, h))] * 2,
        out_shape=[jax.ShapeDtypeStruct((S, cfg.DSB), F32)] * 2,
        compiler_params=_cp(cfg, ("parallel",)),
    )(qkv3, qkv3, qkv3)


def sb_bwd(cfg, qkv3, do_sb, tsum):
    S, QB, DH, NH = cfg.S, cfg.SBT, cfg.DH, cfg.SBHB
    scale = DH ** -0.5

    def body(q_ref, k_ref, v_ref, do_ref, t_ref, d_ref, dk_acc, dv_acc):
        dk_acc[...] = jnp.zeros_like(dk_acc)
        dv_acc[...] = jnp.zeros_like(dv_acc)
        row = lax.broadcasted_iota(jnp.int32, (QB, QB), 0)
        col = lax.broadcasted_iota(jnp.int32, (QB, QB), 1)
        u_upto = (row <= col).astype(BF16)
        u_before = (row < col).astype(BF16)
        causal = col < row
        heads = [slice(h * DH, (h + 1) * DH) for h in range(NH)]

        def q_loop(qb, _):
            rows = pl.ds(pl.multiple_of(qb * QB, QB), QB)
            qs = [q_ref[rows, hd] for hd in heads]
            dos = [do_ref[rows, hd] for hd in heads]
            totals = [t_ref[rows, hd.start:hd.start + 1] for hd in heads]

            def tile(kb, carry, valid):
                krows = pl.ds(pl.multiple_of(kb * QB, QB), QB)
                ks = [k_ref[krows, hd] for hd in heads]
                lbk = [_sb_tile(cfg, q, k, valid) for q, k in zip(qs, ks)]
                das = [_dot_nt(do, v_ref[krows, hd]) for do, hd in zip(dos, heads)]
                pins = [_dot_split(lk, u_upto) for _, lk in lbk]
                aa = [_masked(valid, jnp.exp(lb + (tot - pc - pin)))
                      for (lb, _), tot, (_, pc, _), pin in zip(lbk, totals, carry, pins)]
                gs = [a * da for a, da in zip(aa, das)]
                for a, do, hd in zip(aa, dos, heads):
                    dv_acc[krows, hd] += _dot_tn(a.astype(BF16), do)
                cums = [gc + _dot(g.astype(BF16), u_before) for g, (_, _, gc) in zip(gs, carry)]
                dzs = [(_masked(valid, g - jnp.exp(lb) * (g + cum)) * scale).astype(BF16)
                       for g, (lb, _), cum in zip(gs, lbk, cums)]
                for dz, q, hd in zip(dzs, qs, heads):
                    dk_acc[krows, hd] += _dot_tn(dz, q)
                return tuple((dq + _dot(dz, k), pc + jnp.sum(lk, axis=1, keepdims=True), gc + jnp.sum(g, axis=1, keepdims=True))
                             for dz, k, (_, lk), g, (dq, pc, gc) in zip(dzs, ks, lbk, gs, carry))

            z1 = jnp.zeros((QB, 1), F32)
            carry = lax.fori_loop(0, qb, lambda kb, cr: tile(kb, cr, None), ((jnp.zeros((QB, DH), F32), z1, z1),) * NH)
            for hd, (dq_acc, _, _) in zip(heads, tile(qb, carry, causal)):
                d_ref[0, rows, hd] = dq_acc.astype(BF16)
            return 0

        lax.fori_loop(0, S // QB, q_loop, 0)
        d_ref[1, :, :] = dk_acc[...].astype(BF16)
        d_ref[2, :, :] = dv_acc[...].astype(BF16)

    def spec(i):
        return pl.BlockSpec((None, S, NH * DH), lambda h: (i, 0, h))

    hd_spec = pl.BlockSpec((S, NH * DH), lambda h: (0, h))
    return pl.pallas_call(
        body, name="sb_bwd", grid=(cfg.HSB // NH,),
        in_specs=[spec(0), spec(1), spec(2), hd_spec, hd_spec],
        out_specs=pl.BlockSpec((3, S, NH * DH), lambda h: (0, 0, h)),
        out_shape=jax.ShapeDtypeStruct((6, S, cfg.DSB), BF16),
        scratch_shapes=[pltpu.VMEM((S, NH * DH), F32), pltpu.VMEM((S, NH * DH), F32)],
        compiler_params=_cp(cfg, ("parallel",)),
    )(qkv3, qkv3, qkv3, do_sb, tsum)


def _band_mask(cfg, n, n_back):
    QB = cfg.QB
    qi = lax.broadcasted_iota(jnp.int32, (QB, 2 * QB), 0)
    kj = lax.broadcasted_iota(jnp.int32, (QB, 2 * QB), 1)
    dist = QB + qi - kj
    return (dist >= 0) & (dist <= n_back) & jnp.logical_or(n > 0, kj >= QB)


def _sub_rows(start, n, dil):
    if dil > 1:
        return pl.ds(start, n, stride=dil)
    return pl.ds(start if isinstance(start, int) else pl.multiple_of(start, 8), n)


def _stage_residues(cfg, dil, pairs):
    QB, L = cfg.QB, cfg.S // dil
    for src, dst in pairs:
        for r in range(dil):
            dst[pl.ds(r * (QB + L), QB), :] = jnp.zeros((QB, cfg.DH), BF16)
            dst[pl.ds(r * (QB + L) + QB, L), :] = src[_sub_rows(r, L, dil), :].astype(BF16)


def _staged_rows(cfg):
    return cfg.S + cfg.QB * max(d for _, d in cfg.branches)


def _lane_value(x):
    return jnp.max(x, axis=1, keepdims=True)


def dil_fwd(cfg, qkv3):
    S, QB, DH = cfg.S, cfg.QB, cfg.DH
    scale = DH ** -0.5
    nb = len(cfg.branches)
    mix_rows = min(256, S)

    def body(q_ref, k_ref, v_ref, o_ref, lt_ref, qf, kf, vf, kp, vp, *obl):
        obs, lbs = obl[:nb], obl[nb:]
        qf[...] = q_ref[...].astype(F32)
        kf[...] = k_ref[...].astype(F32)
        vf[...] = v_ref[...].astype(F32)
        for b, (window, dil) in enumerate(cfg.branches):
            L, n_back = S // dil, window // dil
            assert n_back <= QB and L % QB == 0
            _stage_residues(cfg, dil, [(kf, kp), (vf, vp)])
            for r in range(dil):
                for n in range(L // QB):
                    rows = _sub_rows(r + n * (QB * dil), QB, dil)
                    band = pl.ds(r * (QB + L) + n * QB, 2 * QB)
                    s = _dot_nt(qf[rows, :].astype(BF16), kp[band, :]) * scale
                    s = jnp.where(_band_mask(cfg, n, n_back), s, NEG_BIG)
                    m = jnp.max(s, axis=1, keepdims=True)
                    p = jnp.exp(s - m)
                    den = jnp.sum(p, axis=1, keepdims=True)
                    obs[b][rows, :] = _dot(p.astype(BF16), vp[band, :]) / den
                    lbs[b][rows, :] = jnp.broadcast_to(m + jnp.log(den), (QB, DH))

        def mix(i, _):
            rows = pl.ds(pl.multiple_of(i * mix_rows, mix_rows), mix_rows)
            ls = [r[rows, :] for r in lbs]
            m = functools.reduce(jnp.maximum, ls)
            es = [jnp.exp(l - m) for l in ls]
            tot = functools.reduce(jnp.add, es)
            o_ref[rows, :] = functools.reduce(jnp.add, [(e / tot) * r[rows, :] for e, r in zip(es, obs)])
            lt_ref[rows, :] = m + jnp.log(tot)
            return 0

        lax.fori_loop(0, S // mix_rows, mix, 0)

    def spec(i):
        return pl.BlockSpec((None, S, DH), lambda h: (i, 0, h))

    o_spec = pl.BlockSpec((S, DH), lambda h: (0, h))
    return pl.pallas_call(
        body, name="dil_fwd", grid=(cfg.HDL,),
        in_specs=[spec(3), spec(4), spec(5)], out_specs=[o_spec, o_spec],
        out_shape=[jax.ShapeDtypeStruct((S, cfg.DDL), F32)] * 2,
        scratch_shapes=[pltpu.VMEM((S, DH), F32)] * 3 + [pltpu.VMEM((_staged_rows(cfg), DH), BF16)] * 2
        + [pltpu.VMEM((S, DH), F32)] * (2 * nb),
        compiler_params=_cp(cfg, ("parallel",)),
    )(qkv3, qkv3, qkv3)


def dil_bwd(cfg, qkv3, do_dl, delta, lse_tot, cos2, sin2, d_sb3):
    S, QB, DH = cfg.S, cfg.QB, cfg.DH
    scale = DH ** -0.5
    out_rows = min(256, S)
    GROUP = 4

    def body(q_ref, k_ref, v_ref, do_ref, dl_ref, lt_ref, c_ref, s_ref, base_ref, d_ref,
             qf, kf, vf, dof, kp, vp, dkp, dvp, dqn, dkn, dvn):
        qf[...] = q_ref[...].astype(F32)
        kf[...] = k_ref[...].astype(F32)
        vf[...] = v_ref[...].astype(F32)
        dof[...] = do_ref[...].astype(F32)
        for acc in (dqn, dkn, dvn):
            acc[...] = jnp.zeros_like(acc)
        for window, dil in cfg.branches:
            L, n_back = S // dil, window // dil
            reg = QB + L
            _stage_residues(cfg, dil, [(kf, kp), (vf, vp)])
            dkp[pl.ds(0, dil * reg), :] = jnp.zeros((dil * reg, DH), F32)
            dvp[pl.ds(0, dil * reg), :] = jnp.zeros((dil * reg, DH), F32)
            blocks = [(r, n) for n in range(L // QB) for r in range(dil)]
            for g0 in range(0, len(blocks), GROUP):
                grp = blocks[g0:g0 + GROUP]
                rows = [_sub_rows(r + n * (QB * dil), QB, dil) for r, n in grp]
                bands = [pl.ds(r * reg + n * QB, 2 * QB) for r, n in grp]
                qs = [qf[rw, :].astype(BF16) for rw in rows]
                dos = [dof[rw, :].astype(BF16) for rw in rows]
                kbs = [kp[bd, :] for bd in bands]
                ss = [_dot_nt(q, kb) * scale for q, kb in zip(qs, kbs)]
                dps = [_dot_nt(do, vp[bd, :]) for do, bd in zip(dos, bands)]
                ps = [jnp.exp(jnp.where(_band_mask(cfg, n, n_back), s, NEG_BIG) - _lane_value(lt_ref[rw, :]))
                      for s, rw, (_, n) in zip(ss, rows, grp)]
                dss = [(p * (dp - _lane_value(dl_ref[rw, :])) * scale).astype(BF16) for p, dp, rw in zip(ps, dps, rows)]
                for rw, bd, ds, p, q, do, kb in zip(rows, bands, dss, ps, qs, dos, kbs):
                    dqn[rw, :] += _dot(ds, kb)
                    dkp[bd, :] += _dot_tn(ds, q)
                    dvp[bd, :] += _dot_tn(p.astype(BF16), do)
            for r in range(dil):
                sub = _sub_rows(r, L, dil)
                dkn[sub, :] += dkp[pl.ds(r * reg + QB, L), :]
                dvn[sub, :] += dvp[pl.ds(r * reg + QB, L), :]

        def finish(i, _):
            rows = pl.ds(pl.multiple_of(i * out_rows, out_rows), out_rows)
            c, sn = c_ref[rows, :], s_ref[rows, :]
            for j, acc in enumerate((dqn, dkn)):
                d = acc[rows, :]
                d_ref[j, rows, :] = (d * c + pltpu.roll(d * sn, DH // 2, 1)).astype(BF16)
            d_ref[2, rows, :] = dvn[rows, :].astype(BF16)
            return 0

        lax.fori_loop(0, S // out_rows, finish, 0)

    def spec(i):
        return pl.BlockSpec((None, S, DH), lambda h: (i, 0, h))

    hd = pl.BlockSpec((S, DH), lambda h: (0, h))
    tab = pl.BlockSpec((S, DH), lambda h: (0, 0))
    ns = _staged_rows(cfg)
    return pl.pallas_call(
        body, name="dil_bwd", grid=(cfg.HDL,),
        in_specs=[spec(3), spec(4), spec(5), hd, hd, hd, tab, tab, ANY],
        out_specs=pl.BlockSpec((3, S, DH), lambda h: (1, 0, h)),
        out_shape=jax.ShapeDtypeStruct((6, S, cfg.DDL), BF16),
        input_output_aliases={8: 0},
        scratch_shapes=[pltpu.VMEM((S, DH), F32)] * 4 + [pltpu.VMEM((ns, DH), BF16)] * 2
        + [pltpu.VMEM((ns, DH), F32)] * 2 + [pltpu.VMEM((S, DH), F32)] * 3,
        compiler_params=_cp(cfg, ("parallel",)),
    )(qkv3, qkv3, qkv3, do_dl, delta, lse_tot, cos2, sin2, d_sb3)


def combine_fwd(cfg, o_sb, o_dl, g_sb, g_dl):
    S, DH = cfg.S, cfg.DH

    def head_norm(o, g):
        return o * lax.rsqrt(jnp.mean(o * o, axis=-1, keepdims=True) + RMS_EPS) * g

    def body(osb_ref, odl_ref, gsb_ref, gdl_ref, mix_ref):
        for h in range(cfg.HSB):
            c = slice(h * DH, (h + 1) * DH)
            mix_ref[:, c] = head_norm(osb_ref[:, c], gsb_ref[:, c]).astype(BF16)
        for h in range(cfg.HDL):
            c = slice(h * DH, (h + 1) * DH)
            mix_ref[:, cfg.DSB + h * DH:cfg.DSB + (h + 1) * DH] = head_norm(odl_ref[:, c], gdl_ref[:, c]).astype(BF16)

    return pl.pallas_call(
        body, name="combine_fwd", grid=(S // cfg.TM,),
        in_specs=[_row(cfg, cfg.DSB), _row(cfg, cfg.DDL), _vec(cfg.DSB), _vec(cfg.DDL)],
        out_specs=_row(cfg, cfg.DMIX), out_shape=jax.ShapeDtypeStruct((S, cfg.DMIX), BF16),
        compiler_params=_cp(cfg, ("parallel",)),
    )(o_sb, o_dl, g_sb, g_dl)


def combine_bwd(cfg, dmix, o_sb, o_dl, g_sb, g_dl):
    S, DH = cfg.S, cfg.DH

    def body(dm_ref, osb_ref, odl_ref, gsb_ref, gdl_ref, dsb_ref, ddl_ref, dl_ref, dgsb_ref, dgdl_ref):
        @pl.when(pl.program_id(0) == 0)
        def _():
            dgsb_ref[...] = jnp.zeros_like(dgsb_ref)
            dgdl_ref[...] = jnp.zeros_like(dgdl_ref)

        for h in range(cfg.HSB):
            c = slice(h * DH, (h + 1) * DH)
            dx, dgx = _rms_bwd(dm_ref[:, c], osb_ref[:, c], gsb_ref[:, c])
            dsb_ref[:, c] = dx.astype(BF16)
            dgsb_ref[:, c] += jnp.sum(dgx, axis=0, keepdims=True)
        for h in range(cfg.HDL):
            c = slice(h * DH, (h + 1) * DH)
            o = odl_ref[:, c]
            dx, dgx = _rms_bwd(dm_ref[:, cfg.DSB + h * DH:cfg.DSB + (h + 1) * DH], o, gdl_ref[:, c])
            ddl_ref[:, c] = dx.astype(BF16)
            dl_ref[:, c] = jnp.broadcast_to(jnp.sum(dx * o, axis=-1, keepdims=True), dx.shape)
            dgdl_ref[:, c] += jnp.sum(dgx, axis=0, keepdims=True)

    return pl.pallas_call(
        body, name="combine_bwd", grid=(S // cfg.TM,),
        in_specs=[_row(cfg, cfg.DMIX), _row(cfg, cfg.DSB), _row(cfg, cfg.DDL), _vec(cfg.DSB), _vec(cfg.DDL)],
        out_specs=[_row(cfg, cfg.DSB), _row(cfg, cfg.DDL), _row(cfg, cfg.DDL), _vec(cfg.DSB), _vec(cfg.DDL)],
        out_shape=[jax.ShapeDtypeStruct((S, cfg.DSB), BF16), jax.ShapeDtypeStruct((S, cfg.DDL), BF16),
                   jax.ShapeDtypeStruct((S, cfg.DDL), F32), jax.ShapeDtypeStruct((1, cfg.DSB), F32),
                   jax.ShapeDtypeStruct((1, cfg.DDL), F32)],
        compiler_params=_cp(cfg, ("arbitrary",)),
    )(dmix, o_sb, o_dl, g_sb, g_dl)


SUB = 8


def _shift_down(u, prev, j):
    rolled = pltpu.roll(u, j, 0)
    row = lax.broadcasted_iota(jnp.int32, (SUB, u.shape[1]), 0)
    head = jnp.where(row >= j, rolled[:SUB], pltpu.roll(prev, j, 0))
    return jnp.concatenate([head, rolled[SUB:]], axis=0)


def _shift_up(u, nxt, j):
    n = u.shape[0]
    rolled = pltpu.roll(u, n - j, 0)
    row = lax.broadcasted_iota(jnp.int32, (SUB, u.shape[1]), 0)
    tail = jnp.where(row < SUB - j, rolled[n - SUB:], pltpu.roll(nxt, SUB - j, 0))
    return jnp.concatenate([rolled[:n - SUB], tail], axis=0)


def _conv(u, s1, s2, cw, cb):
    return u * cw[2:3, :] + s1 * cw[1:2, :] + s2 * cw[0:1, :] + cb


def _chunk_rows(cfg):
    ch = min(cfg.FCH, cfg.S)
    return ch, cfg.S // ch


def ffn_fwd(cfg, h2, w_up, conv_w, conv_b):
    S, D = h2.shape
    tn, nt = cfg.TNF, cfg.FFP // cfg.TNF
    ch, nch = _chunk_rows(cfg)

    def body(h_ref, wg_ref, wv_ref, cwg_ref, cwv_ref, cbg_ref, cbv_ref, u_ref, y_ref):
        prev = [jnp.zeros((SUB, tn), F32)] * 2
        pending = None
        for ci in range(nch + 1):
            if ci < nch:
                h = h_ref[pl.ds(ci * ch, ch), :]
                us_next = [_dot_nt(h, wg_ref[...]), _dot_nt(h, wv_ref[...])]
            if pending is not None:
                rows, us = pending
                cs = []
                for i, (cw_ref, cb_ref) in enumerate(((cwg_ref, cbg_ref), (cwv_ref, cbv_ref))):
                    u_ref[i, rows, :] = us[i]
                    cs.append(_conv(us[i], _shift_down(us[i], prev[i], 1), _shift_down(us[i], prev[i], 2),
                                    cw_ref[...], cb_ref[...]))
                y_ref[rows, :] = (_gelu(cs[0])[0] * cs[1]).astype(BF16)
                prev = [u[ch - SUB:] for u in us]
            pending = (pl.ds(ci * ch, ch), us_next) if ci < nch else None

    return pl.pallas_call(
        body, name="ffn_fwd", grid=(nt,),
        in_specs=[pl.BlockSpec((S, D), lambda n: (0, 0)),
                  pl.BlockSpec((tn, D), lambda n: (n, 0)), pl.BlockSpec((tn, D), lambda n: (n + nt, 0)),
                  pl.BlockSpec((3, tn), lambda n: (0, n)), pl.BlockSpec((3, tn), lambda n: (0, n + nt)),
                  pl.BlockSpec((1, tn), lambda n: (0, n)), pl.BlockSpec((1, tn), lambda n: (0, n + nt))],
        out_specs=[pl.BlockSpec((2, S, tn), lambda n: (0, 0, n)), pl.BlockSpec((S, tn), lambda n: (0, n))],
        out_shape=[jax.ShapeDtypeStruct((2, S, cfg.FFP), F32), jax.ShapeDtypeStruct((S, cfg.FFP), BF16)],
        compiler_params=_cp(cfg, ("parallel",)),
    )(h2, w_up, w_up, conv_w, conv_w, conv_b, conv_b)


def ffn_bwd(cfg, df, h2, w_down, u, conv_w, conv_b):
    S, D = df.shape
    tn, nt = cfg.TNF, cfg.FFP // cfg.TNF

    ch, nch = _chunk_rows(cfg)

    def body(df_ref, h_ref, wd_ref, u_ref, cwg_ref, cwv_ref, cbg_ref, cbv_ref,
             du_ref, dwd_ref, dwu_ref, dcw_ref, dcb_ref):
        cws = (cwg_ref[...], cwv_ref[...])
        cbs = (cbg_ref[...], cbv_ref[...])
        zero = jnp.zeros((SUB, tn), F32)
        nxt = [zero, zero]
        dws = [[jnp.zeros((1, tn), F32)] * 4 for _ in range(2)]
        dwd = jnp.zeros((tn, D), F32)
        dwu = [jnp.zeros((tn, D), F32)] * 2
        order = list(reversed(range(nch)))
        dys, done = {}, {}
        for step in range(nch + 2):
            if step < nch:
                ci = order[step]
                dys[ci] = _dot_nt(df_ref[pl.ds(ci * ch, ch), :], wd_ref[...])
            if 1 <= step <= nch:
                ci = order[step - 1]
                rows = pl.ds(ci * ch, ch)
                dy = dys.pop(ci)
                us, s1, s2, cs = [], [], [], []
                for i in range(2):
                    u = u_ref[i, rows, :]
                    prev = u_ref[i, pl.ds(ci * ch - SUB, SUB), :] if ci else zero
                    us.append(u)
                    s1.append(_shift_down(u, prev, 1))
                    s2.append(_shift_down(u, prev, 2))
                    cs.append(_conv(u, s1[i], s2[i], cws[i], cbs[i]))
                gl, t = _gelu(cs[0])
                dcs = (dy * cs[1] * _gelu_grad(cs[0], t), dy * gl)
                dus = []
                for i, dc in enumerate(dcs):
                    du = dc * cws[i][2:3, :] + _shift_up(dc, nxt[i], 1) * cws[i][1:2, :] + _shift_up(dc, nxt[i], 2) * cws[i][0:1, :]
                    dus.append(du.astype(BF16))
                    du_ref[i, rows, :] = dus[i]
                    for j, tap in enumerate((s2[i], s1[i], us[i])):
                        dws[i][j] = dws[i][j] + jnp.sum(dc * tap, axis=0, keepdims=True)
                    dws[i][3] = dws[i][3] + jnp.sum(dc, axis=0, keepdims=True)
                nxt = [dc[:SUB] for dc in dcs]
                done[ci] = ((gl * cs[1]).astype(BF16), dus)
            if step >= 2:
                ci = order[step - 2]
                rows = pl.ds(ci * ch, ch)
                yv, dus = done.pop(ci)
                dwd = dwd + _dot_tn(yv, df_ref[rows, :])
                hv = h_ref[rows, :]
                dwu = [acc + _dot_tn(du, hv) for acc, du in zip(dwu, dus)]
        dwd_ref[...] = dwd.astype(BF16)
        for i in range(2):
            dwu_ref[i] = dwu[i].astype(BF16)
            for j in range(3):
                dcw_ref[i, j:j + 1, :] = dws[i][j]
            dcb_ref[i] = dws[i][3]

    whole = pl.BlockSpec((S, D), lambda n: (0, 0), pipeline_mode=pl.Buffered(1))
    du, dwd, dwu, dcw, dcb = pl.pallas_call(
        body, name="ffn_bwd", grid=(nt,),
        in_specs=[whole, whole, pl.BlockSpec((tn, D), lambda n: (n, 0)),
                  pl.BlockSpec((2, S, tn), lambda n: (0, 0, n)),
                  pl.BlockSpec((3, tn), lambda n: (0, n)), pl.BlockSpec((3, tn), lambda n: (0, n + nt)),
                  pl.BlockSpec((1, tn), lambda n: (0, n)), pl.BlockSpec((1, tn), lambda n: (0, n + nt))],
        out_specs=[pl.BlockSpec((2, S, tn), lambda n: (0, 0, n)), pl.BlockSpec((tn, D), lambda n: (n, 0)),
                   pl.BlockSpec((2, tn, D), lambda n: (0, n, 0)),
                   pl.BlockSpec((2, 3, tn), lambda n: (0, 0, n)), pl.BlockSpec((2, 1, tn), lambda n: (0, 0, n))],
        out_shape=[jax.ShapeDtypeStruct((2, S, cfg.FFP), BF16), jax.ShapeDtypeStruct((cfg.FFP, D), BF16),
                   jax.ShapeDtypeStruct((2, cfg.FFP, D), BF16),
                   jax.ShapeDtypeStruct((2, 3, cfg.FFP), F32), jax.ShapeDtypeStruct((2, 1, cfg.FFP), F32)],
        compiler_params=_cp(cfg, ("parallel",)),
    )(df, h2, w_down, u, conv_w, conv_w, conv_b, conv_b)
    return du, dwd, dwu.reshape(cfg.FF2P, D), dcw, dcb


def rope_tables(cfg):
    inv_freq = ROPE_THETA ** (-jnp.arange(0, cfg.DH, 2, dtype=F32) / cfg.DH)
    ang = jnp.arange(cfg.S, dtype=F32)[:, None] * inv_freq[None, :]
    cos, sin = jnp.cos(ang), jnp.sin(ang)
    return jnp.concatenate([cos, cos], axis=1), jnp.concatenate([-sin, sin], axis=1)


class LocalWeights:
    def __init__(self, w_in, w_out, w_up, conv_w, w_down):
        self.w = (w_in, w_out, w_up, conv_w, w_down)
        self.grads = {}

    def first_start(self):
        return None

    def weights_first(self, after):
        return self.w[0], self.w[3]

    def start_rest(self, after):
        return None

    def weights_rest(self, group, after):
        return ((self.w[1],), None) if group == 0 else ((self.w[4],), None)

    def forwarded(self, group, after):
        return (self.w[2],)

    def pair_start(self, grads):
        self.grads.update(grads)
        return None

    def reduce_start(self, grads, after=None):
        self.grads.update(grads)
        return None

    def reduce_wait(self, names, after, finish_later=False):
        return None

    def reduce_finish(self, after):
        pass


def _after(a, token):
    return a if token is None else a + token[0, 0].astype(a.dtype)


def local_step(cfg, comm, x, target, g1, g2, g3, g4, g_sb, g_dl, conv_b):
    S, D = cfg.S, cfg.D
    cos2, sin2 = rope_tables(cfg)
    full = lambda r, c: pl.BlockSpec((r, c), lambda j, k: (0, 0))

    h1 = rms_fwd(cfg, x, _after(g1, comm.first_start()))
    w_in, conv_w = comm.weights_first(after=h1)
    qkv3 = qkv_proj(cfg, h1, w_in, _after(cos2, comm.start_rest(after=w_in)), sin2)
    o_sb, tsum = sb_fwd(cfg, qkv3)
    o_dl, lse_tot = dil_fwd(cfg, qkv3)
    mixed = combine_fwd(cfg, o_sb, o_dl, g_sb, g_dl)
    (w_out,), token = comm.weights_rest(0, after=mixed)
    tn = cfg.TN
    mo = _mm(cfg, "mix_out", mixed, w_out, nt=False, grid=(D // tn, 1),
             a_spec=full(S, cfg.DMIX), b_spec=pl.BlockSpec((cfg.DMIX, tn), lambda j, k: (0, j)),
             o_spec=pl.BlockSpec((S, tn), lambda j, k: (0, j)),
             out_shape=jax.ShapeDtypeStruct((S, D), F32), acc_shape=(8, LANE), dep=token)
    x1, h2 = mid_fwd(cfg, x, mo, g2, g3)
    w_up, = comm.forwarded(0, after=h2)
    u, y = ffn_fwd(cfg, h2, w_up, conv_w, conv_b)
    (w_down,), _ = comm.weights_rest(1, after=y)
    tk = cfg.FFP // 2
    f = _mm(cfg, "ffn_down", y, w_down, nt=False, grid=(D // tn, cfg.FFP // tk),
            a_spec=pl.BlockSpec((S, tk), lambda j, k: (0, k)), b_spec=pl.BlockSpec((tk, tn), lambda j, k: (k, j)),
            o_spec=pl.BlockSpec((S, tn), lambda j, k: (0, j)),
            out_shape=jax.ShapeDtypeStruct((S, D), F32), acc_shape=(S, tn))
    dout, df, dg4, loss = final_fwd_bwd(cfg, x1, f, g4, target)

    du, dw_down, dw_up, dconv_w, dconv_b = ffn_bwd(cfg, df, h2, w_down, u, conv_w, conv_b)
    kt = cfg.FFP // tk
    dh2 = _mm(cfg, "d_h2", du, w_up, nt=False, grid=(D // tn, 2 * kt),
              a_spec=pl.BlockSpec((None, S, tk), lambda j, k: (k // kt, 0, k % kt)),
              b_spec=pl.BlockSpec((tk, tn), lambda j, k: (k, j)),
              o_spec=pl.BlockSpec((S, tn), lambda j, k: (0, j)),
              out_shape=jax.ShapeDtypeStruct((S, D), F32), acc_shape=(S, tn),
              dep=comm.pair_start(dict(w_down=dw_down, w_up=dw_up)))
    token = comm.reduce_start({}, after=dh2)
    dx1, dmo, dg3, dg2 = mid_bwd(cfg, dh2, x1, _after(g3, token), dout, mo, g2)

    dmix = _mm(cfg, "d_mixed", dmo, w_out, nt=True, grid=(cfg.DMIX // tn, 1),
               a_spec=full(S, D), b_spec=pl.BlockSpec((tn, D), lambda j, k: (j, 0)),
               o_spec=pl.BlockSpec((S, tn), lambda j, k: (0, j)),
               out_shape=jax.ShapeDtypeStruct((S, cfg.DMIX), F32), acc_shape=(8, LANE))
    dw_out = _mm_tn(cfg, "d_w_out", mixed, dmo, grid=(D // tn,),
                    a_spec=pl.BlockSpec((S, cfg.DMIX), lambda j: (0, 0)),
                    b_spec=pl.BlockSpec((S, tn), lambda j: (0, j)),
                    o_spec=pl.BlockSpec((cfg.DMIX, tn), lambda j: (0, j)),
                    out_shape=jax.ShapeDtypeStruct((cfg.DMIX, D), BF16))
    token = comm.reduce_start(dict(w_out=dw_out))
    do_sb, do_dl, delta, dg_sb, dg_dl = combine_bwd(cfg, dmix, o_sb, o_dl, _after(g_sb, token), g_dl)
    d_sb3 = sb_bwd(cfg, qkv3, do_sb, tsum)
    dqkv3 = dil_bwd(cfg, qkv3, do_dl, delta, lse_tot, cos2, sin2, d_sb3)
    token = comm.reduce_wait(("w_up", "w_down"), after=dqkv3, finish_later=True)
    tkq = min(tn, cfg.DSB)
    kq = cfg.DSB // tkq
    dw_in = _mm_tn(cfg, "d_w_in", h1, dqkv3, grid=(6 * kq,),
                   a_spec=pl.BlockSpec((S, D), lambda j: (0, 0)),
                   b_spec=pl.BlockSpec((None, S, tkq), lambda j: (j // kq, 0, j % kq)),
                   o_spec=pl.BlockSpec((D, tkq), lambda j: (0, j)),
                   out_shape=jax.ShapeDtypeStruct((D, 6 * cfg.DSB), BF16), dep=token)
    comm.reduce_finish(after=dw_in)
    comm.reduce_wait(("w_out",), after=dw_in)
    token = comm.late_token = comm.reduce_start(dict(w_in=dw_in))
    dh1 = _mm(cfg, "d_h1", dqkv3, w_in, nt=True, grid=(D // tn, 6),
              a_spec=pl.BlockSpec((None, S, cfg.DSB), lambda j, k: (k, 0, 0)),
              b_spec=pl.BlockSpec((tn, cfg.DSB), lambda j, k: (j, k)),
              o_spec=pl.BlockSpec((S, tn), lambda j, k: (0, j)),
              out_shape=jax.ShapeDtypeStruct((S, D), F32), acc_shape=(S, tn), dep=token)
    grad_x, dg1 = first_bwd(cfg, dh1, x, g1, dx1)
    small = dict(loss=loss, g1=dg1, g2=dg2, g3=dg3, g4=dg4, g_sb=dg_sb, g_dl=dg_dl,
                 conv_b=dconv_b.reshape(1, cfg.FF2P), conv_w=dconv_w.transpose(1, 0, 2).reshape(3, cfg.FF2P))
    return grad_x, small


ANY = pl.BlockSpec(memory_space=pl.ANY)


def _me():
    return lax.axis_index("x"), lax.axis_index("y"), lax.axis_index("c")


def _other_chips(x, y):
    return [(1 - x, y), (x, 1 - y), (1 - x, 1 - y)]


def pad_conv_w(cfg, conv_w, pos):
    r, c = conv_w.shape

    def body(pos_ref, w_ref, full_ref, scr, sem):
        scr[:, :c] = w_ref[...]
        scr[:, c:] = jnp.zeros((r, cfg.FSHP - c), F32)
        cols = pl.ds(pl.multiple_of(pos_ref[0] * cfg.FSHP, LANE), cfg.FSHP)
        cp = pltpu.make_async_copy(scr, full_ref.at[:, cols], sem)
        cp.start()
        cp.wait()

    return pl.pallas_call(
        body, name="pad_conv_w",
        grid_spec=pltpu.PrefetchScalarGridSpec(
            num_scalar_prefetch=1, grid=(1,), in_specs=[pl.BlockSpec((r, c), lambda i, p: (0, 0))], out_specs=ANY,
            scratch_shapes=[pltpu.VMEM((r, cfg.FSHP), F32), pltpu.SemaphoreType.DMA]),
        out_shape=jax.ShapeDtypeStruct(_full_shape(cfg, "conv_w"), F32),
    )(pos, conv_w)


def _tile2(r, c):
    return (256, c) if r % 256 == 0 else (r, 512 if c % 512 == 0 else c)


def cast_into(cfg, name, w, pos, dep=None):
    r, c = w.shape
    _, nr, _, nc = _slab(cfg, name, 0)
    tm, tc = _tile2(r, c)
    wr = nr if tm == r else tm
    assert nc == c and (nr == r or tm == r)
    gap = cfg.FSHP - cfg.FSH if name == "w_down" else 0
    deps = [] if dep is None else [dep]

    def body(pos_ref, w_ref, *rest):
        full_ref, token, scr, sem = rest[len(deps):]
        token[...] = jnp.zeros_like(token)
        tile = w_ref[...]
        if deps:
            tile = tile + rest[0][0:1, 0:1]
        scr[pl.ds(0, tm), :] = tile.astype(BF16)
        if wr > tm:
            scr[pl.ds(tm, wr - tm), :] = jnp.zeros((wr - tm, tc), BF16)
        r0, _, c0, _ = _slab(cfg, name, pos_ref[0])
        rows = pl.ds(pl.multiple_of(r0 + pl.program_id(0) * tm, 16), wr)
        cols = pl.ds(pl.multiple_of(c0 + pl.program_id(1) * tc, LANE), tc)
        cps = [pltpu.make_async_copy(scr.at[pl.ds(0, wr), :], full_ref.at[rows, cols], sem.at[0])]
        if gap:
            scr[pl.ds(wr, gap), :] = jnp.zeros((gap, tc), BF16)
            for h in range(2):
                pad_rows = pl.ds(h * cfg.FSHP + cfg.FSH, gap)
                cps.append(pltpu.make_async_copy(scr.at[pl.ds(wr, gap), :], full_ref.at[pad_rows, cols], sem.at[1 + h]))
        for cp in cps:
            cp.start()
        for cp in cps:
            cp.wait()

    return pl.pallas_call(
        body, name=f"cast_{name}",
        grid_spec=pltpu.PrefetchScalarGridSpec(
            num_scalar_prefetch=1, grid=(r // tm, c // tc),
            in_specs=[pl.BlockSpec((tm, tc), lambda i, j, p: (i, j))]
            + [pl.BlockSpec((8, LANE), lambda i, j, p: (0, 0))] * len(deps),
            out_specs=[ANY, pl.BlockSpec((8, LANE), lambda i, j, p: (0, 0))],
            scratch_shapes=[pltpu.VMEM((wr + gap, tc), BF16), pltpu.SemaphoreType.DMA((3,))]),
        out_shape=[jax.ShapeDtypeStruct(_full_shape(cfg, name), BF16), jax.ShapeDtypeStruct((8, LANE), F32)],
        compiler_params=_cp(cfg, ("arbitrary", "arbitrary")),
    )(pos, w, *deps)


HBM = pl.BlockSpec(memory_space=pltpu.HBM)
SEM = pl.BlockSpec(memory_space=pltpu.SEMAPHORE)
TOKEN = pl.BlockSpec(memory_space=pltpu.VMEM)
EFFECT = pltpu.SideEffectType.DATAFLOW_SIDE_EFFECTING


def _slab(cfg, name, k):
    D = cfg.D
    if name == "w_in":
        cin = 6 * cfg.DSB // N_CHIPS
        return 0, D, k * cin, cin
    if name == "w_out":
        rout = cfg.DMIX // N_CHIPS
        return k * rout, rout, 0, D
    if name == "w_up":
        return k * cfg.FSHP, cfg.FSHP, 0, D
    if name == "conv_w":
        return 0, 3, k * cfg.FSHP, cfg.FSHP
    rdn = cfg.FSH // 2
    return (k // 2) * cfg.FSHP + (k % 2) * rdn, rdn, 0, D


def _full_shape(cfg, name):
    return dict(w_in=(cfg.D, 6 * cfg.DSB), w_out=(cfg.DMIX, cfg.D), w_up=(cfg.FF2P, cfg.D), w_down=(cfg.FFP, cfg.D),
                conv_w=(3, cfg.FF2P))[name]


def _half(cfg, name, ref, k, h):
    r0, nr, c0, nc = _slab(cfg, name, k)
    if name == "conv_w":
        return ref.at[:, pl.ds(c0, nc)]
    return ref.at[pl.ds(r0 + h * (nr // 2), nr // 2), pl.ds(c0, nc)]


def _rows_half(ref, h):
    nr = ref.shape[0] // 2
    return ref.at[pl.ds(h * nr, nr), :]


def _remote(src, dst, send_sem, recv_sem, dev):
    return pltpu.make_async_remote_copy(src_ref=src, dst_ref=dst, send_sem=send_sem, recv_sem=recv_sem,
                                        device_id=dev, device_id_type=MESH)


REST = ("w_out", "w_up", "w_down")
FIRST = (("w_in", "conv_w"),)
GROUPS = (("w_out", "w_up"), ("w_down",))


def _hbm(a):
    return pltpu.with_memory_space_constraint(a, pltpu.HBM)


def gather_start(cfg, tag, groups, fulls, after):
    order = [k for names in groups for k in names]
    n, ng = len(order), len(groups)

    def body(*refs):
        lands = dict(zip(order, refs[:n]))
        sems = refs[n + 1:n + 1 + 2 * ng]
        token = refs[-1]
        x, y, c = _me()
        me = 2 * x + y
        for g, names in enumerate(groups):
            for i, name in enumerate(names):
                mine = _half(cfg, name, lands[name], me, c)
                for j, (px, py) in enumerate(_other_chips(x, y)):
                    _remote(mine, mine, sems[2 * g].at[3 * i + j], sems[2 * g + 1].at[3 * i + j], (px, py, c)).start()
        token[...] = jnp.zeros_like(token)

    ops = [_hbm(fulls[k]) for k in order]
    sem_shapes = [pltpu.SemaphoreType.DMA((3 * len(names),)) for names in groups for _ in range(2)]
    outs = pl.pallas_call(
        body, name=f"gather_start_{tag}",
        in_specs=[HBM] * n + [ANY],
        out_specs=[SEM] * (2 * ng) + [HBM] * n + [TOKEN],
        out_shape=sem_shapes + [pltpu.HBM(a.shape, a.dtype) for a in ops] + [jax.ShapeDtypeStruct((8, LANE), F32)],
        input_output_aliases={i: 2 * ng + i for i in range(n)},
        compiler_params=pltpu.CompilerParams(has_side_effects=EFFECT),
    )(*ops, after)
    thru = dict(zip(order, outs[2 * ng:2 * ng + n]))
    return [(outs[2 * g], outs[2 * g + 1], [thru[k] for k in names]) for g, names in enumerate(groups)], outs[-1]


def gather_wait(cfg, names, ssem, rsem, lands, after):
    n = len(names)

    def body(*refs):
        lands_ = refs[:n]
        ssem_, rsem_ = refs[n], refs[n + 1]
        x, y, c = _me()
        me = 2 * x + y
        for i, name in enumerate(names):
            for j, (px, py) in enumerate(_other_chips(x, y)):
                cp = _remote(_half(cfg, name, lands_[i], me, c), _half(cfg, name, lands_[i], 2 * px + py, c),
                             ssem_.at[3 * i + j], rsem_.at[3 * i + j], (px, py, c))
                cp.wait_send()
                cp.wait_recv()

    return pl.pallas_call(
        body, name="gather_wait_" + "_".join(names),
        in_specs=[HBM] * n + [SEM, SEM, ANY], out_specs=[HBM] * n,
        out_shape=[pltpu.HBM(a.shape, a.dtype) for a in lands],
        input_output_aliases={i: i for i in range(n)},
        compiler_params=pltpu.CompilerParams(has_side_effects=EFFECT),
    )(*lands, ssem, rsem, after)


def gather_finish(cfg, names, lands):
    n = len(names)

    def body(*refs):
        outs = refs[n:2 * n]
        ssem, rsem = refs[2 * n:]
        x, y, c = _me()
        sib = (x, y, 1 - c)
        fwds = []
        for i, name in enumerate(names):
            for j, (px, py) in enumerate(_other_chips(x, y)):
                landed = _half(cfg, name, outs[i], 2 * px + py, c)
                fwds.append(_remote(landed, landed, ssem.at[3 * i + j], rsem.at[3 * i + j], sib))
        for cp in fwds:
            cp.start()
        for i, name in enumerate(names):
            for j, (px, py) in enumerate(_other_chips(x, y)):
                passed = _half(cfg, name, outs[i], 2 * px + py, 1 - c)
                _remote(passed, passed, ssem.at[3 * i + j], rsem.at[3 * i + j], sib).wait_recv()
        for cp in fwds:
            cp.wait_send()

    return pl.pallas_call(
        body, name="gather_finish_" + "_".join(names), in_specs=[ANY] * n, out_specs=[ANY] * n,
        out_shape=[jax.ShapeDtypeStruct(a.shape, a.dtype) for a in lands],
        input_output_aliases={i: i for i in range(n)},
        scratch_shapes=[pltpu.SemaphoreType.DMA((3 * n,)), pltpu.SemaphoreType.DMA((3 * n,))],
    )(*lands)


def forward_start(cfg, names, lands, after):
    n = len(names)

    def body(*refs):
        outs = refs[:n]
        ssem, rsem = refs[n + 1], refs[n + 2]
        token = refs[-1]
        x, y, c = _me()
        for i, name in enumerate(names):
            for j, (px, py) in enumerate(_other_chips(x, y)):
                landed = _half(cfg, name, outs[i], 2 * px + py, c)
                _remote(landed, landed, ssem.at[3 * i + j], rsem.at[3 * i + j], (x, y, 1 - c)).start()
        token[...] = jnp.zeros_like(token)

    ops = [_hbm(a) for a in lands]
    outs = pl.pallas_call(
        body, name="forward_start_" + "_".join(names),
        in_specs=[HBM] * n + [ANY], out_specs=[SEM, SEM] + [HBM] * n + [TOKEN],
        out_shape=[pltpu.SemaphoreType.DMA((3 * n,)), pltpu.SemaphoreType.DMA((3 * n,))]
        + [pltpu.HBM(a.shape, a.dtype) for a in ops] + [jax.ShapeDtypeStruct((8, LANE), F32)],
        input_output_aliases={i: 2 + i for i in range(n)},
        compiler_params=pltpu.CompilerParams(has_side_effects=EFFECT),
    )(*ops, after)
    return outs[0], outs[1], outs[2:2 + n], outs[-1]


def forward_wait(cfg, names, ssem, rsem, lands, after):
    n = len(names)

    def body(*refs):
        outs = refs[:n]
        ssem_, rsem_ = refs[n], refs[n + 1]
        x, y, c = _me()
        for i, name in enumerate(names):
            for j, (px, py) in enumerate(_other_chips(x, y)):
                cp = _remote(_half(cfg, name, outs[i], 2 * px + py, c), _half(cfg, name, outs[i], 2 * px + py, 1 - c),
                             ssem_.at[3 * i + j], rsem_.at[3 * i + j], (x, y, 1 - c))
                cp.wait_send()
                cp.wait_recv()

    return pl.pallas_call(
        body, name="forward_wait_" + "_".join(names),
        in_specs=[HBM] * n + [SEM, SEM, ANY], out_specs=[HBM] * n,
        out_shape=[pltpu.HBM(a.shape, a.dtype) for a in lands],
        input_output_aliases={i: i for i in range(n)},
        compiler_params=pltpu.CompilerParams(has_side_effects=EFFECT),
    )(*lands, ssem, rsem, after)


def pair_send(cfg, grads):
    names = list(grads)
    n = len(names)

    def half_shape(name):
        _, nr, _, nc = _slab(cfg, name, 0)
        return (N_CHIPS, nr // 2, nc)

    def body(*refs):
        srcs, theirs = refs[:n], refs[n:2 * n]
        ssem, rsem = refs[2 * n:]
        x, y, c = _me()
        cps = []
        for i, name in enumerate(names):
            for k in range(N_CHIPS):
                cps.append(_remote(_half(cfg, name, srcs[i], k, 1 - c), theirs[i].at[k],
                                   ssem.at[N_CHIPS * i + k], rsem.at[N_CHIPS * i + k], (x, y, 1 - c)))
        for cp in cps:
            cp.start()
        for cp in cps:
            cp.wait()

    outs = pl.pallas_call(
        body, name="pair_send_" + "_".join(names), in_specs=[ANY] * n, out_specs=[ANY] * n,
        out_shape=[jax.ShapeDtypeStruct(half_shape(name), BF16) for name in names],
        scratch_shapes=[pltpu.SemaphoreType.DMA((N_CHIPS * n,))] * 2,
    )(*[grads[k] for k in names])
    return dict(zip(names, outs))


def pair_start(cfg, grads, after):
    names = list(grads)
    n = len(names)

    def body(*refs):
        srcs, theirs = refs[:n], refs[n:2 * n]
        ssem, rsem = refs[2 * n + 1], refs[2 * n + 2]
        token = refs[-1]
        x, y, c = _me()
        for i, name in enumerate(names):
            for k in range(N_CHIPS):
                _remote(_half(cfg, name, srcs[i], k, 1 - c), theirs[i].at[k],
                        ssem.at[N_CHIPS * i + k], rsem.at[N_CHIPS * i + k], (x, y, 1 - c)).start()
        token[...] = jnp.zeros_like(token)

    def half_shape(name):
        _, nr, _, nc = _slab(cfg, name, 0)
        return (N_CHIPS, nr // 2, nc)

    ops = [_hbm(grads[k]) for k in names] + [_hbm(lax.empty(half_shape(k), BF16)) for k in names]
    outs = pl.pallas_call(
        body, name="pair_start_" + "_".join(names),
        in_specs=[HBM] * (2 * n) + [ANY],
        out_specs=[SEM, SEM] + [HBM] * (2 * n) + [TOKEN],
        out_shape=[pltpu.SemaphoreType.DMA((N_CHIPS * n,)), pltpu.SemaphoreType.DMA((N_CHIPS * n,))]
        + [pltpu.HBM(a.shape, a.dtype) for a in ops] + [jax.ShapeDtypeStruct((8, LANE), F32)],
        input_output_aliases={i: 2 + i for i in range(2 * n)},
        compiler_params=pltpu.CompilerParams(has_side_effects=EFFECT),
    )(*ops, after)
    return outs[0], outs[1], dict(zip(names, outs[2:2 + n])), dict(zip(names, outs[2 + n:2 + 2 * n])), outs[-1]


def pair_wait(cfg, ssem, rsem, grads, theirs, after):
    names = list(grads)
    n = len(names)

    def body(*refs):
        srcs, theirs_ = refs[:n], refs[n:2 * n]
        ssem_, rsem_ = refs[2 * n], refs[2 * n + 1]
        x, y, c = _me()
        for i, name in enumerate(names):
            for k in range(N_CHIPS):
                cp = _remote(_half(cfg, name, srcs[i], k, 1 - c), theirs_[i].at[k],
                             ssem_.at[N_CHIPS * i + k], rsem_.at[N_CHIPS * i + k], (x, y, 1 - c))
                cp.wait_send()
                cp.wait_recv()

    ops = [grads[k] for k in names] + [theirs[k] for k in names]
    outs = pl.pallas_call(
        body, name="pair_wait_" + "_".join(names),
        in_specs=[HBM] * (2 * n) + [SEM, SEM, ANY], out_specs=[HBM] * (2 * n),
        out_shape=[pltpu.HBM(a.shape, a.dtype) for a in ops],
        input_output_aliases={i: i for i in range(2 * n)},
        compiler_params=pltpu.CompilerParams(has_side_effects=EFFECT),
    )(*ops, ssem, rsem, after)
    return dict(zip(names, outs[:n])), dict(zip(names, outs[n:]))


def pair_sum(cfg, name, grad, theirs, pos):
    _, r, c = theirs.shape
    tm, tc = _tile2(r, c)

    ni, nj = r // tm, c // tc
    total = N_CHIPS * ni * nj

    def body(pos_ref, g_ref, t_ref, o_ref, scr, sem):
        step = (pl.program_id(0) * ni + pl.program_id(1)) * nj + pl.program_id(2)

        def fetch(flat, slot):
            k, rem = flat // (ni * nj), flat % (ni * nj)
            r0, nr, c0, _ = _slab(cfg, name, k)
            rows = pl.ds(pl.multiple_of(r0 + pos_ref[1] * (nr // 2) + (rem // nj) * tm, 16), tm)
            cols = pl.ds(pl.multiple_of(c0 + (rem % nj) * tc, LANE), tc)
            return pltpu.make_async_copy(g_ref.at[rows, cols], scr.at[slot], sem.at[slot])

        @pl.when(step == 0)
        def _():
            fetch(0, 0).start()

        @pl.when(step + 1 < total)
        def _():
            fetch(step + 1, (step + 1) % 2).start()

        fetch(step, step % 2).wait()
        o_ref[...] = (scr[step % 2].astype(F32) + t_ref[...].astype(F32)).astype(BF16)

    blk = pl.BlockSpec((None, tm, tc), lambda k, i, j, p: (k, i, j))
    return pl.pallas_call(
        body, name=f"pair_sum_{name}",
        grid_spec=pltpu.PrefetchScalarGridSpec(
            num_scalar_prefetch=1, grid=(N_CHIPS, ni, nj), in_specs=[ANY, blk], out_specs=blk,
            scratch_shapes=[pltpu.VMEM((2, tm, tc), BF16), pltpu.SemaphoreType.DMA((2,))]),
        out_shape=jax.ShapeDtypeStruct(theirs.shape, BF16),
        compiler_params=_cp(cfg, ("arbitrary",) * 3),
    )(pos, grad, theirs)


def scatter_start(cfg, pres, after):
    names = list(pres)
    n = len(names)

    def body(*refs):
        srcs, lands = refs[:n], refs[n:2 * n]
        ssem, rsem = refs[2 * n + 1], refs[2 * n + 2]
        token = refs[-1]
        x, y, c = _me()
        for i in range(n):
            for j, (px, py) in enumerate(_other_chips(x, y)):
                _remote(srcs[i].at[2 * px + py], lands[i].at[j], ssem.at[3 * i + j], rsem.at[3 * i + j], (px, py, c)).start()
        token[...] = jnp.zeros_like(token)

    lands = [lax.empty((3,) + pres[k].shape[1:], BF16) for k in names]
    ops = [_hbm(a) for a in [pres[k] for k in names] + lands]
    outs = pl.pallas_call(
        body, name="scatter_start_" + "_".join(names),
        in_specs=[HBM] * (2 * n) + [ANY],
        out_specs=[SEM, SEM] + [HBM] * (2 * n) + [TOKEN],
        out_shape=[pltpu.SemaphoreType.DMA((3 * n,)), pltpu.SemaphoreType.DMA((3 * n,))]
        + [pltpu.HBM(a.shape, a.dtype) for a in ops] + [jax.ShapeDtypeStruct((8, LANE), F32)],
        input_output_aliases={i: 2 + i for i in range(2 * n)},
        compiler_params=pltpu.CompilerParams(has_side_effects=EFFECT),
    )(*ops, after)
    return outs[0], outs[1], dict(zip(names, outs[2:2 + n])), dict(zip(names, outs[2 + n:2 + 2 * n])), outs[-1]


def scatter_wait(cfg, ssem, rsem, pres, lands, after):
    names = list(pres)
    n = len(names)

    def body(*refs):
        srcs, lands_ = refs[:n], refs[n:2 * n]
        ssem_, rsem_ = refs[2 * n], refs[2 * n + 1]
        x, y, c = _me()
        for i in range(n):
            for j, (px, py) in enumerate(_other_chips(x, y)):
                cp = _remote(srcs[i].at[2 * px + py], lands_[i].at[j], ssem_.at[3 * i + j], rsem_.at[3 * i + j], (px, py, c))
                cp.wait_send()
                cp.wait_recv()

    ops = [pres[k] for k in names] + [lands[k] for k in names]
    outs = pl.pallas_call(
        body, name="scatter_wait_" + "_".join(names),
        in_specs=[HBM] * (2 * n) + [SEM, SEM, ANY], out_specs=[HBM] * (2 * n),
        out_shape=[pltpu.HBM(a.shape, a.dtype) for a in ops],
        input_output_aliases={i: i for i in range(2 * n)},
        compiler_params=pltpu.CompilerParams(has_side_effects=EFFECT),
    )(*ops, ssem, rsem, after)
    return dict(zip(names, outs[:n])), dict(zip(names, outs[n:]))


def sum_landed(cfg, name, pre, land, pos):
    _, r, c = pre.shape
    tm, tc = _tile2(r, c)
    nrt = r // tm

    def body(pos_ref, p_ref, l_ref, o_ref):
        acc = p_ref[...].astype(F32)
        for j in range(3):
            acc = acc + l_ref[j].astype(F32)
        o_ref[...] = acc

    return pl.pallas_call(
        body, name=f"sum_landed_{name}",
        grid_spec=pltpu.PrefetchScalarGridSpec(
            num_scalar_prefetch=1, grid=(nrt, c // tc),
            in_specs=[pl.BlockSpec((None, tm, tc), lambda i, j, p: (p[0], i, j)),
                      pl.BlockSpec((3, tm, tc), lambda i, j, p: (0, i, j))],
            out_specs=pl.BlockSpec((tm, tc), lambda i, j, p: (p[1] * nrt + i, j))),
        out_shape=jax.ShapeDtypeStruct((2 * r, c), F32), compiler_params=_cp(cfg, ("parallel", "parallel")),
    )(pos, pre, land)


def half_swap(cfg, sums):
    names = list(sums)
    n = len(names)

    def body(*refs):
        outs = refs[n:2 * n]
        ssem, rsem = refs[2 * n:]
        x, y, c = _me()
        cps = [_remote(_rows_half(outs[i], c), _rows_half(outs[i], c), ssem.at[i], rsem.at[i], (x, y, 1 - c))
               for i in range(n)]
        for cp in cps:
            cp.start()
        for i in range(n):
            theirs = _rows_half(outs[i], 1 - c)
            _remote(theirs, theirs, ssem.at[i], rsem.at[i], (x, y, 1 - c)).wait_recv()
        for cp in cps:
            cp.wait_send()

    outs = pl.pallas_call(
        body, name="half_swap_" + "_".join(names), in_specs=[ANY] * n, out_specs=[ANY] * n,
        out_shape=[jax.ShapeDtypeStruct(sums[k].shape, F32) for k in names],
        input_output_aliases={i: i for i in range(n)},
        scratch_shapes=[pltpu.SemaphoreType.DMA((n,))] * 2,
    )(*[sums[k] for k in names])
    return dict(zip(names, outs))


def half_swap_start(cfg, sums):
    names = list(sums)
    n = len(names)

    def body(*refs):
        outs = refs[:n]
        ssem, rsem = refs[n + 1], refs[n + 2]
        token = refs[-1]
        x, y, c = _me()
        for i in range(n):
            mine = _rows_half(outs[i], c)
            _remote(mine, mine, ssem.at[i], rsem.at[i], (x, y, 1 - c)).start()
        token[...] = jnp.zeros_like(token)

    ops = [_hbm(sums[k]) for k in names]
    outs = pl.pallas_call(
        body, name="half_swap_start_" + "_".join(names),
        in_specs=[HBM] * n + [ANY], out_specs=[SEM, SEM] + [HBM] * n + [TOKEN],
        out_shape=[pltpu.SemaphoreType.DMA((n,)), pltpu.SemaphoreType.DMA((n,))]
        + [pltpu.HBM(a.shape, a.dtype) for a in ops] + [jax.ShapeDtypeStruct((8, LANE), F32)],
        input_output_aliases={i: 2 + i for i in range(n)},
        compiler_params=pltpu.CompilerParams(has_side_effects=EFFECT),
    )(*ops, jnp.zeros((8, LANE), F32))
    return outs[0], outs[1], dict(zip(names, outs[2:2 + n])), outs[-1]


def half_swap_wait(cfg, ssem, rsem, sums, after):
    names = list(sums)
    n = len(names)

    def body(*refs):
        outs = refs[:n]
        ssem_, rsem_ = refs[n], refs[n + 1]
        x, y, c = _me()
        for i in range(n):
            cp = _remote(_rows_half(outs[i], c), _rows_half(outs[i], 1 - c), ssem_.at[i], rsem_.at[i], (x, y, 1 - c))
            cp.wait_send()
            cp.wait_recv()

    outs = pl.pallas_call(
        body, name="half_swap_wait_" + "_".join(names),
        in_specs=[HBM] * n + [SEM, SEM, ANY], out_specs=[HBM] * n,
        out_shape=[pltpu.HBM(sums[k].shape, sums[k].dtype) for k in names],
        input_output_aliases={i: i for i in range(n)},
        compiler_params=pltpu.CompilerParams(has_side_effects=EFFECT),
    )(*[sums[k] for k in names], ssem, rsem, after)
    return dict(zip(names, outs))


class MeshWeights:
    def __init__(self, cfg, w_sh):
        self.cfg = cfg
        self.pos = jnp.stack([2 * lax.axis_index("x") + lax.axis_index("y"), lax.axis_index("c")]).astype(jnp.int32)
        self.w_sh = w_sh
        self.full = {"w_in": cast_into(cfg, "w_in", w_sh["w_in"], self.pos)[0],
                     "conv_w": pad_conv_w(cfg, w_sh["conv_w"], self.pos)}
        self.inflight = {}
        self.forwards = {}
        self.grads = {}

    def first_start(self):
        cfg = self.cfg
        self.first, token = gather_start(cfg, "first", FIRST, self.full, jnp.zeros((8, LANE), F32))
        for k in REST:
            self.full[k], token = cast_into(cfg, k, self.w_sh[k], self.pos, dep=token)
        return token

    def weights_first(self, after):
        cfg = self.cfg
        ssem, rsem, lands = self.first[0]
        w_in, conv_w = gather_wait(cfg, FIRST[0], ssem, rsem, lands, after)
        return gather_finish(cfg, ("w_in",), [w_in])[0], conv_w

    def start_rest(self, after):
        self.rest, token = gather_start(self.cfg, "rest", GROUPS, self.full, after)
        return token

    def weights_rest(self, group, after):
        cfg = self.cfg
        names = GROUPS[group]
        ssem, rsem, lands = self.rest[group]
        lands = dict(zip(names, gather_wait(cfg, names, ssem, rsem, lands, after)))
        now = [k for k in names if k != "w_up"]
        later = [k for k in names if k == "w_up"]
        ready = gather_finish(cfg, tuple(now), [lands[k] for k in now])
        if not later:
            return tuple(ready), None
        out = forward_start(cfg, tuple(later), [lands[k] for k in later], ready[0])
        self.forwards[group] = (tuple(later),) + tuple(out[:3])
        return tuple(ready), out[3]

    def forwarded(self, group, after):
        names, ssem, rsem, lands = self.forwards.pop(group)
        return tuple(forward_wait(self.cfg, names, ssem, rsem, lands, after))

    def pair_start(self, grads):
        out = pair_start(self.cfg, grads, jnp.zeros((8, LANE), F32))
        self.pairs = out[:4]
        return out[4]

    def reduce_start(self, grads, after=None):
        theirs = pair_send(self.cfg, grads) if grads else {}
        if after is not None:
            early, early_theirs = pair_wait(self.cfg, *self.pairs, after)
            grads, theirs = {**early, **grads}, {**early_theirs, **theirs}
        pres = {k: pair_sum(self.cfg, k, grads[k], theirs[k], self.pos) for k in grads}
        out = scatter_start(self.cfg, pres, jnp.zeros((8, LANE), F32))
        self.inflight[tuple(sorted(grads))] = out[:4]
        return out[4]

    def reduce_wait(self, names, after, finish_later=False):
        cfg = self.cfg
        pres, lands = scatter_wait(cfg, *self.inflight.pop(tuple(sorted(names))), after)
        sums = {k: sum_landed(cfg, k, pres[k], lands[k], self.pos) for k in names}
        if not finish_later:
            self.grads.update(half_swap(cfg, sums))
            return None
        out = half_swap_start(cfg, sums)
        self.swaps = out[:3]
        return out[3]

    def reduce_finish(self, after):
        self.grads.update(half_swap_wait(self.cfg, *self.swaps, after))


def allreduce_small(cfg, vec):
    R = vec.shape[0]

    def body(v_ref, o_ref, buf, send_sems, recv_sems):
        x, y, c = _me()
        me = 4 * x + 2 * y + c
        buf[me] = v_ref[...]
        sends = []
        for k in range(1, N_DEV):
            px, py, pc = x ^ (k >> 2), y ^ ((k >> 1) & 1), c ^ (k & 1)
            sends.append(pltpu.make_async_remote_copy(
                src_ref=v_ref, dst_ref=buf.at[me], send_sem=send_sems.at[k], recv_sem=recv_sems.at[k],
                device_id=(px, py, pc), device_id_type=MESH))
        for cp in sends:
            cp.start()
        for k in range(1, N_DEV):
            px, py, pc = x ^ (k >> 2), y ^ ((k >> 1) & 1), c ^ (k & 1)
            pltpu.make_async_remote_copy(
                src_ref=v_ref, dst_ref=buf.at[4 * px + 2 * py + pc], send_sem=send_sems.at[k],
                recv_sem=recv_sems.at[k], device_id=(px, py, pc), device_id_type=MESH).wait_recv()
        for cp in sends:
            cp.wait_send()
        acc = buf[0]
        for j in range(1, N_DEV):
            acc = acc + buf[j]
        o_ref[...] = acc

    return pl.pallas_call(
        body, name="allreduce_small",
        in_specs=[pl.BlockSpec(memory_space=pltpu.VMEM)], out_specs=pl.BlockSpec(memory_space=pltpu.VMEM),
        out_shape=jax.ShapeDtypeStruct((R, LANE), F32),
        scratch_shapes=[pltpu.VMEM((N_DEV, R, LANE), F32), pltpu.SemaphoreType.DMA((N_DEV,)),
                        pltpu.SemaphoreType.DMA((N_DEV,))],
    )(vec)


def _adamw_update(w_ref, m_ref, v_ref, g, outs):
    g_out, d_out, m_out, v_out = outs
    m_new = ADAM_B1 * m_ref[...] + (1.0 - ADAM_B1) * g
    v_new = ADAM_B2 * v_ref[...] + (1.0 - ADAM_B2) * jnp.square(g)
    m_hat = m_new / (1.0 - ADAM_B1 ** ADAM_STEP)
    v_hat = v_new / (1.0 - ADAM_B2 ** ADAM_STEP)
    g_out[...] = g
    d_out[...] = -ADAM_LR * (m_hat / (jnp.sqrt(v_hat) + ADAM_EPS) + ADAM_WD * w_ref[...])
    m_out[...] = m_new
    v_out[...] = v_new


def adamw_small(cfg, params):
    names = list(params)
    n = len(names)

    def body(*refs):
        for i in range(n):
            w_ref, m_ref, v_ref, g_ref = refs[4 * i:4 * i + 4]
            _adamw_update(w_ref, m_ref, v_ref, g_ref[...], refs[4 * n + 4 * i:4 * n + 4 * i + 4])

    outs = pl.pallas_call(
        body, name="adamw_small",
        out_shape=[jax.ShapeDtypeStruct(params[k][0].shape, F32) for k in names for _ in range(4)],
    )(*[a for k in names for a in params[k]])
    return {k: list(outs[4 * i:4 * i + 4]) for i, k in enumerate(names)}


def adamw(cfg, name, w, m, v, g_parts, tile, dep=None):
    r, c = w.shape
    tm, tc = tile[0] or r, tile[1] or c
    assert tc == c or all(g.shape[1] == c for g in g_parts)
    n = len(g_parts)
    deps = [] if dep is None else [dep]

    def body(*refs):
        w_ref, m_ref, v_ref = refs[:3]
        g_refs = refs[3:3 + n]
        g = g_refs[0][:, :tc]
        for gr in g_refs[1:]:
            g = g + gr[:, :tc]
        if deps:
            g = g + refs[3 + n][0:1, 0:1]
        _adamw_update(w_ref, m_ref, v_ref, g, refs[3 + n + len(deps):])

    blk = pl.BlockSpec((tm, tc), lambda i, j: (i, j))
    return pl.pallas_call(
        body, name=f"adamw_{name}", grid=(r // tm, c // tc),
        in_specs=[blk] * 3 + [pl.BlockSpec((tm, tc if tc < c else g.shape[1]), lambda i, j: (i, j)) for g in g_parts]
        + [pl.BlockSpec((8, LANE), lambda i, j: (0, 0))] * len(deps),
        out_specs=[blk] * 4, out_shape=[jax.ShapeDtypeStruct((r, c), F32)] * 4,
        compiler_params=_cp(cfg, ("parallel", "parallel")),
    )(w, m, v, *g_parts, *deps)


SMALL_ORDER = ("loss", "g1", "g2", "g3", "g4", "g_sb", "g_dl", "conv_b", "conv_w")


def pack_small(small):
    rows = []
    for k in SMALL_ORDER:
        a = small[k].reshape(-1, LANE)
        rows.append(a)
    flat = jnp.concatenate(rows, axis=0)
    pad = (-flat.shape[0]) % 8
    return jnp.pad(flat, ((0, pad), (0, 0))), [r.shape[0] for r in rows]


def unpack_small(red, small, counts):
    out, at = {}, 0
    for k, n in zip(SMALL_ORDER, counts):
        out[k] = red[at:at + n].reshape(small[k].shape)
        at += n
    return out


def pad_ff(cfg, a):
    r = a.shape[0]
    return jnp.pad(a.reshape(r, N_CHIPS, cfg.FSH), ((0, 0), (0, 0), (0, cfg.FSHP - cfg.FSH))).reshape(r, cfg.FF2P)


def step(cfg, x, target, gains, w_sh, conv_b, m_all, v_all):
    chip = 2 * lax.axis_index("x") + lax.axis_index("y")
    comm = MeshWeights(cfg, w_sh)
    grad_x, small = local_step(cfg, comm, x, target, gains["g1"], gains["g2"], gains["g3"], gains["g4"],
                               gains["g_sb"], gains["g_dl"], pad_ff(cfg, conv_b))

    packed, counts = pack_small(small)
    summed = allreduce_small(cfg, packed)
    comm.reduce_wait(("w_in",), after=summed)
    red = unpack_small(summed, small, counts)

    names = ("w_in", "w_out", "w_up", "w_down")
    up_rows = max(t for t in range(SUB, 513, SUB) if cfg.FSH % t == 0)
    tms = dict(w_in=(cfg.TM, None), w_out=(cfg.TM, None), w_up=(up_rows, None), w_down=(None, cfg.TN // 2))
    res = {}
    for n in names:
        res[n] = adamw(cfg, n, w_sh[n], m_all[n], v_all[n], [comm.grads[n]], tms[n],
                       dep=None if n == "w_in" else comm.late_token)
    g_cw = lax.dynamic_slice_in_dim(red["conv_w"].reshape(3, N_CHIPS, cfg.FSHP), chip, 1, axis=1)[:, 0, :cfg.FSH]
    g_cb = red["conv_b"].reshape(1, N_CHIPS, cfg.FSHP)[:, :, :cfg.FSH].reshape(1, N_CHIPS * cfg.FSH)
    smalls = {"conv_w": (w_sh["conv_w"], g_cw), "conv_b": (conv_b, g_cb)}
    smalls.update({k: (gains[k], red[k]) for k in ("g1", "g2", "g3", "g4", "g_sb", "g_dl")})
    res.update(adamw_small(cfg, {k: (w, m_all[k], v_all[k], g) for k, (w, g) in smalls.items()}))
    return red["loss"][0, 0], grad_x, res


PARAMS = ("pre_mix_gain", "post_mix_gain", "pre_ffn_gain", "post_ffn_gain", "w_in", "sb_out_gain", "dil_out_gain",
          "w_out", "w_up", "conv_w", "conv_b", "w_down")
SHORT = dict(pre_mix_gain="g1", post_mix_gain="g2", pre_ffn_gain="g3", post_ffn_gain="g4", sb_out_gain="g_sb",
             dil_out_gain="g_dl", w_in="w_in", w_out="w_out", w_up="w_up", conv_w="conv_w", conv_b="conv_b",
             w_down="w_down")


def kernel(x, pre_mix_gain, post_mix_gain, pre_ffn_gain, post_ffn_gain, w_in, sb_out_gain, dil_out_gain, w_out, w_up, conv_w, conv_b, w_down, loss_target, m_pre_mix_gain, m_post_mix_gain, m_pre_ffn_gain, m_post_ffn_gain, m_w_in, m_sb_out_gain, m_dil_out_gain, m_w_out, m_w_up, m_conv_w, m_conv_b, m_w_down, v_pre_mix_gain, v_post_mix_gain, v_pre_ffn_gain, v_post_ffn_gain, v_w_in, v_sb_out_gain, v_dil_out_gain, v_w_out, v_w_up, v_conv_w, v_conv_b, v_w_down):
    cfg = CFG
    w = dict(zip(PARAMS, (pre_mix_gain, post_mix_gain, pre_ffn_gain, post_ffn_gain, w_in, sb_out_gain, dil_out_gain,
                          w_out, w_up, conv_w, conv_b, w_down)))
    m = dict(zip(PARAMS, (m_pre_mix_gain, m_post_mix_gain, m_pre_ffn_gain, m_post_ffn_gain, m_w_in, m_sb_out_gain,
                          m_dil_out_gain, m_w_out, m_w_up, m_conv_w, m_conv_b, m_w_down)))
    v = dict(zip(PARAMS, (v_pre_mix_gain, v_post_mix_gain, v_pre_ffn_gain, v_post_ffn_gain, v_w_in, v_sb_out_gain,
                          v_dil_out_gain, v_w_out, v_w_up, v_conv_w, v_conv_b, v_w_down)))
    sq = lambda a: a.reshape(a.shape[1:])
    ws = {SHORT[k]: sq(a) if a.ndim == 3 else a for k, a in w.items()}
    ms = {SHORT[k]: sq(a) if a.ndim == 3 else a for k, a in m.items()}
    vs = {SHORT[k]: sq(a) if a.ndim == 3 else a for k, a in v.items()}
    for d in (ws, ms, vs):
        d["w_up"] = d["w_up"].T
    gains = {k: ws[k] for k in ("g1", "g2", "g3", "g4", "g_sb", "g_dl")}
    w_sh = {k: ws[k] for k in ("w_in", "w_out", "w_up", "conv_w", "w_down")}
    loss, grad_x, res = step(cfg, sq(x), sq(loss_target), gains, w_sh, ws["conv_b"], ms, vs)
    res["w_up"] = [a.T for a in res["w_up"]]
    outs = [loss, grad_x.reshape(x.shape)]
    for i in range(4):
        for k in PARAMS:
            outs.append(res[SHORT[k]][i].reshape(w[k].shape))
    return tuple(outs)
```

```python
import functools
import math
from typing import NamedTuple

import jax
import jax.numpy as jnp
from jax import lax
from jax.experimental import pallas as pl
from jax.experimental.pallas import tpu as pltpu

F32 = jnp.float32
BF16 = jnp.bfloat16
MESH = pl.DeviceIdType.MESH

ROPE_THETA = 10000.0
RMS_EPS = 1e-6
ADAM_LR = 0.001
ADAM_B1 = 0.9
ADAM_B2 = 0.999
ADAM_EPS = 1e-08
ADAM_WD = 0.01
ADAM_STEP = 10
GELU_C = math.sqrt(2.0 / math.pi)
NEG_BIG = -1e30
LANE = 128
N_CHIPS = 4
N_DEV = 8


class Cfg(NamedTuple):
    S: int = 2048
    D: int = 2048
    DH: int = 128
    HSB: int = 8
    HDL: int = 8
    QB: int = 128
    SBT: int = 256
    SBH: int = 4
    SBHB: int = 4
    branches: tuple = ((128, 1), (512, 4), (2048, 16))
    FSH: int = 2752
    FSHP: int = 2816
    TM: int = 256
    TNF: int = 256
    FCH: int = 512
    TN: int = 512
    VMEM_MB: int = 56

    @property
    def DSB(self):
        return self.HSB * self.DH

    @property
    def DDL(self):
        return self.HDL * self.DH

    @property
    def DMIX(self):
        return self.DSB + self.DDL

    @property
    def FFP(self):
        return 2 * self.FSHP

    @property
    def FF2P(self):
        return 4 * self.FSHP


CFG = Cfg()


def _cp(cfg, sem=None):
    return pltpu.CompilerParams(dimension_semantics=sem, vmem_limit_bytes=cfg.VMEM_MB * 2**20)


def _dot(a, b):
    return jnp.dot(a, b, preferred_element_type=F32)


def _dot_nt(a, b):
    return lax.dot_general(a, b, (((1,), (1,)), ((), ())), preferred_element_type=F32)


def _dot_tn(a, b):
    return lax.dot_general(a, b, (((0,), (0,)), ((), ())), preferred_element_type=F32)


def _dot_split(x, u):
    hi = x.astype(BF16)
    lo = (x - hi.astype(F32)).astype(BF16)
    return _dot(hi, u) + _dot(lo, u)


def _rstd(x):
    return lax.rsqrt(jnp.mean(x * x, axis=-1, keepdims=True) + RMS_EPS)


def _rms_bwd(dy, x, g):
    r = _rstd(x)
    xh = x * r
    dxh = dy * g
    dx = r * (dxh - xh * jnp.mean(dxh * xh, axis=-1, keepdims=True))
    return dx, dy * xh


def _gelu(x):
    t = jnp.tanh(GELU_C * (x + 0.044715 * (x * x * x)))
    return 0.5 * x * (1.0 + t), t


def _gelu_grad(x, t):
    return 0.5 * (1.0 + t) + 0.5 * x * (1.0 - t * t) * (GELU_C * (1.0 + 3 * 0.044715 * (x * x)))


def _row(cfg, w):
    return pl.BlockSpec((cfg.TM, w), lambda i: (i, 0))


def _vec(w):
    return pl.BlockSpec((1, w), lambda i: (0, 0))


def rms_fwd(cfg, x, g):
    S, D = x.shape

    def body(x_ref, g_ref, h_ref):
        xv = x_ref[...]
        h_ref[...] = (xv * _rstd(xv) * g_ref[...]).astype(BF16)

    return pl.pallas_call(
        body, name="rms_fwd", grid=(S // cfg.TM,),
        in_specs=[_row(cfg, D), _vec(D)], out_specs=_row(cfg, D),
        out_shape=jax.ShapeDtypeStruct((S, D), BF16), compiler_params=_cp(cfg, ("parallel",)),
    )(x, g)


def mid_fwd(cfg, x, mo, g_post, g_pre):
    S, D = x.shape

    def body(x_ref, mo_ref, gp_ref, gn_ref, x1_ref, h2_ref):
        mo_v = mo_ref[...]
        x1 = x_ref[...] + mo_v * _rstd(mo_v) * gp_ref[...]
        x1_ref[...] = x1
        h2_ref[...] = (x1 * _rstd(x1) * gn_ref[...]).astype(BF16)

    return pl.pallas_call(
        body, name="mid_fwd", grid=(S // cfg.TM,),
        in_specs=[_row(cfg, D), _row(cfg, D), _vec(D), _vec(D)],
        out_specs=[_row(cfg, D), _row(cfg, D)],
        out_shape=[jax.ShapeDtypeStruct((S, D), F32), jax.ShapeDtypeStruct((S, D), BF16)],
        compiler_params=_cp(cfg, ("parallel",)),
    )(x, mo, g_post, g_pre)


def final_fwd_bwd(cfg, x1, f, g_post, target):
    S, D = x1.shape

    def body(x1_ref, f_ref, g_ref, t_ref, dout_ref, df_ref, dg_ref, loss_ref):
        @pl.when(pl.program_id(0) == 0)
        def _():
            dg_ref[...] = jnp.zeros_like(dg_ref)
            loss_ref[...] = jnp.zeros_like(loss_ref)

        fv = f_ref[...]
        g = g_ref[...]
        out = x1_ref[...] + fv * _rstd(fv) * g
        err = out - t_ref[...]
        loss_ref[...] += 0.5 * jnp.sum(jnp.mean(err * err, axis=-1, keepdims=True), axis=0, keepdims=True)
        dout = err * (1.0 / D)
        dout_ref[...] = dout
        df, dgx = _rms_bwd(dout, fv, g)
        df_ref[...] = df.astype(BF16)
        dg_ref[...] += jnp.sum(dgx, axis=0, keepdims=True)

    return pl.pallas_call(
        body, name="final_fwd_bwd", grid=(S // cfg.TM,),
        in_specs=[_row(cfg, D), _row(cfg, D), _vec(D), _row(cfg, D)],
        out_specs=[_row(cfg, D), _row(cfg, D), _vec(D), _vec(LANE)],
        out_shape=[jax.ShapeDtypeStruct((S, D), F32), jax.ShapeDtypeStruct((S, D), BF16),
                   jax.ShapeDtypeStruct((1, D), F32), jax.ShapeDtypeStruct((1, LANE), F32)],
        compiler_params=_cp(cfg, ("arbitrary",)),
    )(x1, f, g_post, target)


def mid_bwd(cfg, dh2, x1, g_pre, dout, mo, g_post):
    S, D = x1.shape

    def body(dh_ref, x1_ref, gn_ref, do_ref, mo_ref, gp_ref, dx1_ref, dmo_ref, dgn_ref, dgp_ref):
        @pl.when(pl.program_id(0) == 0)
        def _():
            dgn_ref[...] = jnp.zeros_like(dgn_ref)
            dgp_ref[...] = jnp.zeros_like(dgp_ref)

        dx, dgx = _rms_bwd(dh_ref[...], x1_ref[...], gn_ref[...])
        dx1 = do_ref[...] + dx
        dx1_ref[...] = dx1
        dgn_ref[...] += jnp.sum(dgx, axis=0, keepdims=True)
        dmo, dgy = _rms_bwd(dx1, mo_ref[...], gp_ref[...])
        dmo_ref[...] = dmo.astype(BF16)
        dgp_ref[...] += jnp.sum(dgy, axis=0, keepdims=True)

    return pl.pallas_call(
        body, name="mid_bwd", grid=(S // cfg.TM,),
        in_specs=[_row(cfg, D), _row(cfg, D), _vec(D), _row(cfg, D), _row(cfg, D), _vec(D)],
        out_specs=[_row(cfg, D), _row(cfg, D), _vec(D), _vec(D)],
        out_shape=[jax.ShapeDtypeStruct((S, D), F32), jax.ShapeDtypeStruct((S, D), BF16),
                   jax.ShapeDtypeStruct((1, D), F32), jax.ShapeDtypeStruct((1, D), F32)],
        compiler_params=_cp(cfg, ("arbitrary",)),
    )(dh2, x1, g_pre, dout, mo, g_post)


def first_bwd(cfg, dh1, x, g_pre, dx1):
    S, D = x.shape

    def body(dh_ref, x_ref, g_ref, r_ref, dx_ref, dg_ref):
        @pl.when(pl.program_id(0) == 0)
        def _():
            dg_ref[...] = jnp.zeros_like(dg_ref)

        dx, dgx = _rms_bwd(dh_ref[...], x_ref[...], g_ref[...])
        dx_ref[...] = r_ref[...] + dx
        dg_ref[...] += jnp.sum(dgx, axis=0, keepdims=True)

    return pl.pallas_call(
        body, name="first_bwd", grid=(S // cfg.TM,),
        in_specs=[_row(cfg, D), _row(cfg, D), _vec(D), _row(cfg, D)],
        out_specs=[_row(cfg, D), _vec(D)],
        out_shape=[jax.ShapeDtypeStruct((S, D), F32), jax.ShapeDtypeStruct((1, D), F32)],
        compiler_params=_cp(cfg, ("arbitrary",)),
    )(dh1, x, g_pre, dx1)


def _mm(cfg, name, a, b, *, nt, a_spec, b_spec, o_spec, grid, out_shape, acc_shape, dep=None):
    nk = grid[-1]
    dot = _dot_nt if nt else _dot
    deps = [] if dep is None else [dep]

    def body(a_ref, b_ref, *rest):
        o_ref, acc_ref = rest[-2:]
        k = pl.program_id(len(grid) - 1)
        part = dot(a_ref[...], b_ref[...])
        if deps:
            part = part + rest[0][0:1, 0:1]
        if nk == 1:
            o_ref[...] = part.astype(o_ref.dtype)
            return

        @pl.when(k == 0)
        def _():
            acc_ref[...] = part

        @pl.when(k > 0)
        def _():
            acc_ref[...] += part

        @pl.when(k == nk - 1)
        def _():
            o_ref[...] = acc_ref[...].astype(o_ref.dtype)

    sem = ("parallel",) * (len(grid) - 1) + ("arbitrary",)
    dep_specs = [pl.BlockSpec((8, LANE), lambda *_: (0, 0))] * len(deps)
    return pl.pallas_call(
        body, name=name, grid=grid, in_specs=[a_spec, b_spec] + dep_specs, out_specs=o_spec, out_shape=out_shape,
        scratch_shapes=[pltpu.VMEM(acc_shape, F32)], compiler_params=_cp(cfg, sem),
    )(a, b, *deps)


def _mm_tn(cfg, name, a, b, *, a_spec, b_spec, o_spec, grid, out_shape, dep=None):
    deps = [] if dep is None else [dep]

    def body(a_ref, b_ref, *rest):
        part = _dot_tn(a_ref[...], b_ref[...])
        if deps:
            part = part + rest[0][0:1, 0:1]
        rest[-1][...] = part.astype(rest[-1].dtype)

    dep_specs = [pl.BlockSpec((8, LANE), lambda *_: (0, 0))] * len(deps)
    return pl.pallas_call(
        body, name=name, grid=grid, in_specs=[a_spec, b_spec] + dep_specs, out_specs=o_spec, out_shape=out_shape,
        compiler_params=_cp(cfg, ("parallel",) * len(grid)),
    )(a, b, *deps)


def qkv_proj(cfg, h1, w_in, cos2, sin2):
    S, D = h1.shape
    tn = 2 * cfg.DH
    per = cfg.DSB // tn
    assert cfg.DSB == cfg.DDL
    nblk = 6 * per

    def body(a_ref, b_ref, c_ref, s_ref, o_ref):
        j = pl.program_id(0)
        acc = _dot(a_ref[...], b_ref[...])
        rope = jnp.logical_and(j >= 3 * per, j < 5 * per)

        @pl.when(rope)
        def _():
            for c in range(tn // cfg.DH):
                xh = acc[:, c * cfg.DH:(c + 1) * cfg.DH]
                o_ref[:, c * cfg.DH:(c + 1) * cfg.DH] = (
                    xh * c_ref[...] + pltpu.roll(xh, cfg.DH // 2, 1) * s_ref[...]).astype(BF16)

        @pl.when(jnp.logical_not(rope))
        def _():
            o_ref[...] = acc.astype(BF16)

    return pl.pallas_call(
        body, name="qkv_proj", grid=(nblk,),
        in_specs=[pl.BlockSpec((S, D), lambda j: (0, 0)), pl.BlockSpec((D, tn), lambda j: (0, j)),
                  pl.BlockSpec((S, cfg.DH), lambda j: (0, 0)), pl.BlockSpec((S, cfg.DH), lambda j: (0, 0))],
        out_specs=pl.BlockSpec((None, S, tn), lambda j: (j // per, 0, j % per)),
        out_shape=jax.ShapeDtypeStruct((6, S, cfg.DSB), BF16),
        compiler_params=_cp(cfg, ("parallel",)),
    )(h1, w_in, cos2, sin2)


def _sb_tile(cfg, q, k, valid):
    z = _dot_nt(q, k) * (cfg.DH ** -0.5)
    lb = jnp.minimum(z, 0.0) - jnp.log1p(jnp.exp(-jnp.abs(z)))
    lk = lb - z
    return lb, (lk if valid is None else jnp.where(valid, lk, 0.0))


def _masked(valid, x):
    return x if valid is None else jnp.where(valid, x, 0.0)


def sb_fwd(cfg, qkv3):
    S, QB, DH, NH = cfg.S, cfg.SBT, cfg.DH, cfg.SBH

    def body(q_ref, k_ref, v_ref, o_ref, t_ref):
        row = lax.broadcasted_iota(jnp.int32, (QB, QB), 0)
        col = lax.broadcasted_iota(jnp.int32, (QB, QB), 1)
        u_after = (row > col).astype(BF16)
        causal = col < row
        heads = [slice(h * DH, (h + 1) * DH) for h in range(NH)]

        def q_loop(qb, _):
            rows = pl.ds(pl.multiple_of(qb * QB, QB), QB)
            qs = [q_ref[rows, hd] for hd in heads]

            def tile(kb, carry, valid):
                krows = pl.ds(pl.multiple_of(kb * QB, QB), QB)
                lbk = [_sb_tile(cfg, q, k_ref[krows, hd], valid) for q, hd in zip(qs, heads)]
                rems = [_dot_split(lk, u_after) for _, lk in lbk]
                aa = [_masked(valid, jnp.exp(lb + rem + c)).astype(BF16) for (lb, _), rem, (_, c) in zip(lbk, rems, carry)]
                return tuple((o_acc + _dot(a, v_ref[krows, hd]), c + jnp.sum(lk, axis=1, keepdims=True))
                             for a, hd, (_, lk), (o_acc, c) in zip(aa, heads, lbk, carry))

            carry = tile(qb, ((jnp.zeros((QB, DH), F32), jnp.zeros((QB, 1), F32)),) * NH, causal)
            carry = lax.fori_loop(0, qb, lambda i, cr: tile(qb - 1 - i, cr, None), carry)
            for hd, (o_acc, c) in zip(heads, carry):
                o_ref[rows, hd] = o_acc
                t_ref[rows, hd] = jnp.broadcast_to(c, (QB, DH))
            return 0

        lax.fori_loop(0, S // QB, q_loop, 0)

    def spec(i):
        return pl.BlockSpec((None, S, NH * DH), lambda h: (i, 0, h))

    return pl.pallas_call(
        body, name="sb_fwd", grid=(cfg.HSB // NH,),
        in_specs=[spec(0), spec(1), spec(2)],
        out_specs=[pl.BlockSpec((S, NH * DH), lambda h: (0, h))] * 2,
        out_shape=[jax.ShapeDtypeStruct((S, cfg.DSB), F32)] * 2,
        compiler_params=_cp(cfg, ("parallel",)),
    )(qkv3, qkv3, qkv3)


def sb_bwd(cfg, qkv3, do_sb, tsum):
    S, QB, DH, NH = cfg.S, cfg.SBT, cfg.DH, cfg.SBHB
    scale = DH ** -0.5

    def body(q_ref, k_ref, v_ref, do_ref, t_ref, d_ref, dk_acc, dv_acc):
        dk_acc[...] = jnp.zeros_like(dk_acc)
        dv_acc[...] = jnp.zeros_like(dv_acc)
        row = lax.broadcasted_iota(jnp.int32, (QB, QB), 0)
        col = lax.broadcasted_iota(jnp.int32, (QB, QB), 1)
        u_upto = (row <= col).astype(BF16)
        u_before = (row < col).astype(BF16)
        causal = col < row
        heads = [slice(h * DH, (h + 1) * DH) for h in range(NH)]

        def q_loop(qb, _):
            rows = pl.ds(pl.multiple_of(qb * QB, QB), QB)
            qs = [q_ref[rows, hd] for hd in heads]
            dos = [do_ref[rows, hd] for hd in heads]
            totals = [t_ref[rows, hd.start:hd.start + 1] for hd in heads]

            def tile(kb, carry, valid):
                krows = pl.ds(pl.multiple_of(kb * QB, QB), QB)
                ks = [k_ref[krows, hd] for hd in heads]
                lbk = [_sb_tile(cfg, q, k, valid) for q, k in zip(qs, ks)]
                das = [_dot_nt(do, v_ref[krows, hd]) for do, hd in zip(dos, heads)]
                pins = [_dot_split(lk, u_upto) for _, lk in lbk]
                aa = [_masked(valid, jnp.exp(lb + (tot - pc - pin)))
                      for (lb, _), tot, (_, pc, _), pin in zip(lbk, totals, carry, pins)]
                gs = [a * da for a, da in zip(aa, das)]
                for a, do, hd in zip(aa, dos, heads):
                    dv_acc[krows, hd] += _dot_tn(a.astype(BF16), do)
                cums = [gc + _dot(g.astype(BF16), u_before) for g, (_, _, gc) in zip(gs, carry)]
                dzs = [(_masked(valid, g - jnp.exp(lb) * (g + cum)) * scale).astype(BF16)
                       for g, (lb, _), cum in zip(gs, lbk, cums)]
                for dz, q, hd in zip(dzs, qs, heads):
                    dk_acc[krows, hd] += _dot_tn(dz, q)
                return tuple((dq + _dot(dz, k), pc + jnp.sum(lk, axis=1, keepdims=True), gc + jnp.sum(g, axis=1, keepdims=True))
                             for dz, k, (_, lk), g, (dq, pc, gc) in zip(dzs, ks, lbk, gs, carry))

            z1 = jnp.zeros((QB, 1), F32)
            carry = lax.fori_loop(0, qb, lambda kb, cr: tile(kb, cr, None), ((jnp.zeros((QB, DH), F32), z1, z1),) * NH)
            for hd, (dq_acc, _, _) in zip(heads, tile(qb, carry, causal)):
                d_ref[0, rows, hd] = dq_acc.astype(BF16)
            return 0

        lax.fori_loop(0, S // QB, q_loop, 0)
        d_ref[1, :, :] = dk_acc[...].astype(BF16)
        d_ref[2, :, :] = dv_acc[...].astype(BF16)

    def spec(i):
        return pl.BlockSpec((None, S, NH * DH), lambda h: (i, 0, h))

    hd_spec = pl.BlockSpec((S, NH * DH), lambda h: (0, h))
    return pl.pallas_call(
        body, name="sb_bwd", grid=(cfg.HSB // NH,),
        in_specs=[spec(0), spec(1), spec(2), hd_spec, hd_spec],
        out_specs=pl.BlockSpec((3, S, NH * DH), lambda h: (0, 0, h)),
        out_shape=jax.ShapeDtypeStruct((6, S, cfg.DSB), BF16),
        scratch_shapes=[pltpu.VMEM((S, NH * DH), F32), pltpu.VMEM((S, NH * DH), F32)],
        compiler_params=_cp(cfg, ("parallel",)),
    )(qkv3, qkv3, qkv3, do_sb, tsum)


def _band_mask(cfg, n, n_back):
    QB = cfg.QB
    qi = lax.broadcasted_iota(jnp.int32, (QB, 2 * QB), 0)
    kj = lax.broadcasted_iota(jnp.int32, (QB, 2 * QB), 1)
    dist = QB + qi - kj
    return (dist >= 0) & (dist <= n_back) & jnp.logical_or(n > 0, kj >= QB)


def _sub_rows(start, n, dil):
    if dil > 1:
        return pl.ds(start, n, stride=dil)
    return pl.ds(start if isinstance(start, int) else pl.multiple_of(start, 8), n)


def _stage_residues(cfg, dil, pairs):
    QB, L = cfg.QB, cfg.S // dil
    for src, dst in pairs:
        for r in range(dil):
            dst[pl.ds(r * (QB + L), QB), :] = jnp.zeros((QB, cfg.DH), BF16)
            dst[pl.ds(r * (QB + L) + QB, L), :] = src[_sub_rows(r, L, dil), :].astype(BF16)


def _staged_rows(cfg):
    return cfg.S + cfg.QB * max(d for _, d in cfg.branches)


def _lane_value(x):
    return jnp.max(x, axis=1, keepdims=True)


def dil_fwd(cfg, qkv3):
    S, QB, DH = cfg.S, cfg.QB, cfg.DH
    scale = DH ** -0.5
    nb = len(cfg.branches)
    mix_rows = min(256, S)

    def body(q_ref, k_ref, v_ref, o_ref, lt_ref, qf, kf, vf, kp, vp, *obl):
        obs, lbs = obl[:nb], obl[nb:]
        qf[...] = q_ref[...].astype(F32)
        kf[...] = k_ref[...].astype(F32)
        vf[...] = v_ref[...].astype(F32)
        for b, (window, dil) in enumerate(cfg.branches):
            L, n_back = S // dil, window // dil
            assert n_back <= QB and L % QB == 0
            _stage_residues(cfg, dil, [(kf, kp), (vf, vp)])
            for r in range(dil):
                for n in range(L // QB):
                    rows = _sub_rows(r + n * (QB * dil), QB, dil)
                    band = pl.ds(r * (QB + L) + n * QB, 2 * QB)
                    s = _dot_nt(qf[rows, :].astype(BF16), kp[band, :]) * scale
                    s = jnp.where(_band_mask(cfg, n, n_back), s, NEG_BIG)
                    m = jnp.max(s, axis=1, keepdims=True)
                    p = jnp.exp(s - m)
                    den = jnp.sum(p, axis=1, keepdims=True)
                    obs[b][rows, :] = _dot(p.astype(BF16), vp[band, :]) / den
                    lbs[b][rows, :] = jnp.broadcast_to(m + jnp.log(den), (QB, DH))

        def mix(i, _):
            rows = pl.ds(pl.multiple_of(i * mix_rows, mix_rows), mix_rows)
            ls = [r[rows, :] for r in lbs]
            m = functools.reduce(jnp.maximum, ls)
            es = [jnp.exp(l - m) for l in ls]
            tot = functools.reduce(jnp.add, es)
            o_ref[rows, :] = functools.reduce(jnp.add, [(e / tot) * r[rows, :] for e, r in zip(es, obs)])
            lt_ref[rows, :] = m + jnp.log(tot)
            return 0

        lax.fori_loop(0, S // mix_rows, mix, 0)

    def spec(i):
        return pl.BlockSpec((None, S, DH), lambda h: (i, 0, h))

    o_spec = pl.BlockSpec((S, DH), lambda h: (0, h))
    return pl.pallas_call(
        body, name="dil_fwd", grid=(cfg.HDL,),
        in_specs=[spec(3), spec(4), spec(5)], out_specs=[o_spec, o_spec],
        out_shape=[jax.ShapeDtypeStruct((S, cfg.DDL), F32)] * 2,
        scratch_shapes=[pltpu.VMEM((S, DH), F32)] * 3 + [pltpu.VMEM((_staged_rows(cfg), DH), BF16)] * 2
        + [pltpu.VMEM((S, DH), F32)] * (2 * nb),
        compiler_params=_cp(cfg, ("parallel",)),
    )(qkv3, qkv3, qkv3)


def dil_bwd(cfg, qkv3, do_dl, delta, lse_tot, cos2, sin2, d_sb3):
    S, QB, DH = cfg.S, cfg.QB, cfg.DH
    scale = DH ** -0.5
    out_rows = min(256, S)
    GROUP = 4

    def body(q_ref, k_ref, v_ref, do_ref, dl_ref, lt_ref, c_ref, s_ref, base_ref, d_ref,
             qf, kf, vf, dof, kp, vp, dkp, dvp, dqn, dkn, dvn):
        qf[...] = q_ref[...].astype(F32)
        kf[...] = k_ref[...].astype(F32)
        vf[...] = v_ref[...].astype(F32)
        dof[...] = do_ref[...].astype(F32)
        for acc in (dqn, dkn, dvn):
            acc[...] = jnp.zeros_like(acc)
        for window, dil in cfg.branches:
            L, n_back = S // dil, window // dil
            reg = QB + L
            _stage_residues(cfg, dil, [(kf, kp), (vf, vp)])
            dkp[pl.ds(0, dil * reg), :] = jnp.zeros((dil * reg, DH), F32)
            dvp[pl.ds(0, dil * reg), :] = jnp.zeros((dil * reg, DH), F32)
            blocks = [(r, n) for n in range(L // QB) for r in range(dil)]
            for g0 in range(0, len(blocks), GROUP):
                grp = blocks[g0:g0 + GROUP]
                rows = [_sub_rows(r + n * (QB * dil), QB, dil) for r, n in grp]
                bands = [pl.ds(r * reg + n * QB, 2 * QB) for r, n in grp]
                qs = [qf[rw, :].astype(BF16) for rw in rows]
                dos = [dof[rw, :].astype(BF16) for rw in rows]
                kbs = [kp[bd, :] for bd in bands]
                ss = [_dot_nt(q, kb) * scale for q, kb in zip(qs, kbs)]
                dps = [_dot_nt(do, vp[bd, :]) for do, bd in zip(dos, bands)]
                ps = [jnp.exp(jnp.where(_band_mask(cfg, n, n_back), s, NEG_BIG) - _lane_value(lt_ref[rw, :]))
                      for s, rw, (_, n) in zip(ss, rows, grp)]
                dss = [(p * (dp - _lane_value(dl_ref[rw, :])) * scale).astype(BF16) for p, dp, rw in zip(ps, dps, rows)]
                for rw, bd, ds, p, q, do, kb in zip(rows, bands, dss, ps, qs, dos, kbs):
                    dqn[rw, :] += _dot(ds, kb)
                    dkp[bd, :] += _dot_tn(ds, q)
                    dvp[bd, :] += _dot_tn(p.astype(BF16), do)
            for r in range(dil):
                sub = _sub_rows(r, L, dil)
                dkn[sub, :] += dkp[pl.ds(r * reg + QB, L), :]
                dvn[sub, :] += dvp[pl.ds(r * reg + QB, L), :]

        def finish(i, _):
            rows = pl.ds(pl.multiple_of(i * out_rows, out_rows), out_rows)
            c, sn = c_ref[rows, :], s_ref[rows, :]
            for j, acc in enumerate((dqn, dkn)):
                d = acc[rows, :]
                d_ref[j, rows, :] = (d * c + pltpu.roll(d * sn, DH // 2, 1)).astype(BF16)
            d_ref[2, rows, :] = dvn[rows, :].astype(BF16)
            return 0

        lax.fori_loop(0, S // out_rows, finish, 0)

    def spec(i):
        return pl.BlockSpec((None, S, DH), lambda h: (i, 0, h))

    hd = pl.BlockSpec((S, DH), lambda h: (0, h))
    tab = pl.BlockSpec((S, DH), lambda h: (0, 0))
    ns = _staged_rows(cfg)
    return pl.pallas_call(
        body, name="dil_bwd", grid=(cfg.HDL,),
        in_specs=[spec(3), spec(4), spec(5), hd, hd, hd, tab, tab, ANY],
        out_specs=pl.BlockSpec((3, S, DH), lambda h: (1, 0, h)),
        out_shape=jax.ShapeDtypeStruct((6, S, cfg.DDL), BF16),
        input_output_aliases={8: 0},
        scratch_shapes=[pltpu.VMEM((S, DH), F32)] * 4 + [pltpu.VMEM((ns, DH), BF16)] * 2
        + [pltpu.VMEM((ns, DH), F32)] * 2 + [pltpu.VMEM((S, DH), F32)] * 3,
        compiler_params=_cp(cfg, ("parallel",)),
    )(qkv3, qkv3, qkv3, do_dl, delta, lse_tot, cos2, sin2, d_sb3)


def combine_fwd(cfg, o_sb, o_dl, g_sb, g_dl):
    S, DH = cfg.S, cfg.DH

    def head_norm(o, g):
        return o * lax.rsqrt(jnp.mean(o * o, axis=-1, keepdims=True) + RMS_EPS) * g

    def body(osb_ref, odl_ref, gsb_ref, gdl_ref, mix_ref):
        for h in range(cfg.HSB):
            c = slice(h * DH, (h + 1) * DH)
            mix_ref[:, c] = head_norm(osb_ref[:, c], gsb_ref[:, c]).astype(BF16)
        for h in range(cfg.HDL):
            c = slice(h * DH, (h + 1) * DH)
            mix_ref[:, cfg.DSB + h * DH:cfg.DSB + (h + 1) * DH] = head_norm(odl_ref[:, c], gdl_ref[:, c]).astype(BF16)

    return pl.pallas_call(
        body, name="combine_fwd", grid=(S // cfg.TM,),
        in_specs=[_row(cfg, cfg.DSB), _row(cfg, cfg.DDL), _vec(cfg.DSB), _vec(cfg.DDL)],
        out_specs=_row(cfg, cfg.DMIX), out_shape=jax.ShapeDtypeStruct((S, cfg.DMIX), BF16),
        compiler_params=_cp(cfg, ("parallel",)),
    )(o_sb, o_dl, g_sb, g_dl)


def combine_bwd(cfg, dmix, o_sb, o_dl, g_sb, g_dl):
    S, DH = cfg.S, cfg.DH

    def body(dm_ref, osb_ref, odl_ref, gsb_ref, gdl_ref, dsb_ref, ddl_ref, dl_ref, dgsb_ref, dgdl_ref):
        @pl.when(pl.program_id(0) == 0)
        def _():
            dgsb_ref[...] = jnp.zeros_like(dgsb_ref)
            dgdl_ref[...] = jnp.zeros_like(dgdl_ref)

        for h in range(cfg.HSB):
            c = slice(h * DH, (h + 1) * DH)
            dx, dgx = _rms_bwd(dm_ref[:, c], osb_ref[:, c], gsb_ref[:, c])
            dsb_ref[:, c] = dx.astype(BF16)
            dgsb_ref[:, c] += jnp.sum(dgx, axis=0, keepdims=True)
        for h in range(cfg.HDL):
            c = slice(h * DH, (h + 1) * DH)
            o = odl_ref[:, c]
            dx, dgx = _rms_bwd(dm_ref[:, cfg.DSB + h * DH:cfg.DSB + (h + 1) * DH], o, gdl_ref[:, c])
            ddl_ref[:, c] = dx.astype(BF16)
            dl_ref[:, c] = jnp.broadcast_to(jnp.sum(dx * o, axis=-1, keepdims=True), dx.shape)
            dgdl_ref[:, c] += jnp.sum(dgx, axis=0, keepdims=True)

    return pl.pallas_call(
        body, name="combine_bwd", grid=(S // cfg.TM,),
        in_specs=[_row(cfg, cfg.DMIX), _row(cfg, cfg.DSB), _row(cfg, cfg.DDL), _vec(cfg.DSB), _vec(cfg.DDL)],
        out_specs=[_row(cfg, cfg.DSB), _row(cfg, cfg.DDL), _row(cfg, cfg.DDL), _vec(cfg.DSB), _vec(cfg.DDL)],
        out_shape=[jax.ShapeDtypeStruct((S, cfg.DSB), BF16), jax.ShapeDtypeStruct((S, cfg.DDL), BF16),
                   jax.ShapeDtypeStruct((S, cfg.DDL), F32), jax.ShapeDtypeStruct((1, cfg.DSB), F32),
                   jax.ShapeDtypeStruct((1, cfg.DDL), F32)],
        compiler_params=_cp(cfg, ("arbitrary",)),
    )(dmix, o_sb, o_dl, g_sb, g_dl)


SUB = 8


def _shift_down(u, prev, j):
    rolled = pltpu.roll(u, j, 0)
    row = lax.broadcasted_iota(jnp.int32, (SUB, u.shape[1]), 0)
    head = jnp.where(row >= j, rolled[:SUB], pltpu.roll(prev, j, 0))
    return jnp.concatenate([head, rolled[SUB:]], axis=0)


def _shift_up(u, nxt, j):
    n = u.shape[0]
    rolled = pltpu.roll(u, n - j, 0)
    row = lax.broadcasted_iota(jnp.int32, (SUB, u.shape[1]), 0)
    tail = jnp.where(row < SUB - j, rolled[n - SUB:], pltpu.roll(nxt, SUB - j, 0))
    return jnp.concatenate([rolled[:n - SUB], tail], axis=0)


def _conv(u, s1, s2, cw, cb):
    return u * cw[2:3, :] + s1 * cw[1:2, :] + s2 * cw[0:1, :] + cb


def _chunk_rows(cfg):
    ch = min(cfg.FCH, cfg.S)
    return ch, cfg.S // ch


def ffn_fwd(cfg, h2, w_up, conv_w, conv_b):
    S, D = h2.shape
    tn, nt = cfg.TNF, cfg.FFP // cfg.TNF
    ch, nch = _chunk_rows(cfg)

    def body(h_ref, wg_ref, wv_ref, cwg_ref, cwv_ref, cbg_ref, cbv_ref, u_ref, y_ref):
        prev = [jnp.zeros((SUB, tn), F32)] * 2
        pending = None
        for ci in range(nch + 1):
            if ci < nch:
                h = h_ref[pl.ds(ci * ch, ch), :]
                us_next = [_dot_nt(h, wg_ref[...]), _dot_nt(h, wv_ref[...])]
            if pending is not None:
                rows, us = pending
                cs = []
                for i, (cw_ref, cb_ref) in enumerate(((cwg_ref, cbg_ref), (cwv_ref, cbv_ref))):
                    u_ref[i, rows, :] = us[i]
                    cs.append(_conv(us[i], _shift_down(us[i], prev[i], 1), _shift_down(us[i], prev[i], 2),
                                    cw_ref[...], cb_ref[...]))
                y_ref[rows, :] = (_gelu(cs[0])[0] * cs[1]).astype(BF16)
                prev = [u[ch - SUB:] for u in us]
            pending = (pl.ds(ci * ch, ch), us_next) if ci < nch else None

    return pl.pallas_call(
        body, name="ffn_fwd", grid=(nt,),
        in_specs=[pl.BlockSpec((S, D), lambda n: (0, 0)),
                  pl.BlockSpec((tn, D), lambda n: (n, 0)), pl.BlockSpec((tn, D), lambda n: (n + nt, 0)),
                  pl.BlockSpec((3, tn), lambda n: (0, n)), pl.BlockSpec((3, tn), lambda n: (0, n + nt)),
                  pl.BlockSpec((1, tn), lambda n: (0, n)), pl.BlockSpec((1, tn), lambda n: (0, n + nt))],
        out_specs=[pl.BlockSpec((2, S, tn), lambda n: (0, 0, n)), pl.BlockSpec((S, tn), lambda n: (0, n))],
        out_shape=[jax.ShapeDtypeStruct((2, S, cfg.FFP), F32), jax.ShapeDtypeStruct((S, cfg.FFP), BF16)],
        compiler_params=_cp(cfg, ("parallel",)),
    )(h2, w_up, w_up, conv_w, conv_w, conv_b, conv_b)


def ffn_bwd(cfg, df, h2, w_down, u, conv_w, conv_b):
    S, D = df.shape
    tn, nt = cfg.TNF, cfg.FFP // cfg.TNF

    ch, nch = _chunk_rows(cfg)

    def body(df_ref, h_ref, wd_ref, u_ref, cwg_ref, cwv_ref, cbg_ref, cbv_ref,
             du_ref, dwd_ref, dwu_ref, dcw_ref, dcb_ref):
        cws = (cwg_ref[...], cwv_ref[...])
        cbs = (cbg_ref[...], cbv_ref[...])
        zero = jnp.zeros((SUB, tn), F32)
        nxt = [zero, zero]
        dws = [[jnp.zeros((1, tn), F32)] * 4 for _ in range(2)]
        dwd = jnp.zeros((tn, D), F32)
        dwu = [jnp.zeros((tn, D), F32)] * 2
        order = list(reversed(range(nch)))
        dys, done = {}, {}
        for step in range(nch + 2):
            if step < nch:
                ci = order[step]
                dys[ci] = _dot_nt(df_ref[pl.ds(ci * ch, ch), :], wd_ref[...])
            if 1 <= step <= nch:
                ci = order[step - 1]
                rows = pl.ds(ci * ch, ch)
                dy = dys.pop(ci)
                us, s1, s2, cs = [], [], [], []
                for i in range(2):
                    u = u_ref[i, rows, :]
                    prev = u_ref[i, pl.ds(ci * ch - SUB, SUB), :] if ci else zero
                    us.append(u)
                    s1.append(_shift_down(u, prev, 1))
                    s2.append(_shift_down(u, prev, 2))
                    cs.append(_conv(u, s1[i], s2[i], cws[i], cbs[i]))
                gl, t = _gelu(cs[0])
                dcs = (dy * cs[1] * _gelu_grad(cs[0], t), dy * gl)
                dus = []
                for i, dc in enumerate(dcs):
                    du = dc * cws[i][2:3, :] + _shift_up(dc, nxt[i], 1) * cws[i][1:2, :] + _shift_up(dc, nxt[i], 2) * cws[i][0:1, :]
                    dus.append(du.astype(BF16))
                    du_ref[i, rows, :] = dus[i]
                    for j, tap in enumerate((s2[i], s1[i], us[i])):
                        dws[i][j] = dws[i][j] + jnp.sum(dc * tap, axis=0, keepdims=True)
                    dws[i][3] = dws[i][3] + jnp.sum(dc, axis=0, keepdims=True)
                nxt = [dc[:SUB] for dc in dcs]
                done[ci] = ((gl * cs[1]).astype(BF16), dus)
            if step >= 2:
                ci = order[step - 2]
                rows = pl.ds(ci * ch, ch)
                yv, dus = done.pop(ci)
                dwd = dwd + _dot_tn(yv, df_ref[rows, :])
                hv = h_ref[rows, :]
                dwu = [acc + _dot_tn(du, hv) for acc, du in zip(dwu, dus)]
        dwd_ref[...] = dwd.astype(BF16)
        for i in range(2):
            dwu_ref[i] = dwu[i].astype(BF16)
            for j in range(3):
                dcw_ref[i, j:j + 1, :] = dws[i][j]
            dcb_ref[i] = dws[i][3]

    whole = pl.BlockSpec((S, D), lambda n: (0, 0), pipeline_mode=pl.Buffered(1))
    du, dwd, dwu, dcw, dcb = pl.pallas_call(
        body, name="ffn_bwd", grid=(nt,),
        in_specs=[whole, whole, pl.BlockSpec((tn, D), lambda n: (n, 0)),
                  pl.BlockSpec((2, S, tn), lambda n: (0, 0, n)),
                  pl.BlockSpec((3, tn), lambda n: (0, n)), pl.BlockSpec((3, tn), lambda n: (0, n + nt)),
                  pl.BlockSpec((1, tn), lambda n: (0, n)), pl.BlockSpec((1, tn), lambda n: (0, n + nt))],
        out_specs=[pl.BlockSpec((2, S, tn), lambda n: (0, 0, n)), pl.BlockSpec((tn, D), lambda n: (n, 0)),
                   pl.BlockSpec((2, tn, D), lambda n: (0, n, 0)),
                   pl.BlockSpec((2, 3, tn), lambda n: (0, 0, n)), pl.BlockSpec((2, 1, tn), lambda n: (0, 0, n))],
        out_shape=[jax.ShapeDtypeStruct((2, S, cfg.FFP), BF16), jax.ShapeDtypeStruct((cfg.FFP, D), BF16),
                   jax.ShapeDtypeStruct((2, cfg.FFP, D), BF16),
                   jax.ShapeDtypeStruct((2, 3, cfg.FFP), F32), jax.ShapeDtypeStruct((2, 1, cfg.FFP), F32)],
        compiler_params=_cp(cfg, ("parallel",)),
    )(df, h2, w_down, u, conv_w, conv_w, conv_b, conv_b)
    return du, dwd, dwu.reshape(cfg.FF2P, D), dcw, dcb


def rope_tables(cfg):
    inv_freq = ROPE_THETA ** (-jnp.arange(0, cfg.DH, 2, dtype=F32) / cfg.DH)
    ang = jnp.arange(cfg.S, dtype=F32)[:, None] * inv_freq[None, :]
    cos, sin = jnp.cos(ang), jnp.sin(ang)
    return jnp.concatenate([cos, cos], axis=1), jnp.concatenate([-sin, sin], axis=1)


class LocalWeights:
    def __init__(self, w_in, w_out, w_up, conv_w, w_down):
        self.w = (w_in, w_out, w_up, conv_w, w_down)
        self.grads = {}

    def first_start(self):
        return None

    def weights_first(self, after):
        return self.w[0], self.w[3]

    def start_rest(self, after):
        return None

    def weights_rest(self, group, after):
        return ((self.w[1],), None) if group == 0 else ((self.w[4],), None)

    def forwarded(self, group, after):
        return (self.w[2],)

    def pair_start(self, grads):
        self.grads.update(grads)
        return None

    def reduce_start(self, grads, after=None):
        self.grads.update(grads)
        return None

    def reduce_wait(self, names, after, finish_later=False):
        return None

    def reduce_finish(self, after):
        pass


def _after(a, token):
    return a if token is None else a + token[0, 0].astype(a.dtype)


def local_step(cfg, comm, x, target, g1, g2, g3, g4, g_sb, g_dl, conv_b):
    S, D = cfg.S, cfg.D
    cos2, sin2 = rope_tables(cfg)
    full = lambda r, c: pl.BlockSpec((r, c), lambda j, k: (0, 0))

    h1 = rms_fwd(cfg, x, _after(g1, comm.first_start()))
    w_in, conv_w = comm.weights_first(after=h1)
    qkv3 = qkv_proj(cfg, h1, w_in, _after(cos2, comm.start_rest(after=w_in)), sin2)
    o_sb, tsum = sb_fwd(cfg, qkv3)
    o_dl, lse_tot = dil_fwd(cfg, qkv3)
    mixed = combine_fwd(cfg, o_sb, o_dl, g_sb, g_dl)
    (w_out,), token = comm.weights_rest(0, after=mixed)
    tn = cfg.TN
    mo = _mm(cfg, "mix_out", mixed, w_out, nt=False, grid=(D // tn, 1),
             a_spec=full(S, cfg.DMIX), b_spec=pl.BlockSpec((cfg.DMIX, tn), lambda j, k: (0, j)),
             o_spec=pl.BlockSpec((S, tn), lambda j, k: (0, j)),
             out_shape=jax.ShapeDtypeStruct((S, D), F32), acc_shape=(8, LANE), dep=token)
    x1, h2 = mid_fwd(cfg, x, mo, g2, g3)
    w_up, = comm.forwarded(0, after=h2)
    u, y = ffn_fwd(cfg, h2, w_up, conv_w, conv_b)
    (w_down,), _ = comm.weights_rest(1, after=y)
    tk = cfg.FFP // 2
    f = _mm(cfg, "ffn_down", y, w_down, nt=False, grid=(D // tn, cfg.FFP // tk),
            a_spec=pl.BlockSpec((S, tk), lambda j, k: (0, k)), b_spec=pl.BlockSpec((tk, tn), lambda j, k: (k, j)),
            o_spec=pl.BlockSpec((S, tn), lambda j, k: (0, j)),
            out_shape=jax.ShapeDtypeStruct((S, D), F32), acc_shape=(S, tn))
    dout, df, dg4, loss = final_fwd_bwd(cfg, x1, f, g4, target)

    du, dw_down, dw_up, dconv_w, dconv_b = ffn_bwd(cfg, df, h2, w_down, u, conv_w, conv_b)
    kt = cfg.FFP // tk
    dh2 = _mm(cfg, "d_h2", du, w_up, nt=False, grid=(D // tn, 2 * kt),
              a_spec=pl.BlockSpec((None, S, tk), lambda j, k: (k // kt, 0, k % kt)),
              b_spec=pl.BlockSpec((tk, tn), lambda j, k: (k, j)),
              o_spec=pl.BlockSpec((S, tn), lambda j, k: (0, j)),
              out_shape=jax.ShapeDtypeStruct((S, D), F32), acc_shape=(S, tn),
              dep=comm.pair_start(dict(w_down=dw_down, w_up=dw_up)))
    token = comm.reduce_start({}, after=dh2)
    dx1, dmo, dg3, dg2 = mid_bwd(cfg, dh2, x1, _after(g3, token), dout, mo, g2)

    dmix = _mm(cfg, "d_mixed", dmo, w_out, nt=True, grid=(cfg.DMIX // tn, 1),
               a_spec=full(S, D), b_spec=pl.BlockSpec((tn, D), lambda j, k: (j, 0)),
               o_spec=pl.BlockSpec((S, tn), lambda j, k: (0, j)),
               out_shape=jax.ShapeDtypeStruct((S, cfg.DMIX), F32), acc_shape=(8, LANE))
    dw_out = _mm_tn(cfg, "d_w_out", mixed, dmo, grid=(D // tn,),
                    a_spec=pl.BlockSpec((S, cfg.DMIX), lambda j: (0, 0)),
                    b_spec=pl.BlockSpec((S, tn), lambda j: (0, j)),
                    o_spec=pl.BlockSpec((cfg.DMIX, tn), lambda j: (0, j)),
                    out_shape=jax.ShapeDtypeStruct((cfg.DMIX, D), BF16))
    token = comm.reduce_start(dict(w_out=dw_out))
    do_sb, do_dl, delta, dg_sb, dg_dl = combine_bwd(cfg, dmix, o_sb, o_dl, _after(g_sb, token), g_dl)
    d_sb3 = sb_bwd(cfg, qkv3, do_sb, tsum)
    dqkv3 = dil_bwd(cfg, qkv3, do_dl, delta, lse_tot, cos2, sin2, d_sb3)
    token = comm.reduce_wait(("w_up", "w_down"), after=dqkv3, finish_later=True)
    tkq = min(tn, cfg.DSB)
    kq = cfg.DSB // tkq
    dw_in = _mm_tn(cfg, "d_w_in", h1, dqkv3, grid=(6 * kq,),
                   a_spec=pl.BlockSpec((S, D), lambda j: (0, 0)),
                   b_spec=pl.BlockSpec((None, S, tkq), lambda j: (j // kq, 0, j % kq)),
                   o_spec=pl.BlockSpec((D, tkq), lambda j: (0, j)),
                   out_shape=jax.ShapeDtypeStruct((D, 6 * cfg.DSB), BF16), dep=token)
    comm.reduce_finish(after=dw_in)
    comm.reduce_wait(("w_out",), after=dw_in)
    token = comm.late_token = comm.reduce_start(dict(w_in=dw_in))
    dh1 = _mm(cfg, "d_h1", dqkv3, w_in, nt=True, grid=(D // tn, 6),
              a_spec=pl.BlockSpec((None, S, cfg.DSB), lambda j, k: (k, 0, 0)),
              b_spec=pl.BlockSpec((tn, cfg.DSB), lambda j, k: (j, k)),
              o_spec=pl.BlockSpec((S, tn), lambda j, k: (0, j)),
              out_shape=jax.ShapeDtypeStruct((S, D), F32), acc_shape=(S, tn), dep=token)
    grad_x, dg1 = first_bwd(cfg, dh1, x, g1, dx1)
    small = dict(loss=loss, g1=dg1, g2=dg2, g3=dg3, g4=dg4, g_sb=dg_sb, g_dl=dg_dl,
                 conv_b=dconv_b.reshape(1, cfg.FF2P), conv_w=dconv_w.transpose(1, 0, 2).reshape(3, cfg.FF2P))
    return grad_x, small


ANY = pl.BlockSpec(memory_space=pl.ANY)


def _me():
    return lax.axis_index("x"), lax.axis_index("y"), lax.axis_index("c")


def _other_chips(x, y):
    return [(1 - x, y), (x, 1 - y), (1 - x, 1 - y)]


def pad_conv_w(cfg, conv_w, pos):
    r, c = conv_w.shape

    def body(pos_ref, w_ref, full_ref, scr, sem):
        scr[:, :c] = w_ref[...]
        scr[:, c:] = jnp.zeros((r, cfg.FSHP - c), F32)
        cols = pl.ds(pl.multiple_of(pos_ref[0] * cfg.FSHP, LANE), cfg.FSHP)
        cp = pltpu.make_async_copy(scr, full_ref.at[:, cols], sem)
        cp.start()
        cp.wait()

    return pl.pallas_call(
        body, name="pad_conv_w",
        grid_spec=pltpu.PrefetchScalarGridSpec(
            num_scalar_prefetch=1, grid=(1,), in_specs=[pl.BlockSpec((r, c), lambda i, p: (0, 0))], out_specs=ANY,
            scratch_shapes=[pltpu.VMEM((r, cfg.FSHP), F32), pltpu.SemaphoreType.DMA]),
        out_shape=jax.ShapeDtypeStruct(_full_shape(cfg, "conv_w"), F32),
    )(pos, conv_w)


def _tile2(r, c):
    return (256, c) if r % 256 == 0 else (r, 512 if c % 512 == 0 else c)


def cast_into(cfg, name, w, pos, dep=None):
    r, c = w.shape
    _, nr, _, nc = _slab(cfg, name, 0)
    tm, tc = _tile2(r, c)
    wr = nr if tm == r else tm
    assert nc == c and (nr == r or tm == r)
    gap = cfg.FSHP - cfg.FSH if name == "w_down" else 0
    deps = [] if dep is None else [dep]

    def body(pos_ref, w_ref, *rest):
        full_ref, token, scr, sem = rest[len(deps):]
        token[...] = jnp.zeros_like(token)
        tile = w_ref[...]
        if deps:
            tile = tile + rest[0][0:1, 0:1]
        scr[pl.ds(0, tm), :] = tile.astype(BF16)
        if wr > tm:
            scr[pl.ds(tm, wr - tm), :] = jnp.zeros((wr - tm, tc), BF16)
        r0, _, c0, _ = _slab(cfg, name, pos_ref[0])
        rows = pl.ds(pl.multiple_of(r0 + pl.program_id(0) * tm, 16), wr)
        cols = pl.ds(pl.multiple_of(c0 + pl.program_id(1) * tc, LANE), tc)
        cps = [pltpu.make_async_copy(scr.at[pl.ds(0, wr), :], full_ref.at[rows, cols], sem.at[0])]
        if gap:
            scr[pl.ds(wr, gap), :] = jnp.zeros((gap, tc), BF16)
            for h in range(2):
                pad_rows = pl.ds(h * cfg.FSHP + cfg.FSH, gap)
                cps.append(pltpu.make_async_copy(scr.at[pl.ds(wr, gap), :], full_ref.at[pad_rows, cols], sem.at[1 + h]))
        for cp in cps:
            cp.start()
        for cp in cps:
            cp.wait()

    return pl.pallas_call(
        body, name=f"cast_{name}",
        grid_spec=pltpu.PrefetchScalarGridSpec(
            num_scalar_prefetch=1, grid=(r // tm, c // tc),
            in_specs=[pl.BlockSpec((tm, tc), lambda i, j, p: (i, j))]
            + [pl.BlockSpec((8, LANE), lambda i, j, p: (0, 0))] * len(deps),
            out_specs=[ANY, pl.BlockSpec((8, LANE), lambda i, j, p: (0, 0))],
            scratch_shapes=[pltpu.VMEM((wr + gap, tc), BF16), pltpu.SemaphoreType.DMA((3,))]),
        out_shape=[jax.ShapeDtypeStruct(_full_shape(cfg, name), BF16), jax.ShapeDtypeStruct((8, LANE), F32)],
        compiler_params=_cp(cfg, ("arbitrary", "arbitrary")),
    )(pos, w, *deps)


HBM = pl.BlockSpec(memory_space=pltpu.HBM)
SEM = pl.BlockSpec(memory_space=pltpu.SEMAPHORE)
TOKEN = pl.BlockSpec(memory_space=pltpu.VMEM)
EFFECT = pltpu.SideEffectType.DATAFLOW_SIDE_EFFECTING


def _slab(cfg, name, k):
    D = cfg.D
    if name == "w_in":
        cin = 6 * cfg.DSB // N_CHIPS
        return 0, D, k * cin, cin
    if name == "w_out":
        rout = cfg.DMIX // N_CHIPS
        return k * rout, rout, 0, D
    if name == "w_up":
        return k * cfg.FSHP, cfg.FSHP, 0, D
    if name == "conv_w":
        return 0, 3, k * cfg.FSHP, cfg.FSHP
    rdn = cfg.FSH // 2
    return (k // 2) * cfg.FSHP + (k % 2) * rdn, rdn, 0, D


def _full_shape(cfg, name):
    return dict(w_in=(cfg.D, 6 * cfg.DSB), w_out=(cfg.DMIX, cfg.D), w_up=(cfg.FF2P, cfg.D), w_down=(cfg.FFP, cfg.D),
                conv_w=(3, cfg.FF2P))[name]


def _half(cfg, name, ref, k, h):
    r0, nr, c0, nc = _slab(cfg, name, k)
    if name == "conv_w":
        return ref.at[:, pl.ds(c0, nc)]
    return ref.at[pl.ds(r0 + h * (nr // 2), nr // 2), pl.ds(c0, nc)]


def _rows_half(ref, h):
    nr = ref.shape[0] // 2
    return ref.at[pl.ds(h * nr, nr), :]


def _remote(src, dst, send_sem, recv_sem, dev):
    return pltpu.make_async_remote_copy(src_ref=src, dst_ref=dst, send_sem=send_sem, recv_sem=recv_sem,
                                        device_id=dev, device_id_type=MESH)


REST = ("w_out", "w_up", "w_down")
FIRST = (("w_in", "conv_w"),)
GROUPS = (("w_out", "w_up"), ("w_down",))


def _hbm(a):
    return pltpu.with_memory_space_constraint(a, pltpu.HBM)


def gather_start(cfg, tag, groups, fulls, after):
    order = [k for names in groups for k in names]
    n, ng = len(order), len(groups)

    def body(*refs):
        lands = dict(zip(order, refs[:n]))
        sems = refs[n + 1:n + 1 + 2 * ng]
        token = refs[-1]
        x, y, c = _me()
        me = 2 * x + y
        for g, names in enumerate(groups):
            for i, name in enumerate(names):
                mine = _half(cfg, name, lands[name], me, c)
                for j, (px, py) in enumerate(_other_chips(x, y)):
                    _remote(mine, mine, sems[2 * g].at[3 * i + j], sems[2 * g + 1].at[3 * i + j], (px, py, c)).start()
        token[...] = jnp.zeros_like(token)

    ops = [_hbm(fulls[k]) for k in order]
    sem_shapes = [pltpu.SemaphoreType.DMA((3 * len(names),)) for names in groups for _ in range(2)]
    outs = pl.pallas_call(
        body, name=f"gather_start_{tag}",
        in_specs=[HBM] * n + [ANY],
        out_specs=[SEM] * (2 * ng) + [HBM] * n + [TOKEN],
        out_shape=sem_shapes + [pltpu.HBM(a.shape, a.dtype) for a in ops] + [jax.ShapeDtypeStruct((8, LANE), F32)],
        input_output_aliases={i: 2 * ng + i for i in range(n)},
        compiler_params=pltpu.CompilerParams(has_side_effects=EFFECT),
    )(*ops, after)
    thru = dict(zip(order, outs[2 * ng:2 * ng + n]))
    return [(outs[2 * g], outs[2 * g + 1], [thru[k] for k in names]) for g, names in enumerate(groups)], outs[-1]


def gather_wait(cfg, names, ssem, rsem, lands, after):
    n = len(names)

    def body(*refs):
        lands_ = refs[:n]
        ssem_, rsem_ = refs[n], refs[n + 1]
        x, y, c = _me()
        me = 2 * x + y
        for i, name in enumerate(names):
            for j, (px, py) in enumerate(_other_chips(x, y)):
                cp = _remote(_half(cfg, name, lands_[i], me, c), _half(cfg, name, lands_[i], 2 * px + py, c),
                             ssem_.at[3 * i + j], rsem_.at[3 * i + j], (px, py, c))
                cp.wait_send()
                cp.wait_recv()

    return pl.pallas_call(
        body, name="gather_wait_" + "_".join(names),
        in_specs=[HBM] * n + [SEM, SEM, ANY], out_specs=[HBM] * n,
        out_shape=[pltpu.HBM(a.shape, a.dtype) for a in lands],
        input_output_aliases={i: i for i in range(n)},
        compiler_params=pltpu.CompilerParams(has_side_effects=EFFECT),
    )(*lands, ssem, rsem, after)


def gather_finish(cfg, names, lands):
    n = len(names)

    def body(*refs):
        outs = refs[n:2 * n]
        ssem, rsem = refs[2 * n:]
        x, y, c = _me()
        sib = (x, y, 1 - c)
        fwds = []
        for i, name in enumerate(names):
            for j, (px, py) in enumerate(_other_chips(x, y)):
                landed = _half(cfg, name, outs[i], 2 * px + py, c)
                fwds.append(_remote(landed, landed, ssem.at[3 * i + j], rsem.at[3 * i + j], sib))
        for cp in fwds:
            cp.start()
        for i, name in enumerate(names):
            for j, (px, py) in enumerate(_other_chips(x, y)):
                passed = _half(cfg, name, outs[i], 2 * px + py, 1 - c)
                _remote(passed, passed, ssem.at[3 * i + j], rsem.at[3 * i + j], sib).wait_recv()
        for cp in fwds:
            cp.wait_send()

    return pl.pallas_call(
        body, name="gather_finish_" + "_".join(names), in_specs=[ANY] * n, out_specs=[ANY] * n,
        out_shape=[jax.ShapeDtypeStruct(a.shape, a.dtype) for a in lands],
        input_output_aliases={i: i for i in range(n)},
        scratch_shapes=[pltpu.SemaphoreType.DMA((3 * n,)), pltpu.SemaphoreType.DMA((3 * n,))],
    )(*lands)


def forward_start(cfg, names, lands, after):
    n = len(names)

    def body(*refs):
        outs = refs[:n]
        ssem, rsem = refs[n + 1], refs[n + 2]
        token = refs[-1]
        x, y, c = _me()
        for i, name in enumerate(names):
            for j, (px, py) in enumerate(_other_chips(x, y)):
                landed = _half(cfg, name, outs[i], 2 * px + py, c)
                _remote(landed, landed, ssem.at[3 * i + j], rsem.at[3 * i + j], (x, y, 1 - c)).start()
        token[...] = jnp.zeros_like(token)

    ops = [_hbm(a) for a in lands]
    outs = pl.pallas_call(
        body, name="forward_start_" + "_".join(names),
        in_specs=[HBM] * n + [ANY], out_specs=[SEM, SEM] + [HBM] * n + [TOKEN],
        out_shape=[pltpu.SemaphoreType.DMA((3 * n,)), pltpu.SemaphoreType.DMA((3 * n,))]
        + [pltpu.HBM(a.shape, a.dtype) for a in ops] + [jax.ShapeDtypeStruct((8, LANE), F32)],
        input_output_aliases={i: 2 + i for i in range(n)},
        compiler_params=pltpu.CompilerParams(has_side_effects=EFFECT),
    )(*ops, after)
    return outs[0], outs[1], outs[2:2 + n], outs[-1]


def forward_wait(cfg, names, ssem, rsem, lands, after):
    n = len(names)

    def body(*refs):
        outs = refs[:n]
        ssem_, rsem_ = refs[n], refs[n + 1]
        x, y, c = _me()
        for i, name in enumerate(names):
            for j, (px, py) in enumerate(_other_chips(x, y)):
                cp = _remote(_half(cfg, name, outs[i], 2 * px + py, c), _half(cfg, name, outs[i], 2 * px + py, 1 - c),
                             ssem_.at[3 * i + j], rsem_.at[3 * i + j], (x, y, 1 - c))
                cp.wait_send()
                cp.wait_recv()

    return pl.pallas_call(
        body, name="forward_wait_" + "_".join(names),
        in_specs=[HBM] * n + [SEM, SEM, ANY], out_specs=[HBM] * n,
        out_shape=[pltpu.HBM(a.shape, a.dtype) for a in lands],
        input_output_aliases={i: i for i in range(n)},
        compiler_params=pltpu.CompilerParams(has_side_effects=EFFECT),
    )(*lands, ssem, rsem, after)


def pair_send(cfg, grads):
    names = list(grads)
    n = len(names)

    def half_shape(name):
        _, nr, _, nc = _slab(cfg, name, 0)
        return (N_CHIPS, nr // 2, nc)

    def body(*refs):
        srcs, theirs = refs[:n], refs[n:2 * n]
        ssem, rsem = refs[2 * n:]
        x, y, c = _me()
        cps = []
        for i, name in enumerate(names):
            for k in range(N_CHIPS):
                cps.append(_remote(_half(cfg, name, srcs[i], k, 1 - c), theirs[i].at[k],
                                   ssem.at[N_CHIPS * i + k], rsem.at[N_CHIPS * i + k], (x, y, 1 - c)))
        for cp in cps:
            cp.start()
        for cp in cps:
            cp.wait()

    outs = pl.pallas_call(
        body, name="pair_send_" + "_".join(names), in_specs=[ANY] * n, out_specs=[ANY] * n,
        out_shape=[jax.ShapeDtypeStruct(half_shape(name), BF16) for name in names],
        scratch_shapes=[pltpu.SemaphoreType.DMA((N_CHIPS * n,))] * 2,
    )(*[grads[k] for k in names])
    return dict(zip(names, outs))


def pair_start(cfg, grads, after):
    names = list(grads)
    n = len(names)

    def body(*refs):
        srcs, theirs = refs[:n], refs[n:2 * n]
        ssem, rsem = refs[2 * n + 1], refs[2 * n + 2]
        token = refs[-1]
        x, y, c = _me()
        for i, name in enumerate(names):
            for k in range(N_CHIPS):
                _remote(_half(cfg, name, srcs[i], k, 1 - c), theirs[i].at[k],
                        ssem.at[N_CHIPS * i + k], rsem.at[N_CHIPS * i + k], (x, y, 1 - c)).start()
        token[...] = jnp.zeros_like(token)

    def half_shape(name):
        _, nr, _, nc = _slab(cfg, name, 0)
        return (N_CHIPS, nr // 2, nc)

    ops = [_hbm(grads[k]) for k in names] + [_hbm(lax.empty(half_shape(k), BF16)) for k in names]
    outs = pl.pallas_call(
        body, name="pair_start_" + "_".join(names),
        in_specs=[HBM] * (2 * n) + [ANY],
        out_specs=[SEM, SEM] + [HBM] * (2 * n) + [TOKEN],
        out_shape=[pltpu.SemaphoreType.DMA((N_CHIPS * n,)), pltpu.SemaphoreType.DMA((N_CHIPS * n,))]
        + [pltpu.HBM(a.shape, a.dtype) for a in ops] + [jax.ShapeDtypeStruct((8, LANE), F32)],
        input_output_aliases={i: 2 + i for i in range(2 * n)},
        compiler_params=pltpu.CompilerParams(has_side_effects=EFFECT),
    )(*ops, after)
    return outs[0], outs[1], dict(zip(names, outs[2:2 + n])), dict(zip(names, outs[2 + n:2 + 2 * n])), outs[-1]


def pair_wait(cfg, ssem, rsem, grads, theirs, after):
    names = list(grads)
    n = len(names)

    def body(*refs):
        srcs, theirs_ = refs[:n], refs[n:2 * n]
        ssem_, rsem_ = refs[2 * n], refs[2 * n + 1]
        x, y, c = _me()
        for i, name in enumerate(names):
            for k in range(N_CHIPS):
                cp = _remote(_half(cfg, name, srcs[i], k, 1 - c), theirs_[i].at[k],
                             ssem_.at[N_CHIPS * i + k], rsem_.at[N_CHIPS * i + k], (x, y, 1 - c))
                cp.wait_send()
                cp.wait_recv()

    ops = [grads[k] for k in names] + [theirs[k] for k in names]
    outs = pl.pallas_call(
        body, name="pair_wait_" + "_".join(names),
        in_specs=[HBM] * (2 * n) + [SEM, SEM, ANY], out_specs=[HBM] * (2 * n),
        out_shape=[pltpu.HBM(a.shape, a.dtype) for a in ops],
        input_output_aliases={i: i for i in range(2 * n)},
        compiler_params=pltpu.CompilerParams(has_side_effects=EFFECT),
    )(*ops, ssem, rsem, after)
    return dict(zip(names, outs[:n])), dict(zip(names, outs[n:]))


def pair_sum(cfg, name, grad, theirs, pos):
    _, r, c = theirs.shape
    tm, tc = _tile2(r, c)

    ni, nj = r // tm, c // tc
    total = N_CHIPS * ni * nj

    def body(pos_ref, g_ref, t_ref, o_ref, scr, sem):
        step = (pl.program_id(0) * ni + pl.program_id(1)) * nj + pl.program_id(2)

        def fetch(flat, slot):
            k, rem = flat // (ni * nj), flat % (ni * nj)
            r0, nr, c0, _ = _slab(cfg, name, k)
            rows = pl.ds(pl.multiple_of(r0 + pos_ref[1] * (nr // 2) + (rem // nj) * tm, 16), tm)
            cols = pl.ds(pl.multiple_of(c0 + (rem % nj) * tc, LANE), tc)
            return pltpu.make_async_copy(g_ref.at[rows, cols], scr.at[slot], sem.at[slot])

        @pl.when(step == 0)
        def _():
            fetch(0, 0).start()

        @pl.when(step + 1 < total)
        def _():
            fetch(step + 1, (step + 1) % 2).start()

        fetch(step, step % 2).wait()
        o_ref[...] = (scr[step % 2].astype(F32) + t_ref[...].astype(F32)).astype(BF16)

    blk = pl.BlockSpec((None, tm, tc), lambda k, i, j, p: (k, i, j))
    return pl.pallas_call(
        body, name=f"pair_sum_{name}",
        grid_spec=pltpu.PrefetchScalarGridSpec(
            num_scalar_prefetch=1, grid=(N_CHIPS, ni, nj), in_specs=[ANY, blk], out_specs=blk,
            scratch_shapes=[pltpu.VMEM((2, tm, tc), BF16), pltpu.SemaphoreType.DMA((2,))]),
        out_shape=jax.ShapeDtypeStruct(theirs.shape, BF16),
        compiler_params=_cp(cfg, ("arbitrary",) * 3),
    )(pos, grad, theirs)


def scatter_start(cfg, pres, after):
    names = list(pres)
    n = len(names)

    def body(*refs):
        srcs, lands = refs[:n], refs[n:2 * n]
        ssem, rsem = refs[2 * n + 1], refs[2 * n + 2]
        token = refs[-1]
        x, y, c = _me()
        for i in range(n):
            for j, (px, py) in enumerate(_other_chips(x, y)):
                _remote(srcs[i].at[2 * px + py], lands[i].at[j], ssem.at[3 * i + j], rsem.at[3 * i + j], (px, py, c)).start()
        token[...] = jnp.zeros_like(token)

    lands = [lax.empty((3,) + pres[k].shape[1:], BF16) for k in names]
    ops = [_hbm(a) for a in [pres[k] for k in names] + lands]
    outs = pl.pallas_call(
        body, name="scatter_start_" + "_".join(names),
        in_specs=[HBM] * (2 * n) + [ANY],
        out_specs=[SEM, SEM] + [HBM] * (2 * n) + [TOKEN],
        out_shape=[pltpu.SemaphoreType.DMA((3 * n,)), pltpu.SemaphoreType.DMA((3 * n,))]
        + [pltpu.HBM(a.shape, a.dtype) for a in ops] + [jax.ShapeDtypeStruct((8, LANE), F32)],
        input_output_aliases={i: 2 + i for i in range(2 * n)},
        compiler_params=pltpu.CompilerParams(has_side_effects=EFFECT),
    )(*ops, after)
    return outs[0], outs[1], dict(zip(names, outs[2:2 + n])), dict(zip(names, outs[2 + n:2 + 2 * n])), outs[-1]


def scatter_wait(cfg, ssem, rsem, pres, lands, after):
    names = list(pres)
    n = len(names)

    def body(*refs):
        srcs, lands_ = refs[:n], refs[n:2 * n]
        ssem_, rsem_ = refs[2 * n], refs[2 * n + 1]
        x, y, c = _me()
        for i in range(n):
            for j, (px, py) in enumerate(_other_chips(x, y)):
                cp = _remote(srcs[i].at[2 * px + py], lands_[i].at[j], ssem_.at[3 * i + j], rsem_.at[3 * i + j], (px, py, c))
                cp.wait_send()
                cp.wait_recv()

    ops = [pres[k] for k in names] + [lands[k] for k in names]
    outs = pl.pallas_call(
        body, name="scatter_wait_" + "_".join(names),
        in_specs=[HBM] * (2 * n) + [SEM, SEM, ANY], out_specs=[HBM] * (2 * n),
        out_shape=[pltpu.HBM(a.shape, a.dtype) for a in ops],
        input_output_aliases={i: i for i in range(2 * n)},
        compiler_params=pltpu.CompilerParams(has_side_effects=EFFECT),
    )(*ops, ssem, rsem, after)
    return dict(zip(names, outs[:n])), dict(zip(names, outs[n:]))


def sum_landed(cfg, name, pre, land, pos):
    _, r, c = pre.shape
    tm, tc = _tile2(r, c)
    nrt = r // tm

    def body(pos_ref, p_ref, l_ref, o_ref):
        acc = p_ref[...].astype(F32)
        for j in range(3):
            acc = acc + l_ref[j].astype(F32)
        o_ref[...] = acc

    return pl.pallas_call(
        body, name=f"sum_landed_{name}",
        grid_spec=pltpu.PrefetchScalarGridSpec(
            num_scalar_prefetch=1, grid=(nrt, c // tc),
            in_specs=[pl.BlockSpec((None, tm, tc), lambda i, j, p: (p[0], i, j)),
                      pl.BlockSpec((3, tm, tc), lambda i, j, p: (0, i, j))],
            out_specs=pl.BlockSpec((tm, tc), lambda i, j, p: (p[1] * nrt + i, j))),
        out_shape=jax.ShapeDtypeStruct((2 * r, c), F32), compiler_params=_cp(cfg, ("parallel", "parallel")),
    )(pos, pre, land)


def half_swap(cfg, sums):
    names = list(sums)
    n = len(names)

    def body(*refs):
        outs = refs[n:2 * n]
        ssem, rsem = refs[2 * n:]
        x, y, c = _me()
        cps = [_remote(_rows_half(outs[i], c), _rows_half(outs[i], c), ssem.at[i], rsem.at[i], (x, y, 1 - c))
               for i in range(n)]
        for cp in cps:
            cp.start()
        for i in range(n):
            theirs = _rows_half(outs[i], 1 - c)
            _remote(theirs, theirs, ssem.at[i], rsem.at[i], (x, y, 1 - c)).wait_recv()
        for cp in cps:
            cp.wait_send()

    outs = pl.pallas_call(
        body, name="half_swap_" + "_".join(names), in_specs=[ANY] * n, out_specs=[ANY] * n,
        out_shape=[jax.ShapeDtypeStruct(sums[k].shape, F32) for k in names],
        input_output_aliases={i: i for i in range(n)},
        scratch_shapes=[pltpu.SemaphoreType.DMA((n,))] * 2,
    )(*[sums[k] for k in names])
    return dict(zip(names, outs))


def half_swap_start(cfg, sums):
    names = list(sums)
    n = len(names)

    def body(*refs):
        outs = refs[:n]
        ssem, rsem = refs[n + 1], refs[n + 2]
        token = refs[-1]
        x, y, c = _me()
        for i in range(n):
            mine = _rows_half(outs[i], c)
            _remote(mine, mine, ssem.at[i], rsem.at[i], (x, y, 1 - c)).start()
        token[...] = jnp.zeros_like(token)

    ops = [_hbm(sums[k]) for k in names]
    outs = pl.pallas_call(
        body, name="half_swap_start_" + "_".join(names),
        in_specs=[HBM] * n + [ANY], out_specs=[SEM, SEM] + [HBM] * n + [TOKEN],
        out_shape=[pltpu.SemaphoreType.DMA((n,)), pltpu.SemaphoreType.DMA((n,))]
        + [pltpu.HBM(a.shape, a.dtype) for a in ops] + [jax.ShapeDtypeStruct((8, LANE), F32)],
        input_output_aliases={i: 2 + i for i in range(n)},
        compiler_params=pltpu.CompilerParams(has_side_effects=EFFECT),
    )(*ops, jnp.zeros((8, LANE), F32))
    return outs[0], outs[1], dict(zip(names, outs[2:2 + n])), outs[-1]


def half_swap_wait(cfg, ssem, rsem, sums, after):
    names = list(sums)
    n = len(names)

    def body(*refs):
        outs = refs[:n]
        ssem_, rsem_ = refs[n], refs[n + 1]
        x, y, c = _me()
        for i in range(n):
            cp = _remote(_rows_half(outs[i], c), _rows_half(outs[i], 1 - c), ssem_.at[i], rsem_.at[i], (x, y, 1 - c))
            cp.wait_send()
            cp.wait_recv()

    outs = pl.pallas_call(
        body, name="half_swap_wait_" + "_".join(names),
        in_specs=[HBM] * n + [SEM, SEM, ANY], out_specs=[HBM] * n,
        out_shape=[pltpu.HBM(sums[k].shape, sums[k].dtype) for k in names],
        input_output_aliases={i: i for i in range(n)},
        compiler_params=pltpu.CompilerParams(has_side_effects=EFFECT),
    )(*[sums[k] for k in names], ssem, rsem, after)
    return dict(zip(names, outs))


class MeshWeights:
    def __init__(self, cfg, w_sh):
        self.cfg = cfg
        self.pos = jnp.stack([2 * lax.axis_index("x") + lax.axis_index("y"), lax.axis_index("c")]).astype(jnp.int32)
        self.w_sh = w_sh
        self.full = {"w_in": cast_into(cfg, "w_in", w_sh["w_in"], self.pos)[0],
                     "conv_w": pad_conv_w(cfg, w_sh["conv_w"], self.pos)}
        self.inflight = {}
        self.forwards = {}
        self.grads = {}

    def first_start(self):
        cfg = self.cfg
        self.first, token = gather_start(cfg, "first", FIRST, self.full, jnp.zeros((8, LANE), F32))
        for k in REST:
            self.full[k], token = cast_into(cfg, k, self.w_sh[k], self.pos, dep=token)
        return token

    def weights_first(self, after):
        cfg = self.cfg
        ssem, rsem, lands = self.first[0]
        w_in, conv_w = gather_wait(cfg, FIRST[0], ssem, rsem, lands, after)
        self.rest, self.rest_token = gather_start(cfg, "rest", GROUPS, self.full, conv_w)
        return gather_finish(cfg, ("w_in",), [w_in])[0], conv_w

    def start_rest(self, after):
        return self.rest_token

    def weights_rest(self, group, after):
        cfg = self.cfg
        names = GROUPS[group]
        ssem, rsem, lands = self.rest[group]
        lands = dict(zip(names, gather_wait(cfg, names, ssem, rsem, lands, after)))
        now = [k for k in names if k != "w_up"]
        later = [k for k in names if k == "w_up"]
        ready = gather_finish(cfg, tuple(now), [lands[k] for k in now])
        if not later:
            return tuple(ready), None
        out = forward_start(cfg, tuple(later), [lands[k] for k in later], ready[0])
        self.forwards[group] = (tuple(later),) + tuple(out[:3])
        return tuple(ready), out[3]

    def forwarded(self, group, after):
        names, ssem, rsem, lands = self.forwards.pop(group)
        return tuple(forward_wait(self.cfg, names, ssem, rsem, lands, after))

    def pair_start(self, grads):
        out = pair_start(self.cfg, grads, jnp.zeros((8, LANE), F32))
        self.pairs = out[:4]
        return out[4]

    def reduce_start(self, grads, after=None):
        theirs = pair_send(self.cfg, grads) if grads else {}
        if after is not None:
            early, early_theirs = pair_wait(self.cfg, *self.pairs, after)
            grads, theirs = {**early, **grads}, {**early_theirs, **theirs}
        pres = {k: pair_sum(self.cfg, k, grads[k], theirs[k], self.pos) for k in grads}
        out = scatter_start(self.cfg, pres, jnp.zeros((8, LANE), F32))
        self.inflight[tuple(sorted(grads))] = out[:4]
        return out[4]

    def reduce_wait(self, names, after, finish_later=False):
        cfg = self.cfg
        pres, lands = scatter_wait(cfg, *self.inflight.pop(tuple(sorted(names))), after)
        sums = {k: sum_landed(cfg, k, pres[k], lands[k], self.pos) for k in names}
        if not finish_later:
            self.grads.update(half_swap(cfg, sums))
            return None
        out = half_swap_start(cfg, sums)
        self.swaps = out[:3]
        return out[3]

    def reduce_finish(self, after):
        self.grads.update(half_swap_wait(self.cfg, *self.swaps, after))


def allreduce_small(cfg, vec):
    R = vec.shape[0]

    def body(v_ref, o_ref, buf, send_sems, recv_sems):
        x, y, c = _me()
        me = 4 * x + 2 * y + c
        buf[me] = v_ref[...]
        sends = []
        for k in range(1, N_DEV):
            px, py, pc = x ^ (k >> 2), y ^ ((k >> 1) & 1), c ^ (k & 1)
            sends.append(pltpu.make_async_remote_copy(
                src_ref=v_ref, dst_ref=buf.at[me], send_sem=send_sems.at[k], recv_sem=recv_sems.at[k],
                device_id=(px, py, pc), device_id_type=MESH))
        for cp in sends:
            cp.start()
        for k in range(1, N_DEV):
            px, py, pc = x ^ (k >> 2), y ^ ((k >> 1) & 1), c ^ (k & 1)
            pltpu.make_async_remote_copy(
                src_ref=v_ref, dst_ref=buf.at[4 * px + 2 * py + pc], send_sem=send_sems.at[k],
                recv_sem=recv_sems.at[k], device_id=(px, py, pc), device_id_type=MESH).wait_recv()
        for cp in sends:
            cp.wait_send()
        acc = buf[0]
        for j in range(1, N_DEV):
            acc = acc + buf[j]
        o_ref[...] = acc

    return pl.pallas_call(
        body, name="allreduce_small",
        in_specs=[pl.BlockSpec(memory_space=pltpu.VMEM)], out_specs=pl.BlockSpec(memory_space=pltpu.VMEM),
        out_shape=jax.ShapeDtypeStruct((R, LANE), F32),
        scratch_shapes=[pltpu.VMEM((N_DEV, R, LANE), F32), pltpu.SemaphoreType.DMA((N_DEV,)),
                        pltpu.SemaphoreType.DMA((N_DEV,))],
    )(vec)


def _adamw_update(w_ref, m_ref, v_ref, g, outs):
    g_out, d_out, m_out, v_out = outs
    m_new = ADAM_B1 * m_ref[...] + (1.0 - ADAM_B1) * g
    v_new = ADAM_B2 * v_ref[...] + (1.0 - ADAM_B2) * jnp.square(g)
    m_hat = m_new / (1.0 - ADAM_B1 ** ADAM_STEP)
    v_hat = v_new / (1.0 - ADAM_B2 ** ADAM_STEP)
    g_out[...] = g
    d_out[...] = -ADAM_LR * (m_hat / (jnp.sqrt(v_hat) + ADAM_EPS) + ADAM_WD * w_ref[...])
    m_out[...] = m_new
    v_out[...] = v_new


def adamw_small(cfg, params):
    names = list(params)
    n = len(names)

    def body(*refs):
        for i in range(n):
            w_ref, m_ref, v_ref, g_ref = refs[4 * i:4 * i + 4]
            _adamw_update(w_ref, m_ref, v_ref, g_ref[...], refs[4 * n + 4 * i:4 * n + 4 * i + 4])

    outs = pl.pallas_call(
        body, name="adamw_small",
        out_shape=[jax.ShapeDtypeStruct(params[k][0].shape, F32) for k in names for _ in range(4)],
    )(*[a for k in names for a in params[k]])
    return {k: list(outs[4 * i:4 * i + 4]) for i, k in enumerate(names)}


def adamw(cfg, name, w, m, v, g_parts, tile, dep=None):
    r, c = w.shape
    tm, tc = tile[0] or r, tile[1] or c
    assert tc == c or all(g.shape[1] == c for g in g_parts)
    n = len(g_parts)
    deps = [] if dep is None else [dep]

    def body(*refs):
        w_ref, m_ref, v_ref = refs[:3]
        g_refs = refs[3:3 + n]
        g = g_refs[0][:, :tc]
        for gr in g_refs[1:]:
            g = g + gr[:, :tc]
        if deps:
            g = g + refs[3 + n][0:1, 0:1]
        _adamw_update(w_ref, m_ref, v_ref, g, refs[3 + n + len(deps):])

    blk = pl.BlockSpec((tm, tc), lambda i, j: (i, j))
    return pl.pallas_call(
        body, name=f"adamw_{name}", grid=(r // tm, c // tc),
        in_specs=[blk] * 3 + [pl.BlockSpec((tm, tc if tc < c else g.shape[1]), lambda i, j: (i, j)) for g in g_parts]
        + [pl.BlockSpec((8, LANE), lambda i, j: (0, 0))] * len(deps),
        out_specs=[blk] * 4, out_shape=[jax.ShapeDtypeStruct((r, c), F32)] * 4,
        compiler_params=_cp(cfg, ("parallel", "parallel")),
    )(w, m, v, *g_parts, *deps)


SMALL_ORDER = ("loss", "g1", "g2", "g3", "g4", "g_sb", "g_dl", "conv_b", "conv_w")


def pack_small(small):
    rows = []
    for k in SMALL_ORDER:
        a = small[k].reshape(-1, LANE)
        rows.append(a)
    flat = jnp.concatenate(rows, axis=0)
    pad = (-flat.shape[0]) % 8
    return jnp.pad(flat, ((0, pad), (0, 0))), [r.shape[0] for r in rows]


def unpack_small(red, small, counts):
    out, at = {}, 0
    for k, n in zip(SMALL_ORDER, counts):
        out[k] = red[at:at + n].reshape(small[k].shape)
        at += n
    return out


def pad_ff(cfg, a):
    r = a.shape[0]
    return jnp.pad(a.reshape(r, N_CHIPS, cfg.FSH), ((0, 0), (0, 0), (0, cfg.FSHP - cfg.FSH))).reshape(r, cfg.FF2P)


def step(cfg, x, target, gains, w_sh, conv_b, m_all, v_all):
    chip = 2 * lax.axis_index("x") + lax.axis_index("y")
    comm = MeshWeights(cfg, w_sh)
    grad_x, small = local_step(cfg, comm, x, target, gains["g1"], gains["g2"], gains["g3"], gains["g4"],
                               gains["g_sb"], gains["g_dl"], pad_ff(cfg, conv_b))

    packed, counts = pack_small(small)
    summed = allreduce_small(cfg, packed)
    comm.reduce_wait(("w_in",), after=summed)
    red = unpack_small(summed, small, counts)

    names = ("w_in", "w_out", "w_up", "w_down")
    up_rows = max(t for t in range(SUB, 513, SUB) if cfg.FSH % t == 0)
    tms = dict(w_in=(cfg.TM, None), w_out=(cfg.TM, None), w_up=(up_rows, None), w_down=(None, cfg.TN // 2))
    res = {}
    for n in names:
        res[n] = adamw(cfg, n, w_sh[n], m_all[n], v_all[n], [comm.grads[n]], tms[n],
                       dep=None if n == "w_in" else comm.late_token)
    g_cw = lax.dynamic_slice_in_dim(red["conv_w"].reshape(3, N_CHIPS, cfg.FSHP), chip, 1, axis=1)[:, 0, :cfg.FSH]
    g_cb = red["conv_b"].reshape(1, N_CHIPS, cfg.FSHP)[:, :, :cfg.FSH].reshape(1, N_CHIPS * cfg.FSH)
    smalls = {"conv_w": (w_sh["conv_w"], g_cw), "conv_b": (conv_b, g_cb)}
    smalls.update({k: (gains[k], red[k]) for k in ("g1", "g2", "g3", "g4", "g_sb", "g_dl")})
    res.update(adamw_small(cfg, {k: (w, m_all[k], v_all[k], g) for k, (w, g) in smalls.items()}))
    return red["loss"][0, 0], grad_x, res


PARAMS = ("pre_mix_gain", "post_mix_gain", "pre_ffn_gain", "post_ffn_gain", "w_in", "sb_out_gain", "dil_out_gain",
          "w_out", "w_up", "conv_w", "conv_b", "w_down")
SHORT = dict(pre_mix_gain="g1", post_mix_gain="g2", pre_ffn_gain="g3", post_ffn_gain="g4", sb_out_gain="g_sb",
             dil_out_gain="g_dl", w_in="w_in", w_out="w_out", w_up="w_up", conv_w="conv_w", conv_b="conv_b",
             w_down="w_down")


def kernel(x, pre_mix_gain, post_mix_gain, pre_ffn_gain, post_ffn_gain, w_in, sb_out_gain, dil_out_gain, w_out, w_up, conv_w, conv_b, w_down, loss_target, m_pre_mix_gain, m_post_mix_gain, m_pre_ffn_gain, m_post_ffn_gain, m_w_in, m_sb_out_gain, m_dil_out_gain, m_w_out, m_w_up, m_conv_w, m_conv_b, m_w_down, v_pre_mix_gain, v_post_mix_gain, v_pre_ffn_gain, v_post_ffn_gain, v_w_in, v_sb_out_gain, v_dil_out_gain, v_w_out, v_w_up, v_conv_w, v_conv_b, v_w_down):
    cfg = CFG
    w = dict(zip(PARAMS, (pre_mix_gain, post_mix_gain, pre_ffn_gain, post_ffn_gain, w_in, sb_out_gain, dil_out_gain,
                          w_out, w_up, conv_w, conv_b, w_down)))
    m = dict(zip(PARAMS, (m_pre_mix_gain, m_post_mix_gain, m_pre_ffn_gain, m_post_ffn_gain, m_w_in, m_sb_out_gain,
                          m_dil_out_gain, m_w_out, m_w_up, m_conv_w, m_conv_b, m_w_down)))
    v = dict(zip(PARAMS, (v_pre_mix_gain, v_post_mix_gain, v_pre_ffn_gain, v_post_ffn_gain, v_w_in, v_sb_out_gain,
                          v_dil_out_gain, v_w_out, v_w_up, v_conv_w, v_conv_b, v_w_down)))
    sq = lambda a: a.reshape(a.shape[1:])
    ws = {SHORT[k]: sq(a) if a.ndim == 3 else a for k, a in w.items()}
    ms = {SHORT[k]: sq(a) if a.ndim == 3 else a for k, a in m.items()}
    vs = {SHORT[k]: sq(a) if a.ndim == 3 else a for k, a in v.items()}
    for d in (ws, ms, vs):
        d["w_up"] = d["w_up"].T
    gains = {k: ws[k] for k in ("g1", "g2", "g3", "g4", "g_sb", "g_dl")}
    w_sh = {k: ws[k] for k in ("w_in", "w_out", "w_up", "conv_w", "w_down")}
    loss, grad_x, res = step(cfg, sq(x), sq(loss_target), gains, w_sh, ws["conv_b"], ms, vs)
    res["w_up"] = [a.T for a in res["w_up"]]
    outs = [loss, grad_x.reshape(x.shape)]
    for i in range(4):
        for k in PARAMS:
            outs.append(res[SHORT[k]][i].reshape(w[k].shape))
    return tuple(outs)
```
